```python
import jax, jax.numpy as jnp
from jax import lax
import numpy as np

D_MODEL = 1024
BATCH = 8
SEQ = 4096
DEPTH = 2

N_MEM = 256
MIX_WIDTH = 2 * D_MODEL
XA_HEADS = 4
XA_WIDTH = MIX_WIDTH // 4
XA_HEAD_DIM = XA_WIDTH // XA_HEADS
BRANCH_WIDTH = (MIX_WIDTH - XA_WIDTH) // 2
CHUNK = 128
A_HEADS = 4
A_HEAD_DIM = BRANCH_WIDTH // A_HEADS
SHORT_CONV = 3
POOL_WINDOWS = (2, 4, 8, 16)
C_GROUP = BRANCH_WIDTH // len(POOL_WINDOWS)
CONF_CONV = 31
EPS = 1e-6

N_EVEN = (DEPTH + 1) // 2
N_ODD = DEPTH // 2
EVEN_SPLITS = (BRANCH_WIDTH, BRANCH_WIDTH, BRANCH_WIDTH, BRANCH_WIDTH, BRANCH_WIDTH, XA_WIDTH, MIX_WIDTH)
ODD_SPLITS = (BRANCH_WIDTH, BRANCH_WIDTH, BRANCH_WIDTH, XA_WIDTH, MIX_WIDTH)
EVEN_IN = sum(EVEN_SPLITS)
ODD_IN = sum(ODD_SPLITS)

kernel_name = "hybrid_sgu_shortconv_pool_conformer_memxattn"


def _split(p, sizes):
    idx = [int(v) for v in np.cumsum(sizes)[:-1]]
    return jnp.split(p, idx, axis=-1)


def rms_norm(x, g):
    xf = x.astype(jnp.float32)
    y = xf * lax.rsqrt(jnp.mean(xf * xf, axis=-1, keepdims=True) + EPS)
    return (y * g.astype(jnp.float32)).astype(x.dtype)


def layer_norm(x, g, b):
    xf = x.astype(jnp.float32)
    mu = jnp.mean(xf, axis=-1, keepdims=True)
    var = jnp.mean(jnp.square(xf - mu), axis=-1, keepdims=True)
    y = (xf - mu) * lax.rsqrt(var + EPS)
    return (y * g.astype(jnp.float32) + b.astype(jnp.float32)).astype(x.dtype)


def causal_dwconv(x, w):
    k = w.shape[0]
    return lax.conv_general_dilated(
        x, w[:, None, :].astype(x.dtype), window_strides=(1,), padding=[(k - 1, 0)],
        dimension_numbers=("NWC", "WIO", "NWC"), feature_group_count=x.shape[-1])


def memory_cross_attention(q, mem_n, w_kv):
    bsz, s, _ = q.shape
    k, v = _split(jnp.einsum("bmd,de->bme", mem_n, w_kv), (XA_WIDTH, XA_WIDTH))
    q = q.reshape(bsz, s, XA_HEADS, XA_HEAD_DIM)
    k = k.reshape(bsz, -1, XA_HEADS, XA_HEAD_DIM)
    v = v.reshape(bsz, -1, XA_HEADS, XA_HEAD_DIM)
    scores = jnp.einsum("bshd,bmhd->bhsm", q, k).astype(jnp.float32) * (XA_HEAD_DIM ** -0.5)
    probs = jax.nn.softmax(scores, axis=-1).astype(v.dtype)
    return jnp.einsum("bhsm,bmhd->bshd", probs, v).reshape(bsz, s, XA_WIDTH)


def spatial_gating(u, v, ln_g, ln_b, w_s, b_s):
    bsz, s, _ = v.shape
    v = layer_norm(v, ln_g, ln_b).reshape(bsz, s // CHUNK, CHUNK, A_HEADS, A_HEAD_DIM)
    causal = jnp.tril(jnp.ones((CHUNK, CHUNK), dtype=bool))
    w = jnp.where(causal[None], w_s, 0.0).astype(v.dtype)
    sg = jnp.einsum("hts,bnshd->bnthd", w, v) + b_s.T[None, None, :, :, None]
    return u * sg.reshape(bsz, s, BRANCH_WIDTH)


def short_gated_conv(bg, cg, xin, w_conv):
    return bg * causal_dwconv(cg * xin, w_conv)


def multiscale_pool(z, w_grp, scale):
    s = z.shape[1]
    zf = z.astype(jnp.float32)
    csum = jnp.pad(jnp.cumsum(zf, axis=1), ((0, 0), (1, 0), (0, 0)))
    t = jnp.arange(1, s + 1)
    outs = []
    for g, win in enumerate(POOL_WINDOWS):
        sl = slice(g * C_GROUP, (g + 1) * C_GROUP)
        c = csum[..., sl]
        lo = jnp.pad(c, ((0, 0), (win, 0), (0, 0)))[:, : s + 1]
        cnt = jnp.minimum(t, win).astype(jnp.float32)[None, :, None]
        pooled = (c[:, 1:] - lo[:, 1:]) / cnt - zf[..., sl]
        outs.append(jnp.einsum("bsc,cd->bsd", pooled.astype(z.dtype), w_grp[g]))
    return jnp.concatenate(outs, axis=-1) * scale


def conformer_conv(a, b, w_dw, b_dw, ln_g, ln_b, w_pw, b_pw):
    z = a * jax.nn.sigmoid(b)
    z = causal_dwconv(z, w_dw) + b_dw
    z = jax.nn.silu(layer_norm(z, ln_g, ln_b))
    return jnp.einsum("bsc,cd->bsd", z, w_pw) + b_pw


def even_layer(x, mem, pre_g, w_in, a_ln_g, a_ln_b, a_ws, a_bs, b_conv, mem_g, w_kv, w_out, post_g):
    h = rms_norm(x, pre_g)
    p = jnp.einsum("bsd,de->bse", h, w_in)
    u, v, bg, cg, xin, q, gate = _split(p, EVEN_SPLITS)
    ya = spatial_gating(u, v, a_ln_g, a_ln_b, a_ws, a_bs)
    yb = short_gated_conv(bg, cg, xin, b_conv)
    yx = memory_cross_attention(q, rms_norm(mem, mem_g), w_kv)
    y = jnp.concatenate([ya, yb, yx], axis=-1) * jax.nn.silu(gate)
    return x + rms_norm(jnp.einsum("bse,ed->bsd", y, w_out), post_g)


def odd_layer(x, mem, pre_g, w_in, c_wgrp, c_scale, d_dw_w, d_dw_b, d_ln_g, d_ln_b, d_pw_w, d_pw_b,
              mem_g, w_kv, w_out, post_g):
    h = rms_norm(x, pre_g)
    p = jnp.einsum("bsd,de->bse", h, w_in)
    zc, ga, gb, q, gate = _split(p, ODD_SPLITS)
    yc = multiscale_pool(zc, c_wgrp, c_scale)
    yd = conformer_conv(ga, gb, d_dw_w, d_dw_b, d_ln_g, d_ln_b, d_pw_w, d_pw_b)
    yx = memory_cross_attention(q, rms_norm(mem, mem_g), w_kv)
    y = jnp.concatenate([yc, yd, yx], axis=-1) * jax.nn.silu(gate)
    return x + rms_norm(jnp.einsum("bse,ed->bsd", y, w_out), post_g)


def _fwd_setup_inputs(seed: int = 0) -> dict:
    key = jax.random.key(seed)
    ks = iter(jax.random.split(key, 40))
    f32 = jnp.float32

    def nrm(shape, scale):
        return jax.random.normal(next(ks), shape, f32) * scale

    def gain(shape):
        return 1.0 + 0.05 * jax.random.normal(next(ks), shape, f32)

    ne, no = N_EVEN, N_ODD
    bw = BRANCH_WIDTH
    return {
        "x": jax.random.normal(next(ks), (BATCH, SEQ, D_MODEL), f32),
        "mem": jax.random.normal(next(ks), (BATCH, N_MEM, D_MODEL), f32),
        "even_pre_g": gain((ne, D_MODEL)),
        "even_w_in": nrm((ne, D_MODEL, EVEN_IN), D_MODEL ** -0.5),
        "even_a_ln_g": gain((ne, bw)),
        "even_a_ln_b": nrm((ne, bw), 0.02),
        "even_a_ws": nrm((ne, A_HEADS, CHUNK, CHUNK), CHUNK ** -0.5),
        "even_a_bs": nrm((ne, A_HEADS, CHUNK), 0.02),
        "even_b_conv": nrm((ne, SHORT_CONV, bw), SHORT_CONV ** -0.5),
        "even_mem_g": gain((ne, D_MODEL)),
        "even_w_kv": nrm((ne, D_MODEL, 2 * XA_WIDTH), D_MODEL ** -0.5),
        "even_w_out": nrm((ne, MIX_WIDTH, D_MODEL), MIX_WIDTH ** -0.5),
        "even_post_g": gain((ne, D_MODEL)),
        "odd_pre_g": gain((no, D_MODEL)),
        "odd_w_in": nrm((no, D_MODEL, ODD_IN), D_MODEL ** -0.5),
        "odd_c_wgrp": nrm((no, len(POOL_WINDOWS), C_GROUP, C_GROUP), C_GROUP ** -0.5),
        "odd_c_scale": gain((no, bw)),
        "odd_d_dw_w": nrm((no, CONF_CONV, bw), CONF_CONV ** -0.5),
        "odd_d_dw_b": nrm((no, bw), 0.02),
        "odd_d_ln_g": gain((no, bw)),
        "odd_d_ln_b": nrm((no, bw), 0.02),
        "odd_d_pw_w": nrm((no, bw, bw), bw ** -0.5),
        "odd_d_pw_b": nrm((no, bw), 0.02),
        "odd_mem_g": gain((no, D_MODEL)),
        "odd_w_kv": nrm((no, D_MODEL, 2 * XA_WIDTH), D_MODEL ** -0.5),
        "odd_w_out": nrm((no, MIX_WIDTH, D_MODEL), MIX_WIDTH ** -0.5),
        "odd_post_g": gain((no, D_MODEL)),
    }


def _fwd_reference(x, mem,
              even_pre_g, even_w_in, even_a_ln_g, even_a_ln_b, even_a_ws, even_a_bs, even_b_conv,
              even_mem_g, even_w_kv, even_w_out, even_post_g,
              odd_pre_g, odd_w_in, odd_c_wgrp, odd_c_scale, odd_d_dw_w, odd_d_dw_b, odd_d_ln_g,
              odd_d_ln_b, odd_d_pw_w, odd_d_pw_b, odd_mem_g, odd_w_kv, odd_w_out, odd_post_g):
    for layer in range(DEPTH):
        i = layer // 2
        if layer % 2 == 0:
            x = even_layer(x, mem, even_pre_g[i], even_w_in[i], even_a_ln_g[i], even_a_ln_b[i],
                           even_a_ws[i], even_a_bs[i], even_b_conv[i], even_mem_g[i], even_w_kv[i],
                           even_w_out[i], even_post_g[i])
        else:
            x = odd_layer(x, mem, odd_pre_g[i], odd_w_in[i], odd_c_wgrp[i], odd_c_scale[i],
                          odd_d_dw_w[i], odd_d_dw_b[i], odd_d_ln_g[i], odd_d_ln_b[i], odd_d_pw_w[i],
                          odd_d_pw_b[i], odd_mem_g[i], odd_w_kv[i], odd_w_out[i], odd_post_g[i])
    return x


import jax as _jax
import jax.numpy as _jnp

TWIN_FORMAT = 'train_step'
FWD_PARAMS = ['x', 'mem', 'even_pre_g', 'even_w_in', 'even_a_ln_g', 'even_a_ln_b', 'even_a_ws', 'even_a_bs', 'even_b_conv', 'even_mem_g', 'even_w_kv', 'even_w_out', 'even_post_g', 'odd_pre_g', 'odd_w_in', 'odd_c_wgrp', 'odd_c_scale', 'odd_d_dw_w', 'odd_d_dw_b', 'odd_d_ln_g', 'odd_d_ln_b', 'odd_d_pw_w', 'odd_d_pw_b', 'odd_mem_g', 'odd_w_kv', 'odd_w_out', 'odd_post_g']
TWIN_WEIGHTS = ['even_pre_g', 'even_w_in', 'even_a_ln_g', 'even_a_ln_b', 'even_a_ws', 'even_a_bs', 'even_b_conv', 'even_mem_g', 'even_w_kv', 'even_w_out', 'even_post_g', 'odd_pre_g', 'odd_w_in', 'odd_c_wgrp', 'odd_c_scale', 'odd_d_dw_w', 'odd_d_dw_b', 'odd_d_ln_g', 'odd_d_ln_b', 'odd_d_pw_w', 'odd_d_pw_b', 'odd_mem_g', 'odd_w_kv', 'odd_w_out', 'odd_post_g']
TWIN_DIFF_INPUT = 'x'
TWIN_INPUTS = ['x', 'mem', 'even_pre_g', 'even_w_in', 'even_a_ln_g', 'even_a_ln_b', 'even_a_ws', 'even_a_bs', 'even_b_conv', 'even_mem_g', 'even_w_kv', 'even_w_out', 'even_post_g', 'odd_pre_g', 'odd_w_in', 'odd_c_wgrp', 'odd_c_scale', 'odd_d_dw_w', 'odd_d_dw_b', 'odd_d_ln_g', 'odd_d_ln_b', 'odd_d_pw_w', 'odd_d_pw_b', 'odd_mem_g', 'odd_w_kv', 'odd_w_out', 'odd_post_g', 'loss_target', 'm_even_pre_g', 'm_even_w_in', 'm_even_a_ln_g', 'm_even_a_ln_b', 'm_even_a_ws', 'm_even_a_bs', 'm_even_b_conv', 'm_even_mem_g', 'm_even_w_kv', 'm_even_w_out', 'm_even_post_g', 'm_odd_pre_g', 'm_odd_w_in', 'm_odd_c_wgrp', 'm_odd_c_scale', 'm_odd_d_dw_w', 'm_odd_d_dw_b', 'm_odd_d_ln_g', 'm_odd_d_ln_b', 'm_odd_d_pw_w', 'm_odd_d_pw_b', 'm_odd_mem_g', 'm_odd_w_kv', 'm_odd_w_out', 'm_odd_post_g', 'v_even_pre_g', 'v_even_w_in', 'v_even_a_ln_g', 'v_even_a_ln_b', 'v_even_a_ws', 'v_even_a_bs', 'v_even_b_conv', 'v_even_mem_g', 'v_even_w_kv', 'v_even_w_out', 'v_even_post_g', 'v_odd_pre_g', 'v_odd_w_in', 'v_odd_c_wgrp', 'v_odd_c_scale', 'v_odd_d_dw_w', 'v_odd_d_dw_b', 'v_odd_d_ln_g', 'v_odd_d_ln_b', 'v_odd_d_pw_w', 'v_odd_d_pw_b', 'v_odd_mem_g', 'v_odd_w_kv', 'v_odd_w_out', 'v_odd_post_g']
TWIN_OUTPUTS = ['loss', 'grad_x', 'grad_even_pre_g', 'grad_even_w_in', 'grad_even_a_ln_g', 'grad_even_a_ln_b', 'grad_even_a_ws', 'grad_even_a_bs', 'grad_even_b_conv', 'grad_even_mem_g', 'grad_even_w_kv', 'grad_even_w_out', 'grad_even_post_g', 'grad_odd_pre_g', 'grad_odd_w_in', 'grad_odd_c_wgrp', 'grad_odd_c_scale', 'grad_odd_d_dw_w', 'grad_odd_d_dw_b', 'grad_odd_d_ln_g', 'grad_odd_d_ln_b', 'grad_odd_d_pw_w', 'grad_odd_d_pw_b', 'grad_odd_mem_g', 'grad_odd_w_kv', 'grad_odd_w_out', 'grad_odd_post_g', 'delta_even_pre_g', 'delta_even_w_in', 'delta_even_a_ln_g', 'delta_even_a_ln_b', 'delta_even_a_ws', 'delta_even_a_bs', 'delta_even_b_conv', 'delta_even_mem_g', 'delta_even_w_kv', 'delta_even_w_out', 'delta_even_post_g', 'delta_odd_pre_g', 'delta_odd_w_in', 'delta_odd_c_wgrp', 'delta_odd_c_scale', 'delta_odd_d_dw_w', 'delta_odd_d_dw_b', 'delta_odd_d_ln_g', 'delta_odd_d_ln_b', 'delta_odd_d_pw_w', 'delta_odd_d_pw_b', 'delta_odd_mem_g', 'delta_odd_w_kv', 'delta_odd_w_out', 'delta_odd_post_g', 'new_m_even_pre_g', 'new_m_even_w_in', 'new_m_even_a_ln_g', 'new_m_even_a_ln_b', 'new_m_even_a_ws', 'new_m_even_a_bs', 'new_m_even_b_conv', 'new_m_even_mem_g', 'new_m_even_w_kv', 'new_m_even_w_out', 'new_m_even_post_g', 'new_m_odd_pre_g', 'new_m_odd_w_in', 'new_m_odd_c_wgrp', 'new_m_odd_c_scale', 'new_m_odd_d_dw_w', 'new_m_odd_d_dw_b', 'new_m_odd_d_ln_g', 'new_m_odd_d_ln_b', 'new_m_odd_d_pw_w', 'new_m_odd_d_pw_b', 'new_m_odd_mem_g', 'new_m_odd_w_kv', 'new_m_odd_w_out', 'new_m_odd_post_g', 'new_v_even_pre_g', 'new_v_even_w_in', 'new_v_even_a_ln_g', 'new_v_even_a_ln_b', 'new_v_even_a_ws', 'new_v_even_a_bs', 'new_v_even_b_conv', 'new_v_even_mem_g', 'new_v_even_w_kv', 'new_v_even_w_out', 'new_v_even_post_g', 'new_v_odd_pre_g', 'new_v_odd_w_in', 'new_v_odd_c_wgrp', 'new_v_odd_c_scale', 'new_v_odd_d_dw_w', 'new_v_odd_d_dw_b', 'new_v_odd_d_ln_g', 'new_v_odd_d_ln_b', 'new_v_odd_d_pw_w', 'new_v_odd_d_pw_b', 'new_v_odd_mem_g', 'new_v_odd_w_kv', 'new_v_odd_w_out', 'new_v_odd_post_g']
TWIN_LEAF_KINDS = {'loss': 'loss', 'grad_x': 'grad_x', 'grad_even_pre_g': 'grad_w', 'grad_even_w_in': 'grad_w', 'grad_even_a_ln_g': 'grad_w', 'grad_even_a_ln_b': 'grad_w', 'grad_even_a_ws': 'grad_w', 'grad_even_a_bs': 'grad_w', 'grad_even_b_conv': 'grad_w', 'grad_even_mem_g': 'grad_w', 'grad_even_w_kv': 'grad_w', 'grad_even_w_out': 'grad_w', 'grad_even_post_g': 'grad_w', 'grad_odd_pre_g': 'grad_w', 'grad_odd_w_in': 'grad_w', 'grad_odd_c_wgrp': 'grad_w', 'grad_odd_c_scale': 'grad_w', 'grad_odd_d_dw_w': 'grad_w', 'grad_odd_d_dw_b': 'grad_w', 'grad_odd_d_ln_g': 'grad_w', 'grad_odd_d_ln_b': 'grad_w', 'grad_odd_d_pw_w': 'grad_w', 'grad_odd_d_pw_b': 'grad_w', 'grad_odd_mem_g': 'grad_w', 'grad_odd_w_kv': 'grad_w', 'grad_odd_w_out': 'grad_w', 'grad_odd_post_g': 'grad_w', 'delta_even_pre_g': 'delta_w', 'delta_even_w_in': 'delta_w', 'delta_even_a_ln_g': 'delta_w', 'delta_even_a_ln_b': 'delta_w', 'delta_even_a_ws': 'delta_w', 'delta_even_a_bs': 'delta_w', 'delta_even_b_conv': 'delta_w', 'delta_even_mem_g': 'delta_w', 'delta_even_w_kv': 'delta_w', 'delta_even_w_out': 'delta_w', 'delta_even_post_g': 'delta_w', 'delta_odd_pre_g': 'delta_w', 'delta_odd_w_in': 'delta_w', 'delta_odd_c_wgrp': 'delta_w', 'delta_odd_c_scale': 'delta_w', 'delta_odd_d_dw_w': 'delta_w', 'delta_odd_d_dw_b': 'delta_w', 'delta_odd_d_ln_g': 'delta_w', 'delta_odd_d_ln_b': 'delta_w', 'delta_odd_d_pw_w': 'delta_w', 'delta_odd_d_pw_b': 'delta_w', 'delta_odd_mem_g': 'delta_w', 'delta_odd_w_kv': 'delta_w', 'delta_odd_w_out': 'delta_w', 'delta_odd_post_g': 'delta_w', 'new_m_even_pre_g': 'new_m', 'new_m_even_w_in': 'new_m', 'new_m_even_a_ln_g': 'new_m', 'new_m_even_a_ln_b': 'new_m', 'new_m_even_a_ws': 'new_m', 'new_m_even_a_bs': 'new_m', 'new_m_even_b_conv': 'new_m', 'new_m_even_mem_g': 'new_m', 'new_m_even_w_kv': 'new_m', 'new_m_even_w_out': 'new_m', 'new_m_even_post_g': 'new_m', 'new_m_odd_pre_g': 'new_m', 'new_m_odd_w_in': 'new_m', 'new_m_odd_c_wgrp': 'new_m', 'new_m_odd_c_scale': 'new_m', 'new_m_odd_d_dw_w': 'new_m', 'new_m_odd_d_dw_b': 'new_m', 'new_m_odd_d_ln_g': 'new_m', 'new_m_odd_d_ln_b': 'new_m', 'new_m_odd_d_pw_w': 'new_m', 'new_m_odd_d_pw_b': 'new_m', 'new_m_odd_mem_g': 'new_m', 'new_m_odd_w_kv': 'new_m', 'new_m_odd_w_out': 'new_m', 'new_m_odd_post_g': 'new_m', 'new_v_even_pre_g': 'new_v', 'new_v_even_w_in': 'new_v', 'new_v_even_a_ln_g': 'new_v', 'new_v_even_a_ln_b': 'new_v', 'new_v_even_a_ws': 'new_v', 'new_v_even_a_bs': 'new_v', 'new_v_even_b_conv': 'new_v', 'new_v_even_mem_g': 'new_v', 'new_v_even_w_kv': 'new_v', 'new_v_even_w_out': 'new_v', 'new_v_even_post_g': 'new_v', 'new_v_odd_pre_g': 'new_v', 'new_v_odd_w_in': 'new_v', 'new_v_odd_c_wgrp': 'new_v', 'new_v_odd_c_scale': 'new_v', 'new_v_odd_d_dw_w': 'new_v', 'new_v_odd_d_dw_b': 'new_v', 'new_v_odd_d_ln_g': 'new_v', 'new_v_odd_d_ln_b': 'new_v', 'new_v_odd_d_pw_w': 'new_v', 'new_v_odd_d_pw_b': 'new_v', 'new_v_odd_mem_g': 'new_v', 'new_v_odd_w_kv': 'new_v', 'new_v_odd_w_out': 'new_v', 'new_v_odd_post_g': 'new_v'}


def _forward(args):
    return _fwd_reference(*[args[k] for k in FWD_PARAMS])


def _output_shape():
    out = _jax.eval_shape(lambda: _forward(_fwd_setup_inputs(0)))
    return out.shape, out.dtype

N_MICROBATCH = 1
ADAM_LR = 0.001
ADAM_B1 = 0.9
ADAM_B2 = 0.999
ADAM_EPS = 1e-08
ADAM_WD = 0.01
ADAM_STEP = 10
PER_EXAMPLE_BATCH_AXIS = {'x': 0, 'mem': 0, 'loss_target': 0}
SHARED_INPUTS = []
_WEIGHT_DTYPES = {'even_pre_g': _jnp.float32, 'even_w_in': _jnp.float32, 'even_a_ln_g': _jnp.float32, 'even_a_ln_b': _jnp.float32, 'even_a_ws': _jnp.float32, 'even_a_bs': _jnp.float32, 'even_b_conv': _jnp.float32, 'even_mem_g': _jnp.float32, 'even_w_kv': _jnp.float32, 'even_w_out': _jnp.float32, 'even_post_g': _jnp.float32, 'odd_pre_g': _jnp.float32, 'odd_w_in': _jnp.float32, 'odd_c_wgrp': _jnp.float32, 'odd_c_scale': _jnp.float32, 'odd_d_dw_w': _jnp.float32, 'odd_d_dw_b': _jnp.float32, 'odd_d_ln_g': _jnp.float32, 'odd_d_ln_b': _jnp.float32, 'odd_d_pw_w': _jnp.float32, 'odd_d_pw_b': _jnp.float32, 'odd_mem_g': _jnp.float32, 'odd_w_kv': _jnp.float32, 'odd_w_out': _jnp.float32, 'odd_post_g': _jnp.float32}
MOMENT_SCALE = {'even_pre_g': 9.190626e-01, 'even_w_in': 3.491501e-01, 'even_a_ln_g': 3.316886e-01, 'even_a_ln_b': 3.366513e-01, 'even_a_ws': 3.649468e-01, 'even_a_bs': 5.797506e-01, 'even_b_conv': 4.618760e-01, 'even_mem_g': 5.201518e-02, 'even_w_kv': 5.040677e-02, 'even_w_out': 4.765826e-01, 'even_post_g': 3.205809e+01, 'odd_pre_g': 4.691101e-01, 'odd_w_in': 2.401324e-01, 'odd_c_wgrp': 3.922344e-01, 'odd_c_scale': 4.047247e-01, 'odd_d_dw_w': 2.140784e-01, 'odd_d_dw_b': 8.242782e-01, 'odd_d_ln_g': 3.784543e-01, 'odd_d_ln_b': 5.325035e-01, 'odd_d_pw_w': 2.598236e-01, 'odd_d_pw_b': 1.006265e+00, 'odd_mem_g': 4.429641e-02, 'odd_w_kv': 4.071958e-02, 'odd_w_out': 4.181613e-01, 'odd_post_g': 3.194195e+01}


def _to_microbatches(a, axis):
    t = _jnp.moveaxis(a, axis, 0)
    t = t.reshape((N_MICROBATCH, t.shape[0] // N_MICROBATCH) + t.shape[1:])
    return _jnp.moveaxis(t, 1, axis + 1)


def setup_inputs(seed: int = 0) -> dict:
    inp = _fwd_setup_inputs(seed)
    key = _jax.random.fold_in(_jax.random.key(seed), 7919)
    shape, _ = _output_shape()
    out = dict(inp)
    out["loss_target"] = _jax.random.normal(_jax.random.fold_in(key, 0), shape, _jnp.float32)
    for i, name in enumerate(TWIN_WEIGHTS):
        w = inp[name].astype(_jnp.float32)
        if MOMENT_SCALE is None:
            s = _jnp.sqrt(_jnp.mean(_jnp.square(w)) + 1e-30)
        else:
            s = MOMENT_SCALE[name]
        km, kv = _jax.random.split(_jax.random.fold_in(key, i + 1))
        out[name] = w
        out["m_" + name] = s * _jax.random.normal(km, w.shape, _jnp.float32)
        out["v_" + name] = (s * s) * _jax.random.uniform(kv, w.shape, _jnp.float32, 0.5, 1.5)
    if N_MICROBATCH > 1:
        for name, axis in PER_EXAMPLE_BATCH_AXIS.items():
            out[name] = _to_microbatches(out[name], axis)
    return {'x': out['x'], 'mem': out['mem'], 'even_pre_g': out['even_pre_g'], 'even_w_in': out['even_w_in'], 'even_a_ln_g': out['even_a_ln_g'], 'even_a_ln_b': out['even_a_ln_b'], 'even_a_ws': out['even_a_ws'], 'even_a_bs': out['even_a_bs'], 'even_b_conv': out['even_b_conv'], 'even_mem_g': out['even_mem_g'], 'even_w_kv': out['even_w_kv'], 'even_w_out': out['even_w_out'], 'even_post_g': out['even_post_g'], 'odd_pre_g': out['odd_pre_g'], 'odd_w_in': out['odd_w_in'], 'odd_c_wgrp': out['odd_c_wgrp'], 'odd_c_scale': out['odd_c_scale'], 'odd_d_dw_w': out['odd_d_dw_w'], 'odd_d_dw_b': out['odd_d_dw_b'], 'odd_d_ln_g': out['odd_d_ln_g'], 'odd_d_ln_b': out['odd_d_ln_b'], 'odd_d_pw_w': out['odd_d_pw_w'], 'odd_d_pw_b': out['odd_d_pw_b'], 'odd_mem_g': out['odd_mem_g'], 'odd_w_kv': out['odd_w_kv'], 'odd_w_out': out['odd_w_out'], 'odd_post_g': out['odd_post_g'], 'loss_target': out['loss_target'], 'm_even_pre_g': out['m_even_pre_g'], 'm_even_w_in': out['m_even_w_in'], 'm_even_a_ln_g': out['m_even_a_ln_g'], 'm_even_a_ln_b': out['m_even_a_ln_b'], 'm_even_a_ws': out['m_even_a_ws'], 'm_even_a_bs': out['m_even_a_bs'], 'm_even_b_conv': out['m_even_b_conv'], 'm_even_mem_g': out['m_even_mem_g'], 'm_even_w_kv': out['m_even_w_kv'], 'm_even_w_out': out['m_even_w_out'], 'm_even_post_g': out['m_even_post_g'], 'm_odd_pre_g': out['m_odd_pre_g'], 'm_odd_w_in': out['m_odd_w_in'], 'm_odd_c_wgrp': out['m_odd_c_wgrp'], 'm_odd_c_scale': out['m_odd_c_scale'], 'm_odd_d_dw_w': out['m_odd_d_dw_w'], 'm_odd_d_dw_b': out['m_odd_d_dw_b'], 'm_odd_d_ln_g': out['m_odd_d_ln_g'], 'm_odd_d_ln_b': out['m_odd_d_ln_b'], 'm_odd_d_pw_w': out['m_odd_d_pw_w'], 'm_odd_d_pw_b': out['m_odd_d_pw_b'], 'm_odd_mem_g': out['m_odd_mem_g'], 'm_odd_w_kv': out['m_odd_w_kv'], 'm_odd_w_out': out['m_odd_w_out'], 'm_odd_post_g': out['m_odd_post_g'], 'v_even_pre_g': out['v_even_pre_g'], 'v_even_w_in': out['v_even_w_in'], 'v_even_a_ln_g': out['v_even_a_ln_g'], 'v_even_a_ln_b': out['v_even_a_ln_b'], 'v_even_a_ws': out['v_even_a_ws'], 'v_even_a_bs': out['v_even_a_bs'], 'v_even_b_conv': out['v_even_b_conv'], 'v_even_mem_g': out['v_even_mem_g'], 'v_even_w_kv': out['v_even_w_kv'], 'v_even_w_out': out['v_even_w_out'], 'v_even_post_g': out['v_even_post_g'], 'v_odd_pre_g': out['v_odd_pre_g'], 'v_odd_w_in': out['v_odd_w_in'], 'v_odd_c_wgrp': out['v_odd_c_wgrp'], 'v_odd_c_scale': out['v_odd_c_scale'], 'v_odd_d_dw_w': out['v_odd_d_dw_w'], 'v_odd_d_dw_b': out['v_odd_d_dw_b'], 'v_odd_d_ln_g': out['v_odd_d_ln_g'], 'v_odd_d_ln_b': out['v_odd_d_ln_b'], 'v_odd_d_pw_w': out['v_odd_d_pw_w'], 'v_odd_d_pw_b': out['v_odd_d_pw_b'], 'v_odd_mem_g': out['v_odd_mem_g'], 'v_odd_w_kv': out['v_odd_w_kv'], 'v_odd_w_out': out['v_odd_w_out'], 'v_odd_post_g': out['v_odd_post_g']}


def _loss(weights, diff, rest, loss_target):
    with _jax.named_scope("forward"):
        args = {**rest, TWIN_DIFF_INPUT: diff, **{k: w.astype(_WEIGHT_DTYPES[k]) for k, w in weights.items()}}
        y = _forward(args)
    with _jax.named_scope("loss_head"):
        err = _jnp.square(y.astype(_jnp.float32) - loss_target)
        return 0.5 * _jnp.sum(_jnp.mean(err, axis=-1)) if err.ndim else 0.5 * err


def _adamw(w, g, m, v):
    m = ADAM_B1 * m + (1.0 - ADAM_B1) * g
    v = ADAM_B2 * v + (1.0 - ADAM_B2) * _jnp.square(g)
    m_hat = m / (1.0 - ADAM_B1 ** ADAM_STEP)
    v_hat = v / (1.0 - ADAM_B2 ** ADAM_STEP)
    delta = -ADAM_LR * (m_hat / (_jnp.sqrt(v_hat) + ADAM_EPS) + ADAM_WD * w)
    return delta, m, v


def reference(x, mem, even_pre_g, even_w_in, even_a_ln_g, even_a_ln_b, even_a_ws, even_a_bs, even_b_conv, even_mem_g, even_w_kv, even_w_out, even_post_g, odd_pre_g, odd_w_in, odd_c_wgrp, odd_c_scale, odd_d_dw_w, odd_d_dw_b, odd_d_ln_g, odd_d_ln_b, odd_d_pw_w, odd_d_pw_b, odd_mem_g, odd_w_kv, odd_w_out, odd_post_g, loss_target, m_even_pre_g, m_even_w_in, m_even_a_ln_g, m_even_a_ln_b, m_even_a_ws, m_even_a_bs, m_even_b_conv, m_even_mem_g, m_even_w_kv, m_even_w_out, m_even_post_g, m_odd_pre_g, m_odd_w_in, m_odd_c_wgrp, m_odd_c_scale, m_odd_d_dw_w, m_odd_d_dw_b, m_odd_d_ln_g, m_odd_d_ln_b, m_odd_d_pw_w, m_odd_d_pw_b, m_odd_mem_g, m_odd_w_kv, m_odd_w_out, m_odd_post_g, v_even_pre_g, v_even_w_in, v_even_a_ln_g, v_even_a_ln_b, v_even_a_ws, v_even_a_bs, v_even_b_conv, v_even_mem_g, v_even_w_kv, v_even_w_out, v_even_post_g, v_odd_pre_g, v_odd_w_in, v_odd_c_wgrp, v_odd_c_scale, v_odd_d_dw_w, v_odd_d_dw_b, v_odd_d_ln_g, v_odd_d_ln_b, v_odd_d_pw_w, v_odd_d_pw_b, v_odd_mem_g, v_odd_w_kv, v_odd_w_out, v_odd_post_g):
    given = dict(x=x, mem=mem, even_pre_g=even_pre_g, even_w_in=even_w_in, even_a_ln_g=even_a_ln_g, even_a_ln_b=even_a_ln_b, even_a_ws=even_a_ws, even_a_bs=even_a_bs, even_b_conv=even_b_conv, even_mem_g=even_mem_g, even_w_kv=even_w_kv, even_w_out=even_w_out, even_post_g=even_post_g, odd_pre_g=odd_pre_g, odd_w_in=odd_w_in, odd_c_wgrp=odd_c_wgrp, odd_c_scale=odd_c_scale, odd_d_dw_w=odd_d_dw_w, odd_d_dw_b=odd_d_dw_b, odd_d_ln_g=odd_d_ln_g, odd_d_ln_b=odd_d_ln_b, odd_d_pw_w=odd_d_pw_w, odd_d_pw_b=odd_d_pw_b, odd_mem_g=odd_mem_g, odd_w_kv=odd_w_kv, odd_w_out=odd_w_out, odd_post_g=odd_post_g, loss_target=loss_target, m_even_pre_g=m_even_pre_g, m_even_w_in=m_even_w_in, m_even_a_ln_g=m_even_a_ln_g, m_even_a_ln_b=m_even_a_ln_b, m_even_a_ws=m_even_a_ws, m_even_a_bs=m_even_a_bs, m_even_b_conv=m_even_b_conv, m_even_mem_g=m_even_mem_g, m_even_w_kv=m_even_w_kv, m_even_w_out=m_even_w_out, m_even_post_g=m_even_post_g, m_odd_pre_g=m_odd_pre_g, m_odd_w_in=m_odd_w_in, m_odd_c_wgrp=m_odd_c_wgrp, m_odd_c_scale=m_odd_c_scale, m_odd_d_dw_w=m_odd_d_dw_w, m_odd_d_dw_b=m_odd_d_dw_b, m_odd_d_ln_g=m_odd_d_ln_g, m_odd_d_ln_b=m_odd_d_ln_b, m_odd_d_pw_w=m_odd_d_pw_w, m_odd_d_pw_b=m_odd_d_pw_b, m_odd_mem_g=m_odd_mem_g, m_odd_w_kv=m_odd_w_kv, m_odd_w_out=m_odd_w_out, m_odd_post_g=m_odd_post_g, v_even_pre_g=v_even_pre_g, v_even_w_in=v_even_w_in, v_even_a_ln_g=v_even_a_ln_g, v_even_a_ln_b=v_even_a_ln_b, v_even_a_ws=v_even_a_ws, v_even_a_bs=v_even_a_bs, v_even_b_conv=v_even_b_conv, v_even_mem_g=v_even_mem_g, v_even_w_kv=v_even_w_kv, v_even_w_out=v_even_w_out, v_even_post_g=v_even_post_g, v_odd_pre_g=v_odd_pre_g, v_odd_w_in=v_odd_w_in, v_odd_c_wgrp=v_odd_c_wgrp, v_odd_c_scale=v_odd_c_scale, v_odd_d_dw_w=v_odd_d_dw_w, v_odd_d_dw_b=v_odd_d_dw_b, v_odd_d_ln_g=v_odd_d_ln_g, v_odd_d_ln_b=v_odd_d_ln_b, v_odd_d_pw_w=v_odd_d_pw_w, v_odd_d_pw_b=v_odd_d_pw_b, v_odd_mem_g=v_odd_mem_g, v_odd_w_kv=v_odd_w_kv, v_odd_w_out=v_odd_w_out, v_odd_post_g=v_odd_post_g)
    weights = {n: given[n] for n in TWIN_WEIGHTS}
    shared = {n: given[n] for n in SHARED_INPUTS}
    per_example = {n: given[n] for n in ['x', 'mem']}
    grad_fn = _jax.value_and_grad(_loss, argnums=(0, 1))

    def one_microbatch(ex, loss_target):
        ex = dict(ex)
        diff = ex.pop(TWIN_DIFF_INPUT)
        return grad_fn(weights, diff, {**shared, **ex}, loss_target)

    if N_MICROBATCH == 1:
        loss, (grad_w, grad_x) = one_microbatch(per_example, given["loss_target"])
    else:
        def body(carry, xs):
            loss_sum, grad_sum = carry
            l_k, (gw_k, gx_k) = one_microbatch(xs[0], xs[1])
            with _jax.named_scope("update"):
                return (loss_sum + l_k, _jax.tree.map(_jnp.add, grad_sum, gw_k)), gx_k

        init = (_jnp.zeros((), _jnp.float32), _jax.tree.map(_jnp.zeros_like, weights))
        (loss, grad_w), grad_x = _jax.lax.scan(body, init, (per_example, given["loss_target"]))
    with _jax.named_scope("update"):
        delta_w, new_m, new_v = {}, {}, {}
        for n in TWIN_WEIGHTS:
            delta_w[n], new_m[n], new_v[n] = _adamw(weights[n], grad_w[n], given["m_" + n], given["v_" + n])
    return (loss, grad_x, *[grad_w[n] for n in TWIN_WEIGHTS], *[delta_w[n] for n in TWIN_WEIGHTS],
            *[new_m[n] for n in TWIN_WEIGHTS], *[new_v[n] for n in TWIN_WEIGHTS])
```

```python
import functools

import jax
import jax.numpy as jnp
from jax import lax
from jax.experimental import pallas as pl
from jax.experimental.pallas import tpu as pltpu

F32 = jnp.float32
BF16 = jnp.bfloat16
MESH = pl.DeviceIdType.MESH

D = 1024
N_MEM = 256
MIX = 2048
XA = 512
HD = 128
BW = 768
CH = 128
EPS = 1e-6
SCALE = HD ** -0.5
POOL_WINDOWS = (2, 4, 8, 16)
CONF = 31
EVEN_IN = 6400
ODD_IN = 4864
N_CHIPS = 4

ADAM_LR = 0.001
ADAM_B1 = 0.9
ADAM_B2 = 0.999
ADAM_EPS = 1e-08
ADAM_WD = 0.01
ADAM_STEP = 10

TS = 256
HALO = 32
VMEM_LIMIT = 56 * 1024 * 1024


def _cp(sem=None):
    return pltpu.CompilerParams(dimension_semantics=sem, vmem_limit_bytes=VMEM_LIMIT)


def _dot(a, b):
    return jnp.dot(a, b, preferred_element_type=F32)


def _dot_nt(a, b):
    return lax.dot_general(a, b, (((1,), (1,)), ((), ())), preferred_element_type=F32)


def _dot_tn(a, b):
    return lax.dot_general(a, b, (((0,), (0,)), ((), ())), preferred_element_type=F32)


def _sigmoid(x):
    return 1.0 / (1.0 + jnp.exp(-x))


def _resident(shape):
    return pl.BlockSpec(shape, lambda *_: (0,) * len(shape), pipeline_mode=pl.Buffered(1))


def _const(shape):
    return pl.BlockSpec(shape, lambda *_: (0,) * len(shape))


def _kv_fwd(mem, mem_g, wkv, name):
    def body(mem_ref, g_ref, w_ref, kv_ref):
        m = mem_ref[...]
        r = lax.rsqrt(jnp.mean(m * m, axis=-1, keepdims=True) + EPS)
        mn = (m * r * g_ref[...]).astype(BF16)
        kv_ref[...] = _dot(mn, w_ref[...]).astype(BF16)

    return pl.pallas_call(body, out_shape=jax.ShapeDtypeStruct((N_MEM, D), BF16), name=name,
                          compiler_params=_cp())(mem, mem_g, wkv)


def _kv_bwd(mem, mem_g, wkv, dkv, name):
    def body(mem_ref, g_ref, w_ref, dkv_ref, dw_ref, dg_ref):
        m = mem_ref[...]
        r = lax.rsqrt(jnp.mean(m * m, axis=-1, keepdims=True) + EPS)
        mh = m * r
        mn = (mh * g_ref[...]).astype(BF16)
        dkv = dkv_ref[...].astype(BF16)
        dw_ref[...] = _dot_tn(mn, dkv).astype(BF16)
        dmn = _dot_nt(dkv, w_ref[...])
        dg_ref[...] = jnp.sum(dmn * mh, axis=0, keepdims=True)

    return pl.pallas_call(body, out_shape=(jax.ShapeDtypeStruct((D, D), BF16), jax.ShapeDtypeStruct((1, D), F32)),
                          name=name, compiler_params=_cp())(mem, mem_g, wkv, dkv)


def _in_fwd(x, pre_g, w_t, name):
    s, n = x.shape[0], w_t.shape[0]
    tm = min(512, s)
    nc = 256

    def body(x_ref, g_ref, w_ref, p_ref, h_ref):
        xv = x_ref[...]
        r = lax.rsqrt(jnp.mean(xv * xv, axis=-1, keepdims=True) + EPS)
        h = (xv * r * g_ref[...]).astype(BF16)
        h_ref[...] = h
        for j in range(n // nc):
            p_ref[:, j * nc:(j + 1) * nc] = _dot_nt(h, w_ref[j * nc:(j + 1) * nc, :]).astype(BF16)

    return pl.pallas_call(
        body, grid=(s // tm,), name=name,
        out_shape=(jax.ShapeDtypeStruct((s, n), BF16), jax.ShapeDtypeStruct((s, D), BF16)),
        in_specs=[pl.BlockSpec((tm, D), lambda i: (i, 0)), _const((1, D)), _resident((n, D))],
        out_specs=(pl.BlockSpec((tm, n), lambda i: (i, 0)), pl.BlockSpec((tm, D), lambda i: (i, 0))),
        compiler_params=_cp(("arbitrary",)),
    )(x, pre_g, w_t)


def _xattn_fwd(q, kv_ref):
    outs, probs = [], []
    for h in range(XA // HD):
        qh = q[:, h * HD:(h + 1) * HD]
        kh = kv_ref[:, h * HD:(h + 1) * HD]
        vh = kv_ref[:, XA + h * HD:XA + (h + 1) * HD]
        sc = _dot_nt(qh, kh) * SCALE
        e = jnp.exp(sc - jnp.max(sc, axis=-1, keepdims=True))
        pr = e / jnp.sum(e, axis=-1, keepdims=True)
        outs.append(_dot(pr.astype(BF16), vh))
        probs.append(pr)
    return jnp.concatenate(outs, axis=-1), probs


def _xattn_bwd(dyx, q, probs, kv_ref, dkv_ref):
    dqs = []
    for h in range(XA // HD):
        qh = q[:, h * HD:(h + 1) * HD]
        kh = kv_ref[:, h * HD:(h + 1) * HD]
        vh = kv_ref[:, XA + h * HD:XA + (h + 1) * HD]
        dy = dyx[:, h * HD:(h + 1) * HD].astype(BF16)
        pr = probs[h]
        dp = _dot_nt(dy, vh)
        ds = (pr * (dp - jnp.sum(dp * pr, axis=-1, keepdims=True))).astype(BF16)
        dqs.append(_dot(ds, kh) * SCALE)
        dkv_ref[:, h * HD:(h + 1) * HD] += _dot_tn(ds, qh) * SCALE
        dkv_ref[:, XA + h * HD:XA + (h + 1) * HD] += _dot_tn(pr.astype(BF16), dy)
    return jnp.concatenate(dqs, axis=-1)


def _layer_norm_fwd(v, g, b):
    mu = jnp.mean(v, axis=-1, keepdims=True)
    vc = v - mu
    rstd = lax.rsqrt(jnp.mean(vc * vc, axis=-1, keepdims=True) + EPS)
    vhat = vc * rstd
    return vhat * g + b, vhat, rstd


def _layer_norm_bwd(dy, vhat, rstd, g):
    dvh = dy * g
    return rstd * (dvh - jnp.mean(dvh, axis=-1, keepdims=True) - vhat * jnp.mean(dvh * vhat, axis=-1, keepdims=True))


def _head_masks():
    col = lax.broadcasted_iota(jnp.int32, (1, BW), 1)
    return [(col >= h * (BW // 4)) & (col < (h + 1) * (BW // 4)) for h in range(4)]


def _halo_prev(nblk_per_tile):
    return lambda i: (jnp.maximum(i * nblk_per_tile - 1, 0), 0)


def _row_ids(i, t):
    return i * t + lax.broadcasted_iota(jnp.int32, (t, 1), 0)


def _even_mix(i, p_ref, ph_ref, ln_g, ln_b, wcat_ref, bsg_ref, bconv_ref, wbuf):
    t = p_ref.shape[0]
    u = p_ref[:, 0:BW].astype(F32)
    v = p_ref[:, BW:2 * BW].astype(F32)
    bg = p_ref[:, 2 * BW:3 * BW].astype(F32)
    cg = p_ref[:, 3 * BW:4 * BW].astype(F32)
    xin = p_ref[:, 4 * BW:5 * BW].astype(F32)
    vn, vhat, rstd = _layer_norm_fwd(v, ln_g, ln_b)
    masks = _head_masks()
    sgs, vsts = [], []
    for n in range(t // CH):
        vn_c = vn[n * CH:(n + 1) * CH]
        vst = jnp.concatenate([jnp.where(m, vn_c, 0.0) for m in masks], axis=0).astype(BF16)
        sgs.append(_dot(wcat_ref[...], vst) + bsg_ref[...])
        vsts.append(vst)
    sg = jnp.concatenate(sgs, axis=0)
    ya = u * sg
    w_halo = ph_ref[:, 3 * BW:4 * BW].astype(F32) * ph_ref[:, 4 * BW:5 * BW].astype(F32)
    wbuf[0:HALO, :] = jnp.where(i > 0, w_halo, 0.0)
    wbuf[HALO:HALO + t, :] = cg * xin
    conv = (bconv_ref[0:1, :] * wbuf[pl.ds(HALO - 2, t), :] + bconv_ref[1:2, :] * wbuf[pl.ds(HALO - 1, t), :]
            + bconv_ref[2:3, :] * wbuf[pl.ds(HALO, t), :])
    yb = bg * conv
    return dict(u=u, bg=bg, vhat=vhat, rstd=rstd, sg=sg, vsts=vsts, conv=conv, ya=ya, yb=yb, masks=masks)


def _pool_select(vals):
    col = lax.broadcasted_iota(jnp.int32, (1, BW), 1)
    g = BW // 4
    return jnp.where(col < g, vals[0], jnp.where(col < 2 * g, vals[1], jnp.where(col < 3 * g, vals[2], vals[3])))


def _inv_counts(i, t):
    rows = _row_ids(i, t) + 1
    return [1.0 / jnp.minimum(rows, w).astype(F32) for w in POOL_WINDOWS]


def _odd_mix(i, p_ref, ph_ref, wbd_ref, cscale, dww_ref, dwb, ln_g, ln_b, pww_ref, pwb, zbuf, gbuf):
    t = p_ref.shape[0]
    zc = p_ref[:, 0:BW].astype(F32)
    ga = p_ref[:, BW:2 * BW].astype(F32)
    gb = p_ref[:, 2 * BW:3 * BW].astype(F32)
    zbuf[0:HALO, :] = jnp.where(i > 0, ph_ref[:, 0:BW].astype(F32), 0.0)
    zbuf[HALO:HALO + t, :] = zc
    acc = zc
    sums = []
    k = 1
    for w in POOL_WINDOWS:
        while k < w:
            acc = acc + zbuf[pl.ds(HALO - k, t), :]
            k += 1
        sums.append(acc)
    inv = _inv_counts(i, t)
    pooled = _pool_select([s_ * c_ for s_, c_ in zip(sums, inv)]) - zc
    pooled_bf = pooled.astype(BF16)
    pre = _dot(pooled_bf, wbd_ref[...])
    yc = pre * cscale
    sgb = _sigmoid(gb)
    z = ga * sgb
    gh_a = ph_ref[:, BW:2 * BW].astype(F32)
    gh_b = ph_ref[:, 2 * BW:3 * BW].astype(F32)
    gbuf[0:HALO, :] = jnp.where(i > 0, gh_a * _sigmoid(gh_b), 0.0)
    gbuf[HALO:HALO + t, :] = z
    cv = dwb + dww_ref[CONF - 1:CONF, :] * z
    for k in range(CONF - 1):
        cv = cv + dww_ref[k:k + 1, :] * gbuf[pl.ds(HALO - (CONF - 1) + k, t), :]
    zl, zhat, rstd = _layer_norm_fwd(cv, ln_g, ln_b)
    szl = _sigmoid(zl)
    zs = (zl * szl).astype(BF16)
    yd = _dot(zs, pww_ref[...]) + pwb
    return dict(ga=ga, sgb=sgb, pooled_bf=pooled_bf, pre=pre, yc=yc, zhat=zhat, rstd=rstd, zl=zl, szl=szl,
                zs=zs, yd=yd, inv=inv)


def _post_norm(o, post_g):
    r = lax.rsqrt(jnp.mean(o * o, axis=-1, keepdims=True) + EPS)
    return o * r, r


def _gate_out(y_a, y_b, y_x, gate, wout_ref):
    sgt = _sigmoid(gate)
    sgate = gate * sgt
    ys = [(y_a * sgate[:, 0:BW]).astype(BF16), (y_b * sgate[:, BW:2 * BW]).astype(BF16),
          (y_x * sgate[:, 2 * BW:MIX]).astype(BF16)]
    o = (_dot(ys[0], wout_ref[0:BW, :]) + _dot(ys[1], wout_ref[BW:2 * BW, :]) + _dot(ys[2], wout_ref[2 * BW:MIX, :]))
    return o, ys, sgt, sgate


def _tile_specs(s, n):
    nh = TS // HALO
    return pl.BlockSpec((TS, n), lambda i: (i, 0)), pl.BlockSpec((HALO, n), _halo_prev(nh))


def _even_fwd(x, p, kv, ln_g, ln_b, wcat, bsg, bconv, wout, post_g):
    s = x.shape[0]

    def body(x_ref, p_ref, ph_ref, kv_ref, lng, lnb, wcat_ref, bsg_ref, bconv_ref, wout_ref, pg, x1_ref, o_ref, wbuf):
        i = pl.program_id(0)
        mx = _even_mix(i, p_ref, ph_ref, lng[...], lnb[...], wcat_ref, bsg_ref, bconv_ref, wbuf)
        yx, _ = _xattn_fwd(p_ref[:, 5 * BW:5 * BW + XA], kv_ref)
        gate = p_ref[:, 5 * BW + XA:EVEN_IN].astype(F32)
        o, _, _, _ = _gate_out(mx["ya"], mx["yb"], yx, gate, wout_ref)
        n, _ = _post_norm(o, pg[...])
        o_ref[...] = o
        x1_ref[...] = x_ref[...] + n * pg[...]

    tile, halo = _tile_specs(s, EVEN_IN)
    row = pl.BlockSpec((TS, D), lambda i: (i, 0))
    return pl.pallas_call(
        body, grid=(s // TS,), name="even_fwd",
        out_shape=(jax.ShapeDtypeStruct((s, D), F32), jax.ShapeDtypeStruct((s, D), F32)),
        in_specs=[row, tile, halo, _const((N_MEM, D)), _const((1, BW)), _const((1, BW)), _const((CH, 4 * CH)),
                  _const((CH, BW)), _const((3, BW)), _resident((MIX, D)), _const((1, D))],
        out_specs=(row, row),
        scratch_shapes=[pltpu.VMEM((HALO + TS, BW), F32)],
        compiler_params=_cp(("arbitrary",)),
    )(x, p, p, kv, ln_g, ln_b, wcat, bsg, bconv, wout, post_g)


def _odd_fwd(x1, p, kv, wbd, cscale, dww, dwb, ln_g, ln_b, pww, pwb, wout, post_g, target):
    s = x1.shape[0]

    def body(x_ref, p_ref, ph_ref, kv_ref, wbd_ref, cs, dww_ref, dwb_ref, lng, lnb, pww_ref, pwb_ref, wout_ref, pg,
             tgt_ref, dx_ref, o_ref, loss_ref, zbuf, gbuf):
        i = pl.program_id(0)
        mx = _odd_mix(i, p_ref, ph_ref, wbd_ref, cs[...], dww_ref, dwb_ref[...], lng[...], lnb[...], pww_ref,
                      pwb_ref[...], zbuf, gbuf)
        yx, _ = _xattn_fwd(p_ref[:, 3 * BW:3 * BW + XA], kv_ref)
        gate = p_ref[:, 3 * BW + XA:ODD_IN].astype(F32)
        o, _, _, _ = _gate_out(mx["yc"], mx["yd"], yx, gate, wout_ref)
        n, _ = _post_norm(o, pg[...])
        o_ref[...] = o
        err = x_ref[...] + n * pg[...] - tgt_ref[...]
        dx_ref[...] = err * (1.0 / D)

        @pl.when(i == 0)
        def _():
            loss_ref[...] = jnp.zeros_like(loss_ref)

        loss_ref[...] += 0.5 * jnp.sum(jnp.sum(err * err, axis=-1, keepdims=True) * (1.0 / D), axis=0, keepdims=True)

    tile, halo = _tile_specs(s, ODD_IN)
    row = pl.BlockSpec((TS, D), lambda i: (i, 0))
    vec = _const((1, BW))
    return pl.pallas_call(
        body, grid=(s // TS,), name="odd_fwd",
        out_shape=(jax.ShapeDtypeStruct((s, D), F32), jax.ShapeDtypeStruct((s, D), F32),
                   jax.ShapeDtypeStruct((8, 128), F32)),
        in_specs=[row, tile, halo, _const((N_MEM, D)), _const((BW, BW)), vec, _const((CONF, BW)), vec, vec, vec,
                  _const((BW, BW)), vec, _resident((MIX, D)), _const((1, D)), row],
        out_specs=(row, row, _const((8, 128))),
        scratch_shapes=[pltpu.VMEM((HALO + TS, BW), F32), pltpu.VMEM((HALO + TS, BW), F32)],
        compiler_params=_cp(("arbitrary",)),
    )(x1, p, p, kv, wbd, cscale, dww, dwb, ln_g, ln_b, pww, pwb, wout, post_g, target)


def _acc_init(i, refs):
    @pl.when(i == 0)
    def _():
        for r in refs:
            r[...] = jnp.zeros_like(r)


def _post_norm_bwd(dx, o, pg, dpg_ref):
    n, r = _post_norm(o, pg)
    dpg_ref[...] += jnp.sum(dx * n, axis=0, keepdims=True)
    dn = dx * pg
    return (r * (dn - n * jnp.mean(dn * n, axis=-1, keepdims=True))).astype(BF16)


def _gate_bwd(do, wout_ref, ys_f32, gate, y_ref):
    dy = _dot_nt(do, wout_ref[...])
    sgt = _sigmoid(gate)
    sgate = gate * sgt
    dsilu = sgt * (1.0 + gate * (1.0 - sgt))
    offs = (0, BW, 2 * BW, MIX)
    dys, dgs = [], []
    for j, yv in enumerate(ys_f32):
        a, b = offs[j], offs[j + 1]
        y_ref[:, a:b] = (yv * sgate[:, a:b]).astype(BF16)
        dys.append(dy[:, a:b] * sgate[:, a:b])
        dgs.append(dy[:, a:b] * yv * dsilu[:, a:b])
    return dys, jnp.concatenate(dgs, axis=-1)


def _even_bwd1(dx, o, p, kv, ln_g, ln_b, wcat, bsg, hsel, bconv, wout, post_g):
    s = dx.shape[0]

    def body(dx_ref, o_ref, p_ref, ph_ref, kv_ref, lng, lnb, wcat_ref, bsg_ref, hsel_ref, bconv_ref, wout_ref, pg,
             dpa_ref, dpc_ref, tmp_ref, do_ref, y_ref, dpg_ref, dlng_ref, dlnb_ref, dwcat_ref, dbs_ref, dbconv_ref,
             dkv_ref, wbuf):
        i = pl.program_id(0)
        _acc_init(i, (dpg_ref, dlng_ref, dlnb_ref, dwcat_ref, dbs_ref, dbconv_ref, dkv_ref))
        mx = _even_mix(i, p_ref, ph_ref, lng[...], lnb[...], wcat_ref, bsg_ref, bconv_ref, wbuf)
        q = p_ref[:, 5 * BW:5 * BW + XA]
        yx, probs = _xattn_fwd(q, kv_ref)
        gate = p_ref[:, 5 * BW + XA:EVEN_IN].astype(F32)
        do = _post_norm_bwd(dx_ref[...], o_ref[...], pg[...], dpg_ref)
        do_ref[...] = do
        (dya, dyb, dyx), dgate = _gate_bwd(do, wout_ref, (mx["ya"], mx["yb"], yx), gate, y_ref)
        dpa_ref[:, 0:BW] = (dya * mx["sg"]).astype(BF16)
        dsg = (dya * mx["u"]).astype(BF16)
        dvns = []
        for n in range(TS // CH):
            dsg_c = dsg[n * CH:(n + 1) * CH]
            dvst = _dot_tn(wcat_ref[...], dsg_c)
            dvn_c = jnp.where(mx["masks"][0], dvst[0:CH], 0.0)
            for h in range(1, 4):
                dvn_c = dvn_c + jnp.where(mx["masks"][h], dvst[h * CH:(h + 1) * CH], 0.0)
            dvns.append(dvn_c)
            dwcat_ref[...] += _dot_nt(dsg_c, mx["vsts"][n])
            dbs_ref[...] += _dot(dsg_c, hsel_ref[...])
        dvn = jnp.concatenate(dvns, axis=0)
        dlng_ref[...] += jnp.sum(dvn * mx["vhat"], axis=0, keepdims=True)
        dlnb_ref[...] += jnp.sum(dvn, axis=0, keepdims=True)
        dpa_ref[:, BW:2 * BW] = _layer_norm_bwd(dvn, mx["vhat"], mx["rstd"], lng[...]).astype(BF16)
        dpa_ref[:, 2 * BW:3 * BW] = (dyb * mx["conv"]).astype(BF16)
        dconv = dyb * mx["bg"]
        tmp_ref[...] = dconv.astype(BF16)
        for k in range(3):
            dbconv_ref[k:k + 1, :] += jnp.sum(dconv * wbuf[pl.ds(HALO - 2 + k, TS), :], axis=0, keepdims=True)
        dpc_ref[:, 0:XA] = _xattn_bwd(dyx, q, probs, kv_ref, dkv_ref).astype(BF16)
        dpc_ref[:, XA:XA + MIX] = dgate.astype(BF16)

    tile, halo = _tile_specs(s, EVEN_IN)
    row = pl.BlockSpec((TS, D), lambda i: (i, 0))
    vec = _const((1, BW))

    def out(n):
        return pl.BlockSpec((TS, n), lambda i: (i, 0))

    return pl.pallas_call(
        body, grid=(s // TS,), name="even_bwd1",
        out_shape=(jax.ShapeDtypeStruct((s, 3 * BW), BF16), jax.ShapeDtypeStruct((s, XA + MIX), BF16),
                   jax.ShapeDtypeStruct((s, BW), BF16), jax.ShapeDtypeStruct((s, D), BF16),
                   jax.ShapeDtypeStruct((s, MIX), BF16),
                   jax.ShapeDtypeStruct((1, D), F32), jax.ShapeDtypeStruct((1, BW), F32),
                   jax.ShapeDtypeStruct((1, BW), F32), jax.ShapeDtypeStruct((CH, 4 * CH), F32),
                   jax.ShapeDtypeStruct((CH, 128), F32), jax.ShapeDtypeStruct((8, BW), F32),
                   jax.ShapeDtypeStruct((N_MEM, D), F32)),
        in_specs=[row, row, tile, halo, _const((N_MEM, D)), vec, vec, _const((CH, 4 * CH)), _const((CH, BW)),
                  _const((BW, 128)), _const((3, BW)), _resident((MIX, D)), _const((1, D))],
        out_specs=(out(3 * BW), out(XA + MIX), out(BW), out(D), out(MIX),
                   _const((1, D)), vec, vec, _const((CH, 4 * CH)), _const((CH, 128)), _const((8, BW)),
                   _const((N_MEM, D))),
        scratch_shapes=[pltpu.VMEM((HALO + TS, BW), F32)],
        compiler_params=_cp(("arbitrary",)),
    )(dx, o, p, p, kv, ln_g, ln_b, wcat, bsg, hsel, bconv, wout, post_g)


def _odd_bwd1(dx, o, p, kv, wbd, cscale, dww, dwb, ln_g, ln_b, pww, pwb, wout, post_g):
    s = dx.shape[0]

    def body(dx_ref, o_ref, p_ref, ph_ref, kv_ref, wbd_ref, cs, dww_ref, dwb_ref, lng, lnb, pww_ref, pwb_ref,
             wout_ref, pg,
             dpc_ref, tmpc_ref, tmpd_ref, do_ref, y_ref, dpg_ref, dcs_ref, dwbd_ref, ddww_ref, ddwb_ref, dlng_ref,
             dlnb_ref, dpww_ref, dpwb_ref, dkv_ref, zbuf, gbuf):
        i = pl.program_id(0)
        _acc_init(i, (dpg_ref, dcs_ref, dwbd_ref, ddww_ref, ddwb_ref, dlng_ref, dlnb_ref, dpww_ref, dpwb_ref,
                      dkv_ref))
        mx = _odd_mix(i, p_ref, ph_ref, wbd_ref, cs[...], dww_ref, dwb_ref[...], lng[...], lnb[...], pww_ref,
                      pwb_ref[...], zbuf, gbuf)
        q = p_ref[:, 3 * BW:3 * BW + XA]
        yx, probs = _xattn_fwd(q, kv_ref)
        gate = p_ref[:, 3 * BW + XA:ODD_IN].astype(F32)
        do = _post_norm_bwd(dx_ref[...], o_ref[...], pg[...], dpg_ref)
        do_ref[...] = do
        (dyc, dyd, dyx), dgate = _gate_bwd(do, wout_ref, (mx["yc"], mx["yd"], yx), gate, y_ref)
        dcs_ref[...] += jnp.sum(dyc * mx["pre"], axis=0, keepdims=True)
        dpre = (dyc * cs[...]).astype(BF16)
        dwbd_ref[...] += _dot_tn(mx["pooled_bf"], dpre)
        dpooled = _dot_nt(dpre, wbd_ref[...])
        tmpc_ref[...] = _pool_select([dpooled * c_ for c_ in mx["inv"]]).astype(BF16)
        dyd_bf = dyd.astype(BF16)
        dpwb_ref[...] += jnp.sum(dyd, axis=0, keepdims=True)
        dpww_ref[...] += _dot_tn(mx["zs"], dyd_bf)
        dzs = _dot_nt(dyd_bf, pww_ref[...])
        zl, szl = mx["zl"], mx["szl"]
        dzl = dzs * (szl * (1.0 + zl * (1.0 - szl)))
        dlng_ref[...] += jnp.sum(dzl * mx["zhat"], axis=0, keepdims=True)
        dlnb_ref[...] += jnp.sum(dzl, axis=0, keepdims=True)
        dcv = _layer_norm_bwd(dzl, mx["zhat"], mx["rstd"], lng[...])
        tmpd_ref[...] = dcv.astype(BF16)
        ddwb_ref[...] += jnp.sum(dcv, axis=0, keepdims=True)
        for k in range(CONF):
            ddww_ref[k:k + 1, :] += jnp.sum(dcv * gbuf[pl.ds(HALO - (CONF - 1) + k, TS), :], axis=0, keepdims=True)
        dpc_ref[:, 0:XA] = _xattn_bwd(dyx, q, probs, kv_ref, dkv_ref).astype(BF16)
        dpc_ref[:, XA:XA + MIX] = dgate.astype(BF16)

    tile, halo = _tile_specs(s, ODD_IN)
    row = pl.BlockSpec((TS, D), lambda i: (i, 0))
    vec = _const((1, BW))

    def out(n):
        return pl.BlockSpec((TS, n), lambda i: (i, 0))

    return pl.pallas_call(
        body, grid=(s // TS,), name="odd_bwd1",
        out_shape=(jax.ShapeDtypeStruct((s, XA + MIX), BF16), jax.ShapeDtypeStruct((s, BW), BF16),
                   jax.ShapeDtypeStruct((s, BW), BF16), jax.ShapeDtypeStruct((s, D), BF16),
                   jax.ShapeDtypeStruct((s, MIX), BF16),
                   jax.ShapeDtypeStruct((1, D), F32), jax.ShapeDtypeStruct((1, BW), F32),
                   jax.ShapeDtypeStruct((BW, BW), F32), jax.ShapeDtypeStruct((32, BW), F32),
                   jax.ShapeDtypeStruct((1, BW), F32), jax.ShapeDtypeStruct((1, BW), F32),
                   jax.ShapeDtypeStruct((1, BW), F32), jax.ShapeDtypeStruct((BW, BW), F32),
                   jax.ShapeDtypeStruct((1, BW), F32), jax.ShapeDtypeStruct((N_MEM, D), F32)),
        in_specs=[row, row, tile, halo, _const((N_MEM, D)), _const((BW, BW)), vec, _const((CONF, BW)), vec, vec, vec,
                  _const((BW, BW)), vec, _resident((MIX, D)), _const((1, D))],
        out_specs=(out(XA + MIX), out(BW), out(BW), out(D), out(MIX),
                   _const((1, D)), vec, _const((BW, BW)), _const((32, BW)), vec, vec, vec, _const((BW, BW)), vec,
                   _const((N_MEM, D))),
        scratch_shapes=[pltpu.VMEM((HALO + TS, BW), F32), pltpu.VMEM((HALO + TS, BW), F32)],
        compiler_params=_cp(("arbitrary",)),
    )(dx, o, p, p, kv, wbd, cscale, dww, dwb, ln_g, ln_b, pww, pwb, wout, post_g)


def _halo_next(nblk_per_tile, nblk):
    return lambda i: (jnp.minimum((i + 1) * nblk_per_tile, nblk - 1), 0)


def _pre_norm_bwd(dh, x, pre_g, dres, dpre_ref):
    r = lax.rsqrt(jnp.mean(x * x, axis=-1, keepdims=True) + EPS)
    xh = x * r
    dpre_ref[...] += jnp.sum(dh * xh, axis=0, keepdims=True)
    dxh = dh * pre_g
    return dres + r * (dxh - xh * jnp.mean(dxh * xh, axis=-1, keepdims=True))


def _even_bwd2(dpa, dpc, tmp, p, bconv, w_t, x, pre_g, dres):
    s = x.shape[0]
    nt = s // TS

    def body(dpa_ref, dpc_ref, tmp_ref, tmph_ref, cg_ref, xin_ref, bconv_ref, w_ref, x_ref, pg, dres_ref,
             dpb_ref, dx_ref, dpre_ref, dbuf):
        i = pl.program_id(0)
        _acc_init(i, (dpre_ref,))
        dbuf[0:TS, :] = tmp_ref[...].astype(F32)
        dbuf[TS:TS + HALO, :] = jnp.where(i < nt - 1, tmph_ref[...].astype(F32), 0.0)
        dw = (bconv_ref[2:3, :] * dbuf[pl.ds(0, TS), :] + bconv_ref[1:2, :] * dbuf[pl.ds(1, TS), :]
              + bconv_ref[0:1, :] * dbuf[pl.ds(2, TS), :])
        dcg = (dw * xin_ref[...].astype(F32)).astype(BF16)
        dxin = (dw * cg_ref[...].astype(F32)).astype(BF16)
        dpb_ref[:, 0:BW] = dcg
        dpb_ref[:, BW:2 * BW] = dxin
        dh = (_dot(dpa_ref[...], w_ref[0:3 * BW, :]) + _dot(dcg, w_ref[3 * BW:4 * BW, :])
              + _dot(dxin, w_ref[4 * BW:5 * BW, :]) + _dot(dpc_ref[...], w_ref[5 * BW:EVEN_IN, :]))
        dx_ref[...] = _pre_norm_bwd(dh, x_ref[...], pg[...], dres_ref[...], dpre_ref)

    row = pl.BlockSpec((TS, D), lambda i: (i, 0))

    def tile(n, j=0):
        return pl.BlockSpec((TS, n), lambda i: (i, j))

    return pl.pallas_call(
        body, grid=(nt,), name="even_bwd2",
        out_shape=(jax.ShapeDtypeStruct((s, 2 * BW), BF16), jax.ShapeDtypeStruct((s, D), F32),
                   jax.ShapeDtypeStruct((1, D), F32)),
        in_specs=[tile(3 * BW), tile(XA + MIX), tile(BW), pl.BlockSpec((HALO, BW), _halo_next(TS // HALO, s // HALO)),
                  tile(BW, 3), tile(BW, 4), _const((3, BW)), _resident((EVEN_IN, D)), row, _const((1, D)), row],
        out_specs=(tile(2 * BW), row, _const((1, D))),
        scratch_shapes=[pltpu.VMEM((TS + HALO, BW), F32)],
        compiler_params=_cp(("arbitrary",)),
    )(dpa, dpc, tmp, tmp, p, p, bconv, w_t, x, pre_g, dres)


def _odd_bwd2(dpc, tmpc, tmpd, p, dww, w_t, x, pre_g, dres):
    s = x.shape[0]
    nt = s // TS

    def body(dpc_ref, tc_ref, tch_ref, td_ref, tdh_ref, ga_ref, gb_ref, dww_ref, w_ref, x_ref, pg, dres_ref,
             dpb_ref, dx_ref, dpre_ref, cbuf, dbuf):
        i = pl.program_id(0)
        _acc_init(i, (dpre_ref,))
        last = i < nt - 1
        e = tc_ref[...].astype(F32)
        cbuf[0:TS, :] = e
        cbuf[TS:TS + HALO, :] = jnp.where(last, tch_ref[...].astype(F32), 0.0)
        dbuf[0:TS, :] = td_ref[...].astype(F32)
        dbuf[TS:TS + HALO, :] = jnp.where(last, tdh_ref[...].astype(F32), 0.0)
        acc = e
        sums = []
        k = 1
        for w in POOL_WINDOWS:
            while k < w:
                acc = acc + cbuf[pl.ds(k, TS), :]
                k += 1
            sums.append(acc)
        rows = _row_ids(i, TS) + 1
        cnt = _pool_select([jnp.minimum(rows, w).astype(F32) for w in POOL_WINDOWS])
        dzc = (_pool_select(sums) - e * cnt).astype(BF16)
        dz = dww_ref[CONF - 1:CONF, :] * dbuf[pl.ds(0, TS), :]
        for sft in range(1, CONF):
            dz = dz + dww_ref[CONF - 1 - sft:CONF - sft, :] * dbuf[pl.ds(sft, TS), :]
        ga = ga_ref[...].astype(F32)
        sgb = _sigmoid(gb_ref[...].astype(F32))
        dga = (dz * sgb).astype(BF16)
        dgb = (dz * ga * sgb * (1.0 - sgb)).astype(BF16)
        dpb_ref[:, 0:BW] = dzc
        dpb_ref[:, BW:2 * BW] = dga
        dpb_ref[:, 2 * BW:3 * BW] = dgb
        dh = (_dot(dzc, w_ref[0:BW, :]) + _dot(dga, w_ref[BW:2 * BW, :]) + _dot(dgb, w_ref[2 * BW:3 * BW, :])
              + _dot(dpc_ref[...], w_ref[3 * BW:ODD_IN, :]))
        dx_ref[...] = _pre_norm_bwd(dh, x_ref[...], pg[...], dres_ref[...], dpre_ref)

    row = pl.BlockSpec((TS, D), lambda i: (i, 0))

    def tile(n, j=0):
        return pl.BlockSpec((TS, n), lambda i: (i, j))

    nxt = pl.BlockSpec((HALO, BW), _halo_next(TS // HALO, s // HALO))
    return pl.pallas_call(
        body, grid=(nt,), name="odd_bwd2",
        out_shape=(jax.ShapeDtypeStruct((s, 3 * BW), BF16), jax.ShapeDtypeStruct((s, D), F32),
                   jax.ShapeDtypeStruct((1, D), F32)),
        in_specs=[tile(XA + MIX), tile(BW), nxt, tile(BW), nxt, tile(BW, 1), tile(BW, 2), _const((CONF, BW)),
                  _resident((ODD_IN, D)), row, _const((1, D)), row],
        out_specs=(tile(3 * BW), row, _const((1, D))),
        scratch_shapes=[pltpu.VMEM((TS + HALO, BW), F32), pltpu.VMEM((TS + HALO, BW), F32)],
        compiler_params=_cp(("arbitrary",)),
    )(dpc, tmpc, tmpc, tmpd, tmpd, p, p, dww, w_t, x, pre_g, dres)


def _grad_tn(a, b, tm, out=None, rows=None, row0=0, name="grad_tn"):
    s, m = a.shape
    n = b.shape[1]
    ts = min(2048, s)
    rows = m if rows is None else rows
    blk0 = row0 // tm
    assert m % tm == 0 and row0 % tm == 0 and s % ts == 0
    ns = s // ts

    def body(*refs):
        a_ref, b_ref = refs[0], refs[1]
        o_ref, acc = refs[-2], refs[-1]
        k = pl.program_id(1)

        @pl.when(k == 0)
        def _():
            acc[...] = jnp.zeros_like(acc)

        acc[...] += _dot_tn(a_ref[...], b_ref[...])

        @pl.when(k == ns - 1)
        def _():
            o_ref[...] = acc[...].astype(BF16)

    in_specs = [pl.BlockSpec((ts, tm), lambda i, k: (k, i)), pl.BlockSpec((ts, n), lambda i, k: (k, 0))]
    args = [a, b]
    aliases = {}
    if out is not None:
        in_specs.append(pl.BlockSpec(memory_space=pltpu.HBM))
        args.append(out)
        aliases = {2: 0}
    return pl.pallas_call(
        body, grid=(m // tm, ns), name=name,
        out_shape=jax.ShapeDtypeStruct((rows, n), BF16),
        in_specs=in_specs, out_specs=pl.BlockSpec((tm, n), lambda i, k: (blk0 + i, 0)),
        scratch_shapes=[pltpu.VMEM((tm, n), F32)], input_output_aliases=aliases,
        compiler_params=_cp(("arbitrary", "arbitrary")),
    )(*args)


def _place():
    x, y, c = lax.axis_index("x"), lax.axis_index("y"), lax.axis_index("c")
    chips = [(1 - x, y), (x, 1 - y), (1 - x, 1 - y)]
    return x, y, c, chips


def _hbm_specs(n):
    return [pl.BlockSpec(memory_space=pltpu.HBM)] * n


def _gather_weights(shards):
    n = len(shards)

    def body(*refs):
        ins, outs = refs[:n], refs[n:2 * n]
        send_sems, recv_sems, local_sems = refs[2 * n:]
        x, y, c, chips = _place()
        me_k = 2 * x + y
        sibling = (x, y, 1 - c)

        def rows(a, k, half):
            r = shards[a].shape[0]
            return outs[a].at[pl.ds(k * r + half * (r // 2), r // 2)]

        def copy(a, j, src, dst, to):
            return pltpu.make_async_remote_copy(src_ref=src, dst_ref=dst, send_sem=send_sems.at[a * 6 + j],
                                                recv_sem=recv_sems.at[a * 6 + j], device_id=to, device_id_type=MESH)

        local = []
        for a in range(n):
            r = shards[a].shape[0]
            cp = pltpu.make_async_copy(ins[a], outs[a].at[pl.ds(me_k * r, r)], local_sems.at[a])
            cp.start()
            local.append(cp)
        started = []
        for j, (px, py) in enumerate(chips):
            for a in range(n):
                r = shards[a].shape[0]
                cp = copy(a, j, ins[a].at[pl.ds(c * (r // 2), r // 2)], rows(a, me_k, c), (px, py, c))
                cp.start()
                started.append(cp)
        for j, (px, py) in enumerate(chips):
            k = 2 * px + py
            for a in range(n):
                copy(a, j, rows(a, k, c), rows(a, k, c), (px, py, c)).wait_recv()
                cp = copy(a, 3 + j, rows(a, k, c), rows(a, k, c), sibling)
                cp.start()
                started.append(cp)
        for j, (px, py) in enumerate(chips):
            k = 2 * px + py
            for a in range(n):
                copy(a, 3 + j, rows(a, k, 1 - c), rows(a, k, 1 - c), sibling).wait_recv()
        for cp in started:
            cp.wait_send()
        for cp in local:
            cp.wait()

    return pl.pallas_call(
        body, name="gather_weights",
        out_shape=tuple(jax.ShapeDtypeStruct((N_CHIPS * a.shape[0],) + a.shape[1:], a.dtype) for a in shards),
        in_specs=_hbm_specs(n), out_specs=tuple(_hbm_specs(n)),
        scratch_shapes=[pltpu.SemaphoreType.DMA((6 * n,)), pltpu.SemaphoreType.DMA((6 * n,)),
                        pltpu.SemaphoreType.DMA((n,))],
    )(*shards)


def _swap_halves(grads, small):
    n = len(grads)

    def body(*refs):
        ins, outs = refs[:n + 1], refs[n + 1:2 * n + 2]
        send_sems, recv_sems = refs[2 * n + 2:]
        x, y, c, _ = _place()
        sibling = (x, y, 1 - c)
        cps = []
        for a in range(n + 1):
            src = ins[a].at[:, 1 - c] if a < n else ins[a]
            cp = pltpu.make_async_remote_copy(src_ref=src, dst_ref=outs[a], send_sem=send_sems.at[a],
                                              recv_sem=recv_sems.at[a], device_id=sibling, device_id_type=MESH)
            cp.start()
            cps.append(cp)
        for cp in cps:
            cp.wait_recv()
        for cp in cps:
            cp.wait_send()

    outs = tuple(jax.ShapeDtypeStruct((g.shape[0],) + g.shape[2:], g.dtype) for g in grads)
    outs += (jax.ShapeDtypeStruct(small.shape, small.dtype),)
    return pl.pallas_call(
        body, name="swap_halves", out_shape=outs, in_specs=_hbm_specs(n + 1), out_specs=tuple(_hbm_specs(n + 1)),
        scratch_shapes=[pltpu.SemaphoreType.DMA((n + 1,)), pltpu.SemaphoreType.DMA((n + 1,))],
    )(*grads, small)


def _pair_sum(g, recv, cidx, name):
    _, _, h, cc = g.shape
    th = h
    for cand in (512, 400, 304, 256, 128, 96):
        if h % cand == 0:
            th = cand
            break

    def body(c_ref, g_ref, r_ref, o_ref):
        o_ref[...] = (g_ref[...].astype(F32) + r_ref[...].astype(F32)).astype(o_ref.dtype)

    return pl.pallas_call(
        body, name=name, out_shape=jax.ShapeDtypeStruct(recv.shape, recv.dtype),
        grid_spec=pltpu.PrefetchScalarGridSpec(
            num_scalar_prefetch=1, grid=(N_CHIPS, h // th),
            in_specs=[pl.BlockSpec((None, None, th, cc), lambda k, r, c_ref: (k, c_ref[0], r, 0)),
                      pl.BlockSpec((None, th, cc), lambda k, r, c_ref: (k, r, 0))],
            out_specs=pl.BlockSpec((None, th, cc), lambda k, r, c_ref: (k, r, 0))),
        compiler_params=_cp(("arbitrary", "arbitrary")),
    )(cidx, g, recv)


def _small_sum(a, b):
    def body(a_ref, b_ref, o_ref):
        o_ref[...] = a_ref[...] + b_ref[...]

    return pl.pallas_call(body, name="small_pair_sum", out_shape=jax.ShapeDtypeStruct(a.shape, a.dtype),
                          compiler_params=_cp())(a, b)


def _exchange_chips(sums, small):
    n = len(sums)
    hs = small.shape[0] // 2

    def body(*refs):
        ins, outs = refs[:n + 1], refs[n + 1:2 * n + 2]
        send_sems, recv_sems, local_sems = refs[2 * n + 2:]
        x, y, c, chips = _place()
        me_k = 2 * x + y

        def src(a, k):
            return ins[a].at[k] if a < n else ins[a].at[pl.ds(c * hs, hs)]

        local = []
        for a in range(n + 1):
            cp = pltpu.make_async_copy(src(a, me_k), outs[a].at[me_k], local_sems.at[a])
            cp.start()
            local.append(cp)
        cps = []
        for j, (px, py) in enumerate(chips):
            for a in range(n + 1):
                cp = pltpu.make_async_remote_copy(
                    src_ref=src(a, 2 * px + py), dst_ref=outs[a].at[me_k], send_sem=send_sems.at[a * 3 + j],
                    recv_sem=recv_sems.at[a * 3 + j], device_id=(px, py, c), device_id_type=MESH)
                cp.start()
                cps.append(cp)
        for cp in cps:
            cp.wait_recv()
        for cp in cps:
            cp.wait_send()
        for cp in local:
            cp.wait()

    outs = tuple(jax.ShapeDtypeStruct(g.shape, g.dtype) for g in sums)
    outs += (jax.ShapeDtypeStruct((N_CHIPS, hs, small.shape[1]), small.dtype),)
    return pl.pallas_call(
        body, name="exchange_chips", out_shape=outs, in_specs=_hbm_specs(n + 1), out_specs=tuple(_hbm_specs(n + 1)),
        scratch_shapes=[pltpu.SemaphoreType.DMA((3 * (n + 1),)), pltpu.SemaphoreType.DMA((3 * (n + 1),)),
                        pltpu.SemaphoreType.DMA((n + 1,))],
    )(*sums, small)


def _chip_sum(parts, name):
    _, h, cc = parts.shape
    th = h
    for cand in (512, 400, 304, 256, 128, 96):
        if h % cand == 0:
            th = cand
            break

    def body(p_ref, o_ref):
        acc = p_ref[0].astype(F32) + p_ref[1].astype(F32)
        acc = acc + p_ref[2].astype(F32)
        o_ref[...] = acc + p_ref[3].astype(F32)

    return pl.pallas_call(
        body, name=name, grid=(h // th,), out_shape=jax.ShapeDtypeStruct((h, cc), F32),
        in_specs=[pl.BlockSpec((N_CHIPS, th, cc), lambda r: (0, r, 0))],
        out_specs=pl.BlockSpec((th, cc), lambda r: (r, 0)), compiler_params=_cp(("arbitrary",)),
    )(parts)


def _share_halves(halves):
    n = len(halves)

    def body(*refs):
        ins, outs = refs[:n], refs[n:2 * n]
        send_sems, recv_sems, local_sems = refs[2 * n:]
        x, y, c, _ = _place()
        sibling = (x, y, 1 - c)
        local, cps = [], []
        for a in range(n):
            cp = pltpu.make_async_copy(ins[a], outs[a].at[c], local_sems.at[a])
            cp.start()
            local.append(cp)
            cp = pltpu.make_async_remote_copy(src_ref=ins[a], dst_ref=outs[a].at[c], send_sem=send_sems.at[a],
                                              recv_sem=recv_sems.at[a], device_id=sibling, device_id_type=MESH)
            cp.start()
            cps.append(cp)
        for cp in cps:
            cp.wait_recv()
        for cp in cps:
            cp.wait_send()
        for cp in local:
            cp.wait()

    return pl.pallas_call(
        body, name="share_halves", out_shape=tuple(jax.ShapeDtypeStruct((2,) + g.shape, g.dtype) for g in halves),
        in_specs=_hbm_specs(n), out_specs=tuple(_hbm_specs(n)),
        scratch_shapes=[pltpu.SemaphoreType.DMA((n,)), pltpu.SemaphoreType.DMA((n,)), pltpu.SemaphoreType.DMA((n,))],
    )(*halves)


def _adamw_math(w, g, m, v):
    m = ADAM_B1 * m + (1.0 - ADAM_B1) * g
    v = ADAM_B2 * v + (1.0 - ADAM_B2) * (g * g)
    m_hat = m / (1.0 - ADAM_B1 ** ADAM_STEP)
    v_hat = v / (1.0 - ADAM_B2 ** ADAM_STEP)
    delta = -ADAM_LR * (m_hat / (jnp.sqrt(v_hat) + ADAM_EPS) + ADAM_WD * w)
    return delta, m, v


def _adamw_big(w, g, m, v, name):
    r, cc = w.shape
    tr = r
    for cand in (256, 192, 128):
        if r % cand == 0:
            tr = cand
            break

    def body(w_ref, g_ref, m_ref, v_ref, d_ref, mo_ref, vo_ref):
        d, mm, vv = _adamw_math(w_ref[...], g_ref[...], m_ref[...], v_ref[...])
        d_ref[...] = d
        mo_ref[...] = mm
        vo_ref[...] = vv

    blk = pl.BlockSpec((tr, cc), lambda i: (i, 0))
    sd = jax.ShapeDtypeStruct((r, cc), F32)
    return pl.pallas_call(body, grid=(r // tr,), name=name, out_shape=(sd, sd, sd), in_specs=[blk] * 4,
                          out_specs=(blk, blk, blk), compiler_params=_cp(("arbitrary",)))(w, g, m, v)


def _adamw_small(ws, gs, ms, vs):
    n = len(ws)

    def body(*refs):
        for a in range(n):
            w_ref, g_ref, m_ref, v_ref = refs[4 * a:4 * a + 4]
            d_ref, mo_ref, vo_ref = refs[4 * n + 3 * a:4 * n + 3 * a + 3]
            d, mm, vv = _adamw_math(w_ref[...], g_ref[...], m_ref[...], v_ref[...])
            d_ref[...] = d
            mo_ref[...] = mm
            vo_ref[...] = vv

    args, outs = [], []
    for a in range(n):
        args += [ws[a], gs[a], ms[a], vs[a]]
        outs += [jax.ShapeDtypeStruct(ws[a].shape, F32)] * 3
    res = pl.pallas_call(body, name="adamw_small", out_shape=tuple(outs), compiler_params=_cp())(*args)
    return [res[3 * a:3 * a + 3] for a in range(n)]


def _flat_pack(arrs, rows):
    flat = jnp.concatenate([a.reshape(-1) for a in arrs])
    return jnp.pad(flat, (0, rows * D - flat.shape[0])).reshape(rows, D)


def _flat_unpack(flat, shapes):
    out, off = [], 0
    for shp in shapes:
        size = 1
        for d_ in shp:
            size *= d_
        out.append(flat[off:off + size].reshape(shp))
        off += size
    return out


SMALL_EVEN = ("even_pre_g", "even_a_ln_g", "even_a_ln_b", "even_a_ws", "even_a_bs", "even_b_conv", "even_mem_g",
              "even_post_g")
SMALL_ODD = ("odd_pre_g", "odd_c_wgrp", "odd_c_scale", "odd_d_dw_w", "odd_d_dw_b", "odd_d_ln_g", "odd_d_ln_b",
             "odd_d_pw_b", "odd_mem_g", "odd_post_g")
BIG = ("even_w_in", "even_w_kv", "even_w_out", "odd_w_in", "odd_d_pw_w", "odd_w_kv", "odd_w_out")
WEIGHTS = ("even_pre_g", "even_w_in", "even_a_ln_g", "even_a_ln_b", "even_a_ws", "even_a_bs", "even_b_conv",
           "even_mem_g", "even_w_kv", "even_w_out", "even_post_g", "odd_pre_g", "odd_w_in", "odd_c_wgrp",
           "odd_c_scale", "odd_d_dw_w", "odd_d_dw_b", "odd_d_ln_g", "odd_d_ln_b", "odd_d_pw_w", "odd_d_pw_b",
           "odd_mem_g", "odd_w_kv", "odd_w_out", "odd_post_g")
PACKED = (("even_b_conv", (3, 192)), ("odd_pre_g", (1, 256)), ("odd_c_scale", (1, 192)), ("odd_d_dw_w", (31, 192)),
          ("odd_d_dw_b", (1, 192)), ("odd_d_ln_g", (1, 192)), ("odd_d_ln_b", (1, 192)), ("odd_d_pw_b", (1, 192)),
          ("odd_mem_g", (1, 256)), ("odd_post_g", (1, 256)))
PACK_ROWS = 16
SMALL_ROWS = 256


def _local_step(x, mem, target, wt):
    tril = jnp.tril(jnp.ones((CH, CH), dtype=bool))
    wcat = jnp.where(tril[None], wt["even_a_ws"], 0.0).transpose(1, 0, 2).reshape(CH, 4 * CH).astype(BF16)
    bsg = jnp.repeat(wt["even_a_bs"].T, BW // 4, axis=1)
    hsel = (jnp.arange(BW)[:, None] // (BW // 4) == jnp.arange(128)[None, :]).astype(BF16)
    wg = wt["odd_c_wgrp"]
    g4 = BW // 4
    wbd = jnp.zeros((BW, BW), F32)
    for g in range(4):
        wbd = lax.dynamic_update_slice(wbd, wg[g], (g * g4, g * g4))
    wbd = wbd.astype(BF16)

    kv_e = _kv_fwd(mem, wt["even_mem_g"], wt["even_w_kv"], "even_kv")
    kv_o = _kv_fwd(mem, wt["odd_mem_g"], wt["odd_w_kv"], "odd_kv")
    p_e, h_e = _in_fwd(x, wt["even_pre_g"], wt["even_w_in_t"], "even_in")
    x1, o_e = _even_fwd(x, p_e, kv_e, wt["even_a_ln_g"], wt["even_a_ln_b"], wcat, bsg, wt["even_b_conv"],
                        wt["even_w_out"], wt["even_post_g"])
    p_o, h_o = _in_fwd(x1, wt["odd_pre_g"], wt["odd_w_in_t"], "odd_in")
    dx2, o_o, loss = _odd_fwd(x1, p_o, kv_o, wbd, wt["odd_c_scale"], wt["odd_d_dw_w"], wt["odd_d_dw_b"],
                              wt["odd_d_ln_g"], wt["odd_d_ln_b"], wt["odd_d_pw_w"], wt["odd_d_pw_b"],
                              wt["odd_w_out"], wt["odd_post_g"], target)
    (dpc_o, tmpc, tmpd, do_o, y_o, g_post_o, g_cs, g_wbd, g_dww, g_dwb, g_lng_o, g_lnb_o, g_pww, g_pwb,
     dkv_o) = _odd_bwd1(dx2, o_o, p_o, kv_o, wbd, wt["odd_c_scale"], wt["odd_d_dw_w"], wt["odd_d_dw_b"],
                        wt["odd_d_ln_g"], wt["odd_d_ln_b"], wt["odd_d_pw_w"], wt["odd_d_pw_b"], wt["odd_w_out"],
                        wt["odd_post_g"])
    dpb_o, dx1, g_pre_o = _odd_bwd2(dpc_o, tmpc, tmpd, p_o, wt["odd_d_dw_w"], wt["odd_w_in_t"], x1,
                                    wt["odd_pre_g"], dx2)
    g_win_o = _grad_tn(dpb_o, h_o, 768, rows=ODD_IN, name="odd_gw_in_b")
    g_win_o = _grad_tn(dpc_o, h_o, 256, out=g_win_o, rows=ODD_IN, row0=3 * BW, name="odd_gw_in_c")
    g_wout_o = _grad_tn(y_o, do_o, 1024, name="odd_gw_out")
    g_wkv_o, g_memg_o = _kv_bwd(mem, wt["odd_mem_g"], wt["odd_w_kv"], dkv_o, "odd_kv_bwd")
    (dpa_e, dpc_e, tmp_e, do_e, y_e, g_post_e, g_lng_e, g_lnb_e, g_wcat, g_bs, g_bconv,
     dkv_e) = _even_bwd1(dx1, o_e, p_e, kv_e, wt["even_a_ln_g"], wt["even_a_ln_b"], wcat, bsg, hsel,
                         wt["even_b_conv"], wt["even_w_out"], wt["even_post_g"])
    dpb_e, dx0, g_pre_e = _even_bwd2(dpa_e, dpc_e, tmp_e, p_e, wt["even_b_conv"], wt["even_w_in_t"], x,
                                     wt["even_pre_g"], dx1)
    g_win_e = _grad_tn(dpa_e, h_e, 768, rows=EVEN_IN, name="even_gw_in_a")
    g_win_e = _grad_tn(dpb_e, h_e, 768, out=g_win_e, rows=EVEN_IN, row0=3 * BW, name="even_gw_in_b")
    g_win_e = _grad_tn(dpc_e, h_e, 1280, out=g_win_e, rows=EVEN_IN, row0=5 * BW, name="even_gw_in_c")
    g_wout_e = _grad_tn(y_e, do_e, 1024, name="even_gw_out")
    g_wkv_e, g_memg_e = _kv_bwd(mem, wt["even_mem_g"], wt["even_w_kv"], dkv_e, "even_kv_bwd")

    g_aws = jnp.where(tril[None], g_wcat.reshape(CH, 4, CH).transpose(1, 0, 2), 0.0)
    g_wgrp = jnp.stack([lax.dynamic_slice(g_wbd, (g * g4, g * g4), (g4, g4)) for g in range(4)])
    small = {
        "even_pre_g": g_pre_e, "even_a_ln_g": g_lng_e, "even_a_ln_b": g_lnb_e, "even_a_ws": g_aws,
        "even_a_bs": g_bs[:, 0:4].T, "even_b_conv": g_bconv[0:3], "even_mem_g": g_memg_e, "even_post_g": g_post_e,
        "odd_pre_g": g_pre_o, "odd_c_wgrp": g_wgrp, "odd_c_scale": g_cs, "odd_d_dw_w": g_dww[0:CONF],
        "odd_d_dw_b": g_dwb, "odd_d_ln_g": g_lng_o, "odd_d_ln_b": g_lnb_o, "odd_d_pw_b": g_pwb,
        "odd_mem_g": g_memg_o, "odd_post_g": g_post_o,
    }
    big = {"even_w_in": g_win_e, "even_w_kv": g_wkv_e, "even_w_out": g_wout_e, "odd_w_in": g_win_o,
           "odd_d_pw_w": g_pww.astype(BF16), "odd_w_kv": g_wkv_o, "odd_w_out": g_wout_o}
    return loss[0, 0], dx0, big, small


def kernel(x, mem, even_pre_g, even_w_in, even_a_ln_g, even_a_ln_b, even_a_ws, even_a_bs, even_b_conv, even_mem_g, even_w_kv, even_w_out, even_post_g, odd_pre_g, odd_w_in, odd_c_wgrp, odd_c_scale, odd_d_dw_w, odd_d_dw_b, odd_d_ln_g, odd_d_ln_b, odd_d_pw_w, odd_d_pw_b, odd_mem_g, odd_w_kv, odd_w_out, odd_post_g, loss_target, m_even_pre_g, m_even_w_in, m_even_a_ln_g, m_even_a_ln_b, m_even_a_ws, m_even_a_bs, m_even_b_conv, m_even_mem_g, m_even_w_kv, m_even_w_out, m_even_post_g, m_odd_pre_g, m_odd_w_in, m_odd_c_wgrp, m_odd_c_scale, m_odd_d_dw_w, m_odd_d_dw_b, m_odd_d_ln_g, m_odd_d_ln_b, m_odd_d_pw_w, m_odd_d_pw_b, m_odd_mem_g, m_odd_w_kv, m_odd_w_out, m_odd_post_g, v_even_pre_g, v_even_w_in, v_even_a_ln_g, v_even_a_ln_b, v_even_a_ws, v_even_a_bs, v_even_b_conv, v_even_mem_g, v_even_w_kv, v_even_w_out, v_even_post_g, v_odd_pre_g, v_odd_w_in, v_odd_c_wgrp, v_odd_c_scale, v_odd_d_dw_w, v_odd_d_dw_b, v_odd_d_ln_g, v_odd_d_ln_b, v_odd_d_pw_w, v_odd_d_pw_b, v_odd_mem_g, v_odd_w_kv, v_odd_w_out, v_odd_post_g):
    given = dict(locals())
    w = {n: given[n] for n in WEIGHTS}
    mom = {n: given["m_" + n] for n in WEIGHTS}
    var = {n: given["v_" + n] for n in WEIGHTS}

    pack = _flat_pack([w[n][0] for n, _ in PACKED], PACK_ROWS)
    shards = [w["even_w_in"][0].T.astype(BF16), w["odd_w_in"][0].T.astype(BF16), w["even_w_kv"][0].astype(BF16),
              w["odd_w_kv"][0].astype(BF16), w["even_w_out"][0].astype(BF16), w["odd_w_out"][0].astype(BF16),
              w["odd_d_pw_w"][0].astype(BF16), pack]
    win_e, win_o, wkv_e, wkv_o, wout_e, wout_o, pww, packs = _gather_weights(shards)
    packs = packs.reshape(N_CHIPS, PACK_ROWS * D)
    wt = {"even_w_in_t": win_e, "odd_w_in_t": win_o, "even_w_kv": wkv_e, "odd_w_kv": wkv_o, "even_w_out": wout_e,
          "odd_w_out": wout_o, "odd_d_pw_w": pww}
    per_chip = [_flat_unpack(packs[k], [shp for _, shp in PACKED]) for k in range(N_CHIPS)]
    for a, (name, _) in enumerate(PACKED):
        wt[name] = jnp.concatenate([per_chip[k][a] for k in range(N_CHIPS)], axis=-1)
    for name in ("even_pre_g", "even_a_ln_g", "even_a_ln_b", "even_mem_g", "even_post_g"):
        wt[name] = w[name]
    wt["even_a_ws"] = w["even_a_ws"][0]
    wt["even_a_bs"] = w["even_a_bs"][0]
    wt["odd_c_wgrp"] = w["odd_c_wgrp"][0]

    loss, grad_x, big, small = _local_step(x[0], mem[0], loss_target[0], wt)

    cidx = lax.axis_index("c").astype(jnp.int32).reshape(1)
    small_pack = _flat_pack([small[n] for n in SMALL_EVEN + SMALL_ODD], SMALL_ROWS)
    g4 = [big[n].reshape(N_CHIPS, 2, big[n].shape[0] // 8, big[n].shape[1]) for n in BIG]
    recv = _swap_halves(g4, small_pack)
    sums = [_pair_sum(g4[a], recv[a], cidx, "pair_sum_%d" % a) for a in range(len(BIG))]
    small_sum = _small_sum(small_pack, recv[-1])
    parts = _exchange_chips(sums, small_sum)
    halves = [_chip_sum(parts[a], "chip_sum_%d" % a) for a in range(len(BIG) + 1)]
    full = _share_halves(halves)
    gbig = {n: full[a].reshape(full[a].shape[1] * 2, full[a].shape[2]) for a, n in enumerate(BIG)}
    gsmall_flat = full[-1].reshape(-1)

    grads = {}
    x_, y_ = lax.axis_index("x"), lax.axis_index("y")
    chip = 2 * x_ + y_
    full_shapes = [small[n].shape for n in SMALL_EVEN + SMALL_ODD]
    for n, g in zip(SMALL_EVEN + SMALL_ODD, _flat_unpack(gsmall_flat, full_shapes)):
        shard_shape = w[n].shape[1:]
        if g.shape[-1] != shard_shape[-1]:
            g = lax.dynamic_slice_in_dim(g, chip * shard_shape[-1], shard_shape[-1], axis=g.ndim - 1)
        grads[n] = g.reshape(shard_shape)
    grads["even_w_in"] = gbig["even_w_in"].T
    grads["odd_w_in"] = gbig["odd_w_in"].T
    for n in ("even_w_kv", "even_w_out", "odd_d_pw_w", "odd_w_kv", "odd_w_out"):
        grads[n] = gbig[n]

    def two_d(a):
        return a.reshape(-1, a.shape[-1])

    upd = {}
    for n in BIG:
        upd[n] = _adamw_big(w[n][0], grads[n], mom[n][0], var[n][0], "adamw_" + n)
    names = SMALL_EVEN + SMALL_ODD
    res = _adamw_small([two_d(w[n][0]) for n in names], [two_d(grads[n]) for n in names],
                       [two_d(mom[n][0]) for n in names], [two_d(var[n][0]) for n in names])
    for n, r in zip(names, res):
        upd[n] = r

    loss = lax.psum(loss, ("x", "y", "c"))
    outs = [loss, grad_x[None]]
    outs += [grads[n].reshape(w[n].shape) for n in WEIGHTS]
    for j in range(3):
        outs += [upd[n][j].reshape(w[n].shape) for n in WEIGHTS]
    return tuple(outs)
```

```python
import functools

import jax
import jax.numpy as jnp
from jax import lax
from jax.experimental import pallas as pl
from jax.experimental.pallas import tpu as pltpu

F32 = jnp.float32
BF16 = jnp.bfloat16
MESH = pl.DeviceIdType.MESH

D = 1024
N_MEM = 256
MIX = 2048
XA = 512
HD = 128
BW = 768
CH = 128
EPS = 1e-6
SCALE = HD ** -0.5
POOL_WINDOWS = (2, 4, 8, 16)
CONF = 31
EVEN_IN = 6400
ODD_IN = 4864
N_CHIPS = 4

ADAM_LR = 0.001
ADAM_B1 = 0.9
ADAM_B2 = 0.999
ADAM_EPS = 1e-08
ADAM_WD = 0.01
ADAM_STEP = 10

TS = 256
HALO = 32
VMEM_LIMIT = 56 * 1024 * 1024


def _cp(sem=None):
    return pltpu.CompilerParams(dimension_semantics=sem, vmem_limit_bytes=VMEM_LIMIT)


def _dot(a, b):
    return jnp.dot(a, b, preferred_element_type=F32)


def _dot_nt(a, b):
    return lax.dot_general(a, b, (((1,), (1,)), ((), ())), preferred_element_type=F32)


def _dot_tn(a, b):
    return lax.dot_general(a, b, (((0,), (0,)), ((), ())), preferred_element_type=F32)


def _sigmoid(x):
    return 1.0 / (1.0 + jnp.exp(-x))


def _resident(shape):
    return pl.BlockSpec(shape, lambda *_: (0,) * len(shape), pipeline_mode=pl.Buffered(1))


def _const(shape):
    return pl.BlockSpec(shape, lambda *_: (0,) * len(shape))


def _kv_fwd(mem, mem_g, wkv, name):
    def body(mem_ref, g_ref, w_ref, kv_ref):
        m = mem_ref[...]
        r = lax.rsqrt(jnp.mean(m * m, axis=-1, keepdims=True) + EPS)
        mn = (m * r * g_ref[...]).astype(BF16)
        kv_ref[...] = _dot(mn, w_ref[...]).astype(BF16)

    return pl.pallas_call(body, out_shape=jax.ShapeDtypeStruct((N_MEM, D), BF16), name=name,
                          compiler_params=_cp())(mem, mem_g, wkv)


def _kv_bwd(mem, mem_g, wkv, dkv, name):
    def body(mem_ref, g_ref, w_ref, dkv_ref, dw_ref, dg_ref):
        m = mem_ref[...]
        r = lax.rsqrt(jnp.mean(m * m, axis=-1, keepdims=True) + EPS)
        mh = m * r
        mn = (mh * g_ref[...]).astype(BF16)
        dkv = dkv_ref[...].astype(BF16)
        dw_ref[...] = _dot_tn(mn, dkv).astype(BF16)
        dmn = _dot_nt(dkv, w_ref[...])
        dg_ref[...] = jnp.sum(dmn * mh, axis=0, keepdims=True)

    return pl.pallas_call(body, out_shape=(jax.ShapeDtypeStruct((D, D), BF16), jax.ShapeDtypeStruct((1, D), F32)),
                          name=name, compiler_params=_cp())(mem, mem_g, wkv, dkv)


def _in_fwd(x, pre_g, w_t, name):
    s, n = x.shape[0], w_t.shape[0]
    tm = min(512, s)
    nc = 256

    def body(x_ref, g_ref, w_ref, p_ref, h_ref):
        xv = x_ref[...]
        r = lax.rsqrt(jnp.mean(xv * xv, axis=-1, keepdims=True) + EPS)
        h = (xv * r * g_ref[...]).astype(BF16)
        h_ref[...] = h
        for j in range(n // nc):
            p_ref[:, j * nc:(j + 1) * nc] = _dot_nt(h, w_ref[j * nc:(j + 1) * nc, :]).astype(BF16)

    return pl.pallas_call(
        body, grid=(s // tm,), name=name,
        out_shape=(jax.ShapeDtypeStruct((s, n), BF16), jax.ShapeDtypeStruct((s, D), BF16)),
        in_specs=[pl.BlockSpec((tm, D), lambda i: (i, 0)), _const((1, D)), _resident((n, D))],
        out_specs=(pl.BlockSpec((tm, n), lambda i: (i, 0)), pl.BlockSpec((tm, D), lambda i: (i, 0))),
        compiler_params=_cp(("arbitrary",)),
    )(x, pre_g, w_t)


def _xattn_fwd(q, kv_ref):
    outs, probs = [], []
    for h in range(XA // HD):
        qh = q[:, h * HD:(h + 1) * HD]
        kh = kv_ref[:, h * HD:(h + 1) * HD]
        vh = kv_ref[:, XA + h * HD:XA + (h + 1) * HD]
        sc = _dot_nt(qh, kh) * SCALE
        e = jnp.exp(sc - jnp.max(sc, axis=-1, keepdims=True))
        pr = e / jnp.sum(e, axis=-1, keepdims=True)
        outs.append(_dot(pr.astype(BF16), vh))
        probs.append(pr)
    return jnp.concatenate(outs, axis=-1), probs


def _xattn_bwd(dyx, q, probs, kv_ref, dkv_ref):
    dqs = []
    for h in range(XA // HD):
        qh = q[:, h * HD:(h + 1) * HD]
        kh = kv_ref[:, h * HD:(h + 1) * HD]
        vh = kv_ref[:, XA + h * HD:XA + (h + 1) * HD]
        dy = dyx[:, h * HD:(h + 1) * HD].astype(BF16)
        pr = probs[h]
        dp = _dot_nt(dy, vh)
        ds = (pr * (dp - jnp.sum(dp * pr, axis=-1, keepdims=True))).astype(BF16)
        dqs.append(_dot(ds, kh) * SCALE)
        dkv_ref[:, h * HD:(h + 1) * HD] += _dot_tn(ds, qh) * SCALE
        dkv_ref[:, XA + h * HD:XA + (h + 1) * HD] += _dot_tn(pr.astype(BF16), dy)
    return jnp.concatenate(dqs, axis=-1)


def _layer_norm_fwd(v, g, b):
    mu = jnp.mean(v, axis=-1, keepdims=True)
    vc = v - mu
    rstd = lax.rsqrt(jnp.mean(vc * vc, axis=-1, keepdims=True) + EPS)
    vhat = vc * rstd
    return vhat * g + b, vhat, rstd


def _layer_norm_bwd(dy, vhat, rstd, g):
    dvh = dy * g
    return rstd * (dvh - jnp.mean(dvh, axis=-1, keepdims=True) - vhat * jnp.mean(dvh * vhat, axis=-1, keepdims=True))


def _head_masks():
    col = lax.broadcasted_iota(jnp.int32, (1, BW), 1)
    return [(col >= h * (BW // 4)) & (col < (h + 1) * (BW // 4)) for h in range(4)]


def _halo_prev(nblk_per_tile):
    return lambda i: (jnp.maximum(i * nblk_per_tile - 1, 0), 0)


def _row_ids(i, t):
    return i * t + lax.broadcasted_iota(jnp.int32, (t, 1), 0)


def _even_mix(i, p_ref, ph_ref, ln_g, ln_b, wcat_ref, bsg_ref, bconv_ref, wbuf):
    t = p_ref.shape[0]
    u = p_ref[:, 0:BW].astype(F32)
    v = p_ref[:, BW:2 * BW].astype(F32)
    bg = p_ref[:, 2 * BW:3 * BW].astype(F32)
    cg = p_ref[:, 3 * BW:4 * BW].astype(F32)
    xin = p_ref[:, 4 * BW:5 * BW].astype(F32)
    vn, vhat, rstd = _layer_norm_fwd(v, ln_g, ln_b)
    masks = _head_masks()
    sgs, vsts = [], []
    for n in range(t // CH):
        vn_c = vn[n * CH:(n + 1) * CH]
        vst = jnp.concatenate([jnp.where(m, vn_c, 0.0) for m in masks], axis=0).astype(BF16)
        sgs.append(_dot(wcat_ref[...], vst) + bsg_ref[...])
        vsts.append(vst)
    sg = jnp.concatenate(sgs, axis=0)
    ya = u * sg
    w_halo = ph_ref[:, 3 * BW:4 * BW].astype(F32) * ph_ref[:, 4 * BW:5 * BW].astype(F32)
    wbuf[0:HALO, :] = jnp.where(i > 0, w_halo, 0.0)
    wbuf[HALO:HALO + t, :] = cg * xin
    conv = (bconv_ref[0:1, :] * wbuf[pl.ds(HALO - 2, t), :] + bconv_ref[1:2, :] * wbuf[pl.ds(HALO - 1, t), :]
            + bconv_ref[2:3, :] * wbuf[pl.ds(HALO, t), :])
    yb = bg * conv
    return dict(u=u, bg=bg, vhat=vhat, rstd=rstd, sg=sg, vsts=vsts, conv=conv, ya=ya, yb=yb, masks=masks)


def _pool_select(vals):
    col = lax.broadcasted_iota(jnp.int32, (1, BW), 1)
    g = BW // 4
    return jnp.where(col < g, vals[0], jnp.where(col < 2 * g, vals[1], jnp.where(col < 3 * g, vals[2], vals[3])))


def _inv_counts(i, t):
    rows = _row_ids(i, t) + 1
    return [1.0 / jnp.minimum(rows, w).astype(F32) for w in POOL_WINDOWS]


def _odd_mix(i, p_ref, ph_ref, wbd_ref, cscale, dww_ref, dwb, ln_g, ln_b, pww_ref, pwb, zbuf, gbuf):
    t = p_ref.shape[0]
    zc = p_ref[:, 0:BW].astype(F32)
    ga = p_ref[:, BW:2 * BW].astype(F32)
    gb = p_ref[:, 2 * BW:3 * BW].astype(F32)
    zbuf[0:HALO, :] = jnp.where(i > 0, ph_ref[:, 0:BW].astype(F32), 0.0)
    zbuf[HALO:HALO + t, :] = zc
    acc = zc
    sums = []
    k = 1
    for w in POOL_WINDOWS:
        while k < w:
            acc = acc + zbuf[pl.ds(HALO - k, t), :]
            k += 1
        sums.append(acc)
    inv = _inv_counts(i, t)
    pooled = _pool_select([s_ * c_ for s_, c_ in zip(sums, inv)]) - zc
    pooled_bf = pooled.astype(BF16)
    pre = _dot(pooled_bf, wbd_ref[...])
    yc = pre * cscale
    sgb = _sigmoid(gb)
    z = ga * sgb
    gh_a = ph_ref[:, BW:2 * BW].astype(F32)
    gh_b = ph_ref[:, 2 * BW:3 * BW].astype(F32)
    gbuf[0:HALO, :] = jnp.where(i > 0, gh_a * _sigmoid(gh_b), 0.0)
    gbuf[HALO:HALO + t, :] = z
    cv = dwb + dww_ref[CONF - 1:CONF, :] * z
    for k in range(CONF - 1):
        cv = cv + dww_ref[k:k + 1, :] * gbuf[pl.ds(HALO - (CONF - 1) + k, t), :]
    zl, zhat, rstd = _layer_norm_fwd(cv, ln_g, ln_b)
    szl = _sigmoid(zl)
    zs = (zl * szl).astype(BF16)
    yd = _dot(zs, pww_ref[...]) + pwb
    return dict(ga=ga, sgb=sgb, pooled_bf=pooled_bf, pre=pre, yc=yc, zhat=zhat, rstd=rstd, zl=zl, szl=szl,
                zs=zs, yd=yd, inv=inv)


def _post_norm(o, post_g):
    r = lax.rsqrt(jnp.mean(o * o, axis=-1, keepdims=True) + EPS)
    return o * r, r


def _gate_out(y_a, y_b, y_x, gate, wout_ref):
    sgt = _sigmoid(gate)
    sgate = gate * sgt
    ys = [(y_a * sgate[:, 0:BW]).astype(BF16), (y_b * sgate[:, BW:2 * BW]).astype(BF16),
          (y_x * sgate[:, 2 * BW:MIX]).astype(BF16)]
    o = (_dot(ys[0], wout_ref[0:BW, :]) + _dot(ys[1], wout_ref[BW:2 * BW, :]) + _dot(ys[2], wout_ref[2 * BW:MIX, :]))
    return o, ys, sgt, sgate


def _tile_specs(s, n):
    nh = TS // HALO
    return pl.BlockSpec((TS, n), lambda i: (i, 0)), pl.BlockSpec((HALO, n), _halo_prev(nh))


def _even_fwd(x, p, kv, ln_g, ln_b, wcat, bsg, bconv, wout, post_g):
    s = x.shape[0]

    def body(x_ref, p_ref, ph_ref, kv_ref, lng, lnb, wcat_ref, bsg_ref, bconv_ref, wout_ref, pg, x1_ref, o_ref, wbuf):
        i = pl.program_id(0)
        mx = _even_mix(i, p_ref, ph_ref, lng[...], lnb[...], wcat_ref, bsg_ref, bconv_ref, wbuf)
        yx, _ = _xattn_fwd(p_ref[:, 5 * BW:5 * BW + XA], kv_ref)
        gate = p_ref[:, 5 * BW + XA:EVEN_IN].astype(F32)
        o, _, _, _ = _gate_out(mx["ya"], mx["yb"], yx, gate, wout_ref)
        n, _ = _post_norm(o, pg[...])
        o_ref[...] = o
        x1_ref[...] = x_ref[...] + n * pg[...]

    tile, halo = _tile_specs(s, EVEN_IN)
    row = pl.BlockSpec((TS, D), lambda i: (i, 0))
    return pl.pallas_call(
        body, grid=(s // TS,), name="even_fwd",
        out_shape=(jax.ShapeDtypeStruct((s, D), F32), jax.ShapeDtypeStruct((s, D), F32)),
        in_specs=[row, tile, halo, _const((N_MEM, D)), _const((1, BW)), _const((1, BW)), _const((CH, 4 * CH)),
                  _const((CH, BW)), _const((3, BW)), _resident((MIX, D)), _const((1, D))],
        out_specs=(row, row),
        scratch_shapes=[pltpu.VMEM((HALO + TS, BW), F32)],
        compiler_params=_cp(("arbitrary",)),
    )(x, p, p, kv, ln_g, ln_b, wcat, bsg, bconv, wout, post_g)


def _odd_fwd(x1, p, kv, wbd, cscale, dww, dwb, ln_g, ln_b, pww, pwb, wout, post_g, target):
    s = x1.shape[0]

    def body(x_ref, p_ref, ph_ref, kv_ref, wbd_ref, cs, dww_ref, dwb_ref, lng, lnb, pww_ref, pwb_ref, wout_ref, pg,
             tgt_ref, dx_ref, o_ref, loss_ref, zbuf, gbuf):
        i = pl.program_id(0)
        mx = _odd_mix(i, p_ref, ph_ref, wbd_ref, cs[...], dww_ref, dwb_ref[...], lng[...], lnb[...], pww_ref,
                      pwb_ref[...], zbuf, gbuf)
        yx, _ = _xattn_fwd(p_ref[:, 3 * BW:3 * BW + XA], kv_ref)
        gate = p_ref[:, 3 * BW + XA:ODD_IN].astype(F32)
        o, _, _, _ = _gate_out(mx["yc"], mx["yd"], yx, gate, wout_ref)
        n, _ = _post_norm(o, pg[...])
        o_ref[...] = o
        err = x_ref[...] + n * pg[...] - tgt_ref[...]
        dx_ref[...] = err * (1.0 / D)

        @pl.when(i == 0)
        def _():
            loss_ref[...] = jnp.zeros_like(loss_ref)

        loss_ref[...] += 0.5 * jnp.sum(jnp.sum(err * err, axis=-1, keepdims=True) * (1.0 / D), axis=0, keepdims=True)

    tile, halo = _tile_specs(s, ODD_IN)
    row = pl.BlockSpec((TS, D), lambda i: (i, 0))
    vec = _const((1, BW))
    return pl.pallas_call(
        body, grid=(s // TS,), name="odd_fwd",
        out_shape=(jax.ShapeDtypeStruct((s, D), F32), jax.ShapeDtypeStruct((s, D), F32),
                   jax.ShapeDtypeStruct((8, 128), F32)),
        in_specs=[row, tile, halo, _const((N_MEM, D)), _const((BW, BW)), vec, _const((CONF, BW)), vec, vec, vec,
                  _const((BW, BW)), vec, _resident((MIX, D)), _const((1, D)), row],
        out_specs=(row, row, _const((8, 128))),
        scratch_shapes=[pltpu.VMEM((HALO + TS, BW), F32), pltpu.VMEM((HALO + TS, BW), F32)],
        compiler_params=_cp(("arbitrary",)),
    )(x1, p, p, kv, wbd, cscale, dww, dwb, ln_g, ln_b, pww, pwb, wout, post_g, target)


def _acc_init(i, refs):
    @pl.when(i == 0)
    def _():
        for r in refs:
            r[...] = jnp.zeros_like(r)


def _post_norm_bwd(dx, o, pg, dpg_ref):
    n, r = _post_norm(o, pg)
    dpg_ref[...] += jnp.sum(dx * n, axis=0, keepdims=True)
    dn = dx * pg
    return (r * (dn - n * jnp.mean(dn * n, axis=-1, keepdims=True))).astype(BF16)


def _gate_bwd(do, wout_ref, ys_f32, gate, y_ref):
    dy = _dot_nt(do, wout_ref[...])
    sgt = _sigmoid(gate)
    sgate = gate * sgt
    dsilu = sgt * (1.0 + gate * (1.0 - sgt))
    offs = (0, BW, 2 * BW, MIX)
    dys, dgs = [], []
    for j, yv in enumerate(ys_f32):
        a, b = offs[j], offs[j + 1]
        y_ref[:, a:b] = (yv * sgate[:, a:b]).astype(BF16)
        dys.append(dy[:, a:b] * sgate[:, a:b])
        dgs.append(dy[:, a:b] * yv * dsilu[:, a:b])
    return dys, jnp.concatenate(dgs, axis=-1)


def _even_bwd1(dx, o, p, kv, ln_g, ln_b, wcat, bsg, hsel, bconv, wout, post_g):
    s = dx.shape[0]

    def body(dx_ref, o_ref, p_ref, ph_ref, kv_ref, lng, lnb, wcat_ref, bsg_ref, hsel_ref, bconv_ref, wout_ref, pg,
             dpa_ref, dpc_ref, tmp_ref, do_ref, y_ref, dpg_ref, dlng_ref, dlnb_ref, dwcat_ref, dbs_ref, dbconv_ref,
             dkv_ref, wbuf):
        i = pl.program_id(0)
        _acc_init(i, (dpg_ref, dlng_ref, dlnb_ref, dwcat_ref, dbs_ref, dbconv_ref, dkv_ref))
        mx = _even_mix(i, p_ref, ph_ref, lng[...], lnb[...], wcat_ref, bsg_ref, bconv_ref, wbuf)
        q = p_ref[:, 5 * BW:5 * BW + XA]
        yx, probs = _xattn_fwd(q, kv_ref)
        gate = p_ref[:, 5 * BW + XA:EVEN_IN].astype(F32)
        do = _post_norm_bwd(dx_ref[...], o_ref[...], pg[...], dpg_ref)
        do_ref[...] = do
        (dya, dyb, dyx), dgate = _gate_bwd(do, wout_ref, (mx["ya"], mx["yb"], yx), gate, y_ref)
        dpa_ref[:, 0:BW] = (dya * mx["sg"]).astype(BF16)
        dsg = (dya * mx["u"]).astype(BF16)
        dvns = []
        for n in range(TS // CH):
            dsg_c = dsg[n * CH:(n + 1) * CH]
            dvst = _dot_tn(wcat_ref[...], dsg_c)
            dvn_c = jnp.where(mx["masks"][0], dvst[0:CH], 0.0)
            for h in range(1, 4):
                dvn_c = dvn_c + jnp.where(mx["masks"][h], dvst[h * CH:(h + 1) * CH], 0.0)
            dvns.append(dvn_c)
            dwcat_ref[...] += _dot_nt(dsg_c, mx["vsts"][n])
            dbs_ref[...] += _dot(dsg_c, hsel_ref[...])
        dvn = jnp.concatenate(dvns, axis=0)
        dlng_ref[...] += jnp.sum(dvn * mx["vhat"], axis=0, keepdims=True)
        dlnb_ref[...] += jnp.sum(dvn, axis=0, keepdims=True)
        dpa_ref[:, BW:2 * BW] = _layer_norm_bwd(dvn, mx["vhat"], mx["rstd"], lng[...]).astype(BF16)
        dpa_ref[:, 2 * BW:3 * BW] = (dyb * mx["conv"]).astype(BF16)
        dconv = dyb * mx["bg"]
        tmp_ref[...] = dconv.astype(BF16)
        for k in range(3):
            dbconv_ref[k:k + 1, :] += jnp.sum(dconv * wbuf[pl.ds(HALO - 2 + k, TS), :], axis=0, keepdims=True)
        dpc_ref[:, 0:XA] = _xattn_bwd(dyx, q, probs, kv_ref, dkv_ref).astype(BF16)
        dpc_ref[:, XA:XA + MIX] = dgate.astype(BF16)

    tile, halo = _tile_specs(s, EVEN_IN)
    row = pl.BlockSpec((TS, D), lambda i: (i, 0))
    vec = _const((1, BW))

    def out(n):
        return pl.BlockSpec((TS, n), lambda i: (i, 0))

    return pl.pallas_call(
        body, grid=(s // TS,), name="even_bwd1",
        out_shape=(jax.ShapeDtypeStruct((s, 3 * BW), BF16), jax.ShapeDtypeStruct((s, XA + MIX), BF16),
                   jax.ShapeDtypeStruct((s, BW), BF16), jax.ShapeDtypeStruct((s, D), BF16),
                   jax.ShapeDtypeStruct((s, MIX), BF16),
                   jax.ShapeDtypeStruct((1, D), F32), jax.ShapeDtypeStruct((1, BW), F32),
                   jax.ShapeDtypeStruct((1, BW), F32), jax.ShapeDtypeStruct((CH, 4 * CH), F32),
                   jax.ShapeDtypeStruct((CH, 128), F32), jax.ShapeDtypeStruct((8, BW), F32),
                   jax.ShapeDtypeStruct((N_MEM, D), F32)),
        in_specs=[row, row, tile, halo, _const((N_MEM, D)), vec, vec, _const((CH, 4 * CH)), _const((CH, BW)),
                  _const((BW, 128)), _const((3, BW)), _resident((MIX, D)), _const((1, D))],
        out_specs=(out(3 * BW), out(XA + MIX), out(BW), out(D), out(MIX),
                   _const((1, D)), vec, vec, _const((CH, 4 * CH)), _const((CH, 128)), _const((8, BW)),
                   _const((N_MEM, D))),
        scratch_shapes=[pltpu.VMEM((HALO + TS, BW), F32)],
        compiler_params=_cp(("arbitrary",)),
    )(dx, o, p, p, kv, ln_g, ln_b, wcat, bsg, hsel, bconv, wout, post_g)


def _odd_bwd1(dx, o, p, kv, wbd, cscale, dww, dwb, ln_g, ln_b, pww, pwb, wout, post_g):
    s = dx.shape[0]

    def body(dx_ref, o_ref, p_ref, ph_ref, kv_ref, wbd_ref, cs, dww_ref, dwb_ref, lng, lnb, pww_ref, pwb_ref,
             wout_ref, pg,
             dpc_ref, tmpc_ref, tmpd_ref, do_ref, y_ref, dpg_ref, dcs_ref, dwbd_ref, ddww_ref, ddwb_ref, dlng_ref,
             dlnb_ref, dpww_ref, dpwb_ref, dkv_ref, zbuf, gbuf):
        i = pl.program_id(0)
        _acc_init(i, (dpg_ref, dcs_ref, dwbd_ref, ddww_ref, ddwb_ref, dlng_ref, dlnb_ref, dpww_ref, dpwb_ref,
                      dkv_ref))
        mx = _odd_mix(i, p_ref, ph_ref, wbd_ref, cs[...], dww_ref, dwb_ref[...], lng[...], lnb[...], pww_ref,
                      pwb_ref[...], zbuf, gbuf)
        q = p_ref[:, 3 * BW:3 * BW + XA]
        yx, probs = _xattn_fwd(q, kv_ref)
        gate = p_ref[:, 3 * BW + XA:ODD_IN].astype(F32)
        do = _post_norm_bwd(dx_ref[...], o_ref[...], pg[...], dpg_ref)
        do_ref[...] = do
        (dyc, dyd, dyx), dgate = _gate_bwd(do, wout_ref, (mx["yc"], mx["yd"], yx), gate, y_ref)
        dcs_ref[...] += jnp.sum(dyc * mx["pre"], axis=0, keepdims=True)
        dpre = (dyc * cs[...]).astype(BF16)
        dwbd_ref[...] += _dot_tn(mx["pooled_bf"], dpre)
        dpooled = _dot_nt(dpre, wbd_ref[...])
        tmpc_ref[...] = _pool_select([dpooled * c_ for c_ in mx["inv"]]).astype(BF16)
        dyd_bf = dyd.astype(BF16)
        dpwb_ref[...] += jnp.sum(dyd, axis=0, keepdims=True)
        dpww_ref[...] += _dot_tn(mx["zs"], dyd_bf)
        dzs = _dot_nt(dyd_bf, pww_ref[...])
        zl, szl = mx["zl"], mx["szl"]
        dzl = dzs * (szl * (1.0 + zl * (1.0 - szl)))
        dlng_ref[...] += jnp.sum(dzl * mx["zhat"], axis=0, keepdims=True)
        dlnb_ref[...] += jnp.sum(dzl, axis=0, keepdims=True)
        dcv = _layer_norm_bwd(dzl, mx["zhat"], mx["rstd"], lng[...])
        tmpd_ref[...] = dcv.astype(BF16)
        ddwb_ref[...] += jnp.sum(dcv, axis=0, keepdims=True)
        for k in range(CONF):
            ddww_ref[k:k + 1, :] += jnp.sum(dcv * gbuf[pl.ds(HALO - (CONF - 1) + k, TS), :], axis=0, keepdims=True)
        dpc_ref[:, 0:XA] = _xattn_bwd(dyx, q, probs, kv_ref, dkv_ref).astype(BF16)
        dpc_ref[:, XA:XA + MIX] = dgate.astype(BF16)

    tile, halo = _tile_specs(s, ODD_IN)
    row = pl.BlockSpec((TS, D), lambda i: (i, 0))
    vec = _const((1, BW))

    def out(n):
        return pl.BlockSpec((TS, n), lambda i: (i, 0))

    return pl.pallas_call(
        body, grid=(s // TS,), name="odd_bwd1",
        out_shape=(jax.ShapeDtypeStruct((s, XA + MIX), BF16), jax.ShapeDtypeStruct((s, BW), BF16),
                   jax.ShapeDtypeStruct((s, BW), BF16), jax.ShapeDtypeStruct((s, D), BF16),
                   jax.ShapeDtypeStruct((s, MIX), BF16),
                   jax.ShapeDtypeStruct((1, D), F32), jax.ShapeDtypeStruct((1, BW), F32),
                   jax.ShapeDtypeStruct((BW, BW), F32), jax.ShapeDtypeStruct((32, BW), F32),
                   jax.ShapeDtypeStruct((1, BW), F32), jax.ShapeDtypeStruct((1, BW), F32),
                   jax.ShapeDtypeStruct((1, BW), F32), jax.ShapeDtypeStruct((BW, BW), F32),
                   jax.ShapeDtypeStruct((1, BW), F32), jax.ShapeDtypeStruct((N_MEM, D), F32)),
        in_specs=[row, row, tile, halo, _const((N_MEM, D)), _const((BW, BW)), vec, _const((CONF, BW)), vec, vec, vec,
                  _const((BW, BW)), vec, _resident((MIX, D)), _const((1, D))],
        out_specs=(out(XA + MIX), out(BW), out(BW), out(D), out(MIX),
                   _const((1, D)), vec, _const((BW, BW)), _const((32, BW)), vec, vec, vec, _const((BW, BW)), vec,
                   _const((N_MEM, D))),
        scratch_shapes=[pltpu.VMEM((HALO + TS, BW), F32), pltpu.VMEM((HALO + TS, BW), F32)],
        compiler_params=_cp(("arbitrary",)),
    )(dx, o, p, p, kv, wbd, cscale, dww, dwb, ln_g, ln_b, pww, pwb, wout, post_g)


def _halo_next(nblk_per_tile, nblk):
    return lambda i: (jnp.minimum((i + 1) * nblk_per_tile, nblk - 1), 0)


def _pre_norm_bwd(dh, x, pre_g, dres, dpre_ref):
    r = lax.rsqrt(jnp.mean(x * x, axis=-1, keepdims=True) + EPS)
    xh = x * r
    dpre_ref[...] += jnp.sum(dh * xh, axis=0, keepdims=True)
    dxh = dh * pre_g
    return dres + r * (dxh - xh * jnp.mean(dxh * xh, axis=-1, keepdims=True))


def _even_bwd2(dpa, dpc, tmp, p, bconv, w_t, x, pre_g, dres):
    s = x.shape[0]
    nt = s // TS

    def body(dpa_ref, dpc_ref, tmp_ref, tmph_ref, cg_ref, xin_ref, bconv_ref, w_ref, x_ref, pg, dres_ref,
             dpb_ref, dx_ref, dpre_ref, dbuf):
        i = pl.program_id(0)
        _acc_init(i, (dpre_ref,))
        dbuf[0:TS, :] = tmp_ref[...].astype(F32)
        dbuf[TS:TS + HALO, :] = jnp.where(i < nt - 1, tmph_ref[...].astype(F32), 0.0)
        dw = (bconv_ref[2:3, :] * dbuf[pl.ds(0, TS), :] + bconv_ref[1:2, :] * dbuf[pl.ds(1, TS), :]
              + bconv_ref[0:1, :] * dbuf[pl.ds(2, TS), :])
        dcg = (dw * xin_ref[...].astype(F32)).astype(BF16)
        dxin = (dw * cg_ref[...].astype(F32)).astype(BF16)
        dpb_ref[:, 0:BW] = dcg
        dpb_ref[:, BW:2 * BW] = dxin
        dh = (_dot(dpa_ref[...], w_ref[0:3 * BW, :]) + _dot(dcg, w_ref[3 * BW:4 * BW, :])
              + _dot(dxin, w_ref[4 * BW:5 * BW, :]) + _dot(dpc_ref[...], w_ref[5 * BW:EVEN_IN, :]))
        dx_ref[...] = _pre_norm_bwd(dh, x_ref[...], pg[...], dres_ref[...], dpre_ref)

    row = pl.BlockSpec((TS, D), lambda i: (i, 0))

    def tile(n, j=0):
        return pl.BlockSpec((TS, n), lambda i: (i, j))

    return pl.pallas_call(
        body, grid=(nt,), name="even_bwd2",
        out_shape=(jax.ShapeDtypeStruct((s, 2 * BW), BF16), jax.ShapeDtypeStruct((s, D), F32),
                   jax.ShapeDtypeStruct((1, D), F32)),
        in_specs=[tile(3 * BW), tile(XA + MIX), tile(BW), pl.BlockSpec((HALO, BW), _halo_next(TS // HALO, s // HALO)),
                  tile(BW, 3), tile(BW, 4), _const((3, BW)), _resident((EVEN_IN, D)), row, _const((1, D)), row],
        out_specs=(tile(2 * BW), row, _const((1, D))),
        scratch_shapes=[pltpu.VMEM((TS + HALO, BW), F32)],
        compiler_params=_cp(("arbitrary",)),
    )(dpa, dpc, tmp, tmp, p, p, bconv, w_t, x, pre_g, dres)


def _odd_bwd2(dpc, tmpc, tmpd, p, dww, w_t, x, pre_g, dres):
    s = x.shape[0]
    nt = s // TS

    def body(dpc_ref, tc_ref, tch_ref, td_ref, tdh_ref, ga_ref, gb_ref, dww_ref, w_ref, x_ref, pg, dres_ref,
             dpb_ref, dx_ref, dpre_ref, cbuf, dbuf):
        i = pl.program_id(0)
        _acc_init(i, (dpre_ref,))
        last = i < nt - 1
        e = tc_ref[...].astype(F32)
        cbuf[0:TS, :] = e
        cbuf[TS:TS + HALO, :] = jnp.where(last, tch_ref[...].astype(F32), 0.0)
        dbuf[0:TS, :] = td_ref[...].astype(F32)
        dbuf[TS:TS + HALO, :] = jnp.where(last, tdh_ref[...].astype(F32), 0.0)
        acc = e
        sums = []
        k = 1
        for w in POOL_WINDOWS:
            while k < w:
                acc = acc + cbuf[pl.ds(k, TS), :]
                k += 1
            sums.append(acc)
        rows = _row_ids(i, TS) + 1
        cnt = _pool_select([jnp.minimum(rows, w).astype(F32) for w in POOL_WINDOWS])
        dzc = (_pool_select(sums) - e * cnt).astype(BF16)
        dz = dww_ref[CONF - 1:CONF, :] * dbuf[pl.ds(0, TS), :]
        for sft in range(1, CONF):
            dz = dz + dww_ref[CONF - 1 - sft:CONF - sft, :] * dbuf[pl.ds(sft, TS), :]
        ga = ga_ref[...].astype(F32)
        sgb = _sigmoid(gb_ref[...].astype(F32))
        dga = (dz * sgb).astype(BF16)
        dgb = (dz * ga * sgb * (1.0 - sgb)).astype(BF16)
        dpb_ref[:, 0:BW] = dzc
        dpb_ref[:, BW:2 * BW] = dga
        dpb_ref[:, 2 * BW:3 * BW] = dgb
        dh = (_dot(dzc, w_ref[0:BW, :]) + _dot(dga, w_ref[BW:2 * BW, :]) + _dot(dgb, w_ref[2 * BW:3 * BW, :])
              + _dot(dpc_ref[...], w_ref[3 * BW:ODD_IN, :]))
        dx_ref[...] = _pre_norm_bwd(dh, x_ref[...], pg[...], dres_ref[...], dpre_ref)

    row = pl.BlockSpec((TS, D), lambda i: (i, 0))

    def tile(n, j=0):
        return pl.BlockSpec((TS, n), lambda i: (i, j))

    nxt = pl.BlockSpec((HALO, BW), _halo_next(TS // HALO, s // HALO))
    return pl.pallas_call(
        body, grid=(nt,), name="odd_bwd2",
        out_shape=(jax.ShapeDtypeStruct((s, 3 * BW), BF16), jax.ShapeDtypeStruct((s, D), F32),
                   jax.ShapeDtypeStruct((1, D), F32)),
        in_specs=[tile(XA + MIX), tile(BW), nxt, tile(BW), nxt, tile(BW, 1), tile(BW, 2), _const((CONF, BW)),
                  _resident((ODD_IN, D)), row, _const((1, D)), row],
        out_specs=(tile(3 * BW), row, _const((1, D))),
        scratch_shapes=[pltpu.VMEM((TS + HALO, BW), F32), pltpu.VMEM((TS + HALO, BW), F32)],
        compiler_params=_cp(("arbitrary",)),
    )(dpc, tmpc, tmpc, tmpd, tmpd, p, p, dww, w_t, x, pre_g, dres)


def _grad_tn(a, b, tm, out=None, rows=None, row0=0, name="grad_tn"):
    s, m = a.shape
    n = b.shape[1]
    ts = min(2048, s)
    rows = m if rows is None else rows
    blk0 = row0 // tm
    assert m % tm == 0 and row0 % tm == 0 and s % ts == 0
    ns = s // ts

    def body(*refs):
        a_ref, b_ref = refs[0], refs[1]
        o_ref, acc = refs[-2], refs[-1]
        k = pl.program_id(1)

        @pl.when(k == 0)
        def _():
            acc[...] = jnp.zeros_like(acc)

        acc[...] += _dot_tn(a_ref[...], b_ref[...])

        @pl.when(k == ns - 1)
        def _():
            o_ref[...] = acc[...].astype(BF16)

    in_specs = [pl.BlockSpec((ts, tm), lambda i, k: (k, i)), pl.BlockSpec((ts, n), lambda i, k: (k, 0))]
    args = [a, b]
    aliases = {}
    if out is not None:
        in_specs.append(pl.BlockSpec(memory_space=pltpu.HBM))
        args.append(out)
        aliases = {2: 0}
    return pl.pallas_call(
        body, grid=(m // tm, ns), name=name,
        out_shape=jax.ShapeDtypeStruct((rows, n), BF16),
        in_specs=in_specs, out_specs=pl.BlockSpec((tm, n), lambda i, k: (blk0 + i, 0)),
        scratch_shapes=[pltpu.VMEM((tm, n), F32)], input_output_aliases=aliases,
        compiler_params=_cp(("arbitrary", "arbitrary")),
    )(*args)


def _place():
    x, y, c = lax.axis_index("x"), lax.axis_index("y"), lax.axis_index("c")
    chips = [(1 - x, y), (x, 1 - y), (1 - x, 1 - y)]
    return x, y, c, chips


def _hbm_specs(n):
    return [pl.BlockSpec(memory_space=pltpu.HBM)] * n


def _row_tile(r):
    for cand in (512, 400, 304, 256, 192, 128, 96, 16):
        if r % cand == 0:
            return cand
    raise ValueError(r)


def _place_shard(shard, place, dtype, name):
    r, cc = shard.shape
    tr = _row_tile(r)
    nt = r // tr

    def body(place_ref, s_ref, o_ref):
        o_ref[...] = s_ref[...].astype(dtype)

    return pl.pallas_call(
        body, name=name, out_shape=jax.ShapeDtypeStruct((N_CHIPS * r, cc), dtype),
        grid_spec=pltpu.PrefetchScalarGridSpec(
            num_scalar_prefetch=1, grid=(nt,),
            in_specs=[pl.BlockSpec((tr, cc), lambda i, pr: (i, 0))],
            out_specs=pl.BlockSpec((tr, cc), lambda i, pr: (pr[1] * nt + i, 0))),
        compiler_params=_cp(("arbitrary",)),
    )(place, shard)


def _gather_weights(fulls):
    n = len(fulls)

    def body(*refs):
        outs = refs[n:2 * n]
        send_sems, recv_sems = refs[2 * n:]
        x, y, c, chips = _place()
        me_k = 2 * x + y
        sibling = (x, y, 1 - c)

        def rows(a, k, half):
            r = fulls[a].shape[0] // N_CHIPS
            return outs[a].at[pl.ds(k * r + half * (r // 2), r // 2)]

        def copy(a, j, blk, to):
            return pltpu.make_async_remote_copy(src_ref=blk, dst_ref=blk, send_sem=send_sems.at[a * 6 + j],
                                                recv_sem=recv_sems.at[a * 6 + j], device_id=to, device_id_type=MESH)

        started = []
        for j, (px, py) in enumerate(chips):
            for a in range(n):
                cp = copy(a, j, rows(a, me_k, c), (px, py, c))
                cp.start()
                started.append(cp)
        for j, (px, py) in enumerate(chips):
            k = 2 * px + py
            for a in range(n):
                copy(a, j, rows(a, k, c), (px, py, c)).wait_recv()
                cp = copy(a, 3 + j, rows(a, k, c), sibling)
                cp.start()
                started.append(cp)
        for j, (px, py) in enumerate(chips):
            k = 2 * px + py
            for a in range(n):
                copy(a, 3 + j, rows(a, k, 1 - c), sibling).wait_recv()
        for cp in started:
            cp.wait_send()

    return pl.pallas_call(
        body, name="gather_weights",
        out_shape=tuple(jax.ShapeDtypeStruct(a.shape, a.dtype) for a in fulls),
        in_specs=_hbm_specs(n), out_specs=tuple(_hbm_specs(n)),
        input_output_aliases={a: a for a in range(n)},
        scratch_shapes=[pltpu.SemaphoreType.DMA((6 * n,)), pltpu.SemaphoreType.DMA((6 * n,))],
    )(*fulls)


def _swap_halves(grads, small):
    n = len(grads)

    def body(*refs):
        ins, outs = refs[:n + 1], refs[n + 1:2 * n + 2]
        send_sems, recv_sems = refs[2 * n + 2:]
        x, y, c, _ = _place()
        sibling = (x, y, 1 - c)
        cps = []
        for a in range(n + 1):
            src = ins[a].at[:, 1 - c] if a < n else ins[a]
            cp = pltpu.make_async_remote_copy(src_ref=src, dst_ref=outs[a], send_sem=send_sems.at[a],
                                              recv_sem=recv_sems.at[a], device_id=sibling, device_id_type=MESH)
            cp.start()
            cps.append(cp)
        for cp in cps:
            cp.wait_recv()
        for cp in cps:
            cp.wait_send()

    outs = tuple(jax.ShapeDtypeStruct((g.shape[0],) + g.shape[2:], g.dtype) for g in grads)
    outs += (jax.ShapeDtypeStruct(small.shape, small.dtype),)
    return pl.pallas_call(
        body, name="swap_halves", out_shape=outs, in_specs=_hbm_specs(n + 1), out_specs=tuple(_hbm_specs(n + 1)),
        scratch_shapes=[pltpu.SemaphoreType.DMA((n + 1,)), pltpu.SemaphoreType.DMA((n + 1,))],
    )(*grads, small)


def _pair_sum(g, recv, place, name):
    _, _, h, cc = g.shape
    th = _row_tile(h)

    def body(c_ref, g_ref, r_ref, o_ref):
        o_ref[...] = (g_ref[...].astype(F32) + r_ref[...].astype(F32)).astype(o_ref.dtype)

    return pl.pallas_call(
        body, name=name, out_shape=jax.ShapeDtypeStruct(recv.shape, recv.dtype),
        grid_spec=pltpu.PrefetchScalarGridSpec(
            num_scalar_prefetch=1, grid=(N_CHIPS, h // th),
            in_specs=[pl.BlockSpec((None, None, th, cc), lambda k, r, c_ref: (k, c_ref[0], r, 0)),
                      pl.BlockSpec((None, th, cc), lambda k, r, c_ref: (k, r, 0))],
            out_specs=pl.BlockSpec((None, th, cc), lambda k, r, c_ref: (k, r, 0))),
        compiler_params=_cp(("arbitrary", "arbitrary")),
    )(place, g, recv)


def _small_sum(a, b):
    def body(a_ref, b_ref, o_ref):
        o_ref[...] = a_ref[...] + b_ref[...]

    return pl.pallas_call(body, name="small_pair_sum", out_shape=jax.ShapeDtypeStruct(a.shape, a.dtype),
                          compiler_params=_cp())(a, b)


def _exchange_chips(sums, small):
    n = len(sums)
    hs = small.shape[0] // 2

    def body(*refs):
        ins, outs = refs[:n + 1], refs[n + 1:2 * n + 2]
        send_sems, recv_sems, local_sem = refs[2 * n + 2:]
        x, y, c, chips = _place()
        me_k = 2 * x + y
        mine = ins[n].at[pl.ds(c * hs, hs)]
        local = pltpu.make_async_copy(mine, outs[n].at[me_k], local_sem)
        local.start()
        cps = []
        for j, (px, py) in enumerate(chips):
            for a in range(n + 1):
                src, dst = (ins[a].at[2 * px + py], outs[a].at[j]) if a < n else (mine, outs[n].at[me_k])
                cp = pltpu.make_async_remote_copy(
                    src_ref=src, dst_ref=dst, send_sem=send_sems.at[a * 3 + j],
                    recv_sem=recv_sems.at[a * 3 + j], device_id=(px, py, c), device_id_type=MESH)
                cp.start()
                cps.append(cp)
        for cp in cps:
            cp.wait_recv()
        for cp in cps:
            cp.wait_send()
        local.wait()

    outs = tuple(jax.ShapeDtypeStruct((3,) + g.shape[1:], g.dtype) for g in sums)
    outs += (jax.ShapeDtypeStruct((N_CHIPS, hs, small.shape[1]), small.dtype),)
    return pl.pallas_call(
        body, name="exchange_chips", out_shape=outs, in_specs=_hbm_specs(n + 1), out_specs=tuple(_hbm_specs(n + 1)),
        scratch_shapes=[pltpu.SemaphoreType.DMA((3 * (n + 1),)), pltpu.SemaphoreType.DMA((3 * (n + 1),)),
                        pltpu.SemaphoreType.DMA],
    )(*sums, small)


def _chip_sum(own, parts, place, name):
    npart, h, cc = parts.shape
    th = _row_tile(h)

    def body(*refs):
        p_ref, o_ref = refs[-2], refs[-1]
        acc = p_ref[0].astype(F32)
        if own is not None:
            acc = refs[1][...].astype(F32) + acc
        for k in range(1, npart):
            acc = acc + p_ref[k].astype(F32)
        o_ref[...] = acc

    in_specs = [pl.BlockSpec((npart, th, cc), lambda r, pr: (0, r, 0))]
    args = [parts]
    if own is not None:
        in_specs.insert(0, pl.BlockSpec((None, th, cc), lambda r, pr: (pr[1], r, 0)))
        args.insert(0, own)
    return pl.pallas_call(
        body, name=name, out_shape=jax.ShapeDtypeStruct((2, h, cc), F32),
        grid_spec=pltpu.PrefetchScalarGridSpec(
            num_scalar_prefetch=1, grid=(h // th,), in_specs=in_specs,
            out_specs=pl.BlockSpec((None, th, cc), lambda r, pr: (pr[0], r, 0))),
        compiler_params=_cp(("arbitrary",)),
    )(place, *args)


def _share_halves(halves):
    n = len(halves)

    def body(*refs):
        outs = refs[n:2 * n]
        send_sems, recv_sems = refs[2 * n:]
        x, y, c, _ = _place()
        cps = []
        for a in range(n):
            cp = pltpu.make_async_remote_copy(src_ref=outs[a].at[c], dst_ref=outs[a].at[c], send_sem=send_sems.at[a],
                                              recv_sem=recv_sems.at[a], device_id=(x, y, 1 - c), device_id_type=MESH)
            cp.start()
            cps.append(cp)
        for a in range(n):
            pltpu.make_async_remote_copy(src_ref=outs[a].at[1 - c], dst_ref=outs[a].at[1 - c], send_sem=send_sems.at[a],
                                         recv_sem=recv_sems.at[a], device_id=(x, y, 1 - c),
                                         device_id_type=MESH).wait_recv()
        for cp in cps:
            cp.wait_send()

    return pl.pallas_call(
        body, name="share_halves", out_shape=tuple(jax.ShapeDtypeStruct(g.shape, g.dtype) for g in halves),
        in_specs=_hbm_specs(n), out_specs=tuple(_hbm_specs(n)), input_output_aliases={a: a for a in range(n)},
        scratch_shapes=[pltpu.SemaphoreType.DMA((n,)), pltpu.SemaphoreType.DMA((n,))],
    )(*halves)


def _adamw_math(w, g, m, v):
    m = ADAM_B1 * m + (1.0 - ADAM_B1) * g
    v = ADAM_B2 * v + (1.0 - ADAM_B2) * (g * g)
    m_hat = m / (1.0 - ADAM_B1 ** ADAM_STEP)
    v_hat = v / (1.0 - ADAM_B2 ** ADAM_STEP)
    delta = -ADAM_LR * (m_hat / (jnp.sqrt(v_hat) + ADAM_EPS) + ADAM_WD * w)
    return delta, m, v


def _adamw_big(w, g, m, v, name):
    r, cc = w.shape
    tr = min(_row_tile(r), 256) if r % 256 == 0 else _row_tile(r)

    def body(w_ref, g_ref, m_ref, v_ref, d_ref, mo_ref, vo_ref):
        d, mm, vv = _adamw_math(w_ref[...], g_ref[...], m_ref[...], v_ref[...])
        d_ref[...] = d
        mo_ref[...] = mm
        vo_ref[...] = vv

    blk = pl.BlockSpec((tr, cc), lambda i: (i, 0))
    sd = jax.ShapeDtypeStruct((r, cc), F32)
    return pl.pallas_call(body, grid=(r // tr,), name=name, out_shape=(sd, sd, sd), in_specs=[blk] * 4,
                          out_specs=(blk, blk, blk), compiler_params=_cp(("arbitrary",)))(w, g, m, v)


def _adamw_small(ws, gs, ms, vs):
    n = len(ws)

    def body(*refs):
        for a in range(n):
            w_ref, g_ref, m_ref, v_ref = refs[4 * a:4 * a + 4]
            d_ref, mo_ref, vo_ref = refs[4 * n + 3 * a:4 * n + 3 * a + 3]
            d, mm, vv = _adamw_math(w_ref[...], g_ref[...], m_ref[...], v_ref[...])
            d_ref[...] = d
            mo_ref[...] = mm
            vo_ref[...] = vv

    args, outs = [], []
    for a in range(n):
        args += [ws[a], gs[a], ms[a], vs[a]]
        outs += [jax.ShapeDtypeStruct(ws[a].shape, F32)] * 3
    res = pl.pallas_call(body, name="adamw_small", out_shape=tuple(outs), compiler_params=_cp())(*args)
    return [res[3 * a:3 * a + 3] for a in range(n)]


def _flat_pack(arrs, rows):
    flat = jnp.concatenate([a.reshape(-1) for a in arrs])
    return jnp.pad(flat, (0, rows * D - flat.shape[0])).reshape(rows, D)


def _flat_unpack(flat, shapes):
    out, off = [], 0
    for shp in shapes:
        size = 1
        for d_ in shp:
            size *= d_
        out.append(flat[off:off + size].reshape(shp))
        off += size
    return out


SMALL_EVEN = ("even_pre_g", "even_a_ln_g", "even_a_ln_b", "even_a_ws", "even_a_bs", "even_b_conv", "even_mem_g",
              "even_post_g")
SMALL_ODD = ("odd_pre_g", "odd_c_wgrp", "odd_c_scale", "odd_d_dw_w", "odd_d_dw_b", "odd_d_ln_g", "odd_d_ln_b",
             "odd_d_pw_b", "odd_mem_g", "odd_post_g")
BIG = ("even_w_in", "even_w_kv", "even_w_out", "odd_w_in", "odd_d_pw_w", "odd_w_kv", "odd_w_out")
WEIGHTS = ("even_pre_g", "even_w_in", "even_a_ln_g", "even_a_ln_b", "even_a_ws", "even_a_bs", "even_b_conv",
           "even_mem_g", "even_w_kv", "even_w_out", "even_post_g", "odd_pre_g", "odd_w_in", "odd_c_wgrp",
           "odd_c_scale", "odd_d_dw_w", "odd_d_dw_b", "odd_d_ln_g", "odd_d_ln_b", "odd_d_pw_w", "odd_d_pw_b",
           "odd_mem_g", "odd_w_kv", "odd_w_out", "odd_post_g")
PACKED = (("even_b_conv", (3, 192)), ("odd_pre_g", (1, 256)), ("odd_c_scale", (1, 192)), ("odd_d_dw_w", (31, 192)),
          ("odd_d_dw_b", (1, 192)), ("odd_d_ln_g", (1, 192)), ("odd_d_ln_b", (1, 192)), ("odd_d_pw_b", (1, 192)),
          ("odd_mem_g", (1, 256)), ("odd_post_g", (1, 256)))
PACK_ROWS = 16
SMALL_ROWS = 256


def _local_step(x, mem, target, wt):
    tril = jnp.tril(jnp.ones((CH, CH), dtype=bool))
    wcat = jnp.where(tril[None], wt["even_a_ws"], 0.0).transpose(1, 0, 2).reshape(CH, 4 * CH).astype(BF16)
    bsg = jnp.repeat(wt["even_a_bs"].T, BW // 4, axis=1)
    hsel = (jnp.arange(BW)[:, None] // (BW // 4) == jnp.arange(128)[None, :]).astype(BF16)
    wg = wt["odd_c_wgrp"]
    g4 = BW // 4
    wbd = jnp.zeros((BW, BW), F32)
    for g in range(4):
        wbd = lax.dynamic_update_slice(wbd, wg[g], (g * g4, g * g4))
    wbd = wbd.astype(BF16)

    kv_e = _kv_fwd(mem, wt["even_mem_g"], wt["even_w_kv"], "even_kv")
    kv_o = _kv_fwd(mem, wt["odd_mem_g"], wt["odd_w_kv"], "odd_kv")
    p_e, h_e = _in_fwd(x, wt["even_pre_g"], wt["even_w_in_t"], "even_in")
    x1, o_e = _even_fwd(x, p_e, kv_e, wt["even_a_ln_g"], wt["even_a_ln_b"], wcat, bsg, wt["even_b_conv"],
                        wt["even_w_out"], wt["even_post_g"])
    p_o, h_o = _in_fwd(x1, wt["odd_pre_g"], wt["odd_w_in_t"], "odd_in")
    dx2, o_o, loss = _odd_fwd(x1, p_o, kv_o, wbd, wt["odd_c_scale"], wt["odd_d_dw_w"], wt["odd_d_dw_b"],
                              wt["odd_d_ln_g"], wt["odd_d_ln_b"], wt["odd_d_pw_w"], wt["odd_d_pw_b"],
                              wt["odd_w_out"], wt["odd_post_g"], target)
    (dpc_o, tmpc, tmpd, do_o, y_o, g_post_o, g_cs, g_wbd, g_dww, g_dwb, g_lng_o, g_lnb_o, g_pww, g_pwb,
     dkv_o) = _odd_bwd1(dx2, o_o, p_o, kv_o, wbd, wt["odd_c_scale"], wt["odd_d_dw_w"], wt["odd_d_dw_b"],
                        wt["odd_d_ln_g"], wt["odd_d_ln_b"], wt["odd_d_pw_w"], wt["odd_d_pw_b"], wt["odd_w_out"],
                        wt["odd_post_g"])
    dpb_o, dx1, g_pre_o = _odd_bwd2(dpc_o, tmpc, tmpd, p_o, wt["odd_d_dw_w"], wt["odd_w_in_t"], x1,
                                    wt["odd_pre_g"], dx2)
    g_win_o = _grad_tn(dpb_o, h_o, 768, rows=ODD_IN, name="odd_gw_in_b")
    g_win_o = _grad_tn(dpc_o, h_o, 256, out=g_win_o, rows=ODD_IN, row0=3 * BW, name="odd_gw_in_c")
    g_wout_o = _grad_tn(y_o, do_o, 1024, name="odd_gw_out")
    g_wkv_o, g_memg_o = _kv_bwd(mem, wt["odd_mem_g"], wt["odd_w_kv"], dkv_o, "odd_kv_bwd")
    (dpa_e, dpc_e, tmp_e, do_e, y_e, g_post_e, g_lng_e, g_lnb_e, g_wcat, g_bs, g_bconv,
     dkv_e) = _even_bwd1(dx1, o_e, p_e, kv_e, wt["even_a_ln_g"], wt["even_a_ln_b"], wcat, bsg, hsel,
                         wt["even_b_conv"], wt["even_w_out"], wt["even_post_g"])
    dpb_e, dx0, g_pre_e = _even_bwd2(dpa_e, dpc_e, tmp_e, p_e, wt["even_b_conv"], wt["even_w_in_t"], x,
                                     wt["even_pre_g"], dx1)
    g_win_e = _grad_tn(dpa_e, h_e, 768, rows=EVEN_IN, name="even_gw_in_a")
    g_win_e = _grad_tn(dpb_e, h_e, 768, out=g_win_e, rows=EVEN_IN, row0=3 * BW, name="even_gw_in_b")
    g_win_e = _grad_tn(dpc_e, h_e, 1280, out=g_win_e, rows=EVEN_IN, row0=5 * BW, name="even_gw_in_c")
    g_wout_e = _grad_tn(y_e, do_e, 1024, name="even_gw_out")
    g_wkv_e, g_memg_e = _kv_bwd(mem, wt["even_mem_g"], wt["even_w_kv"], dkv_e, "even_kv_bwd")

    g_aws = jnp.where(tril[None], g_wcat.reshape(CH, 4, CH).transpose(1, 0, 2), 0.0)
    g_wgrp = jnp.stack([lax.dynamic_slice(g_wbd, (g * g4, g * g4), (g4, g4)) for g in range(4)])
    small = {
        "even_pre_g": g_pre_e, "even_a_ln_g": g_lng_e, "even_a_ln_b": g_lnb_e, "even_a_ws": g_aws,
        "even_a_bs": g_bs[:, 0:4].T, "even_b_conv": g_bconv[0:3], "even_mem_g": g_memg_e, "even_post_g": g_post_e,
        "odd_pre_g": g_pre_o, "odd_c_wgrp": g_wgrp, "odd_c_scale": g_cs, "odd_d_dw_w": g_dww[0:CONF],
        "odd_d_dw_b": g_dwb, "odd_d_ln_g": g_lng_o, "odd_d_ln_b": g_lnb_o, "odd_d_pw_b": g_pwb,
        "odd_mem_g": g_memg_o, "odd_post_g": g_post_o,
    }
    big = {"even_w_in": g_win_e, "even_w_kv": g_wkv_e, "even_w_out": g_wout_e, "odd_w_in": g_win_o,
           "odd_d_pw_w": g_pww.astype(BF16), "odd_w_kv": g_wkv_o, "odd_w_out": g_wout_o}
    return loss[0, 0], dx0, big, small


def kernel(x, mem, even_pre_g, even_w_in, even_a_ln_g, even_a_ln_b, even_a_ws, even_a_bs, even_b_conv, even_mem_g, even_w_kv, even_w_out, even_post_g, odd_pre_g, odd_w_in, odd_c_wgrp, odd_c_scale, odd_d_dw_w, odd_d_dw_b, odd_d_ln_g, odd_d_ln_b, odd_d_pw_w, odd_d_pw_b, odd_mem_g, odd_w_kv, odd_w_out, odd_post_g, loss_target, m_even_pre_g, m_even_w_in, m_even_a_ln_g, m_even_a_ln_b, m_even_a_ws, m_even_a_bs, m_even_b_conv, m_even_mem_g, m_even_w_kv, m_even_w_out, m_even_post_g, m_odd_pre_g, m_odd_w_in, m_odd_c_wgrp, m_odd_c_scale, m_odd_d_dw_w, m_odd_d_dw_b, m_odd_d_ln_g, m_odd_d_ln_b, m_odd_d_pw_w, m_odd_d_pw_b, m_odd_mem_g, m_odd_w_kv, m_odd_w_out, m_odd_post_g, v_even_pre_g, v_even_w_in, v_even_a_ln_g, v_even_a_ln_b, v_even_a_ws, v_even_a_bs, v_even_b_conv, v_even_mem_g, v_even_w_kv, v_even_w_out, v_even_post_g, v_odd_pre_g, v_odd_w_in, v_odd_c_wgrp, v_odd_c_scale, v_odd_d_dw_w, v_odd_d_dw_b, v_odd_d_ln_g, v_odd_d_ln_b, v_odd_d_pw_w, v_odd_d_pw_b, v_odd_mem_g, v_odd_w_kv, v_odd_w_out, v_odd_post_g):
    given = dict(locals())
    w = {n: given[n] for n in WEIGHTS}
    mom = {n: given["m_" + n] for n in WEIGHTS}
    var = {n: given["v_" + n] for n in WEIGHTS}

    x_, y_, c_ = lax.axis_index("x"), lax.axis_index("y"), lax.axis_index("c")
    chip = 2 * x_ + y_
    place = jnp.stack([c_, chip]).astype(jnp.int32)
    pack = _flat_pack([w[n][0] for n, _ in PACKED], PACK_ROWS)
    shards = [w["even_w_in"][0].T, w["odd_w_in"][0].T, w["even_w_kv"][0], w["odd_w_kv"][0], w["even_w_out"][0],
              w["odd_w_out"][0], w["odd_d_pw_w"][0]]
    fulls = [_place_shard(a, place, BF16, "place_%d" % j) for j, a in enumerate(shards)]
    fulls.append(_place_shard(pack, place, F32, "place_pack"))
    win_e, win_o, wkv_e, wkv_o, wout_e, wout_o, pww, packs = _gather_weights(fulls)
    packs = packs.reshape(N_CHIPS, PACK_ROWS * D)
    wt = {"even_w_in_t": win_e, "odd_w_in_t": win_o, "even_w_kv": wkv_e, "odd_w_kv": wkv_o, "even_w_out": wout_e,
          "odd_w_out": wout_o, "odd_d_pw_w": pww}
    per_chip = [_flat_unpack(packs[k], [shp for _, shp in PACKED]) for k in range(N_CHIPS)]
    for a, (name, _) in enumerate(PACKED):
        wt[name] = jnp.concatenate([per_chip[k][a] for k in range(N_CHIPS)], axis=-1)
    for name in ("even_pre_g", "even_a_ln_g", "even_a_ln_b", "even_mem_g", "even_post_g"):
        wt[name] = w[name]
    wt["even_a_ws"] = w["even_a_ws"][0]
    wt["even_a_bs"] = w["even_a_bs"][0]
    wt["odd_c_wgrp"] = w["odd_c_wgrp"][0]

    loss, grad_x, big, small = _local_step(x[0], mem[0], loss_target[0], wt)

    names = SMALL_EVEN + SMALL_ODD
    small_pack = _flat_pack([small[n] for n in names] + [loss.reshape(1)], SMALL_ROWS)
    g4 = [big[n].reshape(N_CHIPS, 2, big[n].shape[0] // 8, big[n].shape[1]) for n in BIG]
    recv = _swap_halves(g4, small_pack)
    sums = [_pair_sum(g4[a], recv[a], place, "pair_sum_%d" % a) for a in range(len(BIG))]
    small_sum = _small_sum(small_pack, recv[-1])
    parts = _exchange_chips(sums, small_sum)
    halves = [_chip_sum(sums[a], parts[a], place, "chip_sum_%d" % a) for a in range(len(BIG))]
    halves.append(_chip_sum(None, parts[-1], place, "chip_sum_small"))
    full = _share_halves(halves)
    gbig = {n: full[a].reshape(full[a].shape[1] * 2, full[a].shape[2]) for a, n in enumerate(BIG)}
    gsmall_flat = full[-1].reshape(-1)

    grads = {}
    full_shapes = [small[n].shape for n in names] + [(1,)]
    unpacked = _flat_unpack(gsmall_flat, full_shapes)
    loss = unpacked[-1][0]
    for n, g in zip(names, unpacked[:-1]):
        shard_shape = w[n].shape[1:]
        if g.shape[-1] != shard_shape[-1]:
            g = lax.dynamic_slice_in_dim(g, chip * shard_shape[-1], shard_shape[-1], axis=g.ndim - 1)
        grads[n] = g.reshape(shard_shape)

    def two_d(a):
        return a.reshape(-1, a.shape[-1])

    upd = {}
    for n in BIG:
        if n.endswith("w_in"):
            res = _adamw_big(w[n][0].T, gbig[n], mom[n][0].T, var[n][0].T, "adamw_" + n)
            grads[n] = gbig[n].T
            upd[n] = tuple(r.T for r in res)
        else:
            grads[n] = gbig[n]
            upd[n] = _adamw_big(w[n][0], gbig[n], mom[n][0], var[n][0], "adamw_" + n)
    res = _adamw_small([two_d(w[n][0]) for n in names], [two_d(grads[n]) for n in names],
                       [two_d(mom[n][0]) for n in names], [two_d(var[n][0]) for n in names])
    for n, r in zip(names, res):
        upd[n] = r

    outs = [loss, grad_x[None]]
    outs += [grads[n].reshape(w[n].shape) for n in WEIGHTS]
    for j in range(3):
        outs += [upd[n][j].reshape(w[n].shape) for n in WEIGHTS]
    return tuple(outs)
```

```python
import functools

import jax
import jax.numpy as jnp
from jax import lax
from jax.experimental import pallas as pl
from jax.experimental.pallas import tpu as pltpu

F32 = jnp.float32
BF16 = jnp.bfloat16
MESH = pl.DeviceIdType.MESH

D = 1024
N_MEM = 256
MIX = 2048
XA = 512
HD = 128
BW = 768
CH = 128
EPS = 1e-6
SCALE = HD ** -0.5
POOL_WINDOWS = (2, 4, 8, 16)
CONF = 31
EVEN_IN = 6400
ODD_IN = 4864
N_CHIPS = 4

ADAM_LR = 0.001
ADAM_B1 = 0.9
ADAM_B2 = 0.999
ADAM_EPS = 1e-08
ADAM_WD = 0.01
ADAM_STEP = 10

TS = 256
HALO = 32
VMEM_LIMIT = 56 * 1024 * 1024


def _cp(sem=None):
    return pltpu.CompilerParams(dimension_semantics=sem, vmem_limit_bytes=VMEM_LIMIT)


def _dot(a, b):
    return jnp.dot(a, b, preferred_element_type=F32)


def _dot_nt(a, b):
    return lax.dot_general(a, b, (((1,), (1,)), ((), ())), preferred_element_type=F32)


def _dot_tn(a, b):
    return lax.dot_general(a, b, (((0,), (0,)), ((), ())), preferred_element_type=F32)


def _sigmoid(x):
    return 1.0 / (1.0 + jnp.exp(-x))


def _resident(shape):
    return pl.BlockSpec(shape, lambda *_: (0,) * len(shape), pipeline_mode=pl.Buffered(1))


def _const(shape):
    return pl.BlockSpec(shape, lambda *_: (0,) * len(shape))


def _kv_fwd(mem, mem_g, wkv, name):
    def body(mem_ref, g_ref, w_ref, kv_ref):
        m = mem_ref[...]
        r = lax.rsqrt(jnp.mean(m * m, axis=-1, keepdims=True) + EPS)
        mn = (m * r * g_ref[...]).astype(BF16)
        kv_ref[...] = _dot(mn, w_ref[...]).astype(BF16)

    return pl.pallas_call(body, out_shape=jax.ShapeDtypeStruct((N_MEM, D), BF16), name=name,
                          compiler_params=_cp())(mem, mem_g, wkv)


def _kv_bwd(mem, mem_g, wkv, dkv, name):
    def body(mem_ref, g_ref, w_ref, dkv_ref, dw_ref, dg_ref):
        m = mem_ref[...]
        r = lax.rsqrt(jnp.mean(m * m, axis=-1, keepdims=True) + EPS)
        mh = m * r
        mn = (mh * g_ref[...]).astype(BF16)
        dkv = dkv_ref[...].astype(BF16)
        dw_ref[...] = _dot_tn(mn, dkv).astype(BF16)
        dmn = _dot_nt(dkv, w_ref[...])
        dg_ref[...] = jnp.sum(dmn * mh, axis=0, keepdims=True)

    return pl.pallas_call(body, out_shape=(jax.ShapeDtypeStruct((D, D), BF16), jax.ShapeDtypeStruct((1, D), F32)),
                          name=name, compiler_params=_cp())(mem, mem_g, wkv, dkv)


def _in_fwd(x, pre_g, w_t, name):
    s, n = x.shape[0], w_t.shape[0]
    tm = min(512, s)
    nc = 256

    def body(x_ref, g_ref, w_ref, p_ref, h_ref):
        xv = x_ref[...]
        r = lax.rsqrt(jnp.mean(xv * xv, axis=-1, keepdims=True) + EPS)
        h = (xv * r * g_ref[...]).astype(BF16)
        h_ref[...] = h
        for j in range(n // nc):
            p_ref[:, j * nc:(j + 1) * nc] = _dot_nt(h, w_ref[j * nc:(j + 1) * nc, :]).astype(BF16)

    return pl.pallas_call(
        body, grid=(s // tm,), name=name,
        out_shape=(jax.ShapeDtypeStruct((s, n), BF16), jax.ShapeDtypeStruct((s, D), BF16)),
        in_specs=[pl.BlockSpec((tm, D), lambda i: (i, 0)), _const((1, D)), _resident((n, D))],
        out_specs=(pl.BlockSpec((tm, n), lambda i: (i, 0)), pl.BlockSpec((tm, D), lambda i: (i, 0))),
        compiler_params=_cp(("arbitrary",)),
    )(x, pre_g, w_t)


def _xattn_fwd(q, kv_ref):
    outs, probs = [], []
    for h in range(XA // HD):
        qh = q[:, h * HD:(h + 1) * HD]
        kh = kv_ref[:, h * HD:(h + 1) * HD]
        vh = kv_ref[:, XA + h * HD:XA + (h + 1) * HD]
        sc = _dot_nt(qh, kh) * SCALE
        e = jnp.exp(sc - jnp.max(sc, axis=-1, keepdims=True))
        pr = e / jnp.sum(e, axis=-1, keepdims=True)
        outs.append(_dot(pr.astype(BF16), vh))
        probs.append(pr)
    return jnp.concatenate(outs, axis=-1), probs


def _xattn_bwd(dyx, q, probs, kv_ref, dkv_ref):
    dqs = []
    for h in range(XA // HD):
        qh = q[:, h * HD:(h + 1) * HD]
        kh = kv_ref[:, h * HD:(h + 1) * HD]
        vh = kv_ref[:, XA + h * HD:XA + (h + 1) * HD]
        dy = dyx[:, h * HD:(h + 1) * HD].astype(BF16)
        pr = probs[h]
        dp = _dot_nt(dy, vh)
        ds = (pr * (dp - jnp.sum(dp * pr, axis=-1, keepdims=True))).astype(BF16)
        dqs.append(_dot(ds, kh) * SCALE)
        dkv_ref[:, h * HD:(h + 1) * HD] += _dot_tn(ds, qh) * SCALE
        dkv_ref[:, XA + h * HD:XA + (h + 1) * HD] += _dot_tn(pr.astype(BF16), dy)
    return jnp.concatenate(dqs, axis=-1)


def _layer_norm_fwd(v, g, b):
    mu = jnp.mean(v, axis=-1, keepdims=True)
    vc = v - mu
    rstd = lax.rsqrt(jnp.mean(vc * vc, axis=-1, keepdims=True) + EPS)
    vhat = vc * rstd
    return vhat * g + b, vhat, rstd


def _layer_norm_bwd(dy, vhat, rstd, g):
    dvh = dy * g
    return rstd * (dvh - jnp.mean(dvh, axis=-1, keepdims=True) - vhat * jnp.mean(dvh * vhat, axis=-1, keepdims=True))


def _head_masks():
    col = lax.broadcasted_iota(jnp.int32, (1, BW), 1)
    return [(col >= h * (BW // 4)) & (col < (h + 1) * (BW // 4)) for h in range(4)]


def _halo_prev(nblk_per_tile):
    return lambda i: (jnp.maximum(i * nblk_per_tile - 1, 0), 0)


def _row_ids(i, t):
    return i * t + lax.broadcasted_iota(jnp.int32, (t, 1), 0)


def _even_mix(i, p_ref, ph_ref, ln_g, ln_b, wcat_ref, bsg_ref, bconv_ref, wbuf):
    t = p_ref.shape[0]
    u = p_ref[:, 0:BW].astype(F32)
    v = p_ref[:, BW:2 * BW].astype(F32)
    bg = p_ref[:, 2 * BW:3 * BW].astype(F32)
    cg = p_ref[:, 3 * BW:4 * BW].astype(F32)
    xin = p_ref[:, 4 * BW:5 * BW].astype(F32)
    vn, vhat, rstd = _layer_norm_fwd(v, ln_g, ln_b)
    masks = _head_masks()
    sgs, vsts = [], []
    for n in range(t // CH):
        vn_c = vn[n * CH:(n + 1) * CH]
        vst = jnp.concatenate([jnp.where(m, vn_c, 0.0) for m in masks], axis=0).astype(BF16)
        sgs.append(_dot(wcat_ref[...], vst) + bsg_ref[...])
        vsts.append(vst)
    sg = jnp.concatenate(sgs, axis=0)
    ya = u * sg
    w_halo = ph_ref[:, 3 * BW:4 * BW].astype(F32) * ph_ref[:, 4 * BW:5 * BW].astype(F32)
    wbuf[0:HALO, :] = jnp.where(i > 0, w_halo, 0.0)
    wbuf[HALO:HALO + t, :] = cg * xin
    conv = (bconv_ref[0:1, :] * wbuf[pl.ds(HALO - 2, t), :] + bconv_ref[1:2, :] * wbuf[pl.ds(HALO - 1, t), :]
            + bconv_ref[2:3, :] * wbuf[pl.ds(HALO, t), :])
    yb = bg * conv
    return dict(u=u, bg=bg, vhat=vhat, rstd=rstd, sg=sg, vsts=vsts, conv=conv, ya=ya, yb=yb, masks=masks)


def _pool_select(vals):
    col = lax.broadcasted_iota(jnp.int32, (1, BW), 1)
    g = BW // 4
    return jnp.where(col < g, vals[0], jnp.where(col < 2 * g, vals[1], jnp.where(col < 3 * g, vals[2], vals[3])))


def _inv_counts(i, t):
    rows = _row_ids(i, t) + 1
    return [1.0 / jnp.minimum(rows, w).astype(F32) for w in POOL_WINDOWS]


def _band_matrices(t, forward):
    j = jnp.arange(t)[:, None]
    r = jnp.arange(HALO + t)[None, :]
    if forward:
        return jnp.stack([(r >= j) & (r < j + w) for w in POOL_WINDOWS]).astype(BF16)
    return jnp.stack([(r <= HALO + j) & (r > HALO + j - w) for w in POOL_WINDOWS]).astype(BF16)


SHIFT_ROWS = HALO + TS - 8


def _shifted_copies(buf, sh):
    for b in range(1, 8):
        sh[b - 1] = buf[pl.ds(b, SHIFT_ROWS), :]


def _rows_at(buf, sh, off, t):
    a, b = divmod(off, 8)
    return buf[pl.ds(8 * a, t), :] if b == 0 else sh[b - 1, pl.ds(8 * a, t), :]


def _tap_sums(d_ref, buf, sh, base, out_ref):
    t = d_ref.shape[0]
    group = 4
    for k0 in range(0, CONF, group):
        taps = list(range(k0, min(k0 + group, CONF)))

        def step(r, accs, taps=taps):
            row = pl.multiple_of(r * 8, 8)
            d = d_ref[pl.ds(row, 8), :]
            new = []
            for acc, k in zip(accs, taps):
                a, b = divmod(base + k, 8)
                src = buf[pl.ds(row + 8 * a, 8), :] if b == 0 else sh[b - 1, pl.ds(row + 8 * a, 8), :]
                new.append(acc + d * src)
            return tuple(new)

        accs = lax.fori_loop(0, t // 8, step, tuple(jnp.zeros((8, BW), F32) for _ in taps), unroll=2)
        for acc, k in zip(accs, taps):
            out_ref[8 * k:8 * k + 8, :] += acc


def _odd_mix(i, p_ref, ph_ref, bands_ref, wbd_ref, cscale, dww_ref, dwb, ln_g, ln_b, pww_ref, pwb, gbuf, gsh,
             cv=None):
    t = p_ref.shape[0]
    zc_bf = p_ref[:, 0:BW]
    zc = zc_bf.astype(F32)
    ga = p_ref[:, BW:2 * BW].astype(F32)
    gb = p_ref[:, 2 * BW:3 * BW].astype(F32)
    zh = ph_ref[:, 0:BW]
    zcat = jnp.concatenate([jnp.where(i > 0, zh, jnp.zeros_like(zh)), zc_bf], axis=0)
    inv = _inv_counts(i, t)
    pooled = _pool_select([_dot(bands_ref[w], zcat) * inv[w] for w in range(len(POOL_WINDOWS))]) - zc
    pooled_bf = pooled.astype(BF16)
    pre = _dot(pooled_bf, wbd_ref[...])
    yc = pre * cscale
    sgb = _sigmoid(gb)
    z = ga * sgb
    gh_a = ph_ref[:, BW:2 * BW].astype(F32)
    gh_b = ph_ref[:, 2 * BW:3 * BW].astype(F32)
    gbuf[0:HALO, :] = jnp.where(i > 0, gh_a * _sigmoid(gh_b), 0.0)
    gbuf[HALO:HALO + t, :] = z
    _shifted_copies(gbuf, gsh)
    if cv is None:
        cv = dwb + dww_ref[CONF - 1:CONF, :] * z
        for k in range(CONF - 1):
            cv = cv + dww_ref[k:k + 1, :] * _rows_at(gbuf, gsh, HALO - (CONF - 1) + k, t)
    zl, zhat, rstd = _layer_norm_fwd(cv, ln_g, ln_b)
    szl = _sigmoid(zl)
    zs = (zl * szl).astype(BF16)
    yd = _dot(zs, pww_ref[...]) + pwb
    return dict(ga=ga, sgb=sgb, pooled_bf=pooled_bf, pre=pre, yc=yc, zhat=zhat, rstd=rstd, zl=zl, szl=szl,
                zs=zs, yd=yd, inv=inv, cv=cv)


def _post_norm(o, post_g):
    r = lax.rsqrt(jnp.mean(o * o, axis=-1, keepdims=True) + EPS)
    return o * r, r


def _gate_out(y_a, y_b, y_x, gate, wout_ref):
    sgt = _sigmoid(gate)
    sgate = gate * sgt
    ys = [(y_a * sgate[:, 0:BW]).astype(BF16), (y_b * sgate[:, BW:2 * BW]).astype(BF16),
          (y_x * sgate[:, 2 * BW:MIX]).astype(BF16)]
    o = (_dot(ys[0], wout_ref[0:BW, :]) + _dot(ys[1], wout_ref[BW:2 * BW, :]) + _dot(ys[2], wout_ref[2 * BW:MIX, :]))
    return o, ys, sgt, sgate


def _tile_specs(s, n):
    nh = TS // HALO
    return pl.BlockSpec((TS, n), lambda i: (i, 0)), pl.BlockSpec((HALO, n), _halo_prev(nh))


def _even_fwd(x, p, kv, ln_g, ln_b, wcat, bsg, bconv, wout, post_g):
    s = x.shape[0]

    def body(x_ref, p_ref, ph_ref, kv_ref, lng, lnb, wcat_ref, bsg_ref, bconv_ref, wout_ref, pg, x1_ref, o_ref, wbuf):
        i = pl.program_id(0)
        mx = _even_mix(i, p_ref, ph_ref, lng[...], lnb[...], wcat_ref, bsg_ref, bconv_ref, wbuf)
        yx, _ = _xattn_fwd(p_ref[:, 5 * BW:5 * BW + XA], kv_ref)
        gate = p_ref[:, 5 * BW + XA:EVEN_IN].astype(F32)
        o, _, _, _ = _gate_out(mx["ya"], mx["yb"], yx, gate, wout_ref)
        n, _ = _post_norm(o, pg[...])
        o_ref[...] = o
        x1_ref[...] = x_ref[...] + n * pg[...]

    tile, halo = _tile_specs(s, EVEN_IN)
    row = pl.BlockSpec((TS, D), lambda i: (i, 0))
    return pl.pallas_call(
        body, grid=(s // TS,), name="even_fwd",
        out_shape=(jax.ShapeDtypeStruct((s, D), F32), jax.ShapeDtypeStruct((s, D), F32)),
        in_specs=[row, tile, halo, _const((N_MEM, D)), _const((1, BW)), _const((1, BW)), _const((CH, 4 * CH)),
                  _const((CH, BW)), _const((3, BW)), _resident((MIX, D)), _const((1, D))],
        out_specs=(row, row),
        scratch_shapes=[pltpu.VMEM((HALO + TS, BW), F32)],
        compiler_params=_cp(("arbitrary",)),
    )(x, p, p, kv, ln_g, ln_b, wcat, bsg, bconv, wout, post_g)


def _odd_fwd(x1, p, kv, wbd, cscale, dww, dwb, ln_g, ln_b, pww, pwb, wout, post_g, target):
    s = x1.shape[0]

    def body(x_ref, p_ref, ph_ref, kv_ref, bands_ref, wbd_ref, cs, dww_ref, dwb_ref, lng, lnb, pww_ref, pwb_ref,
             wout_ref, pg, tgt_ref, dx_ref, o_ref, cv_ref, loss_ref, gbuf, gsh):
        i = pl.program_id(0)
        mx = _odd_mix(i, p_ref, ph_ref, bands_ref, wbd_ref, cs[...], dww_ref, dwb_ref[...], lng[...], lnb[...],
                      pww_ref, pwb_ref[...], gbuf, gsh)
        cv_ref[...] = mx["cv"]
        yx, _ = _xattn_fwd(p_ref[:, 3 * BW:3 * BW + XA], kv_ref)
        gate = p_ref[:, 3 * BW + XA:ODD_IN].astype(F32)
        o, _, _, _ = _gate_out(mx["yc"], mx["yd"], yx, gate, wout_ref)
        n, _ = _post_norm(o, pg[...])
        o_ref[...] = o
        err = x_ref[...] + n * pg[...] - tgt_ref[...]
        dx_ref[...] = err * (1.0 / D)

        @pl.when(i == 0)
        def _():
            loss_ref[...] = jnp.zeros_like(loss_ref)

        loss_ref[...] += 0.5 * jnp.sum(jnp.sum(err * err, axis=-1, keepdims=True) * (1.0 / D), axis=0, keepdims=True)

    tile, halo = _tile_specs(s, ODD_IN)
    row = pl.BlockSpec((TS, D), lambda i: (i, 0))
    vec = _const((1, BW))
    return pl.pallas_call(
        body, grid=(s // TS,), name="odd_fwd",
        out_shape=(jax.ShapeDtypeStruct((s, D), F32), jax.ShapeDtypeStruct((s, D), F32),
                   jax.ShapeDtypeStruct((s, BW), F32), jax.ShapeDtypeStruct((8, 128), F32)),
        in_specs=[row, tile, halo, _const((N_MEM, D)), _const((4, TS, HALO + TS)), _const((BW, BW)), vec,
                  _const((CONF, BW)), vec, vec, vec, _const((BW, BW)), vec, _resident((MIX, D)), _const((1, D)), row],
        out_specs=(row, row, pl.BlockSpec((TS, BW), lambda i: (i, 0)), _const((8, 128))),
        scratch_shapes=[pltpu.VMEM((HALO + TS, BW), F32), pltpu.VMEM((7, SHIFT_ROWS, BW), F32)],
        compiler_params=_cp(("arbitrary",)),
    )(x1, p, p, kv, _band_matrices(TS, False), wbd, cscale, dww, dwb, ln_g, ln_b, pww, pwb, wout, post_g, target)


def _acc_init(i, refs):
    @pl.when(i == 0)
    def _():
        for r in refs:
            r[...] = jnp.zeros_like(r)


def _post_norm_bwd(dx, o, pg, dpg_ref):
    n, r = _post_norm(o, pg)
    dpg_ref[...] += jnp.sum(dx * n, axis=0, keepdims=True)
    dn = dx * pg
    return (r * (dn - n * jnp.mean(dn * n, axis=-1, keepdims=True))).astype(BF16)


def _gate_bwd(do, wout_ref, ys_f32, gate, y_ref):
    dy = _dot_nt(do, wout_ref[...])
    sgt = _sigmoid(gate)
    sgate = gate * sgt
    dsilu = sgt * (1.0 + gate * (1.0 - sgt))
    offs = (0, BW, 2 * BW, MIX)
    dys, dgs = [], []
    for j, yv in enumerate(ys_f32):
        a, b = offs[j], offs[j + 1]
        y_ref[:, a:b] = (yv * sgate[:, a:b]).astype(BF16)
        dys.append(dy[:, a:b] * sgate[:, a:b])
        dgs.append(dy[:, a:b] * yv * dsilu[:, a:b])
    return dys, jnp.concatenate(dgs, axis=-1)


def _even_bwd1(dx, o, p, kv, ln_g, ln_b, wcat, bsg, hsel, bconv, wout, post_g):
    s = dx.shape[0]

    def body(dx_ref, o_ref, p_ref, ph_ref, kv_ref, lng, lnb, wcat_ref, bsg_ref, hsel_ref, bconv_ref, wout_ref, pg,
             dpa_ref, dpc_ref, tmp_ref, do_ref, y_ref, dpg_ref, dlng_ref, dlnb_ref, dwcat_ref, dbs_ref, dbconv_ref,
             dkv_ref, wbuf):
        i = pl.program_id(0)
        _acc_init(i, (dpg_ref, dlng_ref, dlnb_ref, dwcat_ref, dbs_ref, dbconv_ref, dkv_ref))
        mx = _even_mix(i, p_ref, ph_ref, lng[...], lnb[...], wcat_ref, bsg_ref, bconv_ref, wbuf)
        q = p_ref[:, 5 * BW:5 * BW + XA]
        yx, probs = _xattn_fwd(q, kv_ref)
        gate = p_ref[:, 5 * BW + XA:EVEN_IN].astype(F32)
        do = _post_norm_bwd(dx_ref[...], o_ref[...], pg[...], dpg_ref)
        do_ref[...] = do
        (dya, dyb, dyx), dgate = _gate_bwd(do, wout_ref, (mx["ya"], mx["yb"], yx), gate, y_ref)
        dpa_ref[:, 0:BW] = (dya * mx["sg"]).astype(BF16)
        dsg = (dya * mx["u"]).astype(BF16)
        dvns = []
        for n in range(TS // CH):
            dsg_c = dsg[n * CH:(n + 1) * CH]
            dvst = _dot_tn(wcat_ref[...], dsg_c)
            dvn_c = jnp.where(mx["masks"][0], dvst[0:CH], 0.0)
            for h in range(1, 4):
                dvn_c = dvn_c + jnp.where(mx["masks"][h], dvst[h * CH:(h + 1) * CH], 0.0)
            dvns.append(dvn_c)
            dwcat_ref[...] += _dot_nt(dsg_c, mx["vsts"][n])
            dbs_ref[...] += _dot(dsg_c, hsel_ref[...])
        dvn = jnp.concatenate(dvns, axis=0)
        dlng_ref[...] += jnp.sum(dvn * mx["vhat"], axis=0, keepdims=True)
        dlnb_ref[...] += jnp.sum(dvn, axis=0, keepdims=True)
        dpa_ref[:, BW:2 * BW] = _layer_norm_bwd(dvn, mx["vhat"], mx["rstd"], lng[...]).astype(BF16)
        dpa_ref[:, 2 * BW:3 * BW] = (dyb * mx["conv"]).astype(BF16)
        dconv = dyb * mx["bg"]
        tmp_ref[...] = dconv.astype(BF16)
        for k in range(3):
            dbconv_ref[k:k + 1, :] += jnp.sum(dconv * wbuf[pl.ds(HALO - 2 + k, TS), :], axis=0, keepdims=True)
        dpc_ref[:, 0:XA] = _xattn_bwd(dyx, q, probs, kv_ref, dkv_ref).astype(BF16)
        dpc_ref[:, XA:XA + MIX] = dgate.astype(BF16)

    tile, halo = _tile_specs(s, EVEN_IN)
    row = pl.BlockSpec((TS, D), lambda i: (i, 0))
    vec = _const((1, BW))

    def out(n):
        return pl.BlockSpec((TS, n), lambda i: (i, 0))

    return pl.pallas_call(
        body, grid=(s // TS,), name="even_bwd1",
        out_shape=(jax.ShapeDtypeStruct((s, 3 * BW), BF16), jax.ShapeDtypeStruct((s, XA + MIX), BF16),
                   jax.ShapeDtypeStruct((s, BW), BF16), jax.ShapeDtypeStruct((s, D), BF16),
                   jax.ShapeDtypeStruct((s, MIX), BF16),
                   jax.ShapeDtypeStruct((1, D), F32), jax.ShapeDtypeStruct((1, BW), F32),
                   jax.ShapeDtypeStruct((1, BW), F32), jax.ShapeDtypeStruct((CH, 4 * CH), F32),
                   jax.ShapeDtypeStruct((CH, 128), F32), jax.ShapeDtypeStruct((8, BW), F32),
                   jax.ShapeDtypeStruct((N_MEM, D), F32)),
        in_specs=[row, row, tile, halo, _const((N_MEM, D)), vec, vec, _const((CH, 4 * CH)), _const((CH, BW)),
                  _const((BW, 128)), _const((3, BW)), _resident((MIX, D)), _const((1, D))],
        out_specs=(out(3 * BW), out(XA + MIX), out(BW), out(D), out(MIX),
                   _const((1, D)), vec, vec, _const((CH, 4 * CH)), _const((CH, 128)), _const((8, BW)),
                   _const((N_MEM, D))),
        scratch_shapes=[pltpu.VMEM((HALO + TS, BW), F32)],
        compiler_params=_cp(("arbitrary",)),
    )(dx, o, p, p, kv, ln_g, ln_b, wcat, bsg, hsel, bconv, wout, post_g)


def _odd_bwd1(dx, o, cv, p, kv, wbd, cscale, dww, dwb, ln_g, ln_b, pww, pwb, wout, post_g):
    s = dx.shape[0]

    def body(dx_ref, o_ref, cv_ref, p_ref, ph_ref, kv_ref, bands_ref, wbd_ref, cs, dww_ref, dwb_ref, lng, lnb,
             pww_ref, pwb_ref, wout_ref, pg,
             dpc_ref, tmpc_ref, tmpd_ref, do_ref, y_ref, dpg_ref, dcs_ref, dwbd_ref, ddww_ref, ddwb_ref, dlng_ref,
             dlnb_ref, dpww_ref, dpwb_ref, dkv_ref, gbuf, gsh, dcv_buf):
        i = pl.program_id(0)
        _acc_init(i, (dpg_ref, dcs_ref, dwbd_ref, ddww_ref, ddwb_ref, dlng_ref, dlnb_ref, dpww_ref, dpwb_ref,
                      dkv_ref))
        mx = _odd_mix(i, p_ref, ph_ref, bands_ref, wbd_ref, cs[...], dww_ref, dwb_ref[...], lng[...], lnb[...],
                      pww_ref, pwb_ref[...], gbuf, gsh, cv=cv_ref[...])
        q = p_ref[:, 3 * BW:3 * BW + XA]
        yx, probs = _xattn_fwd(q, kv_ref)
        gate = p_ref[:, 3 * BW + XA:ODD_IN].astype(F32)
        do = _post_norm_bwd(dx_ref[...], o_ref[...], pg[...], dpg_ref)
        do_ref[...] = do
        (dyc, dyd, dyx), dgate = _gate_bwd(do, wout_ref, (mx["yc"], mx["yd"], yx), gate, y_ref)
        dcs_ref[...] += jnp.sum(dyc * mx["pre"], axis=0, keepdims=True)
        dpre = (dyc * cs[...]).astype(BF16)
        dwbd_ref[...] += _dot_tn(mx["pooled_bf"], dpre)
        dpooled = _dot_nt(dpre, wbd_ref[...])
        tmpc_ref[...] = _pool_select([dpooled * c_ for c_ in mx["inv"]]).astype(BF16)
        dyd_bf = dyd.astype(BF16)
        dpwb_ref[...] += jnp.sum(dyd, axis=0, keepdims=True)
        dpww_ref[...] += _dot_tn(mx["zs"], dyd_bf)
        dzs = _dot_nt(dyd_bf, pww_ref[...])
        zl, szl = mx["zl"], mx["szl"]
        dzl = dzs * (szl * (1.0 + zl * (1.0 - szl)))
        dlng_ref[...] += jnp.sum(dzl * mx["zhat"], axis=0, keepdims=True)
        dlnb_ref[...] += jnp.sum(dzl, axis=0, keepdims=True)
        dcv = _layer_norm_bwd(dzl, mx["zhat"], mx["rstd"], lng[...])
        tmpd_ref[...] = dcv.astype(BF16)
        ddwb_ref[...] += jnp.sum(dcv, axis=0, keepdims=True)
        dcv_buf[...] = dcv
        _tap_sums(dcv_buf, gbuf, gsh, HALO - (CONF - 1), ddww_ref)
        dpc_ref[:, 0:XA] = _xattn_bwd(dyx, q, probs, kv_ref, dkv_ref).astype(BF16)
        dpc_ref[:, XA:XA + MIX] = dgate.astype(BF16)

    tile, halo = _tile_specs(s, ODD_IN)
    row = pl.BlockSpec((TS, D), lambda i: (i, 0))
    vec = _const((1, BW))

    def out(n):
        return pl.BlockSpec((TS, n), lambda i: (i, 0))

    return pl.pallas_call(
        body, grid=(s // TS,), name="odd_bwd1",
        out_shape=(jax.ShapeDtypeStruct((s, XA + MIX), BF16), jax.ShapeDtypeStruct((s, BW), BF16),
                   jax.ShapeDtypeStruct((s, BW), BF16), jax.ShapeDtypeStruct((s, D), BF16),
                   jax.ShapeDtypeStruct((s, MIX), BF16),
                   jax.ShapeDtypeStruct((1, D), F32), jax.ShapeDtypeStruct((1, BW), F32),
                   jax.ShapeDtypeStruct((BW, BW), F32), jax.ShapeDtypeStruct((8 * CONF, BW), F32),
                   jax.ShapeDtypeStruct((1, BW), F32), jax.ShapeDtypeStruct((1, BW), F32),
                   jax.ShapeDtypeStruct((1, BW), F32), jax.ShapeDtypeStruct((BW, BW), F32),
                   jax.ShapeDtypeStruct((1, BW), F32), jax.ShapeDtypeStruct((N_MEM, D), F32)),
        in_specs=[row, row, out(BW), tile, halo, _const((N_MEM, D)), _const((4, TS, HALO + TS)), _const((BW, BW)), vec,
                  _const((CONF, BW)), vec, vec, vec, _const((BW, BW)), vec, _resident((MIX, D)), _const((1, D))],
        out_specs=(out(XA + MIX), out(BW), out(BW), out(D), out(MIX),
                   _const((1, D)), vec, _const((BW, BW)), _const((8 * CONF, BW)), vec, vec, vec, _const((BW, BW)), vec,
                   _const((N_MEM, D))),
        scratch_shapes=[pltpu.VMEM((HALO + TS, BW), F32), pltpu.VMEM((7, SHIFT_ROWS, BW), F32),
                        pltpu.VMEM((TS, BW), F32)],
        compiler_params=_cp(("arbitrary",)),
    )(dx, o, cv, p, p, kv, _band_matrices(TS, False), wbd, cscale, dww, dwb, ln_g, ln_b, pww, pwb, wout, post_g)


def _halo_next(nblk_per_tile, nblk):
    return lambda i: (jnp.minimum((i + 1) * nblk_per_tile, nblk - 1), 0)


def _pre_norm_bwd(dh, x, pre_g, dres, dpre_ref):
    r = lax.rsqrt(jnp.mean(x * x, axis=-1, keepdims=True) + EPS)
    xh = x * r
    dpre_ref[...] += jnp.sum(dh * xh, axis=0, keepdims=True)
    dxh = dh * pre_g
    return dres + r * (dxh - xh * jnp.mean(dxh * xh, axis=-1, keepdims=True))


def _even_bwd2(dpa, dpc, tmp, p, bconv, w_t, x, pre_g, dres):
    s = x.shape[0]
    nt = s // TS

    def body(dpa_ref, dpc_ref, tmp_ref, tmph_ref, cg_ref, xin_ref, bconv_ref, w_ref, x_ref, pg, dres_ref,
             dpb_ref, dx_ref, dpre_ref, dbuf):
        i = pl.program_id(0)
        _acc_init(i, (dpre_ref,))
        dbuf[0:TS, :] = tmp_ref[...].astype(F32)
        dbuf[TS:TS + HALO, :] = jnp.where(i < nt - 1, tmph_ref[...].astype(F32), 0.0)
        dw = (bconv_ref[2:3, :] * dbuf[pl.ds(0, TS), :] + bconv_ref[1:2, :] * dbuf[pl.ds(1, TS), :]
              + bconv_ref[0:1, :] * dbuf[pl.ds(2, TS), :])
        dcg = (dw * xin_ref[...].astype(F32)).astype(BF16)
        dxin = (dw * cg_ref[...].astype(F32)).astype(BF16)
        dpb_ref[:, 0:BW] = dcg
        dpb_ref[:, BW:2 * BW] = dxin
        dh = (_dot(dpa_ref[...], w_ref[0:3 * BW, :]) + _dot(dcg, w_ref[3 * BW:4 * BW, :])
              + _dot(dxin, w_ref[4 * BW:5 * BW, :]) + _dot(dpc_ref[...], w_ref[5 * BW:EVEN_IN, :]))
        dx_ref[...] = _pre_norm_bwd(dh, x_ref[...], pg[...], dres_ref[...], dpre_ref)

    row = pl.BlockSpec((TS, D), lambda i: (i, 0))

    def tile(n, j=0):
        return pl.BlockSpec((TS, n), lambda i: (i, j))

    return pl.pallas_call(
        body, grid=(nt,), name="even_bwd2",
        out_shape=(jax.ShapeDtypeStruct((s, 2 * BW), BF16), jax.ShapeDtypeStruct((s, D), F32),
                   jax.ShapeDtypeStruct((1, D), F32)),
        in_specs=[tile(3 * BW), tile(XA + MIX), tile(BW), pl.BlockSpec((HALO, BW), _halo_next(TS // HALO, s // HALO)),
                  tile(BW, 3), tile(BW, 4), _const((3, BW)), _resident((EVEN_IN, D)), row, _const((1, D)), row],
        out_specs=(tile(2 * BW), row, _const((1, D))),
        scratch_shapes=[pltpu.VMEM((TS + HALO, BW), F32)],
        compiler_params=_cp(("arbitrary",)),
    )(dpa, dpc, tmp, tmp, p, p, bconv, w_t, x, pre_g, dres)


def _odd_bwd2(dpc, tmpc, tmpd, p, dww, w_t, x, pre_g, dres):
    s = x.shape[0]
    nt = s // TS

    def body(dpc_ref, tc_ref, tch_ref, td_ref, tdh_ref, ga_ref, gb_ref, bands_ref, dww_ref, w_ref, x_ref, pg,
             dres_ref, dpb_ref, dx_ref, dpre_ref, dbuf, dsh):
        i = pl.program_id(0)
        _acc_init(i, (dpre_ref,))
        more = i < nt - 1
        e_bf = tc_ref[...]
        eh = tch_ref[...]
        ecat = jnp.concatenate([e_bf, jnp.where(more, eh, jnp.zeros_like(eh))], axis=0)
        dbuf[0:TS, :] = td_ref[...].astype(F32)
        dbuf[TS:TS + HALO, :] = jnp.where(more, tdh_ref[...].astype(F32), 0.0)
        sums = [_dot(bands_ref[w], ecat) for w in range(len(POOL_WINDOWS))]
        rows = _row_ids(i, TS) + 1
        cnt = _pool_select([jnp.minimum(rows, w).astype(F32) for w in POOL_WINDOWS])
        dzc = (_pool_select(sums) - e_bf.astype(F32) * cnt).astype(BF16)
        _shifted_copies(dbuf, dsh)
        dz = dww_ref[CONF - 1:CONF, :] * dbuf[pl.ds(0, TS), :]
        for sft in range(1, CONF):
            dz = dz + dww_ref[CONF - 1 - sft:CONF - sft, :] * _rows_at(dbuf, dsh, sft, TS)
        ga = ga_ref[...].astype(F32)
        sgb = _sigmoid(gb_ref[...].astype(F32))
        dga = (dz * sgb).astype(BF16)
        dgb = (dz * ga * sgb * (1.0 - sgb)).astype(BF16)
        dpb_ref[:, 0:BW] = dzc
        dpb_ref[:, BW:2 * BW] = dga
        dpb_ref[:, 2 * BW:3 * BW] = dgb
        dh = (_dot(dzc, w_ref[0:BW, :]) + _dot(dga, w_ref[BW:2 * BW, :]) + _dot(dgb, w_ref[2 * BW:3 * BW, :])
              + _dot(dpc_ref[...], w_ref[3 * BW:ODD_IN, :]))
        dx_ref[...] = _pre_norm_bwd(dh, x_ref[...], pg[...], dres_ref[...], dpre_ref)

    row = pl.BlockSpec((TS, D), lambda i: (i, 0))

    def tile(n, j=0):
        return pl.BlockSpec((TS, n), lambda i: (i, j))

    nxt = pl.BlockSpec((HALO, BW), _halo_next(TS // HALO, s // HALO))
    return pl.pallas_call(
        body, grid=(nt,), name="odd_bwd2",
        out_shape=(jax.ShapeDtypeStruct((s, 3 * BW), BF16), jax.ShapeDtypeStruct((s, D), F32),
                   jax.ShapeDtypeStruct((1, D), F32)),
        in_specs=[tile(XA + MIX), tile(BW), nxt, tile(BW), nxt, tile(BW, 1), tile(BW, 2), _const((4, TS, HALO + TS)),
                  _const((CONF, BW)), _resident((ODD_IN, D)), row, _const((1, D)), row],
        out_specs=(tile(3 * BW), row, _const((1, D))),
        scratch_shapes=[pltpu.VMEM((TS + HALO, BW), F32), pltpu.VMEM((7, SHIFT_ROWS, BW), F32)],
        compiler_params=_cp(("arbitrary",)),
    )(dpc, tmpc, tmpc, tmpd, tmpd, p, p, _band_matrices(TS, True), dww, w_t, x, pre_g, dres)


def _grad_tn(a, b, tm, out=None, rows=None, row0=0, name="grad_tn"):
    s, m = a.shape
    n = b.shape[1]
    ts = min(2048, s)
    rows = m if rows is None else rows
    blk0 = row0 // tm
    assert m % tm == 0 and row0 % tm == 0 and s % ts == 0
    ns = s // ts

    def body(*refs):
        a_ref, b_ref = refs[0], refs[1]
        o_ref, acc = refs[-2], refs[-1]
        k = pl.program_id(1)

        @pl.when(k == 0)
        def _():
            acc[...] = jnp.zeros_like(acc)

        acc[...] += _dot_tn(a_ref[...], b_ref[...])

        @pl.when(k == ns - 1)
        def _():
            o_ref[...] = acc[...].astype(BF16)

    in_specs = [pl.BlockSpec((ts, tm), lambda i, k: (k, i)), pl.BlockSpec((ts, n), lambda i, k: (k, 0))]
    args = [a, b]
    aliases = {}
    if out is not None:
        in_specs.append(pl.BlockSpec(memory_space=pltpu.HBM))
        args.append(out)
        aliases = {2: 0}
    return pl.pallas_call(
        body, grid=(m // tm, ns), name=name,
        out_shape=jax.ShapeDtypeStruct((rows, n), BF16),
        in_specs=in_specs, out_specs=pl.BlockSpec((tm, n), lambda i, k: (blk0 + i, 0)),
        scratch_shapes=[pltpu.VMEM((tm, n), F32)], input_output_aliases=aliases,
        compiler_params=_cp(("arbitrary", "arbitrary")),
    )(*args)


def _place():
    x, y, c = lax.axis_index("x"), lax.axis_index("y"), lax.axis_index("c")
    chips = [(1 - x, y), (x, 1 - y), (1 - x, 1 - y)]
    return x, y, c, chips


def _hbm_specs(n):
    return [pl.BlockSpec(memory_space=pltpu.HBM)] * n


def _row_tile(r):
    for cand in (512, 400, 304, 256, 192, 128, 96, 16):
        if r % cand == 0:
            return cand
    raise ValueError(r)


def _place_shard(shard, place, dtype, name):
    r, cc = shard.shape
    tr = _row_tile(r)
    nt = r // tr

    def body(place_ref, s_ref, o_ref):
        o_ref[...] = s_ref[...].astype(dtype)

    return pl.pallas_call(
        body, name=name, out_shape=jax.ShapeDtypeStruct((N_CHIPS * r, cc), dtype),
        grid_spec=pltpu.PrefetchScalarGridSpec(
            num_scalar_prefetch=1, grid=(nt,),
            in_specs=[pl.BlockSpec((tr, cc), lambda i, pr: (i, 0))],
            out_specs=pl.BlockSpec((tr, cc), lambda i, pr: (pr[1] * nt + i, 0))),
        compiler_params=_cp(("arbitrary",)),
    )(place, shard)


def _gather_weights(fulls):
    n = len(fulls)

    def body(*refs):
        outs = refs[n:2 * n]
        send_sems, recv_sems = refs[2 * n:]
        x, y, c, chips = _place()
        me_k = 2 * x + y
        sibling = (x, y, 1 - c)

        def rows(a, k, half):
            r = fulls[a].shape[0] // N_CHIPS
            return outs[a].at[pl.ds(k * r + half * (r // 2), r // 2)]

        def copy(a, j, blk, to):
            return pltpu.make_async_remote_copy(src_ref=blk, dst_ref=blk, send_sem=send_sems.at[a * 6 + j],
                                                recv_sem=recv_sems.at[a * 6 + j], device_id=to, device_id_type=MESH)

        started = []
        for j, (px, py) in enumerate(chips):
            for a in range(n):
                cp = copy(a, j, rows(a, me_k, c), (px, py, c))
                cp.start()
                started.append(cp)
        for j, (px, py) in enumerate(chips):
            k = 2 * px + py
            for a in range(n):
                copy(a, j, rows(a, k, c), (px, py, c)).wait_recv()
                cp = copy(a, 3 + j, rows(a, k, c), sibling)
                cp.start()
                started.append(cp)
        for j, (px, py) in enumerate(chips):
            k = 2 * px + py
            for a in range(n):
                copy(a, 3 + j, rows(a, k, 1 - c), sibling).wait_recv()
        for cp in started:
            cp.wait_send()

    return pl.pallas_call(
        body, name="gather_weights",
        out_shape=tuple(jax.ShapeDtypeStruct(a.shape, a.dtype) for a in fulls),
        in_specs=_hbm_specs(n), out_specs=tuple(_hbm_specs(n)),
        input_output_aliases={a: a for a in range(n)},
        scratch_shapes=[pltpu.SemaphoreType.DMA((6 * n,)), pltpu.SemaphoreType.DMA((6 * n,))],
    )(*fulls)


def _swap_halves(grads, small):
    n = len(grads)

    def body(*refs):
        ins, outs = refs[:n + 1], refs[n + 1:2 * n + 2]
        send_sems, recv_sems = refs[2 * n + 2:]
        x, y, c, _ = _place()
        sibling = (x, y, 1 - c)
        cps = []
        for a in range(n + 1):
            src = ins[a].at[:, 1 - c] if a < n else ins[a]
            cp = pltpu.make_async_remote_copy(src_ref=src, dst_ref=outs[a], send_sem=send_sems.at[a],
                                              recv_sem=recv_sems.at[a], device_id=sibling, device_id_type=MESH)
            cp.start()
            cps.append(cp)
        for cp in cps:
            cp.wait_recv()
        for cp in cps:
            cp.wait_send()

    outs = tuple(jax.ShapeDtypeStruct((g.shape[0],) + g.shape[2:], g.dtype) for g in grads)
    outs += (jax.ShapeDtypeStruct(small.shape, small.dtype),)
    return pl.pallas_call(
        body, name="swap_halves", out_shape=outs, in_specs=_hbm_specs(n + 1), out_specs=tuple(_hbm_specs(n + 1)),
        scratch_shapes=[pltpu.SemaphoreType.DMA((n + 1,)), pltpu.SemaphoreType.DMA((n + 1,))],
    )(*grads, small)


def _pair_sum(g, recv, place, name):
    _, _, h, cc = g.shape
    th = _row_tile(h)

    def body(c_ref, g_ref, r_ref, o_ref):
        o_ref[...] = (g_ref[...].astype(F32) + r_ref[...].astype(F32)).astype(o_ref.dtype)

    return pl.pallas_call(
        body, name=name, out_shape=jax.ShapeDtypeStruct(recv.shape, recv.dtype),
        grid_spec=pltpu.PrefetchScalarGridSpec(
            num_scalar_prefetch=1, grid=(N_CHIPS, h // th),
            in_specs=[pl.BlockSpec((None, None, th, cc), lambda k, r, c_ref: (k, c_ref[0], r, 0)),
                      pl.BlockSpec((None, th, cc), lambda k, r, c_ref: (k, r, 0))],
            out_specs=pl.BlockSpec((None, th, cc), lambda k, r, c_ref: (k, r, 0))),
        compiler_params=_cp(("arbitrary", "arbitrary")),
    )(place, g, recv)


def _small_sum(a, b):
    def body(a_ref, b_ref, o_ref):
        o_ref[...] = a_ref[...] + b_ref[...]

    return pl.pallas_call(body, name="small_pair_sum", out_shape=jax.ShapeDtypeStruct(a.shape, a.dtype),
                          compiler_params=_cp())(a, b)


def _exchange_chips(sums, small):
    n = len(sums)
    hs = small.shape[0] // 2

    def body(*refs):
        ins, outs = refs[:n + 1], refs[n + 1:2 * n + 2]
        send_sems, recv_sems, local_sem = refs[2 * n + 2:]
        x, y, c, chips = _place()
        me_k = 2 * x + y
        mine = ins[n].at[pl.ds(c * hs, hs)]
        local = pltpu.make_async_copy(mine, outs[n].at[me_k], local_sem)
        local.start()
        cps = []
        for j, (px, py) in enumerate(chips):
            for a in range(n + 1):
                src, dst = (ins[a].at[2 * px + py], outs[a].at[j]) if a < n else (mine, outs[n].at[me_k])
                cp = pltpu.make_async_remote_copy(
                    src_ref=src, dst_ref=dst, send_sem=send_sems.at[a * 3 + j],
                    recv_sem=recv_sems.at[a * 3 + j], device_id=(px, py, c), device_id_type=MESH)
                cp.start()
                cps.append(cp)
        for cp in cps:
            cp.wait_recv()
        for cp in cps:
            cp.wait_send()
        local.wait()

    outs = tuple(jax.ShapeDtypeStruct((3,) + g.shape[1:], g.dtype) for g in sums)
    outs += (jax.ShapeDtypeStruct((N_CHIPS, hs, small.shape[1]), small.dtype),)
    return pl.pallas_call(
        body, name="exchange_chips", out_shape=outs, in_specs=_hbm_specs(n + 1), out_specs=tuple(_hbm_specs(n + 1)),
        scratch_shapes=[pltpu.SemaphoreType.DMA((3 * (n + 1),)), pltpu.SemaphoreType.DMA((3 * (n + 1),)),
                        pltpu.SemaphoreType.DMA],
    )(*sums, small)


def _chip_sum(own, parts, place, name):
    npart, h, cc = parts.shape
    th = _row_tile(h)

    def body(*refs):
        p_ref, o_ref = refs[-2], refs[-1]
        acc = p_ref[0].astype(F32)
        if own is not None:
            acc = refs[1][...].astype(F32) + acc
        for k in range(1, npart):
            acc = acc + p_ref[k].astype(F32)
        o_ref[...] = acc

    in_specs = [pl.BlockSpec((npart, th, cc), lambda r, pr: (0, r, 0))]
    args = [parts]
    if own is not None:
        in_specs.insert(0, pl.BlockSpec((None, th, cc), lambda r, pr: (pr[1], r, 0)))
        args.insert(0, own)
    return pl.pallas_call(
        body, name=name, out_shape=jax.ShapeDtypeStruct((2, h, cc), F32),
        grid_spec=pltpu.PrefetchScalarGridSpec(
            num_scalar_prefetch=1, grid=(h // th,), in_specs=in_specs,
            out_specs=pl.BlockSpec((None, th, cc), lambda r, pr: (pr[0], r, 0))),
        compiler_params=_cp(("arbitrary",)),
    )(place, *args)


def _share_halves(halves):
    n = len(halves)

    def body(*refs):
        outs = refs[n:2 * n]
        send_sems, recv_sems = refs[2 * n:]
        x, y, c, _ = _place()
        cps = []
        for a in range(n):
            cp = pltpu.make_async_remote_copy(src_ref=outs[a].at[c], dst_ref=outs[a].at[c], send_sem=send_sems.at[a],
                                              recv_sem=recv_sems.at[a], device_id=(x, y, 1 - c), device_id_type=MESH)
            cp.start()
            cps.append(cp)
        for a in range(n):
            pltpu.make_async_remote_copy(src_ref=outs[a].at[1 - c], dst_ref=outs[a].at[1 - c], send_sem=send_sems.at[a],
                                         recv_sem=recv_sems.at[a], device_id=(x, y, 1 - c),
                                         device_id_type=MESH).wait_recv()
        for cp in cps:
            cp.wait_send()

    return pl.pallas_call(
        body, name="share_halves", out_shape=tuple(jax.ShapeDtypeStruct(g.shape, g.dtype) for g in halves),
        in_specs=_hbm_specs(n), out_specs=tuple(_hbm_specs(n)), input_output_aliases={a: a for a in range(n)},
        scratch_shapes=[pltpu.SemaphoreType.DMA((n,)), pltpu.SemaphoreType.DMA((n,))],
    )(*halves)


def _adamw_math(w, g, m, v):
    m = ADAM_B1 * m + (1.0 - ADAM_B1) * g
    v = ADAM_B2 * v + (1.0 - ADAM_B2) * (g * g)
    m_hat = m / (1.0 - ADAM_B1 ** ADAM_STEP)
    v_hat = v / (1.0 - ADAM_B2 ** ADAM_STEP)
    delta = -ADAM_LR * (m_hat / (jnp.sqrt(v_hat) + ADAM_EPS) + ADAM_WD * w)
    return delta, m, v


def _adamw_big(w, g, m, v, name):
    r, cc = w.shape
    tr = min(_row_tile(r), 256) if r % 256 == 0 else _row_tile(r)

    def body(w_ref, g_ref, m_ref, v_ref, d_ref, mo_ref, vo_ref):
        d, mm, vv = _adamw_math(w_ref[...], g_ref[...], m_ref[...], v_ref[...])
        d_ref[...] = d
        mo_ref[...] = mm
        vo_ref[...] = vv

    blk = pl.BlockSpec((tr, cc), lambda i: (i, 0))
    sd = jax.ShapeDtypeStruct((r, cc), F32)
    return pl.pallas_call(body, grid=(r // tr,), name=name, out_shape=(sd, sd, sd), in_specs=[blk] * 4,
                          out_specs=(blk, blk, blk), compiler_params=_cp(("arbitrary",)))(w, g, m, v)


def _adamw_small(ws, gs, ms, vs):
    n = len(ws)

    def body(*refs):
        for a in range(n):
            w_ref, g_ref, m_ref, v_ref = refs[4 * a:4 * a + 4]
            d_ref, mo_ref, vo_ref = refs[4 * n + 3 * a:4 * n + 3 * a + 3]
            d, mm, vv = _adamw_math(w_ref[...], g_ref[...], m_ref[...], v_ref[...])
            d_ref[...] = d
            mo_ref[...] = mm
            vo_ref[...] = vv

    args, outs = [], []
    for a in range(n):
        args += [ws[a], gs[a], ms[a], vs[a]]
        outs += [jax.ShapeDtypeStruct(ws[a].shape, F32)] * 3
    res = pl.pallas_call(body, name="adamw_small", out_shape=tuple(outs), compiler_params=_cp())(*args)
    return [res[3 * a:3 * a + 3] for a in range(n)]


def _flat_pack(arrs, rows):
    flat = jnp.concatenate([a.reshape(-1) for a in arrs])
    return jnp.pad(flat, (0, rows * D - flat.shape[0])).reshape(rows, D)


def _flat_unpack(flat, shapes):
    out, off = [], 0
    for shp in shapes:
        size = 1
        for d_ in shp:
            size *= d_
        out.append(flat[off:off + size].reshape(shp))
        off += size
    return out


SMALL_EVEN = ("even_pre_g", "even_a_ln_g", "even_a_ln_b", "even_a_ws", "even_a_bs", "even_b_conv", "even_mem_g",
              "even_post_g")
SMALL_ODD = ("odd_pre_g", "odd_c_wgrp", "odd_c_scale", "odd_d_dw_w", "odd_d_dw_b", "odd_d_ln_g", "odd_d_ln_b",
             "odd_d_pw_b", "odd_mem_g", "odd_post_g")
BIG = ("even_w_in", "even_w_kv", "even_w_out", "odd_w_in", "odd_d_pw_w", "odd_w_kv", "odd_w_out")
WEIGHTS = ("even_pre_g", "even_w_in", "even_a_ln_g", "even_a_ln_b", "even_a_ws", "even_a_bs", "even_b_conv",
           "even_mem_g", "even_w_kv", "even_w_out", "even_post_g", "odd_pre_g", "odd_w_in", "odd_c_wgrp",
           "odd_c_scale", "odd_d_dw_w", "odd_d_dw_b", "odd_d_ln_g", "odd_d_ln_b", "odd_d_pw_w", "odd_d_pw_b",
           "odd_mem_g", "odd_w_kv", "odd_w_out", "odd_post_g")
PACKED = (("even_b_conv", (3, 192)), ("odd_pre_g", (1, 256)), ("odd_c_scale", (1, 192)), ("odd_d_dw_w", (31, 192)),
          ("odd_d_dw_b", (1, 192)), ("odd_d_ln_g", (1, 192)), ("odd_d_ln_b", (1, 192)), ("odd_d_pw_b", (1, 192)),
          ("odd_mem_g", (1, 256)), ("odd_post_g", (1, 256)))
PACK_ROWS = 16
SMALL_ROWS = 256


def _local_step(x, mem, target, wt):
    tril = jnp.tril(jnp.ones((CH, CH), dtype=bool))
    wcat = jnp.where(tril[None], wt["even_a_ws"], 0.0).transpose(1, 0, 2).reshape(CH, 4 * CH).astype(BF16)
    bsg = jnp.repeat(wt["even_a_bs"].T, BW // 4, axis=1)
    hsel = (jnp.arange(BW)[:, None] // (BW // 4) == jnp.arange(128)[None, :]).astype(BF16)
    wg = wt["odd_c_wgrp"]
    g4 = BW // 4
    wbd = jnp.zeros((BW, BW), F32)
    for g in range(4):
        wbd = lax.dynamic_update_slice(wbd, wg[g], (g * g4, g * g4))
    wbd = wbd.astype(BF16)

    kv_e = _kv_fwd(mem, wt["even_mem_g"], wt["even_w_kv"], "even_kv")
    kv_o = _kv_fwd(mem, wt["odd_mem_g"], wt["odd_w_kv"], "odd_kv")
    p_e, h_e = _in_fwd(x, wt["even_pre_g"], wt["even_w_in_t"], "even_in")
    x1, o_e = _even_fwd(x, p_e, kv_e, wt["even_a_ln_g"], wt["even_a_ln_b"], wcat, bsg, wt["even_b_conv"],
                        wt["even_w_out"], wt["even_post_g"])
    p_o, h_o = _in_fwd(x1, wt["odd_pre_g"], wt["odd_w_in_t"], "odd_in")
    dx2, o_o, cv_o, loss = _odd_fwd(x1, p_o, kv_o, wbd, wt["odd_c_scale"], wt["odd_d_dw_w"], wt["odd_d_dw_b"],
                                    wt["odd_d_ln_g"], wt["odd_d_ln_b"], wt["odd_d_pw_w"], wt["odd_d_pw_b"],
                                    wt["odd_w_out"], wt["odd_post_g"], target)
    (dpc_o, tmpc, tmpd, do_o, y_o, g_post_o, g_cs, g_wbd, g_dww, g_dwb, g_lng_o, g_lnb_o, g_pww, g_pwb,
     dkv_o) = _odd_bwd1(dx2, o_o, cv_o, p_o, kv_o, wbd, wt["odd_c_scale"], wt["odd_d_dw_w"], wt["odd_d_dw_b"],
                        wt["odd_d_ln_g"], wt["odd_d_ln_b"], wt["odd_d_pw_w"], wt["odd_d_pw_b"], wt["odd_w_out"],
                        wt["odd_post_g"])
    dpb_o, dx1, g_pre_o = _odd_bwd2(dpc_o, tmpc, tmpd, p_o, wt["odd_d_dw_w"], wt["odd_w_in_t"], x1,
                                    wt["odd_pre_g"], dx2)
    g_win_o = _grad_tn(dpb_o, h_o, 768, rows=ODD_IN, name="odd_gw_in_b")
    g_win_o = _grad_tn(dpc_o, h_o, 256, out=g_win_o, rows=ODD_IN, row0=3 * BW, name="odd_gw_in_c")
    g_wout_o = _grad_tn(y_o, do_o, 1024, name="odd_gw_out")
    g_wkv_o, g_memg_o = _kv_bwd(mem, wt["odd_mem_g"], wt["odd_w_kv"], dkv_o, "odd_kv_bwd")
    (dpa_e, dpc_e, tmp_e, do_e, y_e, g_post_e, g_lng_e, g_lnb_e, g_wcat, g_bs, g_bconv,
     dkv_e) = _even_bwd1(dx1, o_e, p_e, kv_e, wt["even_a_ln_g"], wt["even_a_ln_b"], wcat, bsg, hsel,
                         wt["even_b_conv"], wt["even_w_out"], wt["even_post_g"])
    dpb_e, dx0, g_pre_e = _even_bwd2(dpa_e, dpc_e, tmp_e, p_e, wt["even_b_conv"], wt["even_w_in_t"], x,
                                     wt["even_pre_g"], dx1)
    g_win_e = _grad_tn(dpa_e, h_e, 768, rows=EVEN_IN, name="even_gw_in_a")
    g_win_e = _grad_tn(dpb_e, h_e, 768, out=g_win_e, rows=EVEN_IN, row0=3 * BW, name="even_gw_in_b")
    g_win_e = _grad_tn(dpc_e, h_e, 1280, out=g_win_e, rows=EVEN_IN, row0=5 * BW, name="even_gw_in_c")
    g_wout_e = _grad_tn(y_e, do_e, 1024, name="even_gw_out")
    g_wkv_e, g_memg_e = _kv_bwd(mem, wt["even_mem_g"], wt["even_w_kv"], dkv_e, "even_kv_bwd")

    g_aws = jnp.where(tril[None], g_wcat.reshape(CH, 4, CH).transpose(1, 0, 2), 0.0)
    g_wgrp = jnp.stack([lax.dynamic_slice(g_wbd, (g * g4, g * g4), (g4, g4)) for g in range(4)])
    small = {
        "even_pre_g": g_pre_e, "even_a_ln_g": g_lng_e, "even_a_ln_b": g_lnb_e, "even_a_ws": g_aws,
        "even_a_bs": g_bs[:, 0:4].T, "even_b_conv": g_bconv[0:3], "even_mem_g": g_memg_e, "even_post_g": g_post_e,
        "odd_pre_g": g_pre_o, "odd_c_wgrp": g_wgrp, "odd_c_scale": g_cs, "odd_d_dw_w": g_dww.reshape(CONF, 8, BW).sum(axis=1),
        "odd_d_dw_b": g_dwb, "odd_d_ln_g": g_lng_o, "odd_d_ln_b": g_lnb_o, "odd_d_pw_b": g_pwb,
        "odd_mem_g": g_memg_o, "odd_post_g": g_post_o,
    }
    big = {"even_w_in": g_win_e, "even_w_kv": g_wkv_e, "even_w_out": g_wout_e, "odd_w_in": g_win_o,
           "odd_d_pw_w": g_pww.astype(BF16), "odd_w_kv": g_wkv_o, "odd_w_out": g_wout_o}
    return loss[0, 0], dx0, big, small


def kernel(x, mem, even_pre_g, even_w_in, even_a_ln_g, even_a_ln_b, even_a_ws, even_a_bs, even_b_conv, even_mem_g, even_w_kv, even_w_out, even_post_g, odd_pre_g, odd_w_in, odd_c_wgrp, odd_c_scale, odd_d_dw_w, odd_d_dw_b, odd_d_ln_g, odd_d_ln_b, odd_d_pw_w, odd_d_pw_b, odd_mem_g, odd_w_kv, odd_w_out, odd_post_g, loss_target, m_even_pre_g, m_even_w_in, m_even_a_ln_g, m_even_a_ln_b, m_even_a_ws, m_even_a_bs, m_even_b_conv, m_even_mem_g, m_even_w_kv, m_even_w_out, m_even_post_g, m_odd_pre_g, m_odd_w_in, m_odd_c_wgrp, m_odd_c_scale, m_odd_d_dw_w, m_odd_d_dw_b, m_odd_d_ln_g, m_odd_d_ln_b, m_odd_d_pw_w, m_odd_d_pw_b, m_odd_mem_g, m_odd_w_kv, m_odd_w_out, m_odd_post_g, v_even_pre_g, v_even_w_in, v_even_a_ln_g, v_even_a_ln_b, v_even_a_ws, v_even_a_bs, v_even_b_conv, v_even_mem_g, v_even_w_kv, v_even_w_out, v_even_post_g, v_odd_pre_g, v_odd_w_in, v_odd_c_wgrp, v_odd_c_scale, v_odd_d_dw_w, v_odd_d_dw_b, v_odd_d_ln_g, v_odd_d_ln_b, v_odd_d_pw_w, v_odd_d_pw_b, v_odd_mem_g, v_odd_w_kv, v_odd_w_out, v_odd_post_g):
    given = dict(locals())
    w = {n: given[n] for n in WEIGHTS}
    mom = {n: given["m_" + n] for n in WEIGHTS}
    var = {n: given["v_" + n] for n in WEIGHTS}

    x_, y_, c_ = lax.axis_index("x"), lax.axis_index("y"), lax.axis_index("c")
    chip = 2 * x_ + y_
    place = jnp.stack([c_, chip]).astype(jnp.int32)
    pack = _flat_pack([w[n][0] for n, _ in PACKED], PACK_ROWS)
    shards = [w["even_w_in"][0].T, w["odd_w_in"][0].T, w["even_w_kv"][0], w["odd_w_kv"][0], w["even_w_out"][0],
              w["odd_w_out"][0], w["odd_d_pw_w"][0]]
    fulls = [_place_shard(a, place, BF16, "place_%d" % j) for j, a in enumerate(shards)]
    fulls.append(_place_shard(pack, place, F32, "place_pack"))
    win_e, win_o, wkv_e, wkv_o, wout_e, wout_o, pww, packs = _gather_weights(fulls)
    packs = packs.reshape(N_CHIPS, PACK_ROWS * D)
    wt = {"even_w_in_t": win_e, "odd_w_in_t": win_o, "even_w_kv": wkv_e, "odd_w_kv": wkv_o, "even_w_out": wout_e,
          "odd_w_out": wout_o, "odd_d_pw_w": pww}
    per_chip = [_flat_unpack(packs[k], [shp for _, shp in PACKED]) for k in range(N_CHIPS)]
    for a, (name, _) in enumerate(PACKED):
        wt[name] = jnp.concatenate([per_chip[k][a] for k in range(N_CHIPS)], axis=-1)
    for name in ("even_pre_g", "even_a_ln_g", "even_a_ln_b", "even_mem_g", "even_post_g"):
        wt[name] = w[name]
    wt["even_a_ws"] = w["even_a_ws"][0]
    wt["even_a_bs"] = w["even_a_bs"][0]
    wt["odd_c_wgrp"] = w["odd_c_wgrp"][0]

    loss, grad_x, big, small = _local_step(x[0], mem[0], loss_target[0], wt)

    names = SMALL_EVEN + SMALL_ODD
    small_pack = _flat_pack([small[n] for n in names] + [loss.reshape(1)], SMALL_ROWS)
    g4 = [big[n].reshape(N_CHIPS, 2, big[n].shape[0] // 8, big[n].shape[1]) for n in BIG]
    recv = _swap_halves(g4, small_pack)
    sums = [_pair_sum(g4[a], recv[a], place, "pair_sum_%d" % a) for a in range(len(BIG))]
    small_sum = _small_sum(small_pack, recv[-1])
    parts = _exchange_chips(sums, small_sum)
    halves = [_chip_sum(sums[a], parts[a], place, "chip_sum_%d" % a) for a in range(len(BIG))]
    halves.append(_chip_sum(None, parts[-1], place, "chip_sum_small"))
    full = _share_halves(halves)
    gbig = {n: full[a].reshape(full[a].shape[1] * 2, full[a].shape[2]) for a, n in enumerate(BIG)}
    gsmall_flat = full[-1].reshape(-1)

    grads = {}
    full_shapes = [small[n].shape for n in names] + [(1,)]
    unpacked = _flat_unpack(gsmall_flat, full_shapes)
    loss = unpacked[-1][0]
    for n, g in zip(names, unpacked[:-1]):
        shard_shape = w[n].shape[1:]
        if g.shape[-1] != shard_shape[-1]:
            g = lax.dynamic_slice_in_dim(g, chip * shard_shape[-1], shard_shape[-1], axis=g.ndim - 1)
        grads[n] = g.reshape(shard_shape)

    def two_d(a):
        return a.reshape(-1, a.shape[-1])

    upd = {}
    for n in BIG:
        if n.endswith("w_in"):
            res = _adamw_big(w[n][0].T, gbig[n], mom[n][0].T, var[n][0].T, "adamw_" + n)
            grads[n] = gbig[n].T
            upd[n] = tuple(r.T for r in res)
        else:
            grads[n] = gbig[n]
            upd[n] = _adamw_big(w[n][0], gbig[n], mom[n][0], var[n][0], "adamw_" + n)
    res = _adamw_small([two_d(w[n][0]) for n in names], [two_d(grads[n]) for n in names],
                       [two_d(mom[n][0]) for n in names], [two_d(var[n][0]) for n in names])
    for n, r in zip(names, res):
        upd[n] = r

    outs = [loss, grad_x[None]]
    outs += [grads[n].reshape(w[n].shape) for n in WEIGHTS]
    for j in range(3):
        outs += [upd[n][j].reshape(w[n].shape) for n in WEIGHTS]
    return tuple(outs)
```

```python
import functools

import jax
import jax.numpy as jnp
from jax import lax
from jax.experimental import pallas as pl
from jax.experimental.pallas import tpu as pltpu

F32 = jnp.float32
BF16 = jnp.bfloat16
MESH = pl.DeviceIdType.MESH

D = 1024
N_MEM = 256
MIX = 2048
XA = 512
HD = 128
BW = 768
CH = 128
EPS = 1e-6
SCALE = HD ** -0.5
POOL_WINDOWS = (2, 4, 8, 16)
CONF = 31
EVEN_IN = 6400
ODD_IN = 4864
N_CHIPS = 4

ADAM_LR = 0.001
ADAM_B1 = 0.9
ADAM_B2 = 0.999
ADAM_EPS = 1e-08
ADAM_WD = 0.01
ADAM_STEP = 10

TS = 256
HALO = 32
VMEM_LIMIT = 56 * 1024 * 1024


def _cp(sem=None):
    return pltpu.CompilerParams(dimension_semantics=sem, vmem_limit_bytes=VMEM_LIMIT)


def _dot(a, b):
    return jnp.dot(a, b, preferred_element_type=F32)


def _dot_nt(a, b):
    return lax.dot_general(a, b, (((1,), (1,)), ((), ())), preferred_element_type=F32)


def _dot_tn(a, b):
    return lax.dot_general(a, b, (((0,), (0,)), ((), ())), preferred_element_type=F32)


def _sigmoid(x):
    return 1.0 / (1.0 + jnp.exp(-x))


def _resident(shape):
    return pl.BlockSpec(shape, lambda *_: (0,) * len(shape), pipeline_mode=pl.Buffered(1))


def _const(shape):
    return pl.BlockSpec(shape, lambda *_: (0,) * len(shape))


def _kv_fwd(mem, mem_g, wkv, name):
    def body(mem_ref, g_ref, w_ref, kv_ref):
        m = mem_ref[...]
        r = lax.rsqrt(jnp.mean(m * m, axis=-1, keepdims=True) + EPS)
        mn = (m * r * g_ref[...]).astype(BF16)
        kv_ref[...] = _dot(mn, w_ref[...]).astype(BF16)

    return pl.pallas_call(body, out_shape=jax.ShapeDtypeStruct((N_MEM, D), BF16), name=name,
                          compiler_params=_cp())(mem, mem_g, wkv)


def _kv_bwd(mem, mem_g, wkv, dkv, name):
    def body(mem_ref, g_ref, w_ref, dkv_ref, dw_ref, dg_ref):
        m = mem_ref[...]
        r = lax.rsqrt(jnp.mean(m * m, axis=-1, keepdims=True) + EPS)
        mh = m * r
        mn = (mh * g_ref[...]).astype(BF16)
        dkv = dkv_ref[...].astype(BF16)
        dw_ref[...] = _dot_tn(mn, dkv).astype(BF16)
        dmn = _dot_nt(dkv, w_ref[...])
        dg_ref[...] = jnp.sum(dmn * mh, axis=0, keepdims=True)

    return pl.pallas_call(body, out_shape=(jax.ShapeDtypeStruct((D, D), BF16), jax.ShapeDtypeStruct((1, D), F32)),
                          name=name, compiler_params=_cp())(mem, mem_g, wkv, dkv)


def _host_call(body, *, grid, name, out_shape, in_specs, out_specs, args, scratch_shapes=(), rider=None):
    if rider is None:
        res = pl.pallas_call(body, grid=grid, name=name, out_shape=tuple(out_shape), in_specs=list(in_specs),
                             out_specs=tuple(out_specs), scratch_shapes=list(scratch_shapes),
                             compiler_params=_cp(("arbitrary",)))(*args)
        return tuple(res), ()
    n_in, n_out, n_sc = len(in_specs), len(out_specs), len(scratch_shapes)
    r_in, r_out = len(rider.inputs), len(rider.out_shapes)
    last = grid[0] - 1

    def full_body(*refs):
        host_in = refs[:n_in]
        rid_in = refs[n_in:n_in + r_in]
        host_out = refs[n_in + r_in:n_in + r_in + n_out]
        rid_out = refs[n_in + r_in + n_out:n_in + r_in + n_out + r_out]
        host_sc = refs[n_in + r_in + n_out + r_out:n_in + r_in + n_out + r_out + n_sc]
        sems = refs[n_in + r_in + n_out + r_out + n_sc:]
        i = pl.program_id(0)

        @pl.when(i == 0)
        def _():
            rider.start(rid_in, rid_out, sems)

        if rider.has_mid:
            @pl.when(i == last)
            def _():
                rider.mid(rid_in, rid_out, sems)

        body(*host_in, *host_out, *host_sc)

        @pl.when(i == last)
        def _():
            rider.end(rid_in, rid_out, sems)

    res = pl.pallas_call(
        full_body, grid=grid, name=name, out_shape=tuple(out_shape) + tuple(rider.out_shapes),
        in_specs=list(in_specs) + _hbm_specs(r_in), out_specs=tuple(out_specs) + tuple(_hbm_specs(r_out)),
        scratch_shapes=list(scratch_shapes) + list(rider.sems),
        input_output_aliases={n_in + j: n_out + k for j, k in rider.aliases.items()},
        compiler_params=_cp(("arbitrary",)),
    )(*args, *rider.inputs)
    return tuple(res[:n_out]), tuple(res[n_out:])


def _in_fwd(x, pre_g, w_t, name, rider=None):
    s, n = x.shape[0], w_t.shape[0]
    tm = min(512, s)
    nc = 256

    def body(x_ref, g_ref, w_ref, p_ref, h_ref):
        xv = x_ref[...]
        r = lax.rsqrt(jnp.mean(xv * xv, axis=-1, keepdims=True) + EPS)
        h = (xv * r * g_ref[...]).astype(BF16)
        h_ref[...] = h
        for j in range(n // nc):
            p_ref[:, j * nc:(j + 1) * nc] = _dot_nt(h, w_ref[j * nc:(j + 1) * nc, :]).astype(BF16)

    return _host_call(
        body, grid=(s // tm,), name=name, rider=rider,
        out_shape=(jax.ShapeDtypeStruct((s, n), BF16), jax.ShapeDtypeStruct((s, D), BF16)),
        in_specs=[pl.BlockSpec((tm, D), lambda i: (i, 0)), _const((1, D)), _resident((n, D))],
        out_specs=(pl.BlockSpec((tm, n), lambda i: (i, 0)), pl.BlockSpec((tm, D), lambda i: (i, 0))),
        args=(x, pre_g, w_t))


def _xattn_fwd(q, kv_ref):
    outs, probs = [], []
    for h in range(XA // HD):
        qh = q[:, h * HD:(h + 1) * HD]
        kh = kv_ref[:, h * HD:(h + 1) * HD]
        vh = kv_ref[:, XA + h * HD:XA + (h + 1) * HD]
        sc = _dot_nt(qh, kh) * SCALE
        e = jnp.exp(sc - jnp.max(sc, axis=-1, keepdims=True))
        pr = e / jnp.sum(e, axis=-1, keepdims=True)
        outs.append(_dot(pr.astype(BF16), vh))
        probs.append(pr)
    return jnp.concatenate(outs, axis=-1), probs


def _xattn_bwd(dyx, q, probs, kv_ref, dkv_ref):
    dqs = []
    for h in range(XA // HD):
        qh = q[:, h * HD:(h + 1) * HD]
        kh = kv_ref[:, h * HD:(h + 1) * HD]
        vh = kv_ref[:, XA + h * HD:XA + (h + 1) * HD]
        dy = dyx[:, h * HD:(h + 1) * HD].astype(BF16)
        pr = probs[h]
        dp = _dot_nt(dy, vh)
        ds = (pr * (dp - jnp.sum(dp * pr, axis=-1, keepdims=True))).astype(BF16)
        dqs.append(_dot(ds, kh) * SCALE)
        dkv_ref[:, h * HD:(h + 1) * HD] += _dot_tn(ds, qh) * SCALE
        dkv_ref[:, XA + h * HD:XA + (h + 1) * HD] += _dot_tn(pr.astype(BF16), dy)
    return jnp.concatenate(dqs, axis=-1)


def _layer_norm_fwd(v, g, b):
    mu = jnp.mean(v, axis=-1, keepdims=True)
    vc = v - mu
    rstd = lax.rsqrt(jnp.mean(vc * vc, axis=-1, keepdims=True) + EPS)
    vhat = vc * rstd
    return vhat * g + b, vhat, rstd


def _layer_norm_bwd(dy, vhat, rstd, g):
    dvh = dy * g
    return rstd * (dvh - jnp.mean(dvh, axis=-1, keepdims=True) - vhat * jnp.mean(dvh * vhat, axis=-1, keepdims=True))


def _head_masks():
    col = lax.broadcasted_iota(jnp.int32, (1, BW), 1)
    return [(col >= h * (BW // 4)) & (col < (h + 1) * (BW // 4)) for h in range(4)]


def _halo_prev(nblk_per_tile):
    return lambda i: (jnp.maximum(i * nblk_per_tile - 1, 0), 0)


def _row_ids(i, t):
    return i * t + lax.broadcasted_iota(jnp.int32, (t, 1), 0)


def _even_mix(i, p_ref, ph_ref, ln_g, ln_b, wcat_ref, bsg_ref, bconv_ref, wbuf):
    t = p_ref.shape[0]
    u = p_ref[:, 0:BW].astype(F32)
    v = p_ref[:, BW:2 * BW].astype(F32)
    bg = p_ref[:, 2 * BW:3 * BW].astype(F32)
    cg = p_ref[:, 3 * BW:4 * BW].astype(F32)
    xin = p_ref[:, 4 * BW:5 * BW].astype(F32)
    vn, vhat, rstd = _layer_norm_fwd(v, ln_g, ln_b)
    masks = _head_masks()
    sgs, vsts = [], []
    for n in range(t // CH):
        vn_c = vn[n * CH:(n + 1) * CH]
        vst = jnp.concatenate([jnp.where(m, vn_c, 0.0) for m in masks], axis=0).astype(BF16)
        sgs.append(_dot(wcat_ref[...], vst) + bsg_ref[...])
        vsts.append(vst)
    sg = jnp.concatenate(sgs, axis=0)
    ya = u * sg
    w_halo = ph_ref[:, 3 * BW:4 * BW].astype(F32) * ph_ref[:, 4 * BW:5 * BW].astype(F32)
    wbuf[0:HALO, :] = jnp.where(i > 0, w_halo, 0.0)
    wbuf[HALO:HALO + t, :] = cg * xin
    conv = (bconv_ref[0:1, :] * wbuf[pl.ds(HALO - 2, t), :] + bconv_ref[1:2, :] * wbuf[pl.ds(HALO - 1, t), :]
            + bconv_ref[2:3, :] * wbuf[pl.ds(HALO, t), :])
    yb = bg * conv
    return dict(u=u, bg=bg, vhat=vhat, rstd=rstd, sg=sg, vsts=vsts, conv=conv, ya=ya, yb=yb, masks=masks)


def _pool_select(vals):
    col = lax.broadcasted_iota(jnp.int32, (1, BW), 1)
    g = BW // 4
    return jnp.where(col < g, vals[0], jnp.where(col < 2 * g, vals[1], jnp.where(col < 3 * g, vals[2], vals[3])))


def _inv_counts(i, t):
    rows = _row_ids(i, t) + 1
    return [1.0 / jnp.minimum(rows, w).astype(F32) for w in POOL_WINDOWS]


def _band_matrices(t, forward):
    j = jnp.arange(t)[:, None]
    r = jnp.arange(HALO + t)[None, :]
    if forward:
        return jnp.stack([(r >= j) & (r < j + w) for w in POOL_WINDOWS]).astype(BF16)
    return jnp.stack([(r <= HALO + j) & (r > HALO + j - w) for w in POOL_WINDOWS]).astype(BF16)


SHIFT_ROWS = HALO + TS - 8


def _shifted_copies(buf, sh):
    for b in range(1, 8):
        sh[b - 1] = buf[pl.ds(b, SHIFT_ROWS), :]


def _rows_at(buf, sh, off, t):
    a, b = divmod(off, 8)
    return buf[pl.ds(8 * a, t), :] if b == 0 else sh[b - 1, pl.ds(8 * a, t), :]


def _tap_sums(d_ref, buf, sh, base, out_ref):
    t = d_ref.shape[0]
    group = 4
    for k0 in range(0, CONF, group):
        taps = list(range(k0, min(k0 + group, CONF)))

        def step(r, accs, taps=taps):
            row = pl.multiple_of(r * 8, 8)
            d = d_ref[pl.ds(row, 8), :]
            new = []
            for acc, k in zip(accs, taps):
                a, b = divmod(base + k, 8)
                src = buf[pl.ds(row + 8 * a, 8), :] if b == 0 else sh[b - 1, pl.ds(row + 8 * a, 8), :]
                new.append(acc + d * src)
            return tuple(new)

        accs = lax.fori_loop(0, t // 8, step, tuple(jnp.zeros((8, BW), F32) for _ in taps), unroll=2)
        for acc, k in zip(accs, taps):
            out_ref[8 * k:8 * k + 8, :] += acc


def _odd_mix(i, p_ref, ph_ref, bands_ref, wbd_ref, cscale, dww_ref, dwb, ln_g, ln_b, pww_ref, pwb, gbuf, gsh,
             cv=None):
    t = p_ref.shape[0]
    zc_bf = p_ref[:, 0:BW]
    zc = zc_bf.astype(F32)
    ga = p_ref[:, BW:2 * BW].astype(F32)
    gb = p_ref[:, 2 * BW:3 * BW].astype(F32)
    zh = ph_ref[:, 0:BW]
    zcat = jnp.concatenate([jnp.where(i > 0, zh, jnp.zeros_like(zh)), zc_bf], axis=0)
    inv = _inv_counts(i, t)
    pooled = _pool_select([_dot(bands_ref[w], zcat) * inv[w] for w in range(len(POOL_WINDOWS))]) - zc
    pooled_bf = pooled.astype(BF16)
    pre = _dot(pooled_bf, wbd_ref[...])
    yc = pre * cscale
    sgb = _sigmoid(gb)
    z = ga * sgb
    gh_a = ph_ref[:, BW:2 * BW].astype(F32)
    gh_b = ph_ref[:, 2 * BW:3 * BW].astype(F32)
    gbuf[0:HALO, :] = jnp.where(i > 0, gh_a * _sigmoid(gh_b), 0.0)
    gbuf[HALO:HALO + t, :] = z
    _shifted_copies(gbuf, gsh)
    if cv is None:
        cv = dwb + dww_ref[CONF - 1:CONF, :] * z
        for k in range(CONF - 1):
            cv = cv + dww_ref[k:k + 1, :] * _rows_at(gbuf, gsh, HALO - (CONF - 1) + k, t)
    zl, zhat, rstd = _layer_norm_fwd(cv, ln_g, ln_b)
    szl = _sigmoid(zl)
    zs = (zl * szl).astype(BF16)
    yd = _dot(zs, pww_ref[...]) + pwb
    return dict(ga=ga, sgb=sgb, pooled_bf=pooled_bf, pre=pre, yc=yc, zhat=zhat, rstd=rstd, zl=zl, szl=szl,
                zs=zs, yd=yd, inv=inv, cv=cv)


def _post_norm(o, post_g):
    r = lax.rsqrt(jnp.mean(o * o, axis=-1, keepdims=True) + EPS)
    return o * r, r


def _gate_out(y_a, y_b, y_x, gate, wout_ref):
    sgt = _sigmoid(gate)
    sgate = gate * sgt
    ys = [(y_a * sgate[:, 0:BW]).astype(BF16), (y_b * sgate[:, BW:2 * BW]).astype(BF16),
          (y_x * sgate[:, 2 * BW:MIX]).astype(BF16)]
    o = (_dot(ys[0], wout_ref[0:BW, :]) + _dot(ys[1], wout_ref[BW:2 * BW, :]) + _dot(ys[2], wout_ref[2 * BW:MIX, :]))
    return o, ys, sgt, sgate


def _tile_specs(s, n):
    nh = TS // HALO
    return pl.BlockSpec((TS, n), lambda i: (i, 0)), pl.BlockSpec((HALO, n), _halo_prev(nh))


def _even_fwd(x, p, kv, ln_g, ln_b, wcat, bsg, bconv, wout, post_g, rider=None):
    s = x.shape[0]

    def body(x_ref, p_ref, ph_ref, kv_ref, lng, lnb, wcat_ref, bsg_ref, bconv_ref, wout_ref, pg, x1_ref, o_ref, wbuf):
        i = pl.program_id(0)
        mx = _even_mix(i, p_ref, ph_ref, lng[...], lnb[...], wcat_ref, bsg_ref, bconv_ref, wbuf)
        yx, _ = _xattn_fwd(p_ref[:, 5 * BW:5 * BW + XA], kv_ref)
        gate = p_ref[:, 5 * BW + XA:EVEN_IN].astype(F32)
        o, _, _, _ = _gate_out(mx["ya"], mx["yb"], yx, gate, wout_ref)
        n, _ = _post_norm(o, pg[...])
        o_ref[...] = o
        x1_ref[...] = x_ref[...] + n * pg[...]

    tile, halo = _tile_specs(s, EVEN_IN)
    row = pl.BlockSpec((TS, D), lambda i: (i, 0))
    return _host_call(
        body, grid=(s // TS,), name="even_fwd", rider=rider,
        out_shape=(jax.ShapeDtypeStruct((s, D), F32), jax.ShapeDtypeStruct((s, D), F32)),
        in_specs=[row, tile, halo, _const((N_MEM, D)), _const((1, BW)), _const((1, BW)), _const((CH, 4 * CH)),
                  _const((CH, BW)), _const((3, BW)), _resident((MIX, D)), _const((1, D))],
        out_specs=(row, row),
        scratch_shapes=[pltpu.VMEM((HALO + TS, BW), F32)],
        args=(x, p, p, kv, ln_g, ln_b, wcat, bsg, bconv, wout, post_g))


def _odd_fwd(x1, p, kv, wbd, cscale, dww, dwb, ln_g, ln_b, pww, pwb, wout, post_g, target):
    s = x1.shape[0]

    def body(x_ref, p_ref, ph_ref, kv_ref, bands_ref, wbd_ref, cs, dww_ref, dwb_ref, lng, lnb, pww_ref, pwb_ref,
             wout_ref, pg, tgt_ref, dx_ref, o_ref, cv_ref, loss_ref, gbuf, gsh):
        i = pl.program_id(0)
        mx = _odd_mix(i, p_ref, ph_ref, bands_ref, wbd_ref, cs[...], dww_ref, dwb_ref[...], lng[...], lnb[...],
                      pww_ref, pwb_ref[...], gbuf, gsh)
        cv_ref[...] = mx["cv"]
        yx, _ = _xattn_fwd(p_ref[:, 3 * BW:3 * BW + XA], kv_ref)
        gate = p_ref[:, 3 * BW + XA:ODD_IN].astype(F32)
        o, _, _, _ = _gate_out(mx["yc"], mx["yd"], yx, gate, wout_ref)
        n, _ = _post_norm(o, pg[...])
        o_ref[...] = o
        err = x_ref[...] + n * pg[...] - tgt_ref[...]
        dx_ref[...] = err * (1.0 / D)

        @pl.when(i == 0)
        def _():
            loss_ref[...] = jnp.zeros_like(loss_ref)

        loss_ref[...] += 0.5 * jnp.sum(jnp.sum(err * err, axis=-1, keepdims=True) * (1.0 / D), axis=0, keepdims=True)

    tile, halo = _tile_specs(s, ODD_IN)
    row = pl.BlockSpec((TS, D), lambda i: (i, 0))
    vec = _const((1, BW))
    return pl.pallas_call(
        body, grid=(s // TS,), name="odd_fwd",
        out_shape=(jax.ShapeDtypeStruct((s, D), F32), jax.ShapeDtypeStruct((s, D), F32),
                   jax.ShapeDtypeStruct((s, BW), F32), jax.ShapeDtypeStruct((8, 128), F32)),
        in_specs=[row, tile, halo, _const((N_MEM, D)), _const((4, TS, HALO + TS)), _const((BW, BW)), vec,
                  _const((CONF, BW)), vec, vec, vec, _const((BW, BW)), vec, _resident((MIX, D)), _const((1, D)), row],
        out_specs=(row, row, pl.BlockSpec((TS, BW), lambda i: (i, 0)), _const((8, 128))),
        scratch_shapes=[pltpu.VMEM((HALO + TS, BW), F32), pltpu.VMEM((7, SHIFT_ROWS, BW), F32)],
        compiler_params=_cp(("arbitrary",)),
    )(x1, p, p, kv, _band_matrices(TS, False), wbd, cscale, dww, dwb, ln_g, ln_b, pww, pwb, wout, post_g, target)


def _acc_init(i, refs):
    @pl.when(i == 0)
    def _():
        for r in refs:
            r[...] = jnp.zeros_like(r)


def _post_norm_bwd(dx, o, pg, dpg_ref):
    n, r = _post_norm(o, pg)
    dpg_ref[...] += jnp.sum(dx * n, axis=0, keepdims=True)
    dn = dx * pg
    return (r * (dn - n * jnp.mean(dn * n, axis=-1, keepdims=True))).astype(BF16)


def _gate_bwd(do, wout_ref, ys_f32, gate, y_ref):
    dy = _dot_nt(do, wout_ref[...])
    sgt = _sigmoid(gate)
    sgate = gate * sgt
    dsilu = sgt * (1.0 + gate * (1.0 - sgt))
    offs = (0, BW, 2 * BW, MIX)
    dys, dgs = [], []
    for j, yv in enumerate(ys_f32):
        a, b = offs[j], offs[j + 1]
        y_ref[:, a:b] = (yv * sgate[:, a:b]).astype(BF16)
        dys.append(dy[:, a:b] * sgate[:, a:b])
        dgs.append(dy[:, a:b] * yv * dsilu[:, a:b])
    return dys, jnp.concatenate(dgs, axis=-1)


def _even_bwd1(dx, o, p, kv, ln_g, ln_b, wcat, bsg, hsel, bconv, wout, post_g, rider=None):
    s = dx.shape[0]

    def body(dx_ref, o_ref, p_ref, ph_ref, kv_ref, lng, lnb, wcat_ref, bsg_ref, hsel_ref, bconv_ref, wout_ref, pg,
             dpa_ref, dpc_ref, tmp_ref, do_ref, y_ref, dpg_ref, dlng_ref, dlnb_ref, dwcat_ref, dbs_ref, dbconv_ref,
             dkv_ref, wbuf):
        i = pl.program_id(0)
        _acc_init(i, (dpg_ref, dlng_ref, dlnb_ref, dwcat_ref, dbs_ref, dbconv_ref, dkv_ref))
        mx = _even_mix(i, p_ref, ph_ref, lng[...], lnb[...], wcat_ref, bsg_ref, bconv_ref, wbuf)
        q = p_ref[:, 5 * BW:5 * BW + XA]
        yx, probs = _xattn_fwd(q, kv_ref)
        gate = p_ref[:, 5 * BW + XA:EVEN_IN].astype(F32)
        do = _post_norm_bwd(dx_ref[...], o_ref[...], pg[...], dpg_ref)
        do_ref[...] = do
        (dya, dyb, dyx), dgate = _gate_bwd(do, wout_ref, (mx["ya"], mx["yb"], yx), gate, y_ref)
        dpa_ref[:, 0:BW] = (dya * mx["sg"]).astype(BF16)
        dsg = (dya * mx["u"]).astype(BF16)
        dvns = []
        for n in range(TS // CH):
            dsg_c = dsg[n * CH:(n + 1) * CH]
            dvst = _dot_tn(wcat_ref[...], dsg_c)
            dvn_c = jnp.where(mx["masks"][0], dvst[0:CH], 0.0)
            for h in range(1, 4):
                dvn_c = dvn_c + jnp.where(mx["masks"][h], dvst[h * CH:(h + 1) * CH], 0.0)
            dvns.append(dvn_c)
            dwcat_ref[...] += _dot_nt(dsg_c, mx["vsts"][n])
            dbs_ref[...] += _dot(dsg_c, hsel_ref[...])
        dvn = jnp.concatenate(dvns, axis=0)
        dlng_ref[...] += jnp.sum(dvn * mx["vhat"], axis=0, keepdims=True)
        dlnb_ref[...] += jnp.sum(dvn, axis=0, keepdims=True)
        dpa_ref[:, BW:2 * BW] = _layer_norm_bwd(dvn, mx["vhat"], mx["rstd"], lng[...]).astype(BF16)
        dpa_ref[:, 2 * BW:3 * BW] = (dyb * mx["conv"]).astype(BF16)
        dconv = dyb * mx["bg"]
        tmp_ref[...] = dconv.astype(BF16)
        for k in range(3):
            dbconv_ref[k:k + 1, :] += jnp.sum(dconv * wbuf[pl.ds(HALO - 2 + k, TS), :], axis=0, keepdims=True)
        dpc_ref[:, 0:XA] = _xattn_bwd(dyx, q, probs, kv_ref, dkv_ref).astype(BF16)
        dpc_ref[:, XA:XA + MIX] = dgate.astype(BF16)

    tile, halo = _tile_specs(s, EVEN_IN)
    row = pl.BlockSpec((TS, D), lambda i: (i, 0))
    vec = _const((1, BW))

    def out(n):
        return pl.BlockSpec((TS, n), lambda i: (i, 0))

    return _host_call(
        body, grid=(s // TS,), name="even_bwd1", rider=rider,
        out_shape=(jax.ShapeDtypeStruct((s, 3 * BW), BF16), jax.ShapeDtypeStruct((s, XA + MIX), BF16),
                   jax.ShapeDtypeStruct((s, BW), BF16), jax.ShapeDtypeStruct((s, D), BF16),
                   jax.ShapeDtypeStruct((s, MIX), BF16),
                   jax.ShapeDtypeStruct((1, D), F32), jax.ShapeDtypeStruct((1, BW), F32),
                   jax.ShapeDtypeStruct((1, BW), F32), jax.ShapeDtypeStruct((CH, 4 * CH), F32),
                   jax.ShapeDtypeStruct((CH, 128), F32), jax.ShapeDtypeStruct((8, BW), F32),
                   jax.ShapeDtypeStruct((N_MEM, D), F32)),
        in_specs=[row, row, tile, halo, _const((N_MEM, D)), vec, vec, _const((CH, 4 * CH)), _const((CH, BW)),
                  _const((BW, 128)), _const((3, BW)), _resident((MIX, D)), _const((1, D))],
        out_specs=(out(3 * BW), out(XA + MIX), out(BW), out(D), out(MIX),
                   _const((1, D)), vec, vec, _const((CH, 4 * CH)), _const((CH, 128)), _const((8, BW)),
                   _const((N_MEM, D))),
        scratch_shapes=[pltpu.VMEM((HALO + TS, BW), F32)],
        args=(dx, o, p, p, kv, ln_g, ln_b, wcat, bsg, hsel, bconv, wout, post_g))


def _odd_bwd1(dx, o, cv, p, kv, wbd, cscale, dww, dwb, ln_g, ln_b, pww, pwb, wout, post_g):
    s = dx.shape[0]

    def body(dx_ref, o_ref, cv_ref, p_ref, ph_ref, kv_ref, bands_ref, wbd_ref, cs, dww_ref, dwb_ref, lng, lnb,
             pww_ref, pwb_ref, wout_ref, pg,
             dpc_ref, tmpc_ref, tmpd_ref, do_ref, y_ref, dpg_ref, dcs_ref, dwbd_ref, ddww_ref, ddwb_ref, dlng_ref,
             dlnb_ref, dpww_ref, dpwb_ref, dkv_ref, gbuf, gsh, dcv_buf):
        i = pl.program_id(0)
        _acc_init(i, (dpg_ref, dcs_ref, dwbd_ref, ddww_ref, ddwb_ref, dlng_ref, dlnb_ref, dpww_ref, dpwb_ref,
                      dkv_ref))
        mx = _odd_mix(i, p_ref, ph_ref, bands_ref, wbd_ref, cs[...], dww_ref, dwb_ref[...], lng[...], lnb[...],
                      pww_ref, pwb_ref[...], gbuf, gsh, cv=cv_ref[...])
        q = p_ref[:, 3 * BW:3 * BW + XA]
        yx, probs = _xattn_fwd(q, kv_ref)
        gate = p_ref[:, 3 * BW + XA:ODD_IN].astype(F32)
        do = _post_norm_bwd(dx_ref[...], o_ref[...], pg[...], dpg_ref)
        do_ref[...] = do
        (dyc, dyd, dyx), dgate = _gate_bwd(do, wout_ref, (mx["yc"], mx["yd"], yx), gate, y_ref)
        dcs_ref[...] += jnp.sum(dyc * mx["pre"], axis=0, keepdims=True)
        dpre = (dyc * cs[...]).astype(BF16)
        dwbd_ref[...] += _dot_tn(mx["pooled_bf"], dpre)
        dpooled = _dot_nt(dpre, wbd_ref[...])
        tmpc_ref[...] = _pool_select([dpooled * c_ for c_ in mx["inv"]]).astype(BF16)
        dyd_bf = dyd.astype(BF16)
        dpwb_ref[...] += jnp.sum(dyd, axis=0, keepdims=True)
        dpww_ref[...] += _dot_tn(mx["zs"], dyd_bf)
        dzs = _dot_nt(dyd_bf, pww_ref[...])
        zl, szl = mx["zl"], mx["szl"]
        dzl = dzs * (szl * (1.0 + zl * (1.0 - szl)))
        dlng_ref[...] += jnp.sum(dzl * mx["zhat"], axis=0, keepdims=True)
        dlnb_ref[...] += jnp.sum(dzl, axis=0, keepdims=True)
        dcv = _layer_norm_bwd(dzl, mx["zhat"], mx["rstd"], lng[...])
        tmpd_ref[...] = dcv.astype(BF16)
        ddwb_ref[...] += jnp.sum(dcv, axis=0, keepdims=True)
        dcv_buf[...] = dcv
        _tap_sums(dcv_buf, gbuf, gsh, HALO - (CONF - 1), ddww_ref)
        dpc_ref[:, 0:XA] = _xattn_bwd(dyx, q, probs, kv_ref, dkv_ref).astype(BF16)
        dpc_ref[:, XA:XA + MIX] = dgate.astype(BF16)

    tile, halo = _tile_specs(s, ODD_IN)
    row = pl.BlockSpec((TS, D), lambda i: (i, 0))
    vec = _const((1, BW))

    def out(n):
        return pl.BlockSpec((TS, n), lambda i: (i, 0))

    return pl.pallas_call(
        body, grid=(s // TS,), name="odd_bwd1",
        out_shape=(jax.ShapeDtypeStruct((s, XA + MIX), BF16), jax.ShapeDtypeStruct((s, BW), BF16),
                   jax.ShapeDtypeStruct((s, BW), BF16), jax.ShapeDtypeStruct((s, D), BF16),
                   jax.ShapeDtypeStruct((s, MIX), BF16),
                   jax.ShapeDtypeStruct((1, D), F32), jax.ShapeDtypeStruct((1, BW), F32),
                   jax.ShapeDtypeStruct((BW, BW), F32), jax.ShapeDtypeStruct((8 * CONF, BW), F32),
                   jax.ShapeDtypeStruct((1, BW), F32), jax.ShapeDtypeStruct((1, BW), F32),
                   jax.ShapeDtypeStruct((1, BW), F32), jax.ShapeDtypeStruct((BW, BW), F32),
                   jax.ShapeDtypeStruct((1, BW), F32), jax.ShapeDtypeStruct((N_MEM, D), F32)),
        in_specs=[row, row, out(BW), tile, halo, _const((N_MEM, D)), _const((4, TS, HALO + TS)), _const((BW, BW)), vec,
                  _const((CONF, BW)), vec, vec, vec, _const((BW, BW)), vec, _resident((MIX, D)), _const((1, D))],
        out_specs=(out(XA + MIX), out(BW), out(BW), out(D), out(MIX),
                   _const((1, D)), vec, _const((BW, BW)), _const((8 * CONF, BW)), vec, vec, vec, _const((BW, BW)), vec,
                   _const((N_MEM, D))),
        scratch_shapes=[pltpu.VMEM((HALO + TS, BW), F32), pltpu.VMEM((7, SHIFT_ROWS, BW), F32),
                        pltpu.VMEM((TS, BW), F32)],
        compiler_params=_cp(("arbitrary",)),
    )(dx, o, cv, p, p, kv, _band_matrices(TS, False), wbd, cscale, dww, dwb, ln_g, ln_b, pww, pwb, wout, post_g)


def _halo_next(nblk_per_tile, nblk):
    return lambda i: (jnp.minimum((i + 1) * nblk_per_tile, nblk - 1), 0)


def _pre_norm_bwd(dh, x, pre_g, dres, dpre_ref):
    r = lax.rsqrt(jnp.mean(x * x, axis=-1, keepdims=True) + EPS)
    xh = x * r
    dpre_ref[...] += jnp.sum(dh * xh, axis=0, keepdims=True)
    dxh = dh * pre_g
    return dres + r * (dxh - xh * jnp.mean(dxh * xh, axis=-1, keepdims=True))


def _even_bwd2(dpa, dpc, tmp, p, bconv, w_t, x, pre_g, dres):
    s = x.shape[0]
    nt = s // TS

    def body(dpa_ref, dpc_ref, tmp_ref, tmph_ref, cg_ref, xin_ref, bconv_ref, w_ref, x_ref, pg, dres_ref,
             dpb_ref, dx_ref, dpre_ref, dbuf):
        i = pl.program_id(0)
        _acc_init(i, (dpre_ref,))
        dbuf[0:TS, :] = tmp_ref[...].astype(F32)
        dbuf[TS:TS + HALO, :] = jnp.where(i < nt - 1, tmph_ref[...].astype(F32), 0.0)
        dw = (bconv_ref[2:3, :] * dbuf[pl.ds(0, TS), :] + bconv_ref[1:2, :] * dbuf[pl.ds(1, TS), :]
              + bconv_ref[0:1, :] * dbuf[pl.ds(2, TS), :])
        dcg = (dw * xin_ref[...].astype(F32)).astype(BF16)
        dxin = (dw * cg_ref[...].astype(F32)).astype(BF16)
        dpb_ref[:, 0:BW] = dcg
        dpb_ref[:, BW:2 * BW] = dxin
        dh = (_dot(dpa_ref[...], w_ref[0:3 * BW, :]) + _dot(dcg, w_ref[3 * BW:4 * BW, :])
              + _dot(dxin, w_ref[4 * BW:5 * BW, :]) + _dot(dpc_ref[...], w_ref[5 * BW:EVEN_IN, :]))
        dx_ref[...] = _pre_norm_bwd(dh, x_ref[...], pg[...], dres_ref[...], dpre_ref)

    row = pl.BlockSpec((TS, D), lambda i: (i, 0))

    def tile(n, j=0):
        return pl.BlockSpec((TS, n), lambda i: (i, j))

    return pl.pallas_call(
        body, grid=(nt,), name="even_bwd2",
        out_shape=(jax.ShapeDtypeStruct((s, 2 * BW), BF16), jax.ShapeDtypeStruct((s, D), F32),
                   jax.ShapeDtypeStruct((1, D), F32)),
        in_specs=[tile(3 * BW), tile(XA + MIX), tile(BW), pl.BlockSpec((HALO, BW), _halo_next(TS // HALO, s // HALO)),
                  tile(BW, 3), tile(BW, 4), _const((3, BW)), _resident((EVEN_IN, D)), row, _const((1, D)), row],
        out_specs=(tile(2 * BW), row, _const((1, D))),
        scratch_shapes=[pltpu.VMEM((TS + HALO, BW), F32)],
        compiler_params=_cp(("arbitrary",)),
    )(dpa, dpc, tmp, tmp, p, p, bconv, w_t, x, pre_g, dres)


def _odd_bwd2(dpc, tmpc, tmpd, p, dww, w_t, x, pre_g, dres):
    s = x.shape[0]
    nt = s // TS

    def body(dpc_ref, tc_ref, tch_ref, td_ref, tdh_ref, ga_ref, gb_ref, bands_ref, dww_ref, w_ref, x_ref, pg,
             dres_ref, dpb_ref, dx_ref, dpre_ref, dbuf, dsh):
        i = pl.program_id(0)
        _acc_init(i, (dpre_ref,))
        more = i < nt - 1
        e_bf = tc_ref[...]
        eh = tch_ref[...]
        ecat = jnp.concatenate([e_bf, jnp.where(more, eh, jnp.zeros_like(eh))], axis=0)
        dbuf[0:TS, :] = td_ref[...].astype(F32)
        dbuf[TS:TS + HALO, :] = jnp.where(more, tdh_ref[...].astype(F32), 0.0)
        sums = [_dot(bands_ref[w], ecat) for w in range(len(POOL_WINDOWS))]
        rows = _row_ids(i, TS) + 1
        cnt = _pool_select([jnp.minimum(rows, w).astype(F32) for w in POOL_WINDOWS])
        dzc = (_pool_select(sums) - e_bf.astype(F32) * cnt).astype(BF16)
        _shifted_copies(dbuf, dsh)
        dz = dww_ref[CONF - 1:CONF, :] * dbuf[pl.ds(0, TS), :]
        for sft in range(1, CONF):
            dz = dz + dww_ref[CONF - 1 - sft:CONF - sft, :] * _rows_at(dbuf, dsh, sft, TS)
        ga = ga_ref[...].astype(F32)
        sgb = _sigmoid(gb_ref[...].astype(F32))
        dga = (dz * sgb).astype(BF16)
        dgb = (dz * ga * sgb * (1.0 - sgb)).astype(BF16)
        dpb_ref[:, 0:BW] = dzc
        dpb_ref[:, BW:2 * BW] = dga
        dpb_ref[:, 2 * BW:3 * BW] = dgb
        dh = (_dot(dzc, w_ref[0:BW, :]) + _dot(dga, w_ref[BW:2 * BW, :]) + _dot(dgb, w_ref[2 * BW:3 * BW, :])
              + _dot(dpc_ref[...], w_ref[3 * BW:ODD_IN, :]))
        dx_ref[...] = _pre_norm_bwd(dh, x_ref[...], pg[...], dres_ref[...], dpre_ref)

    row = pl.BlockSpec((TS, D), lambda i: (i, 0))

    def tile(n, j=0):
        return pl.BlockSpec((TS, n), lambda i: (i, j))

    nxt = pl.BlockSpec((HALO, BW), _halo_next(TS // HALO, s // HALO))
    return pl.pallas_call(
        body, grid=(nt,), name="odd_bwd2",
        out_shape=(jax.ShapeDtypeStruct((s, 3 * BW), BF16), jax.ShapeDtypeStruct((s, D), F32),
                   jax.ShapeDtypeStruct((1, D), F32)),
        in_specs=[tile(XA + MIX), tile(BW), nxt, tile(BW), nxt, tile(BW, 1), tile(BW, 2), _const((4, TS, HALO + TS)),
                  _const((CONF, BW)), _resident((ODD_IN, D)), row, _const((1, D)), row],
        out_specs=(tile(3 * BW), row, _const((1, D))),
        scratch_shapes=[pltpu.VMEM((TS + HALO, BW), F32), pltpu.VMEM((7, SHIFT_ROWS, BW), F32)],
        compiler_params=_cp(("arbitrary",)),
    )(dpc, tmpc, tmpc, tmpd, tmpd, p, p, _band_matrices(TS, True), dww, w_t, x, pre_g, dres)


def _grad_tn(a, b, tm, out=None, rows=None, row0=0, name="grad_tn"):
    s, m = a.shape
    n = b.shape[1]
    ts = min(2048, s)
    rows = m if rows is None else rows
    blk0 = row0 // tm
    assert m % tm == 0 and row0 % tm == 0 and s % ts == 0
    ns = s // ts

    def body(*refs):
        a_ref, b_ref = refs[0], refs[1]
        o_ref, acc = refs[-2], refs[-1]
        k = pl.program_id(1)

        @pl.when(k == 0)
        def _():
            acc[...] = jnp.zeros_like(acc)

        acc[...] += _dot_tn(a_ref[...], b_ref[...])

        @pl.when(k == ns - 1)
        def _():
            o_ref[...] = acc[...].astype(BF16)

    in_specs = [pl.BlockSpec((ts, tm), lambda i, k: (k, i)), pl.BlockSpec((ts, n), lambda i, k: (k, 0))]
    args = [a, b]
    aliases = {}
    if out is not None:
        in_specs.append(pl.BlockSpec(memory_space=pltpu.HBM))
        args.append(out)
        aliases = {2: 0}
    return pl.pallas_call(
        body, grid=(m // tm, ns), name=name,
        out_shape=jax.ShapeDtypeStruct((rows, n), BF16),
        in_specs=in_specs, out_specs=pl.BlockSpec((tm, n), lambda i, k: (blk0 + i, 0)),
        scratch_shapes=[pltpu.VMEM((tm, n), F32)], input_output_aliases=aliases,
        compiler_params=_cp(("arbitrary", "arbitrary")),
    )(*args)


def _place():
    x, y, c = lax.axis_index("x"), lax.axis_index("y"), lax.axis_index("c")
    chips = [(1 - x, y), (x, 1 - y), (1 - x, 1 - y)]
    return x, y, c, chips


def _hbm_specs(n):
    return [pl.BlockSpec(memory_space=pltpu.HBM)] * n


def _row_tile(r):
    for cand in (512, 400, 304, 256, 192, 128, 96, 16):
        if r % cand == 0:
            return cand
    raise ValueError(r)


def _place_shard(shard, place, dtype, name):
    r, cc = shard.shape
    tr = _row_tile(r)
    nt = r // tr

    def body(place_ref, s_ref, o_ref):
        o_ref[...] = s_ref[...].astype(dtype)

    return pl.pallas_call(
        body, name=name, out_shape=jax.ShapeDtypeStruct((N_CHIPS * r, cc), dtype),
        grid_spec=pltpu.PrefetchScalarGridSpec(
            num_scalar_prefetch=1, grid=(nt,),
            in_specs=[pl.BlockSpec((tr, cc), lambda i, pr: (i, 0))],
            out_specs=pl.BlockSpec((tr, cc), lambda i, pr: (pr[1] * nt + i, 0))),
        compiler_params=_cp(("arbitrary",)),
    )(place, shard)


class _GatherRider:
    has_mid = True

    def __init__(self, fulls):
        n = len(fulls)
        self.inputs = list(fulls)
        self.out_shapes = [jax.ShapeDtypeStruct(a.shape, a.dtype) for a in fulls]
        self.aliases = {a: a for a in range(n)}
        self.sems = [pltpu.SemaphoreType.DMA((6 * n,)), pltpu.SemaphoreType.DMA((6 * n,))]
        self.block_rows = [a.shape[0] // N_CHIPS for a in fulls]

    def _ctx(self, outs, sems):
        send_sems, recv_sems = sems
        x, y, c, chips = _place()

        def rows(a, k, half):
            r = self.block_rows[a]
            return outs[a].at[pl.ds(k * r + half * (r // 2), r // 2)]

        def copy(a, j, blk, to):
            return pltpu.make_async_remote_copy(src_ref=blk, dst_ref=blk, send_sem=send_sems.at[a * 6 + j],
                                                recv_sem=recv_sems.at[a * 6 + j], device_id=to, device_id_type=MESH)

        return x, y, c, chips, rows, copy

    def start(self, ins, outs, sems):
        x, y, c, chips, rows, copy = self._ctx(outs, sems)
        for j, (px, py) in enumerate(chips):
            for a in range(len(outs)):
                copy(a, j, rows(a, 2 * x + y, c), (px, py, c)).start()

    def mid(self, ins, outs, sems):
        x, y, c, chips, rows, copy = self._ctx(outs, sems)
        for j, (px, py) in enumerate(chips):
            for a in range(len(outs)):
                copy(a, j, rows(a, 2 * px + py, c), (px, py, c)).wait_recv()
                copy(a, 3 + j, rows(a, 2 * px + py, c), (x, y, 1 - c)).start()

    def end(self, ins, outs, sems):
        x, y, c, chips, rows, copy = self._ctx(outs, sems)
        for j, (px, py) in enumerate(chips):
            for a in range(len(outs)):
                copy(a, 3 + j, rows(a, 2 * px + py, 1 - c), (x, y, 1 - c)).wait_recv()
        for j, (px, py) in enumerate(chips):
            for a in range(len(outs)):
                copy(a, j, rows(a, 2 * x + y, c), (px, py, c)).wait_send()
                copy(a, 3 + j, rows(a, 2 * px + py, c), (x, y, 1 - c)).wait_send()


def _run_rider(rider, name):
    r_in, r_out = len(rider.inputs), len(rider.out_shapes)

    def body(*refs):
        ins, outs, sems = refs[:r_in], refs[r_in:r_in + r_out], refs[r_in + r_out:]
        rider.start(ins, outs, sems)
        rider.mid(ins, outs, sems)
        rider.end(ins, outs, sems)

    return pl.pallas_call(
        body, name=name, out_shape=tuple(rider.out_shapes), in_specs=_hbm_specs(r_in),
        out_specs=tuple(_hbm_specs(r_out)), input_output_aliases=dict(rider.aliases),
        scratch_shapes=list(rider.sems),
    )(*rider.inputs)


def _swap_halves(grads, small, name):
    n = len(grads)
    arrs = list(grads) + ([small] if small is not None else [])
    m = len(arrs)

    def body(*refs):
        ins, outs = refs[:m], refs[m:2 * m]
        send_sems, recv_sems = refs[2 * m:]
        x, y, c, _ = _place()
        sibling = (x, y, 1 - c)
        cps = []
        for a in range(m):
            src = ins[a].at[:, 1 - c] if a < n else ins[a]
            cp = pltpu.make_async_remote_copy(src_ref=src, dst_ref=outs[a], send_sem=send_sems.at[a],
                                              recv_sem=recv_sems.at[a], device_id=sibling, device_id_type=MESH)
            cp.start()
            cps.append(cp)
        for cp in cps:
            cp.wait_recv()
        for cp in cps:
            cp.wait_send()

    outs = tuple(jax.ShapeDtypeStruct((g.shape[0],) + g.shape[2:], g.dtype) for g in grads)
    if small is not None:
        outs += (jax.ShapeDtypeStruct(small.shape, small.dtype),)
    return pl.pallas_call(
        body, name=name, out_shape=outs, in_specs=_hbm_specs(m), out_specs=tuple(_hbm_specs(m)),
        scratch_shapes=[pltpu.SemaphoreType.DMA((m,)), pltpu.SemaphoreType.DMA((m,))],
    )(*arrs)


def _pair_sum(g, recv, place, name):
    _, _, h, cc = g.shape
    th = _row_tile(h)

    def body(c_ref, g_ref, r_ref, o_ref):
        o_ref[...] = (g_ref[...].astype(F32) + r_ref[...].astype(F32)).astype(o_ref.dtype)

    return pl.pallas_call(
        body, name=name, out_shape=jax.ShapeDtypeStruct(recv.shape, recv.dtype),
        grid_spec=pltpu.PrefetchScalarGridSpec(
            num_scalar_prefetch=1, grid=(N_CHIPS, h // th),
            in_specs=[pl.BlockSpec((None, None, th, cc), lambda k, r, c_ref: (k, c_ref[0], r, 0)),
                      pl.BlockSpec((None, th, cc), lambda k, r, c_ref: (k, r, 0))],
            out_specs=pl.BlockSpec((None, th, cc), lambda k, r, c_ref: (k, r, 0))),
        compiler_params=_cp(("arbitrary", "arbitrary")),
    )(place, g, recv)


def _small_sum(a, b):
    def body(a_ref, b_ref, o_ref):
        o_ref[...] = a_ref[...] + b_ref[...]

    return pl.pallas_call(body, name="small_pair_sum", out_shape=jax.ShapeDtypeStruct(a.shape, a.dtype),
                          compiler_params=_cp())(a, b)


class _ExchangeRider:
    has_mid = False

    def __init__(self, sums, small=None):
        self.n = len(sums)
        self.inputs = list(sums) + ([small] if small is not None else [])
        self.out_shapes = [jax.ShapeDtypeStruct((3,) + g.shape[1:], g.dtype) for g in sums]
        self.hs = 0
        if small is not None:
            self.hs = small.shape[0] // 2
            self.out_shapes.append(jax.ShapeDtypeStruct((N_CHIPS, self.hs, small.shape[1]), small.dtype))
        m = len(self.inputs)
        self.aliases = {}
        self.sems = [pltpu.SemaphoreType.DMA((3 * m,)), pltpu.SemaphoreType.DMA((3 * m,)), pltpu.SemaphoreType.DMA]

    def _copies(self, ins, outs, sems):
        send_sems, recv_sems, local_sem = sems
        x, y, c, chips = _place()
        me_k = 2 * x + y
        local, cps = None, []
        if self.hs:
            mine = ins[self.n].at[pl.ds(c * self.hs, self.hs)]
            local = pltpu.make_async_copy(mine, outs[self.n].at[me_k], local_sem)
        for j, (px, py) in enumerate(chips):
            for a in range(len(ins)):
                src, dst = (ins[a].at[2 * px + py], outs[a].at[j]) if a < self.n else (mine, outs[a].at[me_k])
                cps.append(pltpu.make_async_remote_copy(
                    src_ref=src, dst_ref=dst, send_sem=send_sems.at[a * 3 + j], recv_sem=recv_sems.at[a * 3 + j],
                    device_id=(px, py, c), device_id_type=MESH))
        return local, cps

    def start(self, ins, outs, sems):
        local, cps = self._copies(ins, outs, sems)
        if local is not None:
            local.start()
        for cp in cps:
            cp.start()

    def mid(self, ins, outs, sems):
        pass

    def end(self, ins, outs, sems):
        local, cps = self._copies(ins, outs, sems)
        for cp in cps:
            cp.wait_recv()
        for cp in cps:
            cp.wait_send()
        if local is not None:
            local.wait()


def _chip_sum(own, parts, place, name):
    npart, h, cc = parts.shape
    th = _row_tile(h)

    def body(*refs):
        p_ref, o_ref = refs[-2], refs[-1]
        acc = p_ref[0].astype(F32)
        if own is not None:
            acc = refs[1][...].astype(F32) + acc
        for k in range(1, npart):
            acc = acc + p_ref[k].astype(F32)
        o_ref[...] = acc

    in_specs = [pl.BlockSpec((npart, th, cc), lambda r, pr: (0, r, 0))]
    args = [parts]
    if own is not None:
        in_specs.insert(0, pl.BlockSpec((None, th, cc), lambda r, pr: (pr[1], r, 0)))
        args.insert(0, own)
    return pl.pallas_call(
        body, name=name, out_shape=jax.ShapeDtypeStruct((2, h, cc), F32),
        grid_spec=pltpu.PrefetchScalarGridSpec(
            num_scalar_prefetch=1, grid=(h // th,), in_specs=in_specs,
            out_specs=pl.BlockSpec((None, th, cc), lambda r, pr: (pr[0], r, 0))),
        compiler_params=_cp(("arbitrary",)),
    )(place, *args)


def _share_halves(halves):
    n = len(halves)

    def body(*refs):
        outs = refs[n:2 * n]
        send_sems, recv_sems = refs[2 * n:]
        x, y, c, _ = _place()
        cps = []
        for a in range(n):
            cp = pltpu.make_async_remote_copy(src_ref=outs[a].at[c], dst_ref=outs[a].at[c], send_sem=send_sems.at[a],
                                              recv_sem=recv_sems.at[a], device_id=(x, y, 1 - c), device_id_type=MESH)
            cp.start()
            cps.append(cp)
        for a in range(n):
            pltpu.make_async_remote_copy(src_ref=outs[a].at[1 - c], dst_ref=outs[a].at[1 - c], send_sem=send_sems.at[a],
                                         recv_sem=recv_sems.at[a], device_id=(x, y, 1 - c),
                                         device_id_type=MESH).wait_recv()
        for cp in cps:
            cp.wait_send()

    return pl.pallas_call(
        body, name="share_halves", out_shape=tuple(jax.ShapeDtypeStruct(g.shape, g.dtype) for g in halves),
        in_specs=_hbm_specs(n), out_specs=tuple(_hbm_specs(n)), input_output_aliases={a: a for a in range(n)},
        scratch_shapes=[pltpu.SemaphoreType.DMA((n,)), pltpu.SemaphoreType.DMA((n,))],
    )(*halves)


def _adamw_math(w, g, m, v):
    m = ADAM_B1 * m + (1.0 - ADAM_B1) * g
    v = ADAM_B2 * v + (1.0 - ADAM_B2) * (g * g)
    m_hat = m / (1.0 - ADAM_B1 ** ADAM_STEP)
    v_hat = v / (1.0 - ADAM_B2 ** ADAM_STEP)
    delta = -ADAM_LR * (m_hat / (jnp.sqrt(v_hat) + ADAM_EPS) + ADAM_WD * w)
    return delta, m, v


def _adamw_big(w, g, m, v, name):
    r, cc = w.shape
    tr = min(_row_tile(r), 256) if r % 256 == 0 else _row_tile(r)

    def body(w_ref, g_ref, m_ref, v_ref, d_ref, mo_ref, vo_ref):
        d, mm, vv = _adamw_math(w_ref[...], g_ref[...], m_ref[...], v_ref[...])
        d_ref[...] = d
        mo_ref[...] = mm
        vo_ref[...] = vv

    blk = pl.BlockSpec((tr, cc), lambda i: (i, 0))
    sd = jax.ShapeDtypeStruct((r, cc), F32)
    return pl.pallas_call(body, grid=(r // tr,), name=name, out_shape=(sd, sd, sd), in_specs=[blk] * 4,
                          out_specs=(blk, blk, blk), compiler_params=_cp(("arbitrary",)))(w, g, m, v)


def _adamw_small(ws, gs, ms, vs):
    n = len(ws)

    def body(*refs):
        for a in range(n):
            w_ref, g_ref, m_ref, v_ref = refs[4 * a:4 * a + 4]
            d_ref, mo_ref, vo_ref = refs[4 * n + 3 * a:4 * n + 3 * a + 3]
            d, mm, vv = _adamw_math(w_ref[...], g_ref[...], m_ref[...], v_ref[...])
            d_ref[...] = d
            mo_ref[...] = mm
            vo_ref[...] = vv

    args, outs = [], []
    for a in range(n):
        args += [ws[a], gs[a], ms[a], vs[a]]
        outs += [jax.ShapeDtypeStruct(ws[a].shape, F32)] * 3
    res = pl.pallas_call(body, name="adamw_small", out_shape=tuple(outs), compiler_params=_cp())(*args)
    return [res[3 * a:3 * a + 3] for a in range(n)]


def _flat_pack(arrs, rows):
    flat = jnp.concatenate([a.reshape(-1) for a in arrs])
    return jnp.pad(flat, (0, rows * D - flat.shape[0])).reshape(rows, D)


def _flat_unpack(flat, shapes):
    out, off = [], 0
    for shp in shapes:
        size = 1
        for d_ in shp:
            size *= d_
        out.append(flat[off:off + size].reshape(shp))
        off += size
    return out


SMALL_EVEN = ("even_pre_g", "even_a_ln_g", "even_a_ln_b", "even_a_ws", "even_a_bs", "even_b_conv", "even_mem_g",
              "even_post_g")
SMALL_ODD = ("odd_pre_g", "odd_c_wgrp", "odd_c_scale", "odd_d_dw_w", "odd_d_dw_b", "odd_d_ln_g", "odd_d_ln_b",
             "odd_d_pw_b", "odd_mem_g", "odd_post_g")
BIG = ("even_w_in", "even_w_kv", "even_w_out", "odd_w_in", "odd_d_pw_w", "odd_w_kv", "odd_w_out")
WEIGHTS = ("even_pre_g", "even_w_in", "even_a_ln_g", "even_a_ln_b", "even_a_ws", "even_a_bs", "even_b_conv",
           "even_mem_g", "even_w_kv", "even_w_out", "even_post_g", "odd_pre_g", "odd_w_in", "odd_c_wgrp",
           "odd_c_scale", "odd_d_dw_w", "odd_d_dw_b", "odd_d_ln_g", "odd_d_ln_b", "odd_d_pw_w", "odd_d_pw_b",
           "odd_mem_g", "odd_w_kv", "odd_w_out", "odd_post_g")
PACKED = (("even_b_conv", (3, 192)), ("odd_pre_g", (1, 256)), ("odd_c_scale", (1, 192)), ("odd_d_dw_w", (31, 192)),
          ("odd_d_dw_b", (1, 192)), ("odd_d_ln_g", (1, 192)), ("odd_d_ln_b", (1, 192)), ("odd_d_pw_b", (1, 192)),
          ("odd_mem_g", (1, 256)), ("odd_post_g", (1, 256)))
PACK_ROWS = 16
SMALL_ROWS = 256


def _four(g):
    return g.reshape(N_CHIPS, 2, g.shape[0] // (2 * N_CHIPS), g.shape[1])


def _step(x, mem, target, w, place):
    wt = {}
    pack = _flat_pack([w[n][0] for n, _ in PACKED], PACK_ROWS)
    shards = {"even_w_in_t": w["even_w_in"][0].T, "odd_w_in_t": w["odd_w_in"][0].T, "even_w_kv": w["even_w_kv"][0],
              "odd_w_kv": w["odd_w_kv"][0], "even_w_out": w["even_w_out"][0], "odd_w_out": w["odd_w_out"][0],
              "odd_d_pw_w": w["odd_d_pw_w"][0]}
    placed = {n: _place_shard(a, place, BF16, "place_" + n) for n, a in shards.items()}
    placed["pack"] = _place_shard(pack, place, F32, "place_pack")

    wt["even_w_in_t"], packs = _run_rider(_GatherRider([placed["even_w_in_t"], placed["pack"]]), "gather_first")
    packs = packs.reshape(N_CHIPS, PACK_ROWS * D)
    per_chip = [_flat_unpack(packs[k], [shp for _, shp in PACKED]) for k in range(N_CHIPS)]
    for a, (name, _) in enumerate(PACKED):
        wt[name] = jnp.concatenate([per_chip[k][a] for k in range(N_CHIPS)], axis=-1)
    for name in ("even_pre_g", "even_a_ln_g", "even_a_ln_b", "even_mem_g", "even_post_g"):
        wt[name] = w[name]

    tril = jnp.tril(jnp.ones((CH, CH), dtype=bool))
    wcat = jnp.where(tril[None], w["even_a_ws"][0], 0.0).transpose(1, 0, 2).reshape(CH, 4 * CH).astype(BF16)
    bsg = jnp.repeat(w["even_a_bs"][0].T, BW // 4, axis=1)
    hsel = (jnp.arange(BW)[:, None] // (BW // 4) == jnp.arange(128)[None, :]).astype(BF16)
    wg = w["odd_c_wgrp"][0]
    g4 = BW // 4
    wbd = jnp.zeros((BW, BW), F32)
    for g in range(4):
        wbd = lax.dynamic_update_slice(wbd, wg[g], (g * g4, g * g4))
    wbd = wbd.astype(BF16)

    names = ("even_w_kv", "even_w_out", "odd_w_kv", "odd_d_pw_w")
    (p_e, h_e), got = _in_fwd(x, wt["even_pre_g"], wt["even_w_in_t"], "even_in",
                              rider=_GatherRider([placed[n] for n in names]))
    wt.update(zip(names, got))
    kv_e = _kv_fwd(mem, wt["even_mem_g"], wt["even_w_kv"], "even_kv")
    (x1, o_e), got = _even_fwd(x, p_e, kv_e, wt["even_a_ln_g"], wt["even_a_ln_b"], wcat, bsg, wt["even_b_conv"],
                               wt["even_w_out"], wt["even_post_g"], rider=_GatherRider([placed["odd_w_in_t"]]))
    wt["odd_w_in_t"] = got[0]
    (p_o, h_o), got = _in_fwd(x1, wt["odd_pre_g"], wt["odd_w_in_t"], "odd_in",
                              rider=_GatherRider([placed["odd_w_out"]]))
    wt["odd_w_out"] = got[0]
    kv_o = _kv_fwd(mem, wt["odd_mem_g"], wt["odd_w_kv"], "odd_kv")
    dx2, o_o, cv_o, loss = _odd_fwd(x1, p_o, kv_o, wbd, wt["odd_c_scale"], wt["odd_d_dw_w"], wt["odd_d_dw_b"],
                                    wt["odd_d_ln_g"], wt["odd_d_ln_b"], wt["odd_d_pw_w"], wt["odd_d_pw_b"],
                                    wt["odd_w_out"], wt["odd_post_g"], target)
    (dpc_o, tmpc, tmpd, do_o, y_o, g_post_o, g_cs, g_wbd, g_dww, g_dwb, g_lng_o, g_lnb_o, g_pww, g_pwb,
     dkv_o) = _odd_bwd1(dx2, o_o, cv_o, p_o, kv_o, wbd, wt["odd_c_scale"], wt["odd_d_dw_w"], wt["odd_d_dw_b"],
                        wt["odd_d_ln_g"], wt["odd_d_ln_b"], wt["odd_d_pw_w"], wt["odd_d_pw_b"], wt["odd_w_out"],
                        wt["odd_post_g"])
    dpb_o, dx1, g_pre_o = _odd_bwd2(dpc_o, tmpc, tmpd, p_o, wt["odd_d_dw_w"], wt["odd_w_in_t"], x1,
                                    wt["odd_pre_g"], dx2)
    g_win_o = _grad_tn(dpb_o, h_o, 768, rows=ODD_IN, name="odd_gw_in_b")
    g_win_o = _grad_tn(dpc_o, h_o, 256, out=g_win_o, rows=ODD_IN, row0=3 * BW, name="odd_gw_in_c")
    g_wout_o = _grad_tn(y_o, do_o, 1024, name="odd_gw_out")
    g_wkv_o, g_memg_o = _kv_bwd(mem, wt["odd_mem_g"], wt["odd_w_kv"], dkv_o, "odd_kv_bwd")
    big_o = [_four(g) for g in (g_win_o, g_pww.astype(BF16), g_wkv_o, g_wout_o)]
    recv_o = _swap_halves(big_o, None, "swap_halves_odd")
    sums_o = [_pair_sum(big_o[a], recv_o[a], place, "pair_sum_odd_%d" % a) for a in range(len(big_o))]
    (dpa_e, dpc_e, tmp_e, do_e, y_e, g_post_e, g_lng_e, g_lnb_e, g_wcat, g_bs, g_bconv,
     dkv_e), parts_o = _even_bwd1(dx1, o_e, p_e, kv_e, wt["even_a_ln_g"], wt["even_a_ln_b"], wcat, bsg, hsel,
                                  wt["even_b_conv"], wt["even_w_out"], wt["even_post_g"],
                                  rider=_ExchangeRider(sums_o))
    halves_o = [_chip_sum(sums_o[a], parts_o[a], place, "chip_sum_odd_%d" % a) for a in range(len(big_o))]
    dpb_e, dx0, g_pre_e = _even_bwd2(dpa_e, dpc_e, tmp_e, p_e, wt["even_b_conv"], wt["even_w_in_t"], x,
                                     wt["even_pre_g"], dx1)
    g_win_e = _grad_tn(dpa_e, h_e, 768, rows=EVEN_IN, name="even_gw_in_a")
    g_win_e = _grad_tn(dpb_e, h_e, 768, out=g_win_e, rows=EVEN_IN, row0=3 * BW, name="even_gw_in_b")
    g_win_e = _grad_tn(dpc_e, h_e, 1280, out=g_win_e, rows=EVEN_IN, row0=5 * BW, name="even_gw_in_c")
    g_wout_e = _grad_tn(y_e, do_e, 1024, name="even_gw_out")
    g_wkv_e, g_memg_e = _kv_bwd(mem, wt["even_mem_g"], wt["even_w_kv"], dkv_e, "even_kv_bwd")

    g_aws = jnp.where(tril[None], g_wcat.reshape(CH, 4, CH).transpose(1, 0, 2), 0.0)
    g_wgrp = jnp.stack([lax.dynamic_slice(g_wbd, (g * g4, g * g4), (g4, g4)) for g in range(4)])
    small = {
        "even_pre_g": g_pre_e, "even_a_ln_g": g_lng_e, "even_a_ln_b": g_lnb_e, "even_a_ws": g_aws,
        "even_a_bs": g_bs[:, 0:4].T, "even_b_conv": g_bconv[0:3], "even_mem_g": g_memg_e, "even_post_g": g_post_e,
        "odd_pre_g": g_pre_o, "odd_c_wgrp": g_wgrp, "odd_c_scale": g_cs, "odd_d_dw_w": g_dww.reshape(CONF, 8, BW).sum(axis=1),
        "odd_d_dw_b": g_dwb, "odd_d_ln_g": g_lng_o, "odd_d_ln_b": g_lnb_o, "odd_d_pw_b": g_pwb,
        "odd_mem_g": g_memg_o, "odd_post_g": g_post_o,
    }
    small_names = SMALL_EVEN + SMALL_ODD
    small_pack = _flat_pack([small[n] for n in small_names] + [loss[0, 0].reshape(1)], SMALL_ROWS)
    big_e = [_four(g) for g in (g_win_e, g_wkv_e, g_wout_e)]
    recv_e = _swap_halves(big_e, small_pack, "swap_halves_even")
    sums_e = [_pair_sum(big_e[a], recv_e[a], place, "pair_sum_even_%d" % a) for a in range(len(big_e))]
    small_sum = _small_sum(small_pack, recv_e[-1])
    parts_e = _run_rider(_ExchangeRider(sums_e, small_sum), "exchange_even")
    halves_e = [_chip_sum(sums_e[a], parts_e[a], place, "chip_sum_even_%d" % a) for a in range(len(big_e))]
    half_small = _chip_sum(None, parts_e[-1], place, "chip_sum_small")
    full = _share_halves(halves_e + halves_o + [half_small])
    order = ("even_w_in", "even_w_kv", "even_w_out", "odd_w_in", "odd_d_pw_w", "odd_w_kv", "odd_w_out")
    gbig = {n: full[a].reshape(full[a].shape[1] * 2, full[a].shape[2]) for a, n in enumerate(order)}
    return dx0, gbig, full[-1].reshape(-1), [small[n].shape for n in small_names]


def kernel(x, mem, even_pre_g, even_w_in, even_a_ln_g, even_a_ln_b, even_a_ws, even_a_bs, even_b_conv, even_mem_g, even_w_kv, even_w_out, even_post_g, odd_pre_g, odd_w_in, odd_c_wgrp, odd_c_scale, odd_d_dw_w, odd_d_dw_b, odd_d_ln_g, odd_d_ln_b, odd_d_pw_w, odd_d_pw_b, odd_mem_g, odd_w_kv, odd_w_out, odd_post_g, loss_target, m_even_pre_g, m_even_w_in, m_even_a_ln_g, m_even_a_ln_b, m_even_a_ws, m_even_a_bs, m_even_b_conv, m_even_mem_g, m_even_w_kv, m_even_w_out, m_even_post_g, m_odd_pre_g, m_odd_w_in, m_odd_c_wgrp, m_odd_c_scale, m_odd_d_dw_w, m_odd_d_dw_b, m_odd_d_ln_g, m_odd_d_ln_b, m_odd_d_pw_w, m_odd_d_pw_b, m_odd_mem_g, m_odd_w_kv, m_odd_w_out, m_odd_post_g, v_even_pre_g, v_even_w_in, v_even_a_ln_g, v_even_a_ln_b, v_even_a_ws, v_even_a_bs, v_even_b_conv, v_even_mem_g, v_even_w_kv, v_even_w_out, v_even_post_g, v_odd_pre_g, v_odd_w_in, v_odd_c_wgrp, v_odd_c_scale, v_odd_d_dw_w, v_odd_d_dw_b, v_odd_d_ln_g, v_odd_d_ln_b, v_odd_d_pw_w, v_odd_d_pw_b, v_odd_mem_g, v_odd_w_kv, v_odd_w_out, v_odd_post_g):
    given = dict(locals())
    w = {n: given[n] for n in WEIGHTS}
    mom = {n: given["m_" + n] for n in WEIGHTS}
    var = {n: given["v_" + n] for n in WEIGHTS}

    x_, y_, c_ = lax.axis_index("x"), lax.axis_index("y"), lax.axis_index("c")
    chip = 2 * x_ + y_
    place = jnp.stack([c_, chip]).astype(jnp.int32)
    grad_x, gbig, gsmall_flat, small_shapes = _step(x[0], mem[0], loss_target[0], w, place)

    names = SMALL_EVEN + SMALL_ODD
    grads = {}
    unpacked = _flat_unpack(gsmall_flat, small_shapes + [(1,)])
    loss = unpacked[-1][0]
    for n, g in zip(names, unpacked[:-1]):
        shard_shape = w[n].shape[1:]
        if g.shape[-1] != shard_shape[-1]:
            g = lax.dynamic_slice_in_dim(g, chip * shard_shape[-1], shard_shape[-1], axis=g.ndim - 1)
        grads[n] = g.reshape(shard_shape)

    def two_d(a):
        return a.reshape(-1, a.shape[-1])

    upd = {}
    for n in BIG:
        if n.endswith("w_in"):
            res = _adamw_big(w[n][0].T, gbig[n], mom[n][0].T, var[n][0].T, "adamw_" + n)
            grads[n] = gbig[n].T
            upd[n] = tuple(r.T for r in res)
        else:
            grads[n] = gbig[n]
            upd[n] = _adamw_big(w[n][0], gbig[n], mom[n][0], var[n][0], "adamw_" + n)
    res = _adamw_small([two_d(w[n][0]) for n in names], [two_d(grads[n]) for n in names],
                       [two_d(mom[n][0]) for n in names], [two_d(var[n][0]) for n in names])
    for n, r in zip(names, res):
        upd[n] = r

    outs = [loss, grad_x[None]]
    outs += [grads[n].reshape(w[n].shape) for n in WEIGHTS]
    for j in range(3):
        outs += [upd[n][j].reshape(w[n].shape) for n in WEIGHTS]
    return tuple(outs)
```

```python
import functools

import jax
import jax.numpy as jnp
from jax import lax
from jax.experimental import pallas as pl
from jax.experimental.pallas import tpu as pltpu

F32 = jnp.float32
BF16 = jnp.bfloat16
MESH = pl.DeviceIdType.MESH

D = 1024
N_MEM = 256
MIX = 2048
XA = 512
HD = 128
BW = 768
CH = 128
EPS = 1e-6
SCALE = HD ** -0.5
POOL_WINDOWS = (2, 4, 8, 16)
CONF = 31
EVEN_IN = 6400
ODD_IN = 4864
N_CHIPS = 4

ADAM_LR = 0.001
ADAM_B1 = 0.9
ADAM_B2 = 0.999
ADAM_EPS = 1e-08
ADAM_WD = 0.01
ADAM_STEP = 10

TS = 256
HALO = 32
VMEM_LIMIT = 56 * 1024 * 1024


def _cp(sem=None):
    return pltpu.CompilerParams(dimension_semantics=sem, vmem_limit_bytes=VMEM_LIMIT)


def _dot(a, b):
    return jnp.dot(a, b, preferred_element_type=F32)


def _dot_nt(a, b):
    return lax.dot_general(a, b, (((1,), (1,)), ((), ())), preferred_element_type=F32)


def _dot_tn(a, b):
    return lax.dot_general(a, b, (((0,), (0,)), ((), ())), preferred_element_type=F32)


def _sigmoid(x):
    return 1.0 / (1.0 + jnp.exp(-x))


def _resident(shape):
    return pl.BlockSpec(shape, lambda *_: (0,) * len(shape), pipeline_mode=pl.Buffered(1))


def _const(shape):
    return pl.BlockSpec(shape, lambda *_: (0,) * len(shape))


def _kv_fwd(mem, mem_g, wkv, name):
    def body(mem_ref, g_ref, w_ref, kv_ref):
        m = mem_ref[...]
        r = lax.rsqrt(jnp.mean(m * m, axis=-1, keepdims=True) + EPS)
        mn = (m * r * g_ref[...]).astype(BF16)
        kv_ref[...] = _dot(mn, w_ref[...]).astype(BF16)

    return pl.pallas_call(body, out_shape=jax.ShapeDtypeStruct((N_MEM, D), BF16), name=name,
                          compiler_params=_cp())(mem, mem_g, wkv)


def _kv_bwd(mem, mem_g, wkv, dkv, name):
    def body(mem_ref, g_ref, w_ref, dkv_ref, dw_ref, dg_ref):
        m = mem_ref[...]
        r = lax.rsqrt(jnp.mean(m * m, axis=-1, keepdims=True) + EPS)
        mh = m * r
        mn = (mh * g_ref[...]).astype(BF16)
        dkv = dkv_ref[...].astype(BF16)
        dw_ref[...] = _dot_tn(mn, dkv).astype(BF16)
        dmn = _dot_nt(dkv, w_ref[...])
        dg_ref[...] = jnp.sum(dmn * mh, axis=0, keepdims=True)

    return pl.pallas_call(body, out_shape=(jax.ShapeDtypeStruct((D, D), BF16), jax.ShapeDtypeStruct((1, D), F32)),
                          name=name, compiler_params=_cp())(mem, mem_g, wkv, dkv)


def _host_call(body, *, grid, name, out_shape, in_specs, out_specs, args, scratch_shapes=(), rider=None):
    if rider is None:
        res = pl.pallas_call(body, grid=grid, name=name, out_shape=tuple(out_shape), in_specs=list(in_specs),
                             out_specs=tuple(out_specs), scratch_shapes=list(scratch_shapes),
                             compiler_params=_cp(("arbitrary",)))(*args)
        return tuple(res), ()
    n_in, n_out, n_sc = len(in_specs), len(out_specs), len(scratch_shapes)
    r_in, r_out = len(rider.inputs), len(rider.out_shapes)
    last = grid[0] - 1

    def full_body(*refs):
        host_in = refs[:n_in]
        rid_in = refs[n_in:n_in + r_in]
        host_out = refs[n_in + r_in:n_in + r_in + n_out]
        rid_out = refs[n_in + r_in + n_out:n_in + r_in + n_out + r_out]
        host_sc = refs[n_in + r_in + n_out + r_out:n_in + r_in + n_out + r_out + n_sc]
        sems = refs[n_in + r_in + n_out + r_out + n_sc:]
        i = pl.program_id(0)

        @pl.when(i == 0)
        def _():
            rider.start(rid_in, rid_out, sems)

        if rider.has_mid:
            @pl.when(i == last)
            def _():
                rider.mid(rid_in, rid_out, sems)

        body(*host_in, *host_out, *host_sc)

        @pl.when(i == last)
        def _():
            rider.end(rid_in, rid_out, sems)

    res = pl.pallas_call(
        full_body, grid=grid, name=name, out_shape=tuple(out_shape) + tuple(rider.out_shapes),
        in_specs=list(in_specs) + _hbm_specs(r_in), out_specs=tuple(out_specs) + tuple(_hbm_specs(r_out)),
        scratch_shapes=list(scratch_shapes) + list(rider.sems),
        input_output_aliases={n_in + j: n_out + k for j, k in rider.aliases.items()},
        compiler_params=_cp(("arbitrary",)),
    )(*args, *rider.inputs)
    return tuple(res[:n_out]), tuple(res[n_out:])


def _in_fwd(x, pre_g, w_t, name, rider=None):
    s, n = x.shape[0], w_t.shape[0]
    tm = min(512, s)
    nc = 256

    def body(x_ref, g_ref, w_ref, p_ref, h_ref):
        xv = x_ref[...]
        r = lax.rsqrt(jnp.mean(xv * xv, axis=-1, keepdims=True) + EPS)
        h = (xv * r * g_ref[...]).astype(BF16)
        h_ref[...] = h
        for j in range(n // nc):
            p_ref[:, j * nc:(j + 1) * nc] = _dot_nt(h, w_ref[j * nc:(j + 1) * nc, :]).astype(BF16)

    return _host_call(
        body, grid=(s // tm,), name=name, rider=rider,
        out_shape=(jax.ShapeDtypeStruct((s, n), BF16), jax.ShapeDtypeStruct((s, D), BF16)),
        in_specs=[pl.BlockSpec((tm, D), lambda i: (i, 0)), _const((1, D)), _resident((n, D))],
        out_specs=(pl.BlockSpec((tm, n), lambda i: (i, 0)), pl.BlockSpec((tm, D), lambda i: (i, 0))),
        args=(x, pre_g, w_t))


def _xattn_fwd(q, kv_ref):
    outs, probs = [], []
    for h in range(XA // HD):
        qh = q[:, h * HD:(h + 1) * HD]
        kh = kv_ref[:, h * HD:(h + 1) * HD]
        vh = kv_ref[:, XA + h * HD:XA + (h + 1) * HD]
        sc = _dot_nt(qh, kh) * SCALE
        e = jnp.exp(sc - jnp.max(sc, axis=-1, keepdims=True))
        pr = e / jnp.sum(e, axis=-1, keepdims=True)
        outs.append(_dot(pr.astype(BF16), vh))
        probs.append(pr)
    return jnp.concatenate(outs, axis=-1), probs


def _xattn_bwd(dyx, q, probs, kv_ref, dkv_ref):
    dqs = []
    for h in range(XA // HD):
        qh = q[:, h * HD:(h + 1) * HD]
        kh = kv_ref[:, h * HD:(h + 1) * HD]
        vh = kv_ref[:, XA + h * HD:XA + (h + 1) * HD]
        dy = dyx[:, h * HD:(h + 1) * HD].astype(BF16)
        pr = probs[h]
        dp = _dot_nt(dy, vh)
        ds = (pr * (dp - jnp.sum(dp * pr, axis=-1, keepdims=True))).astype(BF16)
        dqs.append(_dot(ds, kh) * SCALE)
        dkv_ref[:, h * HD:(h + 1) * HD] += _dot_tn(ds, qh) * SCALE
        dkv_ref[:, XA + h * HD:XA + (h + 1) * HD] += _dot_tn(pr.astype(BF16), dy)
    return jnp.concatenate(dqs, axis=-1)


def _layer_norm_fwd(v, g, b):
    mu = jnp.mean(v, axis=-1, keepdims=True)
    vc = v - mu
    rstd = lax.rsqrt(jnp.mean(vc * vc, axis=-1, keepdims=True) + EPS)
    vhat = vc * rstd
    return vhat * g + b, vhat, rstd


def _layer_norm_bwd(dy, vhat, rstd, g):
    dvh = dy * g
    return rstd * (dvh - jnp.mean(dvh, axis=-1, keepdims=True) - vhat * jnp.mean(dvh * vhat, axis=-1, keepdims=True))


def _head_masks():
    col = lax.broadcasted_iota(jnp.int32, (1, BW), 1)
    return [(col >= h * (BW // 4)) & (col < (h + 1) * (BW // 4)) for h in range(4)]


def _halo_prev(nblk_per_tile):
    return lambda i: (jnp.maximum(i * nblk_per_tile - 1, 0), 0)


def _row_ids(i, t):
    return i * t + lax.broadcasted_iota(jnp.int32, (t, 1), 0)


def _even_mix(i, p_ref, ph_ref, ln_g, ln_b, wcat_ref, bsg_ref, bconv_ref, wbuf):
    t = p_ref.shape[0]
    u = p_ref[:, 0:BW].astype(F32)
    v = p_ref[:, BW:2 * BW].astype(F32)
    bg = p_ref[:, 2 * BW:3 * BW].astype(F32)
    cg = p_ref[:, 3 * BW:4 * BW].astype(F32)
    xin = p_ref[:, 4 * BW:5 * BW].astype(F32)
    vn, vhat, rstd = _layer_norm_fwd(v, ln_g, ln_b)
    masks = _head_masks()
    sgs, vsts = [], []
    for n in range(t // CH):
        vn_c = vn[n * CH:(n + 1) * CH]
        vst = jnp.concatenate([jnp.where(m, vn_c, 0.0) for m in masks], axis=0).astype(BF16)
        sgs.append(_dot(wcat_ref[...], vst) + bsg_ref[...])
        vsts.append(vst)
    sg = jnp.concatenate(sgs, axis=0)
    ya = u * sg
    w_halo = ph_ref[:, 3 * BW:4 * BW].astype(F32) * ph_ref[:, 4 * BW:5 * BW].astype(F32)
    wbuf[0:HALO, :] = jnp.where(i > 0, w_halo, 0.0)
    wbuf[HALO:HALO + t, :] = cg * xin
    conv = (bconv_ref[0:1, :] * wbuf[pl.ds(HALO - 2, t), :] + bconv_ref[1:2, :] * wbuf[pl.ds(HALO - 1, t), :]
            + bconv_ref[2:3, :] * wbuf[pl.ds(HALO, t), :])
    yb = bg * conv
    return dict(u=u, bg=bg, vhat=vhat, rstd=rstd, sg=sg, vsts=vsts, conv=conv, ya=ya, yb=yb, masks=masks)


def _pool_select(vals):
    col = lax.broadcasted_iota(jnp.int32, (1, BW), 1)
    g = BW // 4
    return jnp.where(col < g, vals[0], jnp.where(col < 2 * g, vals[1], jnp.where(col < 3 * g, vals[2], vals[3])))


def _inv_counts(i, t):
    rows = _row_ids(i, t) + 1
    return [1.0 / jnp.minimum(rows, w).astype(F32) for w in POOL_WINDOWS]


def _band_matrices(t, forward):
    j = jnp.arange(t)[:, None]
    r = jnp.arange(HALO + t)[None, :]
    if forward:
        return jnp.stack([(r >= j) & (r < j + w) for w in POOL_WINDOWS]).astype(BF16)
    return jnp.stack([(r <= HALO + j) & (r > HALO + j - w) for w in POOL_WINDOWS]).astype(BF16)


SHIFT_ROWS = HALO + TS - 8


def _shifted_copies(buf, sh):
    for b in range(1, 8):
        sh[b - 1] = buf[pl.ds(b, SHIFT_ROWS), :]


def _rows_at(buf, sh, off, t):
    a, b = divmod(off, 8)
    return buf[pl.ds(8 * a, t), :] if b == 0 else sh[b - 1, pl.ds(8 * a, t), :]


def _tap_sums(d_ref, buf, sh, base, out_ref):
    t = d_ref.shape[0]
    group = 4
    for k0 in range(0, CONF, group):
        taps = list(range(k0, min(k0 + group, CONF)))

        def step(r, accs, taps=taps):
            row = pl.multiple_of(r * 8, 8)
            d = d_ref[pl.ds(row, 8), :]
            new = []
            for acc, k in zip(accs, taps):
                a, b = divmod(base + k, 8)
                src = buf[pl.ds(row + 8 * a, 8), :] if b == 0 else sh[b - 1, pl.ds(row + 8 * a, 8), :]
                new.append(acc + d * src)
            return tuple(new)

        accs = lax.fori_loop(0, t // 8, step, tuple(jnp.zeros((8, BW), F32) for _ in taps), unroll=2)
        for acc, k in zip(accs, taps):
            out_ref[8 * k:8 * k + 8, :] += acc


def _odd_mix(i, p_ref, ph_ref, bands_ref, wbd_ref, cscale, dww_ref, dwb, ln_g, ln_b, pww_ref, pwb, gbuf, gsh,
             cv=None):
    t = p_ref.shape[0]
    zc_bf = p_ref[:, 0:BW]
    zc = zc_bf.astype(F32)
    ga = p_ref[:, BW:2 * BW].astype(F32)
    gb = p_ref[:, 2 * BW:3 * BW].astype(F32)
    zh = ph_ref[:, 0:BW]
    zcat = jnp.concatenate([jnp.where(i > 0, zh, jnp.zeros_like(zh)), zc_bf], axis=0)
    inv = _inv_counts(i, t)
    pooled = _pool_select([_dot(bands_ref[w], zcat) * inv[w] for w in range(len(POOL_WINDOWS))]) - zc
    pooled_bf = pooled.astype(BF16)
    pre = _dot(pooled_bf, wbd_ref[...])
    yc = pre * cscale
    sgb = _sigmoid(gb)
    z = ga * sgb
    gh_a = ph_ref[:, BW:2 * BW].astype(F32)
    gh_b = ph_ref[:, 2 * BW:3 * BW].astype(F32)
    gbuf[0:HALO, :] = jnp.where(i > 0, gh_a * _sigmoid(gh_b), 0.0)
    gbuf[HALO:HALO + t, :] = z
    _shifted_copies(gbuf, gsh)
    if cv is None:
        cv = dwb + dww_ref[CONF - 1:CONF, :] * z
        for k in range(CONF - 1):
            cv = cv + dww_ref[k:k + 1, :] * _rows_at(gbuf, gsh, HALO - (CONF - 1) + k, t)
    zl, zhat, rstd = _layer_norm_fwd(cv, ln_g, ln_b)
    szl = _sigmoid(zl)
    zs = (zl * szl).astype(BF16)
    yd = _dot(zs, pww_ref[...]) + pwb
    return dict(ga=ga, sgb=sgb, pooled_bf=pooled_bf, pre=pre, yc=yc, zhat=zhat, rstd=rstd, zl=zl, szl=szl,
                zs=zs, yd=yd, inv=inv, cv=cv)


def _post_norm(o, post_g):
    r = lax.rsqrt(jnp.mean(o * o, axis=-1, keepdims=True) + EPS)
    return o * r, r


def _gate_out(y_a, y_b, y_x, gate, wout_ref):
    sgt = _sigmoid(gate)
    sgate = gate * sgt
    ys = [(y_a * sgate[:, 0:BW]).astype(BF16), (y_b * sgate[:, BW:2 * BW]).astype(BF16),
          (y_x * sgate[:, 2 * BW:MIX]).astype(BF16)]
    o = (_dot(ys[0], wout_ref[0:BW, :]) + _dot(ys[1], wout_ref[BW:2 * BW, :]) + _dot(ys[2], wout_ref[2 * BW:MIX, :]))
    return o, ys, sgt, sgate


def _tile_specs(s, n):
    nh = TS // HALO
    return pl.BlockSpec((TS, n), lambda i: (i, 0)), pl.BlockSpec((HALO, n), _halo_prev(nh))


def _even_fwd(x, p, kv, ln_g, ln_b, wcat, bsg, bconv, wout, post_g, rider=None):
    s = x.shape[0]

    def body(x_ref, p_ref, ph_ref, kv_ref, lng, lnb, wcat_ref, bsg_ref, bconv_ref, wout_ref, pg, x1_ref, o_ref, wbuf):
        i = pl.program_id(0)
        mx = _even_mix(i, p_ref, ph_ref, lng[...], lnb[...], wcat_ref, bsg_ref, bconv_ref, wbuf)
        yx, _ = _xattn_fwd(p_ref[:, 5 * BW:5 * BW + XA], kv_ref)
        gate = p_ref[:, 5 * BW + XA:EVEN_IN].astype(F32)
        o, _, _, _ = _gate_out(mx["ya"], mx["yb"], yx, gate, wout_ref)
        n, _ = _post_norm(o, pg[...])
        o_ref[...] = o
        x1_ref[...] = x_ref[...] + n * pg[...]

    tile, halo = _tile_specs(s, EVEN_IN)
    row = pl.BlockSpec((TS, D), lambda i: (i, 0))
    return _host_call(
        body, grid=(s // TS,), name="even_fwd", rider=rider,
        out_shape=(jax.ShapeDtypeStruct((s, D), F32), jax.ShapeDtypeStruct((s, D), F32)),
        in_specs=[row, tile, halo, _const((N_MEM, D)), _const((1, BW)), _const((1, BW)), _const((CH, 4 * CH)),
                  _const((CH, BW)), _const((3, BW)), _resident((MIX, D)), _const((1, D))],
        out_specs=(row, row),
        scratch_shapes=[pltpu.VMEM((HALO + TS, BW), F32)],
        args=(x, p, p, kv, ln_g, ln_b, wcat, bsg, bconv, wout, post_g))


def _odd_fwd(x1, p, kv, wbd, cscale, dww, dwb, ln_g, ln_b, pww, pwb, wout, post_g, target):
    s = x1.shape[0]

    def body(x_ref, p_ref, ph_ref, kv_ref, bands_ref, wbd_ref, cs, dww_ref, dwb_ref, lng, lnb, pww_ref, pwb_ref,
             wout_ref, pg, tgt_ref, dx_ref, o_ref, cv_ref, loss_ref, gbuf, gsh):
        i = pl.program_id(0)
        mx = _odd_mix(i, p_ref, ph_ref, bands_ref, wbd_ref, cs[...], dww_ref, dwb_ref[...], lng[...], lnb[...],
                      pww_ref, pwb_ref[...], gbuf, gsh)
        cv_ref[...] = mx["cv"]
        yx, _ = _xattn_fwd(p_ref[:, 3 * BW:3 * BW + XA], kv_ref)
        gate = p_ref[:, 3 * BW + XA:ODD_IN].astype(F32)
        o, _, _, _ = _gate_out(mx["yc"], mx["yd"], yx, gate, wout_ref)
        n, _ = _post_norm(o, pg[...])
        o_ref[...] = o
        err = x_ref[...] + n * pg[...] - tgt_ref[...]
        dx_ref[...] = err * (1.0 / D)

        @pl.when(i == 0)
        def _():
            loss_ref[...] = jnp.zeros_like(loss_ref)

        loss_ref[...] += 0.5 * jnp.sum(jnp.sum(err * err, axis=-1, keepdims=True) * (1.0 / D), axis=0, keepdims=True)

    tile, halo = _tile_specs(s, ODD_IN)
    row = pl.BlockSpec((TS, D), lambda i: (i, 0))
    vec = _const((1, BW))
    return pl.pallas_call(
        body, grid=(s // TS,), name="odd_fwd",
        out_shape=(jax.ShapeDtypeStruct((s, D), F32), jax.ShapeDtypeStruct((s, D), F32),
                   jax.ShapeDtypeStruct((s, BW), F32), jax.ShapeDtypeStruct((8, 128), F32)),
        in_specs=[row, tile, halo, _const((N_MEM, D)), _const((4, TS, HALO + TS)), _const((BW, BW)), vec,
                  _const((CONF, BW)), vec, vec, vec, _const((BW, BW)), vec, _resident((MIX, D)), _const((1, D)), row],
        out_specs=(row, row, pl.BlockSpec((TS, BW), lambda i: (i, 0)), _const((8, 128))),
        scratch_shapes=[pltpu.VMEM((HALO + TS, BW), F32), pltpu.VMEM((7, SHIFT_ROWS, BW), F32)],
        compiler_params=_cp(("arbitrary",)),
    )(x1, p, p, kv, _band_matrices(TS, False), wbd, cscale, dww, dwb, ln_g, ln_b, pww, pwb, wout, post_g, target)


def _acc_init(i, refs):
    @pl.when(i == 0)
    def _():
        for r in refs:
            r[...] = jnp.zeros_like(r)


def _post_norm_bwd(dx, o, pg, dpg_ref):
    n, r = _post_norm(o, pg)
    dpg_ref[...] += jnp.sum(dx * n, axis=0, keepdims=True)
    dn = dx * pg
    return (r * (dn - n * jnp.mean(dn * n, axis=-1, keepdims=True))).astype(BF16)


def _gate_bwd(do, wout_ref, ys_f32, gate, y_ref):
    dy = _dot_nt(do, wout_ref[...])
    sgt = _sigmoid(gate)
    sgate = gate * sgt
    dsilu = sgt * (1.0 + gate * (1.0 - sgt))
    offs = (0, BW, 2 * BW, MIX)
    dys, dgs = [], []
    for j, yv in enumerate(ys_f32):
        a, b = offs[j], offs[j + 1]
        y_ref[:, a:b] = (yv * sgate[:, a:b]).astype(BF16)
        dys.append(dy[:, a:b] * sgate[:, a:b])
        dgs.append(dy[:, a:b] * yv * dsilu[:, a:b])
    return dys, jnp.concatenate(dgs, axis=-1)


def _even_bwd1(dx, o, p, kv, ln_g, ln_b, wcat, bsg, hsel, bconv, wout, post_g, rider=None):
    s = dx.shape[0]

    def body(dx_ref, o_ref, p_ref, ph_ref, kv_ref, lng, lnb, wcat_ref, bsg_ref, hsel_ref, bconv_ref, wout_ref, pg,
             dpa_ref, dpc_ref, tmp_ref, do_ref, y_ref, dpg_ref, dlng_ref, dlnb_ref, dwcat_ref, dbs_ref, dbconv_ref,
             dkv_ref, wbuf):
        i = pl.program_id(0)
        _acc_init(i, (dpg_ref, dlng_ref, dlnb_ref, dwcat_ref, dbs_ref, dbconv_ref, dkv_ref))
        mx = _even_mix(i, p_ref, ph_ref, lng[...], lnb[...], wcat_ref, bsg_ref, bconv_ref, wbuf)
        q = p_ref[:, 5 * BW:5 * BW + XA]
        yx, probs = _xattn_fwd(q, kv_ref)
        gate = p_ref[:, 5 * BW + XA:EVEN_IN].astype(F32)
        do = _post_norm_bwd(dx_ref[...], o_ref[...], pg[...], dpg_ref)
        do_ref[...] = do
        (dya, dyb, dyx), dgate = _gate_bwd(do, wout_ref, (mx["ya"], mx["yb"], yx), gate, y_ref)
        dpa_ref[:, 0:BW] = (dya * mx["sg"]).astype(BF16)
        dsg = (dya * mx["u"]).astype(BF16)
        dvns = []
        for n in range(TS // CH):
            dsg_c = dsg[n * CH:(n + 1) * CH]
            dvst = _dot_tn(wcat_ref[...], dsg_c)
            dvn_c = jnp.where(mx["masks"][0], dvst[0:CH], 0.0)
            for h in range(1, 4):
                dvn_c = dvn_c + jnp.where(mx["masks"][h], dvst[h * CH:(h + 1) * CH], 0.0)
            dvns.append(dvn_c)
            dwcat_ref[...] += _dot_nt(dsg_c, mx["vsts"][n])
            dbs_ref[...] += _dot(dsg_c, hsel_ref[...])
        dvn = jnp.concatenate(dvns, axis=0)
        dlng_ref[...] += jnp.sum(dvn * mx["vhat"], axis=0, keepdims=True)
        dlnb_ref[...] += jnp.sum(dvn, axis=0, keepdims=True)
        dpa_ref[:, BW:2 * BW] = _layer_norm_bwd(dvn, mx["vhat"], mx["rstd"], lng[...]).astype(BF16)
        dpa_ref[:, 2 * BW:3 * BW] = (dyb * mx["conv"]).astype(BF16)
        dconv = dyb * mx["bg"]
        tmp_ref[...] = dconv.astype(BF16)
        for k in range(3):
            dbconv_ref[k:k + 1, :] += jnp.sum(dconv * wbuf[pl.ds(HALO - 2 + k, TS), :], axis=0, keepdims=True)
        dpc_ref[:, 0:XA] = _xattn_bwd(dyx, q, probs, kv_ref, dkv_ref).astype(BF16)
        dpc_ref[:, XA:XA + MIX] = dgate.astype(BF16)

    tile, halo = _tile_specs(s, EVEN_IN)
    row = pl.BlockSpec((TS, D), lambda i: (i, 0))
    vec = _const((1, BW))

    def out(n):
        return pl.BlockSpec((TS, n), lambda i: (i, 0))

    return _host_call(
        body, grid=(s // TS,), name="even_bwd1", rider=rider,
        out_shape=(jax.ShapeDtypeStruct((s, 3 * BW), BF16), jax.ShapeDtypeStruct((s, XA + MIX), BF16),
                   jax.ShapeDtypeStruct((s, BW), BF16), jax.ShapeDtypeStruct((s, D), BF16),
                   jax.ShapeDtypeStruct((s, MIX), BF16),
                   jax.ShapeDtypeStruct((1, D), F32), jax.ShapeDtypeStruct((1, BW), F32),
                   jax.ShapeDtypeStruct((1, BW), F32), jax.ShapeDtypeStruct((CH, 4 * CH), F32),
                   jax.ShapeDtypeStruct((CH, 128), F32), jax.ShapeDtypeStruct((8, BW), F32),
                   jax.ShapeDtypeStruct((N_MEM, D), F32)),
        in_specs=[row, row, tile, halo, _const((N_MEM, D)), vec, vec, _const((CH, 4 * CH)), _const((CH, BW)),
                  _const((BW, 128)), _const((3, BW)), _resident((MIX, D)), _const((1, D))],
        out_specs=(out(3 * BW), out(XA + MIX), out(BW), out(D), out(MIX),
                   _const((1, D)), vec, vec, _const((CH, 4 * CH)), _const((CH, 128)), _const((8, BW)),
                   _const((N_MEM, D))),
        scratch_shapes=[pltpu.VMEM((HALO + TS, BW), F32)],
        args=(dx, o, p, p, kv, ln_g, ln_b, wcat, bsg, hsel, bconv, wout, post_g))


def _odd_bwd1(dx, o, cv, p, kv, wbd, cscale, dww, dwb, ln_g, ln_b, pww, pwb, wout, post_g):
    s = dx.shape[0]

    def body(dx_ref, o_ref, cv_ref, p_ref, ph_ref, kv_ref, bands_ref, wbd_ref, cs, dww_ref, dwb_ref, lng, lnb,
             pww_ref, pwb_ref, wout_ref, pg,
             dpc_ref, tmpc_ref, tmpd_ref, do_ref, y_ref, dpg_ref, dcs_ref, dwbd_ref, ddww_ref, ddwb_ref, dlng_ref,
             dlnb_ref, dpww_ref, dpwb_ref, dkv_ref, gbuf, gsh, dcv_buf):
        i = pl.program_id(0)
        _acc_init(i, (dpg_ref, dcs_ref, dwbd_ref, ddww_ref, ddwb_ref, dlng_ref, dlnb_ref, dpww_ref, dpwb_ref,
                      dkv_ref))
        mx = _odd_mix(i, p_ref, ph_ref, bands_ref, wbd_ref, cs[...], dww_ref, dwb_ref[...], lng[...], lnb[...],
                      pww_ref, pwb_ref[...], gbuf, gsh, cv=cv_ref[...])
        q = p_ref[:, 3 * BW:3 * BW + XA]
        yx, probs = _xattn_fwd(q, kv_ref)
        gate = p_ref[:, 3 * BW + XA:ODD_IN].astype(F32)
        do = _post_norm_bwd(dx_ref[...], o_ref[...], pg[...], dpg_ref)
        do_ref[...] = do
        (dyc, dyd, dyx), dgate = _gate_bwd(do, wout_ref, (mx["yc"], mx["yd"], yx), gate, y_ref)
        dcs_ref[...] += jnp.sum(dyc * mx["pre"], axis=0, keepdims=True)
        dpre = (dyc * cs[...]).astype(BF16)
        dwbd_ref[...] += _dot_tn(mx["pooled_bf"], dpre)
        dpooled = _dot_nt(dpre, wbd_ref[...])
        tmpc_ref[...] = _pool_select([dpooled * c_ for c_ in mx["inv"]]).astype(BF16)
        dyd_bf = dyd.astype(BF16)
        dpwb_ref[...] += jnp.sum(dyd, axis=0, keepdims=True)
        dpww_ref[...] += _dot_tn(mx["zs"], dyd_bf)
        dzs = _dot_nt(dyd_bf, pww_ref[...])
        zl, szl = mx["zl"], mx["szl"]
        dzl = dzs * (szl * (1.0 + zl * (1.0 - szl)))
        dlng_ref[...] += jnp.sum(dzl * mx["zhat"], axis=0, keepdims=True)
        dlnb_ref[...] += jnp.sum(dzl, axis=0, keepdims=True)
        dcv = _layer_norm_bwd(dzl, mx["zhat"], mx["rstd"], lng[...])
        tmpd_ref[...] = dcv.astype(BF16)
        ddwb_ref[...] += jnp.sum(dcv, axis=0, keepdims=True)
        dcv_buf[...] = dcv
        _tap_sums(dcv_buf, gbuf, gsh, HALO - (CONF - 1), ddww_ref)
        dpc_ref[:, 0:XA] = _xattn_bwd(dyx, q, probs, kv_ref, dkv_ref).astype(BF16)
        dpc_ref[:, XA:XA + MIX] = dgate.astype(BF16)

    tile, halo = _tile_specs(s, ODD_IN)
    row = pl.BlockSpec((TS, D), lambda i: (i, 0))
    vec = _const((1, BW))

    def out(n):
        return pl.BlockSpec((TS, n), lambda i: (i, 0))

    return pl.pallas_call(
        body, grid=(s // TS,), name="odd_bwd1",
        out_shape=(jax.ShapeDtypeStruct((s, XA + MIX), BF16), jax.ShapeDtypeStruct((s, BW), BF16),
                   jax.ShapeDtypeStruct((s, BW), BF16), jax.ShapeDtypeStruct((s, D), BF16),
                   jax.ShapeDtypeStruct((s, MIX), BF16),
                   jax.ShapeDtypeStruct((1, D), F32), jax.ShapeDtypeStruct((1, BW), F32),
                   jax.ShapeDtypeStruct((BW, BW), F32), jax.ShapeDtypeStruct((8 * CONF, BW), F32),
                   jax.ShapeDtypeStruct((1, BW), F32), jax.ShapeDtypeStruct((1, BW), F32),
                   jax.ShapeDtypeStruct((1, BW), F32), jax.ShapeDtypeStruct((BW, BW), F32),
                   jax.ShapeDtypeStruct((1, BW), F32), jax.ShapeDtypeStruct((N_MEM, D), F32)),
        in_specs=[row, row, out(BW), tile, halo, _const((N_MEM, D)), _const((4, TS, HALO + TS)), _const((BW, BW)), vec,
                  _const((CONF, BW)), vec, vec, vec, _const((BW, BW)), vec, _resident((MIX, D)), _const((1, D))],
        out_specs=(out(XA + MIX), out(BW), out(BW), out(D), out(MIX),
                   _const((1, D)), vec, _const((BW, BW)), _const((8 * CONF, BW)), vec, vec, vec, _const((BW, BW)), vec,
                   _const((N_MEM, D))),
        scratch_shapes=[pltpu.VMEM((HALO + TS, BW), F32), pltpu.VMEM((7, SHIFT_ROWS, BW), F32),
                        pltpu.VMEM((TS, BW), F32)],
        compiler_params=_cp(("arbitrary",)),
    )(dx, o, cv, p, p, kv, _band_matrices(TS, False), wbd, cscale, dww, dwb, ln_g, ln_b, pww, pwb, wout, post_g)


def _halo_next(nblk_per_tile, nblk):
    return lambda i: (jnp.minimum((i + 1) * nblk_per_tile, nblk - 1), 0)


def _pre_norm_bwd(dh, x, pre_g, dres, dpre_ref):
    r = lax.rsqrt(jnp.mean(x * x, axis=-1, keepdims=True) + EPS)
    xh = x * r
    dpre_ref[...] += jnp.sum(dh * xh, axis=0, keepdims=True)
    dxh = dh * pre_g
    return dres + r * (dxh - xh * jnp.mean(dxh * xh, axis=-1, keepdims=True))


def _even_bwd2(dpa, dpc, tmp, p, bconv, w_t, x, pre_g, dres, rider=None):
    s = x.shape[0]
    nt = s // TS

    def body(dpa_ref, dpc_ref, tmp_ref, tmph_ref, cg_ref, xin_ref, bconv_ref, w_ref, x_ref, pg, dres_ref,
             dpb_ref, dx_ref, dpre_ref, dbuf):
        i = pl.program_id(0)
        _acc_init(i, (dpre_ref,))
        dbuf[0:TS, :] = tmp_ref[...].astype(F32)
        dbuf[TS:TS + HALO, :] = jnp.where(i < nt - 1, tmph_ref[...].astype(F32), 0.0)
        dw = (bconv_ref[2:3, :] * dbuf[pl.ds(0, TS), :] + bconv_ref[1:2, :] * dbuf[pl.ds(1, TS), :]
              + bconv_ref[0:1, :] * dbuf[pl.ds(2, TS), :])
        dcg = (dw * xin_ref[...].astype(F32)).astype(BF16)
        dxin = (dw * cg_ref[...].astype(F32)).astype(BF16)
        dpb_ref[:, 0:BW] = dcg
        dpb_ref[:, BW:2 * BW] = dxin
        dh = (_dot(dpa_ref[...], w_ref[0:3 * BW, :]) + _dot(dcg, w_ref[3 * BW:4 * BW, :])
              + _dot(dxin, w_ref[4 * BW:5 * BW, :]) + _dot(dpc_ref[...], w_ref[5 * BW:EVEN_IN, :]))
        dx_ref[...] = _pre_norm_bwd(dh, x_ref[...], pg[...], dres_ref[...], dpre_ref)

    row = pl.BlockSpec((TS, D), lambda i: (i, 0))

    def tile(n, j=0):
        return pl.BlockSpec((TS, n), lambda i: (i, j))

    return _host_call(
        body, grid=(nt,), name="even_bwd2", rider=rider,
        out_shape=(jax.ShapeDtypeStruct((s, 2 * BW), BF16), jax.ShapeDtypeStruct((s, D), F32),
                   jax.ShapeDtypeStruct((1, D), F32)),
        in_specs=[tile(3 * BW), tile(XA + MIX), tile(BW), pl.BlockSpec((HALO, BW), _halo_next(TS // HALO, s // HALO)),
                  tile(BW, 3), tile(BW, 4), _const((3, BW)), _resident((EVEN_IN, D)), row, _const((1, D)), row],
        out_specs=(tile(2 * BW), row, _const((1, D))),
        scratch_shapes=[pltpu.VMEM((TS + HALO, BW), F32)],
        args=(dpa, dpc, tmp, tmp, p, p, bconv, w_t, x, pre_g, dres))


def _odd_bwd2(dpc, tmpc, tmpd, p, dww, w_t, x, pre_g, dres):
    s = x.shape[0]
    nt = s // TS

    def body(dpc_ref, tc_ref, tch_ref, td_ref, tdh_ref, ga_ref, gb_ref, bands_ref, dww_ref, w_ref, x_ref, pg,
             dres_ref, dpb_ref, dx_ref, dpre_ref, dbuf, dsh):
        i = pl.program_id(0)
        _acc_init(i, (dpre_ref,))
        more = i < nt - 1
        e_bf = tc_ref[...]
        eh = tch_ref[...]
        ecat = jnp.concatenate([e_bf, jnp.where(more, eh, jnp.zeros_like(eh))], axis=0)
        dbuf[0:TS, :] = td_ref[...].astype(F32)
        dbuf[TS:TS + HALO, :] = jnp.where(more, tdh_ref[...].astype(F32), 0.0)
        sums = [_dot(bands_ref[w], ecat) for w in range(len(POOL_WINDOWS))]
        rows = _row_ids(i, TS) + 1
        cnt = _pool_select([jnp.minimum(rows, w).astype(F32) for w in POOL_WINDOWS])
        dzc = (_pool_select(sums) - e_bf.astype(F32) * cnt).astype(BF16)
        _shifted_copies(dbuf, dsh)
        dz = dww_ref[CONF - 1:CONF, :] * dbuf[pl.ds(0, TS), :]
        for sft in range(1, CONF):
            dz = dz + dww_ref[CONF - 1 - sft:CONF - sft, :] * _rows_at(dbuf, dsh, sft, TS)
        ga = ga_ref[...].astype(F32)
        sgb = _sigmoid(gb_ref[...].astype(F32))
        dga = (dz * sgb).astype(BF16)
        dgb = (dz * ga * sgb * (1.0 - sgb)).astype(BF16)
        dpb_ref[:, 0:BW] = dzc
        dpb_ref[:, BW:2 * BW] = dga
        dpb_ref[:, 2 * BW:3 * BW] = dgb
        dh = (_dot(dzc, w_ref[0:BW, :]) + _dot(dga, w_ref[BW:2 * BW, :]) + _dot(dgb, w_ref[2 * BW:3 * BW, :])
              + _dot(dpc_ref[...], w_ref[3 * BW:ODD_IN, :]))
        dx_ref[...] = _pre_norm_bwd(dh, x_ref[...], pg[...], dres_ref[...], dpre_ref)

    row = pl.BlockSpec((TS, D), lambda i: (i, 0))

    def tile(n, j=0):
        return pl.BlockSpec((TS, n), lambda i: (i, j))

    nxt = pl.BlockSpec((HALO, BW), _halo_next(TS // HALO, s // HALO))
    return pl.pallas_call(
        body, grid=(nt,), name="odd_bwd2",
        out_shape=(jax.ShapeDtypeStruct((s, 3 * BW), BF16), jax.ShapeDtypeStruct((s, D), F32),
                   jax.ShapeDtypeStruct((1, D), F32)),
        in_specs=[tile(XA + MIX), tile(BW), nxt, tile(BW), nxt, tile(BW, 1), tile(BW, 2), _const((4, TS, HALO + TS)),
                  _const((CONF, BW)), _resident((ODD_IN, D)), row, _const((1, D)), row],
        out_specs=(tile(3 * BW), row, _const((1, D))),
        scratch_shapes=[pltpu.VMEM((TS + HALO, BW), F32), pltpu.VMEM((7, SHIFT_ROWS, BW), F32)],
        compiler_params=_cp(("arbitrary",)),
    )(dpc, tmpc, tmpc, tmpd, tmpd, p, p, _band_matrices(TS, True), dww, w_t, x, pre_g, dres)


def _grad_tn(a, b, tm, out=None, rows=None, row0=0, name="grad_tn"):
    s, m = a.shape
    n = b.shape[1]
    ts = min(2048, s)
    rows = m if rows is None else rows
    assert m % tm == 0 and s % ts == 0
    ns = s // ts
    if row0 % tm == 0:
        out_spec = pl.BlockSpec((tm, n), lambda i, k: (row0 // tm + i, 0))
    else:
        align = 16
        assert row0 % align == 0 and tm % align == 0
        out_spec = pl.BlockSpec((pl.Element(tm), pl.Element(n)),
                                lambda i, k: (pl.multiple_of(row0 + i * tm, align), 0))

    def body(*refs):
        a_ref, b_ref = refs[0], refs[1]
        o_ref, acc = refs[-2], refs[-1]
        k = pl.program_id(1)

        @pl.when(k == 0)
        def _():
            acc[...] = jnp.zeros_like(acc)

        acc[...] += _dot_tn(a_ref[...], b_ref[...])

        @pl.when(k == ns - 1)
        def _():
            o_ref[...] = acc[...].astype(BF16)

    in_specs = [pl.BlockSpec((ts, tm), lambda i, k: (k, i)), pl.BlockSpec((ts, n), lambda i, k: (k, 0))]
    args = [a, b]
    aliases = {}
    if out is not None:
        in_specs.append(pl.BlockSpec(memory_space=pltpu.HBM))
        args.append(out)
        aliases = {2: 0}
    return pl.pallas_call(
        body, grid=(m // tm, ns), name=name,
        out_shape=jax.ShapeDtypeStruct((rows, n), BF16),
        in_specs=in_specs, out_specs=out_spec,
        scratch_shapes=[pltpu.VMEM((tm, n), F32)], input_output_aliases=aliases,
        compiler_params=_cp(("arbitrary", "arbitrary")),
    )(*args)


def _place():
    x, y, c = lax.axis_index("x"), lax.axis_index("y"), lax.axis_index("c")
    chips = [(1 - x, y), (x, 1 - y), (1 - x, 1 - y)]
    return x, y, c, chips


def _hbm_specs(n):
    return [pl.BlockSpec(memory_space=pltpu.HBM)] * n


def _row_tile(r):
    for cand in (512, 400, 304, 256, 192, 128, 96, 16):
        if r % cand == 0:
            return cand
    raise ValueError(r)


def _place_shard(shard, place, dtype, name):
    r, cc = shard.shape
    tr = _row_tile(r)
    nt = r // tr

    def body(place_ref, s_ref, o_ref):
        o_ref[...] = s_ref[...].astype(dtype)

    return pl.pallas_call(
        body, name=name, out_shape=jax.ShapeDtypeStruct((N_CHIPS * r, cc), dtype),
        grid_spec=pltpu.PrefetchScalarGridSpec(
            num_scalar_prefetch=1, grid=(nt,),
            in_specs=[pl.BlockSpec((tr, cc), lambda i, pr: (i, 0))],
            out_specs=pl.BlockSpec((tr, cc), lambda i, pr: (pr[1] * nt + i, 0))),
        compiler_params=_cp(("arbitrary",)),
    )(place, shard)


class _GatherRider:
    has_mid = True

    def __init__(self, fulls):
        n = len(fulls)
        self.inputs = list(fulls)
        self.out_shapes = [jax.ShapeDtypeStruct(a.shape, a.dtype) for a in fulls]
        self.aliases = {a: a for a in range(n)}
        self.sems = [pltpu.SemaphoreType.DMA((6 * n,)), pltpu.SemaphoreType.DMA((6 * n,))]
        self.block_rows = [a.shape[0] // N_CHIPS for a in fulls]

    def _ctx(self, outs, sems):
        send_sems, recv_sems = sems
        x, y, c, chips = _place()

        def rows(a, k, half):
            r = self.block_rows[a]
            return outs[a].at[pl.ds(k * r + half * (r // 2), r // 2)]

        def copy(a, j, blk, to):
            return pltpu.make_async_remote_copy(src_ref=blk, dst_ref=blk, send_sem=send_sems.at[a * 6 + j],
                                                recv_sem=recv_sems.at[a * 6 + j], device_id=to, device_id_type=MESH)

        return x, y, c, chips, rows, copy

    def start(self, ins, outs, sems):
        x, y, c, chips, rows, copy = self._ctx(outs, sems)
        for j, (px, py) in enumerate(chips):
            for a in range(len(outs)):
                copy(a, j, rows(a, 2 * x + y, c), (px, py, c)).start()

    def mid(self, ins, outs, sems):
        x, y, c, chips, rows, copy = self._ctx(outs, sems)
        for j, (px, py) in enumerate(chips):
            for a in range(len(outs)):
                copy(a, j, rows(a, 2 * px + py, c), (px, py, c)).wait_recv()
                copy(a, 3 + j, rows(a, 2 * px + py, c), (x, y, 1 - c)).start()

    def end(self, ins, outs, sems):
        x, y, c, chips, rows, copy = self._ctx(outs, sems)
        for j, (px, py) in enumerate(chips):
            for a in range(len(outs)):
                copy(a, 3 + j, rows(a, 2 * px + py, 1 - c), (x, y, 1 - c)).wait_recv()
        for j, (px, py) in enumerate(chips):
            for a in range(len(outs)):
                copy(a, j, rows(a, 2 * x + y, c), (px, py, c)).wait_send()
                copy(a, 3 + j, rows(a, 2 * px + py, c), (x, y, 1 - c)).wait_send()


def _run_rider(rider, name):
    r_in, r_out = len(rider.inputs), len(rider.out_shapes)

    def body(*refs):
        ins, outs, sems = refs[:r_in], refs[r_in:r_in + r_out], refs[r_in + r_out:]
        rider.start(ins, outs, sems)
        rider.mid(ins, outs, sems)
        rider.end(ins, outs, sems)

    return pl.pallas_call(
        body, name=name, out_shape=tuple(rider.out_shapes), in_specs=_hbm_specs(r_in),
        out_specs=tuple(_hbm_specs(r_out)), input_output_aliases=dict(rider.aliases),
        scratch_shapes=list(rider.sems),
    )(*rider.inputs)


def _swap_halves(grads, small, name):
    n = len(grads)
    arrs = list(grads) + ([small] if small is not None else [])
    m = len(arrs)

    def body(*refs):
        ins, outs = refs[:m], refs[m:2 * m]
        send_sems, recv_sems = refs[2 * m:]
        x, y, c, _ = _place()
        sibling = (x, y, 1 - c)
        cps = []
        for a in range(m):
            src = ins[a].at[:, 1 - c] if a < n else ins[a]
            cp = pltpu.make_async_remote_copy(src_ref=src, dst_ref=outs[a], send_sem=send_sems.at[a],
                                              recv_sem=recv_sems.at[a], device_id=sibling, device_id_type=MESH)
            cp.start()
            cps.append(cp)
        for cp in cps:
            cp.wait_recv()
        for cp in cps:
            cp.wait_send()

    outs = tuple(jax.ShapeDtypeStruct((g.shape[0],) + g.shape[2:], g.dtype) for g in grads)
    if small is not None:
        outs += (jax.ShapeDtypeStruct(small.shape, small.dtype),)
    return pl.pallas_call(
        body, name=name, out_shape=outs, in_specs=_hbm_specs(m), out_specs=tuple(_hbm_specs(m)),
        scratch_shapes=[pltpu.SemaphoreType.DMA((m,)), pltpu.SemaphoreType.DMA((m,))],
    )(*arrs)


def _pair_sum(g, recv, place, name):
    _, _, h, cc = g.shape
    th = _row_tile(h)

    def body(c_ref, g_ref, r_ref, o_ref):
        o_ref[...] = (g_ref[...].astype(F32) + r_ref[...].astype(F32)).astype(o_ref.dtype)

    return pl.pallas_call(
        body, name=name, out_shape=jax.ShapeDtypeStruct(recv.shape, recv.dtype),
        grid_spec=pltpu.PrefetchScalarGridSpec(
            num_scalar_prefetch=1, grid=(N_CHIPS, h // th),
            in_specs=[pl.BlockSpec((None, None, th, cc), lambda k, r, c_ref: (k, c_ref[0], r, 0)),
                      pl.BlockSpec((None, th, cc), lambda k, r, c_ref: (k, r, 0))],
            out_specs=pl.BlockSpec((None, th, cc), lambda k, r, c_ref: (k, r, 0))),
        compiler_params=_cp(("arbitrary", "arbitrary")),
    )(place, g, recv)


def _small_sum(a, b):
    def body(a_ref, b_ref, o_ref):
        o_ref[...] = a_ref[...] + b_ref[...]

    return pl.pallas_call(body, name="small_pair_sum", out_shape=jax.ShapeDtypeStruct(a.shape, a.dtype),
                          compiler_params=_cp())(a, b)


class _ExchangeRider:
    has_mid = False

    def __init__(self, sums, small=None):
        self.n = len(sums)
        self.inputs = list(sums) + ([small] if small is not None else [])
        self.out_shapes = [jax.ShapeDtypeStruct((3,) + g.shape[1:], g.dtype) for g in sums]
        self.hs = 0
        if small is not None:
            self.hs = small.shape[0] // 2
            self.out_shapes.append(jax.ShapeDtypeStruct((N_CHIPS, self.hs, small.shape[1]), small.dtype))
        m = len(self.inputs)
        self.aliases = {}
        self.sems = [pltpu.SemaphoreType.DMA((3 * m,)), pltpu.SemaphoreType.DMA((3 * m,)), pltpu.SemaphoreType.DMA]

    def _copies(self, ins, outs, sems):
        send_sems, recv_sems, local_sem = sems
        x, y, c, chips = _place()
        me_k = 2 * x + y
        local, cps = None, []
        if self.hs:
            mine = ins[self.n].at[pl.ds(c * self.hs, self.hs)]
            local = pltpu.make_async_copy(mine, outs[self.n].at[me_k], local_sem)
        for j, (px, py) in enumerate(chips):
            for a in range(len(ins)):
                src, dst = (ins[a].at[2 * px + py], outs[a].at[j]) if a < self.n else (mine, outs[a].at[me_k])
                cps.append(pltpu.make_async_remote_copy(
                    src_ref=src, dst_ref=dst, send_sem=send_sems.at[a * 3 + j], recv_sem=recv_sems.at[a * 3 + j],
                    device_id=(px, py, c), device_id_type=MESH))
        return local, cps

    def start(self, ins, outs, sems):
        local, cps = self._copies(ins, outs, sems)
        if local is not None:
            local.start()
        for cp in cps:
            cp.start()

    def mid(self, ins, outs, sems):
        pass

    def end(self, ins, outs, sems):
        local, cps = self._copies(ins, outs, sems)
        for cp in cps:
            cp.wait_recv()
        for cp in cps:
            cp.wait_send()
        if local is not None:
            local.wait()


def _chip_sum(own, parts, place, name):
    npart, h, cc = parts.shape
    th = _row_tile(h)

    def body(*refs):
        p_ref, o_ref = refs[-2], refs[-1]
        acc = p_ref[0].astype(F32)
        if own is not None:
            acc = refs[1][...].astype(F32) + acc
        for k in range(1, npart):
            acc = acc + p_ref[k].astype(F32)
        o_ref[...] = acc

    in_specs = [pl.BlockSpec((npart, th, cc), lambda r, pr: (0, r, 0))]
    args = [parts]
    if own is not None:
        in_specs.insert(0, pl.BlockSpec((None, th, cc), lambda r, pr: (pr[1], r, 0)))
        args.insert(0, own)
    return pl.pallas_call(
        body, name=name, out_shape=jax.ShapeDtypeStruct((2, h, cc), F32),
        grid_spec=pltpu.PrefetchScalarGridSpec(
            num_scalar_prefetch=1, grid=(h // th,), in_specs=in_specs,
            out_specs=pl.BlockSpec((None, th, cc), lambda r, pr: (pr[0], r, 0))),
        compiler_params=_cp(("arbitrary",)),
    )(place, *args)


def _share_halves(halves):
    n = len(halves)

    def body(*refs):
        outs = refs[n:2 * n]
        send_sems, recv_sems = refs[2 * n:]
        x, y, c, _ = _place()
        cps = []
        for a in range(n):
            cp = pltpu.make_async_remote_copy(src_ref=outs[a].at[c], dst_ref=outs[a].at[c], send_sem=send_sems.at[a],
                                              recv_sem=recv_sems.at[a], device_id=(x, y, 1 - c), device_id_type=MESH)
            cp.start()
            cps.append(cp)
        for a in range(n):
            pltpu.make_async_remote_copy(src_ref=outs[a].at[1 - c], dst_ref=outs[a].at[1 - c], send_sem=send_sems.at[a],
                                         recv_sem=recv_sems.at[a], device_id=(x, y, 1 - c),
                                         device_id_type=MESH).wait_recv()
        for cp in cps:
            cp.wait_send()

    return pl.pallas_call(
        body, name="share_halves", out_shape=tuple(jax.ShapeDtypeStruct(g.shape, g.dtype) for g in halves),
        in_specs=_hbm_specs(n), out_specs=tuple(_hbm_specs(n)), input_output_aliases={a: a for a in range(n)},
        scratch_shapes=[pltpu.SemaphoreType.DMA((n,)), pltpu.SemaphoreType.DMA((n,))],
    )(*halves)


def _adamw_math(w, g, m, v):
    m = ADAM_B1 * m + (1.0 - ADAM_B1) * g
    v = ADAM_B2 * v + (1.0 - ADAM_B2) * (g * g)
    m_hat = m / (1.0 - ADAM_B1 ** ADAM_STEP)
    v_hat = v / (1.0 - ADAM_B2 ** ADAM_STEP)
    delta = -ADAM_LR * (m_hat / (jnp.sqrt(v_hat) + ADAM_EPS) + ADAM_WD * w)
    return delta, m, v


def _adamw_big(w, g, m, v, name):
    r, cc = w.shape
    tr = min(_row_tile(r), 256) if r % 256 == 0 else _row_tile(r)

    def body(w_ref, g_ref, m_ref, v_ref, d_ref, mo_ref, vo_ref):
        d, mm, vv = _adamw_math(w_ref[...], g_ref[...], m_ref[...], v_ref[...])
        d_ref[...] = d
        mo_ref[...] = mm
        vo_ref[...] = vv

    blk = pl.BlockSpec((tr, cc), lambda i: (i, 0))
    sd = jax.ShapeDtypeStruct((r, cc), F32)
    return pl.pallas_call(body, grid=(r // tr,), name=name, out_shape=(sd, sd, sd), in_specs=[blk] * 4,
                          out_specs=(blk, blk, blk), compiler_params=_cp(("arbitrary",)))(w, g, m, v)


def _adamw_small(ws, gs, ms, vs):
    n = len(ws)

    def body(*refs):
        for a in range(n):
            w_ref, g_ref, m_ref, v_ref = refs[4 * a:4 * a + 4]
            d_ref, mo_ref, vo_ref = refs[4 * n + 3 * a:4 * n + 3 * a + 3]
            d, mm, vv = _adamw_math(w_ref[...], g_ref[...], m_ref[...], v_ref[...])
            d_ref[...] = d
            mo_ref[...] = mm
            vo_ref[...] = vv

    args, outs = [], []
    for a in range(n):
        args += [ws[a], gs[a], ms[a], vs[a]]
        outs += [jax.ShapeDtypeStruct(ws[a].shape, F32)] * 3
    res = pl.pallas_call(body, name="adamw_small", out_shape=tuple(outs), compiler_params=_cp())(*args)
    return [res[3 * a:3 * a + 3] for a in range(n)]


def _flat_pack(arrs, rows):
    flat = jnp.concatenate([a.reshape(-1) for a in arrs])
    return jnp.pad(flat, (0, rows * D - flat.shape[0])).reshape(rows, D)


def _flat_unpack(flat, shapes):
    out, off = [], 0
    for shp in shapes:
        size = 1
        for d_ in shp:
            size *= d_
        out.append(flat[off:off + size].reshape(shp))
        off += size
    return out


SMALL_EVEN = ("even_pre_g", "even_a_ln_g", "even_a_ln_b", "even_a_ws", "even_a_bs", "even_b_conv", "even_mem_g",
              "even_post_g")
SMALL_ODD = ("odd_pre_g", "odd_c_wgrp", "odd_c_scale", "odd_d_dw_w", "odd_d_dw_b", "odd_d_ln_g", "odd_d_ln_b",
             "odd_d_pw_b", "odd_mem_g", "odd_post_g")
BIG = ("even_w_in", "even_w_kv", "even_w_out", "odd_w_in", "odd_d_pw_w", "odd_w_kv", "odd_w_out")
WEIGHTS = ("even_pre_g", "even_w_in", "even_a_ln_g", "even_a_ln_b", "even_a_ws", "even_a_bs", "even_b_conv",
           "even_mem_g", "even_w_kv", "even_w_out", "even_post_g", "odd_pre_g", "odd_w_in", "odd_c_wgrp",
           "odd_c_scale", "odd_d_dw_w", "odd_d_dw_b", "odd_d_ln_g", "odd_d_ln_b", "odd_d_pw_w", "odd_d_pw_b",
           "odd_mem_g", "odd_w_kv", "odd_w_out", "odd_post_g")
PACKED = (("even_b_conv", (3, 192)), ("odd_pre_g", (1, 256)), ("odd_c_scale", (1, 192)), ("odd_d_dw_w", (31, 192)),
          ("odd_d_dw_b", (1, 192)), ("odd_d_ln_g", (1, 192)), ("odd_d_ln_b", (1, 192)), ("odd_d_pw_b", (1, 192)),
          ("odd_mem_g", (1, 256)), ("odd_post_g", (1, 256)))
PACK_ROWS = 16
SMALL_ROWS = 256


def _four(g):
    return g.reshape(N_CHIPS, 2, g.shape[0] // (2 * N_CHIPS), g.shape[1])


def _step(x, mem, target, w, place):
    wt = {}
    pack = _flat_pack([w[n][0] for n, _ in PACKED], PACK_ROWS)
    shards = {"even_w_in_t": w["even_w_in"][0].T, "odd_w_in_t": w["odd_w_in"][0].T, "even_w_kv": w["even_w_kv"][0],
              "odd_w_kv": w["odd_w_kv"][0], "even_w_out": w["even_w_out"][0], "odd_w_out": w["odd_w_out"][0],
              "odd_d_pw_w": w["odd_d_pw_w"][0]}
    placed = {n: _place_shard(a, place, BF16, "place_" + n) for n, a in shards.items()}
    placed["pack"] = _place_shard(pack, place, F32, "place_pack")

    wt["even_w_in_t"], packs = _run_rider(_GatherRider([placed["even_w_in_t"], placed["pack"]]), "gather_first")
    packs = packs.reshape(N_CHIPS, PACK_ROWS * D)
    per_chip = [_flat_unpack(packs[k], [shp for _, shp in PACKED]) for k in range(N_CHIPS)]
    for a, (name, _) in enumerate(PACKED):
        wt[name] = jnp.concatenate([per_chip[k][a] for k in range(N_CHIPS)], axis=-1)
    for name in ("even_pre_g", "even_a_ln_g", "even_a_ln_b", "even_mem_g", "even_post_g"):
        wt[name] = w[name]

    tril = jnp.tril(jnp.ones((CH, CH), dtype=bool))
    wcat = jnp.where(tril[None], w["even_a_ws"][0], 0.0).transpose(1, 0, 2).reshape(CH, 4 * CH).astype(BF16)
    bsg = jnp.repeat(w["even_a_bs"][0].T, BW // 4, axis=1)
    hsel = (jnp.arange(BW)[:, None] // (BW // 4) == jnp.arange(128)[None, :]).astype(BF16)
    wg = w["odd_c_wgrp"][0]
    g4 = BW // 4
    wbd = jnp.zeros((BW, BW), F32)
    for g in range(4):
        wbd = lax.dynamic_update_slice(wbd, wg[g], (g * g4, g * g4))
    wbd = wbd.astype(BF16)

    names = ("even_w_kv", "even_w_out", "odd_w_kv", "odd_d_pw_w")
    (p_e, h_e), got = _in_fwd(x, wt["even_pre_g"], wt["even_w_in_t"], "even_in",
                              rider=_GatherRider([placed[n] for n in names]))
    wt.update(zip(names, got))
    kv_e = _kv_fwd(mem, wt["even_mem_g"], wt["even_w_kv"], "even_kv")
    (x1, o_e), got = _even_fwd(x, p_e, kv_e, wt["even_a_ln_g"], wt["even_a_ln_b"], wcat, bsg, wt["even_b_conv"],
                               wt["even_w_out"], wt["even_post_g"], rider=_GatherRider([placed["odd_w_in_t"]]))
    wt["odd_w_in_t"] = got[0]
    (p_o, h_o), got = _in_fwd(x1, wt["odd_pre_g"], wt["odd_w_in_t"], "odd_in",
                              rider=_GatherRider([placed["odd_w_out"]]))
    wt["odd_w_out"] = got[0]
    kv_o = _kv_fwd(mem, wt["odd_mem_g"], wt["odd_w_kv"], "odd_kv")
    dx2, o_o, cv_o, loss = _odd_fwd(x1, p_o, kv_o, wbd, wt["odd_c_scale"], wt["odd_d_dw_w"], wt["odd_d_dw_b"],
                                    wt["odd_d_ln_g"], wt["odd_d_ln_b"], wt["odd_d_pw_w"], wt["odd_d_pw_b"],
                                    wt["odd_w_out"], wt["odd_post_g"], target)
    (dpc_o, tmpc, tmpd, do_o, y_o, g_post_o, g_cs, g_wbd, g_dww, g_dwb, g_lng_o, g_lnb_o, g_pww, g_pwb,
     dkv_o) = _odd_bwd1(dx2, o_o, cv_o, p_o, kv_o, wbd, wt["odd_c_scale"], wt["odd_d_dw_w"], wt["odd_d_dw_b"],
                        wt["odd_d_ln_g"], wt["odd_d_ln_b"], wt["odd_d_pw_w"], wt["odd_d_pw_b"], wt["odd_w_out"],
                        wt["odd_post_g"])
    dpb_o, dx1, g_pre_o = _odd_bwd2(dpc_o, tmpc, tmpd, p_o, wt["odd_d_dw_w"], wt["odd_w_in_t"], x1,
                                    wt["odd_pre_g"], dx2)
    g_win_o = _grad_tn(dpb_o, h_o, 768, rows=ODD_IN, name="odd_gw_in_b")
    g_win_o = _grad_tn(dpc_o, h_o, 1280, out=g_win_o, rows=ODD_IN, row0=3 * BW, name="odd_gw_in_c")
    g_wout_o = _grad_tn(y_o, do_o, 1024, name="odd_gw_out")
    g_wkv_o, g_memg_o = _kv_bwd(mem, wt["odd_mem_g"], wt["odd_w_kv"], dkv_o, "odd_kv_bwd")
    big_o = [_four(g) for g in (g_win_o, g_pww.astype(BF16), g_wkv_o, g_wout_o)]
    recv_o = _swap_halves(big_o, None, "swap_halves_odd")
    sums_o = [_pair_sum(big_o[a], recv_o[a], place, "pair_sum_odd_%d" % a) for a in range(len(big_o))]
    (dpa_e, dpc_e, tmp_e, do_e, y_e, g_post_e, g_lng_e, g_lnb_e, g_wcat, g_bs, g_bconv,
     dkv_e), parts_o = _even_bwd1(dx1, o_e, p_e, kv_e, wt["even_a_ln_g"], wt["even_a_ln_b"], wcat, bsg, hsel,
                                  wt["even_b_conv"], wt["even_w_out"], wt["even_post_g"],
                                  rider=_ExchangeRider(sums_o))
    halves_o = [_chip_sum(sums_o[a], parts_o[a], place, "chip_sum_odd_%d" % a) for a in range(len(big_o))]
    g_wout_e = _grad_tn(y_e, do_e, 1024, name="even_gw_out")
    g_wkv_e, g_memg_e = _kv_bwd(mem, wt["even_mem_g"], wt["even_w_kv"], dkv_e, "even_kv_bwd")
    big_x = [_four(g) for g in (g_wkv_e, g_wout_e)]
    recv_x = _swap_halves(big_x, None, "swap_halves_kv_out")
    sums_x = [_pair_sum(big_x[a], recv_x[a], place, "pair_sum_kv_out_%d" % a) for a in range(len(big_x))]
    (dpb_e, dx0, g_pre_e), parts_x = _even_bwd2(dpa_e, dpc_e, tmp_e, p_e, wt["even_b_conv"], wt["even_w_in_t"], x,
                                                wt["even_pre_g"], dx1, rider=_ExchangeRider(sums_x))
    halves_x = [_chip_sum(sums_x[a], parts_x[a], place, "chip_sum_kv_out_%d" % a) for a in range(len(big_x))]
    g_win_e = _grad_tn(dpa_e, h_e, 768, rows=EVEN_IN, name="even_gw_in_a")
    g_win_e = _grad_tn(dpb_e, h_e, 768, out=g_win_e, rows=EVEN_IN, row0=3 * BW, name="even_gw_in_b")
    g_win_e = _grad_tn(dpc_e, h_e, 1280, out=g_win_e, rows=EVEN_IN, row0=5 * BW, name="even_gw_in_c")

    g_aws = jnp.where(tril[None], g_wcat.reshape(CH, 4, CH).transpose(1, 0, 2), 0.0)
    g_wgrp = jnp.stack([lax.dynamic_slice(g_wbd, (g * g4, g * g4), (g4, g4)) for g in range(4)])
    small = {
        "even_pre_g": g_pre_e, "even_a_ln_g": g_lng_e, "even_a_ln_b": g_lnb_e, "even_a_ws": g_aws,
        "even_a_bs": g_bs[:, 0:4].T, "even_b_conv": g_bconv[0:3], "even_mem_g": g_memg_e, "even_post_g": g_post_e,
        "odd_pre_g": g_pre_o, "odd_c_wgrp": g_wgrp, "odd_c_scale": g_cs, "odd_d_dw_w": g_dww.reshape(CONF, 8, BW).sum(axis=1),
        "odd_d_dw_b": g_dwb, "odd_d_ln_g": g_lng_o, "odd_d_ln_b": g_lnb_o, "odd_d_pw_b": g_pwb,
        "odd_mem_g": g_memg_o, "odd_post_g": g_post_o,
    }
    small_names = SMALL_EVEN + SMALL_ODD
    small_pack = _flat_pack([small[n] for n in small_names] + [loss[0, 0].reshape(1)], SMALL_ROWS)
    big_e = [_four(g_win_e)]
    recv_e = _swap_halves(big_e, small_pack, "swap_halves_even")
    sums_e = [_pair_sum(big_e[0], recv_e[0], place, "pair_sum_even_w_in")]
    small_sum = _small_sum(small_pack, recv_e[-1])
    parts_e = _run_rider(_ExchangeRider(sums_e, small_sum), "exchange_even")
    halves_e = [_chip_sum(sums_e[0], parts_e[0], place, "chip_sum_even_w_in")]
    half_small = _chip_sum(None, parts_e[-1], place, "chip_sum_small")
    full = _share_halves(halves_e + halves_x + halves_o + [half_small])
    order = ("even_w_in", "even_w_kv", "even_w_out", "odd_w_in", "odd_d_pw_w", "odd_w_kv", "odd_w_out")
    gbig = {n: full[a].reshape(full[a].shape[1] * 2, full[a].shape[2]) for a, n in enumerate(order)}
    return dx0, gbig, full[-1].reshape(-1), [small[n].shape for n in small_names]


def kernel(x, mem, even_pre_g, even_w_in, even_a_ln_g, even_a_ln_b, even_a_ws, even_a_bs, even_b_conv, even_mem_g, even_w_kv, even_w_out, even_post_g, odd_pre_g, odd_w_in, odd_c_wgrp, odd_c_scale, odd_d_dw_w, odd_d_dw_b, odd_d_ln_g, odd_d_ln_b, odd_d_pw_w, odd_d_pw_b, odd_mem_g, odd_w_kv, odd_w_out, odd_post_g, loss_target, m_even_pre_g, m_even_w_in, m_even_a_ln_g, m_even_a_ln_b, m_even_a_ws, m_even_a_bs, m_even_b_conv, m_even_mem_g, m_even_w_kv, m_even_w_out, m_even_post_g, m_odd_pre_g, m_odd_w_in, m_odd_c_wgrp, m_odd_c_scale, m_odd_d_dw_w, m_odd_d_dw_b, m_odd_d_ln_g, m_odd_d_ln_b, m_odd_d_pw_w, m_odd_d_pw_b, m_odd_mem_g, m_odd_w_kv, m_odd_w_out, m_odd_post_g, v_even_pre_g, v_even_w_in, v_even_a_ln_g, v_even_a_ln_b, v_even_a_ws, v_even_a_bs, v_even_b_conv, v_even_mem_g, v_even_w_kv, v_even_w_out, v_even_post_g, v_odd_pre_g, v_odd_w_in, v_odd_c_wgrp, v_odd_c_scale, v_odd_d_dw_w, v_odd_d_dw_b, v_odd_d_ln_g, v_odd_d_ln_b, v_odd_d_pw_w, v_odd_d_pw_b, v_odd_mem_g, v_odd_w_kv, v_odd_w_out, v_odd_post_g):
    given = dict(locals())
    w = {n: given[n] for n in WEIGHTS}
    mom = {n: given["m_" + n] for n in WEIGHTS}
    var = {n: given["v_" + n] for n in WEIGHTS}

    x_, y_, c_ = lax.axis_index("x"), lax.axis_index("y"), lax.axis_index("c")
    chip = 2 * x_ + y_
    place = jnp.stack([c_, chip]).astype(jnp.int32)
    grad_x, gbig, gsmall_flat, small_shapes = _step(x[0], mem[0], loss_target[0], w, place)

    names = SMALL_EVEN + SMALL_ODD
    grads = {}
    unpacked = _flat_unpack(gsmall_flat, small_shapes + [(1,)])
    loss = unpacked[-1][0]
    for n, g in zip(names, unpacked[:-1]):
        shard_shape = w[n].shape[1:]
        if g.shape[-1] != shard_shape[-1]:
            g = lax.dynamic_slice_in_dim(g, chip * shard_shape[-1], shard_shape[-1], axis=g.ndim - 1)
        grads[n] = g.reshape(shard_shape)

    def two_d(a):
        return a.reshape(-1, a.shape[-1])

    upd = {}
    for n in BIG:
        if n.endswith("w_in"):
            res = _adamw_big(w[n][0].T, gbig[n], mom[n][0].T, var[n][0].T, "adamw_" + n)
            grads[n] = gbig[n].T
            upd[n] = tuple(r.T for r in res)
        else:
            grads[n] = gbig[n]
            upd[n] = _adamw_big(w[n][0], gbig[n], mom[n][0], var[n][0], "adamw_" + n)
    res = _adamw_small([two_d(w[n][0]) for n in names], [two_d(grads[n]) for n in names],
                       [two_d(mom[n][0]) for n in names], [two_d(var[n][0]) for n in names])
    for n, r in zip(names, res):
        upd[n] = r

    outs = [loss, grad_x[None]]
    outs += [grads[n].reshape(w[n].shape) for n in WEIGHTS]
    for j in range(3):
        outs += [upd[n][j].reshape(w[n].shape) for n in WEIGHTS]
    return tuple(outs)
```

```python
import functools

import jax
import jax.numpy as jnp
from jax import lax
from jax.experimental import pallas as pl
from jax.experimental.pallas import tpu as pltpu

F32 = jnp.float32
BF16 = jnp.bfloat16
MESH = pl.DeviceIdType.MESH

D = 1024
N_MEM = 256
MIX = 2048
XA = 512
HD = 128
BW = 768
CH = 128
EPS = 1e-6
SCALE = HD ** -0.5
POOL_WINDOWS = (2, 4, 8, 16)
CONF = 31
EVEN_IN = 6400
ODD_IN = 4864
N_CHIPS = 4

ADAM_LR = 0.001
ADAM_B1 = 0.9
ADAM_B2 = 0.999
ADAM_EPS = 1e-08
ADAM_WD = 0.01
ADAM_STEP = 10

TS = 256
HALO = 32
VMEM_LIMIT = 56 * 1024 * 1024


def _cp(sem=None):
    return pltpu.CompilerParams(dimension_semantics=sem, vmem_limit_bytes=VMEM_LIMIT)


def _dot(a, b):
    return jnp.dot(a, b, preferred_element_type=F32)


def _dot_nt(a, b):
    return lax.dot_general(a, b, (((1,), (1,)), ((), ())), preferred_element_type=F32)


def _dot_tn(a, b):
    return lax.dot_general(a, b, (((0,), (0,)), ((), ())), preferred_element_type=F32)


def _sigmoid(x):
    return 1.0 / (1.0 + jnp.exp(-x))


def _resident(shape):
    return pl.BlockSpec(shape, lambda *_: (0,) * len(shape), pipeline_mode=pl.Buffered(1))


def _const(shape):
    return pl.BlockSpec(shape, lambda *_: (0,) * len(shape))


def _kv_fwd(mem, mem_g, wkv, name):
    def body(mem_ref, g_ref, w_ref, kv_ref):
        m = mem_ref[...]
        r = lax.rsqrt(jnp.mean(m * m, axis=-1, keepdims=True) + EPS)
        mn = (m * r * g_ref[...]).astype(BF16)
        kv_ref[...] = _dot(mn, w_ref[...]).astype(BF16)

    return pl.pallas_call(body, out_shape=jax.ShapeDtypeStruct((N_MEM, D), BF16), name=name,
                          compiler_params=_cp())(mem, mem_g, wkv)


def _kv_bwd(mem, mem_g, wkv, dkv, name):
    def body(mem_ref, g_ref, w_ref, dkv_ref, dw_ref, dg_ref):
        m = mem_ref[...]
        r = lax.rsqrt(jnp.mean(m * m, axis=-1, keepdims=True) + EPS)
        mh = m * r
        mn = (mh * g_ref[...]).astype(BF16)
        dkv = dkv_ref[...].astype(BF16)
        dw_ref[...] = _dot_tn(mn, dkv).astype(BF16)
        dmn = _dot_nt(dkv, w_ref[...])
        dg_ref[...] = jnp.sum(dmn * mh, axis=0, keepdims=True)

    return pl.pallas_call(body, out_shape=(jax.ShapeDtypeStruct((D, D), BF16), jax.ShapeDtypeStruct((1, D), F32)),
                          name=name, compiler_params=_cp())(mem, mem_g, wkv, dkv)


def _host_call(body, *, grid, name, out_shape, in_specs, out_specs, args, scratch_shapes=(), aliases=None,
               rider=None):
    sem = ("arbitrary",) * len(grid)
    aliases = dict(aliases or {})
    if rider is None:
        res = pl.pallas_call(body, grid=grid, name=name, out_shape=tuple(out_shape), in_specs=list(in_specs),
                             out_specs=tuple(out_specs), scratch_shapes=list(scratch_shapes),
                             input_output_aliases=aliases, compiler_params=_cp(sem))(*args)
        return tuple(res), ()
    n_in, n_out, n_sc = len(in_specs), len(out_specs), len(scratch_shapes)
    r_in, r_out = len(rider.inputs), len(rider.out_shapes)

    def full_body(*refs):
        host_in = refs[:n_in]
        rid_in = refs[n_in:n_in + r_in]
        host_out = refs[n_in + r_in:n_in + r_in + n_out]
        rid_out = refs[n_in + r_in + n_out:n_in + r_in + n_out + r_out]
        host_sc = refs[n_in + r_in + n_out + r_out:n_in + r_in + n_out + r_out + n_sc]
        sems = refs[n_in + r_in + n_out + r_out + n_sc:]
        first = pl.program_id(0) == 0
        last = pl.program_id(0) == grid[0] - 1
        for ax in range(1, len(grid)):
            first = jnp.logical_and(first, pl.program_id(ax) == 0)
            last = jnp.logical_and(last, pl.program_id(ax) == grid[ax] - 1)

        @pl.when(first)
        def _():
            rider.start(rid_in, rid_out, sems)

        if rider.has_mid:
            @pl.when(last)
            def _():
                rider.mid(rid_in, rid_out, sems)

        body(*host_in, *host_out, *host_sc)

        @pl.when(last)
        def _():
            rider.end(rid_in, rid_out, sems)

    aliases.update({n_in + j: n_out + k for j, k in rider.aliases.items()})
    res = pl.pallas_call(
        full_body, grid=grid, name=name, out_shape=tuple(out_shape) + tuple(rider.out_shapes),
        in_specs=list(in_specs) + _hbm_specs(r_in), out_specs=tuple(out_specs) + tuple(_hbm_specs(r_out)),
        scratch_shapes=list(scratch_shapes) + list(rider.sems), input_output_aliases=aliases,
        compiler_params=_cp(sem),
    )(*args, *rider.inputs)
    return tuple(res[:n_out]), tuple(res[n_out:])


def _in_fwd(x, pre_g, w_t, name, rider=None):
    s, n = x.shape[0], w_t.shape[0]
    tm = min(512, s)
    nc = 256

    def body(x_ref, g_ref, w_ref, p_ref, h_ref):
        xv = x_ref[...]
        r = lax.rsqrt(jnp.mean(xv * xv, axis=-1, keepdims=True) + EPS)
        h = (xv * r * g_ref[...]).astype(BF16)
        h_ref[...] = h
        for j in range(n // nc):
            p_ref[:, j * nc:(j + 1) * nc] = _dot_nt(h, w_ref[j * nc:(j + 1) * nc, :]).astype(BF16)

    return _host_call(
        body, grid=(s // tm,), name=name, rider=rider,
        out_shape=(jax.ShapeDtypeStruct((s, n), BF16), jax.ShapeDtypeStruct((s, D), BF16)),
        in_specs=[pl.BlockSpec((tm, D), lambda i: (i, 0)), _const((1, D)), _resident((n, D))],
        out_specs=(pl.BlockSpec((tm, n), lambda i: (i, 0)), pl.BlockSpec((tm, D), lambda i: (i, 0))),
        args=(x, pre_g, w_t))


def _xattn_fwd(q, kv_ref):
    outs, probs = [], []
    for h in range(XA // HD):
        qh = q[:, h * HD:(h + 1) * HD]
        kh = kv_ref[:, h * HD:(h + 1) * HD]
        vh = kv_ref[:, XA + h * HD:XA + (h + 1) * HD]
        sc = _dot_nt(qh, kh) * SCALE
        e = jnp.exp(sc - jnp.max(sc, axis=-1, keepdims=True))
        pr = e / jnp.sum(e, axis=-1, keepdims=True)
        outs.append(_dot(pr.astype(BF16), vh))
        probs.append(pr)
    return jnp.concatenate(outs, axis=-1), probs


def _xattn_bwd(dyx, q, probs, kv_ref, dkv_ref):
    dqs = []
    for h in range(XA // HD):
        qh = q[:, h * HD:(h + 1) * HD]
        kh = kv_ref[:, h * HD:(h + 1) * HD]
        vh = kv_ref[:, XA + h * HD:XA + (h + 1) * HD]
        dy = dyx[:, h * HD:(h + 1) * HD].astype(BF16)
        pr = probs[h]
        dp = _dot_nt(dy, vh)
        ds = (pr * (dp - jnp.sum(dp * pr, axis=-1, keepdims=True))).astype(BF16)
        dqs.append(_dot(ds, kh) * SCALE)
        dkv_ref[:, h * HD:(h + 1) * HD] += _dot_tn(ds, qh) * SCALE
        dkv_ref[:, XA + h * HD:XA + (h + 1) * HD] += _dot_tn(pr.astype(BF16), dy)
    return jnp.concatenate(dqs, axis=-1)


def _layer_norm_fwd(v, g, b):
    mu = jnp.mean(v, axis=-1, keepdims=True)
    vc = v - mu
    rstd = lax.rsqrt(jnp.mean(vc * vc, axis=-1, keepdims=True) + EPS)
    vhat = vc * rstd
    return vhat * g + b, vhat, rstd


def _layer_norm_bwd(dy, vhat, rstd, g):
    dvh = dy * g
    return rstd * (dvh - jnp.mean(dvh, axis=-1, keepdims=True) - vhat * jnp.mean(dvh * vhat, axis=-1, keepdims=True))


def _head_masks():
    col = lax.broadcasted_iota(jnp.int32, (1, BW), 1)
    return [(col >= h * (BW // 4)) & (col < (h + 1) * (BW // 4)) for h in range(4)]


def _halo_prev(nblk_per_tile):
    return lambda i: (jnp.maximum(i * nblk_per_tile - 1, 0), 0)


def _row_ids(i, t):
    return i * t + lax.broadcasted_iota(jnp.int32, (t, 1), 0)


def _even_mix(i, p_ref, ph_ref, ln_g, ln_b, wcat_ref, bsg_ref, bconv_ref, wbuf):
    t = p_ref.shape[0]
    u = p_ref[:, 0:BW].astype(F32)
    v = p_ref[:, BW:2 * BW].astype(F32)
    bg = p_ref[:, 2 * BW:3 * BW].astype(F32)
    cg = p_ref[:, 3 * BW:4 * BW].astype(F32)
    xin = p_ref[:, 4 * BW:5 * BW].astype(F32)
    vn, vhat, rstd = _layer_norm_fwd(v, ln_g, ln_b)
    masks = _head_masks()
    sgs, vsts = [], []
    for n in range(t // CH):
        vn_c = vn[n * CH:(n + 1) * CH]
        vst = jnp.concatenate([jnp.where(m, vn_c, 0.0) for m in masks], axis=0).astype(BF16)
        sgs.append(_dot(wcat_ref[...], vst) + bsg_ref[...])
        vsts.append(vst)
    sg = jnp.concatenate(sgs, axis=0)
    ya = u * sg
    w_halo = ph_ref[:, 3 * BW:4 * BW].astype(F32) * ph_ref[:, 4 * BW:5 * BW].astype(F32)
    wbuf[0:HALO, :] = jnp.where(i > 0, w_halo, 0.0)
    wbuf[HALO:HALO + t, :] = cg * xin
    conv = (bconv_ref[0:1, :] * wbuf[pl.ds(HALO - 2, t), :] + bconv_ref[1:2, :] * wbuf[pl.ds(HALO - 1, t), :]
            + bconv_ref[2:3, :] * wbuf[pl.ds(HALO, t), :])
    yb = bg * conv
    return dict(u=u, bg=bg, cg=cg, xin=xin, vhat=vhat, rstd=rstd, sg=sg, vsts=vsts, conv=conv, ya=ya, yb=yb,
                masks=masks)


def _pool_select(vals):
    col = lax.broadcasted_iota(jnp.int32, (1, BW), 1)
    g = BW // 4
    return jnp.where(col < g, vals[0], jnp.where(col < 2 * g, vals[1], jnp.where(col < 3 * g, vals[2], vals[3])))


def _inv_counts(i, t):
    rows = _row_ids(i, t) + 1
    return [1.0 / jnp.minimum(rows, w).astype(F32) for w in POOL_WINDOWS]


def _band_matrices(t, forward):
    j = jnp.arange(t)[:, None]
    r = jnp.arange(HALO + t)[None, :]
    if forward:
        return jnp.stack([(r >= j) & (r < j + w) for w in POOL_WINDOWS]).astype(BF16)
    return jnp.stack([(r <= HALO + j) & (r > HALO + j - w) for w in POOL_WINDOWS]).astype(BF16)


SHIFT_ROWS = HALO + TS - 8


def _shifted_copies(buf, sh):
    for b in range(1, 8):
        sh[b - 1] = buf[pl.ds(b, SHIFT_ROWS), :]


def _rows_at(buf, sh, off, t):
    a, b = divmod(off, 8)
    return buf[pl.ds(8 * a, t), :] if b == 0 else sh[b - 1, pl.ds(8 * a, t), :]


def _tap_sums(d_ref, buf, sh, base, out_ref):
    t = d_ref.shape[0]
    group = 4
    for k0 in range(0, CONF, group):
        taps = list(range(k0, min(k0 + group, CONF)))

        def step(r, accs, taps=taps):
            row = pl.multiple_of(r * 8, 8)
            d = d_ref[pl.ds(row, 8), :]
            new = []
            for acc, k in zip(accs, taps):
                a, b = divmod(base + k, 8)
                src = buf[pl.ds(row + 8 * a, 8), :] if b == 0 else sh[b - 1, pl.ds(row + 8 * a, 8), :]
                new.append(acc + d * src)
            return tuple(new)

        accs = lax.fori_loop(0, t // 8, step, tuple(jnp.zeros((8, BW), F32) for _ in taps), unroll=2)
        for acc, k in zip(accs, taps):
            out_ref[8 * k:8 * k + 8, :] += acc


def _odd_mix(i, p_ref, ph_ref, bands_ref, wbd_ref, cscale, dww_ref, dwb, ln_g, ln_b, pww_ref, pwb, gbuf, gsh,
             cv=None):
    t = p_ref.shape[0]
    zc_bf = p_ref[:, 0:BW]
    zc = zc_bf.astype(F32)
    ga = p_ref[:, BW:2 * BW].astype(F32)
    gb = p_ref[:, 2 * BW:3 * BW].astype(F32)
    zh = ph_ref[:, 0:BW]
    zcat = jnp.concatenate([jnp.where(i > 0, zh, jnp.zeros_like(zh)), zc_bf], axis=0)
    inv = _inv_counts(i, t)
    pooled = _pool_select([_dot(bands_ref[w], zcat) * inv[w] for w in range(len(POOL_WINDOWS))]) - zc
    pooled_bf = pooled.astype(BF16)
    pre = _dot(pooled_bf, wbd_ref[...])
    yc = pre * cscale
    sgb = _sigmoid(gb)
    z = ga * sgb
    gh_a = ph_ref[:, BW:2 * BW].astype(F32)
    gh_b = ph_ref[:, 2 * BW:3 * BW].astype(F32)
    gbuf[0:HALO, :] = jnp.where(i > 0, gh_a * _sigmoid(gh_b), 0.0)
    gbuf[HALO:HALO + t, :] = z
    _shifted_copies(gbuf, gsh)
    if cv is None:
        cv = dwb + dww_ref[CONF - 1:CONF, :] * z
        for k in range(CONF - 1):
            cv = cv + dww_ref[k:k + 1, :] * _rows_at(gbuf, gsh, HALO - (CONF - 1) + k, t)
    zl, zhat, rstd = _layer_norm_fwd(cv, ln_g, ln_b)
    szl = _sigmoid(zl)
    zs = (zl * szl).astype(BF16)
    yd = _dot(zs, pww_ref[...]) + pwb
    return dict(ga=ga, sgb=sgb, pooled_bf=pooled_bf, pre=pre, yc=yc, zhat=zhat, rstd=rstd, zl=zl, szl=szl,
                zs=zs, yd=yd, inv=inv, cv=cv)


def _post_norm(o, post_g):
    r = lax.rsqrt(jnp.mean(o * o, axis=-1, keepdims=True) + EPS)
    return o * r, r


def _gate_out(y_a, y_b, y_x, gate, wout_ref):
    sgt = _sigmoid(gate)
    sgate = gate * sgt
    ys = [(y_a * sgate[:, 0:BW]).astype(BF16), (y_b * sgate[:, BW:2 * BW]).astype(BF16),
          (y_x * sgate[:, 2 * BW:MIX]).astype(BF16)]
    o = (_dot(ys[0], wout_ref[0:BW, :]) + _dot(ys[1], wout_ref[BW:2 * BW, :]) + _dot(ys[2], wout_ref[2 * BW:MIX, :]))
    return o, ys, sgt, sgate


def _tile_specs(s, n):
    nh = TS // HALO
    return pl.BlockSpec((TS, n), lambda i: (i, 0)), pl.BlockSpec((HALO, n), _halo_prev(nh))


def _even_fwd(x, p, kv, ln_g, ln_b, wcat, bsg, bconv, wout, post_g, rider=None):
    s = x.shape[0]

    def body(x_ref, p_ref, ph_ref, kv_ref, lng, lnb, wcat_ref, bsg_ref, bconv_ref, wout_ref, pg, x1_ref, o_ref, wbuf):
        i = pl.program_id(0)
        mx = _even_mix(i, p_ref, ph_ref, lng[...], lnb[...], wcat_ref, bsg_ref, bconv_ref, wbuf)
        yx, _ = _xattn_fwd(p_ref[:, 5 * BW:5 * BW + XA], kv_ref)
        gate = p_ref[:, 5 * BW + XA:EVEN_IN].astype(F32)
        o, _, _, _ = _gate_out(mx["ya"], mx["yb"], yx, gate, wout_ref)
        n, _ = _post_norm(o, pg[...])
        o_ref[...] = o
        x1_ref[...] = x_ref[...] + n * pg[...]

    tile, halo = _tile_specs(s, EVEN_IN)
    row = pl.BlockSpec((TS, D), lambda i: (i, 0))
    return _host_call(
        body, grid=(s // TS,), name="even_fwd", rider=rider,
        out_shape=(jax.ShapeDtypeStruct((s, D), F32), jax.ShapeDtypeStruct((s, D), F32)),
        in_specs=[row, tile, halo, _const((N_MEM, D)), _const((1, BW)), _const((1, BW)), _const((CH, 4 * CH)),
                  _const((CH, BW)), _const((3, BW)), _resident((MIX, D)), _const((1, D))],
        out_specs=(row, row),
        scratch_shapes=[pltpu.VMEM((HALO + TS, BW), F32)],
        args=(x, p, p, kv, ln_g, ln_b, wcat, bsg, bconv, wout, post_g))


def _odd_fwd(x1, p, kv, wbd, cscale, dww, dwb, ln_g, ln_b, pww, pwb, wout, post_g, target):
    s = x1.shape[0]

    def body(x_ref, p_ref, ph_ref, kv_ref, bands_ref, wbd_ref, cs, dww_ref, dwb_ref, lng, lnb, pww_ref, pwb_ref,
             wout_ref, pg, tgt_ref, dx_ref, o_ref, cv_ref, loss_ref, gbuf, gsh):
        i = pl.program_id(0)
        mx = _odd_mix(i, p_ref, ph_ref, bands_ref, wbd_ref, cs[...], dww_ref, dwb_ref[...], lng[...], lnb[...],
                      pww_ref, pwb_ref[...], gbuf, gsh)
        cv_ref[...] = mx["cv"]
        yx, _ = _xattn_fwd(p_ref[:, 3 * BW:3 * BW + XA], kv_ref)
        gate = p_ref[:, 3 * BW + XA:ODD_IN].astype(F32)
        o, _, _, _ = _gate_out(mx["yc"], mx["yd"], yx, gate, wout_ref)
        n, _ = _post_norm(o, pg[...])
        o_ref[...] = o
        err = x_ref[...] + n * pg[...] - tgt_ref[...]
        dx_ref[...] = err * (1.0 / D)

        @pl.when(i == 0)
        def _():
            loss_ref[...] = jnp.zeros_like(loss_ref)

        loss_ref[...] += 0.5 * jnp.sum(jnp.sum(err * err, axis=-1, keepdims=True) * (1.0 / D), axis=0, keepdims=True)

    tile, halo = _tile_specs(s, ODD_IN)
    row = pl.BlockSpec((TS, D), lambda i: (i, 0))
    vec = _const((1, BW))
    return pl.pallas_call(
        body, grid=(s // TS,), name="odd_fwd",
        out_shape=(jax.ShapeDtypeStruct((s, D), F32), jax.ShapeDtypeStruct((s, D), F32),
                   jax.ShapeDtypeStruct((s, BW), F32), jax.ShapeDtypeStruct((8, 128), F32)),
        in_specs=[row, tile, halo, _const((N_MEM, D)), _const((4, TS, HALO + TS)), _const((BW, BW)), vec,
                  _const((CONF, BW)), vec, vec, vec, _const((BW, BW)), vec, _resident((MIX, D)), _const((1, D)), row],
        out_specs=(row, row, pl.BlockSpec((TS, BW), lambda i: (i, 0)), _const((8, 128))),
        scratch_shapes=[pltpu.VMEM((HALO + TS, BW), F32), pltpu.VMEM((7, SHIFT_ROWS, BW), F32)],
        compiler_params=_cp(("arbitrary",)),
    )(x1, p, p, kv, _band_matrices(TS, False), wbd, cscale, dww, dwb, ln_g, ln_b, pww, pwb, wout, post_g, target)


def _acc_init(i, refs):
    @pl.when(i == 0)
    def _():
        for r in refs:
            r[...] = jnp.zeros_like(r)


def _post_norm_bwd(dx, o, pg, dpg_ref):
    n, r = _post_norm(o, pg)
    dpg_ref[...] += jnp.sum(dx * n, axis=0, keepdims=True)
    dn = dx * pg
    return (r * (dn - n * jnp.mean(dn * n, axis=-1, keepdims=True))).astype(BF16)


def _gate_bwd(do, wout_ref, ys_f32, gate, y_ref):
    dy = _dot_nt(do, wout_ref[...])
    sgt = _sigmoid(gate)
    sgate = gate * sgt
    dsilu = sgt * (1.0 + gate * (1.0 - sgt))
    offs = (0, BW, 2 * BW, MIX)
    dys, dgs = [], []
    for j, yv in enumerate(ys_f32):
        a, b = offs[j], offs[j + 1]
        y_ref[:, a:b] = (yv * sgate[:, a:b]).astype(BF16)
        dys.append(dy[:, a:b] * sgate[:, a:b])
        dgs.append(dy[:, a:b] * yv * dsilu[:, a:b])
    return dys, jnp.concatenate(dgs, axis=-1)


NEXT = 16


def _even_bwd1(dx, o, p, kv, ln_g, ln_b, wcat, bsg, hsel, bconv, wout, post_g, rider=None):
    s = dx.shape[0]
    nt = s // TS

    def body(dx_ref, o_ref, p_ref, ph_ref, dxn_ref, on_ref, pn_ref, kv_ref, lng, lnb, wcat_ref, bsg_ref, hsel_ref,
             bconv_ref, wout_ref, pg,
             dp_ref, do_ref, y_ref, dpg_ref, dlng_ref, dlnb_ref, dwcat_ref, dbs_ref, dbconv_ref, dkv_ref, wbuf, dbuf):
        i = pl.program_id(0)
        _acc_init(i, (dpg_ref, dlng_ref, dlnb_ref, dwcat_ref, dbs_ref, dbconv_ref, dkv_ref))
        mx = _even_mix(i, p_ref, ph_ref, lng[...], lnb[...], wcat_ref, bsg_ref, bconv_ref, wbuf)
        q = p_ref[:, 5 * BW:5 * BW + XA]
        yx, probs = _xattn_fwd(q, kv_ref)
        gate = p_ref[:, 5 * BW + XA:EVEN_IN].astype(F32)
        do = _post_norm_bwd(dx_ref[...], o_ref[...], pg[...], dpg_ref)
        do_ref[...] = do
        (dya, dyb, dyx), dgate = _gate_bwd(do, wout_ref, (mx["ya"], mx["yb"], yx), gate, y_ref)
        dpa_ref = dp_ref
        dpa_ref[:, 0:BW] = (dya * mx["sg"]).astype(BF16)
        dsg = (dya * mx["u"]).astype(BF16)
        dvns = []
        for n in range(TS // CH):
            dsg_c = dsg[n * CH:(n + 1) * CH]
            dvst = _dot_tn(wcat_ref[...], dsg_c)
            dvn_c = jnp.where(mx["masks"][0], dvst[0:CH], 0.0)
            for h in range(1, 4):
                dvn_c = dvn_c + jnp.where(mx["masks"][h], dvst[h * CH:(h + 1) * CH], 0.0)
            dvns.append(dvn_c)
            dwcat_ref[...] += _dot_nt(dsg_c, mx["vsts"][n])
            dbs_ref[...] += _dot(dsg_c, hsel_ref[...])
        dvn = jnp.concatenate(dvns, axis=0)
        dlng_ref[...] += jnp.sum(dvn * mx["vhat"], axis=0, keepdims=True)
        dlnb_ref[...] += jnp.sum(dvn, axis=0, keepdims=True)
        dpa_ref[:, BW:2 * BW] = _layer_norm_bwd(dvn, mx["vhat"], mx["rstd"], lng[...]).astype(BF16)
        dpa_ref[:, 2 * BW:3 * BW] = (dyb * mx["conv"]).astype(BF16)
        dconv = dyb * mx["bg"]
        for k in range(3):
            dbconv_ref[k:k + 1, :] += jnp.sum(dconv * wbuf[pl.ds(HALO - 2 + k, TS), :], axis=0, keepdims=True)
        n_n, r_n = _post_norm(on_ref[...], pg[...])
        dn_n = dxn_ref[...] * pg[...]
        do_n = (r_n * (dn_n - n_n * jnp.mean(dn_n * n_n, axis=-1, keepdims=True))).astype(BF16)
        dy_n = _dot_nt(do_n, wout_ref[BW:2 * BW, :])
        g_n = pn_ref[:, 5 * BW + XA + BW:5 * BW + XA + 2 * BW].astype(F32)
        dconv_n = dy_n * (g_n * _sigmoid(g_n)) * pn_ref[:, 2 * BW:3 * BW].astype(F32)
        dbuf[0:TS, :] = dconv
        dbuf[TS:TS + NEXT, :] = jnp.where(i < nt - 1, dconv_n, 0.0)
        dw = (bconv_ref[2:3, :] * dconv + bconv_ref[1:2, :] * dbuf[pl.ds(1, TS), :]
              + bconv_ref[0:1, :] * dbuf[pl.ds(2, TS), :])
        dp_ref[:, 3 * BW:4 * BW] = (dw * mx["xin"]).astype(BF16)
        dp_ref[:, 4 * BW:5 * BW] = (dw * mx["cg"]).astype(BF16)
        dp_ref[:, 5 * BW:5 * BW + XA] = _xattn_bwd(dyx, q, probs, kv_ref, dkv_ref).astype(BF16)
        dp_ref[:, 5 * BW + XA:EVEN_IN] = dgate.astype(BF16)

    tile, halo = _tile_specs(s, EVEN_IN)
    row = pl.BlockSpec((TS, D), lambda i: (i, 0))
    vec = _const((1, BW))
    nxt = _halo_next(TS // NEXT, s // NEXT)

    def out(n):
        return pl.BlockSpec((TS, n), lambda i: (i, 0))

    return _host_call(
        body, grid=(nt,), name="even_bwd1", rider=rider,
        out_shape=(jax.ShapeDtypeStruct((s, EVEN_IN), BF16), jax.ShapeDtypeStruct((s, D), BF16),
                   jax.ShapeDtypeStruct((s, MIX), BF16),
                   jax.ShapeDtypeStruct((1, D), F32), jax.ShapeDtypeStruct((1, BW), F32),
                   jax.ShapeDtypeStruct((1, BW), F32), jax.ShapeDtypeStruct((CH, 4 * CH), F32),
                   jax.ShapeDtypeStruct((CH, 128), F32), jax.ShapeDtypeStruct((8, BW), F32),
                   jax.ShapeDtypeStruct((N_MEM, D), F32)),
        in_specs=[row, row, tile, halo, pl.BlockSpec((NEXT, D), nxt), pl.BlockSpec((NEXT, D), nxt),
                  pl.BlockSpec((NEXT, EVEN_IN), nxt), _const((N_MEM, D)), vec, vec, _const((CH, 4 * CH)),
                  _const((CH, BW)), _const((BW, 128)), _const((3, BW)), _resident((MIX, D)), _const((1, D))],
        out_specs=(out(EVEN_IN), out(D), out(MIX),
                   _const((1, D)), vec, vec, _const((CH, 4 * CH)), _const((CH, 128)), _const((8, BW)),
                   _const((N_MEM, D))),
        scratch_shapes=[pltpu.VMEM((HALO + TS, BW), F32), pltpu.VMEM((TS + NEXT, BW), F32)],
        args=(dx, o, p, p, dx, o, p, kv, ln_g, ln_b, wcat, bsg, hsel, bconv, wout, post_g))


def _odd_bwd1(dx, o, cv, p, kv, wbd, cscale, dww, dwb, ln_g, ln_b, pww, pwb, wout, post_g):
    s = dx.shape[0]

    def body(dx_ref, o_ref, cv_ref, p_ref, ph_ref, kv_ref, bands_ref, wbd_ref, cs, dww_ref, dwb_ref, lng, lnb,
             pww_ref, pwb_ref, wout_ref, pg,
             dpc_ref, tmpc_ref, tmpd_ref, do_ref, y_ref, dpg_ref, dcs_ref, dwbd_ref, ddww_ref, ddwb_ref, dlng_ref,
             dlnb_ref, dpww_ref, dpwb_ref, dkv_ref, gbuf, gsh, dcv_buf):
        i = pl.program_id(0)
        _acc_init(i, (dpg_ref, dcs_ref, dwbd_ref, ddww_ref, ddwb_ref, dlng_ref, dlnb_ref, dpww_ref, dpwb_ref,
                      dkv_ref))
        mx = _odd_mix(i, p_ref, ph_ref, bands_ref, wbd_ref, cs[...], dww_ref, dwb_ref[...], lng[...], lnb[...],
                      pww_ref, pwb_ref[...], gbuf, gsh, cv=cv_ref[...])
        q = p_ref[:, 3 * BW:3 * BW + XA]
        yx, probs = _xattn_fwd(q, kv_ref)
        gate = p_ref[:, 3 * BW + XA:ODD_IN].astype(F32)
        do = _post_norm_bwd(dx_ref[...], o_ref[...], pg[...], dpg_ref)
        do_ref[...] = do
        (dyc, dyd, dyx), dgate = _gate_bwd(do, wout_ref, (mx["yc"], mx["yd"], yx), gate, y_ref)
        dcs_ref[...] += jnp.sum(dyc * mx["pre"], axis=0, keepdims=True)
        dpre = (dyc * cs[...]).astype(BF16)
        dwbd_ref[...] += _dot_tn(mx["pooled_bf"], dpre)
        dpooled = _dot_nt(dpre, wbd_ref[...])
        tmpc_ref[...] = _pool_select([dpooled * c_ for c_ in mx["inv"]]).astype(BF16)
        dyd_bf = dyd.astype(BF16)
        dpwb_ref[...] += jnp.sum(dyd, axis=0, keepdims=True)
        dpww_ref[...] += _dot_tn(mx["zs"], dyd_bf)
        dzs = _dot_nt(dyd_bf, pww_ref[...])
        zl, szl = mx["zl"], mx["szl"]
        dzl = dzs * (szl * (1.0 + zl * (1.0 - szl)))
        dlng_ref[...] += jnp.sum(dzl * mx["zhat"], axis=0, keepdims=True)
        dlnb_ref[...] += jnp.sum(dzl, axis=0, keepdims=True)
        dcv = _layer_norm_bwd(dzl, mx["zhat"], mx["rstd"], lng[...])
        tmpd_ref[...] = dcv.astype(BF16)
        ddwb_ref[...] += jnp.sum(dcv, axis=0, keepdims=True)
        dcv_buf[...] = dcv
        _tap_sums(dcv_buf, gbuf, gsh, HALO - (CONF - 1), ddww_ref)
        dpc_ref[:, 0:XA] = _xattn_bwd(dyx, q, probs, kv_ref, dkv_ref).astype(BF16)
        dpc_ref[:, XA:XA + MIX] = dgate.astype(BF16)

    tile, halo = _tile_specs(s, ODD_IN)
    row = pl.BlockSpec((TS, D), lambda i: (i, 0))
    vec = _const((1, BW))

    def out(n):
        return pl.BlockSpec((TS, n), lambda i: (i, 0))

    return pl.pallas_call(
        body, grid=(s // TS,), name="odd_bwd1",
        out_shape=(jax.ShapeDtypeStruct((s, XA + MIX), BF16), jax.ShapeDtypeStruct((s, BW), BF16),
                   jax.ShapeDtypeStruct((s, BW), BF16), jax.ShapeDtypeStruct((s, D), BF16),
                   jax.ShapeDtypeStruct((s, MIX), BF16),
                   jax.ShapeDtypeStruct((1, D), F32), jax.ShapeDtypeStruct((1, BW), F32),
                   jax.ShapeDtypeStruct((BW, BW), F32), jax.ShapeDtypeStruct((8 * CONF, BW), F32),
                   jax.ShapeDtypeStruct((1, BW), F32), jax.ShapeDtypeStruct((1, BW), F32),
                   jax.ShapeDtypeStruct((1, BW), F32), jax.ShapeDtypeStruct((BW, BW), F32),
                   jax.ShapeDtypeStruct((1, BW), F32), jax.ShapeDtypeStruct((N_MEM, D), F32)),
        in_specs=[row, row, out(BW), tile, halo, _const((N_MEM, D)), _const((4, TS, HALO + TS)), _const((BW, BW)), vec,
                  _const((CONF, BW)), vec, vec, vec, _const((BW, BW)), vec, _resident((MIX, D)), _const((1, D))],
        out_specs=(out(XA + MIX), out(BW), out(BW), out(D), out(MIX),
                   _const((1, D)), vec, _const((BW, BW)), _const((8 * CONF, BW)), vec, vec, vec, _const((BW, BW)), vec,
                   _const((N_MEM, D))),
        scratch_shapes=[pltpu.VMEM((HALO + TS, BW), F32), pltpu.VMEM((7, SHIFT_ROWS, BW), F32),
                        pltpu.VMEM((TS, BW), F32)],
        compiler_params=_cp(("arbitrary",)),
    )(dx, o, cv, p, p, kv, _band_matrices(TS, False), wbd, cscale, dww, dwb, ln_g, ln_b, pww, pwb, wout, post_g)


def _halo_next(nblk_per_tile, nblk):
    return lambda i: (jnp.minimum((i + 1) * nblk_per_tile, nblk - 1), 0)


def _pre_norm_bwd(dh, x, pre_g, dres, dpre_ref):
    r = lax.rsqrt(jnp.mean(x * x, axis=-1, keepdims=True) + EPS)
    xh = x * r
    dpre_ref[...] += jnp.sum(dh * xh, axis=0, keepdims=True)
    dxh = dh * pre_g
    return dres + r * (dxh - xh * jnp.mean(dxh * xh, axis=-1, keepdims=True))


def _even_bwd2(dp, w_t, x, pre_g, dres, rider=None):
    s = x.shape[0]
    tm = min(512, s)

    def body(dp_ref, w_ref, x_ref, pg, dres_ref, dx_ref, dpre_ref):
        _acc_init(pl.program_id(0), (dpre_ref,))
        dh = _dot(dp_ref[...], w_ref[...])
        dx_ref[...] = _pre_norm_bwd(dh, x_ref[...], pg[...], dres_ref[...], dpre_ref)

    row = pl.BlockSpec((tm, D), lambda i: (i, 0))
    return _host_call(
        body, grid=(s // tm,), name="even_bwd2", rider=rider,
        out_shape=(jax.ShapeDtypeStruct((s, D), F32), jax.ShapeDtypeStruct((1, D), F32)),
        in_specs=[pl.BlockSpec((tm, EVEN_IN), lambda i: (i, 0)), _resident((EVEN_IN, D)), row, _const((1, D)), row],
        out_specs=(row, _const((1, D))),
        args=(dp, w_t, x, pre_g, dres))


def _odd_bwd2(dpc, tmpc, tmpd, p, dww, w_t, x, pre_g, dres):
    s = x.shape[0]
    nt = s // TS

    def body(dpc_ref, tc_ref, tch_ref, td_ref, tdh_ref, ga_ref, gb_ref, bands_ref, dww_ref, w_ref, x_ref, pg,
             dres_ref, dpb_ref, dx_ref, dpre_ref, dbuf, dsh):
        i = pl.program_id(0)
        _acc_init(i, (dpre_ref,))
        more = i < nt - 1
        e_bf = tc_ref[...]
        eh = tch_ref[...]
        ecat = jnp.concatenate([e_bf, jnp.where(more, eh, jnp.zeros_like(eh))], axis=0)
        dbuf[0:TS, :] = td_ref[...].astype(F32)
        dbuf[TS:TS + HALO, :] = jnp.where(more, tdh_ref[...].astype(F32), 0.0)
        sums = [_dot(bands_ref[w], ecat) for w in range(len(POOL_WINDOWS))]
        rows = _row_ids(i, TS) + 1
        cnt = _pool_select([jnp.minimum(rows, w).astype(F32) for w in POOL_WINDOWS])
        dzc = (_pool_select(sums) - e_bf.astype(F32) * cnt).astype(BF16)
        _shifted_copies(dbuf, dsh)
        dz = dww_ref[CONF - 1:CONF, :] * dbuf[pl.ds(0, TS), :]
        for sft in range(1, CONF):
            dz = dz + dww_ref[CONF - 1 - sft:CONF - sft, :] * _rows_at(dbuf, dsh, sft, TS)
        ga = ga_ref[...].astype(F32)
        sgb = _sigmoid(gb_ref[...].astype(F32))
        dga = (dz * sgb).astype(BF16)
        dgb = (dz * ga * sgb * (1.0 - sgb)).astype(BF16)
        dpb_ref[:, 0:BW] = dzc
        dpb_ref[:, BW:2 * BW] = dga
        dpb_ref[:, 2 * BW:3 * BW] = dgb
        dh = (_dot(dzc, w_ref[0:BW, :]) + _dot(dga, w_ref[BW:2 * BW, :]) + _dot(dgb, w_ref[2 * BW:3 * BW, :])
              + _dot(dpc_ref[...], w_ref[3 * BW:ODD_IN, :]))
        dx_ref[...] = _pre_norm_bwd(dh, x_ref[...], pg[...], dres_ref[...], dpre_ref)

    row = pl.BlockSpec((TS, D), lambda i: (i, 0))

    def tile(n, j=0):
        return pl.BlockSpec((TS, n), lambda i: (i, j))

    nxt = pl.BlockSpec((HALO, BW), _halo_next(TS // HALO, s // HALO))
    return pl.pallas_call(
        body, grid=(nt,), name="odd_bwd2",
        out_shape=(jax.ShapeDtypeStruct((s, 3 * BW), BF16), jax.ShapeDtypeStruct((s, D), F32),
                   jax.ShapeDtypeStruct((1, D), F32)),
        in_specs=[tile(XA + MIX), tile(BW), nxt, tile(BW), nxt, tile(BW, 1), tile(BW, 2), _const((4, TS, HALO + TS)),
                  _const((CONF, BW)), _resident((ODD_IN, D)), row, _const((1, D)), row],
        out_specs=(tile(3 * BW), row, _const((1, D))),
        scratch_shapes=[pltpu.VMEM((TS + HALO, BW), F32), pltpu.VMEM((7, SHIFT_ROWS, BW), F32)],
        compiler_params=_cp(("arbitrary",)),
    )(dpc, tmpc, tmpc, tmpd, tmpd, p, p, _band_matrices(TS, True), dww, w_t, x, pre_g, dres)


def _grad_tn(a, b, tm, out=None, rows=None, row0=0, name="grad_tn", rider=None):
    s, m = a.shape
    n = b.shape[1]
    ts = min(2048, s)
    rows = m if rows is None else rows
    assert m % tm == 0 and s % ts == 0
    ns = s // ts
    if row0 % tm == 0:
        out_spec = pl.BlockSpec((tm, n), lambda i, k: (row0 // tm + i, 0))
    else:
        align = 16
        assert row0 % align == 0 and tm % align == 0
        out_spec = pl.BlockSpec((pl.Element(tm), pl.Element(n)),
                                lambda i, k: (pl.multiple_of(row0 + i * tm, align), 0))

    def body(*refs):
        a_ref, b_ref = refs[0], refs[1]
        o_ref, acc = refs[-2], refs[-1]
        k = pl.program_id(1)

        @pl.when(k == 0)
        def _():
            acc[...] = jnp.zeros_like(acc)

        acc[...] += _dot_tn(a_ref[...], b_ref[...])

        @pl.when(k == ns - 1)
        def _():
            o_ref[...] = acc[...].astype(BF16)

    in_specs = [pl.BlockSpec((ts, tm), lambda i, k: (k, i)), pl.BlockSpec((ts, n), lambda i, k: (k, 0))]
    args = [a, b]
    aliases = {}
    if out is not None:
        in_specs.append(pl.BlockSpec(memory_space=pltpu.HBM))
        args.append(out)
        aliases = {2: 0}
    (res,), got = _host_call(
        body, grid=(m // tm, ns), name=name, rider=rider, aliases=aliases,
        out_shape=(jax.ShapeDtypeStruct((rows, n), BF16),), in_specs=in_specs, out_specs=(out_spec,),
        scratch_shapes=[pltpu.VMEM((tm, n), F32)], args=args)
    return res if rider is None else (res, got)


def _place():
    x, y, c = lax.axis_index("x"), lax.axis_index("y"), lax.axis_index("c")
    chips = [(1 - x, y), (x, 1 - y), (1 - x, 1 - y)]
    return x, y, c, chips


def _hbm_specs(n):
    return [pl.BlockSpec(memory_space=pltpu.HBM)] * n


def _row_tile(r):
    for cand in (512, 400, 304, 256, 192, 128, 96, 16):
        if r % cand == 0:
            return cand
    raise ValueError(r)


def _place_shard(shard, place, dtype, name):
    r, cc = shard.shape
    tr = _row_tile(r)
    nt = r // tr

    def body(place_ref, s_ref, o_ref):
        o_ref[...] = s_ref[...].astype(dtype)

    return pl.pallas_call(
        body, name=name, out_shape=jax.ShapeDtypeStruct((N_CHIPS * r, cc), dtype),
        grid_spec=pltpu.PrefetchScalarGridSpec(
            num_scalar_prefetch=1, grid=(nt,),
            in_specs=[pl.BlockSpec((tr, cc), lambda i, pr: (i, 0))],
            out_specs=pl.BlockSpec((tr, cc), lambda i, pr: (pr[1] * nt + i, 0))),
        compiler_params=_cp(("arbitrary",)),
    )(place, shard)


class _GatherRider:
    has_mid = True

    def __init__(self, fulls):
        n = len(fulls)
        self.inputs = list(fulls)
        self.out_shapes = [jax.ShapeDtypeStruct(a.shape, a.dtype) for a in fulls]
        self.aliases = {a: a for a in range(n)}
        self.sems = [pltpu.SemaphoreType.DMA((6 * n,)), pltpu.SemaphoreType.DMA((6 * n,))]
        self.block_rows = [a.shape[0] // N_CHIPS for a in fulls]

    def _ctx(self, outs, sems):
        send_sems, recv_sems = sems
        x, y, c, chips = _place()

        def rows(a, k, half):
            r = self.block_rows[a]
            return outs[a].at[pl.ds(k * r + half * (r // 2), r // 2)]

        def copy(a, j, blk, to):
            return pltpu.make_async_remote_copy(src_ref=blk, dst_ref=blk, send_sem=send_sems.at[a * 6 + j],
                                                recv_sem=recv_sems.at[a * 6 + j], device_id=to, device_id_type=MESH)

        return x, y, c, chips, rows, copy

    def start(self, ins, outs, sems):
        x, y, c, chips, rows, copy = self._ctx(outs, sems)
        for j, (px, py) in enumerate(chips):
            for a in range(len(outs)):
                copy(a, j, rows(a, 2 * x + y, c), (px, py, c)).start()

    def mid(self, ins, outs, sems):
        x, y, c, chips, rows, copy = self._ctx(outs, sems)
        for j, (px, py) in enumerate(chips):
            for a in range(len(outs)):
                copy(a, j, rows(a, 2 * px + py, c), (px, py, c)).wait_recv()
                copy(a, 3 + j, rows(a, 2 * px + py, c), (x, y, 1 - c)).start()

    def end(self, ins, outs, sems):
        x, y, c, chips, rows, copy = self._ctx(outs, sems)
        for j, (px, py) in enumerate(chips):
            for a in range(len(outs)):
                copy(a, 3 + j, rows(a, 2 * px + py, 1 - c), (x, y, 1 - c)).wait_recv()
        for j, (px, py) in enumerate(chips):
            for a in range(len(outs)):
                copy(a, j, rows(a, 2 * x + y, c), (px, py, c)).wait_send()
                copy(a, 3 + j, rows(a, 2 * px + py, c), (x, y, 1 - c)).wait_send()


def _run_rider(rider, name):
    r_in, r_out = len(rider.inputs), len(rider.out_shapes)

    def body(*refs):
        ins, outs, sems = refs[:r_in], refs[r_in:r_in + r_out], refs[r_in + r_out:]
        rider.start(ins, outs, sems)
        rider.mid(ins, outs, sems)
        rider.end(ins, outs, sems)

    return pl.pallas_call(
        body, name=name, out_shape=tuple(rider.out_shapes), in_specs=_hbm_specs(r_in),
        out_specs=tuple(_hbm_specs(r_out)), input_output_aliases=dict(rider.aliases),
        scratch_shapes=list(rider.sems),
    )(*rider.inputs)


def _swap_halves(grads, small, name):
    n = len(grads)
    arrs = list(grads) + ([small] if small is not None else [])
    m = len(arrs)

    def body(*refs):
        ins, outs = refs[:m], refs[m:2 * m]
        send_sems, recv_sems = refs[2 * m:]
        x, y, c, _ = _place()
        sibling = (x, y, 1 - c)
        cps = []
        for a in range(m):
            src = ins[a].at[:, 1 - c] if a < n else ins[a]
            cp = pltpu.make_async_remote_copy(src_ref=src, dst_ref=outs[a], send_sem=send_sems.at[a],
                                              recv_sem=recv_sems.at[a], device_id=sibling, device_id_type=MESH)
            cp.start()
            cps.append(cp)
        for cp in cps:
            cp.wait_recv()
        for cp in cps:
            cp.wait_send()

    outs = tuple(jax.ShapeDtypeStruct((g.shape[0],) + g.shape[2:], g.dtype) for g in grads)
    if small is not None:
        outs += (jax.ShapeDtypeStruct(small.shape, small.dtype),)
    return pl.pallas_call(
        body, name=name, out_shape=outs, in_specs=_hbm_specs(m), out_specs=tuple(_hbm_specs(m)),
        scratch_shapes=[pltpu.SemaphoreType.DMA((m,)), pltpu.SemaphoreType.DMA((m,))],
    )(*arrs)


def _pair_sum(g, recv, place, name):
    _, _, h, cc = g.shape
    th = _row_tile(h)

    def body(c_ref, g_ref, r_ref, o_ref):
        o_ref[...] = (g_ref[...].astype(F32) + r_ref[...].astype(F32)).astype(o_ref.dtype)

    return pl.pallas_call(
        body, name=name, out_shape=jax.ShapeDtypeStruct(recv.shape, recv.dtype),
        grid_spec=pltpu.PrefetchScalarGridSpec(
            num_scalar_prefetch=1, grid=(N_CHIPS, h // th),
            in_specs=[pl.BlockSpec((None, None, th, cc), lambda k, r, c_ref: (k, c_ref[0], r, 0)),
                      pl.BlockSpec((None, th, cc), lambda k, r, c_ref: (k, r, 0))],
            out_specs=pl.BlockSpec((None, th, cc), lambda k, r, c_ref: (k, r, 0))),
        compiler_params=_cp(("arbitrary", "arbitrary")),
    )(place, g, recv)


def _small_allreduce(pack):
    rows, cc = pack.shape
    hs = rows // 2

    def body(pack_ref, out_ref, sib_ref, parts_ref, send_sems, recv_sems):
        x, y, c, chips = _place()
        me_k = 2 * x + y
        sibling = (x, y, 1 - c)
        mine = pl.ds(pl.multiple_of(c * hs, hs), hs)
        theirs = pl.ds(pl.multiple_of((1 - c) * hs, hs), hs)
        first = pltpu.make_async_remote_copy(src_ref=pack_ref, dst_ref=sib_ref, send_sem=send_sems.at[0],
                                             recv_sem=recv_sems.at[0], device_id=sibling, device_id_type=MESH)
        first.start()
        first.wait()
        parts_ref[me_k] = pack_ref[mine, :] + sib_ref[mine, :]
        cps = [pltpu.make_async_remote_copy(src_ref=parts_ref.at[me_k], dst_ref=parts_ref.at[me_k],
                                            send_sem=send_sems.at[1 + j], recv_sem=recv_sems.at[1 + j],
                                            device_id=(px, py, c), device_id_type=MESH)
               for j, (px, py) in enumerate(chips)]
        for cp in cps:
            cp.start()
        for j, (px, py) in enumerate(chips):
            pltpu.make_async_remote_copy(src_ref=parts_ref.at[2 * px + py], dst_ref=parts_ref.at[2 * px + py],
                                         send_sem=send_sems.at[1 + j], recv_sem=recv_sems.at[1 + j],
                                         device_id=(px, py, c), device_id_type=MESH).wait_recv()
        for cp in cps:
            cp.wait_send()
        out_ref[mine, :] = ((parts_ref[0] + parts_ref[1]) + parts_ref[2]) + parts_ref[3]
        last = pltpu.make_async_remote_copy(src_ref=out_ref.at[mine], dst_ref=out_ref.at[mine],
                                            send_sem=send_sems.at[4], recv_sem=recv_sems.at[4], device_id=sibling,
                                            device_id_type=MESH)
        last.start()
        pltpu.make_async_remote_copy(src_ref=out_ref.at[theirs], dst_ref=out_ref.at[theirs],
                                     send_sem=send_sems.at[4], recv_sem=recv_sems.at[4], device_id=sibling,
                                     device_id_type=MESH).wait_recv()
        last.wait_send()

    vmem = pl.BlockSpec(memory_space=pltpu.VMEM)
    return pl.pallas_call(
        body, name="small_allreduce", out_shape=jax.ShapeDtypeStruct(pack.shape, pack.dtype),
        in_specs=[vmem], out_specs=vmem,
        scratch_shapes=[pltpu.VMEM((rows, cc), F32), pltpu.VMEM((N_CHIPS, hs, cc), F32),
                        pltpu.SemaphoreType.DMA((5,)), pltpu.SemaphoreType.DMA((5,))],
        compiler_params=_cp(),
    )(pack)


class _ExchangeRider:
    has_mid = False

    def __init__(self, sums, small=None):
        self.n = len(sums)
        self.inputs = list(sums) + ([small] if small is not None else [])
        self.out_shapes = [jax.ShapeDtypeStruct((3,) + g.shape[1:], g.dtype) for g in sums]
        self.hs = 0
        if small is not None:
            self.hs = small.shape[0] // 2
            self.out_shapes.append(jax.ShapeDtypeStruct((N_CHIPS, self.hs, small.shape[1]), small.dtype))
        m = len(self.inputs)
        self.aliases = {}
        self.sems = [pltpu.SemaphoreType.DMA((3 * m,)), pltpu.SemaphoreType.DMA((3 * m,)), pltpu.SemaphoreType.DMA]

    def _copies(self, ins, outs, sems):
        send_sems, recv_sems, local_sem = sems
        x, y, c, chips = _place()
        me_k = 2 * x + y
        local, cps = None, []
        if self.hs:
            mine = ins[self.n].at[pl.ds(c * self.hs, self.hs)]
            local = pltpu.make_async_copy(mine, outs[self.n].at[me_k], local_sem)
        for j, (px, py) in enumerate(chips):
            for a in range(len(ins)):
                src, dst = (ins[a].at[2 * px + py], outs[a].at[j]) if a < self.n else (mine, outs[a].at[me_k])
                cps.append(pltpu.make_async_remote_copy(
                    src_ref=src, dst_ref=dst, send_sem=send_sems.at[a * 3 + j], recv_sem=recv_sems.at[a * 3 + j],
                    device_id=(px, py, c), device_id_type=MESH))
        return local, cps

    def start(self, ins, outs, sems):
        local, cps = self._copies(ins, outs, sems)
        if local is not None:
            local.start()
        for cp in cps:
            cp.start()

    def mid(self, ins, outs, sems):
        pass

    def end(self, ins, outs, sems):
        local, cps = self._copies(ins, outs, sems)
        for cp in cps:
            cp.wait_recv()
        for cp in cps:
            cp.wait_send()
        if local is not None:
            local.wait()


def _chip_sum(own, parts, place, name):
    npart, h, cc = parts.shape
    th = _row_tile(h)

    def body(*refs):
        p_ref, o_ref = refs[-2], refs[-1]
        acc = p_ref[0].astype(F32)
        if own is not None:
            acc = refs[1][...].astype(F32) + acc
        for k in range(1, npart):
            acc = acc + p_ref[k].astype(F32)
        o_ref[...] = acc

    in_specs = [pl.BlockSpec((npart, th, cc), lambda r, pr: (0, r, 0))]
    args = [parts]
    if own is not None:
        in_specs.insert(0, pl.BlockSpec((None, th, cc), lambda r, pr: (pr[1], r, 0)))
        args.insert(0, own)
    return pl.pallas_call(
        body, name=name, out_shape=jax.ShapeDtypeStruct((2, h, cc), F32),
        grid_spec=pltpu.PrefetchScalarGridSpec(
            num_scalar_prefetch=1, grid=(h // th,), in_specs=in_specs,
            out_specs=pl.BlockSpec((None, th, cc), lambda r, pr: (pr[0], r, 0))),
        compiler_params=_cp(("arbitrary",)),
    )(place, *args)


def _share_halves(halves):
    n = len(halves)

    def body(*refs):
        outs = refs[n:2 * n]
        send_sems, recv_sems = refs[2 * n:]
        x, y, c, _ = _place()
        cps = []
        for a in range(n):
            cp = pltpu.make_async_remote_copy(src_ref=outs[a].at[c], dst_ref=outs[a].at[c], send_sem=send_sems.at[a],
                                              recv_sem=recv_sems.at[a], device_id=(x, y, 1 - c), device_id_type=MESH)
            cp.start()
            cps.append(cp)
        for a in range(n):
            pltpu.make_async_remote_copy(src_ref=outs[a].at[1 - c], dst_ref=outs[a].at[1 - c], send_sem=send_sems.at[a],
                                         recv_sem=recv_sems.at[a], device_id=(x, y, 1 - c),
                                         device_id_type=MESH).wait_recv()
        for cp in cps:
            cp.wait_send()

    return pl.pallas_call(
        body, name="share_halves", out_shape=tuple(jax.ShapeDtypeStruct(g.shape, g.dtype) for g in halves),
        in_specs=_hbm_specs(n), out_specs=tuple(_hbm_specs(n)), input_output_aliases={a: a for a in range(n)},
        scratch_shapes=[pltpu.SemaphoreType.DMA((n,)), pltpu.SemaphoreType.DMA((n,))],
    )(*halves)


def _adamw_math(w, g, m, v):
    m = ADAM_B1 * m + (1.0 - ADAM_B1) * g
    v = ADAM_B2 * v + (1.0 - ADAM_B2) * (g * g)
    m_hat = m / (1.0 - ADAM_B1 ** ADAM_STEP)
    v_hat = v / (1.0 - ADAM_B2 ** ADAM_STEP)
    delta = -ADAM_LR * (m_hat / (jnp.sqrt(v_hat) + ADAM_EPS) + ADAM_WD * w)
    return delta, m, v


def _adamw_big(w, g, m, v, name):
    r, cc = w.shape
    tr = min(_row_tile(r), 256) if r % 256 == 0 else _row_tile(r)

    def body(w_ref, g_ref, m_ref, v_ref, d_ref, mo_ref, vo_ref):
        d, mm, vv = _adamw_math(w_ref[...], g_ref[...], m_ref[...], v_ref[...])
        d_ref[...] = d
        mo_ref[...] = mm
        vo_ref[...] = vv

    blk = pl.BlockSpec((tr, cc), lambda i: (i, 0))
    sd = jax.ShapeDtypeStruct((r, cc), F32)
    return pl.pallas_call(body, grid=(r // tr,), name=name, out_shape=(sd, sd, sd), in_specs=[blk] * 4,
                          out_specs=(blk, blk, blk), compiler_params=_cp(("arbitrary",)))(w, g, m, v)


def _adamw_small(ws, gs, ms, vs):
    n = len(ws)

    def body(*refs):
        for a in range(n):
            w_ref, g_ref, m_ref, v_ref = refs[4 * a:4 * a + 4]
            d_ref, mo_ref, vo_ref = refs[4 * n + 3 * a:4 * n + 3 * a + 3]
            d, mm, vv = _adamw_math(w_ref[...], g_ref[...], m_ref[...], v_ref[...])
            d_ref[...] = d
            mo_ref[...] = mm
            vo_ref[...] = vv

    args, outs = [], []
    for a in range(n):
        args += [ws[a], gs[a], ms[a], vs[a]]
        outs += [jax.ShapeDtypeStruct(ws[a].shape, F32)] * 3
    res = pl.pallas_call(body, name="adamw_small", out_shape=tuple(outs), compiler_params=_cp())(*args)
    return [res[3 * a:3 * a + 3] for a in range(n)]


def _flat_pack(arrs, rows):
    flat = jnp.concatenate([a.reshape(-1) for a in arrs])
    return jnp.pad(flat, (0, rows * D - flat.shape[0])).reshape(rows, D)


def _flat_unpack(flat, shapes):
    out, off = [], 0
    for shp in shapes:
        size = 1
        for d_ in shp:
            size *= d_
        out.append(flat[off:off + size].reshape(shp))
        off += size
    return out


SMALL_EVEN = ("even_pre_g", "even_a_ln_g", "even_a_ln_b", "even_a_ws", "even_a_bs", "even_b_conv", "even_mem_g",
              "even_post_g")
SMALL_ODD = ("odd_pre_g", "odd_c_wgrp", "odd_c_scale", "odd_d_dw_w", "odd_d_dw_b", "odd_d_ln_g", "odd_d_ln_b",
             "odd_d_pw_b", "odd_mem_g", "odd_post_g")
BIG = ("even_w_in", "even_w_kv", "even_w_out", "odd_w_in", "odd_d_pw_w", "odd_w_kv", "odd_w_out")
WEIGHTS = ("even_pre_g", "even_w_in", "even_a_ln_g", "even_a_ln_b", "even_a_ws", "even_a_bs", "even_b_conv",
           "even_mem_g", "even_w_kv", "even_w_out", "even_post_g", "odd_pre_g", "odd_w_in", "odd_c_wgrp",
           "odd_c_scale", "odd_d_dw_w", "odd_d_dw_b", "odd_d_ln_g", "odd_d_ln_b", "odd_d_pw_w", "odd_d_pw_b",
           "odd_mem_g", "odd_w_kv", "odd_w_out", "odd_post_g")
PACKED = (("even_b_conv", (3, 192)), ("odd_pre_g", (1, 256)), ("odd_c_scale", (1, 192)), ("odd_d_dw_w", (31, 192)),
          ("odd_d_dw_b", (1, 192)), ("odd_d_ln_g", (1, 192)), ("odd_d_ln_b", (1, 192)), ("odd_d_pw_b", (1, 192)),
          ("odd_mem_g", (1, 256)), ("odd_post_g", (1, 256)))
PACK_ROWS = 16
SMALL_ROWS = 256


def _four(g):
    return g.reshape(N_CHIPS, 2, g.shape[0] // (2 * N_CHIPS), g.shape[1])


def _step(x, mem, target, w, place):
    wt = {}
    pack = _flat_pack([w[n][0] for n, _ in PACKED], PACK_ROWS)
    shards = {"even_w_in_t": w["even_w_in"][0].T, "odd_w_in_t": w["odd_w_in"][0].T, "even_w_kv": w["even_w_kv"][0],
              "odd_w_kv": w["odd_w_kv"][0], "even_w_out": w["even_w_out"][0], "odd_w_out": w["odd_w_out"][0],
              "odd_d_pw_w": w["odd_d_pw_w"][0]}
    placed = {n: _place_shard(a, place, BF16, "place_" + n) for n, a in shards.items()}
    placed["pack"] = _place_shard(pack, place, F32, "place_pack")

    wt["even_w_in_t"], packs = _run_rider(_GatherRider([placed["even_w_in_t"], placed["pack"]]), "gather_first")
    packs = packs.reshape(N_CHIPS, PACK_ROWS * D)
    per_chip = [_flat_unpack(packs[k], [shp for _, shp in PACKED]) for k in range(N_CHIPS)]
    for a, (name, _) in enumerate(PACKED):
        wt[name] = jnp.concatenate([per_chip[k][a] for k in range(N_CHIPS)], axis=-1)
    for name in ("even_pre_g", "even_a_ln_g", "even_a_ln_b", "even_mem_g", "even_post_g"):
        wt[name] = w[name]

    tril = jnp.tril(jnp.ones((CH, CH), dtype=bool))
    wcat = jnp.where(tril[None], w["even_a_ws"][0], 0.0).transpose(1, 0, 2).reshape(CH, 4 * CH).astype(BF16)
    bsg = jnp.repeat(w["even_a_bs"][0].T, BW // 4, axis=1)
    hsel = (jnp.arange(BW)[:, None] // (BW // 4) == jnp.arange(128)[None, :]).astype(BF16)
    wg = w["odd_c_wgrp"][0]
    g4 = BW // 4
    wbd = jnp.zeros((BW, BW), F32)
    for g in range(4):
        wbd = lax.dynamic_update_slice(wbd, wg[g], (g * g4, g * g4))
    wbd = wbd.astype(BF16)

    names = ("even_w_kv", "even_w_out", "odd_w_kv", "odd_d_pw_w")
    (p_e, h_e), got = _in_fwd(x, wt["even_pre_g"], wt["even_w_in_t"], "even_in",
                              rider=_GatherRider([placed[n] for n in names]))
    wt.update(zip(names, got))
    kv_e = _kv_fwd(mem, wt["even_mem_g"], wt["even_w_kv"], "even_kv")
    (x1, o_e), got = _even_fwd(x, p_e, kv_e, wt["even_a_ln_g"], wt["even_a_ln_b"], wcat, bsg, wt["even_b_conv"],
                               wt["even_w_out"], wt["even_post_g"], rider=_GatherRider([placed["odd_w_in_t"]]))
    wt["odd_w_in_t"] = got[0]
    (p_o, h_o), got = _in_fwd(x1, wt["odd_pre_g"], wt["odd_w_in_t"], "odd_in",
                              rider=_GatherRider([placed["odd_w_out"]]))
    wt["odd_w_out"] = got[0]
    kv_o = _kv_fwd(mem, wt["odd_mem_g"], wt["odd_w_kv"], "odd_kv")
    dx2, o_o, cv_o, loss = _odd_fwd(x1, p_o, kv_o, wbd, wt["odd_c_scale"], wt["odd_d_dw_w"], wt["odd_d_dw_b"],
                                    wt["odd_d_ln_g"], wt["odd_d_ln_b"], wt["odd_d_pw_w"], wt["odd_d_pw_b"],
                                    wt["odd_w_out"], wt["odd_post_g"], target)
    (dpc_o, tmpc, tmpd, do_o, y_o, g_post_o, g_cs, g_wbd, g_dww, g_dwb, g_lng_o, g_lnb_o, g_pww, g_pwb,
     dkv_o) = _odd_bwd1(dx2, o_o, cv_o, p_o, kv_o, wbd, wt["odd_c_scale"], wt["odd_d_dw_w"], wt["odd_d_dw_b"],
                        wt["odd_d_ln_g"], wt["odd_d_ln_b"], wt["odd_d_pw_w"], wt["odd_d_pw_b"], wt["odd_w_out"],
                        wt["odd_post_g"])
    dpb_o, dx1, g_pre_o = _odd_bwd2(dpc_o, tmpc, tmpd, p_o, wt["odd_d_dw_w"], wt["odd_w_in_t"], x1,
                                    wt["odd_pre_g"], dx2)
    g_win_o = _grad_tn(dpb_o, h_o, 768, rows=ODD_IN, name="odd_gw_in_b")
    g_win_o = _grad_tn(dpc_o, h_o, 1280, out=g_win_o, rows=ODD_IN, row0=3 * BW, name="odd_gw_in_c")
    g_wout_o = _grad_tn(y_o, do_o, 1024, name="odd_gw_out")
    g_wkv_o, g_memg_o = _kv_bwd(mem, wt["odd_mem_g"], wt["odd_w_kv"], dkv_o, "odd_kv_bwd")
    big_o = [_four(g) for g in (g_win_o, g_pww.astype(BF16), g_wkv_o, g_wout_o)]
    recv_o = _swap_halves(big_o, None, "swap_halves_odd")
    sums_o = [_pair_sum(big_o[a], recv_o[a], place, "pair_sum_odd_%d" % a) for a in range(len(big_o))]
    (dp_e, do_e, y_e, g_post_e, g_lng_e, g_lnb_e, g_wcat, g_bs, g_bconv,
     dkv_e), parts_o = _even_bwd1(dx1, o_e, p_e, kv_e, wt["even_a_ln_g"], wt["even_a_ln_b"], wcat, bsg, hsel,
                                  wt["even_b_conv"], wt["even_w_out"], wt["even_post_g"],
                                  rider=_ExchangeRider(sums_o))
    halves_o = [_chip_sum(sums_o[a], parts_o[a], place, "chip_sum_odd_%d" % a) for a in range(len(big_o))]
    g_wout_e = _grad_tn(y_e, do_e, 1024, name="even_gw_out")
    g_wkv_e, g_memg_e = _kv_bwd(mem, wt["even_mem_g"], wt["even_w_kv"], dkv_e, "even_kv_bwd")
    big_x = [_four(g) for g in (g_wkv_e, g_wout_e)]
    recv_x = _swap_halves(big_x, None, "swap_halves_kv_out")
    sums_x = [_pair_sum(big_x[a], recv_x[a], place, "pair_sum_kv_out_%d" % a) for a in range(len(big_x))]
    g_win_e, parts_x = _grad_tn(dp_e, h_e, 1280, name="even_gw_in", rider=_ExchangeRider(sums_x))
    halves_x = [_chip_sum(sums_x[a], parts_x[a], place, "chip_sum_kv_out_%d" % a) for a in range(len(big_x))]
    big_e = [_four(g_win_e)]
    recv_e = _swap_halves(big_e, None, "swap_halves_even")
    sums_e = [_pair_sum(big_e[0], recv_e[0], place, "pair_sum_even_w_in")]
    (dx0, g_pre_e), parts_e = _even_bwd2(dp_e, wt["even_w_in_t"], x, wt["even_pre_g"], dx1,
                                         rider=_ExchangeRider(sums_e))
    halves_e = [_chip_sum(sums_e[0], parts_e[0], place, "chip_sum_even_w_in")]

    g_aws = jnp.where(tril[None], g_wcat.reshape(CH, 4, CH).transpose(1, 0, 2), 0.0)
    g_wgrp = jnp.stack([lax.dynamic_slice(g_wbd, (g * g4, g * g4), (g4, g4)) for g in range(4)])
    small = {
        "even_pre_g": g_pre_e, "even_a_ln_g": g_lng_e, "even_a_ln_b": g_lnb_e, "even_a_ws": g_aws,
        "even_a_bs": g_bs[:, 0:4].T, "even_b_conv": g_bconv[0:3], "even_mem_g": g_memg_e, "even_post_g": g_post_e,
        "odd_pre_g": g_pre_o, "odd_c_wgrp": g_wgrp, "odd_c_scale": g_cs, "odd_d_dw_w": g_dww.reshape(CONF, 8, BW).sum(axis=1),
        "odd_d_dw_b": g_dwb, "odd_d_ln_g": g_lng_o, "odd_d_ln_b": g_lnb_o, "odd_d_pw_b": g_pwb,
        "odd_mem_g": g_memg_o, "odd_post_g": g_post_o,
    }
    small_names = SMALL_EVEN + SMALL_ODD
    small_pack = _flat_pack([small[n] for n in small_names] + [loss[0, 0].reshape(1)], SMALL_ROWS)
    small_total = _small_allreduce(small_pack)
    full = _share_halves(halves_e + halves_x + halves_o)
    order = ("even_w_in", "even_w_kv", "even_w_out", "odd_w_in", "odd_d_pw_w", "odd_w_kv", "odd_w_out")
    gbig = {n: full[a].reshape(full[a].shape[1] * 2, full[a].shape[2]) for a, n in enumerate(order)}
    return dx0, gbig, small_total.reshape(-1), [small[n].shape for n in small_names]


def kernel(x, mem, even_pre_g, even_w_in, even_a_ln_g, even_a_ln_b, even_a_ws, even_a_bs, even_b_conv, even_mem_g, even_w_kv, even_w_out, even_post_g, odd_pre_g, odd_w_in, odd_c_wgrp, odd_c_scale, odd_d_dw_w, odd_d_dw_b, odd_d_ln_g, odd_d_ln_b, odd_d_pw_w, odd_d_pw_b, odd_mem_g, odd_w_kv, odd_w_out, odd_post_g, loss_target, m_even_pre_g, m_even_w_in, m_even_a_ln_g, m_even_a_ln_b, m_even_a_ws, m_even_a_bs, m_even_b_conv, m_even_mem_g, m_even_w_kv, m_even_w_out, m_even_post_g, m_odd_pre_g, m_odd_w_in, m_odd_c_wgrp, m_odd_c_scale, m_odd_d_dw_w, m_odd_d_dw_b, m_odd_d_ln_g, m_odd_d_ln_b, m_odd_d_pw_w, m_odd_d_pw_b, m_odd_mem_g, m_odd_w_kv, m_odd_w_out, m_odd_post_g, v_even_pre_g, v_even_w_in, v_even_a_ln_g, v_even_a_ln_b, v_even_a_ws, v_even_a_bs, v_even_b_conv, v_even_mem_g, v_even_w_kv, v_even_w_out, v_even_post_g, v_odd_pre_g, v_odd_w_in, v_odd_c_wgrp, v_odd_c_scale, v_odd_d_dw_w, v_odd_d_dw_b, v_odd_d_ln_g, v_odd_d_ln_b, v_odd_d_pw_w, v_odd_d_pw_b, v_odd_mem_g, v_odd_w_kv, v_odd_w_out, v_odd_post_g):
    given = dict(locals())
    w = {n: given[n] for n in WEIGHTS}
    mom = {n: given["m_" + n] for n in WEIGHTS}
    var = {n: given["v_" + n] for n in WEIGHTS}

    x_, y_, c_ = lax.axis_index("x"), lax.axis_index("y"), lax.axis_index("c")
    chip = 2 * x_ + y_
    place = jnp.stack([c_, chip]).astype(jnp.int32)
    grad_x, gbig, gsmall_flat, small_shapes = _step(x[0], mem[0], loss_target[0], w, place)

    names = SMALL_EVEN + SMALL_ODD
    grads = {}
    unpacked = _flat_unpack(gsmall_flat, small_shapes + [(1,)])
    loss = unpacked[-1][0]
    for n, g in zip(names, unpacked[:-1]):
        shard_shape = w[n].shape[1:]
        if g.shape[-1] != shard_shape[-1]:
            g = lax.dynamic_slice_in_dim(g, chip * shard_shape[-1], shard_shape[-1], axis=g.ndim - 1)
        grads[n] = g.reshape(shard_shape)

    def two_d(a):
        return a.reshape(-1, a.shape[-1])

    upd = {}
    for n in BIG:
        if n.endswith("w_in"):
            res = _adamw_big(w[n][0].T, gbig[n], mom[n][0].T, var[n][0].T, "adamw_" + n)
            grads[n] = gbig[n].T
            upd[n] = tuple(r.T for r in res)
        else:
            grads[n] = gbig[n]
            upd[n] = _adamw_big(w[n][0], gbig[n], mom[n][0], var[n][0], "adamw_" + n)
    res = _adamw_small([two_d(w[n][0]) for n in names], [two_d(grads[n]) for n in names],
                       [two_d(mom[n][0]) for n in names], [two_d(var[n][0]) for n in names])
    for n, r in zip(names, res):
        upd[n] = r

    outs = [loss, grad_x[None]]
    outs += [grads[n].reshape(w[n].shape) for n in WEIGHTS]
    for j in range(3):
        outs += [upd[n][j].reshape(w[n].shape) for n in WEIGHTS]
    return tuple(outs)
```

```python
import functools

import jax
import jax.numpy as jnp
from jax import lax
from jax.experimental import pallas as pl
from jax.experimental.pallas import tpu as pltpu

F32 = jnp.float32
BF16 = jnp.bfloat16
MESH = pl.DeviceIdType.MESH

D = 1024
N_MEM = 256
MIX = 2048
XA = 512
HD = 128
BW = 768
CH = 128
EPS = 1e-6
SCALE = HD ** -0.5
POOL_WINDOWS = (2, 4, 8, 16)
CONF = 31
EVEN_IN = 6400
ODD_IN = 4864
N_CHIPS = 4

ADAM_LR = 0.001
ADAM_B1 = 0.9
ADAM_B2 = 0.999
ADAM_EPS = 1e-08
ADAM_WD = 0.01
ADAM_STEP = 10

TS = 256
HALO = 32
VMEM_LIMIT = 56 * 1024 * 1024


def _cp(sem=None):
    return pltpu.CompilerParams(dimension_semantics=sem, vmem_limit_bytes=VMEM_LIMIT)


def _dot(a, b):
    return jnp.dot(a, b, preferred_element_type=F32)


def _dot_nt(a, b):
    return lax.dot_general(a, b, (((1,), (1,)), ((), ())), preferred_element_type=F32)


def _dot_tn(a, b):
    return lax.dot_general(a, b, (((0,), (0,)), ((), ())), preferred_element_type=F32)


def _sigmoid(x):
    return 1.0 / (1.0 + jnp.exp(-x))


def _resident(shape):
    return pl.BlockSpec(shape, lambda *_: (0,) * len(shape), pipeline_mode=pl.Buffered(1))


def _const(shape):
    return pl.BlockSpec(shape, lambda *_: (0,) * len(shape))


def _kv_fwd(mem, mem_g, wkv, name):
    def body(mem_ref, g_ref, w_ref, kv_ref):
        m = mem_ref[...]
        r = lax.rsqrt(jnp.mean(m * m, axis=-1, keepdims=True) + EPS)
        mn = (m * r * g_ref[...]).astype(BF16)
        kv_ref[...] = _dot(mn, w_ref[...]).astype(BF16)

    return pl.pallas_call(body, out_shape=jax.ShapeDtypeStruct((N_MEM, D), BF16), name=name,
                          compiler_params=_cp())(mem, mem_g, wkv)


def _kv_bwd(mem, mem_g, wkv, dkv, name):
    def body(mem_ref, g_ref, w_ref, dkv_ref, dw_ref, dg_ref):
        m = mem_ref[...]
        r = lax.rsqrt(jnp.mean(m * m, axis=-1, keepdims=True) + EPS)
        mh = m * r
        mn = (mh * g_ref[...]).astype(BF16)
        dkv = dkv_ref[...].astype(BF16)
        dw_ref[...] = _dot_tn(mn, dkv).astype(BF16)
        dmn = _dot_nt(dkv, w_ref[...])
        dg_ref[...] = jnp.sum(dmn * mh, axis=0, keepdims=True)

    return pl.pallas_call(body, out_shape=(jax.ShapeDtypeStruct((D, D), BF16), jax.ShapeDtypeStruct((1, D), F32)),
                          name=name, compiler_params=_cp())(mem, mem_g, wkv, dkv)


def _host_call(body, *, grid, name, out_shape, in_specs, out_specs, args, scratch_shapes=(), aliases=None,
               rider=None):
    sem = ("arbitrary",) * len(grid)
    aliases = dict(aliases or {})
    if rider is None:
        res = pl.pallas_call(body, grid=grid, name=name, out_shape=tuple(out_shape), in_specs=list(in_specs),
                             out_specs=tuple(out_specs), scratch_shapes=list(scratch_shapes),
                             input_output_aliases=aliases, compiler_params=_cp(sem))(*args)
        return tuple(res), ()
    n_in, n_out, n_sc = len(in_specs), len(out_specs), len(scratch_shapes)
    r_in, r_out = len(rider.inputs), len(rider.out_shapes)

    def full_body(*refs):
        host_in = refs[:n_in]
        rid_in = refs[n_in:n_in + r_in]
        host_out = refs[n_in + r_in:n_in + r_in + n_out]
        rid_out = refs[n_in + r_in + n_out:n_in + r_in + n_out + r_out]
        host_sc = refs[n_in + r_in + n_out + r_out:n_in + r_in + n_out + r_out + n_sc]
        sems = refs[n_in + r_in + n_out + r_out + n_sc:]
        first = pl.program_id(0) == 0
        last = pl.program_id(0) == grid[0] - 1
        for ax in range(1, len(grid)):
            first = jnp.logical_and(first, pl.program_id(ax) == 0)
            last = jnp.logical_and(last, pl.program_id(ax) == grid[ax] - 1)

        @pl.when(first)
        def _():
            rider.start(rid_in, rid_out, sems)

        if rider.has_mid:
            @pl.when(last)
            def _():
                rider.mid(rid_in, rid_out, sems)

        body(*host_in, *host_out, *host_sc)

        @pl.when(last)
        def _():
            rider.end(rid_in, rid_out, sems)

    aliases.update({n_in + j: n_out + k for j, k in rider.aliases.items()})
    res = pl.pallas_call(
        full_body, grid=grid, name=name, out_shape=tuple(out_shape) + tuple(rider.out_shapes),
        in_specs=list(in_specs) + _hbm_specs(r_in), out_specs=tuple(out_specs) + tuple(_hbm_specs(r_out)),
        scratch_shapes=list(scratch_shapes) + list(rider.sems), input_output_aliases=aliases,
        compiler_params=_cp(sem),
    )(*args, *rider.inputs)
    return tuple(res[:n_out]), tuple(res[n_out:])


def _in_fwd(x, pre_g, w_t, name, rider=None):
    s, n = x.shape[0], w_t.shape[0]
    tm = min(512, s)
    nc = 256

    def body(x_ref, g_ref, w_ref, p_ref, h_ref):
        xv = x_ref[...]
        r = lax.rsqrt(jnp.mean(xv * xv, axis=-1, keepdims=True) + EPS)
        h = (xv * r * g_ref[...]).astype(BF16)
        h_ref[...] = h
        for j in range(n // nc):
            p_ref[:, j * nc:(j + 1) * nc] = _dot_nt(h, w_ref[j * nc:(j + 1) * nc, :]).astype(BF16)

    return _host_call(
        body, grid=(s // tm,), name=name, rider=rider,
        out_shape=(jax.ShapeDtypeStruct((s, n), BF16), jax.ShapeDtypeStruct((s, D), BF16)),
        in_specs=[pl.BlockSpec((tm, D), lambda i: (i, 0)), _const((1, D)), _resident((n, D))],
        out_specs=(pl.BlockSpec((tm, n), lambda i: (i, 0)), pl.BlockSpec((tm, D), lambda i: (i, 0))),
        args=(x, pre_g, w_t))


def _xattn_fwd(q, kv_ref):
    outs, probs = [], []
    for h in range(XA // HD):
        qh = q[:, h * HD:(h + 1) * HD]
        kh = kv_ref[:, h * HD:(h + 1) * HD]
        vh = kv_ref[:, XA + h * HD:XA + (h + 1) * HD]
        sc = _dot_nt(qh, kh) * SCALE
        e = jnp.exp(sc - jnp.max(sc, axis=-1, keepdims=True))
        pr = e / jnp.sum(e, axis=-1, keepdims=True)
        outs.append(_dot(pr.astype(BF16), vh))
        probs.append(pr)
    return jnp.concatenate(outs, axis=-1), probs


def _xattn_bwd(dyx, q, probs, kv_ref, dkv_ref):
    dqs = []
    for h in range(XA // HD):
        qh = q[:, h * HD:(h + 1) * HD]
        kh = kv_ref[:, h * HD:(h + 1) * HD]
        vh = kv_ref[:, XA + h * HD:XA + (h + 1) * HD]
        dy = dyx[:, h * HD:(h + 1) * HD].astype(BF16)
        pr = probs[h]
        dp = _dot_nt(dy, vh)
        ds = (pr * (dp - jnp.sum(dp * pr, axis=-1, keepdims=True))).astype(BF16)
        dqs.append(_dot(ds, kh) * SCALE)
        dkv_ref[:, h * HD:(h + 1) * HD] += _dot_tn(ds, qh) * SCALE
        dkv_ref[:, XA + h * HD:XA + (h + 1) * HD] += _dot_tn(pr.astype(BF16), dy)
    return jnp.concatenate(dqs, axis=-1)


def _layer_norm_fwd(v, g, b):
    mu = jnp.mean(v, axis=-1, keepdims=True)
    vc = v - mu
    rstd = lax.rsqrt(jnp.mean(vc * vc, axis=-1, keepdims=True) + EPS)
    vhat = vc * rstd
    return vhat * g + b, vhat, rstd


def _layer_norm_bwd(dy, vhat, rstd, g):
    dvh = dy * g
    return rstd * (dvh - jnp.mean(dvh, axis=-1, keepdims=True) - vhat * jnp.mean(dvh * vhat, axis=-1, keepdims=True))


def _head_masks():
    col = lax.broadcasted_iota(jnp.int32, (1, BW), 1)
    return [(col >= h * (BW // 4)) & (col < (h + 1) * (BW // 4)) for h in range(4)]


def _halo_prev(nblk_per_tile):
    return lambda i: (jnp.maximum(i * nblk_per_tile - 1, 0), 0)


def _row_ids(i, t):
    return i * t + lax.broadcasted_iota(jnp.int32, (t, 1), 0)


def _even_mix(i, p_ref, ph_ref, ln_g, ln_b, wcat_ref, bsg_ref, bconv_ref, wbuf):
    t = p_ref.shape[0]
    u = p_ref[:, 0:BW].astype(F32)
    v = p_ref[:, BW:2 * BW].astype(F32)
    bg = p_ref[:, 2 * BW:3 * BW].astype(F32)
    cg = p_ref[:, 3 * BW:4 * BW].astype(F32)
    xin = p_ref[:, 4 * BW:5 * BW].astype(F32)
    vn, vhat, rstd = _layer_norm_fwd(v, ln_g, ln_b)
    masks = _head_masks()
    sgs, vsts = [], []
    for n in range(t // CH):
        vn_c = vn[n * CH:(n + 1) * CH]
        vst = jnp.concatenate([jnp.where(m, vn_c, 0.0) for m in masks], axis=0).astype(BF16)
        sgs.append(_dot(wcat_ref[...], vst) + bsg_ref[...])
        vsts.append(vst)
    sg = jnp.concatenate(sgs, axis=0)
    ya = u * sg
    w_halo = ph_ref[:, 3 * BW:4 * BW].astype(F32) * ph_ref[:, 4 * BW:5 * BW].astype(F32)
    wbuf[0:HALO, :] = jnp.where(i > 0, w_halo, 0.0)
    wbuf[HALO:HALO + t, :] = cg * xin
    conv = (bconv_ref[0:1, :] * wbuf[pl.ds(HALO - 2, t), :] + bconv_ref[1:2, :] * wbuf[pl.ds(HALO - 1, t), :]
            + bconv_ref[2:3, :] * wbuf[pl.ds(HALO, t), :])
    yb = bg * conv
    return dict(u=u, bg=bg, cg=cg, xin=xin, vhat=vhat, rstd=rstd, sg=sg, vsts=vsts, conv=conv, ya=ya, yb=yb,
                masks=masks)


def _pool_select(vals):
    col = lax.broadcasted_iota(jnp.int32, (1, BW), 1)
    g = BW // 4
    return jnp.where(col < g, vals[0], jnp.where(col < 2 * g, vals[1], jnp.where(col < 3 * g, vals[2], vals[3])))


def _inv_counts(i, t):
    rows = _row_ids(i, t) + 1
    return [1.0 / jnp.minimum(rows, w).astype(F32) for w in POOL_WINDOWS]


def _band_matrices(t, forward):
    j = jnp.arange(t)[:, None]
    r = jnp.arange(HALO + t)[None, :]
    if forward:
        return jnp.stack([(r >= j) & (r < j + w) for w in POOL_WINDOWS]).astype(BF16)
    return jnp.stack([(r <= HALO + j) & (r > HALO + j - w) for w in POOL_WINDOWS]).astype(BF16)


SHIFT_ROWS = HALO + TS - 8


def _shifted_copies(buf, sh):
    for b in range(1, 8):
        sh[b - 1] = buf[pl.ds(b, SHIFT_ROWS), :]


def _rows_at(buf, sh, off, t):
    a, b = divmod(off, 8)
    return buf[pl.ds(8 * a, t), :] if b == 0 else sh[b - 1, pl.ds(8 * a, t), :]


def _tap_sums(d_ref, buf, sh, base, out_ref):
    t = d_ref.shape[0]
    group = 4
    for k0 in range(0, CONF, group):
        taps = list(range(k0, min(k0 + group, CONF)))

        def step(r, accs, taps=taps):
            row = pl.multiple_of(r * 8, 8)
            d = d_ref[pl.ds(row, 8), :]
            new = []
            for acc, k in zip(accs, taps):
                a, b = divmod(base + k, 8)
                src = buf[pl.ds(row + 8 * a, 8), :] if b == 0 else sh[b - 1, pl.ds(row + 8 * a, 8), :]
                new.append(acc + d * src)
            return tuple(new)

        accs = lax.fori_loop(0, t // 8, step, tuple(jnp.zeros((8, BW), F32) for _ in taps), unroll=2)
        for acc, k in zip(accs, taps):
            out_ref[8 * k:8 * k + 8, :] += acc


def _odd_mix(i, p_ref, ph_ref, bands_ref, wbd_ref, cscale, dww_ref, dwb, ln_g, ln_b, pww_ref, pwb, gbuf, gsh,
             cv=None):
    t = p_ref.shape[0]
    zc_bf = p_ref[:, 0:BW]
    zc = zc_bf.astype(F32)
    ga = p_ref[:, BW:2 * BW].astype(F32)
    gb = p_ref[:, 2 * BW:3 * BW].astype(F32)
    zh = ph_ref[:, 0:BW]
    zcat = jnp.concatenate([jnp.where(i > 0, zh, jnp.zeros_like(zh)), zc_bf], axis=0)
    inv = _inv_counts(i, t)
    pooled = _pool_select([_dot(bands_ref[w], zcat) * inv[w] for w in range(len(POOL_WINDOWS))]) - zc
    pooled_bf = pooled.astype(BF16)
    pre = _dot(pooled_bf, wbd_ref[...])
    yc = pre * cscale
    sgb = _sigmoid(gb)
    z = ga * sgb
    gh_a = ph_ref[:, BW:2 * BW].astype(F32)
    gh_b = ph_ref[:, 2 * BW:3 * BW].astype(F32)
    gbuf[0:HALO, :] = jnp.where(i > 0, gh_a * _sigmoid(gh_b), 0.0)
    gbuf[HALO:HALO + t, :] = z
    _shifted_copies(gbuf, gsh)
    if cv is None:
        cv = dwb + dww_ref[CONF - 1:CONF, :] * z
        for k in range(CONF - 1):
            cv = cv + dww_ref[k:k + 1, :] * _rows_at(gbuf, gsh, HALO - (CONF - 1) + k, t)
    zl, zhat, rstd = _layer_norm_fwd(cv, ln_g, ln_b)
    szl = _sigmoid(zl)
    zs = (zl * szl).astype(BF16)
    yd = _dot(zs, pww_ref[...]) + pwb
    return dict(ga=ga, sgb=sgb, pooled_bf=pooled_bf, pre=pre, yc=yc, zhat=zhat, rstd=rstd, zl=zl, szl=szl,
                zs=zs, yd=yd, inv=inv, cv=cv)


def _post_norm(o, post_g):
    r = lax.rsqrt(jnp.mean(o * o, axis=-1, keepdims=True) + EPS)
    return o * r, r


def _gate_out(y_a, y_b, y_x, gate, wout_ref):
    sgt = _sigmoid(gate)
    sgate = gate * sgt
    ys = [(y_a * sgate[:, 0:BW]).astype(BF16), (y_b * sgate[:, BW:2 * BW]).astype(BF16),
          (y_x * sgate[:, 2 * BW:MIX]).astype(BF16)]
    o = (_dot(ys[0], wout_ref[0:BW, :]) + _dot(ys[1], wout_ref[BW:2 * BW, :]) + _dot(ys[2], wout_ref[2 * BW:MIX, :]))
    return o, ys, sgt, sgate


def _tile_specs(s, n):
    nh = TS // HALO
    return pl.BlockSpec((TS, n), lambda i: (i, 0)), pl.BlockSpec((HALO, n), _halo_prev(nh))


def _even_fwd(x, p, kv, ln_g, ln_b, wcat, bsg, bconv, wout, post_g, rider=None):
    s = x.shape[0]

    def body(x_ref, p_ref, ph_ref, kv_ref, lng, lnb, wcat_ref, bsg_ref, bconv_ref, wout_ref, pg, x1_ref, o_ref, wbuf):
        i = pl.program_id(0)
        mx = _even_mix(i, p_ref, ph_ref, lng[...], lnb[...], wcat_ref, bsg_ref, bconv_ref, wbuf)
        yx, _ = _xattn_fwd(p_ref[:, 5 * BW:5 * BW + XA], kv_ref)
        gate = p_ref[:, 5 * BW + XA:EVEN_IN].astype(F32)
        o, _, _, _ = _gate_out(mx["ya"], mx["yb"], yx, gate, wout_ref)
        n, _ = _post_norm(o, pg[...])
        o_ref[...] = o
        x1_ref[...] = x_ref[...] + n * pg[...]

    tile, halo = _tile_specs(s, EVEN_IN)
    row = pl.BlockSpec((TS, D), lambda i: (i, 0))
    return _host_call(
        body, grid=(s // TS,), name="even_fwd", rider=rider,
        out_shape=(jax.ShapeDtypeStruct((s, D), F32), jax.ShapeDtypeStruct((s, D), F32)),
        in_specs=[row, tile, halo, _const((N_MEM, D)), _const((1, BW)), _const((1, BW)), _const((CH, 4 * CH)),
                  _const((CH, BW)), _const((3, BW)), _resident((MIX, D)), _const((1, D))],
        out_specs=(row, row),
        scratch_shapes=[pltpu.VMEM((HALO + TS, BW), F32)],
        args=(x, p, p, kv, ln_g, ln_b, wcat, bsg, bconv, wout, post_g))


def _odd_fwd(x1, p, kv, wbd, cscale, dww, dwb, ln_g, ln_b, pww, pwb, wout, post_g, target):
    s = x1.shape[0]

    def body(x_ref, p_ref, ph_ref, kv_ref, bands_ref, wbd_ref, cs, dww_ref, dwb_ref, lng, lnb, pww_ref, pwb_ref,
             wout_ref, pg, tgt_ref, dx_ref, o_ref, cv_ref, loss_ref, gbuf, gsh):
        i = pl.program_id(0)
        mx = _odd_mix(i, p_ref, ph_ref, bands_ref, wbd_ref, cs[...], dww_ref, dwb_ref[...], lng[...], lnb[...],
                      pww_ref, pwb_ref[...], gbuf, gsh)
        cv_ref[...] = mx["cv"]
        yx, _ = _xattn_fwd(p_ref[:, 3 * BW:3 * BW + XA], kv_ref)
        gate = p_ref[:, 3 * BW + XA:ODD_IN].astype(F32)
        o, _, _, _ = _gate_out(mx["yc"], mx["yd"], yx, gate, wout_ref)
        n, _ = _post_norm(o, pg[...])
        o_ref[...] = o
        err = x_ref[...] + n * pg[...] - tgt_ref[...]
        dx_ref[...] = err * (1.0 / D)

        @pl.when(i == 0)
        def _():
            loss_ref[...] = jnp.zeros_like(loss_ref)

        loss_ref[...] += 0.5 * jnp.sum(jnp.sum(err * err, axis=-1, keepdims=True) * (1.0 / D), axis=0, keepdims=True)

    tile, halo = _tile_specs(s, ODD_IN)
    row = pl.BlockSpec((TS, D), lambda i: (i, 0))
    vec = _const((1, BW))
    return pl.pallas_call(
        body, grid=(s // TS,), name="odd_fwd",
        out_shape=(jax.ShapeDtypeStruct((s, D), F32), jax.ShapeDtypeStruct((s, D), F32),
                   jax.ShapeDtypeStruct((s, BW), F32), jax.ShapeDtypeStruct((8, 128), F32)),
        in_specs=[row, tile, halo, _const((N_MEM, D)), _const((4, TS, HALO + TS)), _const((BW, BW)), vec,
                  _const((CONF, BW)), vec, vec, vec, _const((BW, BW)), vec, _resident((MIX, D)), _const((1, D)), row],
        out_specs=(row, row, pl.BlockSpec((TS, BW), lambda i: (i, 0)), _const((8, 128))),
        scratch_shapes=[pltpu.VMEM((HALO + TS, BW), F32), pltpu.VMEM((7, SHIFT_ROWS, BW), F32)],
        compiler_params=_cp(("arbitrary",)),
    )(x1, p, p, kv, _band_matrices(TS, False), wbd, cscale, dww, dwb, ln_g, ln_b, pww, pwb, wout, post_g, target)


def _acc_init(i, refs):
    @pl.when(i == 0)
    def _():
        for r in refs:
            r[...] = jnp.zeros_like(r)


def _post_norm_bwd(dx, o, pg, dpg_ref):
    n, r = _post_norm(o, pg)
    dpg_ref[...] += jnp.sum(dx * n, axis=0, keepdims=True)
    dn = dx * pg
    return (r * (dn - n * jnp.mean(dn * n, axis=-1, keepdims=True))).astype(BF16)


def _gate_bwd(do, wout_ref, ys_f32, gate, y_ref):
    dy = _dot_nt(do, wout_ref[...])
    sgt = _sigmoid(gate)
    sgate = gate * sgt
    dsilu = sgt * (1.0 + gate * (1.0 - sgt))
    offs = (0, BW, 2 * BW, MIX)
    dys, dgs = [], []
    for j, yv in enumerate(ys_f32):
        a, b = offs[j], offs[j + 1]
        y_ref[:, a:b] = (yv * sgate[:, a:b]).astype(BF16)
        dys.append(dy[:, a:b] * sgate[:, a:b])
        dgs.append(dy[:, a:b] * yv * dsilu[:, a:b])
    return dys, jnp.concatenate(dgs, axis=-1)


NEXT = 16


def _even_bwd1(dx, o, p, kv, ln_g, ln_b, wcat, bsg, hsel, bconv, wout, post_g, rider=None):
    s = dx.shape[0]
    nt = s // TS

    def body(dx_ref, o_ref, p_ref, ph_ref, dxn_ref, on_ref, pn_ref, kv_ref, lng, lnb, wcat_ref, bsg_ref, hsel_ref,
             bconv_ref, wout_ref, pg,
             dp_ref, do_ref, y_ref, dpg_ref, dlng_ref, dlnb_ref, dwcat_ref, dbs_ref, dbconv_ref, dkv_ref, wbuf, dbuf):
        i = pl.program_id(0)
        _acc_init(i, (dpg_ref, dlng_ref, dlnb_ref, dwcat_ref, dbs_ref, dbconv_ref, dkv_ref))
        mx = _even_mix(i, p_ref, ph_ref, lng[...], lnb[...], wcat_ref, bsg_ref, bconv_ref, wbuf)
        q = p_ref[:, 5 * BW:5 * BW + XA]
        yx, probs = _xattn_fwd(q, kv_ref)
        gate = p_ref[:, 5 * BW + XA:EVEN_IN].astype(F32)
        do = _post_norm_bwd(dx_ref[...], o_ref[...], pg[...], dpg_ref)
        do_ref[...] = do
        (dya, dyb, dyx), dgate = _gate_bwd(do, wout_ref, (mx["ya"], mx["yb"], yx), gate, y_ref)
        dp_ref[:, 0:BW] = (dya * mx["sg"]).astype(BF16)
        dsg = (dya * mx["u"]).astype(BF16)
        dvns = []
        for n in range(TS // CH):
            dsg_c = dsg[n * CH:(n + 1) * CH]
            dvst = _dot_tn(wcat_ref[...], dsg_c)
            dvn_c = jnp.where(mx["masks"][0], dvst[0:CH], 0.0)
            for h in range(1, 4):
                dvn_c = dvn_c + jnp.where(mx["masks"][h], dvst[h * CH:(h + 1) * CH], 0.0)
            dvns.append(dvn_c)
            dwcat_ref[...] += _dot_nt(dsg_c, mx["vsts"][n])
            dbs_ref[...] += _dot(dsg_c, hsel_ref[...])
        dvn = jnp.concatenate(dvns, axis=0)
        dlng_ref[...] += jnp.sum(dvn * mx["vhat"], axis=0, keepdims=True)
        dlnb_ref[...] += jnp.sum(dvn, axis=0, keepdims=True)
        dp_ref[:, BW:2 * BW] = _layer_norm_bwd(dvn, mx["vhat"], mx["rstd"], lng[...]).astype(BF16)
        dp_ref[:, 2 * BW:3 * BW] = (dyb * mx["conv"]).astype(BF16)
        dconv = dyb * mx["bg"]
        for k in range(3):
            dbconv_ref[k:k + 1, :] += jnp.sum(dconv * wbuf[pl.ds(HALO - 2 + k, TS), :], axis=0, keepdims=True)
        n_n, r_n = _post_norm(on_ref[...], pg[...])
        dn_n = dxn_ref[...] * pg[...]
        do_n = (r_n * (dn_n - n_n * jnp.mean(dn_n * n_n, axis=-1, keepdims=True))).astype(BF16)
        dy_n = _dot_nt(do_n, wout_ref[BW:2 * BW, :])
        g_n = pn_ref[:, 5 * BW + XA + BW:5 * BW + XA + 2 * BW].astype(F32)
        dconv_n = dy_n * (g_n * _sigmoid(g_n)) * pn_ref[:, 2 * BW:3 * BW].astype(F32)
        dbuf[0:TS, :] = dconv
        dbuf[TS:TS + NEXT, :] = jnp.where(i < nt - 1, dconv_n, 0.0)
        dw = (bconv_ref[2:3, :] * dconv + bconv_ref[1:2, :] * dbuf[pl.ds(1, TS), :]
              + bconv_ref[0:1, :] * dbuf[pl.ds(2, TS), :])
        dp_ref[:, 3 * BW:4 * BW] = (dw * mx["xin"]).astype(BF16)
        dp_ref[:, 4 * BW:5 * BW] = (dw * mx["cg"]).astype(BF16)
        dp_ref[:, 5 * BW:5 * BW + XA] = _xattn_bwd(dyx, q, probs, kv_ref, dkv_ref).astype(BF16)
        dp_ref[:, 5 * BW + XA:EVEN_IN] = dgate.astype(BF16)

    tile, halo = _tile_specs(s, EVEN_IN)
    row = pl.BlockSpec((TS, D), lambda i: (i, 0))
    vec = _const((1, BW))
    nxt = _halo_next(TS // NEXT, s // NEXT)

    def out(n):
        return pl.BlockSpec((TS, n), lambda i: (i, 0))

    return _host_call(
        body, grid=(nt,), name="even_bwd1", rider=rider,
        out_shape=(jax.ShapeDtypeStruct((s, EVEN_IN), BF16), jax.ShapeDtypeStruct((s, D), BF16),
                   jax.ShapeDtypeStruct((s, MIX), BF16),
                   jax.ShapeDtypeStruct((1, D), F32), jax.ShapeDtypeStruct((1, BW), F32),
                   jax.ShapeDtypeStruct((1, BW), F32), jax.ShapeDtypeStruct((CH, 4 * CH), F32),
                   jax.ShapeDtypeStruct((CH, 128), F32), jax.ShapeDtypeStruct((8, BW), F32),
                   jax.ShapeDtypeStruct((N_MEM, D), F32)),
        in_specs=[row, row, tile, halo, pl.BlockSpec((NEXT, D), nxt), pl.BlockSpec((NEXT, D), nxt),
                  pl.BlockSpec((NEXT, EVEN_IN), nxt), _const((N_MEM, D)), vec, vec, _const((CH, 4 * CH)),
                  _const((CH, BW)), _const((BW, 128)), _const((3, BW)), _resident((MIX, D)), _const((1, D))],
        out_specs=(out(EVEN_IN), out(D), out(MIX),
                   _const((1, D)), vec, vec, _const((CH, 4 * CH)), _const((CH, 128)), _const((8, BW)),
                   _const((N_MEM, D))),
        scratch_shapes=[pltpu.VMEM((HALO + TS, BW), F32), pltpu.VMEM((TS + NEXT, BW), F32)],
        args=(dx, o, p, p, dx, o, p, kv, ln_g, ln_b, wcat, bsg, hsel, bconv, wout, post_g))


def _odd_bwd1(dx, o, cv, p, kv, wbd, cscale, dww, dwb, ln_g, ln_b, pww, pwb, wout, post_g):
    s = dx.shape[0]

    def body(dx_ref, o_ref, cv_ref, p_ref, ph_ref, kv_ref, bands_ref, wbd_ref, cs, dww_ref, dwb_ref, lng, lnb,
             pww_ref, pwb_ref, wout_ref, pg,
             dpc_ref, tmpc_ref, tmpd_ref, do_ref, y_ref, dpg_ref, dcs_ref, dwbd_ref, ddww_ref, ddwb_ref, dlng_ref,
             dlnb_ref, dpww_ref, dpwb_ref, dkv_ref, gbuf, gsh, dcv_buf):
        i = pl.program_id(0)
        _acc_init(i, (dpg_ref, dcs_ref, dwbd_ref, ddww_ref, ddwb_ref, dlng_ref, dlnb_ref, dpww_ref, dpwb_ref,
                      dkv_ref))
        mx = _odd_mix(i, p_ref, ph_ref, bands_ref, wbd_ref, cs[...], dww_ref, dwb_ref[...], lng[...], lnb[...],
                      pww_ref, pwb_ref[...], gbuf, gsh, cv=cv_ref[...])
        q = p_ref[:, 3 * BW:3 * BW + XA]
        yx, probs = _xattn_fwd(q, kv_ref)
        gate = p_ref[:, 3 * BW + XA:ODD_IN].astype(F32)
        do = _post_norm_bwd(dx_ref[...], o_ref[...], pg[...], dpg_ref)
        do_ref[...] = do
        (dyc, dyd, dyx), dgate = _gate_bwd(do, wout_ref, (mx["yc"], mx["yd"], yx), gate, y_ref)
        dcs_ref[...] += jnp.sum(dyc * mx["pre"], axis=0, keepdims=True)
        dpre = (dyc * cs[...]).astype(BF16)
        dwbd_ref[...] += _dot_tn(mx["pooled_bf"], dpre)
        dpooled = _dot_nt(dpre, wbd_ref[...])
        tmpc_ref[...] = _pool_select([dpooled * c_ for c_ in mx["inv"]]).astype(BF16)
        dyd_bf = dyd.astype(BF16)
        dpwb_ref[...] += jnp.sum(dyd, axis=0, keepdims=True)
        dpww_ref[...] += _dot_tn(mx["zs"], dyd_bf)
        dzs = _dot_nt(dyd_bf, pww_ref[...])
        zl, szl = mx["zl"], mx["szl"]
        dzl = dzs * (szl * (1.0 + zl * (1.0 - szl)))
        dlng_ref[...] += jnp.sum(dzl * mx["zhat"], axis=0, keepdims=True)
        dlnb_ref[...] += jnp.sum(dzl, axis=0, keepdims=True)
        dcv = _layer_norm_bwd(dzl, mx["zhat"], mx["rstd"], lng[...])
        tmpd_ref[...] = dcv.astype(BF16)
        ddwb_ref[...] += jnp.sum(dcv, axis=0, keepdims=True)
        dcv_buf[...] = dcv
        _tap_sums(dcv_buf, gbuf, gsh, HALO - (CONF - 1), ddww_ref)
        dpc_ref[:, 0:XA] = _xattn_bwd(dyx, q, probs, kv_ref, dkv_ref).astype(BF16)
        dpc_ref[:, XA:XA + MIX] = dgate.astype(BF16)

    tile, halo = _tile_specs(s, ODD_IN)
    row = pl.BlockSpec((TS, D), lambda i: (i, 0))
    vec = _const((1, BW))

    def out(n):
        return pl.BlockSpec((TS, n), lambda i: (i, 0))

    return pl.pallas_call(
        body, grid=(s // TS,), name="odd_bwd1",
        out_shape=(jax.ShapeDtypeStruct((s, XA + MIX), BF16), jax.ShapeDtypeStruct((s, BW), BF16),
                   jax.ShapeDtypeStruct((s, BW), BF16), jax.ShapeDtypeStruct((s, D), BF16),
                   jax.ShapeDtypeStruct((s, MIX), BF16),
                   jax.ShapeDtypeStruct((1, D), F32), jax.ShapeDtypeStruct((1, BW), F32),
                   jax.ShapeDtypeStruct((BW, BW), F32), jax.ShapeDtypeStruct((8 * CONF, BW), F32),
                   jax.ShapeDtypeStruct((1, BW), F32), jax.ShapeDtypeStruct((1, BW), F32),
                   jax.ShapeDtypeStruct((1, BW), F32), jax.ShapeDtypeStruct((BW, BW), F32),
                   jax.ShapeDtypeStruct((1, BW), F32), jax.ShapeDtypeStruct((N_MEM, D), F32)),
        in_specs=[row, row, out(BW), tile, halo, _const((N_MEM, D)), _const((4, TS, HALO + TS)), _const((BW, BW)), vec,
                  _const((CONF, BW)), vec, vec, vec, _const((BW, BW)), vec, _resident((MIX, D)), _const((1, D))],
        out_specs=(out(XA + MIX), out(BW), out(BW), out(D), out(MIX),
                   _const((1, D)), vec, _const((BW, BW)), _const((8 * CONF, BW)), vec, vec, vec, _const((BW, BW)), vec,
                   _const((N_MEM, D))),
        scratch_shapes=[pltpu.VMEM((HALO + TS, BW), F32), pltpu.VMEM((7, SHIFT_ROWS, BW), F32),
                        pltpu.VMEM((TS, BW), F32)],
        compiler_params=_cp(("arbitrary",)),
    )(dx, o, cv, p, p, kv, _band_matrices(TS, False), wbd, cscale, dww, dwb, ln_g, ln_b, pww, pwb, wout, post_g)


def _halo_next(nblk_per_tile, nblk):
    return lambda i: (jnp.minimum((i + 1) * nblk_per_tile, nblk - 1), 0)


def _pre_norm_bwd(dh, x, pre_g, dres, dpre_ref):
    r = lax.rsqrt(jnp.mean(x * x, axis=-1, keepdims=True) + EPS)
    xh = x * r
    dpre_ref[...] += jnp.sum(dh * xh, axis=0, keepdims=True)
    dxh = dh * pre_g
    return dres + r * (dxh - xh * jnp.mean(dxh * xh, axis=-1, keepdims=True))


def _even_bwd2(dp, w_t, x, pre_g, dres, rider=None):
    s = x.shape[0]
    tm = min(512, s)

    def body(dp_ref, w_ref, x_ref, pg, dres_ref, dx_ref, dpre_ref):
        _acc_init(pl.program_id(0), (dpre_ref,))
        dh = _dot(dp_ref[...], w_ref[...])
        dx_ref[...] = _pre_norm_bwd(dh, x_ref[...], pg[...], dres_ref[...], dpre_ref)

    row = pl.BlockSpec((tm, D), lambda i: (i, 0))
    return _host_call(
        body, grid=(s // tm,), name="even_bwd2", rider=rider,
        out_shape=(jax.ShapeDtypeStruct((s, D), F32), jax.ShapeDtypeStruct((1, D), F32)),
        in_specs=[pl.BlockSpec((tm, EVEN_IN), lambda i: (i, 0)), _resident((EVEN_IN, D)), row, _const((1, D)), row],
        out_specs=(row, _const((1, D))),
        args=(dp, w_t, x, pre_g, dres))


def _odd_bwd2(dpc, tmpc, tmpd, p, dww, w_t, x, pre_g, dres):
    s = x.shape[0]
    nt = s // TS

    def body(dpc_ref, tc_ref, tch_ref, td_ref, tdh_ref, ga_ref, gb_ref, bands_ref, dww_ref, w_ref, x_ref, pg,
             dres_ref, dpb_ref, dx_ref, dpre_ref, dbuf, dsh):
        i = pl.program_id(0)
        _acc_init(i, (dpre_ref,))
        more = i < nt - 1
        e_bf = tc_ref[...]
        eh = tch_ref[...]
        ecat = jnp.concatenate([e_bf, jnp.where(more, eh, jnp.zeros_like(eh))], axis=0)
        dbuf[0:TS, :] = td_ref[...].astype(F32)
        dbuf[TS:TS + HALO, :] = jnp.where(more, tdh_ref[...].astype(F32), 0.0)
        sums = [_dot(bands_ref[w], ecat) for w in range(len(POOL_WINDOWS))]
        rows = _row_ids(i, TS) + 1
        cnt = _pool_select([jnp.minimum(rows, w).astype(F32) for w in POOL_WINDOWS])
        dzc = (_pool_select(sums) - e_bf.astype(F32) * cnt).astype(BF16)
        _shifted_copies(dbuf, dsh)
        dz = dww_ref[CONF - 1:CONF, :] * dbuf[pl.ds(0, TS), :]
        for sft in range(1, CONF):
            dz = dz + dww_ref[CONF - 1 - sft:CONF - sft, :] * _rows_at(dbuf, dsh, sft, TS)
        ga = ga_ref[...].astype(F32)
        sgb = _sigmoid(gb_ref[...].astype(F32))
        dga = (dz * sgb).astype(BF16)
        dgb = (dz * ga * sgb * (1.0 - sgb)).astype(BF16)
        dpb_ref[:, 0:BW] = dzc
        dpb_ref[:, BW:2 * BW] = dga
        dpb_ref[:, 2 * BW:3 * BW] = dgb
        dh = (_dot(dzc, w_ref[0:BW, :]) + _dot(dga, w_ref[BW:2 * BW, :]) + _dot(dgb, w_ref[2 * BW:3 * BW, :])
              + _dot(dpc_ref[...], w_ref[3 * BW:ODD_IN, :]))
        dx_ref[...] = _pre_norm_bwd(dh, x_ref[...], pg[...], dres_ref[...], dpre_ref)

    row = pl.BlockSpec((TS, D), lambda i: (i, 0))

    def tile(n, j=0):
        return pl.BlockSpec((TS, n), lambda i: (i, j))

    nxt = pl.BlockSpec((HALO, BW), _halo_next(TS // HALO, s // HALO))
    return pl.pallas_call(
        body, grid=(nt,), name="odd_bwd2",
        out_shape=(jax.ShapeDtypeStruct((s, 3 * BW), BF16), jax.ShapeDtypeStruct((s, D), F32),
                   jax.ShapeDtypeStruct((1, D), F32)),
        in_specs=[tile(XA + MIX), tile(BW), nxt, tile(BW), nxt, tile(BW, 1), tile(BW, 2), _const((4, TS, HALO + TS)),
                  _const((CONF, BW)), _resident((ODD_IN, D)), row, _const((1, D)), row],
        out_specs=(tile(3 * BW), row, _const((1, D))),
        scratch_shapes=[pltpu.VMEM((TS + HALO, BW), F32), pltpu.VMEM((7, SHIFT_ROWS, BW), F32)],
        compiler_params=_cp(("arbitrary",)),
    )(dpc, tmpc, tmpc, tmpd, tmpd, p, p, _band_matrices(TS, True), dww, w_t, x, pre_g, dres)


def _grad_tn(a, b, tm, out=None, rows=None, row0=0, name="grad_tn", rider=None):
    s, m = a.shape
    n = b.shape[1]
    ts = min(2048, s)
    rows = m if rows is None else rows
    assert m % tm == 0 and s % ts == 0
    ns = s // ts
    if row0 % tm == 0:
        out_spec = pl.BlockSpec((tm, n), lambda i, k: (row0 // tm + i, 0))
    else:
        align = 16
        assert row0 % align == 0 and tm % align == 0
        out_spec = pl.BlockSpec((pl.Element(tm), pl.Element(n)),
                                lambda i, k: (pl.multiple_of(row0 + i * tm, align), 0))

    def body(*refs):
        a_ref, b_ref = refs[0], refs[1]
        o_ref, acc = refs[-2], refs[-1]
        k = pl.program_id(1)

        @pl.when(k == 0)
        def _():
            acc[...] = jnp.zeros_like(acc)

        acc[...] += _dot_tn(a_ref[...], b_ref[...])

        @pl.when(k == ns - 1)
        def _():
            o_ref[...] = acc[...].astype(BF16)

    in_specs = [pl.BlockSpec((ts, tm), lambda i, k: (k, i)), pl.BlockSpec((ts, n), lambda i, k: (k, 0))]
    args = [a, b]
    aliases = {}
    if out is not None:
        in_specs.append(pl.BlockSpec(memory_space=pltpu.HBM))
        args.append(out)
        aliases = {2: 0}
    (res,), got = _host_call(
        body, grid=(m // tm, ns), name=name, rider=rider, aliases=aliases,
        out_shape=(jax.ShapeDtypeStruct((rows, n), BF16),), in_specs=in_specs, out_specs=(out_spec,),
        scratch_shapes=[pltpu.VMEM((tm, n), F32)], args=args)
    return res if rider is None else (res, got)


def _place():
    x, y, c = lax.axis_index("x"), lax.axis_index("y"), lax.axis_index("c")
    chips = [(1 - x, y), (x, 1 - y), (1 - x, 1 - y)]
    return x, y, c, chips


def _hbm_specs(n):
    return [pl.BlockSpec(memory_space=pltpu.HBM)] * n


def _row_tile(r):
    for cand in (512, 400, 304, 256, 192, 128, 96, 16):
        if r % cand == 0:
            return cand
    raise ValueError(r)


def _place_shard(shard, place, dtype, name):
    r, cc = shard.shape
    tr = _row_tile(r)
    nt = r // tr

    def body(place_ref, s_ref, o_ref):
        o_ref[...] = s_ref[...].astype(dtype)

    return pl.pallas_call(
        body, name=name, out_shape=jax.ShapeDtypeStruct((N_CHIPS * r, cc), dtype),
        grid_spec=pltpu.PrefetchScalarGridSpec(
            num_scalar_prefetch=1, grid=(nt,),
            in_specs=[pl.BlockSpec((tr, cc), lambda i, pr: (i, 0))],
            out_specs=pl.BlockSpec((tr, cc), lambda i, pr: (pr[1] * nt + i, 0))),
        compiler_params=_cp(("arbitrary",)),
    )(place, shard)


class _GatherRider:
    has_mid = True

    def __init__(self, fulls):
        n = len(fulls)
        self.inputs = list(fulls)
        self.out_shapes = [jax.ShapeDtypeStruct(a.shape, a.dtype) for a in fulls]
        self.aliases = {a: a for a in range(n)}
        self.sems = [pltpu.SemaphoreType.DMA((6 * n,)), pltpu.SemaphoreType.DMA((6 * n,))]
        self.block_rows = [a.shape[0] // N_CHIPS for a in fulls]

    def _ctx(self, outs, sems):
        send_sems, recv_sems = sems
        x, y, c, chips = _place()

        def rows(a, k, half):
            r = self.block_rows[a]
            return outs[a].at[pl.ds(k * r + half * (r // 2), r // 2)]

        def copy(a, j, blk, to):
            return pltpu.make_async_remote_copy(src_ref=blk, dst_ref=blk, send_sem=send_sems.at[a * 6 + j],
                                                recv_sem=recv_sems.at[a * 6 + j], device_id=to, device_id_type=MESH)

        return x, y, c, chips, rows, copy

    def start(self, ins, outs, sems):
        x, y, c, chips, rows, copy = self._ctx(outs, sems)
        for j, (px, py) in enumerate(chips):
            for a in range(len(outs)):
                copy(a, j, rows(a, 2 * x + y, c), (px, py, c)).start()

    def mid(self, ins, outs, sems):
        x, y, c, chips, rows, copy = self._ctx(outs, sems)
        for j, (px, py) in enumerate(chips):
            for a in range(len(outs)):
                copy(a, j, rows(a, 2 * px + py, c), (px, py, c)).wait_recv()
                copy(a, 3 + j, rows(a, 2 * px + py, c), (x, y, 1 - c)).start()

    def end(self, ins, outs, sems):
        x, y, c, chips, rows, copy = self._ctx(outs, sems)
        for j, (px, py) in enumerate(chips):
            for a in range(len(outs)):
                copy(a, 3 + j, rows(a, 2 * px + py, 1 - c), (x, y, 1 - c)).wait_recv()
        for j, (px, py) in enumerate(chips):
            for a in range(len(outs)):
                copy(a, j, rows(a, 2 * x + y, c), (px, py, c)).wait_send()
                copy(a, 3 + j, rows(a, 2 * px + py, c), (x, y, 1 - c)).wait_send()


def _run_rider(rider, name):
    r_in, r_out = len(rider.inputs), len(rider.out_shapes)

    def body(*refs):
        ins, outs, sems = refs[:r_in], refs[r_in:r_in + r_out], refs[r_in + r_out:]
        rider.start(ins, outs, sems)
        if rider.has_mid:
            rider.mid(ins, outs, sems)
        rider.end(ins, outs, sems)

    return pl.pallas_call(
        body, name=name, out_shape=tuple(rider.out_shapes), in_specs=_hbm_specs(r_in),
        out_specs=tuple(_hbm_specs(r_out)), input_output_aliases=dict(rider.aliases),
        scratch_shapes=list(rider.sems),
    )(*rider.inputs)


def _swap_halves(grads, small, name):
    n = len(grads)
    arrs = list(grads) + ([small] if small is not None else [])
    m = len(arrs)

    def body(*refs):
        ins, outs = refs[:m], refs[m:2 * m]
        send_sems, recv_sems = refs[2 * m:]
        x, y, c, _ = _place()
        sibling = (x, y, 1 - c)
        cps = []
        for a in range(m):
            src = ins[a].at[:, 1 - c] if a < n else ins[a]
            cp = pltpu.make_async_remote_copy(src_ref=src, dst_ref=outs[a], send_sem=send_sems.at[a],
                                              recv_sem=recv_sems.at[a], device_id=sibling, device_id_type=MESH)
            cp.start()
            cps.append(cp)
        for cp in cps:
            cp.wait_recv()
        for cp in cps:
            cp.wait_send()

    outs = tuple(jax.ShapeDtypeStruct((g.shape[0],) + g.shape[2:], g.dtype) for g in grads)
    if small is not None:
        outs += (jax.ShapeDtypeStruct(small.shape, small.dtype),)
    return pl.pallas_call(
        body, name=name, out_shape=outs, in_specs=_hbm_specs(m), out_specs=tuple(_hbm_specs(m)),
        scratch_shapes=[pltpu.SemaphoreType.DMA((m,)), pltpu.SemaphoreType.DMA((m,))],
    )(*arrs)


def _pair_sum(g, recv, place, name):
    _, _, h, cc = g.shape
    th = _row_tile(h)

    def body(c_ref, g_ref, r_ref, o_ref):
        o_ref[...] = (g_ref[...].astype(F32) + r_ref[...].astype(F32)).astype(o_ref.dtype)

    return pl.pallas_call(
        body, name=name, out_shape=jax.ShapeDtypeStruct(recv.shape, recv.dtype),
        grid_spec=pltpu.PrefetchScalarGridSpec(
            num_scalar_prefetch=1, grid=(N_CHIPS, h // th),
            in_specs=[pl.BlockSpec((None, None, th, cc), lambda k, r, c_ref: (k, c_ref[0], r, 0)),
                      pl.BlockSpec((None, th, cc), lambda k, r, c_ref: (k, r, 0))],
            out_specs=pl.BlockSpec((None, th, cc), lambda k, r, c_ref: (k, r, 0))),
        compiler_params=_cp(("arbitrary", "arbitrary")),
    )(place, g, recv)


def _finish_reduce(pack, halves):
    rows, cc = pack.shape
    hs = rows // 2
    n = len(halves)

    def body(*refs):
        pack_ref = refs[0]
        out_ref = refs[1 + n]
        big = refs[2 + n:2 + 2 * n]
        sib_ref, parts_ref, send_sems, recv_sems, big_send, big_recv = refs[2 + 2 * n:]
        x, y, c, chips = _place()
        me_k = 2 * x + y
        sibling = (x, y, 1 - c)
        mine = pl.ds(pl.multiple_of(c * hs, hs), hs)
        theirs = pl.ds(pl.multiple_of((1 - c) * hs, hs), hs)
        shared = [pltpu.make_async_remote_copy(src_ref=big[a].at[c], dst_ref=big[a].at[c], send_sem=big_send.at[a],
                                               recv_sem=big_recv.at[a], device_id=sibling, device_id_type=MESH)
                  for a in range(n)]
        for cp in shared:
            cp.start()
        first = pltpu.make_async_remote_copy(src_ref=pack_ref, dst_ref=sib_ref, send_sem=send_sems.at[0],
                                             recv_sem=recv_sems.at[0], device_id=sibling, device_id_type=MESH)
        first.start()
        first.wait()
        parts_ref[me_k] = pack_ref[mine, :] + sib_ref[mine, :]
        cps = [pltpu.make_async_remote_copy(src_ref=parts_ref.at[me_k], dst_ref=parts_ref.at[me_k],
                                            send_sem=send_sems.at[1 + j], recv_sem=recv_sems.at[1 + j],
                                            device_id=(px, py, c), device_id_type=MESH)
               for j, (px, py) in enumerate(chips)]
        for cp in cps:
            cp.start()
        for j, (px, py) in enumerate(chips):
            pltpu.make_async_remote_copy(src_ref=parts_ref.at[2 * px + py], dst_ref=parts_ref.at[2 * px + py],
                                         send_sem=send_sems.at[1 + j], recv_sem=recv_sems.at[1 + j],
                                         device_id=(px, py, c), device_id_type=MESH).wait_recv()
        for cp in cps:
            cp.wait_send()
        out_ref[mine, :] = ((parts_ref[0] + parts_ref[1]) + parts_ref[2]) + parts_ref[3]
        last = pltpu.make_async_remote_copy(src_ref=out_ref.at[mine], dst_ref=out_ref.at[mine],
                                            send_sem=send_sems.at[4], recv_sem=recv_sems.at[4], device_id=sibling,
                                            device_id_type=MESH)
        last.start()
        pltpu.make_async_remote_copy(src_ref=out_ref.at[theirs], dst_ref=out_ref.at[theirs],
                                     send_sem=send_sems.at[4], recv_sem=recv_sems.at[4], device_id=sibling,
                                     device_id_type=MESH).wait_recv()
        last.wait_send()
        for a in range(n):
            pltpu.make_async_remote_copy(src_ref=big[a].at[1 - c], dst_ref=big[a].at[1 - c], send_sem=big_send.at[a],
                                         recv_sem=big_recv.at[a], device_id=sibling,
                                         device_id_type=MESH).wait_recv()
        for cp in shared:
            cp.wait_send()

    vmem = pl.BlockSpec(memory_space=pltpu.VMEM)
    res = pl.pallas_call(
        body, name="finish_reduce",
        out_shape=(jax.ShapeDtypeStruct(pack.shape, pack.dtype),)
        + tuple(jax.ShapeDtypeStruct(g.shape, g.dtype) for g in halves),
        in_specs=[vmem] + _hbm_specs(n), out_specs=(vmem,) + tuple(_hbm_specs(n)),
        input_output_aliases={1 + a: 1 + a for a in range(n)},
        scratch_shapes=[pltpu.VMEM((rows, cc), F32), pltpu.VMEM((N_CHIPS, hs, cc), F32),
                        pltpu.SemaphoreType.DMA((5,)), pltpu.SemaphoreType.DMA((5,)),
                        pltpu.SemaphoreType.DMA((n,)), pltpu.SemaphoreType.DMA((n,))],
        compiler_params=_cp(),
    )(pack, *halves)
    return res[0], tuple(res[1:])


class _ExchangeRider:
    has_mid = False

    def __init__(self, sums):
        self.inputs = list(sums)
        self.out_shapes = [jax.ShapeDtypeStruct((3,) + g.shape[1:], g.dtype) for g in sums]
        m = len(self.inputs)
        self.aliases = {}
        self.sems = [pltpu.SemaphoreType.DMA((3 * m,)), pltpu.SemaphoreType.DMA((3 * m,))]

    def _copies(self, ins, outs, sems):
        send_sems, recv_sems = sems
        _, _, c, chips = _place()
        return [pltpu.make_async_remote_copy(
            src_ref=ins[a].at[2 * px + py], dst_ref=outs[a].at[j], send_sem=send_sems.at[a * 3 + j],
            recv_sem=recv_sems.at[a * 3 + j], device_id=(px, py, c), device_id_type=MESH)
            for j, (px, py) in enumerate(chips) for a in range(len(ins))]

    def start(self, ins, outs, sems):
        for cp in self._copies(ins, outs, sems):
            cp.start()

    def end(self, ins, outs, sems):
        cps = self._copies(ins, outs, sems)
        for cp in cps:
            cp.wait_recv()
        for cp in cps:
            cp.wait_send()


def _chip_sum(own, parts, place, name):
    npart, h, cc = parts.shape
    th = _row_tile(h)

    def body(place_ref, own_ref, p_ref, o_ref):
        acc = own_ref[...].astype(F32) + p_ref[0].astype(F32)
        for k in range(1, npart):
            acc = acc + p_ref[k].astype(F32)
        o_ref[...] = acc

    return pl.pallas_call(
        body, name=name, out_shape=jax.ShapeDtypeStruct((2, h, cc), F32),
        grid_spec=pltpu.PrefetchScalarGridSpec(
            num_scalar_prefetch=1, grid=(h // th,),
            in_specs=[pl.BlockSpec((None, th, cc), lambda r, pr: (pr[1], r, 0)),
                      pl.BlockSpec((npart, th, cc), lambda r, pr: (0, r, 0))],
            out_specs=pl.BlockSpec((None, th, cc), lambda r, pr: (pr[0], r, 0))),
        compiler_params=_cp(("arbitrary",)),
    )(place, own, parts)


def _adamw_math(w, g, m, v):
    m = ADAM_B1 * m + (1.0 - ADAM_B1) * g
    v = ADAM_B2 * v + (1.0 - ADAM_B2) * (g * g)
    m_hat = m / (1.0 - ADAM_B1 ** ADAM_STEP)
    v_hat = v / (1.0 - ADAM_B2 ** ADAM_STEP)
    delta = -ADAM_LR * (m_hat / (jnp.sqrt(v_hat) + ADAM_EPS) + ADAM_WD * w)
    return delta, m, v


def _adamw_big(w, g, m, v, name):
    r, cc = w.shape
    tr = min(_row_tile(r), 256) if r % 256 == 0 else _row_tile(r)

    def body(w_ref, g_ref, m_ref, v_ref, go_ref, d_ref, mo_ref, vo_ref):
        g = g_ref[...]
        d, mm, vv = _adamw_math(w_ref[...], g, m_ref[...], v_ref[...])
        go_ref[...] = g
        d_ref[...] = d
        mo_ref[...] = mm
        vo_ref[...] = vv

    blk = pl.BlockSpec((tr, cc), lambda i: (i, 0))
    sd = jax.ShapeDtypeStruct((r, cc), F32)
    return pl.pallas_call(body, grid=(r // tr,), name=name, out_shape=(sd, sd, sd, sd), in_specs=[blk] * 4,
                          out_specs=(blk, blk, blk, blk), compiler_params=_cp(("arbitrary",)))(w, g, m, v)


def _adamw_small(ws, gs, ms, vs):
    n = len(ws)

    def body(*refs):
        for a in range(n):
            w_ref, g_ref, m_ref, v_ref = refs[4 * a:4 * a + 4]
            d_ref, mo_ref, vo_ref = refs[4 * n + 3 * a:4 * n + 3 * a + 3]
            d, mm, vv = _adamw_math(w_ref[...], g_ref[...], m_ref[...], v_ref[...])
            d_ref[...] = d
            mo_ref[...] = mm
            vo_ref[...] = vv

    args, outs = [], []
    for a in range(n):
        args += [ws[a], gs[a], ms[a], vs[a]]
        outs += [jax.ShapeDtypeStruct(ws[a].shape, F32)] * 3
    res = pl.pallas_call(body, name="adamw_small", out_shape=tuple(outs), compiler_params=_cp())(*args)
    return [res[3 * a:3 * a + 3] for a in range(n)]


def _flat_pack(arrs, rows):
    flat = jnp.concatenate([a.reshape(-1) for a in arrs])
    return jnp.pad(flat, (0, rows * D - flat.shape[0])).reshape(rows, D)


def _flat_unpack(flat, shapes):
    out, off = [], 0
    for shp in shapes:
        size = 1
        for d_ in shp:
            size *= d_
        out.append(flat[off:off + size].reshape(shp))
        off += size
    return out


SMALL_EVEN = ("even_pre_g", "even_a_ln_g", "even_a_ln_b", "even_a_ws", "even_a_bs", "even_b_conv", "even_mem_g",
              "even_post_g")
SMALL_ODD = ("odd_pre_g", "odd_c_wgrp", "odd_c_scale", "odd_d_dw_w", "odd_d_dw_b", "odd_d_ln_g", "odd_d_ln_b",
             "odd_d_pw_b", "odd_mem_g", "odd_post_g")
BIG = ("even_w_in", "even_w_kv", "even_w_out", "odd_w_in", "odd_d_pw_w", "odd_w_kv", "odd_w_out")
WEIGHTS = ("even_pre_g", "even_w_in", "even_a_ln_g", "even_a_ln_b", "even_a_ws", "even_a_bs", "even_b_conv",
           "even_mem_g", "even_w_kv", "even_w_out", "even_post_g", "odd_pre_g", "odd_w_in", "odd_c_wgrp",
           "odd_c_scale", "odd_d_dw_w", "odd_d_dw_b", "odd_d_ln_g", "odd_d_ln_b", "odd_d_pw_w", "odd_d_pw_b",
           "odd_mem_g", "odd_w_kv", "odd_w_out", "odd_post_g")
PACKED = (("even_b_conv", (3, 192)), ("odd_pre_g", (1, 256)), ("odd_c_scale", (1, 192)), ("odd_d_dw_w", (31, 192)),
          ("odd_d_dw_b", (1, 192)), ("odd_d_ln_g", (1, 192)), ("odd_d_ln_b", (1, 192)), ("odd_d_pw_b", (1, 192)),
          ("odd_mem_g", (1, 256)), ("odd_post_g", (1, 256)))
PACK_ROWS = 16
SMALL_ROWS = 256


def _four(g):
    return g.reshape(N_CHIPS, 2, g.shape[0] // (2 * N_CHIPS), g.shape[1])


def _step(x, mem, target, w, place):
    wt = {}
    pack = _flat_pack([w[n][0] for n, _ in PACKED], PACK_ROWS)
    shards = {"even_w_in_t": w["even_w_in"][0].T, "odd_w_in_t": w["odd_w_in"][0].T, "even_w_kv": w["even_w_kv"][0],
              "odd_w_kv": w["odd_w_kv"][0], "even_w_out": w["even_w_out"][0], "odd_w_out": w["odd_w_out"][0],
              "odd_d_pw_w": w["odd_d_pw_w"][0]}
    placed = {n: _place_shard(a, place, BF16, "place_" + n) for n, a in shards.items()}
    placed["pack"] = _place_shard(pack, place, F32, "place_pack")

    wt["even_w_in_t"], packs = _run_rider(_GatherRider([placed["even_w_in_t"], placed["pack"]]), "gather_first")
    packs = packs.reshape(N_CHIPS, PACK_ROWS * D)
    per_chip = [_flat_unpack(packs[k], [shp for _, shp in PACKED]) for k in range(N_CHIPS)]
    for a, (name, _) in enumerate(PACKED):
        wt[name] = jnp.concatenate([per_chip[k][a] for k in range(N_CHIPS)], axis=-1)
    for name in ("even_pre_g", "even_a_ln_g", "even_a_ln_b", "even_mem_g", "even_post_g"):
        wt[name] = w[name]

    tril = jnp.tril(jnp.ones((CH, CH), dtype=bool))
    wcat = jnp.where(tril[None], w["even_a_ws"][0], 0.0).transpose(1, 0, 2).reshape(CH, 4 * CH).astype(BF16)
    bsg = jnp.repeat(w["even_a_bs"][0].T, BW // 4, axis=1)
    hsel = (jnp.arange(BW)[:, None] // (BW // 4) == jnp.arange(128)[None, :]).astype(BF16)
    wg = w["odd_c_wgrp"][0]
    g4 = BW // 4
    wbd = jnp.zeros((BW, BW), F32)
    for g in range(4):
        wbd = lax.dynamic_update_slice(wbd, wg[g], (g * g4, g * g4))
    wbd = wbd.astype(BF16)

    names = ("even_w_kv", "even_w_out", "odd_w_kv", "odd_d_pw_w")
    (p_e, h_e), got = _in_fwd(x, wt["even_pre_g"], wt["even_w_in_t"], "even_in",
                              rider=_GatherRider([placed[n] for n in names]))
    wt.update(zip(names, got))
    kv_e = _kv_fwd(mem, wt["even_mem_g"], wt["even_w_kv"], "even_kv")
    (x1, o_e), got = _even_fwd(x, p_e, kv_e, wt["even_a_ln_g"], wt["even_a_ln_b"], wcat, bsg, wt["even_b_conv"],
                               wt["even_w_out"], wt["even_post_g"], rider=_GatherRider([placed["odd_w_in_t"]]))
    wt["odd_w_in_t"] = got[0]
    (p_o, h_o), got = _in_fwd(x1, wt["odd_pre_g"], wt["odd_w_in_t"], "odd_in",
                              rider=_GatherRider([placed["odd_w_out"]]))
    wt["odd_w_out"] = got[0]
    kv_o = _kv_fwd(mem, wt["odd_mem_g"], wt["odd_w_kv"], "odd_kv")
    dx2, o_o, cv_o, loss = _odd_fwd(x1, p_o, kv_o, wbd, wt["odd_c_scale"], wt["odd_d_dw_w"], wt["odd_d_dw_b"],
                                    wt["odd_d_ln_g"], wt["odd_d_ln_b"], wt["odd_d_pw_w"], wt["odd_d_pw_b"],
                                    wt["odd_w_out"], wt["odd_post_g"], target)
    (dpc_o, tmpc, tmpd, do_o, y_o, g_post_o, g_cs, g_wbd, g_dww, g_dwb, g_lng_o, g_lnb_o, g_pww, g_pwb,
     dkv_o) = _odd_bwd1(dx2, o_o, cv_o, p_o, kv_o, wbd, wt["odd_c_scale"], wt["odd_d_dw_w"], wt["odd_d_dw_b"],
                        wt["odd_d_ln_g"], wt["odd_d_ln_b"], wt["odd_d_pw_w"], wt["odd_d_pw_b"], wt["odd_w_out"],
                        wt["odd_post_g"])
    dpb_o, dx1, g_pre_o = _odd_bwd2(dpc_o, tmpc, tmpd, p_o, wt["odd_d_dw_w"], wt["odd_w_in_t"], x1,
                                    wt["odd_pre_g"], dx2)
    g_win_o = _grad_tn(dpb_o, h_o, 768, rows=ODD_IN, name="odd_gw_in_b")
    g_win_o = _grad_tn(dpc_o, h_o, 1280, out=g_win_o, rows=ODD_IN, row0=3 * BW, name="odd_gw_in_c")
    g_wout_o = _grad_tn(y_o, do_o, 1024, name="odd_gw_out")
    g_wkv_o, g_memg_o = _kv_bwd(mem, wt["odd_mem_g"], wt["odd_w_kv"], dkv_o, "odd_kv_bwd")
    big_o = [_four(g) for g in (g_win_o, g_pww.astype(BF16), g_wkv_o, g_wout_o)]
    recv_o = _swap_halves(big_o, None, "swap_halves_odd")
    sums_o = [_pair_sum(big_o[a], recv_o[a], place, "pair_sum_odd_%d" % a) for a in range(len(big_o))]
    (dp_e, do_e, y_e, g_post_e, g_lng_e, g_lnb_e, g_wcat, g_bs, g_bconv,
     dkv_e), parts_o = _even_bwd1(dx1, o_e, p_e, kv_e, wt["even_a_ln_g"], wt["even_a_ln_b"], wcat, bsg, hsel,
                                  wt["even_b_conv"], wt["even_w_out"], wt["even_post_g"],
                                  rider=_ExchangeRider(sums_o))
    halves_o = [_chip_sum(sums_o[a], parts_o[a], place, "chip_sum_odd_%d" % a) for a in range(len(big_o))]
    g_wout_e = _grad_tn(y_e, do_e, 1024, name="even_gw_out")
    g_wkv_e, g_memg_e = _kv_bwd(mem, wt["even_mem_g"], wt["even_w_kv"], dkv_e, "even_kv_bwd")
    big_x = [_four(g) for g in (g_wkv_e, g_wout_e)]
    recv_x = _swap_halves(big_x, None, "swap_halves_kv_out")
    sums_x = [_pair_sum(big_x[a], recv_x[a], place, "pair_sum_kv_out_%d" % a) for a in range(len(big_x))]
    g_win_e, parts_x = _grad_tn(dp_e, h_e, 1280, name="even_gw_in", rider=_ExchangeRider(sums_x))
    halves_x = [_chip_sum(sums_x[a], parts_x[a], place, "chip_sum_kv_out_%d" % a) for a in range(len(big_x))]
    big_e = [_four(g_win_e)]
    recv_e = _swap_halves(big_e, None, "swap_halves_even")
    sums_e = [_pair_sum(big_e[0], recv_e[0], place, "pair_sum_even_w_in")]
    (dx0, g_pre_e), parts_e = _even_bwd2(dp_e, wt["even_w_in_t"], x, wt["even_pre_g"], dx1,
                                         rider=_ExchangeRider(sums_e))
    halves_e = [_chip_sum(sums_e[0], parts_e[0], place, "chip_sum_even_w_in")]

    g_aws = jnp.where(tril[None], g_wcat.reshape(CH, 4, CH).transpose(1, 0, 2), 0.0)
    g_wgrp = jnp.stack([lax.dynamic_slice(g_wbd, (g * g4, g * g4), (g4, g4)) for g in range(4)])
    small = {
        "even_pre_g": g_pre_e, "even_a_ln_g": g_lng_e, "even_a_ln_b": g_lnb_e, "even_a_ws": g_aws,
        "even_a_bs": g_bs[:, 0:4].T, "even_b_conv": g_bconv[0:3], "even_mem_g": g_memg_e, "even_post_g": g_post_e,
        "odd_pre_g": g_pre_o, "odd_c_wgrp": g_wgrp, "odd_c_scale": g_cs, "odd_d_dw_w": g_dww.reshape(CONF, 8, BW).sum(axis=1),
        "odd_d_dw_b": g_dwb, "odd_d_ln_g": g_lng_o, "odd_d_ln_b": g_lnb_o, "odd_d_pw_b": g_pwb,
        "odd_mem_g": g_memg_o, "odd_post_g": g_post_o,
    }
    small_names = SMALL_EVEN + SMALL_ODD
    small_pack = _flat_pack([small[n] for n in small_names] + [loss[0, 0].reshape(1)], SMALL_ROWS)
    small_total, full = _finish_reduce(small_pack, halves_e + halves_x + halves_o)
    order = ("even_w_in", "even_w_kv", "even_w_out", "odd_w_in", "odd_d_pw_w", "odd_w_kv", "odd_w_out")
    gbig = {n: full[a].reshape(full[a].shape[1] * 2, full[a].shape[2]) for a, n in enumerate(order)}
    return dx0, gbig, small_total.reshape(-1), [small[n].shape for n in small_names]


def kernel(x, mem, even_pre_g, even_w_in, even_a_ln_g, even_a_ln_b, even_a_ws, even_a_bs, even_b_conv, even_mem_g, even_w_kv, even_w_out, even_post_g, odd_pre_g, odd_w_in, odd_c_wgrp, odd_c_scale, odd_d_dw_w, odd_d_dw_b, odd_d_ln_g, odd_d_ln_b, odd_d_pw_w, odd_d_pw_b, odd_mem_g, odd_w_kv, odd_w_out, odd_post_g, loss_target, m_even_pre_g, m_even_w_in, m_even_a_ln_g, m_even_a_ln_b, m_even_a_ws, m_even_a_bs, m_even_b_conv, m_even_mem_g, m_even_w_kv, m_even_w_out, m_even_post_g, m_odd_pre_g, m_odd_w_in, m_odd_c_wgrp, m_odd_c_scale, m_odd_d_dw_w, m_odd_d_dw_b, m_odd_d_ln_g, m_odd_d_ln_b, m_odd_d_pw_w, m_odd_d_pw_b, m_odd_mem_g, m_odd_w_kv, m_odd_w_out, m_odd_post_g, v_even_pre_g, v_even_w_in, v_even_a_ln_g, v_even_a_ln_b, v_even_a_ws, v_even_a_bs, v_even_b_conv, v_even_mem_g, v_even_w_kv, v_even_w_out, v_even_post_g, v_odd_pre_g, v_odd_w_in, v_odd_c_wgrp, v_odd_c_scale, v_odd_d_dw_w, v_odd_d_dw_b, v_odd_d_ln_g, v_odd_d_ln_b, v_odd_d_pw_w, v_odd_d_pw_b, v_odd_mem_g, v_odd_w_kv, v_odd_w_out, v_odd_post_g):
    given = dict(locals())
    w = {n: given[n] for n in WEIGHTS}
    mom = {n: given["m_" + n] for n in WEIGHTS}
    var = {n: given["v_" + n] for n in WEIGHTS}

    x_, y_, c_ = lax.axis_index("x"), lax.axis_index("y"), lax.axis_index("c")
    chip = 2 * x_ + y_
    place = jnp.stack([c_, chip]).astype(jnp.int32)
    grad_x, gbig, gsmall_flat, small_shapes = _step(x[0], mem[0], loss_target[0], w, place)

    names = SMALL_EVEN + SMALL_ODD
    grads = {}
    unpacked = _flat_unpack(gsmall_flat, small_shapes + [(1,)])
    loss = unpacked[-1][0]
    for n, g in zip(names, unpacked[:-1]):
        shard_shape = w[n].shape[1:]
        if g.shape[-1] != shard_shape[-1]:
            g = lax.dynamic_slice_in_dim(g, chip * shard_shape[-1], shard_shape[-1], axis=g.ndim - 1)
        grads[n] = g.reshape(shard_shape)

    def two_d(a):
        return a.reshape(-1, a.shape[-1])

    upd = {}
    for n in BIG:
        if n.endswith("w_in"):
            res = _adamw_big(w[n][0].T, gbig[n], mom[n][0].T, var[n][0].T, "adamw_" + n)
            res = tuple(r.T for r in res)
        else:
            res = _adamw_big(w[n][0], gbig[n], mom[n][0], var[n][0], "adamw_" + n)
        grads[n], upd[n] = res[0], res[1:]
    res = _adamw_small([two_d(w[n][0]) for n in names], [two_d(grads[n]) for n in names],
                       [two_d(mom[n][0]) for n in names], [two_d(var[n][0]) for n in names])
    for n, r in zip(names, res):
        upd[n] = r

    outs = [loss, grad_x[None]]
    outs += [grads[n].reshape(w[n].shape) for n in WEIGHTS]
    for j in range(3):
        outs += [upd[n][j].reshape(w[n].shape) for n in WEIGHTS]
    return tuple(outs)
```

```python
import functools

import jax
import jax.numpy as jnp
from jax import lax
from jax.experimental import pallas as pl
from jax.experimental.pallas import tpu as pltpu

F32 = jnp.float32
BF16 = jnp.bfloat16
MESH = pl.DeviceIdType.MESH

D = 1024
N_MEM = 256
MIX = 2048
XA = 512
HD = 128
BW = 768
CH = 128
EPS = 1e-6
SCALE = HD ** -0.5
POOL_WINDOWS = (2, 4, 8, 16)
CONF = 31
EVEN_IN = 6400
ODD_IN = 4864
N_CHIPS = 4

ADAM_LR = 0.001
ADAM_B1 = 0.9
ADAM_B2 = 0.999
ADAM_EPS = 1e-08
ADAM_WD = 0.01
ADAM_STEP = 10

TS = 256
HALO = 32
VMEM_LIMIT = 56 * 1024 * 1024


def _cp(sem=None):
    return pltpu.CompilerParams(dimension_semantics=sem, vmem_limit_bytes=VMEM_LIMIT)


def _dot(a, b):
    return jnp.dot(a, b, preferred_element_type=F32)


def _dot_nt(a, b):
    return lax.dot_general(a, b, (((1,), (1,)), ((), ())), preferred_element_type=F32)


def _dot_tn(a, b):
    return lax.dot_general(a, b, (((0,), (0,)), ((), ())), preferred_element_type=F32)


def _sigmoid(x):
    return 1.0 / (1.0 + jnp.exp(-x))


def _resident(shape):
    return pl.BlockSpec(shape, lambda *_: (0,) * len(shape), pipeline_mode=pl.Buffered(1))


def _const(shape):
    return pl.BlockSpec(shape, lambda *_: (0,) * len(shape))


def _kv_fwd(mem, mem_g, wkv, name):
    def body(mem_ref, g_ref, w_ref, kv_ref):
        m = mem_ref[...]
        r = lax.rsqrt(jnp.mean(m * m, axis=-1, keepdims=True) + EPS)
        mn = (m * r * g_ref[...]).astype(BF16)
        kv_ref[...] = _dot(mn, w_ref[...]).astype(BF16)

    return pl.pallas_call(body, out_shape=jax.ShapeDtypeStruct((N_MEM, D), BF16), name=name,
                          compiler_params=_cp())(mem, mem_g, wkv)


def _kv_bwd(mem, mem_g, wkv, dkv, name):
    def body(mem_ref, g_ref, w_ref, dkv_ref, dw_ref, dg_ref):
        m = mem_ref[...]
        r = lax.rsqrt(jnp.mean(m * m, axis=-1, keepdims=True) + EPS)
        mh = m * r
        mn = (mh * g_ref[...]).astype(BF16)
        dkv = dkv_ref[...].astype(BF16)
        dw_ref[...] = _dot_tn(mn, dkv).astype(BF16)
        dmn = _dot_nt(dkv, w_ref[...])
        dg_ref[...] = jnp.sum(dmn * mh, axis=0, keepdims=True)

    return pl.pallas_call(body, out_shape=(jax.ShapeDtypeStruct((D, D), BF16), jax.ShapeDtypeStruct((1, D), F32)),
                          name=name, compiler_params=_cp())(mem, mem_g, wkv, dkv)


def _host_call(body, *, grid, name, out_shape, in_specs, out_specs, args, scratch_shapes=(), aliases=None,
               rider=None):
    sem = ("arbitrary",) * len(grid)
    aliases = dict(aliases or {})
    if rider is None:
        res = pl.pallas_call(body, grid=grid, name=name, out_shape=tuple(out_shape), in_specs=list(in_specs),
                             out_specs=tuple(out_specs), scratch_shapes=list(scratch_shapes),
                             input_output_aliases=aliases, compiler_params=_cp(sem))(*args)
        return tuple(res), ()
    n_in, n_out, n_sc = len(in_specs), len(out_specs), len(scratch_shapes)
    r_in, r_out = len(rider.inputs), len(rider.out_shapes)

    def full_body(*refs):
        host_in = refs[:n_in]
        rid_in = refs[n_in:n_in + r_in]
        host_out = refs[n_in + r_in:n_in + r_in + n_out]
        rid_out = refs[n_in + r_in + n_out:n_in + r_in + n_out + r_out]
        host_sc = refs[n_in + r_in + n_out + r_out:n_in + r_in + n_out + r_out + n_sc]
        sems = refs[n_in + r_in + n_out + r_out + n_sc:]
        first = pl.program_id(0) == 0
        last = pl.program_id(0) == grid[0] - 1
        for ax in range(1, len(grid)):
            first = jnp.logical_and(first, pl.program_id(ax) == 0)
            last = jnp.logical_and(last, pl.program_id(ax) == grid[ax] - 1)

        @pl.when(first)
        def _():
            rider.start(rid_in, rid_out, sems)

        if rider.has_mid:
            @pl.when(last)
            def _():
                rider.mid(rid_in, rid_out, sems)

        body(*host_in, *host_out, *host_sc)

        @pl.when(last)
        def _():
            rider.end(rid_in, rid_out, sems)

    aliases.update({n_in + j: n_out + k for j, k in rider.aliases.items()})
    res = pl.pallas_call(
        full_body, grid=grid, name=name, out_shape=tuple(out_shape) + tuple(rider.out_shapes),
        in_specs=list(in_specs) + _hbm_specs(r_in), out_specs=tuple(out_specs) + tuple(_hbm_specs(r_out)),
        scratch_shapes=list(scratch_shapes) + list(rider.sems), input_output_aliases=aliases,
        compiler_params=_cp(sem),
    )(*args, *rider.inputs)
    return tuple(res[:n_out]), tuple(res[n_out:])


def _in_fwd(x, pre_g, w_t, name, rider=None):
    s, n = x.shape[0], w_t.shape[0]
    tm = min(512, s)
    nc = 256

    def body(x_ref, g_ref, w_ref, p_ref, h_ref):
        xv = x_ref[...]
        r = lax.rsqrt(jnp.mean(xv * xv, axis=-1, keepdims=True) + EPS)
        h = (xv * r * g_ref[...]).astype(BF16)
        h_ref[...] = h
        for j in range(n // nc):
            p_ref[:, j * nc:(j + 1) * nc] = _dot_nt(h, w_ref[j * nc:(j + 1) * nc, :]).astype(BF16)

    return _host_call(
        body, grid=(s // tm,), name=name, rider=rider,
        out_shape=(jax.ShapeDtypeStruct((s, n), BF16), jax.ShapeDtypeStruct((s, D), BF16)),
        in_specs=[pl.BlockSpec((tm, D), lambda i: (i, 0)), _const((1, D)), _resident((n, D))],
        out_specs=(pl.BlockSpec((tm, n), lambda i: (i, 0)), pl.BlockSpec((tm, D), lambda i: (i, 0))),
        args=(x, pre_g, w_t))


NC = 256


def _norm_fwd(x, pre_g, name):
    s = x.shape[0]
    tm = min(512, s)

    def body(x_ref, g_ref, h_ref):
        xv = x_ref[...]
        r = lax.rsqrt(jnp.mean(xv * xv, axis=-1, keepdims=True) + EPS)
        h_ref[...] = (xv * r * g_ref[...]).astype(BF16)

    row = pl.BlockSpec((tm, D), lambda i: (i, 0))
    return pl.pallas_call(body, grid=(s // tm,), name=name, out_shape=jax.ShapeDtypeStruct((s, D), BF16),
                          in_specs=[row, _const((1, D))], out_specs=row, compiler_params=_cp(("arbitrary",)))(x, pre_g)


def _stream_tables(chip, n):
    nchunk = n // NC
    idx = jnp.arange(nchunk, dtype=jnp.int32)
    rel = jnp.array([0, 2, 1, 3], jnp.int32)
    r = n // N_CHIPS
    grp = jnp.maximum(rel[((idx * NC) // r) ^ chip], rel[((idx * NC + NC - 1) // r) ^ chip])
    order = jnp.argsort(grp * 64 + idx).astype(jnp.int32)
    return order, grp[order]


def _in_fwd_streamed(h, w_placed, pack_placed, later, order, group):
    s, n = h.shape[0], w_placed.shape[0]
    nchunk = n // NC
    rider = _GatherRider([w_placed, pack_placed])
    rider2 = _GatherRider(later)
    m = len(later)

    def body(*refs):
        order_ref, group_ref, h_ref = refs[0:3]
        p_ref, w_hbm, pack_hbm = refs[5 + m:8 + m]
        outs2 = refs[8 + m:8 + 2 * m]
        wbuf, wsem, send_sems, recv_sems, send2, recv2 = refs[8 + 2 * m:]
        j = pl.program_id(0)
        outs, sems, sems2 = (w_hbm, pack_hbm), (send_sems, recv_sems), (send2, recv2)
        grp = group_ref[j]
        new_group = jnp.logical_or(j == 0, group_ref[jnp.maximum(j - 1, 0)] != grp)
        slot = j % 2

        def fetch(step, sl):
            rows = pl.ds(pl.multiple_of(order_ref[step] * NC, NC), NC)
            return pltpu.make_async_copy(w_hbm.at[rows], wbuf.at[sl], wsem.at[sl])

        @pl.when(j == 0)
        def _():
            rider.start(None, outs, sems, peers=(0, 1))

        for src in range(3):
            @pl.when(jnp.logical_and(new_group, grp == src + 1))
            def _(src=src):
                if src == 0:
                    rider.start(None, outs, sems, peers=(2,))
                rider.mid(None, outs, sems, peers=(src,))
                rider.wait_forwarded(outs, sems, peers=(src,))
                if src == 1:
                    rider2.start(None, outs2, sems2)

        @pl.when(new_group)
        def _():
            fetch(j, slot).start()

        fetch(j, slot).wait()
        nxt = jnp.minimum(j + 1, nchunk - 1)

        @pl.when(jnp.logical_and(j + 1 < nchunk, group_ref[nxt] == grp))
        def _():
            fetch(nxt, 1 - slot).start()

        p_ref[...] = _dot_nt(h_ref[...], wbuf[slot]).astype(BF16)

        @pl.when(j == nchunk - 1)
        def _():
            rider.wait_sends(outs, sems)
            rider2.mid(None, outs2, sems2)
            rider2.end(None, outs2, sems2)

    hbm = pl.BlockSpec(memory_space=pltpu.HBM)
    aliases = {3: 1, 4: 2}
    aliases.update({5 + a: 3 + a for a in range(m)})
    res = pl.pallas_call(
        body, name="even_in_streamed",
        out_shape=(jax.ShapeDtypeStruct((s, n), BF16), jax.ShapeDtypeStruct(w_placed.shape, w_placed.dtype),
                   jax.ShapeDtypeStruct(pack_placed.shape, pack_placed.dtype))
        + tuple(jax.ShapeDtypeStruct(a.shape, a.dtype) for a in later),
        grid_spec=pltpu.PrefetchScalarGridSpec(
            num_scalar_prefetch=2, grid=(nchunk,),
            in_specs=[pl.BlockSpec((s, D), lambda j, o, g: (0, 0), pipeline_mode=pl.Buffered(1))] + [hbm] * (2 + m),
            out_specs=(pl.BlockSpec((s, NC), lambda j, o, g: (0, o[j])),) + (hbm,) * (2 + m),
            scratch_shapes=[pltpu.VMEM((2, NC, D), BF16), pltpu.SemaphoreType.DMA((2,))] + list(rider.sems)
            + list(rider2.sems)),
        input_output_aliases=aliases,
        compiler_params=_cp(("arbitrary",)),
    )(order, group, h, w_placed, pack_placed, *later)
    return res[0], res[1], res[2], tuple(res[3:])


def _xattn_fwd(q, kv_ref):
    outs, probs = [], []
    for h in range(XA // HD):
        qh = q[:, h * HD:(h + 1) * HD]
        kh = kv_ref[:, h * HD:(h + 1) * HD]
        vh = kv_ref[:, XA + h * HD:XA + (h + 1) * HD]
        sc = _dot_nt(qh, kh) * SCALE
        e = jnp.exp(sc - jnp.max(sc, axis=-1, keepdims=True))
        pr = e / jnp.sum(e, axis=-1, keepdims=True)
        outs.append(_dot(pr.astype(BF16), vh))
        probs.append(pr)
    return jnp.concatenate(outs, axis=-1), probs


def _xattn_bwd(dyx, q, probs, kv_ref, dkv_ref):
    dqs = []
    for h in range(XA // HD):
        qh = q[:, h * HD:(h + 1) * HD]
        kh = kv_ref[:, h * HD:(h + 1) * HD]
        vh = kv_ref[:, XA + h * HD:XA + (h + 1) * HD]
        dy = dyx[:, h * HD:(h + 1) * HD].astype(BF16)
        pr = probs[h]
        dp = _dot_nt(dy, vh)
        ds = (pr * (dp - jnp.sum(dp * pr, axis=-1, keepdims=True))).astype(BF16)
        dqs.append(_dot(ds, kh) * SCALE)
        dkv_ref[:, h * HD:(h + 1) * HD] += _dot_tn(ds, qh) * SCALE
        dkv_ref[:, XA + h * HD:XA + (h + 1) * HD] += _dot_tn(pr.astype(BF16), dy)
    return jnp.concatenate(dqs, axis=-1)


def _layer_norm_fwd(v, g, b):
    mu = jnp.mean(v, axis=-1, keepdims=True)
    vc = v - mu
    rstd = lax.rsqrt(jnp.mean(vc * vc, axis=-1, keepdims=True) + EPS)
    vhat = vc * rstd
    return vhat * g + b, vhat, rstd


def _layer_norm_bwd(dy, vhat, rstd, g):
    dvh = dy * g
    return rstd * (dvh - jnp.mean(dvh, axis=-1, keepdims=True) - vhat * jnp.mean(dvh * vhat, axis=-1, keepdims=True))


def _head_masks():
    col = lax.broadcasted_iota(jnp.int32, (1, BW), 1)
    return [(col >= h * (BW // 4)) & (col < (h + 1) * (BW // 4)) for h in range(4)]


def _halo_prev(nblk_per_tile):
    return lambda i: (jnp.maximum(i * nblk_per_tile - 1, 0), 0)


def _row_ids(i, t):
    return i * t + lax.broadcasted_iota(jnp.int32, (t, 1), 0)


def _even_mix(i, p_ref, ph_ref, ln_g, ln_b, wcat_ref, bsg_ref, bconv_ref, wbuf):
    t = p_ref.shape[0]
    u = p_ref[:, 0:BW].astype(F32)
    v = p_ref[:, BW:2 * BW].astype(F32)
    bg = p_ref[:, 2 * BW:3 * BW].astype(F32)
    cg = p_ref[:, 3 * BW:4 * BW].astype(F32)
    xin = p_ref[:, 4 * BW:5 * BW].astype(F32)
    vn, vhat, rstd = _layer_norm_fwd(v, ln_g, ln_b)
    masks = _head_masks()
    sgs, vsts = [], []
    for n in range(t // CH):
        vn_c = vn[n * CH:(n + 1) * CH]
        vst = jnp.concatenate([jnp.where(m, vn_c, 0.0) for m in masks], axis=0).astype(BF16)
        sgs.append(_dot(wcat_ref[...], vst) + bsg_ref[...])
        vsts.append(vst)
    sg = jnp.concatenate(sgs, axis=0)
    ya = u * sg
    w_halo = ph_ref[:, 3 * BW:4 * BW].astype(F32) * ph_ref[:, 4 * BW:5 * BW].astype(F32)
    wbuf[0:HALO, :] = jnp.where(i > 0, w_halo, 0.0)
    wbuf[HALO:HALO + t, :] = cg * xin
    conv = (bconv_ref[0:1, :] * wbuf[pl.ds(HALO - 2, t), :] + bconv_ref[1:2, :] * wbuf[pl.ds(HALO - 1, t), :]
            + bconv_ref[2:3, :] * wbuf[pl.ds(HALO, t), :])
    yb = bg * conv
    return dict(u=u, bg=bg, cg=cg, xin=xin, vhat=vhat, rstd=rstd, sg=sg, vsts=vsts, conv=conv, ya=ya, yb=yb,
                masks=masks)


def _pool_select(vals):
    col = lax.broadcasted_iota(jnp.int32, (1, BW), 1)
    g = BW // 4
    return jnp.where(col < g, vals[0], jnp.where(col < 2 * g, vals[1], jnp.where(col < 3 * g, vals[2], vals[3])))


def _inv_counts(i, t):
    rows = _row_ids(i, t) + 1
    return [1.0 / jnp.minimum(rows, w).astype(F32) for w in POOL_WINDOWS]


def _band_matrices(t, forward):
    j = jnp.arange(t)[:, None]
    r = jnp.arange(HALO + t)[None, :]
    if forward:
        return jnp.stack([(r >= j) & (r < j + w) for w in POOL_WINDOWS]).astype(BF16)
    return jnp.stack([(r <= HALO + j) & (r > HALO + j - w) for w in POOL_WINDOWS]).astype(BF16)


SHIFT_ROWS = HALO + TS - 8


def _shifted_copies(buf, sh):
    for b in range(1, 8):
        sh[b - 1] = buf[pl.ds(b, SHIFT_ROWS), :]


def _rows_at(buf, sh, off, t):
    a, b = divmod(off, 8)
    return buf[pl.ds(8 * a, t), :] if b == 0 else sh[b - 1, pl.ds(8 * a, t), :]


def _tap_sums(d_ref, buf, sh, base, out_ref):
    t = d_ref.shape[0]
    group = 4
    for k0 in range(0, CONF, group):
        taps = list(range(k0, min(k0 + group, CONF)))

        def step(r, accs, taps=taps):
            row = pl.multiple_of(r * 8, 8)
            d = d_ref[pl.ds(row, 8), :]
            new = []
            for acc, k in zip(accs, taps):
                a, b = divmod(base + k, 8)
                src = buf[pl.ds(row + 8 * a, 8), :] if b == 0 else sh[b - 1, pl.ds(row + 8 * a, 8), :]
                new.append(acc + d * src)
            return tuple(new)

        accs = lax.fori_loop(0, t // 8, step, tuple(jnp.zeros((8, BW), F32) for _ in taps), unroll=2)
        for acc, k in zip(accs, taps):
            out_ref[8 * k:8 * k + 8, :] += acc


def _odd_mix(i, p_ref, ph_ref, bands_ref, wbd_ref, cscale, dww_ref, dwb, ln_g, ln_b, pww_ref, pwb, gbuf, gsh,
             cv=None):
    t = p_ref.shape[0]
    zc_bf = p_ref[:, 0:BW]
    zc = zc_bf.astype(F32)
    ga = p_ref[:, BW:2 * BW].astype(F32)
    gb = p_ref[:, 2 * BW:3 * BW].astype(F32)
    zh = ph_ref[:, 0:BW]
    zcat = jnp.concatenate([jnp.where(i > 0, zh, jnp.zeros_like(zh)), zc_bf], axis=0)
    inv = _inv_counts(i, t)
    pooled = _pool_select([_dot(bands_ref[w], zcat) * inv[w] for w in range(len(POOL_WINDOWS))]) - zc
    pooled_bf = pooled.astype(BF16)
    pre = _dot(pooled_bf, wbd_ref[...])
    yc = pre * cscale
    sgb = _sigmoid(gb)
    z = ga * sgb
    gh_a = ph_ref[:, BW:2 * BW].astype(F32)
    gh_b = ph_ref[:, 2 * BW:3 * BW].astype(F32)
    gbuf[0:HALO, :] = jnp.where(i > 0, gh_a * _sigmoid(gh_b), 0.0)
    gbuf[HALO:HALO + t, :] = z
    _shifted_copies(gbuf, gsh)
    if cv is None:
        cv = dwb + dww_ref[CONF - 1:CONF, :] * z
        for k in range(CONF - 1):
            cv = cv + dww_ref[k:k + 1, :] * _rows_at(gbuf, gsh, HALO - (CONF - 1) + k, t)
    zl, zhat, rstd = _layer_norm_fwd(cv, ln_g, ln_b)
    szl = _sigmoid(zl)
    zs = (zl * szl).astype(BF16)
    yd = _dot(zs, pww_ref[...]) + pwb
    return dict(ga=ga, sgb=sgb, pooled_bf=pooled_bf, pre=pre, yc=yc, zhat=zhat, rstd=rstd, zl=zl, szl=szl,
                zs=zs, yd=yd, inv=inv, cv=cv)


def _post_norm(o, post_g):
    r = lax.rsqrt(jnp.mean(o * o, axis=-1, keepdims=True) + EPS)
    return o * r, r


def _gate_out(y_a, y_b, y_x, gate, wout_ref):
    sgt = _sigmoid(gate)
    sgate = gate * sgt
    ys = [(y_a * sgate[:, 0:BW]).astype(BF16), (y_b * sgate[:, BW:2 * BW]).astype(BF16),
          (y_x * sgate[:, 2 * BW:MIX]).astype(BF16)]
    o = (_dot(ys[0], wout_ref[0:BW, :]) + _dot(ys[1], wout_ref[BW:2 * BW, :]) + _dot(ys[2], wout_ref[2 * BW:MIX, :]))
    return o, ys, sgt, sgate


def _tile_specs(s, n):
    nh = TS // HALO
    return pl.BlockSpec((TS, n), lambda i: (i, 0)), pl.BlockSpec((HALO, n), _halo_prev(nh))


def _even_fwd(x, p, kv, ln_g, ln_b, wcat, bsg, bconv, wout, post_g, rider=None):
    s = x.shape[0]

    def body(x_ref, p_ref, ph_ref, kv_ref, lng, lnb, wcat_ref, bsg_ref, bconv_ref, wout_ref, pg, x1_ref, o_ref, wbuf):
        i = pl.program_id(0)
        mx = _even_mix(i, p_ref, ph_ref, lng[...], lnb[...], wcat_ref, bsg_ref, bconv_ref, wbuf)
        yx, _ = _xattn_fwd(p_ref[:, 5 * BW:5 * BW + XA], kv_ref)
        gate = p_ref[:, 5 * BW + XA:EVEN_IN].astype(F32)
        o, _, _, _ = _gate_out(mx["ya"], mx["yb"], yx, gate, wout_ref)
        n, _ = _post_norm(o, pg[...])
        o_ref[...] = o
        x1_ref[...] = x_ref[...] + n * pg[...]

    tile, halo = _tile_specs(s, EVEN_IN)
    row = pl.BlockSpec((TS, D), lambda i: (i, 0))
    return _host_call(
        body, grid=(s // TS,), name="even_fwd", rider=rider,
        out_shape=(jax.ShapeDtypeStruct((s, D), F32), jax.ShapeDtypeStruct((s, D), F32)),
        in_specs=[row, tile, halo, _const((N_MEM, D)), _const((1, BW)), _const((1, BW)), _const((CH, 4 * CH)),
                  _const((CH, BW)), _const((3, BW)), _resident((MIX, D)), _const((1, D))],
        out_specs=(row, row),
        scratch_shapes=[pltpu.VMEM((HALO + TS, BW), F32)],
        args=(x, p, p, kv, ln_g, ln_b, wcat, bsg, bconv, wout, post_g))


def _odd_fwd(x1, p, kv, wbd, cscale, dww, dwb, ln_g, ln_b, pww, pwb, wout, post_g, target):
    s = x1.shape[0]

    def body(x_ref, p_ref, ph_ref, kv_ref, bands_ref, wbd_ref, cs, dww_ref, dwb_ref, lng, lnb, pww_ref, pwb_ref,
             wout_ref, pg, tgt_ref, dx_ref, o_ref, cv_ref, loss_ref, gbuf, gsh):
        i = pl.program_id(0)
        mx = _odd_mix(i, p_ref, ph_ref, bands_ref, wbd_ref, cs[...], dww_ref, dwb_ref[...], lng[...], lnb[...],
                      pww_ref, pwb_ref[...], gbuf, gsh)
        cv_ref[...] = mx["cv"]
        yx, _ = _xattn_fwd(p_ref[:, 3 * BW:3 * BW + XA], kv_ref)
        gate = p_ref[:, 3 * BW + XA:ODD_IN].astype(F32)
        o, _, _, _ = _gate_out(mx["yc"], mx["yd"], yx, gate, wout_ref)
        n, _ = _post_norm(o, pg[...])
        o_ref[...] = o
        err = x_ref[...] + n * pg[...] - tgt_ref[...]
        dx_ref[...] = err * (1.0 / D)

        @pl.when(i == 0)
        def _():
            loss_ref[...] = jnp.zeros_like(loss_ref)

        loss_ref[...] += 0.5 * jnp.sum(jnp.sum(err * err, axis=-1, keepdims=True) * (1.0 / D), axis=0, keepdims=True)

    tile, halo = _tile_specs(s, ODD_IN)
    row = pl.BlockSpec((TS, D), lambda i: (i, 0))
    vec = _const((1, BW))
    return pl.pallas_call(
        body, grid=(s // TS,), name="odd_fwd",
        out_shape=(jax.ShapeDtypeStruct((s, D), F32), jax.ShapeDtypeStruct((s, D), F32),
                   jax.ShapeDtypeStruct((s, BW), F32), jax.ShapeDtypeStruct((8, 128), F32)),
        in_specs=[row, tile, halo, _const((N_MEM, D)), _const((4, TS, HALO + TS)), _const((BW, BW)), vec,
                  _const((CONF, BW)), vec, vec, vec, _const((BW, BW)), vec, _resident((MIX, D)), _const((1, D)), row],
        out_specs=(row, row, pl.BlockSpec((TS, BW), lambda i: (i, 0)), _const((8, 128))),
        scratch_shapes=[pltpu.VMEM((HALO + TS, BW), F32), pltpu.VMEM((7, SHIFT_ROWS, BW), F32)],
        compiler_params=_cp(("arbitrary",)),
    )(x1, p, p, kv, _band_matrices(TS, False), wbd, cscale, dww, dwb, ln_g, ln_b, pww, pwb, wout, post_g, target)


def _acc_init(i, refs):
    @pl.when(i == 0)
    def _():
        for r in refs:
            r[...] = jnp.zeros_like(r)


def _post_norm_bwd(dx, o, pg, dpg_ref):
    n, r = _post_norm(o, pg)
    dpg_ref[...] += jnp.sum(dx * n, axis=0, keepdims=True)
    dn = dx * pg
    return (r * (dn - n * jnp.mean(dn * n, axis=-1, keepdims=True))).astype(BF16)


def _gate_bwd(do, wout_ref, ys_f32, gate, y_ref):
    dy = _dot_nt(do, wout_ref[...])
    sgt = _sigmoid(gate)
    sgate = gate * sgt
    dsilu = sgt * (1.0 + gate * (1.0 - sgt))
    offs = (0, BW, 2 * BW, MIX)
    dys, dgs = [], []
    for j, yv in enumerate(ys_f32):
        a, b = offs[j], offs[j + 1]
        y_ref[:, a:b] = (yv * sgate[:, a:b]).astype(BF16)
        dys.append(dy[:, a:b] * sgate[:, a:b])
        dgs.append(dy[:, a:b] * yv * dsilu[:, a:b])
    return dys, jnp.concatenate(dgs, axis=-1)


NEXT = 16


def _even_bwd1(dx, o, p, kv, ln_g, ln_b, wcat, bsg, hsel, bconv, wout, post_g, rider=None):
    s = dx.shape[0]
    nt = s // TS

    def body(dx_ref, o_ref, p_ref, ph_ref, dxn_ref, on_ref, pn_ref, kv_ref, lng, lnb, wcat_ref, bsg_ref, hsel_ref,
             bconv_ref, wout_ref, pg,
             dp_ref, do_ref, y_ref, dpg_ref, dlng_ref, dlnb_ref, dwcat_ref, dbs_ref, dbconv_ref, dkv_ref, wbuf, dbuf):
        i = pl.program_id(0)
        _acc_init(i, (dpg_ref, dlng_ref, dlnb_ref, dwcat_ref, dbs_ref, dbconv_ref, dkv_ref))
        mx = _even_mix(i, p_ref, ph_ref, lng[...], lnb[...], wcat_ref, bsg_ref, bconv_ref, wbuf)
        q = p_ref[:, 5 * BW:5 * BW + XA]
        yx, probs = _xattn_fwd(q, kv_ref)
        gate = p_ref[:, 5 * BW + XA:EVEN_IN].astype(F32)
        do = _post_norm_bwd(dx_ref[...], o_ref[...], pg[...], dpg_ref)
        do_ref[...] = do
        (dya, dyb, dyx), dgate = _gate_bwd(do, wout_ref, (mx["ya"], mx["yb"], yx), gate, y_ref)
        dp_ref[:, 0:BW] = (dya * mx["sg"]).astype(BF16)
        dsg = (dya * mx["u"]).astype(BF16)
        dvns = []
        for n in range(TS // CH):
            dsg_c = dsg[n * CH:(n + 1) * CH]
            dvst = _dot_tn(wcat_ref[...], dsg_c)
            dvn_c = jnp.where(mx["masks"][0], dvst[0:CH], 0.0)
            for h in range(1, 4):
                dvn_c = dvn_c + jnp.where(mx["masks"][h], dvst[h * CH:(h + 1) * CH], 0.0)
            dvns.append(dvn_c)
            dwcat_ref[...] += _dot_nt(dsg_c, mx["vsts"][n])
            dbs_ref[...] += _dot(dsg_c, hsel_ref[...])
        dvn = jnp.concatenate(dvns, axis=0)
        dlng_ref[...] += jnp.sum(dvn * mx["vhat"], axis=0, keepdims=True)
        dlnb_ref[...] += jnp.sum(dvn, axis=0, keepdims=True)
        dp_ref[:, BW:2 * BW] = _layer_norm_bwd(dvn, mx["vhat"], mx["rstd"], lng[...]).astype(BF16)
        dp_ref[:, 2 * BW:3 * BW] = (dyb * mx["conv"]).astype(BF16)
        dconv = dyb * mx["bg"]
        for k in range(3):
            dbconv_ref[k:k + 1, :] += jnp.sum(dconv * wbuf[pl.ds(HALO - 2 + k, TS), :], axis=0, keepdims=True)
        n_n, r_n = _post_norm(on_ref[...], pg[...])
        dn_n = dxn_ref[...] * pg[...]
        do_n = (r_n * (dn_n - n_n * jnp.mean(dn_n * n_n, axis=-1, keepdims=True))).astype(BF16)
        dy_n = _dot_nt(do_n, wout_ref[BW:2 * BW, :])
        g_n = pn_ref[:, 5 * BW + XA + BW:5 * BW + XA + 2 * BW].astype(F32)
        dconv_n = dy_n * (g_n * _sigmoid(g_n)) * pn_ref[:, 2 * BW:3 * BW].astype(F32)
        dbuf[0:TS, :] = dconv
        dbuf[TS:TS + NEXT, :] = jnp.where(i < nt - 1, dconv_n, 0.0)
        dw = (bconv_ref[2:3, :] * dconv + bconv_ref[1:2, :] * dbuf[pl.ds(1, TS), :]
              + bconv_ref[0:1, :] * dbuf[pl.ds(2, TS), :])
        dp_ref[:, 3 * BW:4 * BW] = (dw * mx["xin"]).astype(BF16)
        dp_ref[:, 4 * BW:5 * BW] = (dw * mx["cg"]).astype(BF16)
        dp_ref[:, 5 * BW:5 * BW + XA] = _xattn_bwd(dyx, q, probs, kv_ref, dkv_ref).astype(BF16)
        dp_ref[:, 5 * BW + XA:EVEN_IN] = dgate.astype(BF16)

    tile, halo = _tile_specs(s, EVEN_IN)
    row = pl.BlockSpec((TS, D), lambda i: (i, 0))
    vec = _const((1, BW))
    nxt = _halo_next(TS // NEXT, s // NEXT)

    def out(n):
        return pl.BlockSpec((TS, n), lambda i: (i, 0))

    return _host_call(
        body, grid=(nt,), name="even_bwd1", rider=rider,
        out_shape=(jax.ShapeDtypeStruct((s, EVEN_IN), BF16), jax.ShapeDtypeStruct((s, D), BF16),
                   jax.ShapeDtypeStruct((s, MIX), BF16),
                   jax.ShapeDtypeStruct((1, D), F32), jax.ShapeDtypeStruct((1, BW), F32),
                   jax.ShapeDtypeStruct((1, BW), F32), jax.ShapeDtypeStruct((CH, 4 * CH), F32),
                   jax.ShapeDtypeStruct((CH, 128), F32), jax.ShapeDtypeStruct((8, BW), F32),
                   jax.ShapeDtypeStruct((N_MEM, D), F32)),
        in_specs=[row, row, tile, halo, pl.BlockSpec((NEXT, D), nxt), pl.BlockSpec((NEXT, D), nxt),
                  pl.BlockSpec((NEXT, EVEN_IN), nxt), _const((N_MEM, D)), vec, vec, _const((CH, 4 * CH)),
                  _const((CH, BW)), _const((BW, 128)), _const((3, BW)), _resident((MIX, D)), _const((1, D))],
        out_specs=(out(EVEN_IN), out(D), out(MIX),
                   _const((1, D)), vec, vec, _const((CH, 4 * CH)), _const((CH, 128)), _const((8, BW)),
                   _const((N_MEM, D))),
        scratch_shapes=[pltpu.VMEM((HALO + TS, BW), F32), pltpu.VMEM((TS + NEXT, BW), F32)],
        args=(dx, o, p, p, dx, o, p, kv, ln_g, ln_b, wcat, bsg, hsel, bconv, wout, post_g))


def _odd_bwd1(dx, o, cv, p, kv, wbd, cscale, dww, dwb, ln_g, ln_b, pww, pwb, wout, post_g):
    s = dx.shape[0]

    def body(dx_ref, o_ref, cv_ref, p_ref, ph_ref, kv_ref, bands_ref, wbd_ref, cs, dww_ref, dwb_ref, lng, lnb,
             pww_ref, pwb_ref, wout_ref, pg,
             dpc_ref, tmpc_ref, tmpd_ref, do_ref, y_ref, dpg_ref, dcs_ref, dwbd_ref, ddww_ref, ddwb_ref, dlng_ref,
             dlnb_ref, dpww_ref, dpwb_ref, dkv_ref, gbuf, gsh, dcv_buf):
        i = pl.program_id(0)
        _acc_init(i, (dpg_ref, dcs_ref, dwbd_ref, ddww_ref, ddwb_ref, dlng_ref, dlnb_ref, dpww_ref, dpwb_ref,
                      dkv_ref))
        mx = _odd_mix(i, p_ref, ph_ref, bands_ref, wbd_ref, cs[...], dww_ref, dwb_ref[...], lng[...], lnb[...],
                      pww_ref, pwb_ref[...], gbuf, gsh, cv=cv_ref[...])
        q = p_ref[:, 3 * BW:3 * BW + XA]
        yx, probs = _xattn_fwd(q, kv_ref)
        gate = p_ref[:, 3 * BW + XA:ODD_IN].astype(F32)
        do = _post_norm_bwd(dx_ref[...], o_ref[...], pg[...], dpg_ref)
        do_ref[...] = do
        (dyc, dyd, dyx), dgate = _gate_bwd(do, wout_ref, (mx["yc"], mx["yd"], yx), gate, y_ref)
        dcs_ref[...] += jnp.sum(dyc * mx["pre"], axis=0, keepdims=True)
        dpre = (dyc * cs[...]).astype(BF16)
        dwbd_ref[...] += _dot_tn(mx["pooled_bf"], dpre)
        dpooled = _dot_nt(dpre, wbd_ref[...])
        tmpc_ref[...] = _pool_select([dpooled * c_ for c_ in mx["inv"]]).astype(BF16)
        dyd_bf = dyd.astype(BF16)
        dpwb_ref[...] += jnp.sum(dyd, axis=0, keepdims=True)
        dpww_ref[...] += _dot_tn(mx["zs"], dyd_bf)
        dzs = _dot_nt(dyd_bf, pww_ref[...])
        zl, szl = mx["zl"], mx["szl"]
        dzl = dzs * (szl * (1.0 + zl * (1.0 - szl)))
        dlng_ref[...] += jnp.sum(dzl * mx["zhat"], axis=0, keepdims=True)
        dlnb_ref[...] += jnp.sum(dzl, axis=0, keepdims=True)
        dcv = _layer_norm_bwd(dzl, mx["zhat"], mx["rstd"], lng[...])
        tmpd_ref[...] = dcv.astype(BF16)
        ddwb_ref[...] += jnp.sum(dcv, axis=0, keepdims=True)
        dcv_buf[...] = dcv
        _tap_sums(dcv_buf, gbuf, gsh, HALO - (CONF - 1), ddww_ref)
        dpc_ref[:, 0:XA] = _xattn_bwd(dyx, q, probs, kv_ref, dkv_ref).astype(BF16)
        dpc_ref[:, XA:XA + MIX] = dgate.astype(BF16)

    tile, halo = _tile_specs(s, ODD_IN)
    row = pl.BlockSpec((TS, D), lambda i: (i, 0))
    vec = _const((1, BW))

    def out(n):
        return pl.BlockSpec((TS, n), lambda i: (i, 0))

    return pl.pallas_call(
        body, grid=(s // TS,), name="odd_bwd1",
        out_shape=(jax.ShapeDtypeStruct((s, XA + MIX), BF16), jax.ShapeDtypeStruct((s, BW), BF16),
                   jax.ShapeDtypeStruct((s, BW), BF16), jax.ShapeDtypeStruct((s, D), BF16),
                   jax.ShapeDtypeStruct((s, MIX), BF16),
                   jax.ShapeDtypeStruct((1, D), F32), jax.ShapeDtypeStruct((1, BW), F32),
                   jax.ShapeDtypeStruct((BW, BW), F32), jax.ShapeDtypeStruct((8 * CONF, BW), F32),
                   jax.ShapeDtypeStruct((1, BW), F32), jax.ShapeDtypeStruct((1, BW), F32),
                   jax.ShapeDtypeStruct((1, BW), F32), jax.ShapeDtypeStruct((BW, BW), F32),
                   jax.ShapeDtypeStruct((1, BW), F32), jax.ShapeDtypeStruct((N_MEM, D), F32)),
        in_specs=[row, row, out(BW), tile, halo, _const((N_MEM, D)), _const((4, TS, HALO + TS)), _const((BW, BW)), vec,
                  _const((CONF, BW)), vec, vec, vec, _const((BW, BW)), vec, _resident((MIX, D)), _const((1, D))],
        out_specs=(out(XA + MIX), out(BW), out(BW), out(D), out(MIX),
                   _const((1, D)), vec, _const((BW, BW)), _const((8 * CONF, BW)), vec, vec, vec, _const((BW, BW)), vec,
                   _const((N_MEM, D))),
        scratch_shapes=[pltpu.VMEM((HALO + TS, BW), F32), pltpu.VMEM((7, SHIFT_ROWS, BW), F32),
                        pltpu.VMEM((TS, BW), F32)],
        compiler_params=_cp(("arbitrary",)),
    )(dx, o, cv, p, p, kv, _band_matrices(TS, False), wbd, cscale, dww, dwb, ln_g, ln_b, pww, pwb, wout, post_g)


def _halo_next(nblk_per_tile, nblk):
    return lambda i: (jnp.minimum((i + 1) * nblk_per_tile, nblk - 1), 0)


def _pre_norm_bwd(dh, x, pre_g, dres, dpre_ref):
    r = lax.rsqrt(jnp.mean(x * x, axis=-1, keepdims=True) + EPS)
    xh = x * r
    dpre_ref[...] += jnp.sum(dh * xh, axis=0, keepdims=True)
    dxh = dh * pre_g
    return dres + r * (dxh - xh * jnp.mean(dxh * xh, axis=-1, keepdims=True))


def _even_bwd2(dp, w_t, x, pre_g, dres, rider=None):
    s = x.shape[0]
    tm = min(512, s)

    def body(dp_ref, w_ref, x_ref, pg, dres_ref, dx_ref, dpre_ref):
        _acc_init(pl.program_id(0), (dpre_ref,))
        dh = _dot(dp_ref[...], w_ref[...])
        dx_ref[...] = _pre_norm_bwd(dh, x_ref[...], pg[...], dres_ref[...], dpre_ref)

    row = pl.BlockSpec((tm, D), lambda i: (i, 0))
    return _host_call(
        body, grid=(s // tm,), name="even_bwd2", rider=rider,
        out_shape=(jax.ShapeDtypeStruct((s, D), F32), jax.ShapeDtypeStruct((1, D), F32)),
        in_specs=[pl.BlockSpec((tm, EVEN_IN), lambda i: (i, 0)), _resident((EVEN_IN, D)), row, _const((1, D)), row],
        out_specs=(row, _const((1, D))),
        args=(dp, w_t, x, pre_g, dres))


def _odd_bwd2(dpc, tmpc, tmpd, p, dww, w_t, x, pre_g, dres):
    s = x.shape[0]
    nt = s // TS

    def body(dpc_ref, tc_ref, tch_ref, td_ref, tdh_ref, ga_ref, gb_ref, bands_ref, dww_ref, w_ref, x_ref, pg,
             dres_ref, dpb_ref, dx_ref, dpre_ref, dbuf, dsh):
        i = pl.program_id(0)
        _acc_init(i, (dpre_ref,))
        more = i < nt - 1
        e_bf = tc_ref[...]
        eh = tch_ref[...]
        ecat = jnp.concatenate([e_bf, jnp.where(more, eh, jnp.zeros_like(eh))], axis=0)
        dbuf[0:TS, :] = td_ref[...].astype(F32)
        dbuf[TS:TS + HALO, :] = jnp.where(more, tdh_ref[...].astype(F32), 0.0)
        sums = [_dot(bands_ref[w], ecat) for w in range(len(POOL_WINDOWS))]
        rows = _row_ids(i, TS) + 1
        cnt = _pool_select([jnp.minimum(rows, w).astype(F32) for w in POOL_WINDOWS])
        dzc = (_pool_select(sums) - e_bf.astype(F32) * cnt).astype(BF16)
        _shifted_copies(dbuf, dsh)
        dz = dww_ref[CONF - 1:CONF, :] * dbuf[pl.ds(0, TS), :]
        for sft in range(1, CONF):
            dz = dz + dww_ref[CONF - 1 - sft:CONF - sft, :] * _rows_at(dbuf, dsh, sft, TS)
        ga = ga_ref[...].astype(F32)
        sgb = _sigmoid(gb_ref[...].astype(F32))
        dga = (dz * sgb).astype(BF16)
        dgb = (dz * ga * sgb * (1.0 - sgb)).astype(BF16)
        dpb_ref[:, 0:BW] = dzc
        dpb_ref[:, BW:2 * BW] = dga
        dpb_ref[:, 2 * BW:3 * BW] = dgb
        dh = (_dot(dzc, w_ref[0:BW, :]) + _dot(dga, w_ref[BW:2 * BW, :]) + _dot(dgb, w_ref[2 * BW:3 * BW, :])
              + _dot(dpc_ref[...], w_ref[3 * BW:ODD_IN, :]))
        dx_ref[...] = _pre_norm_bwd(dh, x_ref[...], pg[...], dres_ref[...], dpre_ref)

    row = pl.BlockSpec((TS, D), lambda i: (i, 0))

    def tile(n, j=0):
        return pl.BlockSpec((TS, n), lambda i: (i, j))

    nxt = pl.BlockSpec((HALO, BW), _halo_next(TS // HALO, s // HALO))
    return pl.pallas_call(
        body, grid=(nt,), name="odd_bwd2",
        out_shape=(jax.ShapeDtypeStruct((s, 3 * BW), BF16), jax.ShapeDtypeStruct((s, D), F32),
                   jax.ShapeDtypeStruct((1, D), F32)),
        in_specs=[tile(XA + MIX), tile(BW), nxt, tile(BW), nxt, tile(BW, 1), tile(BW, 2), _const((4, TS, HALO + TS)),
                  _const((CONF, BW)), _resident((ODD_IN, D)), row, _const((1, D)), row],
        out_specs=(tile(3 * BW), row, _const((1, D))),
        scratch_shapes=[pltpu.VMEM((TS + HALO, BW), F32), pltpu.VMEM((7, SHIFT_ROWS, BW), F32)],
        compiler_params=_cp(("arbitrary",)),
    )(dpc, tmpc, tmpc, tmpd, tmpd, p, p, _band_matrices(TS, True), dww, w_t, x, pre_g, dres)


def _grad_tn(a, b, tm, out=None, rows=None, row0=0, name="grad_tn", rider=None):
    s, m = a.shape
    n = b.shape[1]
    ts = min(2048, s)
    rows = m if rows is None else rows
    assert m % tm == 0 and s % ts == 0
    ns = s // ts
    if row0 % tm == 0:
        out_spec = pl.BlockSpec((tm, n), lambda i, k: (row0 // tm + i, 0))
    else:
        align = 16
        assert row0 % align == 0 and tm % align == 0
        out_spec = pl.BlockSpec((pl.Element(tm), pl.Element(n)),
                                lambda i, k: (pl.multiple_of(row0 + i * tm, align), 0))

    def body(*refs):
        a_ref, b_ref = refs[0], refs[1]
        o_ref, acc = refs[-2], refs[-1]
        k = pl.program_id(1)

        @pl.when(k == 0)
        def _():
            acc[...] = jnp.zeros_like(acc)

        acc[...] += _dot_tn(a_ref[...], b_ref[...])

        @pl.when(k == ns - 1)
        def _():
            o_ref[...] = acc[...].astype(BF16)

    in_specs = [pl.BlockSpec((ts, tm), lambda i, k: (k, i)), pl.BlockSpec((ts, n), lambda i, k: (k, 0))]
    args = [a, b]
    aliases = {}
    if out is not None:
        in_specs.append(pl.BlockSpec(memory_space=pltpu.HBM))
        args.append(out)
        aliases = {2: 0}
    (res,), got = _host_call(
        body, grid=(m // tm, ns), name=name, rider=rider, aliases=aliases,
        out_shape=(jax.ShapeDtypeStruct((rows, n), BF16),), in_specs=in_specs, out_specs=(out_spec,),
        scratch_shapes=[pltpu.VMEM((tm, n), F32)], args=args)
    return res if rider is None else (res, got)


def _place():
    x, y, c = lax.axis_index("x"), lax.axis_index("y"), lax.axis_index("c")
    chips = [(1 - x, y), (x, 1 - y), (1 - x, 1 - y)]
    return x, y, c, chips


def _hbm_specs(n):
    return [pl.BlockSpec(memory_space=pltpu.HBM)] * n


def _row_tile(r):
    for cand in (512, 400, 304, 256, 192, 128, 96, 16):
        if r % cand == 0:
            return cand
    raise ValueError(r)


def _place_shard(shard, place, dtype, name):
    r, cc = shard.shape
    tr = _row_tile(r)
    nt = r // tr

    def body(place_ref, s_ref, o_ref):
        o_ref[...] = s_ref[...].astype(dtype)

    return pl.pallas_call(
        body, name=name, out_shape=jax.ShapeDtypeStruct((N_CHIPS * r, cc), dtype),
        grid_spec=pltpu.PrefetchScalarGridSpec(
            num_scalar_prefetch=1, grid=(nt,),
            in_specs=[pl.BlockSpec((tr, cc), lambda i, pr: (i, 0))],
            out_specs=pl.BlockSpec((tr, cc), lambda i, pr: (pr[1] * nt + i, 0))),
        compiler_params=_cp(("arbitrary",)),
    )(place, shard)


class _GatherRider:
    has_mid = True

    def __init__(self, fulls):
        n = len(fulls)
        self.inputs = list(fulls)
        self.out_shapes = [jax.ShapeDtypeStruct(a.shape, a.dtype) for a in fulls]
        self.aliases = {a: a for a in range(n)}
        self.sems = [pltpu.SemaphoreType.DMA((6 * n,)), pltpu.SemaphoreType.DMA((6 * n,))]
        self.block_rows = [a.shape[0] // N_CHIPS for a in fulls]

    def _ctx(self, outs, sems):
        send_sems, recv_sems = sems
        x, y, c, chips = _place()

        def rows(a, k, half):
            r = self.block_rows[a]
            return outs[a].at[pl.ds(k * r + half * (r // 2), r // 2)]

        def copy(a, j, blk, to):
            return pltpu.make_async_remote_copy(src_ref=blk, dst_ref=blk, send_sem=send_sems.at[a * 6 + j],
                                                recv_sem=recv_sems.at[a * 6 + j], device_id=to, device_id_type=MESH)

        return x, y, c, chips, rows, copy

    def start(self, ins, outs, sems, peers=(0, 1, 2)):
        x, y, c, chips, rows, copy = self._ctx(outs, sems)
        for j in peers:
            for a in range(len(outs)):
                copy(a, j, rows(a, 2 * x + y, c), (*chips[j], c)).start()

    def mid(self, ins, outs, sems, peers=(0, 1, 2)):
        x, y, c, chips, rows, copy = self._ctx(outs, sems)
        for j in peers:
            px, py = chips[j]
            for a in range(len(outs)):
                copy(a, j, rows(a, 2 * px + py, c), (px, py, c)).wait_recv()
                copy(a, 3 + j, rows(a, 2 * px + py, c), (x, y, 1 - c)).start()

    def wait_forwarded(self, outs, sems, peers=(0, 1, 2)):
        x, y, c, chips, rows, copy = self._ctx(outs, sems)
        for j in peers:
            px, py = chips[j]
            for a in range(len(outs)):
                copy(a, 3 + j, rows(a, 2 * px + py, 1 - c), (x, y, 1 - c)).wait_recv()

    def wait_sends(self, outs, sems):
        x, y, c, chips, rows, copy = self._ctx(outs, sems)
        for j, (px, py) in enumerate(chips):
            for a in range(len(outs)):
                copy(a, j, rows(a, 2 * x + y, c), (px, py, c)).wait_send()
                copy(a, 3 + j, rows(a, 2 * px + py, c), (x, y, 1 - c)).wait_send()

    def end(self, ins, outs, sems):
        self.wait_forwarded(outs, sems)
        self.wait_sends(outs, sems)


def _swap_halves(grads, small, name):
    n = len(grads)
    arrs = list(grads) + ([small] if small is not None else [])
    m = len(arrs)

    def body(*refs):
        ins, outs = refs[:m], refs[m:2 * m]
        send_sems, recv_sems = refs[2 * m:]
        x, y, c, _ = _place()
        sibling = (x, y, 1 - c)
        cps = []
        for a in range(m):
            src = ins[a].at[:, 1 - c] if a < n else ins[a]
            cp = pltpu.make_async_remote_copy(src_ref=src, dst_ref=outs[a], send_sem=send_sems.at[a],
                                              recv_sem=recv_sems.at[a], device_id=sibling, device_id_type=MESH)
            cp.start()
            cps.append(cp)
        for cp in cps:
            cp.wait_recv()
        for cp in cps:
            cp.wait_send()

    outs = tuple(jax.ShapeDtypeStruct((g.shape[0],) + g.shape[2:], g.dtype) for g in grads)
    if small is not None:
        outs += (jax.ShapeDtypeStruct(small.shape, small.dtype),)
    return pl.pallas_call(
        body, name=name, out_shape=outs, in_specs=_hbm_specs(m), out_specs=tuple(_hbm_specs(m)),
        scratch_shapes=[pltpu.SemaphoreType.DMA((m,)), pltpu.SemaphoreType.DMA((m,))],
    )(*arrs)


def _pair_sum(g, recv, place, name):
    _, _, h, cc = g.shape
    th = _row_tile(h)

    def body(c_ref, g_ref, r_ref, o_ref):
        o_ref[...] = (g_ref[...].astype(F32) + r_ref[...].astype(F32)).astype(o_ref.dtype)

    return pl.pallas_call(
        body, name=name, out_shape=jax.ShapeDtypeStruct(recv.shape, recv.dtype),
        grid_spec=pltpu.PrefetchScalarGridSpec(
            num_scalar_prefetch=1, grid=(N_CHIPS, h // th),
            in_specs=[pl.BlockSpec((None, None, th, cc), lambda k, r, c_ref: (k, c_ref[0], r, 0)),
                      pl.BlockSpec((None, th, cc), lambda k, r, c_ref: (k, r, 0))],
            out_specs=pl.BlockSpec((None, th, cc), lambda k, r, c_ref: (k, r, 0))),
        compiler_params=_cp(("arbitrary", "arbitrary")),
    )(place, g, recv)


def _finish_reduce(pack, halves):
    rows, cc = pack.shape
    hs = rows // 2
    n = len(halves)

    def body(*refs):
        pack_ref = refs[0]
        out_ref = refs[1 + n]
        big = refs[2 + n:2 + 2 * n]
        sib_ref, parts_ref, send_sems, recv_sems, big_send, big_recv = refs[2 + 2 * n:]
        x, y, c, chips = _place()
        me_k = 2 * x + y
        sibling = (x, y, 1 - c)
        mine = pl.ds(pl.multiple_of(c * hs, hs), hs)
        theirs = pl.ds(pl.multiple_of((1 - c) * hs, hs), hs)
        shared = [pltpu.make_async_remote_copy(src_ref=big[a].at[c], dst_ref=big[a].at[c], send_sem=big_send.at[a],
                                               recv_sem=big_recv.at[a], device_id=sibling, device_id_type=MESH)
                  for a in range(n)]
        for cp in shared:
            cp.start()
        first = pltpu.make_async_remote_copy(src_ref=pack_ref, dst_ref=sib_ref, send_sem=send_sems.at[0],
                                             recv_sem=recv_sems.at[0], device_id=sibling, device_id_type=MESH)
        first.start()
        first.wait()
        parts_ref[me_k] = pack_ref[mine, :] + sib_ref[mine, :]
        cps = [pltpu.make_async_remote_copy(src_ref=parts_ref.at[me_k], dst_ref=parts_ref.at[me_k],
                                            send_sem=send_sems.at[1 + j], recv_sem=recv_sems.at[1 + j],
                                            device_id=(px, py, c), device_id_type=MESH)
               for j, (px, py) in enumerate(chips)]
        for cp in cps:
            cp.start()
        for j, (px, py) in enumerate(chips):
            pltpu.make_async_remote_copy(src_ref=parts_ref.at[2 * px + py], dst_ref=parts_ref.at[2 * px + py],
                                         send_sem=send_sems.at[1 + j], recv_sem=recv_sems.at[1 + j],
                                         device_id=(px, py, c), device_id_type=MESH).wait_recv()
        for cp in cps:
            cp.wait_send()
        out_ref[mine, :] = ((parts_ref[0] + parts_ref[1]) + parts_ref[2]) + parts_ref[3]
        last = pltpu.make_async_remote_copy(src_ref=out_ref.at[mine], dst_ref=out_ref.at[mine],
                                            send_sem=send_sems.at[4], recv_sem=recv_sems.at[4], device_id=sibling,
                                            device_id_type=MESH)
        last.start()
        pltpu.make_async_remote_copy(src_ref=out_ref.at[theirs], dst_ref=out_ref.at[theirs],
                                     send_sem=send_sems.at[4], recv_sem=recv_sems.at[4], device_id=sibling,
                                     device_id_type=MESH).wait_recv()
        last.wait_send()
        for a in range(n):
            pltpu.make_async_remote_copy(src_ref=big[a].at[1 - c], dst_ref=big[a].at[1 - c], send_sem=big_send.at[a],
                                         recv_sem=big_recv.at[a], device_id=sibling,
                                         device_id_type=MESH).wait_recv()
        for cp in shared:
            cp.wait_send()

    vmem = pl.BlockSpec(memory_space=pltpu.VMEM)
    res = pl.pallas_call(
        body, name="finish_reduce",
        out_shape=(jax.ShapeDtypeStruct(pack.shape, pack.dtype),)
        + tuple(jax.ShapeDtypeStruct(g.shape, g.dtype) for g in halves),
        in_specs=[vmem] + _hbm_specs(n), out_specs=(vmem,) + tuple(_hbm_specs(n)),
        input_output_aliases={1 + a: 1 + a for a in range(n)},
        scratch_shapes=[pltpu.VMEM((rows, cc), F32), pltpu.VMEM((N_CHIPS, hs, cc), F32),
                        pltpu.SemaphoreType.DMA((5,)), pltpu.SemaphoreType.DMA((5,)),
                        pltpu.SemaphoreType.DMA((n,)), pltpu.SemaphoreType.DMA((n,))],
        compiler_params=_cp(),
    )(pack, *halves)
    return res[0], tuple(res[1:])


class _ExchangeRider:
    has_mid = False

    def __init__(self, sums):
        self.inputs = list(sums)
        self.out_shapes = [jax.ShapeDtypeStruct((3,) + g.shape[1:], g.dtype) for g in sums]
        m = len(self.inputs)
        self.aliases = {}
        self.sems = [pltpu.SemaphoreType.DMA((3 * m,)), pltpu.SemaphoreType.DMA((3 * m,))]

    def _copies(self, ins, outs, sems):
        send_sems, recv_sems = sems
        _, _, c, chips = _place()
        return [pltpu.make_async_remote_copy(
            src_ref=ins[a].at[2 * px + py], dst_ref=outs[a].at[j], send_sem=send_sems.at[a * 3 + j],
            recv_sem=recv_sems.at[a * 3 + j], device_id=(px, py, c), device_id_type=MESH)
            for j, (px, py) in enumerate(chips) for a in range(len(ins))]

    def start(self, ins, outs, sems):
        for cp in self._copies(ins, outs, sems):
            cp.start()

    def end(self, ins, outs, sems):
        cps = self._copies(ins, outs, sems)
        for cp in cps:
            cp.wait_recv()
        for cp in cps:
            cp.wait_send()


def _chip_sum(own, parts, place, name):
    npart, h, cc = parts.shape
    th = _row_tile(h)

    def body(place_ref, own_ref, p_ref, o_ref):
        acc = own_ref[...].astype(F32) + p_ref[0].astype(F32)
        for k in range(1, npart):
            acc = acc + p_ref[k].astype(F32)
        o_ref[...] = acc

    return pl.pallas_call(
        body, name=name, out_shape=jax.ShapeDtypeStruct((2, h, cc), F32),
        grid_spec=pltpu.PrefetchScalarGridSpec(
            num_scalar_prefetch=1, grid=(h // th,),
            in_specs=[pl.BlockSpec((None, th, cc), lambda r, pr: (pr[1], r, 0)),
                      pl.BlockSpec((npart, th, cc), lambda r, pr: (0, r, 0))],
            out_specs=pl.BlockSpec((None, th, cc), lambda r, pr: (pr[0], r, 0))),
        compiler_params=_cp(("arbitrary",)),
    )(place, own, parts)


def _adamw_math(w, g, m, v):
    m = ADAM_B1 * m + (1.0 - ADAM_B1) * g
    v = ADAM_B2 * v + (1.0 - ADAM_B2) * (g * g)
    m_hat = m / (1.0 - ADAM_B1 ** ADAM_STEP)
    v_hat = v / (1.0 - ADAM_B2 ** ADAM_STEP)
    delta = -ADAM_LR * (m_hat / (jnp.sqrt(v_hat) + ADAM_EPS) + ADAM_WD * w)
    return delta, m, v


def _adamw_big(w, g, m, v, name):
    r, cc = w.shape
    tr = min(_row_tile(r), 256) if r % 256 == 0 else _row_tile(r)

    def body(w_ref, g_ref, m_ref, v_ref, go_ref, d_ref, mo_ref, vo_ref):
        g = g_ref[...]
        d, mm, vv = _adamw_math(w_ref[...], g, m_ref[...], v_ref[...])
        go_ref[...] = g
        d_ref[...] = d
        mo_ref[...] = mm
        vo_ref[...] = vv

    blk = pl.BlockSpec((tr, cc), lambda i: (i, 0))
    sd = jax.ShapeDtypeStruct((r, cc), F32)
    return pl.pallas_call(body, grid=(r // tr,), name=name, out_shape=(sd, sd, sd, sd), in_specs=[blk] * 4,
                          out_specs=(blk, blk, blk, blk), compiler_params=_cp(("arbitrary",)))(w, g, m, v)


def _adamw_small(ws, gs, ms, vs):
    n = len(ws)

    def body(*refs):
        for a in range(n):
            w_ref, g_ref, m_ref, v_ref = refs[4 * a:4 * a + 4]
            d_ref, mo_ref, vo_ref = refs[4 * n + 3 * a:4 * n + 3 * a + 3]
            d, mm, vv = _adamw_math(w_ref[...], g_ref[...], m_ref[...], v_ref[...])
            d_ref[...] = d
            mo_ref[...] = mm
            vo_ref[...] = vv

    args, outs = [], []
    for a in range(n):
        args += [ws[a], gs[a], ms[a], vs[a]]
        outs += [jax.ShapeDtypeStruct(ws[a].shape, F32)] * 3
    res = pl.pallas_call(body, name="adamw_small", out_shape=tuple(outs), compiler_params=_cp())(*args)
    return [res[3 * a:3 * a + 3] for a in range(n)]


def _flat_pack(arrs, rows):
    flat = jnp.concatenate([a.reshape(-1) for a in arrs])
    return jnp.pad(flat, (0, rows * D - flat.shape[0])).reshape(rows, D)


def _flat_unpack(flat, shapes):
    out, off = [], 0
    for shp in shapes:
        size = 1
        for d_ in shp:
            size *= d_
        out.append(flat[off:off + size].reshape(shp))
        off += size
    return out


SMALL_EVEN = ("even_pre_g", "even_a_ln_g", "even_a_ln_b", "even_a_ws", "even_a_bs", "even_b_conv", "even_mem_g",
              "even_post_g")
SMALL_ODD = ("odd_pre_g", "odd_c_wgrp", "odd_c_scale", "odd_d_dw_w", "odd_d_dw_b", "odd_d_ln_g", "odd_d_ln_b",
             "odd_d_pw_b", "odd_mem_g", "odd_post_g")
BIG = ("even_w_in", "even_w_kv", "even_w_out", "odd_w_in", "odd_d_pw_w", "odd_w_kv", "odd_w_out")
WEIGHTS = ("even_pre_g", "even_w_in", "even_a_ln_g", "even_a_ln_b", "even_a_ws", "even_a_bs", "even_b_conv",
           "even_mem_g", "even_w_kv", "even_w_out", "even_post_g", "odd_pre_g", "odd_w_in", "odd_c_wgrp",
           "odd_c_scale", "odd_d_dw_w", "odd_d_dw_b", "odd_d_ln_g", "odd_d_ln_b", "odd_d_pw_w", "odd_d_pw_b",
           "odd_mem_g", "odd_w_kv", "odd_w_out", "odd_post_g")
PACKED = (("even_b_conv", (3, 192)), ("odd_pre_g", (1, 256)), ("odd_c_scale", (1, 192)), ("odd_d_dw_w", (31, 192)),
          ("odd_d_dw_b", (1, 192)), ("odd_d_ln_g", (1, 192)), ("odd_d_ln_b", (1, 192)), ("odd_d_pw_b", (1, 192)),
          ("odd_mem_g", (1, 256)), ("odd_post_g", (1, 256)))
PACK_ROWS = 16
SMALL_ROWS = 256


def _four(g):
    return g.reshape(N_CHIPS, 2, g.shape[0] // (2 * N_CHIPS), g.shape[1])


def _step(x, mem, target, w, place):
    wt = {}
    pack = _flat_pack([w[n][0] for n, _ in PACKED], PACK_ROWS)
    shards = {"even_w_in_t": w["even_w_in"][0].T, "odd_w_in_t": w["odd_w_in"][0].T, "even_w_kv": w["even_w_kv"][0],
              "odd_w_kv": w["odd_w_kv"][0], "even_w_out": w["even_w_out"][0], "odd_w_out": w["odd_w_out"][0],
              "odd_d_pw_w": w["odd_d_pw_w"][0]}
    placed = {n: _place_shard(a, place, BF16, "place_" + n) for n, a in shards.items()}
    placed["pack"] = _place_shard(pack, place, F32, "place_pack")

    h_e = _norm_fwd(x, w["even_pre_g"], "even_norm")
    order, group = _stream_tables(place[1], EVEN_IN)
    p_e, wt["even_w_in_t"], packs, got = _in_fwd_streamed(
        h_e, placed["even_w_in_t"], placed["pack"], [placed["even_w_kv"], placed["even_w_out"]], order, group)
    wt["even_w_kv"], wt["even_w_out"] = got
    packs = packs.reshape(N_CHIPS, PACK_ROWS * D)
    per_chip = [_flat_unpack(packs[k], [shp for _, shp in PACKED]) for k in range(N_CHIPS)]
    for a, (name, _) in enumerate(PACKED):
        wt[name] = jnp.concatenate([per_chip[k][a] for k in range(N_CHIPS)], axis=-1)
    for name in ("even_pre_g", "even_a_ln_g", "even_a_ln_b", "even_mem_g", "even_post_g"):
        wt[name] = w[name]

    tril = jnp.tril(jnp.ones((CH, CH), dtype=bool))
    wcat = jnp.where(tril[None], w["even_a_ws"][0], 0.0).transpose(1, 0, 2).reshape(CH, 4 * CH).astype(BF16)
    bsg = jnp.repeat(w["even_a_bs"][0].T, BW // 4, axis=1)
    hsel = (jnp.arange(BW)[:, None] // (BW // 4) == jnp.arange(128)[None, :]).astype(BF16)
    wg = w["odd_c_wgrp"][0]
    g4 = BW // 4
    wbd = jnp.zeros((BW, BW), F32)
    for g in range(4):
        wbd = lax.dynamic_update_slice(wbd, wg[g], (g * g4, g * g4))
    wbd = wbd.astype(BF16)

    kv_e = _kv_fwd(mem, wt["even_mem_g"], wt["even_w_kv"], "even_kv")
    names = ("odd_w_in_t",)
    (x1, o_e), got = _even_fwd(x, p_e, kv_e, wt["even_a_ln_g"], wt["even_a_ln_b"], wcat, bsg, wt["even_b_conv"],
                               wt["even_w_out"], wt["even_post_g"], rider=_GatherRider([placed[n] for n in names]))
    wt.update(zip(names, got))
    names = ("odd_w_out", "odd_d_pw_w", "odd_w_kv")
    (p_o, h_o), got = _in_fwd(x1, wt["odd_pre_g"], wt["odd_w_in_t"], "odd_in",
                              rider=_GatherRider([placed[n] for n in names]))
    wt.update(zip(names, got))
    kv_o = _kv_fwd(mem, wt["odd_mem_g"], wt["odd_w_kv"], "odd_kv")
    dx2, o_o, cv_o, loss = _odd_fwd(x1, p_o, kv_o, wbd, wt["odd_c_scale"], wt["odd_d_dw_w"], wt["odd_d_dw_b"],
                                    wt["odd_d_ln_g"], wt["odd_d_ln_b"], wt["odd_d_pw_w"], wt["odd_d_pw_b"],
                                    wt["odd_w_out"], wt["odd_post_g"], target)
    (dpc_o, tmpc, tmpd, do_o, y_o, g_post_o, g_cs, g_wbd, g_dww, g_dwb, g_lng_o, g_lnb_o, g_pww, g_pwb,
     dkv_o) = _odd_bwd1(dx2, o_o, cv_o, p_o, kv_o, wbd, wt["odd_c_scale"], wt["odd_d_dw_w"], wt["odd_d_dw_b"],
                        wt["odd_d_ln_g"], wt["odd_d_ln_b"], wt["odd_d_pw_w"], wt["odd_d_pw_b"], wt["odd_w_out"],
                        wt["odd_post_g"])
    dpb_o, dx1, g_pre_o = _odd_bwd2(dpc_o, tmpc, tmpd, p_o, wt["odd_d_dw_w"], wt["odd_w_in_t"], x1,
                                    wt["odd_pre_g"], dx2)
    g_win_o = _grad_tn(dpb_o, h_o, 768, rows=ODD_IN, name="odd_gw_in_b")
    g_win_o = _grad_tn(dpc_o, h_o, 1280, out=g_win_o, rows=ODD_IN, row0=3 * BW, name="odd_gw_in_c")
    g_wout_o = _grad_tn(y_o, do_o, 1024, name="odd_gw_out")
    g_wkv_o, g_memg_o = _kv_bwd(mem, wt["odd_mem_g"], wt["odd_w_kv"], dkv_o, "odd_kv_bwd")
    big_o = [_four(g) for g in (g_win_o, g_pww.astype(BF16), g_wkv_o, g_wout_o)]
    recv_o = _swap_halves(big_o, None, "swap_halves_odd")
    sums_o = [_pair_sum(big_o[a], recv_o[a], place, "pair_sum_odd_%d" % a) for a in range(len(big_o))]
    (dp_e, do_e, y_e, g_post_e, g_lng_e, g_lnb_e, g_wcat, g_bs, g_bconv,
     dkv_e), parts_o = _even_bwd1(dx1, o_e, p_e, kv_e, wt["even_a_ln_g"], wt["even_a_ln_b"], wcat, bsg, hsel,
                                  wt["even_b_conv"], wt["even_w_out"], wt["even_post_g"],
                                  rider=_ExchangeRider(sums_o))
    halves_o = [_chip_sum(sums_o[a], parts_o[a], place, "chip_sum_odd_%d" % a) for a in range(len(big_o))]
    g_wout_e = _grad_tn(y_e, do_e, 1024, name="even_gw_out")
    g_wkv_e, g_memg_e = _kv_bwd(mem, wt["even_mem_g"], wt["even_w_kv"], dkv_e, "even_kv_bwd")
    big_x = [_four(g) for g in (g_wkv_e, g_wout_e)]
    recv_x = _swap_halves(big_x, None, "swap_halves_kv_out")
    sums_x = [_pair_sum(big_x[a], recv_x[a], place, "pair_sum_kv_out_%d" % a) for a in range(len(big_x))]
    g_win_e, parts_x = _grad_tn(dp_e, h_e, 1280, name="even_gw_in", rider=_ExchangeRider(sums_x))
    halves_x = [_chip_sum(sums_x[a], parts_x[a], place, "chip_sum_kv_out_%d" % a) for a in range(len(big_x))]
    big_e = [_four(g_win_e)]
    recv_e = _swap_halves(big_e, None, "swap_halves_even")
    sums_e = [_pair_sum(big_e[0], recv_e[0], place, "pair_sum_even_w_in")]
    (dx0, g_pre_e), parts_e = _even_bwd2(dp_e, wt["even_w_in_t"], x, wt["even_pre_g"], dx1,
                                         rider=_ExchangeRider(sums_e))
    halves_e = [_chip_sum(sums_e[0], parts_e[0], place, "chip_sum_even_w_in")]

    g_aws = jnp.where(tril[None], g_wcat.reshape(CH, 4, CH).transpose(1, 0, 2), 0.0)
    g_wgrp = jnp.stack([lax.dynamic_slice(g_wbd, (g * g4, g * g4), (g4, g4)) for g in range(4)])
    small = {
        "even_pre_g": g_pre_e, "even_a_ln_g": g_lng_e, "even_a_ln_b": g_lnb_e, "even_a_ws": g_aws,
        "even_a_bs": g_bs[:, 0:4].T, "even_b_conv": g_bconv[0:3], "even_mem_g": g_memg_e, "even_post_g": g_post_e,
        "odd_pre_g": g_pre_o, "odd_c_wgrp": g_wgrp, "odd_c_scale": g_cs, "odd_d_dw_w": g_dww.reshape(CONF, 8, BW).sum(axis=1),
        "odd_d_dw_b": g_dwb, "odd_d_ln_g": g_lng_o, "odd_d_ln_b": g_lnb_o, "odd_d_pw_b": g_pwb,
        "odd_mem_g": g_memg_o, "odd_post_g": g_post_o,
    }
    small_names = SMALL_EVEN + SMALL_ODD
    small_pack = _flat_pack([small[n] for n in small_names] + [loss[0, 0].reshape(1)], SMALL_ROWS)
    small_total, full = _finish_reduce(small_pack, halves_e + halves_x + halves_o)
    order = ("even_w_in", "even_w_kv", "even_w_out", "odd_w_in", "odd_d_pw_w", "odd_w_kv", "odd_w_out")
    gbig = {n: full[a].reshape(full[a].shape[1] * 2, full[a].shape[2]) for a, n in enumerate(order)}
    return dx0, gbig, small_total.reshape(-1), [small[n].shape for n in small_names]


def kernel(x, mem, even_pre_g, even_w_in, even_a_ln_g, even_a_ln_b, even_a_ws, even_a_bs, even_b_conv, even_mem_g, even_w_kv, even_w_out, even_post_g, odd_pre_g, odd_w_in, odd_c_wgrp, odd_c_scale, odd_d_dw_w, odd_d_dw_b, odd_d_ln_g, odd_d_ln_b, odd_d_pw_w, odd_d_pw_b, odd_mem_g, odd_w_kv, odd_w_out, odd_post_g, loss_target, m_even_pre_g, m_even_w_in, m_even_a_ln_g, m_even_a_ln_b, m_even_a_ws, m_even_a_bs, m_even_b_conv, m_even_mem_g, m_even_w_kv, m_even_w_out, m_even_post_g, m_odd_pre_g, m_odd_w_in, m_odd_c_wgrp, m_odd_c_scale, m_odd_d_dw_w, m_odd_d_dw_b, m_odd_d_ln_g, m_odd_d_ln_b, m_odd_d_pw_w, m_odd_d_pw_b, m_odd_mem_g, m_odd_w_kv, m_odd_w_out, m_odd_post_g, v_even_pre_g, v_even_w_in, v_even_a_ln_g, v_even_a_ln_b, v_even_a_ws, v_even_a_bs, v_even_b_conv, v_even_mem_g, v_even_w_kv, v_even_w_out, v_even_post_g, v_odd_pre_g, v_odd_w_in, v_odd_c_wgrp, v_odd_c_scale, v_odd_d_dw_w, v_odd_d_dw_b, v_odd_d_ln_g, v_odd_d_ln_b, v_odd_d_pw_w, v_odd_d_pw_b, v_odd_mem_g, v_odd_w_kv, v_odd_w_out, v_odd_post_g):
    given = dict(locals())
    w = {n: given[n] for n in WEIGHTS}
    mom = {n: given["m_" + n] for n in WEIGHTS}
    var = {n: given["v_" + n] for n in WEIGHTS}

    x_, y_, c_ = lax.axis_index("x"), lax.axis_index("y"), lax.axis_index("c")
    chip = 2 * x_ + y_
    place = jnp.stack([c_, chip]).astype(jnp.int32)
    grad_x, gbig, gsmall_flat, small_shapes = _step(x[0], mem[0], loss_target[0], w, place)

    names = SMALL_EVEN + SMALL_ODD
    grads = {}
    unpacked = _flat_unpack(gsmall_flat, small_shapes + [(1,)])
    loss = unpacked[-1][0]
    for n, g in zip(names, unpacked[:-1]):
        shard_shape = w[n].shape[1:]
        if g.shape[-1] != shard_shape[-1]:
            g = lax.dynamic_slice_in_dim(g, chip * shard_shape[-1], shard_shape[-1], axis=g.ndim - 1)
        grads[n] = g.reshape(shard_shape)

    def two_d(a):
        return a.reshape(-1, a.shape[-1])

    upd = {}
    for n in BIG:
        if n.endswith("w_in"):
            res = _adamw_big(w[n][0].T, gbig[n], mom[n][0].T, var[n][0].T, "adamw_" + n)
            res = tuple(r.T for r in res)
        else:
            res = _adamw_big(w[n][0], gbig[n], mom[n][0], var[n][0], "adamw_" + n)
        grads[n], upd[n] = res[0], res[1:]
    res = _adamw_small([two_d(w[n][0]) for n in names], [two_d(grads[n]) for n in names],
                       [two_d(mom[n][0]) for n in names], [two_d(var[n][0]) for n in names])
    for n, r in zip(names, res):
        upd[n] = r

    outs = [loss, grad_x[None]]
    outs += [grads[n].reshape(w[n].shape) for n in WEIGHTS]
    for j in range(3):
        outs += [upd[n][j].reshape(w[n].shape) for n in WEIGHTS]
    return tuple(outs)
```

```python
import functools

import jax
import jax.numpy as jnp
from jax import lax
from jax.experimental import pallas as pl
from jax.experimental.pallas import tpu as pltpu

F32 = jnp.float32
BF16 = jnp.bfloat16
MESH = pl.DeviceIdType.MESH

D = 1024
N_MEM = 256
MIX = 2048
XA = 512
HD = 128
BW = 768
CH = 128
EPS = 1e-6
SCALE = HD ** -0.5
POOL_WINDOWS = (2, 4, 8, 16)
CONF = 31
EVEN_IN = 6400
ODD_IN = 4864
N_CHIPS = 4

ADAM_LR = 0.001
ADAM_B1 = 0.9
ADAM_B2 = 0.999
ADAM_EPS = 1e-08
ADAM_WD = 0.01
ADAM_STEP = 10

TS = 256
HALO = 32
VMEM_LIMIT = 56 * 1024 * 1024


def _cp(sem=None):
    return pltpu.CompilerParams(dimension_semantics=sem, vmem_limit_bytes=VMEM_LIMIT)


def _dot(a, b):
    return jnp.dot(a, b, preferred_element_type=F32)


def _dot_nt(a, b):
    return lax.dot_general(a, b, (((1,), (1,)), ((), ())), preferred_element_type=F32)


def _dot_tn(a, b):
    return lax.dot_general(a, b, (((0,), (0,)), ((), ())), preferred_element_type=F32)


def _sigmoid(x):
    return 1.0 / (1.0 + jnp.exp(-x))


def _resident(shape):
    return pl.BlockSpec(shape, lambda *_: (0,) * len(shape), pipeline_mode=pl.Buffered(1))


def _const(shape):
    return pl.BlockSpec(shape, lambda *_: (0,) * len(shape))


def _kv_fwd(mem, mem_g, wkv, name):
    def body(mem_ref, g_ref, w_ref, kv_ref):
        m = mem_ref[...]
        r = lax.rsqrt(jnp.mean(m * m, axis=-1, keepdims=True) + EPS)
        mn = (m * r * g_ref[...]).astype(BF16)
        kv_ref[...] = _dot(mn, w_ref[...]).astype(BF16)

    return pl.pallas_call(body, out_shape=jax.ShapeDtypeStruct((N_MEM, D), BF16), name=name,
                          compiler_params=_cp())(mem, mem_g, wkv)


def _kv_bwd(mem, mem_g, wkv, dkv, name):
    def body(mem_ref, g_ref, w_ref, dkv_ref, dw_ref, dg_ref):
        m = mem_ref[...]
        r = lax.rsqrt(jnp.mean(m * m, axis=-1, keepdims=True) + EPS)
        mh = m * r
        mn = (mh * g_ref[...]).astype(BF16)
        dkv = dkv_ref[...].astype(BF16)
        dw_ref[...] = _dot_tn(mn, dkv).astype(BF16)
        dmn = _dot_nt(dkv, w_ref[...])
        dg_ref[...] = jnp.sum(dmn * mh, axis=0, keepdims=True)

    return pl.pallas_call(body, out_shape=(jax.ShapeDtypeStruct((D, D), BF16), jax.ShapeDtypeStruct((1, D), F32)),
                          name=name, compiler_params=_cp())(mem, mem_g, wkv, dkv)


def _host_call(body, *, grid, name, out_shape, in_specs, out_specs, args, scratch_shapes=(), aliases=None,
               rider=None):
    sem = ("arbitrary",) * len(grid)
    aliases = dict(aliases or {})
    if rider is None:
        res = pl.pallas_call(body, grid=grid, name=name, out_shape=tuple(out_shape), in_specs=list(in_specs),
                             out_specs=tuple(out_specs), scratch_shapes=list(scratch_shapes),
                             input_output_aliases=aliases, compiler_params=_cp(sem))(*args)
        return tuple(res), ()
    n_in, n_out, n_sc = len(in_specs), len(out_specs), len(scratch_shapes)
    r_in, r_out = len(rider.inputs), len(rider.out_shapes)

    def full_body(*refs):
        host_in = refs[:n_in]
        rid_in = refs[n_in:n_in + r_in]
        host_out = refs[n_in + r_in:n_in + r_in + n_out]
        rid_out = refs[n_in + r_in + n_out:n_in + r_in + n_out + r_out]
        host_sc = refs[n_in + r_in + n_out + r_out:n_in + r_in + n_out + r_out + n_sc]
        sems = refs[n_in + r_in + n_out + r_out + n_sc:]
        first = pl.program_id(0) == 0
        last = pl.program_id(0) == grid[0] - 1
        for ax in range(1, len(grid)):
            first = jnp.logical_and(first, pl.program_id(ax) == 0)
            last = jnp.logical_and(last, pl.program_id(ax) == grid[ax] - 1)

        @pl.when(first)
        def _():
            rider.start(rid_in, rid_out, sems)

        if rider.has_mid:
            @pl.when(last)
            def _():
                rider.mid(rid_in, rid_out, sems)

        body(*host_in, *host_out, *host_sc)

        @pl.when(last)
        def _():
            rider.end(rid_in, rid_out, sems)

    aliases.update({n_in + j: n_out + k for j, k in rider.aliases.items()})
    res = pl.pallas_call(
        full_body, grid=grid, name=name, out_shape=tuple(out_shape) + tuple(rider.out_shapes),
        in_specs=list(in_specs) + _hbm_specs(r_in), out_specs=tuple(out_specs) + tuple(_hbm_specs(r_out)),
        scratch_shapes=list(scratch_shapes) + list(rider.sems), input_output_aliases=aliases,
        compiler_params=_cp(sem),
    )(*args, *rider.inputs)
    return tuple(res[:n_out]), tuple(res[n_out:])


def _in_fwd(x, pre_g, w_t, name, rider=None):
    s, n = x.shape[0], w_t.shape[0]
    tm = min(512, s)
    nc = 256

    def body(x_ref, g_ref, w_ref, p_ref, h_ref):
        xv = x_ref[...]
        r = lax.rsqrt(jnp.mean(xv * xv, axis=-1, keepdims=True) + EPS)
        h = (xv * r * g_ref[...]).astype(BF16)
        h_ref[...] = h
        for j in range(n // nc):
            p_ref[:, j * nc:(j + 1) * nc] = _dot_nt(h, w_ref[j * nc:(j + 1) * nc, :]).astype(BF16)

    return _host_call(
        body, grid=(s // tm,), name=name, rider=rider,
        out_shape=(jax.ShapeDtypeStruct((s, n), BF16), jax.ShapeDtypeStruct((s, D), BF16)),
        in_specs=[pl.BlockSpec((tm, D), lambda i: (i, 0)), _const((1, D)), _resident((n, D))],
        out_specs=(pl.BlockSpec((tm, n), lambda i: (i, 0)), pl.BlockSpec((tm, D), lambda i: (i, 0))),
        args=(x, pre_g, w_t))


NC = 256


def _stream_tables(chip, n):
    nchunk = n // NC
    idx = jnp.arange(nchunk, dtype=jnp.int32)
    rel = jnp.array([0, 2, 1, 3], jnp.int32)
    r = n // N_CHIPS
    grp = jnp.maximum(rel[((idx * NC) // r) ^ chip], rel[((idx * NC + NC - 1) // r) ^ chip])
    order = jnp.argsort(grp * 64 + idx).astype(jnp.int32)
    return order, grp[order]


def _in_fwd_streamed(x, pre_g, w_placed, pack_placed, later, order, group):
    s, n = x.shape[0], w_placed.shape[0]
    nchunk = n // NC
    rider = _GatherRider([w_placed, pack_placed])
    rider2 = _GatherRider(later)
    m = len(later)
    tr = min(256, s)

    def body(*refs):
        order_ref, group_ref, x_ref, g_ref = refs[0:4]
        p_ref, h_ref, w_hbm, pack_hbm = refs[6 + m:10 + m]
        outs2 = refs[10 + m:10 + 2 * m]
        wbuf, wsem, send_sems, recv_sems, send2, recv2 = refs[10 + 2 * m:]
        j = pl.program_id(0)
        outs, sems, sems2 = (w_hbm, pack_hbm), (send_sems, recv_sems), (send2, recv2)
        grp = group_ref[j]
        new_group = jnp.logical_or(j == 0, group_ref[jnp.maximum(j - 1, 0)] != grp)
        slot = j % 2

        def fetch(step, sl):
            rows = pl.ds(pl.multiple_of(order_ref[step] * NC, NC), NC)
            return pltpu.make_async_copy(w_hbm.at[rows], wbuf.at[sl], wsem.at[sl])

        @pl.when(j == 0)
        def _():
            rider.start(None, outs, sems, peers=(0, 1))

            @pl.loop(0, s // tr)
            def _(t):
                rows = pl.ds(pl.multiple_of(t * tr, tr), tr)
                xv = x_ref[rows, :]
                r = lax.rsqrt(jnp.mean(xv * xv, axis=-1, keepdims=True) + EPS)
                h_ref[rows, :] = (xv * r * g_ref[...]).astype(BF16)

        for src in range(3):
            @pl.when(jnp.logical_and(new_group, grp == src + 1))
            def _(src=src):
                if src == 0:
                    rider.start(None, outs, sems, peers=(2,))
                rider.mid(None, outs, sems, peers=(src,))
                rider.wait_forwarded(outs, sems, peers=(src,))
                if src == 1:
                    rider2.start(None, outs2, sems2)

        @pl.when(new_group)
        def _():
            fetch(j, slot).start()

        fetch(j, slot).wait()
        nxt = jnp.minimum(j + 1, nchunk - 1)

        @pl.when(jnp.logical_and(j + 1 < nchunk, group_ref[nxt] == grp))
        def _():
            fetch(nxt, 1 - slot).start()

        p_ref[...] = _dot_nt(h_ref[...], wbuf[slot]).astype(BF16)

        @pl.when(j == nchunk - 1)
        def _():
            rider.wait_sends(outs, sems)
            rider2.mid(None, outs2, sems2)
            rider2.end(None, outs2, sems2)

    hbm = pl.BlockSpec(memory_space=pltpu.HBM)
    aliases = {4: 2, 5: 3}
    aliases.update({6 + a: 4 + a for a in range(m)})
    whole = pl.BlockSpec((s, D), lambda j, o, g: (0, 0), pipeline_mode=pl.Buffered(1))
    res = pl.pallas_call(
        body, name="even_in_streamed",
        out_shape=(jax.ShapeDtypeStruct((s, n), BF16), jax.ShapeDtypeStruct((s, D), BF16),
                   jax.ShapeDtypeStruct(w_placed.shape, w_placed.dtype),
                   jax.ShapeDtypeStruct(pack_placed.shape, pack_placed.dtype))
        + tuple(jax.ShapeDtypeStruct(a.shape, a.dtype) for a in later),
        grid_spec=pltpu.PrefetchScalarGridSpec(
            num_scalar_prefetch=2, grid=(nchunk,),
            in_specs=[whole, pl.BlockSpec((1, D), lambda j, o, g: (0, 0))] + [hbm] * (2 + m),
            out_specs=(pl.BlockSpec((s, NC), lambda j, o, g: (0, o[j])),
                       pl.BlockSpec((s, D), lambda j, o, g: (0, 0))) + (hbm,) * (2 + m),
            scratch_shapes=[pltpu.VMEM((2, NC, D), BF16), pltpu.SemaphoreType.DMA((2,))] + list(rider.sems)
            + list(rider2.sems)),
        input_output_aliases=aliases,
        compiler_params=_cp(("arbitrary",)),
    )(order, group, x, pre_g, w_placed, pack_placed, *later)
    return res[0], res[1], res[2], res[3], tuple(res[4:])


def _xattn_fwd(q, kv_ref):
    outs, probs = [], []
    for h in range(XA // HD):
        qh = q[:, h * HD:(h + 1) * HD]
        kh = kv_ref[:, h * HD:(h + 1) * HD]
        vh = kv_ref[:, XA + h * HD:XA + (h + 1) * HD]
        sc = _dot_nt(qh, kh) * SCALE
        e = jnp.exp(sc - jnp.max(sc, axis=-1, keepdims=True))
        pr = e / jnp.sum(e, axis=-1, keepdims=True)
        outs.append(_dot(pr.astype(BF16), vh))
        probs.append(pr)
    return jnp.concatenate(outs, axis=-1), probs


def _xattn_bwd(dyx, q, probs, kv_ref, dkv_ref):
    dqs = []
    for h in range(XA // HD):
        qh = q[:, h * HD:(h + 1) * HD]
        kh = kv_ref[:, h * HD:(h + 1) * HD]
        vh = kv_ref[:, XA + h * HD:XA + (h + 1) * HD]
        dy = dyx[:, h * HD:(h + 1) * HD].astype(BF16)
        pr = probs[h]
        dp = _dot_nt(dy, vh)
        ds = (pr * (dp - jnp.sum(dp * pr, axis=-1, keepdims=True))).astype(BF16)
        dqs.append(_dot(ds, kh) * SCALE)
        dkv_ref[:, h * HD:(h + 1) * HD] += _dot_tn(ds, qh) * SCALE
        dkv_ref[:, XA + h * HD:XA + (h + 1) * HD] += _dot_tn(pr.astype(BF16), dy)
    return jnp.concatenate(dqs, axis=-1)


def _layer_norm_fwd(v, g, b):
    mu = jnp.mean(v, axis=-1, keepdims=True)
    vc = v - mu
    rstd = lax.rsqrt(jnp.mean(vc * vc, axis=-1, keepdims=True) + EPS)
    vhat = vc * rstd
    return vhat * g + b, vhat, rstd


def _layer_norm_bwd(dy, vhat, rstd, g):
    dvh = dy * g
    return rstd * (dvh - jnp.mean(dvh, axis=-1, keepdims=True) - vhat * jnp.mean(dvh * vhat, axis=-1, keepdims=True))


def _head_masks():
    col = lax.broadcasted_iota(jnp.int32, (1, BW), 1)
    return [(col >= h * (BW // 4)) & (col < (h + 1) * (BW // 4)) for h in range(4)]


def _halo_prev(nblk_per_tile):
    return lambda i: (jnp.maximum(i * nblk_per_tile - 1, 0), 0)


def _row_ids(i, t):
    return i * t + lax.broadcasted_iota(jnp.int32, (t, 1), 0)


def _even_mix(i, p_ref, ph_ref, ln_g, ln_b, wcat_ref, bsg_ref, bconv_ref, wbuf):
    t = p_ref.shape[0]
    u = p_ref[:, 0:BW].astype(F32)
    v = p_ref[:, BW:2 * BW].astype(F32)
    bg = p_ref[:, 2 * BW:3 * BW].astype(F32)
    cg = p_ref[:, 3 * BW:4 * BW].astype(F32)
    xin = p_ref[:, 4 * BW:5 * BW].astype(F32)
    vn, vhat, rstd = _layer_norm_fwd(v, ln_g, ln_b)
    masks = _head_masks()
    sgs, vsts = [], []
    for n in range(t // CH):
        vn_c = vn[n * CH:(n + 1) * CH]
        vst = jnp.concatenate([jnp.where(m, vn_c, 0.0) for m in masks], axis=0).astype(BF16)
        sgs.append(_dot(wcat_ref[...], vst) + bsg_ref[...])
        vsts.append(vst)
    sg = jnp.concatenate(sgs, axis=0)
    ya = u * sg
    w_halo = ph_ref[:, 3 * BW:4 * BW].astype(F32) * ph_ref[:, 4 * BW:5 * BW].astype(F32)
    wbuf[0:HALO, :] = jnp.where(i > 0, w_halo, 0.0)
    wbuf[HALO:HALO + t, :] = cg * xin
    conv = (bconv_ref[0:1, :] * wbuf[pl.ds(HALO - 2, t), :] + bconv_ref[1:2, :] * wbuf[pl.ds(HALO - 1, t), :]
            + bconv_ref[2:3, :] * wbuf[pl.ds(HALO, t), :])
    yb = bg * conv
    return dict(u=u, bg=bg, cg=cg, xin=xin, vhat=vhat, rstd=rstd, sg=sg, vsts=vsts, conv=conv, ya=ya, yb=yb,
                masks=masks)


def _pool_select(vals):
    col = lax.broadcasted_iota(jnp.int32, (1, BW), 1)
    g = BW // 4
    return jnp.where(col < g, vals[0], jnp.where(col < 2 * g, vals[1], jnp.where(col < 3 * g, vals[2], vals[3])))


def _inv_counts(i, t):
    rows = _row_ids(i, t) + 1
    return [1.0 / jnp.minimum(rows, w).astype(F32) for w in POOL_WINDOWS]


def _band_matrices(t, forward):
    j = jnp.arange(t)[:, None]
    r = jnp.arange(HALO + t)[None, :]
    if forward:
        return jnp.stack([(r >= j) & (r < j + w) for w in POOL_WINDOWS]).astype(BF16)
    return jnp.stack([(r <= HALO + j) & (r > HALO + j - w) for w in POOL_WINDOWS]).astype(BF16)


SHIFT_ROWS = HALO + TS - 8


def _shifted_copies(buf, sh):
    for b in range(1, 8):
        sh[b - 1] = buf[pl.ds(b, SHIFT_ROWS), :]


def _rows_at(buf, sh, off, t):
    a, b = divmod(off, 8)
    return buf[pl.ds(8 * a, t), :] if b == 0 else sh[b - 1, pl.ds(8 * a, t), :]


def _tap_sums(d_ref, buf, sh, base, out_ref):
    t = d_ref.shape[0]
    group = 4
    for k0 in range(0, CONF, group):
        taps = list(range(k0, min(k0 + group, CONF)))

        def step(r, accs, taps=taps):
            row = pl.multiple_of(r * 8, 8)
            d = d_ref[pl.ds(row, 8), :]
            new = []
            for acc, k in zip(accs, taps):
                a, b = divmod(base + k, 8)
                src = buf[pl.ds(row + 8 * a, 8), :] if b == 0 else sh[b - 1, pl.ds(row + 8 * a, 8), :]
                new.append(acc + d * src)
            return tuple(new)

        accs = lax.fori_loop(0, t // 8, step, tuple(jnp.zeros((8, BW), F32) for _ in taps), unroll=2)
        for acc, k in zip(accs, taps):
            out_ref[8 * k:8 * k + 8, :] += acc


def _odd_mix(i, p_ref, ph_ref, bands_ref, wbd_ref, cscale, dww_ref, dwb, ln_g, ln_b, pww_ref, pwb, gbuf, gsh,
             cv=None):
    t = p_ref.shape[0]
    zc_bf = p_ref[:, 0:BW]
    zc = zc_bf.astype(F32)
    ga = p_ref[:, BW:2 * BW].astype(F32)
    gb = p_ref[:, 2 * BW:3 * BW].astype(F32)
    zh = ph_ref[:, 0:BW]
    zcat = jnp.concatenate([jnp.where(i > 0, zh, jnp.zeros_like(zh)), zc_bf], axis=0)
    inv = _inv_counts(i, t)
    pooled = _pool_select([_dot(bands_ref[w], zcat) * inv[w] for w in range(len(POOL_WINDOWS))]) - zc
    pooled_bf = pooled.astype(BF16)
    pre = _dot(pooled_bf, wbd_ref[...])
    yc = pre * cscale
    sgb = _sigmoid(gb)
    z = ga * sgb
    gh_a = ph_ref[:, BW:2 * BW].astype(F32)
    gh_b = ph_ref[:, 2 * BW:3 * BW].astype(F32)
    gbuf[0:HALO, :] = jnp.where(i > 0, gh_a * _sigmoid(gh_b), 0.0)
    gbuf[HALO:HALO + t, :] = z
    _shifted_copies(gbuf, gsh)
    if cv is None:
        cv = dwb + dww_ref[CONF - 1:CONF, :] * z
        for k in range(CONF - 1):
            cv = cv + dww_ref[k:k + 1, :] * _rows_at(gbuf, gsh, HALO - (CONF - 1) + k, t)
    zl, zhat, rstd = _layer_norm_fwd(cv, ln_g, ln_b)
    szl = _sigmoid(zl)
    zs = (zl * szl).astype(BF16)
    yd = _dot(zs, pww_ref[...]) + pwb
    return dict(ga=ga, sgb=sgb, pooled_bf=pooled_bf, pre=pre, yc=yc, zhat=zhat, rstd=rstd, zl=zl, szl=szl,
                zs=zs, yd=yd, inv=inv, cv=cv)


def _post_norm(o, post_g):
    r = lax.rsqrt(jnp.mean(o * o, axis=-1, keepdims=True) + EPS)
    return o * r, r


def _gate_out(y_a, y_b, y_x, gate, wout_ref):
    sgt = _sigmoid(gate)
    sgate = gate * sgt
    ys = [(y_a * sgate[:, 0:BW]).astype(BF16), (y_b * sgate[:, BW:2 * BW]).astype(BF16),
          (y_x * sgate[:, 2 * BW:MIX]).astype(BF16)]
    o = (_dot(ys[0], wout_ref[0:BW, :]) + _dot(ys[1], wout_ref[BW:2 * BW, :]) + _dot(ys[2], wout_ref[2 * BW:MIX, :]))
    return o, ys, sgt, sgate


def _tile_specs(s, n):
    nh = TS // HALO
    return pl.BlockSpec((TS, n), lambda i: (i, 0)), pl.BlockSpec((HALO, n), _halo_prev(nh))


def _even_fwd(x, p, kv, ln_g, ln_b, wcat, bsg, bconv, wout, post_g, rider=None):
    s = x.shape[0]

    def body(x_ref, p_ref, ph_ref, kv_ref, lng, lnb, wcat_ref, bsg_ref, bconv_ref, wout_ref, pg, x1_ref, o_ref, wbuf):
        i = pl.program_id(0)
        mx = _even_mix(i, p_ref, ph_ref, lng[...], lnb[...], wcat_ref, bsg_ref, bconv_ref, wbuf)
        yx, _ = _xattn_fwd(p_ref[:, 5 * BW:5 * BW + XA], kv_ref)
        gate = p_ref[:, 5 * BW + XA:EVEN_IN].astype(F32)
        o, _, _, _ = _gate_out(mx["ya"], mx["yb"], yx, gate, wout_ref)
        n, _ = _post_norm(o, pg[...])
        o_ref[...] = o
        x1_ref[...] = x_ref[...] + n * pg[...]

    tile, halo = _tile_specs(s, EVEN_IN)
    row = pl.BlockSpec((TS, D), lambda i: (i, 0))
    return _host_call(
        body, grid=(s // TS,), name="even_fwd", rider=rider,
        out_shape=(jax.ShapeDtypeStruct((s, D), F32), jax.ShapeDtypeStruct((s, D), F32)),
        in_specs=[row, tile, halo, _const((N_MEM, D)), _const((1, BW)), _const((1, BW)), _const((CH, 4 * CH)),
                  _const((CH, BW)), _const((3, BW)), _resident((MIX, D)), _const((1, D))],
        out_specs=(row, row),
        scratch_shapes=[pltpu.VMEM((HALO + TS, BW), F32)],
        args=(x, p, p, kv, ln_g, ln_b, wcat, bsg, bconv, wout, post_g))


def _odd_fwd(x1, p, kv, wbd, cscale, dww, dwb, ln_g, ln_b, pww, pwb, wout, post_g, target):
    s = x1.shape[0]

    def body(x_ref, p_ref, ph_ref, kv_ref, bands_ref, wbd_ref, cs, dww_ref, dwb_ref, lng, lnb, pww_ref, pwb_ref,
             wout_ref, pg, tgt_ref, dx_ref, o_ref, cv_ref, loss_ref, gbuf, gsh):
        i = pl.program_id(0)
        mx = _odd_mix(i, p_ref, ph_ref, bands_ref, wbd_ref, cs[...], dww_ref, dwb_ref[...], lng[...], lnb[...],
                      pww_ref, pwb_ref[...], gbuf, gsh)
        cv_ref[...] = mx["cv"]
        yx, _ = _xattn_fwd(p_ref[:, 3 * BW:3 * BW + XA], kv_ref)
        gate = p_ref[:, 3 * BW + XA:ODD_IN].astype(F32)
        o, _, _, _ = _gate_out(mx["yc"], mx["yd"], yx, gate, wout_ref)
        n, _ = _post_norm(o, pg[...])
        o_ref[...] = o
        err = x_ref[...] + n * pg[...] - tgt_ref[...]
        dx_ref[...] = err * (1.0 / D)

        @pl.when(i == 0)
        def _():
            loss_ref[...] = jnp.zeros_like(loss_ref)

        loss_ref[...] += 0.5 * jnp.sum(jnp.sum(err * err, axis=-1, keepdims=True) * (1.0 / D), axis=0, keepdims=True)

    tile, halo = _tile_specs(s, ODD_IN)
    row = pl.BlockSpec((TS, D), lambda i: (i, 0))
    vec = _const((1, BW))
    return pl.pallas_call(
        body, grid=(s // TS,), name="odd_fwd",
        out_shape=(jax.ShapeDtypeStruct((s, D), F32), jax.ShapeDtypeStruct((s, D), F32),
                   jax.ShapeDtypeStruct((s, BW), F32), jax.ShapeDtypeStruct((8, 128), F32)),
        in_specs=[row, tile, halo, _const((N_MEM, D)), _const((4, TS, HALO + TS)), _const((BW, BW)), vec,
                  _const((CONF, BW)), vec, vec, vec, _const((BW, BW)), vec, _resident((MIX, D)), _const((1, D)), row],
        out_specs=(row, row, pl.BlockSpec((TS, BW), lambda i: (i, 0)), _const((8, 128))),
        scratch_shapes=[pltpu.VMEM((HALO + TS, BW), F32), pltpu.VMEM((7, SHIFT_ROWS, BW), F32)],
        compiler_params=_cp(("arbitrary",)),
    )(x1, p, p, kv, _band_matrices(TS, False), wbd, cscale, dww, dwb, ln_g, ln_b, pww, pwb, wout, post_g, target)


def _acc_init(i, refs):
    @pl.when(i == 0)
    def _():
        for r in refs:
            r[...] = jnp.zeros_like(r)


def _post_norm_bwd(dx, o, pg, dpg_ref):
    n, r = _post_norm(o, pg)
    dpg_ref[...] += jnp.sum(dx * n, axis=0, keepdims=True)
    dn = dx * pg
    return (r * (dn - n * jnp.mean(dn * n, axis=-1, keepdims=True))).astype(BF16)


def _gate_bwd(do, wout_ref, ys_f32, gate, y_ref):
    dy = _dot_nt(do, wout_ref[...])
    sgt = _sigmoid(gate)
    sgate = gate * sgt
    dsilu = sgt * (1.0 + gate * (1.0 - sgt))
    offs = (0, BW, 2 * BW, MIX)
    dys, dgs = [], []
    for j, yv in enumerate(ys_f32):
        a, b = offs[j], offs[j + 1]
        y_ref[:, a:b] = (yv * sgate[:, a:b]).astype(BF16)
        dys.append(dy[:, a:b] * sgate[:, a:b])
        dgs.append(dy[:, a:b] * yv * dsilu[:, a:b])
    return dys, jnp.concatenate(dgs, axis=-1)


NEXT = 16


def _even_bwd1(dx, o, p, kv, ln_g, ln_b, wcat, bsg, hsel, bconv, wout, post_g, rider=None):
    s = dx.shape[0]
    nt = s // TS

    def body(dx_ref, o_ref, p_ref, ph_ref, dxn_ref, on_ref, pn_ref, kv_ref, lng, lnb, wcat_ref, bsg_ref, hsel_ref,
             bconv_ref, wout_ref, pg,
             dp_ref, do_ref, y_ref, dpg_ref, dlng_ref, dlnb_ref, dwcat_ref, dbs_ref, dbconv_ref, dkv_ref, wbuf, dbuf):
        i = pl.program_id(0)
        _acc_init(i, (dpg_ref, dlng_ref, dlnb_ref, dwcat_ref, dbs_ref, dbconv_ref, dkv_ref))
        mx = _even_mix(i, p_ref, ph_ref, lng[...], lnb[...], wcat_ref, bsg_ref, bconv_ref, wbuf)
        q = p_ref[:, 5 * BW:5 * BW + XA]
        yx, probs = _xattn_fwd(q, kv_ref)
        gate = p_ref[:, 5 * BW + XA:EVEN_IN].astype(F32)
        do = _post_norm_bwd(dx_ref[...], o_ref[...], pg[...], dpg_ref)
        do_ref[...] = do
        (dya, dyb, dyx), dgate = _gate_bwd(do, wout_ref, (mx["ya"], mx["yb"], yx), gate, y_ref)
        dp_ref[:, 0:BW] = (dya * mx["sg"]).astype(BF16)
        dsg = (dya * mx["u"]).astype(BF16)
        dvns = []
        for n in range(TS // CH):
            dsg_c = dsg[n * CH:(n + 1) * CH]
            dvst = _dot_tn(wcat_ref[...], dsg_c)
            dvn_c = jnp.where(mx["masks"][0], dvst[0:CH], 0.0)
            for h in range(1, 4):
                dvn_c = dvn_c + jnp.where(mx["masks"][h], dvst[h * CH:(h + 1) * CH], 0.0)
            dvns.append(dvn_c)
            dwcat_ref[...] += _dot_nt(dsg_c, mx["vsts"][n])
            dbs_ref[...] += _dot(dsg_c, hsel_ref[...])
        dvn = jnp.concatenate(dvns, axis=0)
        dlng_ref[...] += jnp.sum(dvn * mx["vhat"], axis=0, keepdims=True)
        dlnb_ref[...] += jnp.sum(dvn, axis=0, keepdims=True)
        dp_ref[:, BW:2 * BW] = _layer_norm_bwd(dvn, mx["vhat"], mx["rstd"], lng[...]).astype(BF16)
        dp_ref[:, 2 * BW:3 * BW] = (dyb * mx["conv"]).astype(BF16)
        dconv = dyb * mx["bg"]
        for k in range(3):
            dbconv_ref[k:k + 1, :] += jnp.sum(dconv * wbuf[pl.ds(HALO - 2 + k, TS), :], axis=0, keepdims=True)
        n_n, r_n = _post_norm(on_ref[...], pg[...])
        dn_n = dxn_ref[...] * pg[...]
        do_n = (r_n * (dn_n - n_n * jnp.mean(dn_n * n_n, axis=-1, keepdims=True))).astype(BF16)
        dy_n = _dot_nt(do_n, wout_ref[BW:2 * BW, :])
        g_n = pn_ref[:, 5 * BW + XA + BW:5 * BW + XA + 2 * BW].astype(F32)
        dconv_n = dy_n * (g_n * _sigmoid(g_n)) * pn_ref[:, 2 * BW:3 * BW].astype(F32)
        dbuf[0:TS, :] = dconv
        dbuf[TS:TS + NEXT, :] = jnp.where(i < nt - 1, dconv_n, 0.0)
        dw = (bconv_ref[2:3, :] * dconv + bconv_ref[1:2, :] * dbuf[pl.ds(1, TS), :]
              + bconv_ref[0:1, :] * dbuf[pl.ds(2, TS), :])
        dp_ref[:, 3 * BW:4 * BW] = (dw * mx["xin"]).astype(BF16)
        dp_ref[:, 4 * BW:5 * BW] = (dw * mx["cg"]).astype(BF16)
        dp_ref[:, 5 * BW:5 * BW + XA] = _xattn_bwd(dyx, q, probs, kv_ref, dkv_ref).astype(BF16)
        dp_ref[:, 5 * BW + XA:EVEN_IN] = dgate.astype(BF16)

    tile, halo = _tile_specs(s, EVEN_IN)
    row = pl.BlockSpec((TS, D), lambda i: (i, 0))
    vec = _const((1, BW))
    nxt = _halo_next(TS // NEXT, s // NEXT)

    def out(n):
        return pl.BlockSpec((TS, n), lambda i: (i, 0))

    return _host_call(
        body, grid=(nt,), name="even_bwd1", rider=rider,
        out_shape=(jax.ShapeDtypeStruct((s, EVEN_IN), BF16), jax.ShapeDtypeStruct((s, D), BF16),
                   jax.ShapeDtypeStruct((s, MIX), BF16),
                   jax.ShapeDtypeStruct((1, D), F32), jax.ShapeDtypeStruct((1, BW), F32),
                   jax.ShapeDtypeStruct((1, BW), F32), jax.ShapeDtypeStruct((CH, 4 * CH), F32),
                   jax.ShapeDtypeStruct((CH, 128), F32), jax.ShapeDtypeStruct((8, BW), F32),
                   jax.ShapeDtypeStruct((N_MEM, D), F32)),
        in_specs=[row, row, tile, halo, pl.BlockSpec((NEXT, D), nxt), pl.BlockSpec((NEXT, D), nxt),
                  pl.BlockSpec((NEXT, EVEN_IN), nxt), _const((N_MEM, D)), vec, vec, _const((CH, 4 * CH)),
                  _const((CH, BW)), _const((BW, 128)), _const((3, BW)), _resident((MIX, D)), _const((1, D))],
        out_specs=(out(EVEN_IN), out(D), out(MIX),
                   _const((1, D)), vec, vec, _const((CH, 4 * CH)), _const((CH, 128)), _const((8, BW)),
                   _const((N_MEM, D))),
        scratch_shapes=[pltpu.VMEM((HALO + TS, BW), F32), pltpu.VMEM((TS + NEXT, BW), F32)],
        args=(dx, o, p, p, dx, o, p, kv, ln_g, ln_b, wcat, bsg, hsel, bconv, wout, post_g))


def _odd_bwd1(dx, o, cv, p, kv, wbd, cscale, dww, dwb, ln_g, ln_b, pww, pwb, wout, post_g):
    s = dx.shape[0]

    def body(dx_ref, o_ref, cv_ref, p_ref, ph_ref, kv_ref, bands_ref, wbd_ref, cs, dww_ref, dwb_ref, lng, lnb,
             pww_ref, pwb_ref, wout_ref, pg,
             dpc_ref, tmpc_ref, tmpd_ref, do_ref, y_ref, dpg_ref, dcs_ref, dwbd_ref, ddww_ref, ddwb_ref, dlng_ref,
             dlnb_ref, dpww_ref, dpwb_ref, dkv_ref, gbuf, gsh, dcv_buf):
        i = pl.program_id(0)
        _acc_init(i, (dpg_ref, dcs_ref, dwbd_ref, ddww_ref, ddwb_ref, dlng_ref, dlnb_ref, dpww_ref, dpwb_ref,
                      dkv_ref))
        mx = _odd_mix(i, p_ref, ph_ref, bands_ref, wbd_ref, cs[...], dww_ref, dwb_ref[...], lng[...], lnb[...],
                      pww_ref, pwb_ref[...], gbuf, gsh, cv=cv_ref[...])
        q = p_ref[:, 3 * BW:3 * BW + XA]
        yx, probs = _xattn_fwd(q, kv_ref)
        gate = p_ref[:, 3 * BW + XA:ODD_IN].astype(F32)
        do = _post_norm_bwd(dx_ref[...], o_ref[...], pg[...], dpg_ref)
        do_ref[...] = do
        (dyc, dyd, dyx), dgate = _gate_bwd(do, wout_ref, (mx["yc"], mx["yd"], yx), gate, y_ref)
        dcs_ref[...] += jnp.sum(dyc * mx["pre"], axis=0, keepdims=True)
        dpre = (dyc * cs[...]).astype(BF16)
        dwbd_ref[...] += _dot_tn(mx["pooled_bf"], dpre)
        dpooled = _dot_nt(dpre, wbd_ref[...])
        tmpc_ref[...] = _pool_select([dpooled * c_ for c_ in mx["inv"]]).astype(BF16)
        dyd_bf = dyd.astype(BF16)
        dpwb_ref[...] += jnp.sum(dyd, axis=0, keepdims=True)
        dpww_ref[...] += _dot_tn(mx["zs"], dyd_bf)
        dzs = _dot_nt(dyd_bf, pww_ref[...])
        zl, szl = mx["zl"], mx["szl"]
        dzl = dzs * (szl * (1.0 + zl * (1.0 - szl)))
        dlng_ref[...] += jnp.sum(dzl * mx["zhat"], axis=0, keepdims=True)
        dlnb_ref[...] += jnp.sum(dzl, axis=0, keepdims=True)
        dcv = _layer_norm_bwd(dzl, mx["zhat"], mx["rstd"], lng[...])
        tmpd_ref[...] = dcv.astype(BF16)
        ddwb_ref[...] += jnp.sum(dcv, axis=0, keepdims=True)
        dcv_buf[...] = dcv
        _tap_sums(dcv_buf, gbuf, gsh, HALO - (CONF - 1), ddww_ref)
        dpc_ref[:, 0:XA] = _xattn_bwd(dyx, q, probs, kv_ref, dkv_ref).astype(BF16)
        dpc_ref[:, XA:XA + MIX] = dgate.astype(BF16)

    tile, halo = _tile_specs(s, ODD_IN)
    row = pl.BlockSpec((TS, D), lambda i: (i, 0))
    vec = _const((1, BW))

    def out(n):
        return pl.BlockSpec((TS, n), lambda i: (i, 0))

    return pl.pallas_call(
        body, grid=(s // TS,), name="odd_bwd1",
        out_shape=(jax.ShapeDtypeStruct((s, XA + MIX), BF16), jax.ShapeDtypeStruct((s, BW), BF16),
                   jax.ShapeDtypeStruct((s, BW), BF16), jax.ShapeDtypeStruct((s, D), BF16),
                   jax.ShapeDtypeStruct((s, MIX), BF16),
                   jax.ShapeDtypeStruct((1, D), F32), jax.ShapeDtypeStruct((1, BW), F32),
                   jax.ShapeDtypeStruct((BW, BW), F32), jax.ShapeDtypeStruct((8 * CONF, BW), F32),
                   jax.ShapeDtypeStruct((1, BW), F32), jax.ShapeDtypeStruct((1, BW), F32),
                   jax.ShapeDtypeStruct((1, BW), F32), jax.ShapeDtypeStruct((BW, BW), F32),
                   jax.ShapeDtypeStruct((1, BW), F32), jax.ShapeDtypeStruct((N_MEM, D), F32)),
        in_specs=[row, row, out(BW), tile, halo, _const((N_MEM, D)), _const((4, TS, HALO + TS)), _const((BW, BW)), vec,
                  _const((CONF, BW)), vec, vec, vec, _const((BW, BW)), vec, _resident((MIX, D)), _const((1, D))],
        out_specs=(out(XA + MIX), out(BW), out(BW), out(D), out(MIX),
                   _const((1, D)), vec, _const((BW, BW)), _const((8 * CONF, BW)), vec, vec, vec, _const((BW, BW)), vec,
                   _const((N_MEM, D))),
        scratch_shapes=[pltpu.VMEM((HALO + TS, BW), F32), pltpu.VMEM((7, SHIFT_ROWS, BW), F32),
                        pltpu.VMEM((TS, BW), F32)],
        compiler_params=_cp(("arbitrary",)),
    )(dx, o, cv, p, p, kv, _band_matrices(TS, False), wbd, cscale, dww, dwb, ln_g, ln_b, pww, pwb, wout, post_g)


def _halo_next(nblk_per_tile, nblk):
    return lambda i: (jnp.minimum((i + 1) * nblk_per_tile, nblk - 1), 0)


def _pre_norm_bwd(dh, x, pre_g, dres, dpre_ref):
    r = lax.rsqrt(jnp.mean(x * x, axis=-1, keepdims=True) + EPS)
    xh = x * r
    dpre_ref[...] += jnp.sum(dh * xh, axis=0, keepdims=True)
    dxh = dh * pre_g
    return dres + r * (dxh - xh * jnp.mean(dxh * xh, axis=-1, keepdims=True))


def _even_bwd2(dp, w_t, x, pre_g, dres, rider=None):
    s = x.shape[0]
    tm = min(512, s)

    def body(dp_ref, w_ref, x_ref, pg, dres_ref, dx_ref, dpre_ref):
        _acc_init(pl.program_id(0), (dpre_ref,))
        dh = _dot(dp_ref[...], w_ref[...])
        dx_ref[...] = _pre_norm_bwd(dh, x_ref[...], pg[...], dres_ref[...], dpre_ref)

    row = pl.BlockSpec((tm, D), lambda i: (i, 0))
    return _host_call(
        body, grid=(s // tm,), name="even_bwd2", rider=rider,
        out_shape=(jax.ShapeDtypeStruct((s, D), F32), jax.ShapeDtypeStruct((1, D), F32)),
        in_specs=[pl.BlockSpec((tm, EVEN_IN), lambda i: (i, 0)), _resident((EVEN_IN, D)), row, _const((1, D)), row],
        out_specs=(row, _const((1, D))),
        args=(dp, w_t, x, pre_g, dres))


def _odd_bwd2(dpc, tmpc, tmpd, p, dww, w_t, x, pre_g, dres):
    s = x.shape[0]
    nt = s // TS

    def body(dpc_ref, tc_ref, tch_ref, td_ref, tdh_ref, ga_ref, gb_ref, bands_ref, dww_ref, w_ref, x_ref, pg,
             dres_ref, dpb_ref, dx_ref, dpre_ref, dbuf, dsh):
        i = pl.program_id(0)
        _acc_init(i, (dpre_ref,))
        more = i < nt - 1
        e_bf = tc_ref[...]
        eh = tch_ref[...]
        ecat = jnp.concatenate([e_bf, jnp.where(more, eh, jnp.zeros_like(eh))], axis=0)
        dbuf[0:TS, :] = td_ref[...].astype(F32)
        dbuf[TS:TS + HALO, :] = jnp.where(more, tdh_ref[...].astype(F32), 0.0)
        sums = [_dot(bands_ref[w], ecat) for w in range(len(POOL_WINDOWS))]
        rows = _row_ids(i, TS) + 1
        cnt = _pool_select([jnp.minimum(rows, w).astype(F32) for w in POOL_WINDOWS])
        dzc = (_pool_select(sums) - e_bf.astype(F32) * cnt).astype(BF16)
        _shifted_copies(dbuf, dsh)
        dz = dww_ref[CONF - 1:CONF, :] * dbuf[pl.ds(0, TS), :]
        for sft in range(1, CONF):
            dz = dz + dww_ref[CONF - 1 - sft:CONF - sft, :] * _rows_at(dbuf, dsh, sft, TS)
        ga = ga_ref[...].astype(F32)
        sgb = _sigmoid(gb_ref[...].astype(F32))
        dga = (dz * sgb).astype(BF16)
        dgb = (dz * ga * sgb * (1.0 - sgb)).astype(BF16)
        dpb_ref[:, 0:BW] = dzc
        dpb_ref[:, BW:2 * BW] = dga
        dpb_ref[:, 2 * BW:3 * BW] = dgb
        dh = (_dot(dzc, w_ref[0:BW, :]) + _dot(dga, w_ref[BW:2 * BW, :]) + _dot(dgb, w_ref[2 * BW:3 * BW, :])
              + _dot(dpc_ref[...], w_ref[3 * BW:ODD_IN, :]))
        dx_ref[...] = _pre_norm_bwd(dh, x_ref[...], pg[...], dres_ref[...], dpre_ref)

    row = pl.BlockSpec((TS, D), lambda i: (i, 0))

    def tile(n, j=0):
        return pl.BlockSpec((TS, n), lambda i: (i, j))

    nxt = pl.BlockSpec((HALO, BW), _halo_next(TS // HALO, s // HALO))
    return pl.pallas_call(
        body, grid=(nt,), name="odd_bwd2",
        out_shape=(jax.ShapeDtypeStruct((s, 3 * BW), BF16), jax.ShapeDtypeStruct((s, D), F32),
                   jax.ShapeDtypeStruct((1, D), F32)),
        in_specs=[tile(XA + MIX), tile(BW), nxt, tile(BW), nxt, tile(BW, 1), tile(BW, 2), _const((4, TS, HALO + TS)),
                  _const((CONF, BW)), _resident((ODD_IN, D)), row, _const((1, D)), row],
        out_specs=(tile(3 * BW), row, _const((1, D))),
        scratch_shapes=[pltpu.VMEM((TS + HALO, BW), F32), pltpu.VMEM((7, SHIFT_ROWS, BW), F32)],
        compiler_params=_cp(("arbitrary",)),
    )(dpc, tmpc, tmpc, tmpd, tmpd, p, p, _band_matrices(TS, True), dww, w_t, x, pre_g, dres)


def _grad_tn(a, b, tm, out=None, rows=None, row0=0, name="grad_tn", rider=None):
    s, m = a.shape
    n = b.shape[1]
    ts = min(2048, s)
    rows = m if rows is None else rows
    assert m % tm == 0 and s % ts == 0
    ns = s // ts
    if row0 % tm == 0:
        out_spec = pl.BlockSpec((tm, n), lambda i, k: (row0 // tm + i, 0))
    else:
        align = 16
        assert row0 % align == 0 and tm % align == 0
        out_spec = pl.BlockSpec((pl.Element(tm), pl.Element(n)),
                                lambda i, k: (pl.multiple_of(row0 + i * tm, align), 0))

    def body(*refs):
        a_ref, b_ref = refs[0], refs[1]
        o_ref, acc = refs[-2], refs[-1]
        k = pl.program_id(1)

        @pl.when(k == 0)
        def _():
            acc[...] = jnp.zeros_like(acc)

        acc[...] += _dot_tn(a_ref[...], b_ref[...])

        @pl.when(k == ns - 1)
        def _():
            o_ref[...] = acc[...].astype(BF16)

    in_specs = [pl.BlockSpec((ts, tm), lambda i, k: (k, i)), pl.BlockSpec((ts, n), lambda i, k: (k, 0))]
    args = [a, b]
    aliases = {}
    if out is not None:
        in_specs.append(pl.BlockSpec(memory_space=pltpu.HBM))
        args.append(out)
        aliases = {2: 0}
    (res,), got = _host_call(
        body, grid=(m // tm, ns), name=name, rider=rider, aliases=aliases,
        out_shape=(jax.ShapeDtypeStruct((rows, n), BF16),), in_specs=in_specs, out_specs=(out_spec,),
        scratch_shapes=[pltpu.VMEM((tm, n), F32)], args=args)
    return res if rider is None else (res, got)


def _place():
    x, y, c = lax.axis_index("x"), lax.axis_index("y"), lax.axis_index("c")
    chips = [(1 - x, y), (x, 1 - y), (1 - x, 1 - y)]
    return x, y, c, chips


def _hbm_specs(n):
    return [pl.BlockSpec(memory_space=pltpu.HBM)] * n


def _row_tile(r):
    for cand in (512, 400, 304, 256, 192, 128, 96, 16):
        if r % cand == 0:
            return cand
    raise ValueError(r)


def _place_shard(shard, place, dtype, name, after=None):
    r, cc = shard.shape
    tr = _row_tile(r)
    nt = r // tr

    def body(place_ref, s_ref, *rest):
        rest[-1][...] = s_ref[...].astype(dtype)

    in_specs = [pl.BlockSpec((tr, cc), lambda i, pr: (i, 0))]
    args = [shard]
    if after is not None:
        in_specs.append(pl.BlockSpec(after.shape, lambda i, pr: (0, 0)))
        args.append(after)
    return pl.pallas_call(
        body, name=name, out_shape=jax.ShapeDtypeStruct((N_CHIPS * r, cc), dtype),
        grid_spec=pltpu.PrefetchScalarGridSpec(
            num_scalar_prefetch=1, grid=(nt,), in_specs=in_specs,
            out_specs=pl.BlockSpec((tr, cc), lambda i, pr: (pr[1] * nt + i, 0))),
        compiler_params=_cp(("arbitrary",)),
    )(place, *args)


class _GatherRider:
    has_mid = True

    def __init__(self, fulls):
        n = len(fulls)
        self.inputs = list(fulls)
        self.out_shapes = [jax.ShapeDtypeStruct(a.shape, a.dtype) for a in fulls]
        self.aliases = {a: a for a in range(n)}
        self.sems = [pltpu.SemaphoreType.DMA((6 * n,)), pltpu.SemaphoreType.DMA((6 * n,))]
        self.block_rows = [a.shape[0] // N_CHIPS for a in fulls]

    def _ctx(self, outs, sems):
        send_sems, recv_sems = sems
        x, y, c, chips = _place()

        def rows(a, k, half):
            r = self.block_rows[a]
            return outs[a].at[pl.ds(k * r + half * (r // 2), r // 2)]

        def copy(a, j, blk, to):
            return pltpu.make_async_remote_copy(src_ref=blk, dst_ref=blk, send_sem=send_sems.at[a * 6 + j],
                                                recv_sem=recv_sems.at[a * 6 + j], device_id=to, device_id_type=MESH)

        return x, y, c, chips, rows, copy

    def start(self, ins, outs, sems, peers=(0, 1, 2)):
        x, y, c, chips, rows, copy = self._ctx(outs, sems)
        for j in peers:
            for a in range(len(outs)):
                copy(a, j, rows(a, 2 * x + y, c), (*chips[j], c)).start()

    def mid(self, ins, outs, sems, peers=(0, 1, 2)):
        x, y, c, chips, rows, copy = self._ctx(outs, sems)
        for j in peers:
            px, py = chips[j]
            for a in range(len(outs)):
                copy(a, j, rows(a, 2 * px + py, c), (px, py, c)).wait_recv()
                copy(a, 3 + j, rows(a, 2 * px + py, c), (x, y, 1 - c)).start()

    def wait_forwarded(self, outs, sems, peers=(0, 1, 2)):
        x, y, c, chips, rows, copy = self._ctx(outs, sems)
        for j in peers:
            px, py = chips[j]
            for a in range(len(outs)):
                copy(a, 3 + j, rows(a, 2 * px + py, 1 - c), (x, y, 1 - c)).wait_recv()

    def wait_sends(self, outs, sems):
        x, y, c, chips, rows, copy = self._ctx(outs, sems)
        for j, (px, py) in enumerate(chips):
            for a in range(len(outs)):
                copy(a, j, rows(a, 2 * x + y, c), (px, py, c)).wait_send()
                copy(a, 3 + j, rows(a, 2 * px + py, c), (x, y, 1 - c)).wait_send()

    def end(self, ins, outs, sems):
        self.wait_forwarded(outs, sems)
        self.wait_sends(outs, sems)


def _swap_halves(grads, small, name):
    n = len(grads)
    arrs = list(grads) + ([small] if small is not None else [])
    m = len(arrs)

    def body(*refs):
        ins, outs = refs[:m], refs[m:2 * m]
        send_sems, recv_sems = refs[2 * m:]
        x, y, c, _ = _place()
        sibling = (x, y, 1 - c)
        cps = []
        for a in range(m):
            src = ins[a].at[:, 1 - c] if a < n else ins[a]
            cp = pltpu.make_async_remote_copy(src_ref=src, dst_ref=outs[a], send_sem=send_sems.at[a],
                                              recv_sem=recv_sems.at[a], device_id=sibling, device_id_type=MESH)
            cp.start()
            cps.append(cp)
        for cp in cps:
            cp.wait_recv()
        for cp in cps:
            cp.wait_send()

    outs = tuple(jax.ShapeDtypeStruct((g.shape[0],) + g.shape[2:], g.dtype) for g in grads)
    if small is not None:
        outs += (jax.ShapeDtypeStruct(small.shape, small.dtype),)
    return pl.pallas_call(
        body, name=name, out_shape=outs, in_specs=_hbm_specs(m), out_specs=tuple(_hbm_specs(m)),
        scratch_shapes=[pltpu.SemaphoreType.DMA((m,)), pltpu.SemaphoreType.DMA((m,))],
    )(*arrs)


def _pair_sum(g, recv, place, name):
    _, _, h, cc = g.shape
    th = _row_tile(h)

    def body(c_ref, g_ref, r_ref, o_ref):
        o_ref[...] = (g_ref[...].astype(F32) + r_ref[...].astype(F32)).astype(o_ref.dtype)

    return pl.pallas_call(
        body, name=name, out_shape=jax.ShapeDtypeStruct(recv.shape, recv.dtype),
        grid_spec=pltpu.PrefetchScalarGridSpec(
            num_scalar_prefetch=1, grid=(N_CHIPS, h // th),
            in_specs=[pl.BlockSpec((None, None, th, cc), lambda k, r, c_ref: (k, c_ref[0], r, 0)),
                      pl.BlockSpec((None, th, cc), lambda k, r, c_ref: (k, r, 0))],
            out_specs=pl.BlockSpec((None, th, cc), lambda k, r, c_ref: (k, r, 0))),
        compiler_params=_cp(("arbitrary", "arbitrary")),
    )(place, g, recv)


def _finish_reduce(pack, halves):
    rows, cc = pack.shape
    hs = rows // 2
    n = len(halves)

    def body(*refs):
        pack_ref = refs[0]
        out_ref = refs[1 + n]
        big = refs[2 + n:2 + 2 * n]
        sib_ref, parts_ref, send_sems, recv_sems, big_send, big_recv = refs[2 + 2 * n:]
        x, y, c, chips = _place()
        me_k = 2 * x + y
        sibling = (x, y, 1 - c)
        mine = pl.ds(pl.multiple_of(c * hs, hs), hs)
        theirs = pl.ds(pl.multiple_of((1 - c) * hs, hs), hs)
        shared = [pltpu.make_async_remote_copy(src_ref=big[a].at[c], dst_ref=big[a].at[c], send_sem=big_send.at[a],
                                               recv_sem=big_recv.at[a], device_id=sibling, device_id_type=MESH)
                  for a in range(n)]
        for cp in shared:
            cp.start()
        first = pltpu.make_async_remote_copy(src_ref=pack_ref, dst_ref=sib_ref, send_sem=send_sems.at[0],
                                             recv_sem=recv_sems.at[0], device_id=sibling, device_id_type=MESH)
        first.start()
        first.wait()
        parts_ref[me_k] = pack_ref[mine, :] + sib_ref[mine, :]
        cps = [pltpu.make_async_remote_copy(src_ref=parts_ref.at[me_k], dst_ref=parts_ref.at[me_k],
                                            send_sem=send_sems.at[1 + j], recv_sem=recv_sems.at[1 + j],
                                            device_id=(px, py, c), device_id_type=MESH)
               for j, (px, py) in enumerate(chips)]
        for cp in cps:
            cp.start()
        for j, (px, py) in enumerate(chips):
            pltpu.make_async_remote_copy(src_ref=parts_ref.at[2 * px + py], dst_ref=parts_ref.at[2 * px + py],
                                         send_sem=send_sems.at[1 + j], recv_sem=recv_sems.at[1 + j],
                                         device_id=(px, py, c), device_id_type=MESH).wait_recv()
        for cp in cps:
            cp.wait_send()
        out_ref[mine, :] = ((parts_ref[0] + parts_ref[1]) + parts_ref[2]) + parts_ref[3]
        last = pltpu.make_async_remote_copy(src_ref=out_ref.at[mine], dst_ref=out_ref.at[mine],
                                            send_sem=send_sems.at[4], recv_sem=recv_sems.at[4], device_id=sibling,
                                            device_id_type=MESH)
        last.start()
        pltpu.make_async_remote_copy(src_ref=out_ref.at[theirs], dst_ref=out_ref.at[theirs],
                                     send_sem=send_sems.at[4], recv_sem=recv_sems.at[4], device_id=sibling,
                                     device_id_type=MESH).wait_recv()
        last.wait_send()
        for a in range(n):
            pltpu.make_async_remote_copy(src_ref=big[a].at[1 - c], dst_ref=big[a].at[1 - c], send_sem=big_send.at[a],
                                         recv_sem=big_recv.at[a], device_id=sibling,
                                         device_id_type=MESH).wait_recv()
        for cp in shared:
            cp.wait_send()

    vmem = pl.BlockSpec(memory_space=pltpu.VMEM)
    res = pl.pallas_call(
        body, name="finish_reduce",
        out_shape=(jax.ShapeDtypeStruct(pack.shape, pack.dtype),)
        + tuple(jax.ShapeDtypeStruct(g.shape, g.dtype) for g in halves),
        in_specs=[vmem] + _hbm_specs(n), out_specs=(vmem,) + tuple(_hbm_specs(n)),
        input_output_aliases={1 + a: 1 + a for a in range(n)},
        scratch_shapes=[pltpu.VMEM((rows, cc), F32), pltpu.VMEM((N_CHIPS, hs, cc), F32),
                        pltpu.SemaphoreType.DMA((5,)), pltpu.SemaphoreType.DMA((5,)),
                        pltpu.SemaphoreType.DMA((n,)), pltpu.SemaphoreType.DMA((n,))],
        compiler_params=_cp(),
    )(pack, *halves)
    return res[0], tuple(res[1:])


class _ExchangeRider:
    has_mid = False

    def __init__(self, sums):
        self.inputs = list(sums)
        self.out_shapes = [jax.ShapeDtypeStruct((3,) + g.shape[1:], g.dtype) for g in sums]
        m = len(self.inputs)
        self.aliases = {}
        self.sems = [pltpu.SemaphoreType.DMA((3 * m,)), pltpu.SemaphoreType.DMA((3 * m,))]

    def _copies(self, ins, outs, sems):
        send_sems, recv_sems = sems
        _, _, c, chips = _place()
        return [pltpu.make_async_remote_copy(
            src_ref=ins[a].at[2 * px + py], dst_ref=outs[a].at[j], send_sem=send_sems.at[a * 3 + j],
            recv_sem=recv_sems.at[a * 3 + j], device_id=(px, py, c), device_id_type=MESH)
            for j, (px, py) in enumerate(chips) for a in range(len(ins))]

    def start(self, ins, outs, sems):
        for cp in self._copies(ins, outs, sems):
            cp.start()

    def end(self, ins, outs, sems):
        cps = self._copies(ins, outs, sems)
        for cp in cps:
            cp.wait_recv()
        for cp in cps:
            cp.wait_send()


def _chip_sum(own, parts, place, name):
    npart, h, cc = parts.shape
    th = _row_tile(h)

    def body(place_ref, own_ref, p_ref, o_ref):
        acc = own_ref[...].astype(F32) + p_ref[0].astype(F32)
        for k in range(1, npart):
            acc = acc + p_ref[k].astype(F32)
        o_ref[...] = acc

    return pl.pallas_call(
        body, name=name, out_shape=jax.ShapeDtypeStruct((2, h, cc), F32),
        grid_spec=pltpu.PrefetchScalarGridSpec(
            num_scalar_prefetch=1, grid=(h // th,),
            in_specs=[pl.BlockSpec((None, th, cc), lambda r, pr: (pr[1], r, 0)),
                      pl.BlockSpec((npart, th, cc), lambda r, pr: (0, r, 0))],
            out_specs=pl.BlockSpec((None, th, cc), lambda r, pr: (pr[0], r, 0))),
        compiler_params=_cp(("arbitrary",)),
    )(place, own, parts)


def _adamw_math(w, g, m, v):
    m = ADAM_B1 * m + (1.0 - ADAM_B1) * g
    v = ADAM_B2 * v + (1.0 - ADAM_B2) * (g * g)
    m_hat = m / (1.0 - ADAM_B1 ** ADAM_STEP)
    v_hat = v / (1.0 - ADAM_B2 ** ADAM_STEP)
    delta = -ADAM_LR * (m_hat / (jnp.sqrt(v_hat) + ADAM_EPS) + ADAM_WD * w)
    return delta, m, v


def _adamw_big(w, g, m, v, name):
    r, cc = w.shape
    tr = min(_row_tile(r), 256) if r % 256 == 0 else _row_tile(r)

    def body(w_ref, g_ref, m_ref, v_ref, go_ref, d_ref, mo_ref, vo_ref):
        g = g_ref[...]
        d, mm, vv = _adamw_math(w_ref[...], g, m_ref[...], v_ref[...])
        go_ref[...] = g
        d_ref[...] = d
        mo_ref[...] = mm
        vo_ref[...] = vv

    blk = pl.BlockSpec((tr, cc), lambda i: (i, 0))
    sd = jax.ShapeDtypeStruct((r, cc), F32)
    return pl.pallas_call(body, grid=(r // tr,), name=name, out_shape=(sd, sd, sd, sd), in_specs=[blk] * 4,
                          out_specs=(blk, blk, blk, blk), compiler_params=_cp(("arbitrary",)))(w, g, m, v)


def _adamw_small(ws, gs, ms, vs):
    n = len(ws)

    def body(*refs):
        for a in range(n):
            w_ref, g_ref, m_ref, v_ref = refs[4 * a:4 * a + 4]
            d_ref, mo_ref, vo_ref = refs[4 * n + 3 * a:4 * n + 3 * a + 3]
            d, mm, vv = _adamw_math(w_ref[...], g_ref[...], m_ref[...], v_ref[...])
            d_ref[...] = d
            mo_ref[...] = mm
            vo_ref[...] = vv

    args, outs = [], []
    for a in range(n):
        args += [ws[a], gs[a], ms[a], vs[a]]
        outs += [jax.ShapeDtypeStruct(ws[a].shape, F32)] * 3
    res = pl.pallas_call(body, name="adamw_small", out_shape=tuple(outs), compiler_params=_cp())(*args)
    return [res[3 * a:3 * a + 3] for a in range(n)]


def _flat_pack(arrs, rows):
    flat = jnp.concatenate([a.reshape(-1) for a in arrs])
    return jnp.pad(flat, (0, rows * D - flat.shape[0])).reshape(rows, D)


def _flat_unpack(flat, shapes):
    out, off = [], 0
    for shp in shapes:
        size = 1
        for d_ in shp:
            size *= d_
        out.append(flat[off:off + size].reshape(shp))
        off += size
    return out


SMALL_EVEN = ("even_pre_g", "even_a_ln_g", "even_a_ln_b", "even_a_ws", "even_a_bs", "even_b_conv", "even_mem_g",
              "even_post_g")
SMALL_ODD = ("odd_pre_g", "odd_c_wgrp", "odd_c_scale", "odd_d_dw_w", "odd_d_dw_b", "odd_d_ln_g", "odd_d_ln_b",
             "odd_d_pw_b", "odd_mem_g", "odd_post_g")
BIG = ("even_w_in", "even_w_kv", "even_w_out", "odd_w_in", "odd_d_pw_w", "odd_w_kv", "odd_w_out")
WEIGHTS = ("even_pre_g", "even_w_in", "even_a_ln_g", "even_a_ln_b", "even_a_ws", "even_a_bs", "even_b_conv",
           "even_mem_g", "even_w_kv", "even_w_out", "even_post_g", "odd_pre_g", "odd_w_in", "odd_c_wgrp",
           "odd_c_scale", "odd_d_dw_w", "odd_d_dw_b", "odd_d_ln_g", "odd_d_ln_b", "odd_d_pw_w", "odd_d_pw_b",
           "odd_mem_g", "odd_w_kv", "odd_w_out", "odd_post_g")
PACKED = (("even_b_conv", (3, 192)), ("odd_pre_g", (1, 256)), ("odd_c_scale", (1, 192)), ("odd_d_dw_w", (31, 192)),
          ("odd_d_dw_b", (1, 192)), ("odd_d_ln_g", (1, 192)), ("odd_d_ln_b", (1, 192)), ("odd_d_pw_b", (1, 192)),
          ("odd_mem_g", (1, 256)), ("odd_post_g", (1, 256)))
PACK_ROWS = 16
SMALL_ROWS = 256


def _four(g):
    return g.reshape(N_CHIPS, 2, g.shape[0] // (2 * N_CHIPS), g.shape[1])


def _step(x, mem, target, w, place):
    wt = {}
    pack = _flat_pack([w[n][0] for n, _ in PACKED], PACK_ROWS)
    shards = {"even_w_in_t": w["even_w_in"][0].T, "odd_w_in_t": w["odd_w_in"][0].T, "even_w_kv": w["even_w_kv"][0],
              "odd_w_kv": w["odd_w_kv"][0], "even_w_out": w["even_w_out"][0], "odd_w_out": w["odd_w_out"][0],
              "odd_d_pw_w": w["odd_d_pw_w"][0]}
    placed = {n: _place_shard(shards[n], place, BF16, "place_" + n) for n in ("even_w_in_t", "even_w_kv", "even_w_out")}
    placed["pack"] = _place_shard(pack, place, F32, "place_pack")

    order, group = _stream_tables(place[1], EVEN_IN)
    p_e, h_e, wt["even_w_in_t"], packs, got = _in_fwd_streamed(
        x, w["even_pre_g"], placed["even_w_in_t"], placed["pack"], [placed["even_w_kv"], placed["even_w_out"]],
        order, group)
    wt["even_w_kv"], wt["even_w_out"] = got
    for n in ("odd_w_in_t", "odd_w_kv", "odd_w_out", "odd_d_pw_w"):
        placed[n] = _place_shard(shards[n], place, BF16, "place_" + n, after=p_e[0:16, 0:128])
    packs = packs.reshape(N_CHIPS, PACK_ROWS * D)
    per_chip = [_flat_unpack(packs[k], [shp for _, shp in PACKED]) for k in range(N_CHIPS)]
    for a, (name, _) in enumerate(PACKED):
        wt[name] = jnp.concatenate([per_chip[k][a] for k in range(N_CHIPS)], axis=-1)
    for name in ("even_pre_g", "even_a_ln_g", "even_a_ln_b", "even_mem_g", "even_post_g"):
        wt[name] = w[name]

    tril = jnp.tril(jnp.ones((CH, CH), dtype=bool))
    wcat = jnp.where(tril[None], w["even_a_ws"][0], 0.0).transpose(1, 0, 2).reshape(CH, 4 * CH).astype(BF16)
    bsg = jnp.repeat(w["even_a_bs"][0].T, BW // 4, axis=1)
    hsel = (jnp.arange(BW)[:, None] // (BW // 4) == jnp.arange(128)[None, :]).astype(BF16)
    wg = w["odd_c_wgrp"][0]
    g4 = BW // 4
    wbd = jnp.zeros((BW, BW), F32)
    for g in range(4):
        wbd = lax.dynamic_update_slice(wbd, wg[g], (g * g4, g * g4))
    wbd = wbd.astype(BF16)

    kv_e = _kv_fwd(mem, wt["even_mem_g"], wt["even_w_kv"], "even_kv")
    names = ("odd_w_in_t",)
    (x1, o_e), got = _even_fwd(x, p_e, kv_e, wt["even_a_ln_g"], wt["even_a_ln_b"], wcat, bsg, wt["even_b_conv"],
                               wt["even_w_out"], wt["even_post_g"], rider=_GatherRider([placed[n] for n in names]))
    wt.update(zip(names, got))
    names = ("odd_w_out", "odd_d_pw_w", "odd_w_kv")
    (p_o, h_o), got = _in_fwd(x1, wt["odd_pre_g"], wt["odd_w_in_t"], "odd_in",
                              rider=_GatherRider([placed[n] for n in names]))
    wt.update(zip(names, got))
    kv_o = _kv_fwd(mem, wt["odd_mem_g"], wt["odd_w_kv"], "odd_kv")
    dx2, o_o, cv_o, loss = _odd_fwd(x1, p_o, kv_o, wbd, wt["odd_c_scale"], wt["odd_d_dw_w"], wt["odd_d_dw_b"],
                                    wt["odd_d_ln_g"], wt["odd_d_ln_b"], wt["odd_d_pw_w"], wt["odd_d_pw_b"],
                                    wt["odd_w_out"], wt["odd_post_g"], target)
    (dpc_o, tmpc, tmpd, do_o, y_o, g_post_o, g_cs, g_wbd, g_dww, g_dwb, g_lng_o, g_lnb_o, g_pww, g_pwb,
     dkv_o) = _odd_bwd1(dx2, o_o, cv_o, p_o, kv_o, wbd, wt["odd_c_scale"], wt["odd_d_dw_w"], wt["odd_d_dw_b"],
                        wt["odd_d_ln_g"], wt["odd_d_ln_b"], wt["odd_d_pw_w"], wt["odd_d_pw_b"], wt["odd_w_out"],
                        wt["odd_post_g"])
    dpb_o, dx1, g_pre_o = _odd_bwd2(dpc_o, tmpc, tmpd, p_o, wt["odd_d_dw_w"], wt["odd_w_in_t"], x1,
                                    wt["odd_pre_g"], dx2)
    g_win_o = _grad_tn(dpb_o, h_o, 768, rows=ODD_IN, name="odd_gw_in_b")
    g_win_o = _grad_tn(dpc_o, h_o, 1280, out=g_win_o, rows=ODD_IN, row0=3 * BW, name="odd_gw_in_c")
    g_wout_o = _grad_tn(y_o, do_o, 1024, name="odd_gw_out")
    g_wkv_o, g_memg_o = _kv_bwd(mem, wt["odd_mem_g"], wt["odd_w_kv"], dkv_o, "odd_kv_bwd")
    big_o = [_four(g) for g in (g_win_o, g_pww.astype(BF16), g_wkv_o, g_wout_o)]
    recv_o = _swap_halves(big_o, None, "swap_halves_odd")
    sums_o = [_pair_sum(big_o[a], recv_o[a], place, "pair_sum_odd_%d" % a) for a in range(len(big_o))]
    (dp_e, do_e, y_e, g_post_e, g_lng_e, g_lnb_e, g_wcat, g_bs, g_bconv,
     dkv_e), parts_o = _even_bwd1(dx1, o_e, p_e, kv_e, wt["even_a_ln_g"], wt["even_a_ln_b"], wcat, bsg, hsel,
                                  wt["even_b_conv"], wt["even_w_out"], wt["even_post_g"],
                                  rider=_ExchangeRider(sums_o))
    halves_o = [_chip_sum(sums_o[a], parts_o[a], place, "chip_sum_odd_%d" % a) for a in range(len(big_o))]
    g_wout_e = _grad_tn(y_e, do_e, 1024, name="even_gw_out")
    g_wkv_e, g_memg_e = _kv_bwd(mem, wt["even_mem_g"], wt["even_w_kv"], dkv_e, "even_kv_bwd")
    big_x = [_four(g) for g in (g_wkv_e, g_wout_e)]
    recv_x = _swap_halves(big_x, None, "swap_halves_kv_out")
    sums_x = [_pair_sum(big_x[a], recv_x[a], place, "pair_sum_kv_out_%d" % a) for a in range(len(big_x))]
    g_win_e, parts_x = _grad_tn(dp_e, h_e, 1280, name="even_gw_in", rider=_ExchangeRider(sums_x))
    halves_x = [_chip_sum(sums_x[a], parts_x[a], place, "chip_sum_kv_out_%d" % a) for a in range(len(big_x))]
    big_e = [_four(g_win_e)]
    recv_e = _swap_halves(big_e, None, "swap_halves_even")
    sums_e = [_pair_sum(big_e[0], recv_e[0], place, "pair_sum_even_w_in")]
    (dx0, g_pre_e), parts_e = _even_bwd2(dp_e, wt["even_w_in_t"], x, wt["even_pre_g"], dx1,
                                         rider=_ExchangeRider(sums_e))
    halves_e = [_chip_sum(sums_e[0], parts_e[0], place, "chip_sum_even_w_in")]

    g_aws = jnp.where(tril[None], g_wcat.reshape(CH, 4, CH).transpose(1, 0, 2), 0.0)
    g_wgrp = jnp.stack([lax.dynamic_slice(g_wbd, (g * g4, g * g4), (g4, g4)) for g in range(4)])
    small = {
        "even_pre_g": g_pre_e, "even_a_ln_g": g_lng_e, "even_a_ln_b": g_lnb_e, "even_a_ws": g_aws,
        "even_a_bs": g_bs[:, 0:4].T, "even_b_conv": g_bconv[0:3], "even_mem_g": g_memg_e, "even_post_g": g_post_e,
        "odd_pre_g": g_pre_o, "odd_c_wgrp": g_wgrp, "odd_c_scale": g_cs, "odd_d_dw_w": g_dww.reshape(CONF, 8, BW).sum(axis=1),
        "odd_d_dw_b": g_dwb, "odd_d_ln_g": g_lng_o, "odd_d_ln_b": g_lnb_o, "odd_d_pw_b": g_pwb,
        "odd_mem_g": g_memg_o, "odd_post_g": g_post_o,
    }
    small_names = SMALL_EVEN + SMALL_ODD
    small_pack = _flat_pack([small[n] for n in small_names] + [loss[0, 0].reshape(1)], SMALL_ROWS)
    small_total, full = _finish_reduce(small_pack, halves_e + halves_x + halves_o)
    order = ("even_w_in", "even_w_kv", "even_w_out", "odd_w_in", "odd_d_pw_w", "odd_w_kv", "odd_w_out")
    gbig = {n: full[a].reshape(full[a].shape[1] * 2, full[a].shape[2]) for a, n in enumerate(order)}
    return dx0, gbig, small_total.reshape(-1), [small[n].shape for n in small_names]


def kernel(x, mem, even_pre_g, even_w_in, even_a_ln_g, even_a_ln_b, even_a_ws, even_a_bs, even_b_conv, even_mem_g, even_w_kv, even_w_out, even_post_g, odd_pre_g, odd_w_in, odd_c_wgrp, odd_c_scale, odd_d_dw_w, odd_d_dw_b, odd_d_ln_g, odd_d_ln_b, odd_d_pw_w, odd_d_pw_b, odd_mem_g, odd_w_kv, odd_w_out, odd_post_g, loss_target, m_even_pre_g, m_even_w_in, m_even_a_ln_g, m_even_a_ln_b, m_even_a_ws, m_even_a_bs, m_even_b_conv, m_even_mem_g, m_even_w_kv, m_even_w_out, m_even_post_g, m_odd_pre_g, m_odd_w_in, m_odd_c_wgrp, m_odd_c_scale, m_odd_d_dw_w, m_odd_d_dw_b, m_odd_d_ln_g, m_odd_d_ln_b, m_odd_d_pw_w, m_odd_d_pw_b, m_odd_mem_g, m_odd_w_kv, m_odd_w_out, m_odd_post_g, v_even_pre_g, v_even_w_in, v_even_a_ln_g, v_even_a_ln_b, v_even_a_ws, v_even_a_bs, v_even_b_conv, v_even_mem_g, v_even_w_kv, v_even_w_out, v_even_post_g, v_odd_pre_g, v_odd_w_in, v_odd_c_wgrp, v_odd_c_scale, v_odd_d_dw_w, v_odd_d_dw_b, v_odd_d_ln_g, v_odd_d_ln_b, v_odd_d_pw_w, v_odd_d_pw_b, v_odd_mem_g, v_odd_w_kv, v_odd_w_out, v_odd_post_g):
    given = dict(locals())
    w = {n: given[n] for n in WEIGHTS}
    mom = {n: given["m_" + n] for n in WEIGHTS}
    var = {n: given["v_" + n] for n in WEIGHTS}

    x_, y_, c_ = lax.axis_index("x"), lax.axis_index("y"), lax.axis_index("c")
    chip = 2 * x_ + y_
    place = jnp.stack([c_, chip]).astype(jnp.int32)
    grad_x, gbig, gsmall_flat, small_shapes = _step(x[0], mem[0], loss_target[0], w, place)

    names = SMALL_EVEN + SMALL_ODD
    grads = {}
    unpacked = _flat_unpack(gsmall_flat, small_shapes + [(1,)])
    loss = unpacked[-1][0]
    for n, g in zip(names, unpacked[:-1]):
        shard_shape = w[n].shape[1:]
        if g.shape[-1] != shard_shape[-1]:
            g = lax.dynamic_slice_in_dim(g, chip * shard_shape[-1], shard_shape[-1], axis=g.ndim - 1)
        grads[n] = g.reshape(shard_shape)

    def two_d(a):
        return a.reshape(-1, a.shape[-1])

    upd = {}
    for n in BIG:
        if n.endswith("w_in"):
            res = _adamw_big(w[n][0].T, gbig[n], mom[n][0].T, var[n][0].T, "adamw_" + n)
            res = tuple(r.T for r in res)
        else:
            res = _adamw_big(w[n][0], gbig[n], mom[n][0], var[n][0], "adamw_" + n)
        grads[n], upd[n] = res[0], res[1:]
    res = _adamw_small([two_d(w[n][0]) for n in names], [two_d(grads[n]) for n in names],
                       [two_d(mom[n][0]) for n in names], [two_d(var[n][0]) for n in names])
    for n, r in zip(names, res):
        upd[n] = r

    outs = [loss, grad_x[None]]
    outs += [grads[n].reshape(w[n].shape) for n in WEIGHTS]
    for j in range(3):
        outs += [upd[n][j].reshape(w[n].shape) for n in WEIGHTS]
    return tuple(outs)
```

```python
import functools

import jax
import jax.numpy as jnp
from jax import lax
from jax.experimental import pallas as pl
from jax.experimental.pallas import tpu as pltpu

F32 = jnp.float32
BF16 = jnp.bfloat16
MESH = pl.DeviceIdType.MESH

D = 1024
N_MEM = 256
MIX = 2048
XA = 512
HD = 128
BW = 768
CH = 128
EPS = 1e-6
SCALE = HD ** -0.5
POOL_WINDOWS = (2, 4, 8, 16)
CONF = 31
EVEN_IN = 6400
ODD_IN = 4864
N_CHIPS = 4

ADAM_LR = 0.001
ADAM_B1 = 0.9
ADAM_B2 = 0.999
ADAM_EPS = 1e-08
ADAM_WD = 0.01
ADAM_STEP = 10

TS = 256
HALO = 32
VMEM_LIMIT = 56 * 1024 * 1024


def _cp(sem=None):
    return pltpu.CompilerParams(dimension_semantics=sem, vmem_limit_bytes=VMEM_LIMIT)


def _dot(a, b):
    return jnp.dot(a, b, preferred_element_type=F32)


def _dot_nt(a, b):
    return lax.dot_general(a, b, (((1,), (1,)), ((), ())), preferred_element_type=F32)


def _dot_tn(a, b):
    return lax.dot_general(a, b, (((0,), (0,)), ((), ())), preferred_element_type=F32)


def _sigmoid(x):
    return 1.0 / (1.0 + jnp.exp(-x))


def _resident(shape):
    return pl.BlockSpec(shape, lambda *_: (0,) * len(shape), pipeline_mode=pl.Buffered(1))


def _const(shape):
    return pl.BlockSpec(shape, lambda *_: (0,) * len(shape))


def _kv_fwd(mem, mem_g, wkv, name):
    def body(mem_ref, g_ref, w_ref, kv_ref):
        m = mem_ref[...]
        r = lax.rsqrt(jnp.mean(m * m, axis=-1, keepdims=True) + EPS)
        mn = (m * r * g_ref[...]).astype(BF16)
        kv_ref[...] = _dot(mn, w_ref[...]).astype(BF16)

    return pl.pallas_call(body, out_shape=jax.ShapeDtypeStruct((N_MEM, D), BF16), name=name,
                          compiler_params=_cp())(mem, mem_g, wkv)


def _kv_bwd(mem, mem_g, wkv, dkv, name):
    def body(mem_ref, g_ref, w_ref, dkv_ref, dw_ref, dg_ref):
        m = mem_ref[...]
        r = lax.rsqrt(jnp.mean(m * m, axis=-1, keepdims=True) + EPS)
        mh = m * r
        mn = (mh * g_ref[...]).astype(BF16)
        dkv = dkv_ref[...].astype(BF16)
        dw_ref[...] = _dot_tn(mn, dkv).astype(BF16)
        dmn = _dot_nt(dkv, w_ref[...])
        dg_ref[...] = jnp.sum(dmn * mh, axis=0, keepdims=True)

    return pl.pallas_call(body, out_shape=(jax.ShapeDtypeStruct((D, D), BF16), jax.ShapeDtypeStruct((1, D), F32)),
                          name=name, compiler_params=_cp())(mem, mem_g, wkv, dkv)


def _host_call(body, *, grid, name, out_shape, in_specs, out_specs, args, scratch_shapes=(), aliases=None,
               rider=None):
    sem = ("arbitrary",) * len(grid)
    aliases = dict(aliases or {})
    if rider is None:
        res = pl.pallas_call(body, grid=grid, name=name, out_shape=tuple(out_shape), in_specs=list(in_specs),
                             out_specs=tuple(out_specs), scratch_shapes=list(scratch_shapes),
                             input_output_aliases=aliases, compiler_params=_cp(sem))(*args)
        return tuple(res), ()
    n_in, n_out, n_sc = len(in_specs), len(out_specs), len(scratch_shapes)
    r_in, r_out = len(rider.inputs), len(rider.out_shapes)

    def full_body(*refs):
        host_in = refs[:n_in]
        rid_in = refs[n_in:n_in + r_in]
        host_out = refs[n_in + r_in:n_in + r_in + n_out]
        rid_out = refs[n_in + r_in + n_out:n_in + r_in + n_out + r_out]
        host_sc = refs[n_in + r_in + n_out + r_out:n_in + r_in + n_out + r_out + n_sc]
        sems = refs[n_in + r_in + n_out + r_out + n_sc:]
        first = pl.program_id(0) == 0
        last = pl.program_id(0) == grid[0] - 1
        for ax in range(1, len(grid)):
            first = jnp.logical_and(first, pl.program_id(ax) == 0)
            last = jnp.logical_and(last, pl.program_id(ax) == grid[ax] - 1)

        @pl.when(first)
        def _():
            rider.start(rid_in, rid_out, sems)

        if rider.has_mid:
            @pl.when(last)
            def _():
                rider.mid(rid_in, rid_out, sems)

        body(*host_in, *host_out, *host_sc)

        @pl.when(last)
        def _():
            rider.end(rid_in, rid_out, sems)

    aliases.update({n_in + j: n_out + k for j, k in rider.aliases.items()})
    res = pl.pallas_call(
        full_body, grid=grid, name=name, out_shape=tuple(out_shape) + tuple(rider.out_shapes),
        in_specs=list(in_specs) + _hbm_specs(r_in), out_specs=tuple(out_specs) + tuple(_hbm_specs(r_out)),
        scratch_shapes=list(scratch_shapes) + list(rider.sems), input_output_aliases=aliases,
        compiler_params=_cp(sem),
    )(*args, *rider.inputs)
    return tuple(res[:n_out]), tuple(res[n_out:])


NC = 256


def _stream_tables(chip, n):
    nchunk = n // NC
    idx = jnp.arange(nchunk, dtype=jnp.int32)
    rel = jnp.array([0, 2, 1, 3], jnp.int32)
    r = n // N_CHIPS
    grp = jnp.maximum(rel[((idx * NC) // r) ^ chip], rel[((idx * NC + NC - 1) // r) ^ chip])
    order = jnp.argsort(grp * 64 + idx).astype(jnp.int32)
    return order, grp[order]


def _in_fwd_streamed(x, pre_g, first, later, order, group, name):
    s, n = x.shape[0], first[0].shape[0]
    nchunk = n // NC
    rider = _GatherRider(first)
    rider2 = _GatherRider(later) if later else None
    a, m = len(first), len(later)
    tr = min(256, s)

    def body(*refs):
        order_ref, group_ref, x_ref, g_ref = refs[0:4]
        p_ref, h_ref = refs[4 + a + m:6 + a + m]
        outs = refs[6 + a + m:6 + 2 * a + m]
        outs2 = refs[6 + 2 * a + m:6 + 2 * a + 2 * m]
        wbuf, wsem, send_sems, recv_sems = refs[6 + 2 * a + 2 * m:10 + 2 * a + 2 * m]
        sems2 = refs[10 + 2 * a + 2 * m:]
        w_hbm = outs[0]
        j = pl.program_id(0)
        sems = (send_sems, recv_sems)
        grp = group_ref[j]
        new_group = jnp.logical_or(j == 0, group_ref[jnp.maximum(j - 1, 0)] != grp)
        slot = j % 2

        def fetch(step, sl):
            rows = pl.ds(pl.multiple_of(order_ref[step] * NC, NC), NC)
            return pltpu.make_async_copy(w_hbm.at[rows], wbuf.at[sl], wsem.at[sl])

        @pl.when(j == 0)
        def _():
            rider.start(None, outs, sems, peers=(0, 1))

            @pl.loop(0, s // tr)
            def _(t):
                rows = pl.ds(pl.multiple_of(t * tr, tr), tr)
                xv = x_ref[rows, :]
                r = lax.rsqrt(jnp.mean(xv * xv, axis=-1, keepdims=True) + EPS)
                h_ref[rows, :] = (xv * r * g_ref[...]).astype(BF16)

        for src in range(3):
            @pl.when(jnp.logical_and(new_group, grp == src + 1))
            def _(src=src):
                if src == 0:
                    rider.start(None, outs, sems, peers=(2,))
                rider.mid(None, outs, sems, peers=(src,))
                rider.wait_forwarded(outs, sems, peers=(src,))
                if src == 1 and rider2 is not None:
                    rider2.start(None, outs2, sems2)

        @pl.when(new_group)
        def _():
            fetch(j, slot).start()

        fetch(j, slot).wait()
        nxt = jnp.minimum(j + 1, nchunk - 1)

        @pl.when(jnp.logical_and(j + 1 < nchunk, group_ref[nxt] == grp))
        def _():
            fetch(nxt, 1 - slot).start()

        p_ref[...] = _dot_nt(h_ref[...], wbuf[slot]).astype(BF16)

        @pl.when(j == nchunk - 1)
        def _():
            rider.wait_sends(outs, sems)
            if rider2 is not None:
                rider2.mid(None, outs2, sems2)
                rider2.end(None, outs2, sems2)

    hbm = pl.BlockSpec(memory_space=pltpu.HBM)
    arrs = list(first) + list(later)
    whole = pl.BlockSpec((s, D), lambda j, o, g: (0, 0), pipeline_mode=pl.Buffered(1))
    res = pl.pallas_call(
        body, name=name,
        out_shape=(jax.ShapeDtypeStruct((s, n), BF16), jax.ShapeDtypeStruct((s, D), BF16))
        + tuple(jax.ShapeDtypeStruct(v.shape, v.dtype) for v in arrs),
        grid_spec=pltpu.PrefetchScalarGridSpec(
            num_scalar_prefetch=2, grid=(nchunk,),
            in_specs=[whole, pl.BlockSpec((1, D), lambda j, o, g: (0, 0))] + [hbm] * (a + m),
            out_specs=(pl.BlockSpec((s, NC), lambda j, o, g: (0, o[j])),
                       pl.BlockSpec((s, D), lambda j, o, g: (0, 0))) + (hbm,) * (a + m),
            scratch_shapes=[pltpu.VMEM((2, NC, D), BF16), pltpu.SemaphoreType.DMA((2,))] + list(rider.sems)
            + (list(rider2.sems) if rider2 is not None else [])),
        input_output_aliases={4 + v: 2 + v for v in range(a + m)},
        compiler_params=_cp(("arbitrary",)),
    )(order, group, x, pre_g, *arrs)
    return res[0], res[1], tuple(res[2:2 + a]), tuple(res[2 + a:])


def _xattn_fwd(q, kv_ref):
    outs, probs = [], []
    for h in range(XA // HD):
        qh = q[:, h * HD:(h + 1) * HD]
        kh = kv_ref[:, h * HD:(h + 1) * HD]
        vh = kv_ref[:, XA + h * HD:XA + (h + 1) * HD]
        sc = _dot_nt(qh, kh) * SCALE
        e = jnp.exp(sc - jnp.max(sc, axis=-1, keepdims=True))
        pr = e / jnp.sum(e, axis=-1, keepdims=True)
        outs.append(_dot(pr.astype(BF16), vh))
        probs.append(pr)
    return jnp.concatenate(outs, axis=-1), probs


def _xattn_bwd(dyx, q, probs, kv_ref, dkv_ref):
    dqs = []
    for h in range(XA // HD):
        qh = q[:, h * HD:(h + 1) * HD]
        kh = kv_ref[:, h * HD:(h + 1) * HD]
        vh = kv_ref[:, XA + h * HD:XA + (h + 1) * HD]
        dy = dyx[:, h * HD:(h + 1) * HD].astype(BF16)
        pr = probs[h]
        dp = _dot_nt(dy, vh)
        ds = (pr * (dp - jnp.sum(dp * pr, axis=-1, keepdims=True))).astype(BF16)
        dqs.append(_dot(ds, kh) * SCALE)
        dkv_ref[:, h * HD:(h + 1) * HD] += _dot_tn(ds, qh) * SCALE
        dkv_ref[:, XA + h * HD:XA + (h + 1) * HD] += _dot_tn(pr.astype(BF16), dy)
    return jnp.concatenate(dqs, axis=-1)


def _layer_norm_fwd(v, g, b):
    mu = jnp.mean(v, axis=-1, keepdims=True)
    vc = v - mu
    rstd = lax.rsqrt(jnp.mean(vc * vc, axis=-1, keepdims=True) + EPS)
    vhat = vc * rstd
    return vhat * g + b, vhat, rstd


def _layer_norm_bwd(dy, vhat, rstd, g):
    dvh = dy * g
    return rstd * (dvh - jnp.mean(dvh, axis=-1, keepdims=True) - vhat * jnp.mean(dvh * vhat, axis=-1, keepdims=True))


def _head_masks():
    col = lax.broadcasted_iota(jnp.int32, (1, BW), 1)
    return [(col >= h * (BW // 4)) & (col < (h + 1) * (BW // 4)) for h in range(4)]


def _halo_prev(nblk_per_tile):
    return lambda i: (jnp.maximum(i * nblk_per_tile - 1, 0), 0)


def _row_ids(i, t):
    return i * t + lax.broadcasted_iota(jnp.int32, (t, 1), 0)


def _even_mix(i, p_ref, ph_ref, ln_g, ln_b, wcat_ref, bsg_ref, bconv_ref, wbuf):
    t = p_ref.shape[0]
    u = p_ref[:, 0:BW].astype(F32)
    v = p_ref[:, BW:2 * BW].astype(F32)
    bg = p_ref[:, 2 * BW:3 * BW].astype(F32)
    cg = p_ref[:, 3 * BW:4 * BW].astype(F32)
    xin = p_ref[:, 4 * BW:5 * BW].astype(F32)
    vn, vhat, rstd = _layer_norm_fwd(v, ln_g, ln_b)
    masks = _head_masks()
    sgs, vsts = [], []
    for n in range(t // CH):
        vn_c = vn[n * CH:(n + 1) * CH]
        vst = jnp.concatenate([jnp.where(m, vn_c, 0.0) for m in masks], axis=0).astype(BF16)
        sgs.append(_dot(wcat_ref[...], vst) + bsg_ref[...])
        vsts.append(vst)
    sg = jnp.concatenate(sgs, axis=0)
    ya = u * sg
    w_halo = ph_ref[:, 3 * BW:4 * BW].astype(F32) * ph_ref[:, 4 * BW:5 * BW].astype(F32)
    wbuf[0:HALO, :] = jnp.where(i > 0, w_halo, 0.0)
    wbuf[HALO:HALO + t, :] = cg * xin
    conv = (bconv_ref[0:1, :] * wbuf[pl.ds(HALO - 2, t), :] + bconv_ref[1:2, :] * wbuf[pl.ds(HALO - 1, t), :]
            + bconv_ref[2:3, :] * wbuf[pl.ds(HALO, t), :])
    yb = bg * conv
    return dict(u=u, bg=bg, cg=cg, xin=xin, vhat=vhat, rstd=rstd, sg=sg, vsts=vsts, conv=conv, ya=ya, yb=yb,
                masks=masks)


def _pool_select(vals):
    col = lax.broadcasted_iota(jnp.int32, (1, BW), 1)
    g = BW // 4
    return jnp.where(col < g, vals[0], jnp.where(col < 2 * g, vals[1], jnp.where(col < 3 * g, vals[2], vals[3])))


def _inv_counts(i, t):
    rows = _row_ids(i, t) + 1
    return [1.0 / jnp.minimum(rows, w).astype(F32) for w in POOL_WINDOWS]


def _band_matrices(t, forward):
    j = jnp.arange(t)[:, None]
    r = jnp.arange(HALO + t)[None, :]
    if forward:
        return jnp.stack([(r >= j) & (r < j + w) for w in POOL_WINDOWS]).astype(BF16)
    return jnp.stack([(r <= HALO + j) & (r > HALO + j - w) for w in POOL_WINDOWS]).astype(BF16)


SHIFT_ROWS = HALO + TS - 8


def _shifted_copies(buf, sh):
    for b in range(1, 8):
        sh[b - 1] = buf[pl.ds(b, SHIFT_ROWS), :]


def _rows_at(buf, sh, off, t):
    a, b = divmod(off, 8)
    return buf[pl.ds(8 * a, t), :] if b == 0 else sh[b - 1, pl.ds(8 * a, t), :]


def _tap_sums(d_ref, buf, sh, base, out_ref):
    t = d_ref.shape[0]
    group = 4
    for k0 in range(0, CONF, group):
        taps = list(range(k0, min(k0 + group, CONF)))

        def step(r, accs, taps=taps):
            row = pl.multiple_of(r * 8, 8)
            d = d_ref[pl.ds(row, 8), :]
            new = []
            for acc, k in zip(accs, taps):
                a, b = divmod(base + k, 8)
                src = buf[pl.ds(row + 8 * a, 8), :] if b == 0 else sh[b - 1, pl.ds(row + 8 * a, 8), :]
                new.append(acc + d * src)
            return tuple(new)

        accs = lax.fori_loop(0, t // 8, step, tuple(jnp.zeros((8, BW), F32) for _ in taps), unroll=2)
        for acc, k in zip(accs, taps):
            out_ref[8 * k:8 * k + 8, :] += acc


def _odd_mix(i, p_ref, ph_ref, bands_ref, wbd_ref, cscale, dww_ref, dwb, ln_g, ln_b, pww_ref, pwb, gbuf, gsh,
             cv=None):
    t = p_ref.shape[0]
    zc_bf = p_ref[:, 0:BW]
    zc = zc_bf.astype(F32)
    ga = p_ref[:, BW:2 * BW].astype(F32)
    gb = p_ref[:, 2 * BW:3 * BW].astype(F32)
    zh = ph_ref[:, 0:BW]
    zcat = jnp.concatenate([jnp.where(i > 0, zh, jnp.zeros_like(zh)), zc_bf], axis=0)
    inv = _inv_counts(i, t)
    pooled = _pool_select([_dot(bands_ref[w], zcat) * inv[w] for w in range(len(POOL_WINDOWS))]) - zc
    pooled_bf = pooled.astype(BF16)
    pre = _dot(pooled_bf, wbd_ref[...])
    yc = pre * cscale
    sgb = _sigmoid(gb)
    z = ga * sgb
    gh_a = ph_ref[:, BW:2 * BW].astype(F32)
    gh_b = ph_ref[:, 2 * BW:3 * BW].astype(F32)
    gbuf[0:HALO, :] = jnp.where(i > 0, gh_a * _sigmoid(gh_b), 0.0)
    gbuf[HALO:HALO + t, :] = z
    _shifted_copies(gbuf, gsh)
    if cv is None:
        cv = dwb + dww_ref[CONF - 1:CONF, :] * z
        for k in range(CONF - 1):
            cv = cv + dww_ref[k:k + 1, :] * _rows_at(gbuf, gsh, HALO - (CONF - 1) + k, t)
    zl, zhat, rstd = _layer_norm_fwd(cv, ln_g, ln_b)
    szl = _sigmoid(zl)
    zs = (zl * szl).astype(BF16)
    yd = _dot(zs, pww_ref[...]) + pwb
    return dict(ga=ga, sgb=sgb, pooled_bf=pooled_bf, pre=pre, yc=yc, zhat=zhat, rstd=rstd, zl=zl, szl=szl,
                zs=zs, yd=yd, inv=inv, cv=cv)


def _post_norm(o, post_g):
    r = lax.rsqrt(jnp.mean(o * o, axis=-1, keepdims=True) + EPS)
    return o * r, r


def _gate_out(y_a, y_b, y_x, gate, wout_ref):
    sgt = _sigmoid(gate)
    sgate = gate * sgt
    ys = [(y_a * sgate[:, 0:BW]).astype(BF16), (y_b * sgate[:, BW:2 * BW]).astype(BF16),
          (y_x * sgate[:, 2 * BW:MIX]).astype(BF16)]
    o = (_dot(ys[0], wout_ref[0:BW, :]) + _dot(ys[1], wout_ref[BW:2 * BW, :]) + _dot(ys[2], wout_ref[2 * BW:MIX, :]))
    return o, ys, sgt, sgate


def _tile_specs(s, n):
    nh = TS // HALO
    return pl.BlockSpec((TS, n), lambda i: (i, 0)), pl.BlockSpec((HALO, n), _halo_prev(nh))


def _even_mix_fwd(p, kv, ln_g, ln_b, wcat, bsg, bconv, rider=None):
    s = p.shape[0]

    def body(p_ref, ph_ref, kv_ref, lng, lnb, wcat_ref, bsg_ref, bconv_ref, y_ref, wbuf):
        i = pl.program_id(0)
        mx = _even_mix(i, p_ref, ph_ref, lng[...], lnb[...], wcat_ref, bsg_ref, bconv_ref, wbuf)
        yx, _ = _xattn_fwd(p_ref[:, 5 * BW:5 * BW + XA], kv_ref)
        gate = p_ref[:, 5 * BW + XA:EVEN_IN].astype(F32)
        sgate = gate * _sigmoid(gate)
        y_ref[:, 0:BW] = (mx["ya"] * sgate[:, 0:BW]).astype(BF16)
        y_ref[:, BW:2 * BW] = (mx["yb"] * sgate[:, BW:2 * BW]).astype(BF16)
        y_ref[:, 2 * BW:MIX] = (yx * sgate[:, 2 * BW:MIX]).astype(BF16)

    tile, halo = _tile_specs(s, EVEN_IN)
    return _host_call(
        body, grid=(s // TS,), name="even_mix_fwd", rider=rider,
        out_shape=(jax.ShapeDtypeStruct((s, MIX), BF16),),
        in_specs=[tile, halo, _const((N_MEM, D)), _const((1, BW)), _const((1, BW)), _const((CH, 4 * CH)),
                  _const((CH, BW)), _const((3, BW))],
        out_specs=(pl.BlockSpec((TS, MIX), lambda i: (i, 0)),),
        scratch_shapes=[pltpu.VMEM((HALO + TS, BW), F32)],
        args=(p, p, kv, ln_g, ln_b, wcat, bsg, bconv))


def _out_fwd(x, y, wout, post_g, name, rider=None):
    s = x.shape[0]
    tm = min(512, s)

    def body(x_ref, y_ref, wout_ref, pg, x1_ref, o_ref):
        o = _dot(y_ref[...], wout_ref[...])
        n, _ = _post_norm(o, pg[...])
        o_ref[...] = o
        x1_ref[...] = x_ref[...] + n * pg[...]

    row = pl.BlockSpec((tm, D), lambda i: (i, 0))
    return _host_call(
        body, grid=(s // tm,), name=name, rider=rider,
        out_shape=(jax.ShapeDtypeStruct((s, D), F32), jax.ShapeDtypeStruct((s, D), F32)),
        in_specs=[row, pl.BlockSpec((tm, MIX), lambda i: (i, 0)), _resident((MIX, D)), _const((1, D))],
        out_specs=(row, row), args=(x, y, wout, post_g))


def _odd_fwd(x1, p, kv, wbd, cscale, dww, dwb, ln_g, ln_b, pww, pwb, wout, post_g, target):
    s = x1.shape[0]

    def body(x_ref, p_ref, ph_ref, kv_ref, bands_ref, wbd_ref, cs, dww_ref, dwb_ref, lng, lnb, pww_ref, pwb_ref,
             wout_ref, pg, tgt_ref, dx_ref, o_ref, cv_ref, loss_ref, gbuf, gsh):
        i = pl.program_id(0)
        mx = _odd_mix(i, p_ref, ph_ref, bands_ref, wbd_ref, cs[...], dww_ref, dwb_ref[...], lng[...], lnb[...],
                      pww_ref, pwb_ref[...], gbuf, gsh)
        cv_ref[...] = mx["cv"]
        yx, _ = _xattn_fwd(p_ref[:, 3 * BW:3 * BW + XA], kv_ref)
        gate = p_ref[:, 3 * BW + XA:ODD_IN].astype(F32)
        o, _, _, _ = _gate_out(mx["yc"], mx["yd"], yx, gate, wout_ref)
        n, _ = _post_norm(o, pg[...])
        o_ref[...] = o
        err = x_ref[...] + n * pg[...] - tgt_ref[...]
        dx_ref[...] = err * (1.0 / D)

        @pl.when(i == 0)
        def _():
            loss_ref[...] = jnp.zeros_like(loss_ref)

        loss_ref[...] += 0.5 * jnp.sum(jnp.sum(err * err, axis=-1, keepdims=True) * (1.0 / D), axis=0, keepdims=True)

    tile, halo = _tile_specs(s, ODD_IN)
    row = pl.BlockSpec((TS, D), lambda i: (i, 0))
    vec = _const((1, BW))
    return pl.pallas_call(
        body, grid=(s // TS,), name="odd_fwd",
        out_shape=(jax.ShapeDtypeStruct((s, D), F32), jax.ShapeDtypeStruct((s, D), F32),
                   jax.ShapeDtypeStruct((s, BW), F32), jax.ShapeDtypeStruct((8, 128), F32)),
        in_specs=[row, tile, halo, _const((N_MEM, D)), _const((4, TS, HALO + TS)), _const((BW, BW)), vec,
                  _const((CONF, BW)), vec, vec, vec, _const((BW, BW)), vec, _resident((MIX, D)), _const((1, D)), row],
        out_specs=(row, row, pl.BlockSpec((TS, BW), lambda i: (i, 0)), _const((8, 128))),
        scratch_shapes=[pltpu.VMEM((HALO + TS, BW), F32), pltpu.VMEM((7, SHIFT_ROWS, BW), F32)],
        compiler_params=_cp(("arbitrary",)),
    )(x1, p, p, kv, _band_matrices(TS, False), wbd, cscale, dww, dwb, ln_g, ln_b, pww, pwb, wout, post_g, target)


def _acc_init(i, refs):
    @pl.when(i == 0)
    def _():
        for r in refs:
            r[...] = jnp.zeros_like(r)


def _post_norm_bwd(dx, o, pg, dpg_ref):
    n, r = _post_norm(o, pg)
    dpg_ref[...] += jnp.sum(dx * n, axis=0, keepdims=True)
    dn = dx * pg
    return (r * (dn - n * jnp.mean(dn * n, axis=-1, keepdims=True))).astype(BF16)


def _gate_bwd(do, wout_ref, ys_f32, gate, y_ref):
    dy = _dot_nt(do, wout_ref[...])
    sgt = _sigmoid(gate)
    sgate = gate * sgt
    dsilu = sgt * (1.0 + gate * (1.0 - sgt))
    offs = (0, BW, 2 * BW, MIX)
    dys, dgs = [], []
    for j, yv in enumerate(ys_f32):
        a, b = offs[j], offs[j + 1]
        if y_ref is not None:
            y_ref[:, a:b] = (yv * sgate[:, a:b]).astype(BF16)
        dys.append(dy[:, a:b] * sgate[:, a:b])
        dgs.append(dy[:, a:b] * yv * dsilu[:, a:b])
    return dys, jnp.concatenate(dgs, axis=-1)


NEXT = 16


def _even_bwd1(dx, o, p, kv, ln_g, ln_b, wcat, bsg, hsel, bconv, wout, post_g, rider=None):
    s = dx.shape[0]
    nt = s // TS

    def body(dx_ref, o_ref, p_ref, ph_ref, dxn_ref, on_ref, pn_ref, kv_ref, lng, lnb, wcat_ref, bsg_ref, hsel_ref,
             bconv_ref, wout_ref, pg,
             dp_ref, do_ref, dpg_ref, dlng_ref, dlnb_ref, dwcat_ref, dbs_ref, dbconv_ref, dkv_ref, wbuf, dbuf):
        i = pl.program_id(0)
        _acc_init(i, (dpg_ref, dlng_ref, dlnb_ref, dwcat_ref, dbs_ref, dbconv_ref, dkv_ref))
        mx = _even_mix(i, p_ref, ph_ref, lng[...], lnb[...], wcat_ref, bsg_ref, bconv_ref, wbuf)
        q = p_ref[:, 5 * BW:5 * BW + XA]
        yx, probs = _xattn_fwd(q, kv_ref)
        gate = p_ref[:, 5 * BW + XA:EVEN_IN].astype(F32)
        do = _post_norm_bwd(dx_ref[...], o_ref[...], pg[...], dpg_ref)
        do_ref[...] = do
        (dya, dyb, dyx), dgate = _gate_bwd(do, wout_ref, (mx["ya"], mx["yb"], yx), gate, None)
        dp_ref[:, 0:BW] = (dya * mx["sg"]).astype(BF16)
        dsg = (dya * mx["u"]).astype(BF16)
        dvns = []
        for n in range(TS // CH):
            dsg_c = dsg[n * CH:(n + 1) * CH]
            dvst = _dot_tn(wcat_ref[...], dsg_c)
            dvn_c = jnp.where(mx["masks"][0], dvst[0:CH], 0.0)
            for h in range(1, 4):
                dvn_c = dvn_c + jnp.where(mx["masks"][h], dvst[h * CH:(h + 1) * CH], 0.0)
            dvns.append(dvn_c)
            dwcat_ref[...] += _dot_nt(dsg_c, mx["vsts"][n])
            dbs_ref[...] += _dot(dsg_c, hsel_ref[...])
        dvn = jnp.concatenate(dvns, axis=0)
        dlng_ref[...] += jnp.sum(dvn * mx["vhat"], axis=0, keepdims=True)
        dlnb_ref[...] += jnp.sum(dvn, axis=0, keepdims=True)
        dp_ref[:, BW:2 * BW] = _layer_norm_bwd(dvn, mx["vhat"], mx["rstd"], lng[...]).astype(BF16)
        dp_ref[:, 2 * BW:3 * BW] = (dyb * mx["conv"]).astype(BF16)
        dconv = dyb * mx["bg"]
        for k in range(3):
            dbconv_ref[k:k + 1, :] += jnp.sum(dconv * wbuf[pl.ds(HALO - 2 + k, TS), :], axis=0, keepdims=True)
        n_n, r_n = _post_norm(on_ref[...], pg[...])
        dn_n = dxn_ref[...] * pg[...]
        do_n = (r_n * (dn_n - n_n * jnp.mean(dn_n * n_n, axis=-1, keepdims=True))).astype(BF16)
        dy_n = _dot_nt(do_n, wout_ref[BW:2 * BW, :])
        g_n = pn_ref[:, 5 * BW + XA + BW:5 * BW + XA + 2 * BW].astype(F32)
        dconv_n = dy_n * (g_n * _sigmoid(g_n)) * pn_ref[:, 2 * BW:3 * BW].astype(F32)
        dbuf[0:TS, :] = dconv
        dbuf[TS:TS + NEXT, :] = jnp.where(i < nt - 1, dconv_n, 0.0)
        dw = (bconv_ref[2:3, :] * dconv + bconv_ref[1:2, :] * dbuf[pl.ds(1, TS), :]
              + bconv_ref[0:1, :] * dbuf[pl.ds(2, TS), :])
        dp_ref[:, 3 * BW:4 * BW] = (dw * mx["xin"]).astype(BF16)
        dp_ref[:, 4 * BW:5 * BW] = (dw * mx["cg"]).astype(BF16)
        dp_ref[:, 5 * BW:5 * BW + XA] = _xattn_bwd(dyx, q, probs, kv_ref, dkv_ref).astype(BF16)
        dp_ref[:, 5 * BW + XA:EVEN_IN] = dgate.astype(BF16)

    tile, halo = _tile_specs(s, EVEN_IN)
    row = pl.BlockSpec((TS, D), lambda i: (i, 0))
    vec = _const((1, BW))
    nxt = _halo_next(TS // NEXT, s // NEXT)

    def out(n):
        return pl.BlockSpec((TS, n), lambda i: (i, 0))

    return _host_call(
        body, grid=(nt,), name="even_bwd1", rider=rider,
        out_shape=(jax.ShapeDtypeStruct((s, EVEN_IN), BF16), jax.ShapeDtypeStruct((s, D), BF16),
                   jax.ShapeDtypeStruct((1, D), F32), jax.ShapeDtypeStruct((1, BW), F32),
                   jax.ShapeDtypeStruct((1, BW), F32), jax.ShapeDtypeStruct((CH, 4 * CH), F32),
                   jax.ShapeDtypeStruct((CH, 128), F32), jax.ShapeDtypeStruct((8, BW), F32),
                   jax.ShapeDtypeStruct((N_MEM, D), F32)),
        in_specs=[row, row, tile, halo, pl.BlockSpec((NEXT, D), nxt), pl.BlockSpec((NEXT, D), nxt),
                  pl.BlockSpec((NEXT, EVEN_IN), nxt), _const((N_MEM, D)), vec, vec, _const((CH, 4 * CH)),
                  _const((CH, BW)), _const((BW, 128)), _const((3, BW)), _resident((MIX, D)), _const((1, D))],
        out_specs=(out(EVEN_IN), out(D),
                   _const((1, D)), vec, vec, _const((CH, 4 * CH)), _const((CH, 128)), _const((8, BW)),
                   _const((N_MEM, D))),
        scratch_shapes=[pltpu.VMEM((HALO + TS, BW), F32), pltpu.VMEM((TS + NEXT, BW), F32)],
        args=(dx, o, p, p, dx, o, p, kv, ln_g, ln_b, wcat, bsg, hsel, bconv, wout, post_g))


def _odd_bwd1(dx, o, cv, p, kv, wbd, cscale, dww, dwb, ln_g, ln_b, pww, pwb, wout, post_g):
    s = dx.shape[0]

    def body(dx_ref, o_ref, cv_ref, p_ref, ph_ref, kv_ref, bands_ref, wbd_ref, cs, dww_ref, dwb_ref, lng, lnb,
             pww_ref, pwb_ref, wout_ref, pg,
             dpc_ref, tmpc_ref, tmpd_ref, do_ref, y_ref, dpg_ref, dcs_ref, dwbd_ref, ddww_ref, ddwb_ref, dlng_ref,
             dlnb_ref, dpww_ref, dpwb_ref, dkv_ref, gbuf, gsh, dcv_buf):
        i = pl.program_id(0)
        _acc_init(i, (dpg_ref, dcs_ref, dwbd_ref, ddww_ref, ddwb_ref, dlng_ref, dlnb_ref, dpww_ref, dpwb_ref,
                      dkv_ref))
        mx = _odd_mix(i, p_ref, ph_ref, bands_ref, wbd_ref, cs[...], dww_ref, dwb_ref[...], lng[...], lnb[...],
                      pww_ref, pwb_ref[...], gbuf, gsh, cv=cv_ref[...])
        q = p_ref[:, 3 * BW:3 * BW + XA]
        yx, probs = _xattn_fwd(q, kv_ref)
        gate = p_ref[:, 3 * BW + XA:ODD_IN].astype(F32)
        do = _post_norm_bwd(dx_ref[...], o_ref[...], pg[...], dpg_ref)
        do_ref[...] = do
        (dyc, dyd, dyx), dgate = _gate_bwd(do, wout_ref, (mx["yc"], mx["yd"], yx), gate, y_ref)
        dcs_ref[...] += jnp.sum(dyc * mx["pre"], axis=0, keepdims=True)
        dpre = (dyc * cs[...]).astype(BF16)
        dwbd_ref[...] += _dot_tn(mx["pooled_bf"], dpre)
        dpooled = _dot_nt(dpre, wbd_ref[...])
        tmpc_ref[...] = _pool_select([dpooled * c_ for c_ in mx["inv"]]).astype(BF16)
        dyd_bf = dyd.astype(BF16)
        dpwb_ref[...] += jnp.sum(dyd, axis=0, keepdims=True)
        dpww_ref[...] += _dot_tn(mx["zs"], dyd_bf)
        dzs = _dot_nt(dyd_bf, pww_ref[...])
        zl, szl = mx["zl"], mx["szl"]
        dzl = dzs * (szl * (1.0 + zl * (1.0 - szl)))
        dlng_ref[...] += jnp.sum(dzl * mx["zhat"], axis=0, keepdims=True)
        dlnb_ref[...] += jnp.sum(dzl, axis=0, keepdims=True)
        dcv = _layer_norm_bwd(dzl, mx["zhat"], mx["rstd"], lng[...])
        tmpd_ref[...] = dcv.astype(BF16)
        ddwb_ref[...] += jnp.sum(dcv, axis=0, keepdims=True)
        dcv_buf[...] = dcv
        _tap_sums(dcv_buf, gbuf, gsh, HALO - (CONF - 1), ddww_ref)
        dpc_ref[:, 0:XA] = _xattn_bwd(dyx, q, probs, kv_ref, dkv_ref).astype(BF16)
        dpc_ref[:, XA:XA + MIX] = dgate.astype(BF16)

    tile, halo = _tile_specs(s, ODD_IN)
    row = pl.BlockSpec((TS, D), lambda i: (i, 0))
    vec = _const((1, BW))

    def out(n):
        return pl.BlockSpec((TS, n), lambda i: (i, 0))

    return pl.pallas_call(
        body, grid=(s // TS,), name="odd_bwd1",
        out_shape=(jax.ShapeDtypeStruct((s, XA + MIX), BF16), jax.ShapeDtypeStruct((s, BW), BF16),
                   jax.ShapeDtypeStruct((s, BW), BF16), jax.ShapeDtypeStruct((s, D), BF16),
                   jax.ShapeDtypeStruct((s, MIX), BF16),
                   jax.ShapeDtypeStruct((1, D), F32), jax.ShapeDtypeStruct((1, BW), F32),
                   jax.ShapeDtypeStruct((BW, BW), F32), jax.ShapeDtypeStruct((8 * CONF, BW), F32),
                   jax.ShapeDtypeStruct((1, BW), F32), jax.ShapeDtypeStruct((1, BW), F32),
                   jax.ShapeDtypeStruct((1, BW), F32), jax.ShapeDtypeStruct((BW, BW), F32),
                   jax.ShapeDtypeStruct((1, BW), F32), jax.ShapeDtypeStruct((N_MEM, D), F32)),
        in_specs=[row, row, out(BW), tile, halo, _const((N_MEM, D)), _const((4, TS, HALO + TS)), _const((BW, BW)), vec,
                  _const((CONF, BW)), vec, vec, vec, _const((BW, BW)), vec, _resident((MIX, D)), _const((1, D))],
        out_specs=(out(XA + MIX), out(BW), out(BW), out(D), out(MIX),
                   _const((1, D)), vec, _const((BW, BW)), _const((8 * CONF, BW)), vec, vec, vec, _const((BW, BW)), vec,
                   _const((N_MEM, D))),
        scratch_shapes=[pltpu.VMEM((HALO + TS, BW), F32), pltpu.VMEM((7, SHIFT_ROWS, BW), F32),
                        pltpu.VMEM((TS, BW), F32)],
        compiler_params=_cp(("arbitrary",)),
    )(dx, o, cv, p, p, kv, _band_matrices(TS, False), wbd, cscale, dww, dwb, ln_g, ln_b, pww, pwb, wout, post_g)


def _halo_next(nblk_per_tile, nblk):
    return lambda i: (jnp.minimum((i + 1) * nblk_per_tile, nblk - 1), 0)


def _pre_norm_bwd(dh, x, pre_g, dres, dpre_ref):
    r = lax.rsqrt(jnp.mean(x * x, axis=-1, keepdims=True) + EPS)
    xh = x * r
    dpre_ref[...] += jnp.sum(dh * xh, axis=0, keepdims=True)
    dxh = dh * pre_g
    return dres + r * (dxh - xh * jnp.mean(dxh * xh, axis=-1, keepdims=True))


def _even_bwd2(dp, w_t, x, pre_g, dres, rider=None):
    s = x.shape[0]
    tm = min(512, s)

    def body(dp_ref, w_ref, x_ref, pg, dres_ref, dx_ref, dpre_ref):
        _acc_init(pl.program_id(0), (dpre_ref,))
        dh = _dot(dp_ref[...], w_ref[...])
        dx_ref[...] = _pre_norm_bwd(dh, x_ref[...], pg[...], dres_ref[...], dpre_ref)

    row = pl.BlockSpec((tm, D), lambda i: (i, 0))
    return _host_call(
        body, grid=(s // tm,), name="even_bwd2", rider=rider,
        out_shape=(jax.ShapeDtypeStruct((s, D), F32), jax.ShapeDtypeStruct((1, D), F32)),
        in_specs=[pl.BlockSpec((tm, EVEN_IN), lambda i: (i, 0)), _resident((EVEN_IN, D)), row, _const((1, D)), row],
        out_specs=(row, _const((1, D))),
        args=(dp, w_t, x, pre_g, dres))


def _odd_bwd2(dpc, tmpc, tmpd, p, dww, w_t, x, pre_g, dres):
    s = x.shape[0]
    nt = s // TS

    def body(dpc_ref, tc_ref, tch_ref, td_ref, tdh_ref, ga_ref, gb_ref, bands_ref, dww_ref, w_ref, x_ref, pg,
             dres_ref, dpb_ref, dx_ref, dpre_ref, dbuf, dsh):
        i = pl.program_id(0)
        _acc_init(i, (dpre_ref,))
        more = i < nt - 1
        e_bf = tc_ref[...]
        eh = tch_ref[...]
        ecat = jnp.concatenate([e_bf, jnp.where(more, eh, jnp.zeros_like(eh))], axis=0)
        dbuf[0:TS, :] = td_ref[...].astype(F32)
        dbuf[TS:TS + HALO, :] = jnp.where(more, tdh_ref[...].astype(F32), 0.0)
        sums = [_dot(bands_ref[w], ecat) for w in range(len(POOL_WINDOWS))]
        rows = _row_ids(i, TS) + 1
        cnt = _pool_select([jnp.minimum(rows, w).astype(F32) for w in POOL_WINDOWS])
        dzc = (_pool_select(sums) - e_bf.astype(F32) * cnt).astype(BF16)
        _shifted_copies(dbuf, dsh)
        dz = dww_ref[CONF - 1:CONF, :] * dbuf[pl.ds(0, TS), :]
        for sft in range(1, CONF):
            dz = dz + dww_ref[CONF - 1 - sft:CONF - sft, :] * _rows_at(dbuf, dsh, sft, TS)
        ga = ga_ref[...].astype(F32)
        sgb = _sigmoid(gb_ref[...].astype(F32))
        dga = (dz * sgb).astype(BF16)
        dgb = (dz * ga * sgb * (1.0 - sgb)).astype(BF16)
        dpb_ref[:, 0:BW] = dzc
        dpb_ref[:, BW:2 * BW] = dga
        dpb_ref[:, 2 * BW:3 * BW] = dgb
        dh = (_dot(dzc, w_ref[0:BW, :]) + _dot(dga, w_ref[BW:2 * BW, :]) + _dot(dgb, w_ref[2 * BW:3 * BW, :])
              + _dot(dpc_ref[...], w_ref[3 * BW:ODD_IN, :]))
        dx_ref[...] = _pre_norm_bwd(dh, x_ref[...], pg[...], dres_ref[...], dpre_ref)

    row = pl.BlockSpec((TS, D), lambda i: (i, 0))

    def tile(n, j=0):
        return pl.BlockSpec((TS, n), lambda i: (i, j))

    nxt = pl.BlockSpec((HALO, BW), _halo_next(TS // HALO, s // HALO))
    return pl.pallas_call(
        body, grid=(nt,), name="odd_bwd2",
        out_shape=(jax.ShapeDtypeStruct((s, 3 * BW), BF16), jax.ShapeDtypeStruct((s, D), F32),
                   jax.ShapeDtypeStruct((1, D), F32)),
        in_specs=[tile(XA + MIX), tile(BW), nxt, tile(BW), nxt, tile(BW, 1), tile(BW, 2), _const((4, TS, HALO + TS)),
                  _const((CONF, BW)), _resident((ODD_IN, D)), row, _const((1, D)), row],
        out_specs=(tile(3 * BW), row, _const((1, D))),
        scratch_shapes=[pltpu.VMEM((TS + HALO, BW), F32), pltpu.VMEM((7, SHIFT_ROWS, BW), F32)],
        compiler_params=_cp(("arbitrary",)),
    )(dpc, tmpc, tmpc, tmpd, tmpd, p, p, _band_matrices(TS, True), dww, w_t, x, pre_g, dres)


def _grad_tn(a, b, tm, out=None, rows=None, row0=0, name="grad_tn", rider=None):
    s, m = a.shape
    n = b.shape[1]
    ts = min(2048, s)
    rows = m if rows is None else rows
    assert m % tm == 0 and s % ts == 0
    ns = s // ts
    if row0 % tm == 0:
        out_spec = pl.BlockSpec((tm, n), lambda i, k: (row0 // tm + i, 0))
    else:
        align = 16
        assert row0 % align == 0 and tm % align == 0
        out_spec = pl.BlockSpec((pl.Element(tm), pl.Element(n)),
                                lambda i, k: (pl.multiple_of(row0 + i * tm, align), 0))

    def body(*refs):
        a_ref, b_ref = refs[0], refs[1]
        o_ref, acc = refs[-2], refs[-1]
        k = pl.program_id(1)

        @pl.when(k == 0)
        def _():
            acc[...] = jnp.zeros_like(acc)

        acc[...] += _dot_tn(a_ref[...], b_ref[...])

        @pl.when(k == ns - 1)
        def _():
            o_ref[...] = acc[...].astype(BF16)

    in_specs = [pl.BlockSpec((ts, tm), lambda i, k: (k, i)), pl.BlockSpec((ts, n), lambda i, k: (k, 0))]
    args = [a, b]
    aliases = {}
    if out is not None:
        in_specs.append(pl.BlockSpec(memory_space=pltpu.HBM))
        args.append(out)
        aliases = {2: 0}
    (res,), got = _host_call(
        body, grid=(m // tm, ns), name=name, rider=rider, aliases=aliases,
        out_shape=(jax.ShapeDtypeStruct((rows, n), BF16),), in_specs=in_specs, out_specs=(out_spec,),
        scratch_shapes=[pltpu.VMEM((tm, n), F32)], args=args)
    return res if rider is None else (res, got)


def _place():
    x, y, c = lax.axis_index("x"), lax.axis_index("y"), lax.axis_index("c")
    chips = [(1 - x, y), (x, 1 - y), (1 - x, 1 - y)]
    return x, y, c, chips


def _hbm_specs(n):
    return [pl.BlockSpec(memory_space=pltpu.HBM)] * n


def _row_tile(r):
    for cand in (512, 400, 304, 256, 192, 128, 96, 16):
        if r % cand == 0:
            return cand
    raise ValueError(r)


def _place_shard(shard, place, dtype, name, after=None):
    r, cc = shard.shape
    tr = _row_tile(r)
    nt = r // tr

    def body(place_ref, s_ref, *rest):
        rest[-1][...] = s_ref[...].astype(dtype)

    in_specs = [pl.BlockSpec((tr, cc), lambda i, pr: (i, 0))]
    args = [shard]
    if after is not None:
        in_specs.append(pl.BlockSpec(after.shape, lambda i, pr: (0, 0)))
        args.append(after)
    return pl.pallas_call(
        body, name=name, out_shape=jax.ShapeDtypeStruct((N_CHIPS * r, cc), dtype),
        grid_spec=pltpu.PrefetchScalarGridSpec(
            num_scalar_prefetch=1, grid=(nt,), in_specs=in_specs,
            out_specs=pl.BlockSpec((tr, cc), lambda i, pr: (pr[1] * nt + i, 0))),
        compiler_params=_cp(("arbitrary",)),
    )(place, *args)


class _GatherRider:
    has_mid = True

    def __init__(self, fulls):
        n = len(fulls)
        self.inputs = list(fulls)
        self.out_shapes = [jax.ShapeDtypeStruct(a.shape, a.dtype) for a in fulls]
        self.aliases = {a: a for a in range(n)}
        self.sems = [pltpu.SemaphoreType.DMA((6 * n,)), pltpu.SemaphoreType.DMA((6 * n,))]
        self.block_rows = [a.shape[0] // N_CHIPS for a in fulls]

    def _ctx(self, outs, sems):
        send_sems, recv_sems = sems
        x, y, c, chips = _place()

        def rows(a, k, half):
            r = self.block_rows[a]
            return outs[a].at[pl.ds(k * r + half * (r // 2), r // 2)]

        def copy(a, j, blk, to):
            return pltpu.make_async_remote_copy(src_ref=blk, dst_ref=blk, send_sem=send_sems.at[a * 6 + j],
                                                recv_sem=recv_sems.at[a * 6 + j], device_id=to, device_id_type=MESH)

        return x, y, c, chips, rows, copy

    def start(self, ins, outs, sems, peers=(0, 1, 2)):
        x, y, c, chips, rows, copy = self._ctx(outs, sems)
        for j in peers:
            for a in range(len(outs)):
                copy(a, j, rows(a, 2 * x + y, c), (*chips[j], c)).start()

    def mid(self, ins, outs, sems, peers=(0, 1, 2)):
        x, y, c, chips, rows, copy = self._ctx(outs, sems)
        for j in peers:
            px, py = chips[j]
            for a in range(len(outs)):
                copy(a, j, rows(a, 2 * px + py, c), (px, py, c)).wait_recv()
                copy(a, 3 + j, rows(a, 2 * px + py, c), (x, y, 1 - c)).start()

    def wait_forwarded(self, outs, sems, peers=(0, 1, 2)):
        x, y, c, chips, rows, copy = self._ctx(outs, sems)
        for j in peers:
            px, py = chips[j]
            for a in range(len(outs)):
                copy(a, 3 + j, rows(a, 2 * px + py, 1 - c), (x, y, 1 - c)).wait_recv()

    def wait_sends(self, outs, sems):
        x, y, c, chips, rows, copy = self._ctx(outs, sems)
        for j, (px, py) in enumerate(chips):
            for a in range(len(outs)):
                copy(a, j, rows(a, 2 * x + y, c), (px, py, c)).wait_send()
                copy(a, 3 + j, rows(a, 2 * px + py, c), (x, y, 1 - c)).wait_send()

    def end(self, ins, outs, sems):
        self.wait_forwarded(outs, sems)
        self.wait_sends(outs, sems)


def _swap_halves(grads, small, name):
    n = len(grads)
    arrs = list(grads) + ([small] if small is not None else [])
    m = len(arrs)

    def body(*refs):
        ins, outs = refs[:m], refs[m:2 * m]
        send_sems, recv_sems = refs[2 * m:]
        x, y, c, _ = _place()
        sibling = (x, y, 1 - c)
        cps = []
        for a in range(m):
            src = ins[a].at[:, 1 - c] if a < n else ins[a]
            cp = pltpu.make_async_remote_copy(src_ref=src, dst_ref=outs[a], send_sem=send_sems.at[a],
                                              recv_sem=recv_sems.at[a], device_id=sibling, device_id_type=MESH)
            cp.start()
            cps.append(cp)
        for cp in cps:
            cp.wait_recv()
        for cp in cps:
            cp.wait_send()

    outs = tuple(jax.ShapeDtypeStruct((g.shape[0],) + g.shape[2:], g.dtype) for g in grads)
    if small is not None:
        outs += (jax.ShapeDtypeStruct(small.shape, small.dtype),)
    return pl.pallas_call(
        body, name=name, out_shape=outs, in_specs=_hbm_specs(m), out_specs=tuple(_hbm_specs(m)),
        scratch_shapes=[pltpu.SemaphoreType.DMA((m,)), pltpu.SemaphoreType.DMA((m,))],
    )(*arrs)


def _pair_sum(g, recv, place, name):
    _, _, h, cc = g.shape
    th = _row_tile(h)

    def body(c_ref, g_ref, r_ref, o_ref):
        o_ref[...] = (g_ref[...].astype(F32) + r_ref[...].astype(F32)).astype(o_ref.dtype)

    return pl.pallas_call(
        body, name=name, out_shape=jax.ShapeDtypeStruct(recv.shape, recv.dtype),
        grid_spec=pltpu.PrefetchScalarGridSpec(
            num_scalar_prefetch=1, grid=(N_CHIPS, h // th),
            in_specs=[pl.BlockSpec((None, None, th, cc), lambda k, r, c_ref: (k, c_ref[0], r, 0)),
                      pl.BlockSpec((None, th, cc), lambda k, r, c_ref: (k, r, 0))],
            out_specs=pl.BlockSpec((None, th, cc), lambda k, r, c_ref: (k, r, 0))),
        compiler_params=_cp(("arbitrary", "arbitrary")),
    )(place, g, recv)


def _finish_reduce(pack, halves):
    rows, cc = pack.shape
    hs = rows // 2
    n = len(halves)

    def body(*refs):
        pack_ref = refs[0]
        out_ref = refs[1 + n]
        big = refs[2 + n:2 + 2 * n]
        sib_ref, parts_ref, send_sems, recv_sems, big_send, big_recv = refs[2 + 2 * n:]
        x, y, c, chips = _place()
        me_k = 2 * x + y
        sibling = (x, y, 1 - c)
        mine = pl.ds(pl.multiple_of(c * hs, hs), hs)
        theirs = pl.ds(pl.multiple_of((1 - c) * hs, hs), hs)
        shared = [pltpu.make_async_remote_copy(src_ref=big[a].at[c], dst_ref=big[a].at[c], send_sem=big_send.at[a],
                                               recv_sem=big_recv.at[a], device_id=sibling, device_id_type=MESH)
                  for a in range(n)]
        for cp in shared:
            cp.start()
        first = pltpu.make_async_remote_copy(src_ref=pack_ref, dst_ref=sib_ref, send_sem=send_sems.at[0],
                                             recv_sem=recv_sems.at[0], device_id=sibling, device_id_type=MESH)
        first.start()
        first.wait()
        parts_ref[me_k] = pack_ref[mine, :] + sib_ref[mine, :]
        cps = [pltpu.make_async_remote_copy(src_ref=parts_ref.at[me_k], dst_ref=parts_ref.at[me_k],
                                            send_sem=send_sems.at[1 + j], recv_sem=recv_sems.at[1 + j],
                                            device_id=(px, py, c), device_id_type=MESH)
               for j, (px, py) in enumerate(chips)]
        for cp in cps:
            cp.start()
        for j, (px, py) in enumerate(chips):
            pltpu.make_async_remote_copy(src_ref=parts_ref.at[2 * px + py], dst_ref=parts_ref.at[2 * px + py],
                                         send_sem=send_sems.at[1 + j], recv_sem=recv_sems.at[1 + j],
                                         device_id=(px, py, c), device_id_type=MESH).wait_recv()
        for cp in cps:
            cp.wait_send()
        out_ref[mine, :] = ((parts_ref[0] + parts_ref[1]) + parts_ref[2]) + parts_ref[3]
        last = pltpu.make_async_remote_copy(src_ref=out_ref.at[mine], dst_ref=out_ref.at[mine],
                                            send_sem=send_sems.at[4], recv_sem=recv_sems.at[4], device_id=sibling,
                                            device_id_type=MESH)
        last.start()
        pltpu.make_async_remote_copy(src_ref=out_ref.at[theirs], dst_ref=out_ref.at[theirs],
                                     send_sem=send_sems.at[4], recv_sem=recv_sems.at[4], device_id=sibling,
                                     device_id_type=MESH).wait_recv()
        last.wait_send()
        for a in range(n):
            pltpu.make_async_remote_copy(src_ref=big[a].at[1 - c], dst_ref=big[a].at[1 - c], send_sem=big_send.at[a],
                                         recv_sem=big_recv.at[a], device_id=sibling,
                                         device_id_type=MESH).wait_recv()
        for cp in shared:
            cp.wait_send()

    vmem = pl.BlockSpec(memory_space=pltpu.VMEM)
    res = pl.pallas_call(
        body, name="finish_reduce",
        out_shape=(jax.ShapeDtypeStruct(pack.shape, pack.dtype),)
        + tuple(jax.ShapeDtypeStruct(g.shape, g.dtype) for g in halves),
        in_specs=[vmem] + _hbm_specs(n), out_specs=(vmem,) + tuple(_hbm_specs(n)),
        input_output_aliases={1 + a: 1 + a for a in range(n)},
        scratch_shapes=[pltpu.VMEM((rows, cc), F32), pltpu.VMEM((N_CHIPS, hs, cc), F32),
                        pltpu.SemaphoreType.DMA((5,)), pltpu.SemaphoreType.DMA((5,)),
                        pltpu.SemaphoreType.DMA((n,)), pltpu.SemaphoreType.DMA((n,))],
        compiler_params=_cp(),
    )(pack, *halves)
    return res[0], tuple(res[1:])


class _ExchangeRider:
    has_mid = False

    def __init__(self, sums):
        self.inputs = list(sums)
        self.out_shapes = [jax.ShapeDtypeStruct((3,) + g.shape[1:], g.dtype) for g in sums]
        m = len(self.inputs)
        self.aliases = {}
        self.sems = [pltpu.SemaphoreType.DMA((3 * m,)), pltpu.SemaphoreType.DMA((3 * m,))]

    def _copies(self, ins, outs, sems):
        send_sems, recv_sems = sems
        _, _, c, chips = _place()
        return [pltpu.make_async_remote_copy(
            src_ref=ins[a].at[2 * px + py], dst_ref=outs[a].at[j], send_sem=send_sems.at[a * 3 + j],
            recv_sem=recv_sems.at[a * 3 + j], device_id=(px, py, c), device_id_type=MESH)
            for j, (px, py) in enumerate(chips) for a in range(len(ins))]

    def start(self, ins, outs, sems):
        for cp in self._copies(ins, outs, sems):
            cp.start()

    def end(self, ins, outs, sems):
        cps = self._copies(ins, outs, sems)
        for cp in cps:
            cp.wait_recv()
        for cp in cps:
            cp.wait_send()


def _chip_sum(own, parts, place, name):
    npart, h, cc = parts.shape
    th = _row_tile(h)

    def body(place_ref, own_ref, p_ref, o_ref):
        acc = own_ref[...].astype(F32) + p_ref[0].astype(F32)
        for k in range(1, npart):
            acc = acc + p_ref[k].astype(F32)
        o_ref[...] = acc

    return pl.pallas_call(
        body, name=name, out_shape=jax.ShapeDtypeStruct((2, h, cc), F32),
        grid_spec=pltpu.PrefetchScalarGridSpec(
            num_scalar_prefetch=1, grid=(h // th,),
            in_specs=[pl.BlockSpec((None, th, cc), lambda r, pr: (pr[1], r, 0)),
                      pl.BlockSpec((npart, th, cc), lambda r, pr: (0, r, 0))],
            out_specs=pl.BlockSpec((None, th, cc), lambda r, pr: (pr[0], r, 0))),
        compiler_params=_cp(("arbitrary",)),
    )(place, own, parts)


def _adamw_math(w, g, m, v):
    m = ADAM_B1 * m + (1.0 - ADAM_B1) * g
    v = ADAM_B2 * v + (1.0 - ADAM_B2) * (g * g)
    m_hat = m / (1.0 - ADAM_B1 ** ADAM_STEP)
    v_hat = v / (1.0 - ADAM_B2 ** ADAM_STEP)
    delta = -ADAM_LR * (m_hat / (jnp.sqrt(v_hat) + ADAM_EPS) + ADAM_WD * w)
    return delta, m, v


def _adamw_big(w, g, m, v, name):
    r, cc = w.shape
    tr = min(_row_tile(r), 256) if r % 256 == 0 else _row_tile(r)

    def body(w_ref, g_ref, m_ref, v_ref, go_ref, d_ref, mo_ref, vo_ref):
        g = g_ref[...]
        d, mm, vv = _adamw_math(w_ref[...], g, m_ref[...], v_ref[...])
        go_ref[...] = g
        d_ref[...] = d
        mo_ref[...] = mm
        vo_ref[...] = vv

    blk = pl.BlockSpec((tr, cc), lambda i: (i, 0))
    sd = jax.ShapeDtypeStruct((r, cc), F32)
    return pl.pallas_call(body, grid=(r // tr,), name=name, out_shape=(sd, sd, sd, sd), in_specs=[blk] * 4,
                          out_specs=(blk, blk, blk, blk), compiler_params=_cp(("arbitrary",)))(w, g, m, v)


def _adamw_small(ws, gs, ms, vs):
    n = len(ws)

    def body(*refs):
        for a in range(n):
            w_ref, g_ref, m_ref, v_ref = refs[4 * a:4 * a + 4]
            d_ref, mo_ref, vo_ref = refs[4 * n + 3 * a:4 * n + 3 * a + 3]
            d, mm, vv = _adamw_math(w_ref[...], g_ref[...], m_ref[...], v_ref[...])
            d_ref[...] = d
            mo_ref[...] = mm
            vo_ref[...] = vv

    args, outs = [], []
    for a in range(n):
        args += [ws[a], gs[a], ms[a], vs[a]]
        outs += [jax.ShapeDtypeStruct(ws[a].shape, F32)] * 3
    res = pl.pallas_call(body, name="adamw_small", out_shape=tuple(outs), compiler_params=_cp())(*args)
    return [res[3 * a:3 * a + 3] for a in range(n)]


def _flat_pack(arrs, rows):
    flat = jnp.concatenate([a.reshape(-1) for a in arrs])
    return jnp.pad(flat, (0, rows * D - flat.shape[0])).reshape(rows, D)


def _flat_unpack(flat, shapes):
    out, off = [], 0
    for shp in shapes:
        size = 1
        for d_ in shp:
            size *= d_
        out.append(flat[off:off + size].reshape(shp))
        off += size
    return out


SMALL_EVEN = ("even_pre_g", "even_a_ln_g", "even_a_ln_b", "even_a_ws", "even_a_bs", "even_b_conv", "even_mem_g",
              "even_post_g")
SMALL_ODD = ("odd_pre_g", "odd_c_wgrp", "odd_c_scale", "odd_d_dw_w", "odd_d_dw_b", "odd_d_ln_g", "odd_d_ln_b",
             "odd_d_pw_b", "odd_mem_g", "odd_post_g")
BIG = ("even_w_in", "even_w_kv", "even_w_out", "odd_w_in", "odd_d_pw_w", "odd_w_kv", "odd_w_out")
WEIGHTS = ("even_pre_g", "even_w_in", "even_a_ln_g", "even_a_ln_b", "even_a_ws", "even_a_bs", "even_b_conv",
           "even_mem_g", "even_w_kv", "even_w_out", "even_post_g", "odd_pre_g", "odd_w_in", "odd_c_wgrp",
           "odd_c_scale", "odd_d_dw_w", "odd_d_dw_b", "odd_d_ln_g", "odd_d_ln_b", "odd_d_pw_w", "odd_d_pw_b",
           "odd_mem_g", "odd_w_kv", "odd_w_out", "odd_post_g")
PACKED = (("even_b_conv", (3, 192)), ("odd_pre_g", (1, 256)), ("odd_c_scale", (1, 192)), ("odd_d_dw_w", (31, 192)),
          ("odd_d_dw_b", (1, 192)), ("odd_d_ln_g", (1, 192)), ("odd_d_ln_b", (1, 192)), ("odd_d_pw_b", (1, 192)),
          ("odd_mem_g", (1, 256)), ("odd_post_g", (1, 256)))
PACK_ROWS = 16
SMALL_ROWS = 256


def _four(g):
    return g.reshape(N_CHIPS, 2, g.shape[0] // (2 * N_CHIPS), g.shape[1])


def _step(x, mem, target, w, place):
    wt = {}
    pack = _flat_pack([w[n][0] for n, _ in PACKED], PACK_ROWS)
    shards = {"even_w_in_t": w["even_w_in"][0].T, "odd_w_in_t": w["odd_w_in"][0].T, "even_w_kv": w["even_w_kv"][0],
              "odd_w_kv": w["odd_w_kv"][0], "even_w_out": w["even_w_out"][0], "odd_w_out": w["odd_w_out"][0],
              "odd_d_pw_w": w["odd_d_pw_w"][0]}
    placed = {n: _place_shard(shards[n], place, BF16, "place_" + n) for n in ("even_w_in_t", "even_w_kv")}
    placed["pack"] = _place_shard(pack, place, F32, "place_pack")

    order, group = _stream_tables(place[1], EVEN_IN)
    p_e, h_e, (wt["even_w_in_t"], packs), (wt["even_w_kv"],) = _in_fwd_streamed(
        x, w["even_pre_g"], [placed["even_w_in_t"], placed["pack"]], [placed["even_w_kv"]], order, group,
        "even_in_streamed")
    for n in ("even_w_out", "odd_w_in_t", "odd_w_kv", "odd_w_out", "odd_d_pw_w"):
        placed[n] = _place_shard(shards[n], place, BF16, "place_" + n, after=p_e[0:16, 0:128])
    packs = packs.reshape(N_CHIPS, PACK_ROWS * D)
    per_chip = [_flat_unpack(packs[k], [shp for _, shp in PACKED]) for k in range(N_CHIPS)]
    for a, (name, _) in enumerate(PACKED):
        wt[name] = jnp.concatenate([per_chip[k][a] for k in range(N_CHIPS)], axis=-1)
    for name in ("even_pre_g", "even_a_ln_g", "even_a_ln_b", "even_mem_g", "even_post_g"):
        wt[name] = w[name]

    tril = jnp.tril(jnp.ones((CH, CH), dtype=bool))
    wcat = jnp.where(tril[None], w["even_a_ws"][0], 0.0).transpose(1, 0, 2).reshape(CH, 4 * CH).astype(BF16)
    bsg = jnp.repeat(w["even_a_bs"][0].T, BW // 4, axis=1)
    hsel = (jnp.arange(BW)[:, None] // (BW // 4) == jnp.arange(128)[None, :]).astype(BF16)
    g4 = BW // 4
    eye = jnp.eye(4, dtype=F32)
    wbd = (w["odd_c_wgrp"][0][:, :, None, :] * eye[:, None, :, None]).reshape(BW, BW).astype(BF16)

    kv_e = _kv_fwd(mem, wt["even_mem_g"], wt["even_w_kv"], "even_kv")
    names = ("even_w_out", "odd_w_kv", "odd_d_pw_w")
    (y_e,), got = _even_mix_fwd(p_e, kv_e, wt["even_a_ln_g"], wt["even_a_ln_b"], wcat, bsg, wt["even_b_conv"],
                                rider=_GatherRider([placed[n] for n in names]))
    wt.update(zip(names, got))
    (x1, o_e), (wt["odd_w_out"],) = _out_fwd(x, y_e, wt["even_w_out"], wt["even_post_g"], "even_out_fwd",
                                             rider=_GatherRider([placed["odd_w_out"]]))
    order_o, group_o = _stream_tables(place[1], ODD_IN)
    p_o, h_o, (wt["odd_w_in_t"],), _ = _in_fwd_streamed(x1, wt["odd_pre_g"], [placed["odd_w_in_t"]], [], order_o,
                                                        group_o, "odd_in_streamed")
    kv_o = _kv_fwd(mem, wt["odd_mem_g"], wt["odd_w_kv"], "odd_kv")
    dx2, o_o, cv_o, loss = _odd_fwd(x1, p_o, kv_o, wbd, wt["odd_c_scale"], wt["odd_d_dw_w"], wt["odd_d_dw_b"],
                                    wt["odd_d_ln_g"], wt["odd_d_ln_b"], wt["odd_d_pw_w"], wt["odd_d_pw_b"],
                                    wt["odd_w_out"], wt["odd_post_g"], target)
    (dpc_o, tmpc, tmpd, do_o, y_o, g_post_o, g_cs, g_wbd, g_dww, g_dwb, g_lng_o, g_lnb_o, g_pww, g_pwb,
     dkv_o) = _odd_bwd1(dx2, o_o, cv_o, p_o, kv_o, wbd, wt["odd_c_scale"], wt["odd_d_dw_w"], wt["odd_d_dw_b"],
                        wt["odd_d_ln_g"], wt["odd_d_ln_b"], wt["odd_d_pw_w"], wt["odd_d_pw_b"], wt["odd_w_out"],
                        wt["odd_post_g"])
    dpb_o, dx1, g_pre_o = _odd_bwd2(dpc_o, tmpc, tmpd, p_o, wt["odd_d_dw_w"], wt["odd_w_in_t"], x1,
                                    wt["odd_pre_g"], dx2)
    g_win_o = _grad_tn(dpb_o, h_o, 768, rows=ODD_IN, name="odd_gw_in_b")
    g_win_o = _grad_tn(dpc_o, h_o, 1280, out=g_win_o, rows=ODD_IN, row0=3 * BW, name="odd_gw_in_c")
    g_wout_o = _grad_tn(y_o, do_o, 1024, name="odd_gw_out")
    g_wkv_o, g_memg_o = _kv_bwd(mem, wt["odd_mem_g"], wt["odd_w_kv"], dkv_o, "odd_kv_bwd")
    big_o = [_four(g) for g in (g_win_o, g_pww.astype(BF16), g_wkv_o, g_wout_o)]
    recv_o = _swap_halves(big_o, None, "swap_halves_odd")
    sums_o = [_pair_sum(big_o[a], recv_o[a], place, "pair_sum_odd_%d" % a) for a in range(len(big_o))]
    (dp_e, do_e, g_post_e, g_lng_e, g_lnb_e, g_wcat, g_bs, g_bconv,
     dkv_e), parts_o = _even_bwd1(dx1, o_e, p_e, kv_e, wt["even_a_ln_g"], wt["even_a_ln_b"], wcat, bsg, hsel,
                                  wt["even_b_conv"], wt["even_w_out"], wt["even_post_g"],
                                  rider=_ExchangeRider(sums_o))
    halves_o = [_chip_sum(sums_o[a], parts_o[a], place, "chip_sum_odd_%d" % a) for a in range(len(big_o))]
    g_wout_e = _grad_tn(y_e, do_e, 1024, name="even_gw_out")
    g_wkv_e, g_memg_e = _kv_bwd(mem, wt["even_mem_g"], wt["even_w_kv"], dkv_e, "even_kv_bwd")
    big_x = [_four(g) for g in (g_wkv_e, g_wout_e)]
    recv_x = _swap_halves(big_x, None, "swap_halves_kv_out")
    sums_x = [_pair_sum(big_x[a], recv_x[a], place, "pair_sum_kv_out_%d" % a) for a in range(len(big_x))]
    g_win_e, parts_x = _grad_tn(dp_e, h_e, 1280, name="even_gw_in", rider=_ExchangeRider(sums_x))
    halves_x = [_chip_sum(sums_x[a], parts_x[a], place, "chip_sum_kv_out_%d" % a) for a in range(len(big_x))]
    big_e = [_four(g_win_e)]
    recv_e = _swap_halves(big_e, None, "swap_halves_even")
    sums_e = [_pair_sum(big_e[0], recv_e[0], place, "pair_sum_even_w_in")]
    (dx0, g_pre_e), parts_e = _even_bwd2(dp_e, wt["even_w_in_t"], x, wt["even_pre_g"], dx1,
                                         rider=_ExchangeRider(sums_e))
    halves_e = [_chip_sum(sums_e[0], parts_e[0], place, "chip_sum_even_w_in")]

    g_aws = jnp.where(tril[None], g_wcat.reshape(CH, 4, CH).transpose(1, 0, 2), 0.0)
    g_wgrp = jnp.stack([lax.dynamic_slice(g_wbd, (g * g4, g * g4), (g4, g4)) for g in range(4)])
    small = {
        "even_pre_g": g_pre_e, "even_a_ln_g": g_lng_e, "even_a_ln_b": g_lnb_e, "even_a_ws": g_aws,
        "even_a_bs": g_bs[:, 0:4].T, "even_b_conv": g_bconv[0:3], "even_mem_g": g_memg_e, "even_post_g": g_post_e,
        "odd_pre_g": g_pre_o, "odd_c_wgrp": g_wgrp, "odd_c_scale": g_cs, "odd_d_dw_w": g_dww.reshape(CONF, 8, BW).sum(axis=1),
        "odd_d_dw_b": g_dwb, "odd_d_ln_g": g_lng_o, "odd_d_ln_b": g_lnb_o, "odd_d_pw_b": g_pwb,
        "odd_mem_g": g_memg_o, "odd_post_g": g_post_o,
    }
    small_names = SMALL_EVEN + SMALL_ODD
    small_pack = _flat_pack([small[n] for n in small_names] + [loss[0, 0].reshape(1)], SMALL_ROWS)
    small_total, full = _finish_reduce(small_pack, halves_e + halves_x + halves_o)
    order = ("even_w_in", "even_w_kv", "even_w_out", "odd_w_in", "odd_d_pw_w", "odd_w_kv", "odd_w_out")
    gbig = {n: full[a].reshape(full[a].shape[1] * 2, full[a].shape[2]) for a, n in enumerate(order)}
    return dx0, gbig, small_total.reshape(-1), [small[n].shape for n in small_names]


def kernel(x, mem, even_pre_g, even_w_in, even_a_ln_g, even_a_ln_b, even_a_ws, even_a_bs, even_b_conv, even_mem_g, even_w_kv, even_w_out, even_post_g, odd_pre_g, odd_w_in, odd_c_wgrp, odd_c_scale, odd_d_dw_w, odd_d_dw_b, odd_d_ln_g, odd_d_ln_b, odd_d_pw_w, odd_d_pw_b, odd_mem_g, odd_w_kv, odd_w_out, odd_post_g, loss_target, m_even_pre_g, m_even_w_in, m_even_a_ln_g, m_even_a_ln_b, m_even_a_ws, m_even_a_bs, m_even_b_conv, m_even_mem_g, m_even_w_kv, m_even_w_out, m_even_post_g, m_odd_pre_g, m_odd_w_in, m_odd_c_wgrp, m_odd_c_scale, m_odd_d_dw_w, m_odd_d_dw_b, m_odd_d_ln_g, m_odd_d_ln_b, m_odd_d_pw_w, m_odd_d_pw_b, m_odd_mem_g, m_odd_w_kv, m_odd_w_out, m_odd_post_g, v_even_pre_g, v_even_w_in, v_even_a_ln_g, v_even_a_ln_b, v_even_a_ws, v_even_a_bs, v_even_b_conv, v_even_mem_g, v_even_w_kv, v_even_w_out, v_even_post_g, v_odd_pre_g, v_odd_w_in, v_odd_c_wgrp, v_odd_c_scale, v_odd_d_dw_w, v_odd_d_dw_b, v_odd_d_ln_g, v_odd_d_ln_b, v_odd_d_pw_w, v_odd_d_pw_b, v_odd_mem_g, v_odd_w_kv, v_odd_w_out, v_odd_post_g):
    given = dict(locals())
    w = {n: given[n] for n in WEIGHTS}
    mom = {n: given["m_" + n] for n in WEIGHTS}
    var = {n: given["v_" + n] for n in WEIGHTS}

    x_, y_, c_ = lax.axis_index("x"), lax.axis_index("y"), lax.axis_index("c")
    chip = 2 * x_ + y_
    place = jnp.stack([c_, chip]).astype(jnp.int32)
    grad_x, gbig, gsmall_flat, small_shapes = _step(x[0], mem[0], loss_target[0], w, place)

    names = SMALL_EVEN + SMALL_ODD
    grads = {}
    unpacked = _flat_unpack(gsmall_flat, small_shapes + [(1,)])
    loss = unpacked[-1][0]
    for n, g in zip(names, unpacked[:-1]):
        shard_shape = w[n].shape[1:]
        if g.shape[-1] != shard_shape[-1]:
            g = lax.dynamic_slice_in_dim(g, chip * shard_shape[-1], shard_shape[-1], axis=g.ndim - 1)
        grads[n] = g.reshape(shard_shape)

    def two_d(a):
        return a.reshape(-1, a.shape[-1])

    upd = {}
    for n in BIG:
        if n.endswith("w_in"):
            res = _adamw_big(w[n][0].T, gbig[n], mom[n][0].T, var[n][0].T, "adamw_" + n)
            res = tuple(r.T for r in res)
        else:
            res = _adamw_big(w[n][0], gbig[n], mom[n][0], var[n][0], "adamw_" + n)
        grads[n], upd[n] = res[0], res[1:]
    res = _adamw_small([two_d(w[n][0]) for n in names], [two_d(grads[n]) for n in names],
                       [two_d(mom[n][0]) for n in names], [two_d(var[n][0]) for n in names])
    for n, r in zip(names, res):
        upd[n] = r

    outs = [loss, grad_x[None]]
    outs += [grads[n].reshape(w[n].shape) for n in WEIGHTS]
    for j in range(3):
        outs += [upd[n][j].reshape(w[n].shape) for n in WEIGHTS]
    return tuple(outs)
```

```python
import functools

import jax
import jax.numpy as jnp
from jax import lax
from jax.experimental import pallas as pl
from jax.experimental.pallas import tpu as pltpu

F32 = jnp.float32
BF16 = jnp.bfloat16
MESH = pl.DeviceIdType.MESH

D = 1024
N_MEM = 256
MIX = 2048
XA = 512
HD = 128
BW = 768
CH = 128
EPS = 1e-6
SCALE = HD ** -0.5
POOL_WINDOWS = (2, 4, 8, 16)
CONF = 31
EVEN_IN = 6400
ODD_IN = 4864
N_CHIPS = 4

ADAM_LR = 0.001
ADAM_B1 = 0.9
ADAM_B2 = 0.999
ADAM_EPS = 1e-08
ADAM_WD = 0.01
ADAM_STEP = 10

TS = 256
HALO = 32
VMEM_LIMIT = 56 * 1024 * 1024


def _cp(sem=None):
    return pltpu.CompilerParams(dimension_semantics=sem, vmem_limit_bytes=VMEM_LIMIT)


def _dot(a, b):
    return jnp.dot(a, b, preferred_element_type=F32)


def _dot_nt(a, b):
    return lax.dot_general(a, b, (((1,), (1,)), ((), ())), preferred_element_type=F32)


def _dot_tn(a, b):
    return lax.dot_general(a, b, (((0,), (0,)), ((), ())), preferred_element_type=F32)


def _sigmoid(x):
    return 1.0 / (1.0 + jnp.exp(-x))


def _resident(shape):
    return pl.BlockSpec(shape, lambda *_: (0,) * len(shape), pipeline_mode=pl.Buffered(1))


def _const(shape):
    return pl.BlockSpec(shape, lambda *_: (0,) * len(shape))


def _kv_fwd(mem, mem_g, wkv, name):
    def body(mem_ref, g_ref, w_ref, kv_ref):
        m = mem_ref[...]
        r = lax.rsqrt(jnp.mean(m * m, axis=-1, keepdims=True) + EPS)
        mn = (m * r * g_ref[...]).astype(BF16)
        kv_ref[...] = _dot(mn, w_ref[...]).astype(BF16)

    return pl.pallas_call(body, out_shape=jax.ShapeDtypeStruct((N_MEM, D), BF16), name=name,
                          compiler_params=_cp())(mem, mem_g, wkv)


def _kv_bwd(mem, mem_g, wkv, dkv, name):
    def body(mem_ref, g_ref, w_ref, dkv_ref, dw_ref, dg_ref):
        m = mem_ref[...]
        r = lax.rsqrt(jnp.mean(m * m, axis=-1, keepdims=True) + EPS)
        mh = m * r
        mn = (mh * g_ref[...]).astype(BF16)
        dkv = dkv_ref[...].astype(BF16)
        dw_ref[...] = _dot_tn(mn, dkv).astype(BF16)
        dmn = _dot_nt(dkv, w_ref[...])
        dg_ref[...] = jnp.sum(dmn * mh, axis=0, keepdims=True)

    return pl.pallas_call(body, out_shape=(jax.ShapeDtypeStruct((D, D), BF16), jax.ShapeDtypeStruct((1, D), F32)),
                          name=name, compiler_params=_cp())(mem, mem_g, wkv, dkv)


def _host_call(body, *, grid, name, out_shape, in_specs, out_specs, args, scratch_shapes=(), aliases=None,
               rider=None):
    sem = ("arbitrary",) * len(grid)
    aliases = dict(aliases or {})
    if rider is None:
        res = pl.pallas_call(body, grid=grid, name=name, out_shape=tuple(out_shape), in_specs=list(in_specs),
                             out_specs=tuple(out_specs), scratch_shapes=list(scratch_shapes),
                             input_output_aliases=aliases, compiler_params=_cp(sem))(*args)
        return tuple(res), ()
    n_in, n_out, n_sc = len(in_specs), len(out_specs), len(scratch_shapes)
    r_in, r_out = len(rider.inputs), len(rider.out_shapes)

    def full_body(*refs):
        host_in = refs[:n_in]
        rid_in = refs[n_in:n_in + r_in]
        host_out = refs[n_in + r_in:n_in + r_in + n_out]
        rid_out = refs[n_in + r_in + n_out:n_in + r_in + n_out + r_out]
        host_sc = refs[n_in + r_in + n_out + r_out:n_in + r_in + n_out + r_out + n_sc]
        sems = refs[n_in + r_in + n_out + r_out + n_sc:]
        first = pl.program_id(0) == 0
        last = pl.program_id(0) == grid[0] - 1
        for ax in range(1, len(grid)):
            first = jnp.logical_and(first, pl.program_id(ax) == 0)
            last = jnp.logical_and(last, pl.program_id(ax) == grid[ax] - 1)

        @pl.when(first)
        def _():
            rider.start(rid_in, rid_out, sems)

        if rider.has_mid:
            @pl.when(last)
            def _():
                rider.mid(rid_in, rid_out, sems)

        body(*host_in, *host_out, *host_sc)

        @pl.when(last)
        def _():
            rider.end(rid_in, rid_out, sems)

    aliases.update({n_in + j: n_out + k for j, k in rider.aliases.items()})
    res = pl.pallas_call(
        full_body, grid=grid, name=name, out_shape=tuple(out_shape) + tuple(rider.out_shapes),
        in_specs=list(in_specs) + _hbm_specs(r_in), out_specs=tuple(out_specs) + tuple(_hbm_specs(r_out)),
        scratch_shapes=list(scratch_shapes) + list(rider.sems), input_output_aliases=aliases,
        compiler_params=_cp(sem),
    )(*args, *rider.inputs)
    return tuple(res[:n_out]), tuple(res[n_out:])


def _in_fwd(x, pre_g, w_t, name, rider=None):
    s, n = x.shape[0], w_t.shape[0]
    tm = min(512, s)
    nc = 256

    def body(x_ref, g_ref, w_ref, p_ref, h_ref):
        xv = x_ref[...]
        r = lax.rsqrt(jnp.mean(xv * xv, axis=-1, keepdims=True) + EPS)
        h = (xv * r * g_ref[...]).astype(BF16)
        h_ref[...] = h
        for j in range(n // nc):
            p_ref[:, j * nc:(j + 1) * nc] = _dot_nt(h, w_ref[j * nc:(j + 1) * nc, :]).astype(BF16)

    return _host_call(
        body, grid=(s // tm,), name=name, rider=rider,
        out_shape=(jax.ShapeDtypeStruct((s, n), BF16), jax.ShapeDtypeStruct((s, D), BF16)),
        in_specs=[pl.BlockSpec((tm, D), lambda i: (i, 0)), _const((1, D)), _resident((n, D))],
        out_specs=(pl.BlockSpec((tm, n), lambda i: (i, 0)), pl.BlockSpec((tm, D), lambda i: (i, 0))),
        args=(x, pre_g, w_t))


NC = 256


def _stream_tables(core, chip, n):
    nchunk = n // NC
    idx = jnp.arange(nchunk, dtype=jnp.int32)
    src = jnp.array([0, 2, 1, 3], jnp.int32)
    r = n // N_CHIPS

    def group_of(row):
        j = src[(row // r) ^ chip]
        through_sibling = ((row % r) // (r // 2) != core).astype(jnp.int32)
        return jnp.where(j == 0, 0, 2 * j - 1 + through_sibling)

    grp = jnp.maximum(group_of(idx * NC), group_of(idx * NC + NC - 1))
    order = jnp.argsort(grp * 64 + idx).astype(jnp.int32)
    return order, grp[order]


def _in_fwd_streamed(x, pre_g, first, later, order, group, name):
    s, n = x.shape[0], first[0].shape[0]
    nchunk = n // NC
    rider = _GatherRider(first)
    rider2 = _GatherRider(later) if later else None
    a, m = len(first), len(later)
    tr = min(256, s)

    def body(*refs):
        order_ref, group_ref, x_ref, g_ref = refs[0:4]
        p_ref, h_ref = refs[4 + a + m:6 + a + m]
        outs = refs[6 + a + m:6 + 2 * a + m]
        outs2 = refs[6 + 2 * a + m:6 + 2 * a + 2 * m]
        wbuf, wsem, send_sems, recv_sems = refs[6 + 2 * a + 2 * m:10 + 2 * a + 2 * m]
        sems2 = refs[10 + 2 * a + 2 * m:]
        w_hbm = outs[0]
        j = pl.program_id(0)
        sems = (send_sems, recv_sems)
        grp = group_ref[j]
        new_group = jnp.logical_or(j == 0, group_ref[jnp.maximum(j - 1, 0)] != grp)
        slot = j % 2

        def fetch(step, sl):
            rows = pl.ds(pl.multiple_of(order_ref[step] * NC, NC), NC)
            return pltpu.make_async_copy(w_hbm.at[rows], wbuf.at[sl], wsem.at[sl])

        @pl.when(j == 0)
        def _():
            rider.start(None, outs, sems, peers=(0, 1))

            @pl.loop(0, s // tr)
            def _(t):
                rows = pl.ds(pl.multiple_of(t * tr, tr), tr)
                xv = x_ref[rows, :]
                r = lax.rsqrt(jnp.mean(xv * xv, axis=-1, keepdims=True) + EPS)
                h_ref[rows, :] = (xv * r * g_ref[...]).astype(BF16)

        before = jnp.where(j == 0, 0, group_ref[jnp.maximum(j - 1, 0)])

        def entering(b):
            return jnp.logical_and(before < b, b <= grp)

        for src in range(3):
            @pl.when(entering(2 * src + 1))
            def _(src=src):
                if src == 0:
                    rider.start(None, outs, sems, peers=(2,))
                rider.mid(None, outs, sems, peers=(src,))
                if src == 1 and rider2 is not None:
                    rider2.start(None, outs2, sems2)

            @pl.when(entering(2 * src + 2))
            def _(src=src):
                rider.wait_forwarded(outs, sems, peers=(src,))

        @pl.when(new_group)
        def _():
            fetch(j, slot).start()

        fetch(j, slot).wait()
        nxt = jnp.minimum(j + 1, nchunk - 1)

        @pl.when(jnp.logical_and(j + 1 < nchunk, group_ref[nxt] == grp))
        def _():
            fetch(nxt, 1 - slot).start()

        p_ref[...] = _dot_nt(h_ref[...], wbuf[slot]).astype(BF16)

        @pl.when(j == nchunk - 1)
        def _():
            rider.wait_sends(outs, sems)
            if rider2 is not None:
                rider2.mid(None, outs2, sems2)
                rider2.end(None, outs2, sems2)

    hbm = pl.BlockSpec(memory_space=pltpu.HBM)
    arrs = list(first) + list(later)
    whole = pl.BlockSpec((s, D), lambda j, o, g: (0, 0), pipeline_mode=pl.Buffered(1))
    res = pl.pallas_call(
        body, name=name,
        out_shape=(jax.ShapeDtypeStruct((s, n), BF16), jax.ShapeDtypeStruct((s, D), BF16))
        + tuple(jax.ShapeDtypeStruct(v.shape, v.dtype) for v in arrs),
        grid_spec=pltpu.PrefetchScalarGridSpec(
            num_scalar_prefetch=2, grid=(nchunk,),
            in_specs=[whole, pl.BlockSpec((1, D), lambda j, o, g: (0, 0))] + [hbm] * (a + m),
            out_specs=(pl.BlockSpec((s, NC), lambda j, o, g: (0, o[j])),
                       pl.BlockSpec((s, D), lambda j, o, g: (0, 0))) + (hbm,) * (a + m),
            scratch_shapes=[pltpu.VMEM((2, NC, D), BF16), pltpu.SemaphoreType.DMA((2,))] + list(rider.sems)
            + (list(rider2.sems) if rider2 is not None else [])),
        input_output_aliases={4 + v: 2 + v for v in range(a + m)},
        compiler_params=_cp(("arbitrary",)),
    )(order, group, x, pre_g, *arrs)
    return res[0], res[1], tuple(res[2:2 + a]), tuple(res[2 + a:])


def _xattn_fwd(q, kv_ref):
    outs, probs = [], []
    for h in range(XA // HD):
        qh = q[:, h * HD:(h + 1) * HD]
        kh = kv_ref[:, h * HD:(h + 1) * HD]
        vh = kv_ref[:, XA + h * HD:XA + (h + 1) * HD]
        sc = _dot_nt(qh, kh) * SCALE
        e = jnp.exp(sc - jnp.max(sc, axis=-1, keepdims=True))
        pr = e / jnp.sum(e, axis=-1, keepdims=True)
        outs.append(_dot(pr.astype(BF16), vh))
        probs.append(pr)
    return jnp.concatenate(outs, axis=-1), probs


def _xattn_bwd(dyx, q, probs, kv_ref, dkv_ref):
    dqs = []
    for h in range(XA // HD):
        qh = q[:, h * HD:(h + 1) * HD]
        kh = kv_ref[:, h * HD:(h + 1) * HD]
        vh = kv_ref[:, XA + h * HD:XA + (h + 1) * HD]
        dy = dyx[:, h * HD:(h + 1) * HD].astype(BF16)
        pr = probs[h]
        dp = _dot_nt(dy, vh)
        ds = (pr * (dp - jnp.sum(dp * pr, axis=-1, keepdims=True))).astype(BF16)
        dqs.append(_dot(ds, kh) * SCALE)
        dkv_ref[:, h * HD:(h + 1) * HD] += _dot_tn(ds, qh) * SCALE
        dkv_ref[:, XA + h * HD:XA + (h + 1) * HD] += _dot_tn(pr.astype(BF16), dy)
    return jnp.concatenate(dqs, axis=-1)


def _layer_norm_fwd(v, g, b):
    mu = jnp.mean(v, axis=-1, keepdims=True)
    vc = v - mu
    rstd = lax.rsqrt(jnp.mean(vc * vc, axis=-1, keepdims=True) + EPS)
    vhat = vc * rstd
    return vhat * g + b, vhat, rstd


def _layer_norm_bwd(dy, vhat, rstd, g):
    dvh = dy * g
    return rstd * (dvh - jnp.mean(dvh, axis=-1, keepdims=True) - vhat * jnp.mean(dvh * vhat, axis=-1, keepdims=True))


def _head_masks():
    col = lax.broadcasted_iota(jnp.int32, (1, BW), 1)
    return [(col >= h * (BW // 4)) & (col < (h + 1) * (BW // 4)) for h in range(4)]


def _halo_prev(nblk_per_tile):
    return lambda i: (jnp.maximum(i * nblk_per_tile - 1, 0), 0)


def _row_ids(i, t):
    return i * t + lax.broadcasted_iota(jnp.int32, (t, 1), 0)


def _even_mix(i, p_ref, ph_ref, ln_g, ln_b, wcat_ref, bsg_ref, bconv_ref, wbuf):
    t = p_ref.shape[0]
    u = p_ref[:, 0:BW].astype(F32)
    v = p_ref[:, BW:2 * BW].astype(F32)
    bg = p_ref[:, 2 * BW:3 * BW].astype(F32)
    cg = p_ref[:, 3 * BW:4 * BW].astype(F32)
    xin = p_ref[:, 4 * BW:5 * BW].astype(F32)
    vn, vhat, rstd = _layer_norm_fwd(v, ln_g, ln_b)
    masks = _head_masks()
    sgs, vsts = [], []
    for n in range(t // CH):
        vn_c = vn[n * CH:(n + 1) * CH]
        vst = jnp.concatenate([jnp.where(m, vn_c, 0.0) for m in masks], axis=0).astype(BF16)
        sgs.append(_dot(wcat_ref[...], vst) + bsg_ref[...])
        vsts.append(vst)
    sg = jnp.concatenate(sgs, axis=0)
    ya = u * sg
    w_halo = ph_ref[:, 3 * BW:4 * BW].astype(F32) * ph_ref[:, 4 * BW:5 * BW].astype(F32)
    wbuf[0:HALO, :] = jnp.where(i > 0, w_halo, 0.0)
    wbuf[HALO:HALO + t, :] = cg * xin
    conv = (bconv_ref[0:1, :] * wbuf[pl.ds(HALO - 2, t), :] + bconv_ref[1:2, :] * wbuf[pl.ds(HALO - 1, t), :]
            + bconv_ref[2:3, :] * wbuf[pl.ds(HALO, t), :])
    yb = bg * conv
    return dict(u=u, bg=bg, cg=cg, xin=xin, vhat=vhat, rstd=rstd, sg=sg, vsts=vsts, conv=conv, ya=ya, yb=yb,
                masks=masks)


def _pool_select(vals):
    col = lax.broadcasted_iota(jnp.int32, (1, BW), 1)
    g = BW // 4
    return jnp.where(col < g, vals[0], jnp.where(col < 2 * g, vals[1], jnp.where(col < 3 * g, vals[2], vals[3])))


def _inv_counts(i, t):
    rows = _row_ids(i, t) + 1
    return [1.0 / jnp.minimum(rows, w).astype(F32) for w in POOL_WINDOWS]


def _band_matrices(t, forward):
    j = jnp.arange(t)[:, None]
    r = jnp.arange(HALO + t)[None, :]
    if forward:
        return jnp.stack([(r >= j) & (r < j + w) for w in POOL_WINDOWS]).astype(BF16)
    return jnp.stack([(r <= HALO + j) & (r > HALO + j - w) for w in POOL_WINDOWS]).astype(BF16)


SHIFT_ROWS = HALO + TS - 8


def _shifted_copies(buf, sh):
    for b in range(1, 8):
        sh[b - 1] = buf[pl.ds(b, SHIFT_ROWS), :]


def _rows_at(buf, sh, off, t):
    a, b = divmod(off, 8)
    return buf[pl.ds(8 * a, t), :] if b == 0 else sh[b - 1, pl.ds(8 * a, t), :]


def _tap_sums(d_ref, buf, sh, base, out_ref):
    t = d_ref.shape[0]
    group = 4
    for k0 in range(0, CONF, group):
        taps = list(range(k0, min(k0 + group, CONF)))

        def step(r, accs, taps=taps):
            row = pl.multiple_of(r * 8, 8)
            d = d_ref[pl.ds(row, 8), :]
            new = []
            for acc, k in zip(accs, taps):
                a, b = divmod(base + k, 8)
                src = buf[pl.ds(row + 8 * a, 8), :] if b == 0 else sh[b - 1, pl.ds(row + 8 * a, 8), :]
                new.append(acc + d * src)
            return tuple(new)

        accs = lax.fori_loop(0, t // 8, step, tuple(jnp.zeros((8, BW), F32) for _ in taps), unroll=2)
        for acc, k in zip(accs, taps):
            out_ref[8 * k:8 * k + 8, :] += acc


def _odd_mix(i, p_ref, ph_ref, bands_ref, wbd_ref, cscale, dww_ref, dwb, ln_g, ln_b, pww_ref, pwb, gbuf, gsh,
             cv=None):
    t = p_ref.shape[0]
    zc_bf = p_ref[:, 0:BW]
    zc = zc_bf.astype(F32)
    ga = p_ref[:, BW:2 * BW].astype(F32)
    gb = p_ref[:, 2 * BW:3 * BW].astype(F32)
    zh = ph_ref[:, 0:BW]
    zcat = jnp.concatenate([jnp.where(i > 0, zh, jnp.zeros_like(zh)), zc_bf], axis=0)
    inv = _inv_counts(i, t)
    pooled = _pool_select([_dot(bands_ref[w], zcat) * inv[w] for w in range(len(POOL_WINDOWS))]) - zc
    pooled_bf = pooled.astype(BF16)
    pre = _dot(pooled_bf, wbd_ref[...])
    yc = pre * cscale
    sgb = _sigmoid(gb)
    z = ga * sgb
    gh_a = ph_ref[:, BW:2 * BW].astype(F32)
    gh_b = ph_ref[:, 2 * BW:3 * BW].astype(F32)
    gbuf[0:HALO, :] = jnp.where(i > 0, gh_a * _sigmoid(gh_b), 0.0)
    gbuf[HALO:HALO + t, :] = z
    _shifted_copies(gbuf, gsh)
    if cv is None:
        cv = dwb + dww_ref[CONF - 1:CONF, :] * z
        for k in range(CONF - 1):
            cv = cv + dww_ref[k:k + 1, :] * _rows_at(gbuf, gsh, HALO - (CONF - 1) + k, t)
    zl, zhat, rstd = _layer_norm_fwd(cv, ln_g, ln_b)
    szl = _sigmoid(zl)
    zs = (zl * szl).astype(BF16)
    yd = _dot(zs, pww_ref[...]) + pwb
    return dict(ga=ga, sgb=sgb, pooled_bf=pooled_bf, pre=pre, yc=yc, zhat=zhat, rstd=rstd, zl=zl, szl=szl,
                zs=zs, yd=yd, inv=inv, cv=cv)


def _post_norm(o, post_g):
    r = lax.rsqrt(jnp.mean(o * o, axis=-1, keepdims=True) + EPS)
    return o * r, r


def _gate_out(y_a, y_b, y_x, gate, wout_ref):
    sgt = _sigmoid(gate)
    sgate = gate * sgt
    ys = [(y_a * sgate[:, 0:BW]).astype(BF16), (y_b * sgate[:, BW:2 * BW]).astype(BF16),
          (y_x * sgate[:, 2 * BW:MIX]).astype(BF16)]
    o = (_dot(ys[0], wout_ref[0:BW, :]) + _dot(ys[1], wout_ref[BW:2 * BW, :]) + _dot(ys[2], wout_ref[2 * BW:MIX, :]))
    return o, ys, sgt, sgate


def _tile_specs(s, n):
    nh = TS // HALO
    return pl.BlockSpec((TS, n), lambda i: (i, 0)), pl.BlockSpec((HALO, n), _halo_prev(nh))


def _even_fwd(x, p, kv, ln_g, ln_b, wcat, bsg, bconv, wout, post_g, rider=None):
    s = x.shape[0]

    def body(x_ref, p_ref, ph_ref, kv_ref, lng, lnb, wcat_ref, bsg_ref, bconv_ref, wout_ref, pg, x1_ref, o_ref,
             y_ref, wbuf):
        i = pl.program_id(0)
        mx = _even_mix(i, p_ref, ph_ref, lng[...], lnb[...], wcat_ref, bsg_ref, bconv_ref, wbuf)
        yx, _ = _xattn_fwd(p_ref[:, 5 * BW:5 * BW + XA], kv_ref)
        gate = p_ref[:, 5 * BW + XA:EVEN_IN].astype(F32)
        o, ys, _, _ = _gate_out(mx["ya"], mx["yb"], yx, gate, wout_ref)
        y_ref[:, 0:BW] = ys[0]
        y_ref[:, BW:2 * BW] = ys[1]
        y_ref[:, 2 * BW:MIX] = ys[2]
        n, _ = _post_norm(o, pg[...])
        o_ref[...] = o
        x1_ref[...] = x_ref[...] + n * pg[...]

    tile, halo = _tile_specs(s, EVEN_IN)
    row = pl.BlockSpec((TS, D), lambda i: (i, 0))
    return _host_call(
        body, grid=(s // TS,), name="even_fwd", rider=rider,
        out_shape=(jax.ShapeDtypeStruct((s, D), F32), jax.ShapeDtypeStruct((s, D), F32),
                   jax.ShapeDtypeStruct((s, MIX), BF16)),
        in_specs=[row, tile, halo, _const((N_MEM, D)), _const((1, BW)), _const((1, BW)), _const((CH, 4 * CH)),
                  _const((CH, BW)), _const((3, BW)), _resident((MIX, D)), _const((1, D))],
        out_specs=(row, row, pl.BlockSpec((TS, MIX), lambda i: (i, 0))),
        scratch_shapes=[pltpu.VMEM((HALO + TS, BW), F32)],
        args=(x, p, p, kv, ln_g, ln_b, wcat, bsg, bconv, wout, post_g))


def _odd_fwd(x1, p, kv, wbd, cscale, dww, dwb, ln_g, ln_b, pww, pwb, wout, post_g, target):
    s = x1.shape[0]

    def body(x_ref, p_ref, ph_ref, kv_ref, bands_ref, wbd_ref, cs, dww_ref, dwb_ref, lng, lnb, pww_ref, pwb_ref,
             wout_ref, pg, tgt_ref, dx_ref, o_ref, cv_ref, loss_ref, gbuf, gsh):
        i = pl.program_id(0)
        mx = _odd_mix(i, p_ref, ph_ref, bands_ref, wbd_ref, cs[...], dww_ref, dwb_ref[...], lng[...], lnb[...],
                      pww_ref, pwb_ref[...], gbuf, gsh)
        cv_ref[...] = mx["cv"]
        yx, _ = _xattn_fwd(p_ref[:, 3 * BW:3 * BW + XA], kv_ref)
        gate = p_ref[:, 3 * BW + XA:ODD_IN].astype(F32)
        o, _, _, _ = _gate_out(mx["yc"], mx["yd"], yx, gate, wout_ref)
        n, _ = _post_norm(o, pg[...])
        o_ref[...] = o
        err = x_ref[...] + n * pg[...] - tgt_ref[...]
        dx_ref[...] = err * (1.0 / D)

        @pl.when(i == 0)
        def _():
            loss_ref[...] = jnp.zeros_like(loss_ref)

        loss_ref[...] += 0.5 * jnp.sum(jnp.sum(err * err, axis=-1, keepdims=True) * (1.0 / D), axis=0, keepdims=True)

    tile, halo = _tile_specs(s, ODD_IN)
    row = pl.BlockSpec((TS, D), lambda i: (i, 0))
    vec = _const((1, BW))
    return pl.pallas_call(
        body, grid=(s // TS,), name="odd_fwd",
        out_shape=(jax.ShapeDtypeStruct((s, D), F32), jax.ShapeDtypeStruct((s, D), F32),
                   jax.ShapeDtypeStruct((s, BW), F32), jax.ShapeDtypeStruct((8, 128), F32)),
        in_specs=[row, tile, halo, _const((N_MEM, D)), _const((4, TS, HALO + TS)), _const((BW, BW)), vec,
                  _const((CONF, BW)), vec, vec, vec, _const((BW, BW)), vec, _resident((MIX, D)), _const((1, D)), row],
        out_specs=(row, row, pl.BlockSpec((TS, BW), lambda i: (i, 0)), _const((8, 128))),
        scratch_shapes=[pltpu.VMEM((HALO + TS, BW), F32), pltpu.VMEM((7, SHIFT_ROWS, BW), F32)],
        compiler_params=_cp(("arbitrary",)),
    )(x1, p, p, kv, _band_matrices(TS, False), wbd, cscale, dww, dwb, ln_g, ln_b, pww, pwb, wout, post_g, target)


def _acc_init(i, refs):
    @pl.when(i == 0)
    def _():
        for r in refs:
            r[...] = jnp.zeros_like(r)


def _post_norm_bwd(dx, o, pg, dpg_ref):
    n, r = _post_norm(o, pg)
    dpg_ref[...] += jnp.sum(dx * n, axis=0, keepdims=True)
    dn = dx * pg
    return (r * (dn - n * jnp.mean(dn * n, axis=-1, keepdims=True))).astype(BF16)


def _gate_bwd(do, wout_ref, ys_f32, gate, y_ref):
    dy = _dot_nt(do, wout_ref[...])
    sgt = _sigmoid(gate)
    sgate = gate * sgt
    dsilu = sgt * (1.0 + gate * (1.0 - sgt))
    offs = (0, BW, 2 * BW, MIX)
    dys, dgs = [], []
    for j, yv in enumerate(ys_f32):
        a, b = offs[j], offs[j + 1]
        if y_ref is not None:
            y_ref[:, a:b] = (yv * sgate[:, a:b]).astype(BF16)
        dys.append(dy[:, a:b] * sgate[:, a:b])
        dgs.append(dy[:, a:b] * yv * dsilu[:, a:b])
    return dys, jnp.concatenate(dgs, axis=-1)


NEXT = 16


def _even_bwd1(dx, o, p, kv, ln_g, ln_b, wcat, bsg, hsel, bconv, wout, post_g, rider=None):
    s = dx.shape[0]
    nt = s // TS

    def body(dx_ref, o_ref, p_ref, ph_ref, dxn_ref, on_ref, pn_ref, kv_ref, lng, lnb, wcat_ref, bsg_ref, hsel_ref,
             bconv_ref, wout_ref, pg,
             dp_ref, do_ref, dpg_ref, dlng_ref, dlnb_ref, dwcat_ref, dbs_ref, dbconv_ref, dkv_ref, wbuf, dbuf):
        i = pl.program_id(0)
        _acc_init(i, (dpg_ref, dlng_ref, dlnb_ref, dwcat_ref, dbs_ref, dbconv_ref, dkv_ref))
        mx = _even_mix(i, p_ref, ph_ref, lng[...], lnb[...], wcat_ref, bsg_ref, bconv_ref, wbuf)
        q = p_ref[:, 5 * BW:5 * BW + XA]
        yx, probs = _xattn_fwd(q, kv_ref)
        gate = p_ref[:, 5 * BW + XA:EVEN_IN].astype(F32)
        do = _post_norm_bwd(dx_ref[...], o_ref[...], pg[...], dpg_ref)
        do_ref[...] = do
        (dya, dyb, dyx), dgate = _gate_bwd(do, wout_ref, (mx["ya"], mx["yb"], yx), gate, None)
        dp_ref[:, 0:BW] = (dya * mx["sg"]).astype(BF16)
        dsg = (dya * mx["u"]).astype(BF16)
        dvns = []
        for n in range(TS // CH):
            dsg_c = dsg[n * CH:(n + 1) * CH]
            dvst = _dot_tn(wcat_ref[...], dsg_c)
            dvn_c = jnp.where(mx["masks"][0], dvst[0:CH], 0.0)
            for h in range(1, 4):
                dvn_c = dvn_c + jnp.where(mx["masks"][h], dvst[h * CH:(h + 1) * CH], 0.0)
            dvns.append(dvn_c)
            dwcat_ref[...] += _dot_nt(dsg_c, mx["vsts"][n])
            dbs_ref[...] += _dot(dsg_c, hsel_ref[...])
        dvn = jnp.concatenate(dvns, axis=0)
        dlng_ref[...] += jnp.sum(dvn * mx["vhat"], axis=0, keepdims=True)
        dlnb_ref[...] += jnp.sum(dvn, axis=0, keepdims=True)
        dp_ref[:, BW:2 * BW] = _layer_norm_bwd(dvn, mx["vhat"], mx["rstd"], lng[...]).astype(BF16)
        dp_ref[:, 2 * BW:3 * BW] = (dyb * mx["conv"]).astype(BF16)
        dconv = dyb * mx["bg"]
        for k in range(3):
            dbconv_ref[k:k + 1, :] += jnp.sum(dconv * wbuf[pl.ds(HALO - 2 + k, TS), :], axis=0, keepdims=True)
        n_n, r_n = _post_norm(on_ref[...], pg[...])
        dn_n = dxn_ref[...] * pg[...]
        do_n = (r_n * (dn_n - n_n * jnp.mean(dn_n * n_n, axis=-1, keepdims=True))).astype(BF16)
        dy_n = _dot_nt(do_n, wout_ref[BW:2 * BW, :])
        g_n = pn_ref[:, 5 * BW + XA + BW:5 * BW + XA + 2 * BW].astype(F32)
        dconv_n = dy_n * (g_n * _sigmoid(g_n)) * pn_ref[:, 2 * BW:3 * BW].astype(F32)
        dbuf[0:TS, :] = dconv
        dbuf[TS:TS + NEXT, :] = jnp.where(i < nt - 1, dconv_n, 0.0)
        dw = (bconv_ref[2:3, :] * dconv + bconv_ref[1:2, :] * dbuf[pl.ds(1, TS), :]
              + bconv_ref[0:1, :] * dbuf[pl.ds(2, TS), :])
        dp_ref[:, 3 * BW:4 * BW] = (dw * mx["xin"]).astype(BF16)
        dp_ref[:, 4 * BW:5 * BW] = (dw * mx["cg"]).astype(BF16)
        dp_ref[:, 5 * BW:5 * BW + XA] = _xattn_bwd(dyx, q, probs, kv_ref, dkv_ref).astype(BF16)
        dp_ref[:, 5 * BW + XA:EVEN_IN] = dgate.astype(BF16)

    tile, halo = _tile_specs(s, EVEN_IN)
    row = pl.BlockSpec((TS, D), lambda i: (i, 0))
    vec = _const((1, BW))
    nxt = _halo_next(TS // NEXT, s // NEXT)

    def out(n):
        return pl.BlockSpec((TS, n), lambda i: (i, 0))

    return _host_call(
        body, grid=(nt,), name="even_bwd1", rider=rider,
        out_shape=(jax.ShapeDtypeStruct((s, EVEN_IN), BF16), jax.ShapeDtypeStruct((s, D), BF16),
                   jax.ShapeDtypeStruct((1, D), F32), jax.ShapeDtypeStruct((1, BW), F32),
                   jax.ShapeDtypeStruct((1, BW), F32), jax.ShapeDtypeStruct((CH, 4 * CH), F32),
                   jax.ShapeDtypeStruct((CH, 128), F32), jax.ShapeDtypeStruct((8, BW), F32),
                   jax.ShapeDtypeStruct((N_MEM, D), F32)),
        in_specs=[row, row, tile, halo, pl.BlockSpec((NEXT, D), nxt), pl.BlockSpec((NEXT, D), nxt),
                  pl.BlockSpec((NEXT, EVEN_IN), nxt), _const((N_MEM, D)), vec, vec, _const((CH, 4 * CH)),
                  _const((CH, BW)), _const((BW, 128)), _const((3, BW)), _resident((MIX, D)), _const((1, D))],
        out_specs=(out(EVEN_IN), out(D),
                   _const((1, D)), vec, vec, _const((CH, 4 * CH)), _const((CH, 128)), _const((8, BW)),
                   _const((N_MEM, D))),
        scratch_shapes=[pltpu.VMEM((HALO + TS, BW), F32), pltpu.VMEM((TS + NEXT, BW), F32)],
        args=(dx, o, p, p, dx, o, p, kv, ln_g, ln_b, wcat, bsg, hsel, bconv, wout, post_g))


def _odd_bwd1(dx, o, cv, p, kv, wbd, cscale, dww, dwb, ln_g, ln_b, pww, pwb, wout, post_g):
    s = dx.shape[0]

    def body(dx_ref, o_ref, cv_ref, p_ref, ph_ref, kv_ref, bands_ref, wbd_ref, cs, dww_ref, dwb_ref, lng, lnb,
             pww_ref, pwb_ref, wout_ref, pg,
             dpc_ref, tmpc_ref, tmpd_ref, do_ref, y_ref, dpg_ref, dcs_ref, dwbd_ref, ddww_ref, ddwb_ref, dlng_ref,
             dlnb_ref, dpww_ref, dpwb_ref, dkv_ref, gbuf, gsh, dcv_buf):
        i = pl.program_id(0)
        _acc_init(i, (dpg_ref, dcs_ref, dwbd_ref, ddww_ref, ddwb_ref, dlng_ref, dlnb_ref, dpww_ref, dpwb_ref,
                      dkv_ref))
        mx = _odd_mix(i, p_ref, ph_ref, bands_ref, wbd_ref, cs[...], dww_ref, dwb_ref[...], lng[...], lnb[...],
                      pww_ref, pwb_ref[...], gbuf, gsh, cv=cv_ref[...])
        q = p_ref[:, 3 * BW:3 * BW + XA]
        yx, probs = _xattn_fwd(q, kv_ref)
        gate = p_ref[:, 3 * BW + XA:ODD_IN].astype(F32)
        do = _post_norm_bwd(dx_ref[...], o_ref[...], pg[...], dpg_ref)
        do_ref[...] = do
        (dyc, dyd, dyx), dgate = _gate_bwd(do, wout_ref, (mx["yc"], mx["yd"], yx), gate, y_ref)
        dcs_ref[...] += jnp.sum(dyc * mx["pre"], axis=0, keepdims=True)
        dpre = (dyc * cs[...]).astype(BF16)
        dwbd_ref[...] += _dot_tn(mx["pooled_bf"], dpre)
        dpooled = _dot_nt(dpre, wbd_ref[...])
        tmpc_ref[...] = _pool_select([dpooled * c_ for c_ in mx["inv"]]).astype(BF16)
        dyd_bf = dyd.astype(BF16)
        dpwb_ref[...] += jnp.sum(dyd, axis=0, keepdims=True)
        dpww_ref[...] += _dot_tn(mx["zs"], dyd_bf)
        dzs = _dot_nt(dyd_bf, pww_ref[...])
        zl, szl = mx["zl"], mx["szl"]
        dzl = dzs * (szl * (1.0 + zl * (1.0 - szl)))
        dlng_ref[...] += jnp.sum(dzl * mx["zhat"], axis=0, keepdims=True)
        dlnb_ref[...] += jnp.sum(dzl, axis=0, keepdims=True)
        dcv = _layer_norm_bwd(dzl, mx["zhat"], mx["rstd"], lng[...])
        tmpd_ref[...] = dcv.astype(BF16)
        ddwb_ref[...] += jnp.sum(dcv, axis=0, keepdims=True)
        dcv_buf[...] = dcv
        _tap_sums(dcv_buf, gbuf, gsh, HALO - (CONF - 1), ddww_ref)
        dpc_ref[:, 0:XA] = _xattn_bwd(dyx, q, probs, kv_ref, dkv_ref).astype(BF16)
        dpc_ref[:, XA:XA + MIX] = dgate.astype(BF16)

    tile, halo = _tile_specs(s, ODD_IN)
    row = pl.BlockSpec((TS, D), lambda i: (i, 0))
    vec = _const((1, BW))

    def out(n):
        return pl.BlockSpec((TS, n), lambda i: (i, 0))

    return pl.pallas_call(
        body, grid=(s // TS,), name="odd_bwd1",
        out_shape=(jax.ShapeDtypeStruct((s, XA + MIX), BF16), jax.ShapeDtypeStruct((s, BW), BF16),
                   jax.ShapeDtypeStruct((s, BW), BF16), jax.ShapeDtypeStruct((s, D), BF16),
                   jax.ShapeDtypeStruct((s, MIX), BF16),
                   jax.ShapeDtypeStruct((1, D), F32), jax.ShapeDtypeStruct((1, BW), F32),
                   jax.ShapeDtypeStruct((BW, BW), F32), jax.ShapeDtypeStruct((8 * CONF, BW), F32),
                   jax.ShapeDtypeStruct((1, BW), F32), jax.ShapeDtypeStruct((1, BW), F32),
                   jax.ShapeDtypeStruct((1, BW), F32), jax.ShapeDtypeStruct((BW, BW), F32),
                   jax.ShapeDtypeStruct((1, BW), F32), jax.ShapeDtypeStruct((N_MEM, D), F32)),
        in_specs=[row, row, out(BW), tile, halo, _const((N_MEM, D)), _const((4, TS, HALO + TS)), _const((BW, BW)), vec,
                  _const((CONF, BW)), vec, vec, vec, _const((BW, BW)), vec, _resident((MIX, D)), _const((1, D))],
        out_specs=(out(XA + MIX), out(BW), out(BW), out(D), out(MIX),
                   _const((1, D)), vec, _const((BW, BW)), _const((8 * CONF, BW)), vec, vec, vec, _const((BW, BW)), vec,
                   _const((N_MEM, D))),
        scratch_shapes=[pltpu.VMEM((HALO + TS, BW), F32), pltpu.VMEM((7, SHIFT_ROWS, BW), F32),
                        pltpu.VMEM((TS, BW), F32)],
        compiler_params=_cp(("arbitrary",)),
    )(dx, o, cv, p, p, kv, _band_matrices(TS, False), wbd, cscale, dww, dwb, ln_g, ln_b, pww, pwb, wout, post_g)


def _halo_next(nblk_per_tile, nblk):
    return lambda i: (jnp.minimum((i + 1) * nblk_per_tile, nblk - 1), 0)


def _pre_norm_bwd(dh, x, pre_g, dres, dpre_ref):
    r = lax.rsqrt(jnp.mean(x * x, axis=-1, keepdims=True) + EPS)
    xh = x * r
    dpre_ref[...] += jnp.sum(dh * xh, axis=0, keepdims=True)
    dxh = dh * pre_g
    return dres + r * (dxh - xh * jnp.mean(dxh * xh, axis=-1, keepdims=True))


def _even_bwd2(dp, w_t, x, pre_g, dres, rider=None):
    s = x.shape[0]
    tm = min(512, s)

    def body(dp_ref, w_ref, x_ref, pg, dres_ref, dx_ref, dpre_ref):
        _acc_init(pl.program_id(0), (dpre_ref,))
        dh = _dot(dp_ref[...], w_ref[...])
        dx_ref[...] = _pre_norm_bwd(dh, x_ref[...], pg[...], dres_ref[...], dpre_ref)

    row = pl.BlockSpec((tm, D), lambda i: (i, 0))
    return _host_call(
        body, grid=(s // tm,), name="even_bwd2", rider=rider,
        out_shape=(jax.ShapeDtypeStruct((s, D), F32), jax.ShapeDtypeStruct((1, D), F32)),
        in_specs=[pl.BlockSpec((tm, EVEN_IN), lambda i: (i, 0)), _resident((EVEN_IN, D)), row, _const((1, D)), row],
        out_specs=(row, _const((1, D))),
        args=(dp, w_t, x, pre_g, dres))


def _odd_bwd2(dpc, tmpc, tmpd, p, dww, w_t, x, pre_g, dres):
    s = x.shape[0]
    nt = s // TS

    def body(dpc_ref, tc_ref, tch_ref, td_ref, tdh_ref, ga_ref, gb_ref, bands_ref, dww_ref, w_ref, x_ref, pg,
             dres_ref, dpb_ref, dx_ref, dpre_ref, dbuf, dsh):
        i = pl.program_id(0)
        _acc_init(i, (dpre_ref,))
        more = i < nt - 1
        e_bf = tc_ref[...]
        eh = tch_ref[...]
        ecat = jnp.concatenate([e_bf, jnp.where(more, eh, jnp.zeros_like(eh))], axis=0)
        dbuf[0:TS, :] = td_ref[...].astype(F32)
        dbuf[TS:TS + HALO, :] = jnp.where(more, tdh_ref[...].astype(F32), 0.0)
        sums = [_dot(bands_ref[w], ecat) for w in range(len(POOL_WINDOWS))]
        rows = _row_ids(i, TS) + 1
        cnt = _pool_select([jnp.minimum(rows, w).astype(F32) for w in POOL_WINDOWS])
        dzc = (_pool_select(sums) - e_bf.astype(F32) * cnt).astype(BF16)
        _shifted_copies(dbuf, dsh)
        dz = dww_ref[CONF - 1:CONF, :] * dbuf[pl.ds(0, TS), :]
        for sft in range(1, CONF):
            dz = dz + dww_ref[CONF - 1 - sft:CONF - sft, :] * _rows_at(dbuf, dsh, sft, TS)
        ga = ga_ref[...].astype(F32)
        sgb = _sigmoid(gb_ref[...].astype(F32))
        dga = (dz * sgb).astype(BF16)
        dgb = (dz * ga * sgb * (1.0 - sgb)).astype(BF16)
        dpb_ref[:, 0:BW] = dzc
        dpb_ref[:, BW:2 * BW] = dga
        dpb_ref[:, 2 * BW:3 * BW] = dgb
        dh = (_dot(dzc, w_ref[0:BW, :]) + _dot(dga, w_ref[BW:2 * BW, :]) + _dot(dgb, w_ref[2 * BW:3 * BW, :])
              + _dot(dpc_ref[...], w_ref[3 * BW:ODD_IN, :]))
        dx_ref[...] = _pre_norm_bwd(dh, x_ref[...], pg[...], dres_ref[...], dpre_ref)

    row = pl.BlockSpec((TS, D), lambda i: (i, 0))

    def tile(n, j=0):
        return pl.BlockSpec((TS, n), lambda i: (i, j))

    nxt = pl.BlockSpec((HALO, BW), _halo_next(TS // HALO, s // HALO))
    return pl.pallas_call(
        body, grid=(nt,), name="odd_bwd2",
        out_shape=(jax.ShapeDtypeStruct((s, 3 * BW), BF16), jax.ShapeDtypeStruct((s, D), F32),
                   jax.ShapeDtypeStruct((1, D), F32)),
        in_specs=[tile(XA + MIX), tile(BW), nxt, tile(BW), nxt, tile(BW, 1), tile(BW, 2), _const((4, TS, HALO + TS)),
                  _const((CONF, BW)), _resident((ODD_IN, D)), row, _const((1, D)), row],
        out_specs=(tile(3 * BW), row, _const((1, D))),
        scratch_shapes=[pltpu.VMEM((TS + HALO, BW), F32), pltpu.VMEM((7, SHIFT_ROWS, BW), F32)],
        compiler_params=_cp(("arbitrary",)),
    )(dpc, tmpc, tmpc, tmpd, tmpd, p, p, _band_matrices(TS, True), dww, w_t, x, pre_g, dres)


def _grad_tn(a, b, tm, out=None, rows=None, row0=0, name="grad_tn", rider=None):
    s, m = a.shape
    n = b.shape[1]
    ts = min(2048, s)
    rows = m if rows is None else rows
    assert m % tm == 0 and s % ts == 0
    ns = s // ts
    if row0 % tm == 0:
        out_spec = pl.BlockSpec((tm, n), lambda i, k: (row0 // tm + i, 0))
    else:
        align = 16
        assert row0 % align == 0 and tm % align == 0
        out_spec = pl.BlockSpec((pl.Element(tm), pl.Element(n)),
                                lambda i, k: (pl.multiple_of(row0 + i * tm, align), 0))

    def body(*refs):
        a_ref, b_ref = refs[0], refs[1]
        o_ref, acc = refs[-2], refs[-1]
        k = pl.program_id(1)

        @pl.when(k == 0)
        def _():
            acc[...] = jnp.zeros_like(acc)

        acc[...] += _dot_tn(a_ref[...], b_ref[...])

        @pl.when(k == ns - 1)
        def _():
            o_ref[...] = acc[...].astype(BF16)

    in_specs = [pl.BlockSpec((ts, tm), lambda i, k: (k, i)), pl.BlockSpec((ts, n), lambda i, k: (k, 0))]
    args = [a, b]
    aliases = {}
    if out is not None:
        in_specs.append(pl.BlockSpec(memory_space=pltpu.HBM))
        args.append(out)
        aliases = {2: 0}
    (res,), got = _host_call(
        body, grid=(m // tm, ns), name=name, rider=rider, aliases=aliases,
        out_shape=(jax.ShapeDtypeStruct((rows, n), BF16),), in_specs=in_specs, out_specs=(out_spec,),
        scratch_shapes=[pltpu.VMEM((tm, n), F32)], args=args)
    return res if rider is None else (res, got)


def _place():
    x, y, c = lax.axis_index("x"), lax.axis_index("y"), lax.axis_index("c")
    chips = [(1 - x, y), (x, 1 - y), (1 - x, 1 - y)]
    return x, y, c, chips


def _hbm_specs(n):
    return [pl.BlockSpec(memory_space=pltpu.HBM)] * n


def _row_tile(r):
    for cand in (512, 400, 304, 256, 192, 128, 96, 16):
        if r % cand == 0:
            return cand
    raise ValueError(r)


def _place_shard(shard, place, dtype, name, after=None):
    r, cc = shard.shape
    tr = _row_tile(r)
    nt = r // tr

    def body(place_ref, s_ref, *rest):
        rest[-1][...] = s_ref[...].astype(dtype)

    in_specs = [pl.BlockSpec((tr, cc), lambda i, pr: (i, 0))]
    args = [shard]
    if after is not None:
        in_specs.append(pl.BlockSpec(after.shape, lambda i, pr: (0, 0)))
        args.append(after)
    return pl.pallas_call(
        body, name=name, out_shape=jax.ShapeDtypeStruct((N_CHIPS * r, cc), dtype),
        grid_spec=pltpu.PrefetchScalarGridSpec(
            num_scalar_prefetch=1, grid=(nt,), in_specs=in_specs,
            out_specs=pl.BlockSpec((tr, cc), lambda i, pr: (pr[1] * nt + i, 0))),
        compiler_params=_cp(("arbitrary",)),
    )(place, *args)


class _GatherRider:
    has_mid = True

    def __init__(self, fulls):
        n = len(fulls)
        self.inputs = list(fulls)
        self.out_shapes = [jax.ShapeDtypeStruct(a.shape, a.dtype) for a in fulls]
        self.aliases = {a: a for a in range(n)}
        self.sems = [pltpu.SemaphoreType.DMA((6 * n,)), pltpu.SemaphoreType.DMA((6 * n,))]
        self.block_rows = [a.shape[0] // N_CHIPS for a in fulls]

    def _ctx(self, outs, sems):
        send_sems, recv_sems = sems
        x, y, c, chips = _place()

        def rows(a, k, half):
            r = self.block_rows[a]
            return outs[a].at[pl.ds(k * r + half * (r // 2), r // 2)]

        def copy(a, j, blk, to):
            return pltpu.make_async_remote_copy(src_ref=blk, dst_ref=blk, send_sem=send_sems.at[a * 6 + j],
                                                recv_sem=recv_sems.at[a * 6 + j], device_id=to, device_id_type=MESH)

        return x, y, c, chips, rows, copy

    def start(self, ins, outs, sems, peers=(0, 1, 2)):
        x, y, c, chips, rows, copy = self._ctx(outs, sems)
        for j in peers:
            for a in range(len(outs)):
                copy(a, j, rows(a, 2 * x + y, c), (*chips[j], c)).start()

    def mid(self, ins, outs, sems, peers=(0, 1, 2)):
        x, y, c, chips, rows, copy = self._ctx(outs, sems)
        for j in peers:
            px, py = chips[j]
            for a in range(len(outs)):
                copy(a, j, rows(a, 2 * px + py, c), (px, py, c)).wait_recv()
                copy(a, 3 + j, rows(a, 2 * px + py, c), (x, y, 1 - c)).start()

    def wait_forwarded(self, outs, sems, peers=(0, 1, 2)):
        x, y, c, chips, rows, copy = self._ctx(outs, sems)
        for j in peers:
            px, py = chips[j]
            for a in range(len(outs)):
                copy(a, 3 + j, rows(a, 2 * px + py, 1 - c), (x, y, 1 - c)).wait_recv()

    def wait_sends(self, outs, sems):
        x, y, c, chips, rows, copy = self._ctx(outs, sems)
        for j, (px, py) in enumerate(chips):
            for a in range(len(outs)):
                copy(a, j, rows(a, 2 * x + y, c), (px, py, c)).wait_send()
                copy(a, 3 + j, rows(a, 2 * px + py, c), (x, y, 1 - c)).wait_send()

    def end(self, ins, outs, sems):
        self.wait_forwarded(outs, sems)
        self.wait_sends(outs, sems)


def _swap_halves(grads, small, name):
    n = len(grads)
    arrs = list(grads) + ([small] if small is not None else [])
    m = len(arrs)

    def body(*refs):
        ins, outs = refs[:m], refs[m:2 * m]
        send_sems, recv_sems = refs[2 * m:]
        x, y, c, _ = _place()
        sibling = (x, y, 1 - c)
        cps = []
        for a in range(m):
            src = ins[a].at[:, 1 - c] if a < n else ins[a]
            cp = pltpu.make_async_remote_copy(src_ref=src, dst_ref=outs[a], send_sem=send_sems.at[a],
                                              recv_sem=recv_sems.at[a], device_id=sibling, device_id_type=MESH)
            cp.start()
            cps.append(cp)
        for cp in cps:
            cp.wait_recv()
        for cp in cps:
            cp.wait_send()

    outs = tuple(jax.ShapeDtypeStruct((g.shape[0],) + g.shape[2:], g.dtype) for g in grads)
    if small is not None:
        outs += (jax.ShapeDtypeStruct(small.shape, small.dtype),)
    return pl.pallas_call(
        body, name=name, out_shape=outs, in_specs=_hbm_specs(m), out_specs=tuple(_hbm_specs(m)),
        scratch_shapes=[pltpu.SemaphoreType.DMA((m,)), pltpu.SemaphoreType.DMA((m,))],
    )(*arrs)


def _pair_sum(g, recv, place, name):
    _, _, h, cc = g.shape
    th = _row_tile(h)

    def body(c_ref, g_ref, r_ref, o_ref):
        o_ref[...] = (g_ref[...].astype(F32) + r_ref[...].astype(F32)).astype(o_ref.dtype)

    return pl.pallas_call(
        body, name=name, out_shape=jax.ShapeDtypeStruct(recv.shape, recv.dtype),
        grid_spec=pltpu.PrefetchScalarGridSpec(
            num_scalar_prefetch=1, grid=(N_CHIPS, h // th),
            in_specs=[pl.BlockSpec((None, None, th, cc), lambda k, r, c_ref: (k, c_ref[0], r, 0)),
                      pl.BlockSpec((None, th, cc), lambda k, r, c_ref: (k, r, 0))],
            out_specs=pl.BlockSpec((None, th, cc), lambda k, r, c_ref: (k, r, 0))),
        compiler_params=_cp(("arbitrary", "arbitrary")),
    )(place, g, recv)


def _finish_reduce(pack, halves):
    rows, cc = pack.shape
    hs = rows // 2
    n = len(halves)

    def body(*refs):
        pack_ref = refs[0]
        out_ref = refs[1 + n]
        big = refs[2 + n:2 + 2 * n]
        sib_ref, parts_ref, send_sems, recv_sems, big_send, big_recv = refs[2 + 2 * n:]
        x, y, c, chips = _place()
        me_k = 2 * x + y
        sibling = (x, y, 1 - c)
        mine = pl.ds(pl.multiple_of(c * hs, hs), hs)
        theirs = pl.ds(pl.multiple_of((1 - c) * hs, hs), hs)
        shared = [pltpu.make_async_remote_copy(src_ref=big[a].at[c], dst_ref=big[a].at[c], send_sem=big_send.at[a],
                                               recv_sem=big_recv.at[a], device_id=sibling, device_id_type=MESH)
                  for a in range(n)]
        for cp in shared:
            cp.start()
        first = pltpu.make_async_remote_copy(src_ref=pack_ref, dst_ref=sib_ref, send_sem=send_sems.at[0],
                                             recv_sem=recv_sems.at[0], device_id=sibling, device_id_type=MESH)
        first.start()
        first.wait()
        parts_ref[me_k] = pack_ref[mine, :] + sib_ref[mine, :]
        cps = [pltpu.make_async_remote_copy(src_ref=parts_ref.at[me_k], dst_ref=parts_ref.at[me_k],
                                            send_sem=send_sems.at[1 + j], recv_sem=recv_sems.at[1 + j],
                                            device_id=(px, py, c), device_id_type=MESH)
               for j, (px, py) in enumerate(chips)]
        for cp in cps:
            cp.start()
        for j, (px, py) in enumerate(chips):
            pltpu.make_async_remote_copy(src_ref=parts_ref.at[2 * px + py], dst_ref=parts_ref.at[2 * px + py],
                                         send_sem=send_sems.at[1 + j], recv_sem=recv_sems.at[1 + j],
                                         device_id=(px, py, c), device_id_type=MESH).wait_recv()
        for cp in cps:
            cp.wait_send()
        out_ref[mine, :] = ((parts_ref[0] + parts_ref[1]) + parts_ref[2]) + parts_ref[3]
        last = pltpu.make_async_remote_copy(src_ref=out_ref.at[mine], dst_ref=out_ref.at[mine],
                                            send_sem=send_sems.at[4], recv_sem=recv_sems.at[4], device_id=sibling,
                                            device_id_type=MESH)
        last.start()
        pltpu.make_async_remote_copy(src_ref=out_ref.at[theirs], dst_ref=out_ref.at[theirs],
                                     send_sem=send_sems.at[4], recv_sem=recv_sems.at[4], device_id=sibling,
                                     device_id_type=MESH).wait_recv()
        last.wait_send()
        for a in range(n):
            pltpu.make_async_remote_copy(src_ref=big[a].at[1 - c], dst_ref=big[a].at[1 - c], send_sem=big_send.at[a],
                                         recv_sem=big_recv.at[a], device_id=sibling,
                                         device_id_type=MESH).wait_recv()
        for cp in shared:
            cp.wait_send()

    vmem = pl.BlockSpec(memory_space=pltpu.VMEM)
    res = pl.pallas_call(
        body, name="finish_reduce",
        out_shape=(jax.ShapeDtypeStruct(pack.shape, pack.dtype),)
        + tuple(jax.ShapeDtypeStruct(g.shape, g.dtype) for g in halves),
        in_specs=[vmem] + _hbm_specs(n), out_specs=(vmem,) + tuple(_hbm_specs(n)),
        input_output_aliases={1 + a: 1 + a for a in range(n)},
        scratch_shapes=[pltpu.VMEM((rows, cc), F32), pltpu.VMEM((N_CHIPS, hs, cc), F32),
                        pltpu.SemaphoreType.DMA((5,)), pltpu.SemaphoreType.DMA((5,)),
                        pltpu.SemaphoreType.DMA((n,)), pltpu.SemaphoreType.DMA((n,))],
        compiler_params=_cp(),
    )(pack, *halves)
    return res[0], tuple(res[1:])


class _ExchangeRider:
    has_mid = False

    def __init__(self, sums):
        self.inputs = list(sums)
        self.out_shapes = [jax.ShapeDtypeStruct((3,) + g.shape[1:], g.dtype) for g in sums]
        m = len(self.inputs)
        self.aliases = {}
        self.sems = [pltpu.SemaphoreType.DMA((3 * m,)), pltpu.SemaphoreType.DMA((3 * m,))]

    def _copies(self, ins, outs, sems):
        send_sems, recv_sems = sems
        _, _, c, chips = _place()
        return [pltpu.make_async_remote_copy(
            src_ref=ins[a].at[2 * px + py], dst_ref=outs[a].at[j], send_sem=send_sems.at[a * 3 + j],
            recv_sem=recv_sems.at[a * 3 + j], device_id=(px, py, c), device_id_type=MESH)
            for j, (px, py) in enumerate(chips) for a in range(len(ins))]

    def start(self, ins, outs, sems):
        for cp in self._copies(ins, outs, sems):
            cp.start()

    def end(self, ins, outs, sems):
        cps = self._copies(ins, outs, sems)
        for cp in cps:
            cp.wait_recv()
        for cp in cps:
            cp.wait_send()


def _chip_sum(own, parts, place, name):
    npart, h, cc = parts.shape
    th = _row_tile(h)

    def body(place_ref, own_ref, p_ref, o_ref):
        acc = own_ref[...].astype(F32) + p_ref[0].astype(F32)
        for k in range(1, npart):
            acc = acc + p_ref[k].astype(F32)
        o_ref[...] = acc

    return pl.pallas_call(
        body, name=name, out_shape=jax.ShapeDtypeStruct((2, h, cc), F32),
        grid_spec=pltpu.PrefetchScalarGridSpec(
            num_scalar_prefetch=1, grid=(h // th,),
            in_specs=[pl.BlockSpec((None, th, cc), lambda r, pr: (pr[1], r, 0)),
                      pl.BlockSpec((npart, th, cc), lambda r, pr: (0, r, 0))],
            out_specs=pl.BlockSpec((None, th, cc), lambda r, pr: (pr[0], r, 0))),
        compiler_params=_cp(("arbitrary",)),
    )(place, own, parts)


def _adamw_math(w, g, m, v):
    m = ADAM_B1 * m + (1.0 - ADAM_B1) * g
    v = ADAM_B2 * v + (1.0 - ADAM_B2) * (g * g)
    m_hat = m / (1.0 - ADAM_B1 ** ADAM_STEP)
    v_hat = v / (1.0 - ADAM_B2 ** ADAM_STEP)
    delta = -ADAM_LR * (m_hat / (jnp.sqrt(v_hat) + ADAM_EPS) + ADAM_WD * w)
    return delta, m, v


def _adamw_big(w, g, m, v, name):
    r, cc = w.shape
    tr = min(_row_tile(r), 256) if r % 256 == 0 else _row_tile(r)

    def body(w_ref, g_ref, m_ref, v_ref, go_ref, d_ref, mo_ref, vo_ref):
        g = g_ref[...]
        d, mm, vv = _adamw_math(w_ref[...], g, m_ref[...], v_ref[...])
        go_ref[...] = g
        d_ref[...] = d
        mo_ref[...] = mm
        vo_ref[...] = vv

    blk = pl.BlockSpec((tr, cc), lambda i: (i, 0))
    sd = jax.ShapeDtypeStruct((r, cc), F32)
    return pl.pallas_call(body, grid=(r // tr,), name=name, out_shape=(sd, sd, sd, sd), in_specs=[blk] * 4,
                          out_specs=(blk, blk, blk, blk), compiler_params=_cp(("arbitrary",)))(w, g, m, v)


def _adamw_small(ws, gs, ms, vs):
    n = len(ws)

    def body(*refs):
        for a in range(n):
            w_ref, g_ref, m_ref, v_ref = refs[4 * a:4 * a + 4]
            d_ref, mo_ref, vo_ref = refs[4 * n + 3 * a:4 * n + 3 * a + 3]
            d, mm, vv = _adamw_math(w_ref[...], g_ref[...], m_ref[...], v_ref[...])
            d_ref[...] = d
            mo_ref[...] = mm
            vo_ref[...] = vv

    args, outs = [], []
    for a in range(n):
        args += [ws[a], gs[a], ms[a], vs[a]]
        outs += [jax.ShapeDtypeStruct(ws[a].shape, F32)] * 3
    res = pl.pallas_call(body, name="adamw_small", out_shape=tuple(outs), compiler_params=_cp())(*args)
    return [res[3 * a:3 * a + 3] for a in range(n)]


def _flat_pack(arrs, rows):
    flat = jnp.concatenate([a.reshape(-1) for a in arrs])
    return jnp.pad(flat, (0, rows * D - flat.shape[0])).reshape(rows, D)


def _flat_unpack(flat, shapes):
    out, off = [], 0
    for shp in shapes:
        size = 1
        for d_ in shp:
            size *= d_
        out.append(flat[off:off + size].reshape(shp))
        off += size
    return out


SMALL_EVEN = ("even_pre_g", "even_a_ln_g", "even_a_ln_b", "even_a_ws", "even_a_bs", "even_b_conv", "even_mem_g",
              "even_post_g")
SMALL_ODD = ("odd_pre_g", "odd_c_wgrp", "odd_c_scale", "odd_d_dw_w", "odd_d_dw_b", "odd_d_ln_g", "odd_d_ln_b",
             "odd_d_pw_b", "odd_mem_g", "odd_post_g")
BIG = ("even_w_in", "even_w_kv", "even_w_out", "odd_w_in", "odd_d_pw_w", "odd_w_kv", "odd_w_out")
WEIGHTS = ("even_pre_g", "even_w_in", "even_a_ln_g", "even_a_ln_b", "even_a_ws", "even_a_bs", "even_b_conv",
           "even_mem_g", "even_w_kv", "even_w_out", "even_post_g", "odd_pre_g", "odd_w_in", "odd_c_wgrp",
           "odd_c_scale", "odd_d_dw_w", "odd_d_dw_b", "odd_d_ln_g", "odd_d_ln_b", "odd_d_pw_w", "odd_d_pw_b",
           "odd_mem_g", "odd_w_kv", "odd_w_out", "odd_post_g")
PACKED = (("even_b_conv", (3, 192)), ("odd_pre_g", (1, 256)), ("odd_c_scale", (1, 192)), ("odd_d_dw_w", (31, 192)),
          ("odd_d_dw_b", (1, 192)), ("odd_d_ln_g", (1, 192)), ("odd_d_ln_b", (1, 192)), ("odd_d_pw_b", (1, 192)),
          ("odd_mem_g", (1, 256)), ("odd_post_g", (1, 256)))
PACK_ROWS = 16
SMALL_ROWS = 256


def _four(g):
    return g.reshape(N_CHIPS, 2, g.shape[0] // (2 * N_CHIPS), g.shape[1])


def _step(x, mem, target, w, place):
    wt = {}
    pack = _flat_pack([w[n][0] for n, _ in PACKED], PACK_ROWS)
    shards = {"even_w_in_t": w["even_w_in"][0].T, "odd_w_in_t": w["odd_w_in"][0].T, "even_w_kv": w["even_w_kv"][0],
              "odd_w_kv": w["odd_w_kv"][0], "even_w_out": w["even_w_out"][0], "odd_w_out": w["odd_w_out"][0],
              "odd_d_pw_w": w["odd_d_pw_w"][0]}
    placed = {n: _place_shard(shards[n], place, BF16, "place_" + n) for n in ("even_w_in_t", "even_w_kv", "even_w_out")}
    placed["pack"] = _place_shard(pack, place, F32, "place_pack")

    order, group = _stream_tables(place[0], place[1], EVEN_IN)
    p_e, h_e, (wt["even_w_in_t"], packs), (wt["even_w_kv"], wt["even_w_out"]) = _in_fwd_streamed(
        x, w["even_pre_g"], [placed["even_w_in_t"], placed["pack"]], [placed["even_w_kv"], placed["even_w_out"]],
        order, group, "even_in_streamed")
    for n in ("odd_w_in_t", "odd_w_kv", "odd_w_out", "odd_d_pw_w"):
        placed[n] = _place_shard(shards[n], place, BF16, "place_" + n, after=p_e[0:16, 0:128])
    packs = packs.reshape(N_CHIPS, PACK_ROWS * D)
    per_chip = [_flat_unpack(packs[k], [shp for _, shp in PACKED]) for k in range(N_CHIPS)]
    for a, (name, _) in enumerate(PACKED):
        wt[name] = jnp.concatenate([per_chip[k][a] for k in range(N_CHIPS)], axis=-1)
    for name in ("even_pre_g", "even_a_ln_g", "even_a_ln_b", "even_mem_g", "even_post_g"):
        wt[name] = w[name]

    tril = jnp.tril(jnp.ones((CH, CH), dtype=bool))
    wcat = jnp.where(tril[None], w["even_a_ws"][0], 0.0).transpose(1, 0, 2).reshape(CH, 4 * CH).astype(BF16)
    bsg = jnp.repeat(w["even_a_bs"][0].T, BW // 4, axis=1)
    hsel = (jnp.arange(BW)[:, None] // (BW // 4) == jnp.arange(128)[None, :]).astype(BF16)
    g4 = BW // 4
    eye = jnp.eye(4, dtype=F32)
    wbd = (w["odd_c_wgrp"][0][:, :, None, :] * eye[:, None, :, None]).reshape(BW, BW).astype(BF16)

    kv_e = _kv_fwd(mem, wt["even_mem_g"], wt["even_w_kv"], "even_kv")
    (x1, o_e, y_e), (wt["odd_w_in_t"],) = _even_fwd(
        x, p_e, kv_e, wt["even_a_ln_g"], wt["even_a_ln_b"], wcat, bsg, wt["even_b_conv"], wt["even_w_out"],
        wt["even_post_g"], rider=_GatherRider([placed["odd_w_in_t"]]))
    names = ("odd_w_out", "odd_d_pw_w", "odd_w_kv")
    (p_o, h_o), got = _in_fwd(x1, wt["odd_pre_g"], wt["odd_w_in_t"], "odd_in",
                              rider=_GatherRider([placed[n] for n in names]))
    wt.update(zip(names, got))
    kv_o = _kv_fwd(mem, wt["odd_mem_g"], wt["odd_w_kv"], "odd_kv")
    dx2, o_o, cv_o, loss = _odd_fwd(x1, p_o, kv_o, wbd, wt["odd_c_scale"], wt["odd_d_dw_w"], wt["odd_d_dw_b"],
                                    wt["odd_d_ln_g"], wt["odd_d_ln_b"], wt["odd_d_pw_w"], wt["odd_d_pw_b"],
                                    wt["odd_w_out"], wt["odd_post_g"], target)
    (dpc_o, tmpc, tmpd, do_o, y_o, g_post_o, g_cs, g_wbd, g_dww, g_dwb, g_lng_o, g_lnb_o, g_pww, g_pwb,
     dkv_o) = _odd_bwd1(dx2, o_o, cv_o, p_o, kv_o, wbd, wt["odd_c_scale"], wt["odd_d_dw_w"], wt["odd_d_dw_b"],
                        wt["odd_d_ln_g"], wt["odd_d_ln_b"], wt["odd_d_pw_w"], wt["odd_d_pw_b"], wt["odd_w_out"],
                        wt["odd_post_g"])
    dpb_o, dx1, g_pre_o = _odd_bwd2(dpc_o, tmpc, tmpd, p_o, wt["odd_d_dw_w"], wt["odd_w_in_t"], x1,
                                    wt["odd_pre_g"], dx2)
    g_win_o = _grad_tn(dpb_o, h_o, 768, rows=ODD_IN, name="odd_gw_in_b")
    g_win_o = _grad_tn(dpc_o, h_o, 1280, out=g_win_o, rows=ODD_IN, row0=3 * BW, name="odd_gw_in_c")
    g_wout_o = _grad_tn(y_o, do_o, 1024, name="odd_gw_out")
    g_wkv_o, g_memg_o = _kv_bwd(mem, wt["odd_mem_g"], wt["odd_w_kv"], dkv_o, "odd_kv_bwd")
    big_o = [_four(g) for g in (g_win_o, g_pww.astype(BF16), g_wkv_o, g_wout_o)]
    recv_o = _swap_halves(big_o, None, "swap_halves_odd")
    sums_o = [_pair_sum(big_o[a], recv_o[a], place, "pair_sum_odd_%d" % a) for a in range(len(big_o))]
    (dp_e, do_e, g_post_e, g_lng_e, g_lnb_e, g_wcat, g_bs, g_bconv,
     dkv_e), parts_o = _even_bwd1(dx1, o_e, p_e, kv_e, wt["even_a_ln_g"], wt["even_a_ln_b"], wcat, bsg, hsel,
                                  wt["even_b_conv"], wt["even_w_out"], wt["even_post_g"],
                                  rider=_ExchangeRider(sums_o))
    halves_o = [_chip_sum(sums_o[a], parts_o[a], place, "chip_sum_odd_%d" % a) for a in range(len(big_o))]
    g_wout_e = _grad_tn(y_e, do_e, 1024, name="even_gw_out")
    g_wkv_e, g_memg_e = _kv_bwd(mem, wt["even_mem_g"], wt["even_w_kv"], dkv_e, "even_kv_bwd")
    big_x = [_four(g) for g in (g_wkv_e, g_wout_e)]
    recv_x = _swap_halves(big_x, None, "swap_halves_kv_out")
    sums_x = [_pair_sum(big_x[a], recv_x[a], place, "pair_sum_kv_out_%d" % a) for a in range(len(big_x))]
    g_win_e, parts_x = _grad_tn(dp_e, h_e, 1280, name="even_gw_in", rider=_ExchangeRider(sums_x))
    halves_x = [_chip_sum(sums_x[a], parts_x[a], place, "chip_sum_kv_out_%d" % a) for a in range(len(big_x))]
    big_e = [_four(g_win_e)]
    recv_e = _swap_halves(big_e, None, "swap_halves_even")
    sums_e = [_pair_sum(big_e[0], recv_e[0], place, "pair_sum_even_w_in")]
    (dx0, g_pre_e), parts_e = _even_bwd2(dp_e, wt["even_w_in_t"], x, wt["even_pre_g"], dx1,
                                         rider=_ExchangeRider(sums_e))
    halves_e = [_chip_sum(sums_e[0], parts_e[0], place, "chip_sum_even_w_in")]

    g_aws = jnp.where(tril[None], g_wcat.reshape(CH, 4, CH).transpose(1, 0, 2), 0.0)
    g_wgrp = jnp.stack([lax.dynamic_slice(g_wbd, (g * g4, g * g4), (g4, g4)) for g in range(4)])
    small = {
        "even_pre_g": g_pre_e, "even_a_ln_g": g_lng_e, "even_a_ln_b": g_lnb_e, "even_a_ws": g_aws,
        "even_a_bs": g_bs[:, 0:4].T, "even_b_conv": g_bconv[0:3], "even_mem_g": g_memg_e, "even_post_g": g_post_e,
        "odd_pre_g": g_pre_o, "odd_c_wgrp": g_wgrp, "odd_c_scale": g_cs, "odd_d_dw_w": g_dww.reshape(CONF, 8, BW).sum(axis=1),
        "odd_d_dw_b": g_dwb, "odd_d_ln_g": g_lng_o, "odd_d_ln_b": g_lnb_o, "odd_d_pw_b": g_pwb,
        "odd_mem_g": g_memg_o, "odd_post_g": g_post_o,
    }
    small_names = SMALL_EVEN + SMALL_ODD
    small_pack = _flat_pack([small[n] for n in small_names] + [loss[0, 0].reshape(1)], SMALL_ROWS)
    small_total, full = _finish_reduce(small_pack, halves_e + halves_x + halves_o)
    order = ("even_w_in", "even_w_kv", "even_w_out", "odd_w_in", "odd_d_pw_w", "odd_w_kv", "odd_w_out")
    gbig = {n: full[a].reshape(full[a].shape[1] * 2, full[a].shape[2]) for a, n in enumerate(order)}
    return dx0, gbig, small_total.reshape(-1), [small[n].shape for n in small_names]


def kernel(x, mem, even_pre_g, even_w_in, even_a_ln_g, even_a_ln_b, even_a_ws, even_a_bs, even_b_conv, even_mem_g, even_w_kv, even_w_out, even_post_g, odd_pre_g, odd_w_in, odd_c_wgrp, odd_c_scale, odd_d_dw_w, odd_d_dw_b, odd_d_ln_g, odd_d_ln_b, odd_d_pw_w, odd_d_pw_b, odd_mem_g, odd_w_kv, odd_w_out, odd_post_g, loss_target, m_even_pre_g, m_even_w_in, m_even_a_ln_g, m_even_a_ln_b, m_even_a_ws, m_even_a_bs, m_even_b_conv, m_even_mem_g, m_even_w_kv, m_even_w_out, m_even_post_g, m_odd_pre_g, m_odd_w_in, m_odd_c_wgrp, m_odd_c_scale, m_odd_d_dw_w, m_odd_d_dw_b, m_odd_d_ln_g, m_odd_d_ln_b, m_odd_d_pw_w, m_odd_d_pw_b, m_odd_mem_g, m_odd_w_kv, m_odd_w_out, m_odd_post_g, v_even_pre_g, v_even_w_in, v_even_a_ln_g, v_even_a_ln_b, v_even_a_ws, v_even_a_bs, v_even_b_conv, v_even_mem_g, v_even_w_kv, v_even_w_out, v_even_post_g, v_odd_pre_g, v_odd_w_in, v_odd_c_wgrp, v_odd_c_scale, v_odd_d_dw_w, v_odd_d_dw_b, v_odd_d_ln_g, v_odd_d_ln_b, v_odd_d_pw_w, v_odd_d_pw_b, v_odd_mem_g, v_odd_w_kv, v_odd_w_out, v_odd_post_g):
    given = dict(locals())
    w = {n: given[n] for n in WEIGHTS}
    mom = {n: given["m_" + n] for n in WEIGHTS}
    var = {n: given["v_" + n] for n in WEIGHTS}

    x_, y_, c_ = lax.axis_index("x"), lax.axis_index("y"), lax.axis_index("c")
    chip = 2 * x_ + y_
    place = jnp.stack([c_, chip]).astype(jnp.int32)
    grad_x, gbig, gsmall_flat, small_shapes = _step(x[0], mem[0], loss_target[0], w, place)

    names = SMALL_EVEN + SMALL_ODD
    grads = {}
    unpacked = _flat_unpack(gsmall_flat, small_shapes + [(1,)])
    loss = unpacked[-1][0]
    for n, g in zip(names, unpacked[:-1]):
        shard_shape = w[n].shape[1:]
        if g.shape[-1] != shard_shape[-1]:
            g = lax.dynamic_slice_in_dim(g, chip * shard_shape[-1], shard_shape[-1], axis=g.ndim - 1)
        grads[n] = g.reshape(shard_shape)

    def two_d(a):
        return a.reshape(-1, a.shape[-1])

    upd = {}
    for n in BIG:
        if n.endswith("w_in"):
            res = _adamw_big(w[n][0].T, gbig[n], mom[n][0].T, var[n][0].T, "adamw_" + n)
            res = tuple(r.T for r in res)
        else:
            res = _adamw_big(w[n][0], gbig[n], mom[n][0], var[n][0], "adamw_" + n)
        grads[n], upd[n] = res[0], res[1:]
    res = _adamw_small([two_d(w[n][0]) for n in names], [two_d(grads[n]) for n in names],
                       [two_d(mom[n][0]) for n in names], [two_d(var[n][0]) for n in names])
    for n, r in zip(names, res):
        upd[n] = r

    outs = [loss, grad_x[None]]
    outs += [grads[n].reshape(w[n].shape) for n in WEIGHTS]
    for j in range(3):
        outs += [upd[n][j].reshape(w[n].shape) for n in WEIGHTS]
    return tuple(outs)
```

```python
import jax
import jax.numpy as jnp
from jax import lax
from jax.experimental import pallas as pl
from jax.experimental.pallas import tpu as pltpu

F32 = jnp.float32
BF16 = jnp.bfloat16
MESH = pl.DeviceIdType.MESH

D = 1024
N_MEM = 256
MIX = 2048
XA = 512
HD = 128
BW = 768
CH = 128
EPS = 1e-6
SCALE = HD ** -0.5
POOL_WINDOWS = (2, 4, 8, 16)
CONF = 31
EVEN_IN = 6400
ODD_IN = 4864
N_CHIPS = 4

ADAM_LR = 0.001
ADAM_B1 = 0.9
ADAM_B2 = 0.999
ADAM_EPS = 1e-08
ADAM_WD = 0.01
ADAM_STEP = 10

TS = 256
HALO = 32
VMEM_LIMIT = 56 * 1024 * 1024


def _cp(sem=None):
    return pltpu.CompilerParams(dimension_semantics=sem, vmem_limit_bytes=VMEM_LIMIT)


def _dot(a, b):
    return jnp.dot(a, b, preferred_element_type=F32)


def _dot_nt(a, b):
    return lax.dot_general(a, b, (((1,), (1,)), ((), ())), preferred_element_type=F32)


def _dot_tn(a, b):
    return lax.dot_general(a, b, (((0,), (0,)), ((), ())), preferred_element_type=F32)


def _sigmoid(x):
    return 1.0 / (1.0 + jnp.exp(-x))


def _resident(shape):
    return pl.BlockSpec(shape, lambda *_: (0,) * len(shape), pipeline_mode=pl.Buffered(1))


def _const(shape):
    return pl.BlockSpec(shape, lambda *_: (0,) * len(shape))


def _kv_fwd(mem, mem_g, wkv, name):
    def body(mem_ref, g_ref, w_ref, kv_ref):
        m = mem_ref[...]
        r = lax.rsqrt(jnp.mean(m * m, axis=-1, keepdims=True) + EPS)
        mn = (m * r * g_ref[...]).astype(BF16)
        kv_ref[...] = _dot(mn, w_ref[...]).astype(BF16)

    return pl.pallas_call(body, out_shape=jax.ShapeDtypeStruct((N_MEM, D), BF16), name=name,
                          compiler_params=_cp())(mem, mem_g, wkv)


def _kv_bwd(mem, mem_g, wkv, dkv, name):
    def body(mem_ref, g_ref, w_ref, dkv_ref, dw_ref, dg_ref):
        m = mem_ref[...]
        r = lax.rsqrt(jnp.mean(m * m, axis=-1, keepdims=True) + EPS)
        mh = m * r
        mn = (mh * g_ref[...]).astype(BF16)
        dkv = dkv_ref[...].astype(BF16)
        dw_ref[...] = _dot_tn(mn, dkv).astype(BF16)
        dmn = _dot_nt(dkv, w_ref[...])
        dg_ref[...] = jnp.sum(dmn * mh, axis=0, keepdims=True)

    return pl.pallas_call(body, out_shape=(jax.ShapeDtypeStruct((D, D), BF16), jax.ShapeDtypeStruct((1, D), F32)),
                          name=name, compiler_params=_cp())(mem, mem_g, wkv, dkv)


def _host_call(body, *, grid, name, out_shape, in_specs, out_specs, args, scratch_shapes=(), aliases=None,
               rider=None):
    sem = ("arbitrary",) * len(grid)
    aliases = dict(aliases or {})
    if rider is None:
        res = pl.pallas_call(body, grid=grid, name=name, out_shape=tuple(out_shape), in_specs=list(in_specs),
                             out_specs=tuple(out_specs), scratch_shapes=list(scratch_shapes),
                             input_output_aliases=aliases, compiler_params=_cp(sem))(*args)
        return tuple(res), ()
    n_in, n_out, n_sc = len(in_specs), len(out_specs), len(scratch_shapes)
    r_in, r_out = len(rider.inputs), len(rider.out_shapes)

    def full_body(*refs):
        host_in = refs[:n_in]
        rid_in = refs[n_in:n_in + r_in]
        host_out = refs[n_in + r_in:n_in + r_in + n_out]
        rid_out = refs[n_in + r_in + n_out:n_in + r_in + n_out + r_out]
        host_sc = refs[n_in + r_in + n_out + r_out:n_in + r_in + n_out + r_out + n_sc]
        sems = refs[n_in + r_in + n_out + r_out + n_sc:]
        first = pl.program_id(0) == 0
        last = pl.program_id(0) == grid[0] - 1
        for ax in range(1, len(grid)):
            first = jnp.logical_and(first, pl.program_id(ax) == 0)
            last = jnp.logical_and(last, pl.program_id(ax) == grid[ax] - 1)

        @pl.when(first)
        def _():
            rider.start(rid_in, rid_out, sems)

        if rider.has_mid:
            @pl.when(last)
            def _():
                rider.mid(rid_in, rid_out, sems)

        body(*host_in, *host_out, *host_sc)

        @pl.when(last)
        def _():
            rider.end(rid_in, rid_out, sems)

    aliases.update({n_in + j: n_out + k for j, k in rider.aliases.items()})
    res = pl.pallas_call(
        full_body, grid=grid, name=name, out_shape=tuple(out_shape) + tuple(rider.out_shapes),
        in_specs=list(in_specs) + _hbm_specs(r_in), out_specs=tuple(out_specs) + tuple(_hbm_specs(r_out)),
        scratch_shapes=list(scratch_shapes) + list(rider.sems), input_output_aliases=aliases,
        compiler_params=_cp(sem),
    )(*args, *rider.inputs)
    return tuple(res[:n_out]), tuple(res[n_out:])


def _in_fwd(x, pre_g, w_t, name, rider=None):
    s, n = x.shape[0], w_t.shape[0]
    tm = min(512, s)
    nc = 256

    def body(x_ref, g_ref, w_ref, p_ref, h_ref):
        xv = x_ref[...]
        r = lax.rsqrt(jnp.mean(xv * xv, axis=-1, keepdims=True) + EPS)
        h = (xv * r * g_ref[...]).astype(BF16)
        h_ref[...] = h
        for j in range(n // nc):
            p_ref[:, j * nc:(j + 1) * nc] = _dot_nt(h, w_ref[j * nc:(j + 1) * nc, :]).astype(BF16)

    return _host_call(
        body, grid=(s // tm,), name=name, rider=rider,
        out_shape=(jax.ShapeDtypeStruct((s, n), BF16), jax.ShapeDtypeStruct((s, D), BF16)),
        in_specs=[pl.BlockSpec((tm, D), lambda i: (i, 0)), _const((1, D)), _resident((n, D))],
        out_specs=(pl.BlockSpec((tm, n), lambda i: (i, 0)), pl.BlockSpec((tm, D), lambda i: (i, 0))),
        args=(x, pre_g, w_t))


NC = 256


def _stream_tables(core, chip, n):
    nchunk = n // NC
    idx = jnp.arange(nchunk, dtype=jnp.int32)
    src = jnp.array([0, 2, 1, 3], jnp.int32)
    r = n // N_CHIPS

    def group_of(row):
        j = src[(row // r) ^ chip]
        through_sibling = ((row % r) // (r // 2) != core).astype(jnp.int32)
        return jnp.where(j == 0, 0, 2 * j - 1 + through_sibling)

    grp = jnp.maximum(group_of(idx * NC), group_of(idx * NC + NC - 1))
    order = jnp.argsort(grp * 64 + idx).astype(jnp.int32)
    return order, grp[order]


def _in_fwd_streamed(x, pre_g, first, later, order, group, name):
    s, n = x.shape[0], first[0].shape[0]
    nchunk = n // NC
    rider = _GatherRider(first)
    rider2 = _GatherRider(later) if later else None
    a, m = len(first), len(later)
    tr = min(256, s)

    def body(*refs):
        order_ref, group_ref, x_ref, g_ref = refs[0:4]
        p_ref, h_ref = refs[4 + a + m:6 + a + m]
        outs = refs[6 + a + m:6 + 2 * a + m]
        outs2 = refs[6 + 2 * a + m:6 + 2 * a + 2 * m]
        wbuf, wsem, send_sems, recv_sems = refs[6 + 2 * a + 2 * m:10 + 2 * a + 2 * m]
        sems2 = refs[10 + 2 * a + 2 * m:]
        w_hbm = outs[0]
        j = pl.program_id(0)
        sems = (send_sems, recv_sems)
        grp = group_ref[j]
        new_group = jnp.logical_or(j == 0, group_ref[jnp.maximum(j - 1, 0)] != grp)
        slot = j % 2

        def fetch(step, sl):
            rows = pl.ds(pl.multiple_of(order_ref[step] * NC, NC), NC)
            return pltpu.make_async_copy(w_hbm.at[rows], wbuf.at[sl], wsem.at[sl])

        @pl.when(j == 0)
        def _():
            rider.start(None, outs, sems, peers=(0, 1))

            @pl.loop(0, s // tr)
            def _(t):
                rows = pl.ds(pl.multiple_of(t * tr, tr), tr)
                xv = x_ref[rows, :]
                r = lax.rsqrt(jnp.mean(xv * xv, axis=-1, keepdims=True) + EPS)
                h_ref[rows, :] = (xv * r * g_ref[...]).astype(BF16)

        before = jnp.where(j == 0, 0, group_ref[jnp.maximum(j - 1, 0)])

        def entering(b):
            return jnp.logical_and(before < b, b <= grp)

        for src in range(3):
            @pl.when(entering(2 * src + 1))
            def _(src=src):
                if src == 0:
                    rider.start(None, outs, sems, peers=(2,))
                rider.mid(None, outs, sems, peers=(src,))
                if src == 1 and rider2 is not None:
                    rider2.start(None, outs2, sems2)

            @pl.when(entering(2 * src + 2))
            def _(src=src):
                rider.wait_forwarded(outs, sems, peers=(src,))

        @pl.when(new_group)
        def _():
            fetch(j, slot).start()

        fetch(j, slot).wait()
        nxt = jnp.minimum(j + 1, nchunk - 1)

        @pl.when(jnp.logical_and(j + 1 < nchunk, group_ref[nxt] == grp))
        def _():
            fetch(nxt, 1 - slot).start()

        p_ref[...] = _dot_nt(h_ref[...], wbuf[slot]).astype(BF16)

        @pl.when(j == nchunk - 1)
        def _():
            rider.wait_sends(outs, sems)
            if rider2 is not None:
                rider2.mid(None, outs2, sems2)
                rider2.end(None, outs2, sems2)

    hbm = pl.BlockSpec(memory_space=pltpu.HBM)
    arrs = list(first) + list(later)
    whole = pl.BlockSpec((s, D), lambda j, o, g: (0, 0), pipeline_mode=pl.Buffered(1))
    res = pl.pallas_call(
        body, name=name,
        out_shape=(jax.ShapeDtypeStruct((s, n), BF16), jax.ShapeDtypeStruct((s, D), BF16))
        + tuple(jax.ShapeDtypeStruct(v.shape, v.dtype) for v in arrs),
        grid_spec=pltpu.PrefetchScalarGridSpec(
            num_scalar_prefetch=2, grid=(nchunk,),
            in_specs=[whole, pl.BlockSpec((1, D), lambda j, o, g: (0, 0))] + [hbm] * (a + m),
            out_specs=(pl.BlockSpec((s, NC), lambda j, o, g: (0, o[j])),
                       pl.BlockSpec((s, D), lambda j, o, g: (0, 0))) + (hbm,) * (a + m),
            scratch_shapes=[pltpu.VMEM((2, NC, D), BF16), pltpu.SemaphoreType.DMA((2,))] + list(rider.sems)
            + (list(rider2.sems) if rider2 is not None else [])),
        input_output_aliases={4 + v: 2 + v for v in range(a + m)},
        compiler_params=_cp(("arbitrary",)),
    )(order, group, x, pre_g, *arrs)
    return res[0], res[1], tuple(res[2:2 + a]), tuple(res[2 + a:])


def _xattn_fwd(q, kv_ref):
    outs, probs = [], []
    for h in range(XA // HD):
        qh = q[:, h * HD:(h + 1) * HD]
        kh = kv_ref[:, h * HD:(h + 1) * HD]
        vh = kv_ref[:, XA + h * HD:XA + (h + 1) * HD]
        sc = _dot_nt(qh, kh) * SCALE
        e = jnp.exp(sc - jnp.max(sc, axis=-1, keepdims=True))
        pr = e / jnp.sum(e, axis=-1, keepdims=True)
        outs.append(_dot(pr.astype(BF16), vh))
        probs.append(pr)
    return jnp.concatenate(outs, axis=-1), probs


def _xattn_bwd(dyx, q, probs, kv_ref, dkv_ref):
    dqs = []
    for h in range(XA // HD):
        qh = q[:, h * HD:(h + 1) * HD]
        kh = kv_ref[:, h * HD:(h + 1) * HD]
        vh = kv_ref[:, XA + h * HD:XA + (h + 1) * HD]
        dy = dyx[:, h * HD:(h + 1) * HD].astype(BF16)
        pr = probs[h]
        dp = _dot_nt(dy, vh)
        ds = (pr * (dp - jnp.sum(dp * pr, axis=-1, keepdims=True))).astype(BF16)
        dqs.append(_dot(ds, kh) * SCALE)
        dkv_ref[:, h * HD:(h + 1) * HD] += _dot_tn(ds, qh) * SCALE
        dkv_ref[:, XA + h * HD:XA + (h + 1) * HD] += _dot_tn(pr.astype(BF16), dy)
    return jnp.concatenate(dqs, axis=-1)


def _layer_norm_fwd(v, g, b):
    mu = jnp.mean(v, axis=-1, keepdims=True)
    vc = v - mu
    rstd = lax.rsqrt(jnp.mean(vc * vc, axis=-1, keepdims=True) + EPS)
    vhat = vc * rstd
    return vhat * g + b, vhat, rstd


def _layer_norm_bwd(dy, vhat, rstd, g):
    dvh = dy * g
    return rstd * (dvh - jnp.mean(dvh, axis=-1, keepdims=True) - vhat * jnp.mean(dvh * vhat, axis=-1, keepdims=True))


def _head_masks():
    col = lax.broadcasted_iota(jnp.int32, (1, BW), 1)
    return [(col >= h * (BW // 4)) & (col < (h + 1) * (BW // 4)) for h in range(4)]


def _halo_prev(nblk_per_tile):
    return lambda i: (jnp.maximum(i * nblk_per_tile - 1, 0), 0)


def _row_ids(i, t):
    return i * t + lax.broadcasted_iota(jnp.int32, (t, 1), 0)


def _even_mix(i, p_ref, ph_ref, ln_g, ln_b, wcat_ref, bsg_ref, bconv_ref, wbuf):
    t = p_ref.shape[0]
    u = p_ref[:, 0:BW].astype(F32)
    v = p_ref[:, BW:2 * BW].astype(F32)
    bg = p_ref[:, 2 * BW:3 * BW].astype(F32)
    cg = p_ref[:, 3 * BW:4 * BW].astype(F32)
    xin = p_ref[:, 4 * BW:5 * BW].astype(F32)
    vn, vhat, rstd = _layer_norm_fwd(v, ln_g, ln_b)
    masks = _head_masks()
    sgs, vsts = [], []
    for n in range(t // CH):
        vn_c = vn[n * CH:(n + 1) * CH]
        vst = jnp.concatenate([jnp.where(m, vn_c, 0.0) for m in masks], axis=0).astype(BF16)
        sgs.append(_dot(wcat_ref[...], vst) + bsg_ref[...])
        vsts.append(vst)
    sg = jnp.concatenate(sgs, axis=0)
    ya = u * sg
    w_halo = ph_ref[:, 3 * BW:4 * BW].astype(F32) * ph_ref[:, 4 * BW:5 * BW].astype(F32)
    wbuf[0:HALO, :] = jnp.where(i > 0, w_halo, 0.0)
    wbuf[HALO:HALO + t, :] = cg * xin
    conv = (bconv_ref[0:1, :] * wbuf[pl.ds(HALO - 2, t), :] + bconv_ref[1:2, :] * wbuf[pl.ds(HALO - 1, t), :]
            + bconv_ref[2:3, :] * wbuf[pl.ds(HALO, t), :])
    yb = bg * conv
    return dict(u=u, bg=bg, cg=cg, xin=xin, vhat=vhat, rstd=rstd, sg=sg, vsts=vsts, conv=conv, ya=ya, yb=yb,
                masks=masks)


def _pool_select(vals):
    col = lax.broadcasted_iota(jnp.int32, (1, BW), 1)
    g = BW // 4
    return jnp.where(col < g, vals[0], jnp.where(col < 2 * g, vals[1], jnp.where(col < 3 * g, vals[2], vals[3])))


def _inv_counts(i, t):
    rows = _row_ids(i, t) + 1
    return [1.0 / jnp.minimum(rows, w).astype(F32) for w in POOL_WINDOWS]


def _band_matrices(t, forward):
    j = jnp.arange(t)[:, None]
    r = jnp.arange(HALO + t)[None, :]
    if forward:
        return jnp.stack([(r >= j) & (r < j + w) for w in POOL_WINDOWS]).astype(BF16)
    return jnp.stack([(r <= HALO + j) & (r > HALO + j - w) for w in POOL_WINDOWS]).astype(BF16)


SHIFT_ROWS = HALO + TS - 8


def _shifted_copies(buf, sh):
    for b in range(1, 8):
        sh[b - 1] = buf[pl.ds(b, SHIFT_ROWS), :]


def _rows_at(buf, sh, off, t):
    a, b = divmod(off, 8)
    return buf[pl.ds(8 * a, t), :] if b == 0 else sh[b - 1, pl.ds(8 * a, t), :]


def _tap_sums(d_ref, buf, sh, base, out_ref):
    t = d_ref.shape[0]
    group = 4
    for k0 in range(0, CONF, group):
        taps = list(range(k0, min(k0 + group, CONF)))

        def step(r, accs, taps=taps):
            row = pl.multiple_of(r * 8, 8)
            d = d_ref[pl.ds(row, 8), :]
            new = []
            for acc, k in zip(accs, taps):
                a, b = divmod(base + k, 8)
                src = buf[pl.ds(row + 8 * a, 8), :] if b == 0 else sh[b - 1, pl.ds(row + 8 * a, 8), :]
                new.append(acc + d * src)
            return tuple(new)

        accs = lax.fori_loop(0, t // 8, step, tuple(jnp.zeros((8, BW), F32) for _ in taps), unroll=2)
        for acc, k in zip(accs, taps):
            out_ref[8 * k:8 * k + 8, :] += acc


def _odd_mix(i, p_ref, ph_ref, bands_ref, wbd_ref, cscale, dww_ref, dwb, ln_g, ln_b, pww_ref, pwb, gbuf, gsh,
             cv=None):
    t = p_ref.shape[0]
    zc_bf = p_ref[:, 0:BW]
    zc = zc_bf.astype(F32)
    ga = p_ref[:, BW:2 * BW].astype(F32)
    gb = p_ref[:, 2 * BW:3 * BW].astype(F32)
    zh = ph_ref[:, 0:BW]
    zcat = jnp.concatenate([jnp.where(i > 0, zh, jnp.zeros_like(zh)), zc_bf], axis=0)
    inv = _inv_counts(i, t)
    pooled = _pool_select([_dot(bands_ref[w], zcat) * inv[w] for w in range(len(POOL_WINDOWS))]) - zc
    pooled_bf = pooled.astype(BF16)
    pre = _dot(pooled_bf, wbd_ref[...])
    yc = pre * cscale
    sgb = _sigmoid(gb)
    z = ga * sgb
    gh_a = ph_ref[:, BW:2 * BW].astype(F32)
    gh_b = ph_ref[:, 2 * BW:3 * BW].astype(F32)
    gbuf[0:HALO, :] = jnp.where(i > 0, gh_a * _sigmoid(gh_b), 0.0)
    gbuf[HALO:HALO + t, :] = z
    _shifted_copies(gbuf, gsh)
    if cv is None:
        cv = dwb + dww_ref[CONF - 1:CONF, :] * z
        for k in range(CONF - 1):
            cv = cv + dww_ref[k:k + 1, :] * _rows_at(gbuf, gsh, HALO - (CONF - 1) + k, t)
    zl, zhat, rstd = _layer_norm_fwd(cv, ln_g, ln_b)
    szl = _sigmoid(zl)
    zs = (zl * szl).astype(BF16)
    yd = _dot(zs, pww_ref[...]) + pwb
    return dict(ga=ga, sgb=sgb, pooled_bf=pooled_bf, pre=pre, yc=yc, zhat=zhat, rstd=rstd, zl=zl, szl=szl,
                zs=zs, yd=yd, inv=inv, cv=cv)


def _post_norm(o, post_g):
    r = lax.rsqrt(jnp.mean(o * o, axis=-1, keepdims=True) + EPS)
    return o * r, r


def _gate_out(y_a, y_b, y_x, gate, wout_ref):
    sgt = _sigmoid(gate)
    sgate = gate * sgt
    ys = [(y_a * sgate[:, 0:BW]).astype(BF16), (y_b * sgate[:, BW:2 * BW]).astype(BF16),
          (y_x * sgate[:, 2 * BW:MIX]).astype(BF16)]
    o = (_dot(ys[0], wout_ref[0:BW, :]) + _dot(ys[1], wout_ref[BW:2 * BW, :]) + _dot(ys[2], wout_ref[2 * BW:MIX, :]))
    return o, ys, sgt, sgate


def _tile_specs(s, n):
    nh = TS // HALO
    return pl.BlockSpec((TS, n), lambda i: (i, 0)), pl.BlockSpec((HALO, n), _halo_prev(nh))


def _even_fwd(x, p, kv, ln_g, ln_b, wcat, bsg, bconv, wout, post_g, rider=None):
    s = x.shape[0]

    def body(x_ref, p_ref, ph_ref, kv_ref, lng, lnb, wcat_ref, bsg_ref, bconv_ref, wout_ref, pg, x1_ref, o_ref,
             y_ref, wbuf):
        i = pl.program_id(0)
        mx = _even_mix(i, p_ref, ph_ref, lng[...], lnb[...], wcat_ref, bsg_ref, bconv_ref, wbuf)
        yx, _ = _xattn_fwd(p_ref[:, 5 * BW:5 * BW + XA], kv_ref)
        gate = p_ref[:, 5 * BW + XA:EVEN_IN].astype(F32)
        o, ys, _, _ = _gate_out(mx["ya"], mx["yb"], yx, gate, wout_ref)
        y_ref[:, 0:BW] = ys[0]
        y_ref[:, BW:2 * BW] = ys[1]
        y_ref[:, 2 * BW:MIX] = ys[2]
        n, _ = _post_norm(o, pg[...])
        o_ref[...] = o
        x1_ref[...] = x_ref[...] + n * pg[...]

    tile, halo = _tile_specs(s, EVEN_IN)
    row = pl.BlockSpec((TS, D), lambda i: (i, 0))
    return _host_call(
        body, grid=(s // TS,), name="even_fwd", rider=rider,
        out_shape=(jax.ShapeDtypeStruct((s, D), F32), jax.ShapeDtypeStruct((s, D), F32),
                   jax.ShapeDtypeStruct((s, MIX), BF16)),
        in_specs=[row, tile, halo, _const((N_MEM, D)), _const((1, BW)), _const((1, BW)), _const((CH, 4 * CH)),
                  _const((CH, BW)), _const((3, BW)), _resident((MIX, D)), _const((1, D))],
        out_specs=(row, row, pl.BlockSpec((TS, MIX), lambda i: (i, 0))),
        scratch_shapes=[pltpu.VMEM((HALO + TS, BW), F32)],
        args=(x, p, p, kv, ln_g, ln_b, wcat, bsg, bconv, wout, post_g))


def _odd_fwd(x1, p, kv, wbd, cscale, dww, dwb, ln_g, ln_b, pww, pwb, wout, post_g, target):
    s = x1.shape[0]

    def body(x_ref, p_ref, ph_ref, kv_ref, bands_ref, wbd_ref, cs, dww_ref, dwb_ref, lng, lnb, pww_ref, pwb_ref,
             wout_ref, pg, tgt_ref, dx_ref, o_ref, cv_ref, loss_ref, gbuf, gsh):
        i = pl.program_id(0)
        mx = _odd_mix(i, p_ref, ph_ref, bands_ref, wbd_ref, cs[...], dww_ref, dwb_ref[...], lng[...], lnb[...],
                      pww_ref, pwb_ref[...], gbuf, gsh)
        cv_ref[...] = mx["cv"]
        yx, _ = _xattn_fwd(p_ref[:, 3 * BW:3 * BW + XA], kv_ref)
        gate = p_ref[:, 3 * BW + XA:ODD_IN].astype(F32)
        o, _, _, _ = _gate_out(mx["yc"], mx["yd"], yx, gate, wout_ref)
        n, _ = _post_norm(o, pg[...])
        o_ref[...] = o
        err = x_ref[...] + n * pg[...] - tgt_ref[...]
        dx_ref[...] = err * (1.0 / D)

        @pl.when(i == 0)
        def _():
            loss_ref[...] = jnp.zeros_like(loss_ref)

        loss_ref[...] += 0.5 * jnp.sum(jnp.sum(err * err, axis=-1, keepdims=True) * (1.0 / D), axis=0, keepdims=True)

    tile, halo = _tile_specs(s, ODD_IN)
    row = pl.BlockSpec((TS, D), lambda i: (i, 0))
    vec = _const((1, BW))
    return pl.pallas_call(
        body, grid=(s // TS,), name="odd_fwd",
        out_shape=(jax.ShapeDtypeStruct((s, D), F32), jax.ShapeDtypeStruct((s, D), F32),
                   jax.ShapeDtypeStruct((s, BW), F32), jax.ShapeDtypeStruct((8, 128), F32)),
        in_specs=[row, tile, halo, _const((N_MEM, D)), _const((4, TS, HALO + TS)), _const((BW, BW)), vec,
                  _const((CONF, BW)), vec, vec, vec, _const((BW, BW)), vec, _resident((MIX, D)), _const((1, D)), row],
        out_specs=(row, row, pl.BlockSpec((TS, BW), lambda i: (i, 0)), _const((8, 128))),
        scratch_shapes=[pltpu.VMEM((HALO + TS, BW), F32), pltpu.VMEM((7, SHIFT_ROWS, BW), F32)],
        compiler_params=_cp(("arbitrary",)),
    )(x1, p, p, kv, _band_matrices(TS, False), wbd, cscale, dww, dwb, ln_g, ln_b, pww, pwb, wout, post_g, target)


def _acc_init(i, refs):
    @pl.when(i == 0)
    def _():
        for r in refs:
            r[...] = jnp.zeros_like(r)


def _post_norm_bwd(dx, o, pg, dpg_ref):
    n, r = _post_norm(o, pg)
    dpg_ref[...] += jnp.sum(dx * n, axis=0, keepdims=True)
    dn = dx * pg
    return (r * (dn - n * jnp.mean(dn * n, axis=-1, keepdims=True))).astype(BF16)


def _gate_bwd(do, wout_ref, ys_f32, gate, y_ref):
    dy = _dot_nt(do, wout_ref[...])
    sgt = _sigmoid(gate)
    sgate = gate * sgt
    dsilu = sgt * (1.0 + gate * (1.0 - sgt))
    offs = (0, BW, 2 * BW, MIX)
    dys, dgs = [], []
    for j, yv in enumerate(ys_f32):
        a, b = offs[j], offs[j + 1]
        if y_ref is not None:
            y_ref[:, a:b] = (yv * sgate[:, a:b]).astype(BF16)
        dys.append(dy[:, a:b] * sgate[:, a:b])
        dgs.append(dy[:, a:b] * yv * dsilu[:, a:b])
    return dys, jnp.concatenate(dgs, axis=-1)


NEXT = 16


def _even_bwd1(dx, o, p, kv, ln_g, ln_b, wcat, bsg, hsel, bconv, wout, post_g, rider=None):
    s = dx.shape[0]
    nt = s // TS

    def body(dx_ref, o_ref, p_ref, ph_ref, dxn_ref, on_ref, pn_ref, kv_ref, lng, lnb, wcat_ref, bsg_ref, hsel_ref,
             bconv_ref, wout_ref, pg,
             dp_ref, do_ref, dpg_ref, dlng_ref, dlnb_ref, dwcat_ref, dbs_ref, dbconv_ref, dkv_ref, wbuf, dbuf):
        i = pl.program_id(0)
        _acc_init(i, (dpg_ref, dlng_ref, dlnb_ref, dwcat_ref, dbs_ref, dbconv_ref, dkv_ref))
        mx = _even_mix(i, p_ref, ph_ref, lng[...], lnb[...], wcat_ref, bsg_ref, bconv_ref, wbuf)
        q = p_ref[:, 5 * BW:5 * BW + XA]
        yx, probs = _xattn_fwd(q, kv_ref)
        gate = p_ref[:, 5 * BW + XA:EVEN_IN].astype(F32)
        do = _post_norm_bwd(dx_ref[...], o_ref[...], pg[...], dpg_ref)
        do_ref[...] = do
        (dya, dyb, dyx), dgate = _gate_bwd(do, wout_ref, (mx["ya"], mx["yb"], yx), gate, None)
        dp_ref[:, 0:BW] = (dya * mx["sg"]).astype(BF16)
        dsg = (dya * mx["u"]).astype(BF16)
        dvns = []
        for n in range(TS // CH):
            dsg_c = dsg[n * CH:(n + 1) * CH]
            dvst = _dot_tn(wcat_ref[...], dsg_c)
            dvn_c = jnp.where(mx["masks"][0], dvst[0:CH], 0.0)
            for h in range(1, 4):
                dvn_c = dvn_c + jnp.where(mx["masks"][h], dvst[h * CH:(h + 1) * CH], 0.0)
            dvns.append(dvn_c)
            dwcat_ref[...] += _dot_nt(dsg_c, mx["vsts"][n])
            dbs_ref[...] += _dot(dsg_c, hsel_ref[...])
        dvn = jnp.concatenate(dvns, axis=0)
        dlng_ref[...] += jnp.sum(dvn * mx["vhat"], axis=0, keepdims=True)
        dlnb_ref[...] += jnp.sum(dvn, axis=0, keepdims=True)
        dp_ref[:, BW:2 * BW] = _layer_norm_bwd(dvn, mx["vhat"], mx["rstd"], lng[...]).astype(BF16)
        dp_ref[:, 2 * BW:3 * BW] = (dyb * mx["conv"]).astype(BF16)
        dconv = dyb * mx["bg"]
        for k in range(3):
            dbconv_ref[k:k + 1, :] += jnp.sum(dconv * wbuf[pl.ds(HALO - 2 + k, TS), :], axis=0, keepdims=True)
        n_n, r_n = _post_norm(on_ref[...], pg[...])
        dn_n = dxn_ref[...] * pg[...]
        do_n = (r_n * (dn_n - n_n * jnp.mean(dn_n * n_n, axis=-1, keepdims=True))).astype(BF16)
        dy_n = _dot_nt(do_n, wout_ref[BW:2 * BW, :])
        g_n = pn_ref[:, 5 * BW + XA + BW:5 * BW + XA + 2 * BW].astype(F32)
        dconv_n = dy_n * (g_n * _sigmoid(g_n)) * pn_ref[:, 2 * BW:3 * BW].astype(F32)
        dbuf[0:TS, :] = dconv
        dbuf[TS:TS + NEXT, :] = jnp.where(i < nt - 1, dconv_n, 0.0)
        dw = (bconv_ref[2:3, :] * dconv + bconv_ref[1:2, :] * dbuf[pl.ds(1, TS), :]
              + bconv_ref[0:1, :] * dbuf[pl.ds(2, TS), :])
        dp_ref[:, 3 * BW:4 * BW] = (dw * mx["xin"]).astype(BF16)
        dp_ref[:, 4 * BW:5 * BW] = (dw * mx["cg"]).astype(BF16)
        dp_ref[:, 5 * BW:5 * BW + XA] = _xattn_bwd(dyx, q, probs, kv_ref, dkv_ref).astype(BF16)
        dp_ref[:, 5 * BW + XA:EVEN_IN] = dgate.astype(BF16)

    tile, halo = _tile_specs(s, EVEN_IN)
    row = pl.BlockSpec((TS, D), lambda i: (i, 0))
    vec = _const((1, BW))
    nxt = _halo_next(TS // NEXT, s // NEXT)

    def out(n):
        return pl.BlockSpec((TS, n), lambda i: (i, 0))

    return _host_call(
        body, grid=(nt,), name="even_bwd1", rider=rider,
        out_shape=(jax.ShapeDtypeStruct((s, EVEN_IN), BF16), jax.ShapeDtypeStruct((s, D), BF16),
                   jax.ShapeDtypeStruct((1, D), F32), jax.ShapeDtypeStruct((1, BW), F32),
                   jax.ShapeDtypeStruct((1, BW), F32), jax.ShapeDtypeStruct((CH, 4 * CH), F32),
                   jax.ShapeDtypeStruct((CH, 128), F32), jax.ShapeDtypeStruct((8, BW), F32),
                   jax.ShapeDtypeStruct((N_MEM, D), F32)),
        in_specs=[row, row, tile, halo, pl.BlockSpec((NEXT, D), nxt), pl.BlockSpec((NEXT, D), nxt),
                  pl.BlockSpec((NEXT, EVEN_IN), nxt), _const((N_MEM, D)), vec, vec, _const((CH, 4 * CH)),
                  _const((CH, BW)), _const((BW, 128)), _const((3, BW)), _resident((MIX, D)), _const((1, D))],
        out_specs=(out(EVEN_IN), out(D),
                   _const((1, D)), vec, vec, _const((CH, 4 * CH)), _const((CH, 128)), _const((8, BW)),
                   _const((N_MEM, D))),
        scratch_shapes=[pltpu.VMEM((HALO + TS, BW), F32), pltpu.VMEM((TS + NEXT, BW), F32)],
        args=(dx, o, p, p, dx, o, p, kv, ln_g, ln_b, wcat, bsg, hsel, bconv, wout, post_g))


def _odd_bwd1(dx, o, cv, p, kv, wbd, cscale, dww, dwb, ln_g, ln_b, pww, pwb, wout, post_g):
    s = dx.shape[0]

    def body(dx_ref, o_ref, cv_ref, p_ref, ph_ref, kv_ref, bands_ref, wbd_ref, cs, dww_ref, dwb_ref, lng, lnb,
             pww_ref, pwb_ref, wout_ref, pg,
             dpc_ref, tmpc_ref, tmpd_ref, do_ref, y_ref, dpg_ref, dcs_ref, dwbd_ref, ddww_ref, ddwb_ref, dlng_ref,
             dlnb_ref, dpww_ref, dpwb_ref, dkv_ref, gbuf, gsh, dcv_buf):
        i = pl.program_id(0)
        _acc_init(i, (dpg_ref, dcs_ref, dwbd_ref, ddww_ref, ddwb_ref, dlng_ref, dlnb_ref, dpww_ref, dpwb_ref,
                      dkv_ref))
        mx = _odd_mix(i, p_ref, ph_ref, bands_ref, wbd_ref, cs[...], dww_ref, dwb_ref[...], lng[...], lnb[...],
                      pww_ref, pwb_ref[...], gbuf, gsh, cv=cv_ref[...])
        q = p_ref[:, 3 * BW:3 * BW + XA]
        yx, probs = _xattn_fwd(q, kv_ref)
        gate = p_ref[:, 3 * BW + XA:ODD_IN].astype(F32)
        do = _post_norm_bwd(dx_ref[...], o_ref[...], pg[...], dpg_ref)
        do_ref[...] = do
        (dyc, dyd, dyx), dgate = _gate_bwd(do, wout_ref, (mx["yc"], mx["yd"], yx), gate, y_ref)
        dcs_ref[...] += jnp.sum(dyc * mx["pre"], axis=0, keepdims=True)
        dpre = (dyc * cs[...]).astype(BF16)
        dwbd_ref[...] += _dot_tn(mx["pooled_bf"], dpre)
        dpooled = _dot_nt(dpre, wbd_ref[...])
        tmpc_ref[...] = _pool_select([dpooled * c_ for c_ in mx["inv"]]).astype(BF16)
        dyd_bf = dyd.astype(BF16)
        dpwb_ref[...] += jnp.sum(dyd, axis=0, keepdims=True)
        dpww_ref[...] += _dot_tn(mx["zs"], dyd_bf)
        dzs = _dot_nt(dyd_bf, pww_ref[...])
        zl, szl = mx["zl"], mx["szl"]
        dzl = dzs * (szl * (1.0 + zl * (1.0 - szl)))
        dlng_ref[...] += jnp.sum(dzl * mx["zhat"], axis=0, keepdims=True)
        dlnb_ref[...] += jnp.sum(dzl, axis=0, keepdims=True)
        dcv = _layer_norm_bwd(dzl, mx["zhat"], mx["rstd"], lng[...])
        tmpd_ref[...] = dcv.astype(BF16)
        ddwb_ref[...] += jnp.sum(dcv, axis=0, keepdims=True)
        dcv_buf[...] = dcv
        _tap_sums(dcv_buf, gbuf, gsh, HALO - (CONF - 1), ddww_ref)
        dpc_ref[:, 0:XA] = _xattn_bwd(dyx, q, probs, kv_ref, dkv_ref).astype(BF16)
        dpc_ref[:, XA:XA + MIX] = dgate.astype(BF16)

    tile, halo = _tile_specs(s, ODD_IN)
    row = pl.BlockSpec((TS, D), lambda i: (i, 0))
    vec = _const((1, BW))

    def out(n):
        return pl.BlockSpec((TS, n), lambda i: (i, 0))

    return pl.pallas_call(
        body, grid=(s // TS,), name="odd_bwd1",
        out_shape=(jax.ShapeDtypeStruct((s, XA + MIX), BF16), jax.ShapeDtypeStruct((s, BW), BF16),
                   jax.ShapeDtypeStruct((s, BW), BF16), jax.ShapeDtypeStruct((s, D), BF16),
                   jax.ShapeDtypeStruct((s, MIX), BF16),
                   jax.ShapeDtypeStruct((1, D), F32), jax.ShapeDtypeStruct((1, BW), F32),
                   jax.ShapeDtypeStruct((BW, BW), F32), jax.ShapeDtypeStruct((8 * CONF, BW), F32),
                   jax.ShapeDtypeStruct((1, BW), F32), jax.ShapeDtypeStruct((1, BW), F32),
                   jax.ShapeDtypeStruct((1, BW), F32), jax.ShapeDtypeStruct((BW, BW), F32),
                   jax.ShapeDtypeStruct((1, BW), F32), jax.ShapeDtypeStruct((N_MEM, D), F32)),
        in_specs=[row, row, out(BW), tile, halo, _const((N_MEM, D)), _const((4, TS, HALO + TS)), _const((BW, BW)), vec,
                  _const((CONF, BW)), vec, vec, vec, _const((BW, BW)), vec, _resident((MIX, D)), _const((1, D))],
        out_specs=(out(XA + MIX), out(BW), out(BW), out(D), out(MIX),
                   _const((1, D)), vec, _const((BW, BW)), _const((8 * CONF, BW)), vec, vec, vec, _const((BW, BW)), vec,
                   _const((N_MEM, D))),
        scratch_shapes=[pltpu.VMEM((HALO + TS, BW), F32), pltpu.VMEM((7, SHIFT_ROWS, BW), F32),
                        pltpu.VMEM((TS, BW), F32)],
        compiler_params=_cp(("arbitrary",)),
    )(dx, o, cv, p, p, kv, _band_matrices(TS, False), wbd, cscale, dww, dwb, ln_g, ln_b, pww, pwb, wout, post_g)


def _halo_next(nblk_per_tile, nblk):
    return lambda i: (jnp.minimum((i + 1) * nblk_per_tile, nblk - 1), 0)


def _pre_norm_bwd(dh, x, pre_g, dres, dpre_ref):
    r = lax.rsqrt(jnp.mean(x * x, axis=-1, keepdims=True) + EPS)
    xh = x * r
    dpre_ref[...] += jnp.sum(dh * xh, axis=0, keepdims=True)
    dxh = dh * pre_g
    return dres + r * (dxh - xh * jnp.mean(dxh * xh, axis=-1, keepdims=True))


def _even_bwd2(dp, w_t, x, pre_g, dres, rider=None):
    s = x.shape[0]
    tm = min(512, s)

    def body(dp_ref, w_ref, x_ref, pg, dres_ref, dx_ref, dpre_ref):
        _acc_init(pl.program_id(0), (dpre_ref,))
        dh = _dot(dp_ref[...], w_ref[...])
        dx_ref[...] = _pre_norm_bwd(dh, x_ref[...], pg[...], dres_ref[...], dpre_ref)

    row = pl.BlockSpec((tm, D), lambda i: (i, 0))
    return _host_call(
        body, grid=(s // tm,), name="even_bwd2", rider=rider,
        out_shape=(jax.ShapeDtypeStruct((s, D), F32), jax.ShapeDtypeStruct((1, D), F32)),
        in_specs=[pl.BlockSpec((tm, EVEN_IN), lambda i: (i, 0)), _resident((EVEN_IN, D)), row, _const((1, D)), row],
        out_specs=(row, _const((1, D))),
        args=(dp, w_t, x, pre_g, dres))


def _odd_bwd2(dpc, tmpc, tmpd, p, dww, w_t, x, pre_g, dres):
    s = x.shape[0]
    nt = s // TS

    def body(dpc_ref, tc_ref, tch_ref, td_ref, tdh_ref, ga_ref, gb_ref, bands_ref, dww_ref, w_ref, x_ref, pg,
             dres_ref, dpb_ref, dx_ref, dpre_ref, dbuf, dsh):
        i = pl.program_id(0)
        _acc_init(i, (dpre_ref,))
        more = i < nt - 1
        e_bf = tc_ref[...]
        eh = tch_ref[...]
        ecat = jnp.concatenate([e_bf, jnp.where(more, eh, jnp.zeros_like(eh))], axis=0)
        dbuf[0:TS, :] = td_ref[...].astype(F32)
        dbuf[TS:TS + HALO, :] = jnp.where(more, tdh_ref[...].astype(F32), 0.0)
        sums = [_dot(bands_ref[w], ecat) for w in range(len(POOL_WINDOWS))]
        rows = _row_ids(i, TS) + 1
        cnt = _pool_select([jnp.minimum(rows, w).astype(F32) for w in POOL_WINDOWS])
        dzc = (_pool_select(sums) - e_bf.astype(F32) * cnt).astype(BF16)
        _shifted_copies(dbuf, dsh)
        dz = dww_ref[CONF - 1:CONF, :] * dbuf[pl.ds(0, TS), :]
        for sft in range(1, CONF):
            dz = dz + dww_ref[CONF - 1 - sft:CONF - sft, :] * _rows_at(dbuf, dsh, sft, TS)
        ga = ga_ref[...].astype(F32)
        sgb = _sigmoid(gb_ref[...].astype(F32))
        dga = (dz * sgb).astype(BF16)
        dgb = (dz * ga * sgb * (1.0 - sgb)).astype(BF16)
        dpb_ref[:, 0:BW] = dzc
        dpb_ref[:, BW:2 * BW] = dga
        dpb_ref[:, 2 * BW:3 * BW] = dgb
        dh = (_dot(dzc, w_ref[0:BW, :]) + _dot(dga, w_ref[BW:2 * BW, :]) + _dot(dgb, w_ref[2 * BW:3 * BW, :])
              + _dot(dpc_ref[...], w_ref[3 * BW:ODD_IN, :]))
        dx_ref[...] = _pre_norm_bwd(dh, x_ref[...], pg[...], dres_ref[...], dpre_ref)

    row = pl.BlockSpec((TS, D), lambda i: (i, 0))

    def tile(n, j=0):
        return pl.BlockSpec((TS, n), lambda i: (i, j))

    nxt = pl.BlockSpec((HALO, BW), _halo_next(TS // HALO, s // HALO))
    return pl.pallas_call(
        body, grid=(nt,), name="odd_bwd2",
        out_shape=(jax.ShapeDtypeStruct((s, 3 * BW), BF16), jax.ShapeDtypeStruct((s, D), F32),
                   jax.ShapeDtypeStruct((1, D), F32)),
        in_specs=[tile(XA + MIX), tile(BW), nxt, tile(BW), nxt, tile(BW, 1), tile(BW, 2), _const((4, TS, HALO + TS)),
                  _const((CONF, BW)), _resident((ODD_IN, D)), row, _const((1, D)), row],
        out_specs=(tile(3 * BW), row, _const((1, D))),
        scratch_shapes=[pltpu.VMEM((TS + HALO, BW), F32), pltpu.VMEM((7, SHIFT_ROWS, BW), F32)],
        compiler_params=_cp(("arbitrary",)),
    )(dpc, tmpc, tmpc, tmpd, tmpd, p, p, _band_matrices(TS, True), dww, w_t, x, pre_g, dres)


def _grad_tn(a, b, tm, out=None, rows=None, row0=0, name="grad_tn", rider=None):
    s, m = a.shape
    n = b.shape[1]
    ts = min(2048, s)
    rows = m if rows is None else rows
    assert m % tm == 0 and s % ts == 0
    ns = s // ts
    if row0 % tm == 0:
        out_spec = pl.BlockSpec((tm, n), lambda i, k: (row0 // tm + i, 0))
    else:
        align = 16
        assert row0 % align == 0 and tm % align == 0
        out_spec = pl.BlockSpec((pl.Element(tm), pl.Element(n)),
                                lambda i, k: (pl.multiple_of(row0 + i * tm, align), 0))

    def body(*refs):
        a_ref, b_ref = refs[0], refs[1]
        o_ref, acc = refs[-2], refs[-1]
        k = pl.program_id(1)

        @pl.when(k == 0)
        def _():
            acc[...] = jnp.zeros_like(acc)

        acc[...] += _dot_tn(a_ref[...], b_ref[...])

        @pl.when(k == ns - 1)
        def _():
            o_ref[...] = acc[...].astype(BF16)

    in_specs = [pl.BlockSpec((ts, tm), lambda i, k: (k, i)), pl.BlockSpec((ts, n), lambda i, k: (k, 0))]
    args = [a, b]
    aliases = {}
    if out is not None:
        in_specs.append(pl.BlockSpec(memory_space=pltpu.HBM))
        args.append(out)
        aliases = {2: 0}
    (res,), got = _host_call(
        body, grid=(m // tm, ns), name=name, rider=rider, aliases=aliases,
        out_shape=(jax.ShapeDtypeStruct((rows, n), BF16),), in_specs=in_specs, out_specs=(out_spec,),
        scratch_shapes=[pltpu.VMEM((tm, n), F32)], args=args)
    return res if rider is None else (res, got)


def _place():
    x, y, c = lax.axis_index("x"), lax.axis_index("y"), lax.axis_index("c")
    chips = [(1 - x, y), (x, 1 - y), (1 - x, 1 - y)]
    return x, y, c, chips


def _hbm_specs(n):
    return [pl.BlockSpec(memory_space=pltpu.HBM)] * n


def _row_tile(r):
    for cand in (512, 400, 304, 256, 192, 128, 96, 16):
        if r % cand == 0:
            return cand
    raise ValueError(r)


def _place_shard(shard, place, dtype, name, after=None):
    r, cc = shard.shape
    tr = _row_tile(r)
    nt = r // tr

    def body(place_ref, s_ref, *rest):
        rest[-1][...] = s_ref[...].astype(dtype)

    in_specs = [pl.BlockSpec((tr, cc), lambda i, pr: (i, 0))]
    args = [shard]
    if after is not None:
        in_specs.append(pl.BlockSpec(after.shape, lambda i, pr: (0, 0)))
        args.append(after)
    return pl.pallas_call(
        body, name=name, out_shape=jax.ShapeDtypeStruct((N_CHIPS * r, cc), dtype),
        grid_spec=pltpu.PrefetchScalarGridSpec(
            num_scalar_prefetch=1, grid=(nt,), in_specs=in_specs,
            out_specs=pl.BlockSpec((tr, cc), lambda i, pr: (pr[1] * nt + i, 0))),
        compiler_params=_cp(("arbitrary",)),
    )(place, *args)


class _GatherRider:
    has_mid = True

    def __init__(self, fulls):
        n = len(fulls)
        self.inputs = list(fulls)
        self.out_shapes = [jax.ShapeDtypeStruct(a.shape, a.dtype) for a in fulls]
        self.aliases = {a: a for a in range(n)}
        self.sems = [pltpu.SemaphoreType.DMA((6 * n,)), pltpu.SemaphoreType.DMA((6 * n,))]
        self.block_rows = [a.shape[0] // N_CHIPS for a in fulls]

    def _ctx(self, outs, sems):
        send_sems, recv_sems = sems
        x, y, c, chips = _place()

        def rows(a, k, half):
            r = self.block_rows[a]
            return outs[a].at[pl.ds(k * r + half * (r // 2), r // 2)]

        def copy(a, j, blk, to):
            return pltpu.make_async_remote_copy(src_ref=blk, dst_ref=blk, send_sem=send_sems.at[a * 6 + j],
                                                recv_sem=recv_sems.at[a * 6 + j], device_id=to, device_id_type=MESH)

        return x, y, c, chips, rows, copy

    def start(self, ins, outs, sems, peers=(0, 1, 2)):
        x, y, c, chips, rows, copy = self._ctx(outs, sems)
        for j in peers:
            for a in range(len(outs)):
                copy(a, j, rows(a, 2 * x + y, c), (*chips[j], c)).start()

    def mid(self, ins, outs, sems, peers=(0, 1, 2)):
        x, y, c, chips, rows, copy = self._ctx(outs, sems)
        for j in peers:
            px, py = chips[j]
            for a in range(len(outs)):
                copy(a, j, rows(a, 2 * px + py, c), (px, py, c)).wait_recv()
                copy(a, 3 + j, rows(a, 2 * px + py, c), (x, y, 1 - c)).start()

    def wait_forwarded(self, outs, sems, peers=(0, 1, 2)):
        x, y, c, chips, rows, copy = self._ctx(outs, sems)
        for j in peers:
            px, py = chips[j]
            for a in range(len(outs)):
                copy(a, 3 + j, rows(a, 2 * px + py, 1 - c), (x, y, 1 - c)).wait_recv()

    def wait_sends(self, outs, sems):
        x, y, c, chips, rows, copy = self._ctx(outs, sems)
        for j, (px, py) in enumerate(chips):
            for a in range(len(outs)):
                copy(a, j, rows(a, 2 * x + y, c), (px, py, c)).wait_send()
                copy(a, 3 + j, rows(a, 2 * px + py, c), (x, y, 1 - c)).wait_send()

    def end(self, ins, outs, sems):
        self.wait_forwarded(outs, sems)
        self.wait_sends(outs, sems)


def _swap_halves(grads, small, name):
    n = len(grads)
    arrs = list(grads) + ([small] if small is not None else [])
    m = len(arrs)

    def body(*refs):
        ins, outs = refs[:m], refs[m:2 * m]
        send_sems, recv_sems = refs[2 * m:]
        x, y, c, _ = _place()
        sibling = (x, y, 1 - c)
        cps = []
        for a in range(m):
            src = ins[a].at[:, 1 - c] if a < n else ins[a]
            cp = pltpu.make_async_remote_copy(src_ref=src, dst_ref=outs[a], send_sem=send_sems.at[a],
                                              recv_sem=recv_sems.at[a], device_id=sibling, device_id_type=MESH)
            cp.start()
            cps.append(cp)
        for cp in cps:
            cp.wait_recv()
        for cp in cps:
            cp.wait_send()

    outs = tuple(jax.ShapeDtypeStruct((g.shape[0],) + g.shape[2:], g.dtype) for g in grads)
    if small is not None:
        outs += (jax.ShapeDtypeStruct(small.shape, small.dtype),)
    return pl.pallas_call(
        body, name=name, out_shape=outs, in_specs=_hbm_specs(m), out_specs=tuple(_hbm_specs(m)),
        scratch_shapes=[pltpu.SemaphoreType.DMA((m,)), pltpu.SemaphoreType.DMA((m,))],
    )(*arrs)


def _pair_sum(g, recv, place, name):
    _, _, h, cc = g.shape
    th = h

    def body(c_ref, g_ref, r_ref, o_ref):
        o_ref[...] = (g_ref[...].astype(F32) + r_ref[...].astype(F32)).astype(o_ref.dtype)

    return pl.pallas_call(
        body, name=name, out_shape=jax.ShapeDtypeStruct(recv.shape, recv.dtype),
        grid_spec=pltpu.PrefetchScalarGridSpec(
            num_scalar_prefetch=1, grid=(N_CHIPS, h // th),
            in_specs=[pl.BlockSpec((None, None, th, cc), lambda k, r, c_ref: (k, c_ref[0], r, 0)),
                      pl.BlockSpec((None, th, cc), lambda k, r, c_ref: (k, r, 0))],
            out_specs=pl.BlockSpec((None, th, cc), lambda k, r, c_ref: (k, r, 0))),
        compiler_params=_cp(("arbitrary", "arbitrary")),
    )(place, g, recv)


def _finish_reduce(pack, halves):
    rows, cc = pack.shape
    hs = rows // 2
    n = len(halves)

    def body(*refs):
        pack_ref = refs[0]
        out_ref = refs[1 + n]
        big = refs[2 + n:2 + 2 * n]
        sib_ref, parts_ref, send_sems, recv_sems, big_send, big_recv = refs[2 + 2 * n:]
        x, y, c, chips = _place()
        me_k = 2 * x + y
        sibling = (x, y, 1 - c)
        mine = pl.ds(pl.multiple_of(c * hs, hs), hs)
        theirs = pl.ds(pl.multiple_of((1 - c) * hs, hs), hs)
        shared = [pltpu.make_async_remote_copy(src_ref=big[a].at[c], dst_ref=big[a].at[c], send_sem=big_send.at[a],
                                               recv_sem=big_recv.at[a], device_id=sibling, device_id_type=MESH)
                  for a in range(n)]
        for cp in shared:
            cp.start()
        first = pltpu.make_async_remote_copy(src_ref=pack_ref, dst_ref=sib_ref, send_sem=send_sems.at[0],
                                             recv_sem=recv_sems.at[0], device_id=sibling, device_id_type=MESH)
        first.start()
        first.wait()
        parts_ref[me_k] = pack_ref[mine, :] + sib_ref[mine, :]
        cps = [pltpu.make_async_remote_copy(src_ref=parts_ref.at[me_k], dst_ref=parts_ref.at[me_k],
                                            send_sem=send_sems.at[1 + j], recv_sem=recv_sems.at[1 + j],
                                            device_id=(px, py, c), device_id_type=MESH)
               for j, (px, py) in enumerate(chips)]
        for cp in cps:
            cp.start()
        for j, (px, py) in enumerate(chips):
            pltpu.make_async_remote_copy(src_ref=parts_ref.at[2 * px + py], dst_ref=parts_ref.at[2 * px + py],
                                         send_sem=send_sems.at[1 + j], recv_sem=recv_sems.at[1 + j],
                                         device_id=(px, py, c), device_id_type=MESH).wait_recv()
        for cp in cps:
            cp.wait_send()
        out_ref[mine, :] = ((parts_ref[0] + parts_ref[1]) + parts_ref[2]) + parts_ref[3]
        last = pltpu.make_async_remote_copy(src_ref=out_ref.at[mine], dst_ref=out_ref.at[mine],
                                            send_sem=send_sems.at[4], recv_sem=recv_sems.at[4], device_id=sibling,
                                            device_id_type=MESH)
        last.start()
        pltpu.make_async_remote_copy(src_ref=out_ref.at[theirs], dst_ref=out_ref.at[theirs],
                                     send_sem=send_sems.at[4], recv_sem=recv_sems.at[4], device_id=sibling,
                                     device_id_type=MESH).wait_recv()
        last.wait_send()
        for a in range(n):
            pltpu.make_async_remote_copy(src_ref=big[a].at[1 - c], dst_ref=big[a].at[1 - c], send_sem=big_send.at[a],
                                         recv_sem=big_recv.at[a], device_id=sibling,
                                         device_id_type=MESH).wait_recv()
        for cp in shared:
            cp.wait_send()

    vmem = pl.BlockSpec(memory_space=pltpu.VMEM)
    res = pl.pallas_call(
        body, name="finish_reduce",
        out_shape=(jax.ShapeDtypeStruct(pack.shape, pack.dtype),)
        + tuple(jax.ShapeDtypeStruct(g.shape, g.dtype) for g in halves),
        in_specs=[vmem] + _hbm_specs(n), out_specs=(vmem,) + tuple(_hbm_specs(n)),
        input_output_aliases={1 + a: 1 + a for a in range(n)},
        scratch_shapes=[pltpu.VMEM((rows, cc), F32), pltpu.VMEM((N_CHIPS, hs, cc), F32),
                        pltpu.SemaphoreType.DMA((5,)), pltpu.SemaphoreType.DMA((5,)),
                        pltpu.SemaphoreType.DMA((n,)), pltpu.SemaphoreType.DMA((n,))],
        compiler_params=_cp(),
    )(pack, *halves)
    return res[0], tuple(res[1:])


class _ExchangeRider:
    has_mid = False

    def __init__(self, sums):
        self.inputs = list(sums)
        self.out_shapes = [jax.ShapeDtypeStruct((3,) + g.shape[1:], g.dtype) for g in sums]
        m = len(self.inputs)
        self.aliases = {}
        self.sems = [pltpu.SemaphoreType.DMA((3 * m,)), pltpu.SemaphoreType.DMA((3 * m,))]

    def _copies(self, ins, outs, sems):
        send_sems, recv_sems = sems
        _, _, c, chips = _place()
        return [pltpu.make_async_remote_copy(
            src_ref=ins[a].at[2 * px + py], dst_ref=outs[a].at[j], send_sem=send_sems.at[a * 3 + j],
            recv_sem=recv_sems.at[a * 3 + j], device_id=(px, py, c), device_id_type=MESH)
            for j, (px, py) in enumerate(chips) for a in range(len(ins))]

    def start(self, ins, outs, sems):
        for cp in self._copies(ins, outs, sems):
            cp.start()

    def end(self, ins, outs, sems):
        cps = self._copies(ins, outs, sems)
        for cp in cps:
            cp.wait_recv()
        for cp in cps:
            cp.wait_send()


def _chip_sum(own, parts, place, name):
    npart, h, cc = parts.shape
    th = _row_tile(h)

    def body(place_ref, own_ref, p_ref, o_ref):
        acc = own_ref[...].astype(F32) + p_ref[0].astype(F32)
        for k in range(1, npart):
            acc = acc + p_ref[k].astype(F32)
        o_ref[...] = acc

    return pl.pallas_call(
        body, name=name, out_shape=jax.ShapeDtypeStruct((2, h, cc), F32),
        grid_spec=pltpu.PrefetchScalarGridSpec(
            num_scalar_prefetch=1, grid=(h // th,),
            in_specs=[pl.BlockSpec((None, th, cc), lambda r, pr: (pr[1], r, 0)),
                      pl.BlockSpec((npart, th, cc), lambda r, pr: (0, r, 0))],
            out_specs=pl.BlockSpec((None, th, cc), lambda r, pr: (pr[0], r, 0))),
        compiler_params=_cp(("arbitrary",)),
    )(place, own, parts)


def _adamw_math(w, g, m, v):
    m = ADAM_B1 * m + (1.0 - ADAM_B1) * g
    v = ADAM_B2 * v + (1.0 - ADAM_B2) * (g * g)
    m_hat = m / (1.0 - ADAM_B1 ** ADAM_STEP)
    v_hat = v / (1.0 - ADAM_B2 ** ADAM_STEP)
    delta = -ADAM_LR * (m_hat / (jnp.sqrt(v_hat) + ADAM_EPS) + ADAM_WD * w)
    return delta, m, v


def _adamw_big(w, g, m, v, name):
    r, cc = w.shape
    tr = min(_row_tile(r), 256) if r % 256 == 0 else _row_tile(r)

    def body(w_ref, g_ref, m_ref, v_ref, go_ref, d_ref, mo_ref, vo_ref):
        g = g_ref[...]
        d, mm, vv = _adamw_math(w_ref[...], g, m_ref[...], v_ref[...])
        go_ref[...] = g
        d_ref[...] = d
        mo_ref[...] = mm
        vo_ref[...] = vv

    blk = pl.BlockSpec((tr, cc), lambda i: (i, 0))
    sd = jax.ShapeDtypeStruct((r, cc), F32)
    return pl.pallas_call(body, grid=(r // tr,), name=name, out_shape=(sd, sd, sd, sd), in_specs=[blk] * 4,
                          out_specs=(blk, blk, blk, blk), compiler_params=_cp(("arbitrary",)))(w, g, m, v)


def _adamw_small(ws, gs, ms, vs):
    n = len(ws)

    def body(*refs):
        for a in range(n):
            w_ref, g_ref, m_ref, v_ref = refs[4 * a:4 * a + 4]
            d_ref, mo_ref, vo_ref = refs[4 * n + 3 * a:4 * n + 3 * a + 3]
            d, mm, vv = _adamw_math(w_ref[...], g_ref[...], m_ref[...], v_ref[...])
            d_ref[...] = d
            mo_ref[...] = mm
            vo_ref[...] = vv

    args, outs = [], []
    for a in range(n):
        args += [ws[a], gs[a], ms[a], vs[a]]
        outs += [jax.ShapeDtypeStruct(ws[a].shape, F32)] * 3
    res = pl.pallas_call(body, name="adamw_small", out_shape=tuple(outs), compiler_params=_cp())(*args)
    return [res[3 * a:3 * a + 3] for a in range(n)]


def _flat_pack(arrs, rows):
    flat = jnp.concatenate([a.reshape(-1) for a in arrs])
    return jnp.pad(flat, (0, rows * D - flat.shape[0])).reshape(rows, D)


def _flat_unpack(flat, shapes):
    out, off = [], 0
    for shp in shapes:
        size = 1
        for d_ in shp:
            size *= d_
        out.append(flat[off:off + size].reshape(shp))
        off += size
    return out


SMALL_EVEN = ("even_pre_g", "even_a_ln_g", "even_a_ln_b", "even_a_ws", "even_a_bs", "even_b_conv", "even_mem_g",
              "even_post_g")
SMALL_ODD = ("odd_pre_g", "odd_c_wgrp", "odd_c_scale", "odd_d_dw_w", "odd_d_dw_b", "odd_d_ln_g", "odd_d_ln_b",
             "odd_d_pw_b", "odd_mem_g", "odd_post_g")
BIG = ("even_w_in", "even_w_kv", "even_w_out", "odd_w_in", "odd_d_pw_w", "odd_w_kv", "odd_w_out")
WEIGHTS = ("even_pre_g", "even_w_in", "even_a_ln_g", "even_a_ln_b", "even_a_ws", "even_a_bs", "even_b_conv",
           "even_mem_g", "even_w_kv", "even_w_out", "even_post_g", "odd_pre_g", "odd_w_in", "odd_c_wgrp",
           "odd_c_scale", "odd_d_dw_w", "odd_d_dw_b", "odd_d_ln_g", "odd_d_ln_b", "odd_d_pw_w", "odd_d_pw_b",
           "odd_mem_g", "odd_w_kv", "odd_w_out", "odd_post_g")
PACKED = (("even_b_conv", (3, 192)), ("odd_pre_g", (1, 256)), ("odd_c_scale", (1, 192)), ("odd_d_dw_w", (31, 192)),
          ("odd_d_dw_b", (1, 192)), ("odd_d_ln_g", (1, 192)), ("odd_d_ln_b", (1, 192)), ("odd_d_pw_b", (1, 192)),
          ("odd_mem_g", (1, 256)), ("odd_post_g", (1, 256)))
PACK_ROWS = 16
SMALL_ROWS = 256


def _four(g):
    return g.reshape(N_CHIPS, 2, g.shape[0] // (2 * N_CHIPS), g.shape[1])


def _step(x, mem, target, w, place):
    wt = {}
    pack = _flat_pack([w[n][0] for n, _ in PACKED], PACK_ROWS)
    shards = {"even_w_in_t": w["even_w_in"][0].T, "odd_w_in_t": w["odd_w_in"][0].T, "even_w_kv": w["even_w_kv"][0],
              "odd_w_kv": w["odd_w_kv"][0], "even_w_out": w["even_w_out"][0], "odd_w_out": w["odd_w_out"][0],
              "odd_d_pw_w": w["odd_d_pw_w"][0]}
    placed = {n: _place_shard(shards[n], place, BF16, "place_" + n) for n in ("even_w_in_t", "even_w_kv", "even_w_out")}
    placed["pack"] = _place_shard(pack, place, F32, "place_pack")

    order, group = _stream_tables(place[0], place[1], EVEN_IN)
    p_e, h_e, (wt["even_w_in_t"], packs), (wt["even_w_kv"], wt["even_w_out"]) = _in_fwd_streamed(
        x, w["even_pre_g"], [placed["even_w_in_t"], placed["pack"]], [placed["even_w_kv"], placed["even_w_out"]],
        order, group, "even_in_streamed")
    for n in ("odd_w_in_t", "odd_w_kv", "odd_w_out", "odd_d_pw_w"):
        placed[n] = _place_shard(shards[n], place, BF16, "place_" + n, after=p_e[0:16, 0:128])
    packs = packs.reshape(N_CHIPS, PACK_ROWS * D)
    per_chip = [_flat_unpack(packs[k], [shp for _, shp in PACKED]) for k in range(N_CHIPS)]
    for a, (name, _) in enumerate(PACKED):
        wt[name] = jnp.concatenate([per_chip[k][a] for k in range(N_CHIPS)], axis=-1)
    for name in ("even_pre_g", "even_a_ln_g", "even_a_ln_b", "even_mem_g", "even_post_g"):
        wt[name] = w[name]

    tril = jnp.tril(jnp.ones((CH, CH), dtype=bool))
    wcat = jnp.where(tril[None], w["even_a_ws"][0], 0.0).transpose(1, 0, 2).reshape(CH, 4 * CH).astype(BF16)
    bsg = jnp.repeat(w["even_a_bs"][0].T, BW // 4, axis=1)
    hsel = (jnp.arange(BW)[:, None] // (BW // 4) == jnp.arange(128)[None, :]).astype(BF16)
    g4 = BW // 4
    eye = jnp.eye(4, dtype=F32)
    wbd = (w["odd_c_wgrp"][0][:, :, None, :] * eye[:, None, :, None]).reshape(BW, BW).astype(BF16)

    kv_e = _kv_fwd(mem, wt["even_mem_g"], wt["even_w_kv"], "even_kv")
    (x1, o_e, y_e), (wt["odd_w_in_t"],) = _even_fwd(
        x, p_e, kv_e, wt["even_a_ln_g"], wt["even_a_ln_b"], wcat, bsg, wt["even_b_conv"], wt["even_w_out"],
        wt["even_post_g"], rider=_GatherRider([placed["odd_w_in_t"]]))
    names = ("odd_w_out", "odd_d_pw_w", "odd_w_kv")
    (p_o, h_o), got = _in_fwd(x1, wt["odd_pre_g"], wt["odd_w_in_t"], "odd_in",
                              rider=_GatherRider([placed[n] for n in names]))
    wt.update(zip(names, got))
    kv_o = _kv_fwd(mem, wt["odd_mem_g"], wt["odd_w_kv"], "odd_kv")
    dx2, o_o, cv_o, loss = _odd_fwd(x1, p_o, kv_o, wbd, wt["odd_c_scale"], wt["odd_d_dw_w"], wt["odd_d_dw_b"],
                                    wt["odd_d_ln_g"], wt["odd_d_ln_b"], wt["odd_d_pw_w"], wt["odd_d_pw_b"],
                                    wt["odd_w_out"], wt["odd_post_g"], target)
    (dpc_o, tmpc, tmpd, do_o, y_o, g_post_o, g_cs, g_wbd, g_dww, g_dwb, g_lng_o, g_lnb_o, g_pww, g_pwb,
     dkv_o) = _odd_bwd1(dx2, o_o, cv_o, p_o, kv_o, wbd, wt["odd_c_scale"], wt["odd_d_dw_w"], wt["odd_d_dw_b"],
                        wt["odd_d_ln_g"], wt["odd_d_ln_b"], wt["odd_d_pw_w"], wt["odd_d_pw_b"], wt["odd_w_out"],
                        wt["odd_post_g"])
    dpb_o, dx1, g_pre_o = _odd_bwd2(dpc_o, tmpc, tmpd, p_o, wt["odd_d_dw_w"], wt["odd_w_in_t"], x1,
                                    wt["odd_pre_g"], dx2)
    g_win_o = _grad_tn(dpb_o, h_o, 768, rows=ODD_IN, name="odd_gw_in_b")
    g_win_o = _grad_tn(dpc_o, h_o, 1280, out=g_win_o, rows=ODD_IN, row0=3 * BW, name="odd_gw_in_c")
    g_wout_o = _grad_tn(y_o, do_o, 1024, name="odd_gw_out")
    g_wkv_o, g_memg_o = _kv_bwd(mem, wt["odd_mem_g"], wt["odd_w_kv"], dkv_o, "odd_kv_bwd")
    big_o = [_four(g) for g in (g_win_o, g_pww.astype(BF16), g_wkv_o, g_wout_o)]
    recv_o = _swap_halves(big_o, None, "swap_halves_odd")
    sums_o = [_pair_sum(big_o[a], recv_o[a], place, "pair_sum_odd_%d" % a) for a in range(len(big_o))]
    (dp_e, do_e, g_post_e, g_lng_e, g_lnb_e, g_wcat, g_bs, g_bconv,
     dkv_e), parts_o = _even_bwd1(dx1, o_e, p_e, kv_e, wt["even_a_ln_g"], wt["even_a_ln_b"], wcat, bsg, hsel,
                                  wt["even_b_conv"], wt["even_w_out"], wt["even_post_g"],
                                  rider=_ExchangeRider(sums_o))
    halves_o = [_chip_sum(sums_o[a], parts_o[a], place, "chip_sum_odd_%d" % a) for a in range(len(big_o))]
    g_wout_e = _grad_tn(y_e, do_e, 1024, name="even_gw_out")
    g_wkv_e, g_memg_e = _kv_bwd(mem, wt["even_mem_g"], wt["even_w_kv"], dkv_e, "even_kv_bwd")
    big_x = [_four(g) for g in (g_wkv_e, g_wout_e)]
    recv_x = _swap_halves(big_x, None, "swap_halves_kv_out")
    sums_x = [_pair_sum(big_x[a], recv_x[a], place, "pair_sum_kv_out_%d" % a) for a in range(len(big_x))]
    g_win_e, parts_x = _grad_tn(dp_e, h_e, 1280, name="even_gw_in", rider=_ExchangeRider(sums_x))
    halves_x = [_chip_sum(sums_x[a], parts_x[a], place, "chip_sum_kv_out_%d" % a) for a in range(len(big_x))]
    big_e = [_four(g_win_e)]
    recv_e = _swap_halves(big_e, None, "swap_halves_even")
    sums_e = [_pair_sum(big_e[0], recv_e[0], place, "pair_sum_even_w_in")]
    (dx0, g_pre_e), parts_e = _even_bwd2(dp_e, wt["even_w_in_t"], x, wt["even_pre_g"], dx1,
                                         rider=_ExchangeRider(sums_e))
    halves_e = [_chip_sum(sums_e[0], parts_e[0], place, "chip_sum_even_w_in")]

    g_aws = jnp.where(tril[None], g_wcat.reshape(CH, 4, CH).transpose(1, 0, 2), 0.0)
    g_wgrp = jnp.stack([lax.dynamic_slice(g_wbd, (g * g4, g * g4), (g4, g4)) for g in range(4)])
    small = {
        "even_pre_g": g_pre_e, "even_a_ln_g": g_lng_e, "even_a_ln_b": g_lnb_e, "even_a_ws": g_aws,
        "even_a_bs": g_bs[:, 0:4].T, "even_b_conv": g_bconv[0:3], "even_mem_g": g_memg_e, "even_post_g": g_post_e,
        "odd_pre_g": g_pre_o, "odd_c_wgrp": g_wgrp, "odd_c_scale": g_cs, "odd_d_dw_w": g_dww.reshape(CONF, 8, BW).sum(axis=1),
        "odd_d_dw_b": g_dwb, "odd_d_ln_g": g_lng_o, "odd_d_ln_b": g_lnb_o, "odd_d_pw_b": g_pwb,
        "odd_mem_g": g_memg_o, "odd_post_g": g_post_o,
    }
    small_names = SMALL_EVEN + SMALL_ODD
    small_pack = _flat_pack([small[n] for n in small_names] + [loss[0, 0].reshape(1)], SMALL_ROWS)
    small_total, full = _finish_reduce(small_pack, halves_e + halves_x + halves_o)
    order = ("even_w_in", "even_w_kv", "even_w_out", "odd_w_in", "odd_d_pw_w", "odd_w_kv", "odd_w_out")
    gbig = {n: full[a].reshape(full[a].shape[1] * 2, full[a].shape[2]) for a, n in enumerate(order)}
    return dx0, gbig, small_total.reshape(-1), [small[n].shape for n in small_names]


def kernel(x, mem, even_pre_g, even_w_in, even_a_ln_g, even_a_ln_b, even_a_ws, even_a_bs, even_b_conv, even_mem_g, even_w_kv, even_w_out, even_post_g, odd_pre_g, odd_w_in, odd_c_wgrp, odd_c_scale, odd_d_dw_w, odd_d_dw_b, odd_d_ln_g, odd_d_ln_b, odd_d_pw_w, odd_d_pw_b, odd_mem_g, odd_w_kv, odd_w_out, odd_post_g, loss_target, m_even_pre_g, m_even_w_in, m_even_a_ln_g, m_even_a_ln_b, m_even_a_ws, m_even_a_bs, m_even_b_conv, m_even_mem_g, m_even_w_kv, m_even_w_out, m_even_post_g, m_odd_pre_g, m_odd_w_in, m_odd_c_wgrp, m_odd_c_scale, m_odd_d_dw_w, m_odd_d_dw_b, m_odd_d_ln_g, m_odd_d_ln_b, m_odd_d_pw_w, m_odd_d_pw_b, m_odd_mem_g, m_odd_w_kv, m_odd_w_out, m_odd_post_g, v_even_pre_g, v_even_w_in, v_even_a_ln_g, v_even_a_ln_b, v_even_a_ws, v_even_a_bs, v_even_b_conv, v_even_mem_g, v_even_w_kv, v_even_w_out, v_even_post_g, v_odd_pre_g, v_odd_w_in, v_odd_c_wgrp, v_odd_c_scale, v_odd_d_dw_w, v_odd_d_dw_b, v_odd_d_ln_g, v_odd_d_ln_b, v_odd_d_pw_w, v_odd_d_pw_b, v_odd_mem_g, v_odd_w_kv, v_odd_w_out, v_odd_post_g):
    given = dict(locals())
    w = {n: given[n] for n in WEIGHTS}
    mom = {n: given["m_" + n] for n in WEIGHTS}
    var = {n: given["v_" + n] for n in WEIGHTS}

    x_, y_, c_ = lax.axis_index("x"), lax.axis_index("y"), lax.axis_index("c")
    chip = 2 * x_ + y_
    place = jnp.stack([c_, chip]).astype(jnp.int32)
    grad_x, gbig, gsmall_flat, small_shapes = _step(x[0], mem[0], loss_target[0], w, place)

    names = SMALL_EVEN + SMALL_ODD
    grads = {}
    unpacked = _flat_unpack(gsmall_flat, small_shapes + [(1,)])
    loss = unpacked[-1][0]
    for n, g in zip(names, unpacked[:-1]):
        shard_shape = w[n].shape[1:]
        if g.shape[-1] != shard_shape[-1]:
            g = lax.dynamic_slice_in_dim(g, chip * shard_shape[-1], shard_shape[-1], axis=g.ndim - 1)
        grads[n] = g.reshape(shard_shape)

    def two_d(a):
        return a.reshape(-1, a.shape[-1])

    upd = {}
    for n in BIG:
        if n.endswith("w_in"):
            res = _adamw_big(w[n][0].T, gbig[n], mom[n][0].T, var[n][0].T, "adamw_" + n)
            res = tuple(r.T for r in res)
        else:
            res = _adamw_big(w[n][0], gbig[n], mom[n][0], var[n][0], "adamw_" + n)
        grads[n], upd[n] = res[0], res[1:]
    res = _adamw_small([two_d(w[n][0]) for n in names], [two_d(grads[n]) for n in names],
                       [two_d(mom[n][0]) for n in names], [two_d(var[n][0]) for n in names])
    for n, r in zip(names, res):
        upd[n] = r

    outs = [loss, grad_x[None]]
    outs += [grads[n].reshape(w[n].shape) for n in WEIGHTS]
    for j in range(3):
        outs += [upd[n][j].reshape(w[n].shape) for n in WEIGHTS]
    return tuple(outs)
```

```python
import jax
import jax.numpy as jnp
from jax import lax
from jax.experimental import pallas as pl
from jax.experimental.pallas import tpu as pltpu
from jax.experimental.pallas import tpu_sc as plsc

F32 = jnp.float32
BF16 = jnp.bfloat16
MESH = pl.DeviceIdType.MESH

D = 1024
N_MEM = 256
MIX = 2048
XA = 512
HD = 128
BW = 768
CH = 128
EPS = 1e-6
SCALE = HD ** -0.5
POOL_WINDOWS = (2, 4, 8, 16)
CONF = 31
EVEN_IN = 6400
ODD_IN = 4864
N_CHIPS = 4

ADAM_LR = 0.001
ADAM_B1 = 0.9
ADAM_B2 = 0.999
ADAM_EPS = 1e-08
ADAM_WD = 0.01
ADAM_STEP = 10

TS = 256
HALO = 32
VMEM_LIMIT = 56 * 1024 * 1024


def _cp(sem=None):
    return pltpu.CompilerParams(dimension_semantics=sem, vmem_limit_bytes=VMEM_LIMIT)


def _dot(a, b):
    return jnp.dot(a, b, preferred_element_type=F32)


def _dot_nt(a, b):
    return lax.dot_general(a, b, (((1,), (1,)), ((), ())), preferred_element_type=F32)


def _dot_tn(a, b):
    return lax.dot_general(a, b, (((0,), (0,)), ((), ())), preferred_element_type=F32)


def _sigmoid(x):
    return 1.0 / (1.0 + jnp.exp(-x))


def _resident(shape):
    return pl.BlockSpec(shape, lambda *_: (0,) * len(shape), pipeline_mode=pl.Buffered(1))


def _const(shape):
    return pl.BlockSpec(shape, lambda *_: (0,) * len(shape))


def _kv_fwd(mem, mem_g, wkv, name):
    def body(mem_ref, g_ref, w_ref, kv_ref):
        m = mem_ref[...]
        r = lax.rsqrt(jnp.mean(m * m, axis=-1, keepdims=True) + EPS)
        mn = (m * r * g_ref[...]).astype(BF16)
        kv_ref[...] = _dot(mn, w_ref[...]).astype(BF16)

    return pl.pallas_call(body, out_shape=jax.ShapeDtypeStruct((N_MEM, D), BF16), name=name,
                          compiler_params=_cp())(mem, mem_g, wkv)


def _kv_bwd(mem, mem_g, wkv, dkv, name):
    def body(mem_ref, g_ref, w_ref, dkv_ref, dw_ref, dg_ref):
        m = mem_ref[...]
        r = lax.rsqrt(jnp.mean(m * m, axis=-1, keepdims=True) + EPS)
        mh = m * r
        mn = (mh * g_ref[...]).astype(BF16)
        dkv = dkv_ref[...].astype(BF16)
        dw_ref[...] = _dot_tn(mn, dkv).astype(BF16)
        dmn = _dot_nt(dkv, w_ref[...])
        dg_ref[...] = jnp.sum(dmn * mh, axis=0, keepdims=True)

    return pl.pallas_call(body, out_shape=(jax.ShapeDtypeStruct((D, D), BF16), jax.ShapeDtypeStruct((1, D), F32)),
                          name=name, compiler_params=_cp())(mem, mem_g, wkv, dkv)


def _host_call(body, *, grid, name, out_shape, in_specs, out_specs, args, scratch_shapes=(), aliases=None,
               rider=None):
    sem = ("arbitrary",) * len(grid)
    aliases = dict(aliases or {})
    if rider is None:
        res = pl.pallas_call(body, grid=grid, name=name, out_shape=tuple(out_shape), in_specs=list(in_specs),
                             out_specs=tuple(out_specs), scratch_shapes=list(scratch_shapes),
                             input_output_aliases=aliases, compiler_params=_cp(sem))(*args)
        return tuple(res), ()
    n_in, n_out, n_sc = len(in_specs), len(out_specs), len(scratch_shapes)
    r_in, r_out = len(rider.inputs), len(rider.out_shapes)

    def full_body(*refs):
        host_in = refs[:n_in]
        rid_in = refs[n_in:n_in + r_in]
        host_out = refs[n_in + r_in:n_in + r_in + n_out]
        rid_out = refs[n_in + r_in + n_out:n_in + r_in + n_out + r_out]
        host_sc = refs[n_in + r_in + n_out + r_out:n_in + r_in + n_out + r_out + n_sc]
        sems = refs[n_in + r_in + n_out + r_out + n_sc:]
        first = pl.program_id(0) == 0
        last = pl.program_id(0) == grid[0] - 1
        for ax in range(1, len(grid)):
            first = jnp.logical_and(first, pl.program_id(ax) == 0)
            last = jnp.logical_and(last, pl.program_id(ax) == grid[ax] - 1)

        @pl.when(first)
        def _():
            rider.start(rid_in, rid_out, sems)

        if rider.has_mid:
            @pl.when(last)
            def _():
                rider.mid(rid_in, rid_out, sems)

        body(*host_in, *host_out, *host_sc)

        @pl.when(last)
        def _():
            rider.end(rid_in, rid_out, sems)

    aliases.update({n_in + j: n_out + k for j, k in rider.aliases.items()})
    res = pl.pallas_call(
        full_body, grid=grid, name=name, out_shape=tuple(out_shape) + tuple(rider.out_shapes),
        in_specs=list(in_specs) + _hbm_specs(r_in), out_specs=tuple(out_specs) + tuple(_hbm_specs(r_out)),
        scratch_shapes=list(scratch_shapes) + list(rider.sems), input_output_aliases=aliases,
        compiler_params=_cp(sem),
    )(*args, *rider.inputs)
    return tuple(res[:n_out]), tuple(res[n_out:])


def _in_fwd(x, pre_g, w_t, name, rider=None):
    s, n = x.shape[0], w_t.shape[0]
    tm = min(512, s)
    nc = 256

    def body(x_ref, g_ref, w_ref, p_ref, h_ref):
        xv = x_ref[...]
        r = lax.rsqrt(jnp.mean(xv * xv, axis=-1, keepdims=True) + EPS)
        h = (xv * r * g_ref[...]).astype(BF16)
        h_ref[...] = h
        for j in range(n // nc):
            p_ref[:, j * nc:(j + 1) * nc] = _dot_nt(h, w_ref[j * nc:(j + 1) * nc, :]).astype(BF16)

    return _host_call(
        body, grid=(s // tm,), name=name, rider=rider,
        out_shape=(jax.ShapeDtypeStruct((s, n), BF16), jax.ShapeDtypeStruct((s, D), BF16)),
        in_specs=[pl.BlockSpec((tm, D), lambda i: (i, 0)), _const((1, D)), _resident((n, D))],
        out_specs=(pl.BlockSpec((tm, n), lambda i: (i, 0)), pl.BlockSpec((tm, D), lambda i: (i, 0))),
        args=(x, pre_g, w_t))


NC = 256


def _stream_tables(core, chip, n):
    nchunk = n // NC
    idx = jnp.arange(nchunk, dtype=jnp.int32)
    src = jnp.array([0, 2, 1, 3], jnp.int32)
    r = n // N_CHIPS

    def group_of(row):
        j = src[(row // r) ^ chip]
        through_sibling = ((row % r) // (r // 2) != core).astype(jnp.int32)
        return jnp.where(j == 0, 0, 2 * j - 1 + through_sibling)

    grp = jnp.maximum(group_of(idx * NC), group_of(idx * NC + NC - 1))
    order = jnp.argsort(grp * 64 + idx).astype(jnp.int32)
    return order, grp[order]


def _in_fwd_streamed(x, pre_g, first, later, order, group, name):
    s, n = x.shape[0], first[0].shape[0]
    nchunk = n // NC
    rider = _GatherRider(first)
    rider2 = _GatherRider(later) if later else None
    a, m = len(first), len(later)
    tr = min(256, s)

    def body(*refs):
        order_ref, group_ref, x_ref, g_ref = refs[0:4]
        p_ref, h_ref = refs[4 + a + m:6 + a + m]
        outs = refs[6 + a + m:6 + 2 * a + m]
        outs2 = refs[6 + 2 * a + m:6 + 2 * a + 2 * m]
        wbuf, wsem, send_sems, recv_sems = refs[6 + 2 * a + 2 * m:10 + 2 * a + 2 * m]
        sems2 = refs[10 + 2 * a + 2 * m:]
        w_hbm = outs[0]
        j = pl.program_id(0)
        sems = (send_sems, recv_sems)
        grp = group_ref[j]
        new_group = jnp.logical_or(j == 0, group_ref[jnp.maximum(j - 1, 0)] != grp)
        slot = j % 2

        def fetch(step, sl):
            rows = pl.ds(pl.multiple_of(order_ref[step] * NC, NC), NC)
            return pltpu.make_async_copy(w_hbm.at[rows], wbuf.at[sl], wsem.at[sl])

        @pl.when(j == 0)
        def _():
            rider.start(None, outs, sems, peers=(0, 1))

            @pl.loop(0, s // tr)
            def _(t):
                rows = pl.ds(pl.multiple_of(t * tr, tr), tr)
                xv = x_ref[rows, :]
                r = lax.rsqrt(jnp.mean(xv * xv, axis=-1, keepdims=True) + EPS)
                h_ref[rows, :] = (xv * r * g_ref[...]).astype(BF16)

        before = jnp.where(j == 0, 0, group_ref[jnp.maximum(j - 1, 0)])

        def entering(b):
            return jnp.logical_and(before < b, b <= grp)

        for src in range(3):
            @pl.when(entering(2 * src + 1))
            def _(src=src):
                if src == 0:
                    rider.start(None, outs, sems, peers=(2,))
                rider.mid(None, outs, sems, peers=(src,))
                if src == 1 and rider2 is not None:
                    rider2.start(None, outs2, sems2)

            @pl.when(entering(2 * src + 2))
            def _(src=src):
                rider.wait_forwarded(outs, sems, peers=(src,))

        @pl.when(new_group)
        def _():
            fetch(j, slot).start()

        fetch(j, slot).wait()
        nxt = jnp.minimum(j + 1, nchunk - 1)

        @pl.when(jnp.logical_and(j + 1 < nchunk, group_ref[nxt] == grp))
        def _():
            fetch(nxt, 1 - slot).start()

        p_ref[...] = _dot_nt(h_ref[...], wbuf[slot]).astype(BF16)

        @pl.when(j == nchunk - 1)
        def _():
            rider.wait_sends(outs, sems)
            if rider2 is not None:
                rider2.mid(None, outs2, sems2)
                rider2.end(None, outs2, sems2)

    hbm = pl.BlockSpec(memory_space=pltpu.HBM)
    arrs = list(first) + list(later)
    whole = pl.BlockSpec((s, D), lambda j, o, g: (0, 0), pipeline_mode=pl.Buffered(1))
    res = pl.pallas_call(
        body, name=name,
        out_shape=(jax.ShapeDtypeStruct((s, n), BF16), jax.ShapeDtypeStruct((s, D), BF16))
        + tuple(jax.ShapeDtypeStruct(v.shape, v.dtype) for v in arrs),
        grid_spec=pltpu.PrefetchScalarGridSpec(
            num_scalar_prefetch=2, grid=(nchunk,),
            in_specs=[whole, pl.BlockSpec((1, D), lambda j, o, g: (0, 0))] + [hbm] * (a + m),
            out_specs=(pl.BlockSpec((s, NC), lambda j, o, g: (0, o[j])),
                       pl.BlockSpec((s, D), lambda j, o, g: (0, 0))) + (hbm,) * (a + m),
            scratch_shapes=[pltpu.VMEM((2, NC, D), BF16), pltpu.SemaphoreType.DMA((2,))] + list(rider.sems)
            + (list(rider2.sems) if rider2 is not None else [])),
        input_output_aliases={4 + v: 2 + v for v in range(a + m)},
        compiler_params=_cp(("arbitrary",)),
    )(order, group, x, pre_g, *arrs)
    return res[0], res[1], tuple(res[2:2 + a]), tuple(res[2 + a:])


def _xattn_fwd(q, kv_ref):
    outs, probs = [], []
    for h in range(XA // HD):
        qh = q[:, h * HD:(h + 1) * HD]
        kh = kv_ref[:, h * HD:(h + 1) * HD]
        vh = kv_ref[:, XA + h * HD:XA + (h + 1) * HD]
        sc = _dot_nt(qh, kh) * SCALE
        e = jnp.exp(sc - jnp.max(sc, axis=-1, keepdims=True))
        pr = e / jnp.sum(e, axis=-1, keepdims=True)
        outs.append(_dot(pr.astype(BF16), vh))
        probs.append(pr)
    return jnp.concatenate(outs, axis=-1), probs


def _xattn_bwd(dyx, q, probs, kv_ref, dkv_ref):
    dqs = []
    for h in range(XA // HD):
        qh = q[:, h * HD:(h + 1) * HD]
        kh = kv_ref[:, h * HD:(h + 1) * HD]
        vh = kv_ref[:, XA + h * HD:XA + (h + 1) * HD]
        dy = dyx[:, h * HD:(h + 1) * HD].astype(BF16)
        pr = probs[h]
        dp = _dot_nt(dy, vh)
        ds = (pr * (dp - jnp.sum(dp * pr, axis=-1, keepdims=True))).astype(BF16)
        dqs.append(_dot(ds, kh) * SCALE)
        dkv_ref[:, h * HD:(h + 1) * HD] += _dot_tn(ds, qh) * SCALE
        dkv_ref[:, XA + h * HD:XA + (h + 1) * HD] += _dot_tn(pr.astype(BF16), dy)
    return jnp.concatenate(dqs, axis=-1)


def _layer_norm_fwd(v, g, b):
    mu = jnp.mean(v, axis=-1, keepdims=True)
    vc = v - mu
    rstd = lax.rsqrt(jnp.mean(vc * vc, axis=-1, keepdims=True) + EPS)
    vhat = vc * rstd
    return vhat * g + b, vhat, rstd


def _layer_norm_bwd(dy, vhat, rstd, g):
    dvh = dy * g
    return rstd * (dvh - jnp.mean(dvh, axis=-1, keepdims=True) - vhat * jnp.mean(dvh * vhat, axis=-1, keepdims=True))


def _head_masks():
    col = lax.broadcasted_iota(jnp.int32, (1, BW), 1)
    return [(col >= h * (BW // 4)) & (col < (h + 1) * (BW // 4)) for h in range(4)]


def _halo_prev(nblk_per_tile):
    return lambda i: (jnp.maximum(i * nblk_per_tile - 1, 0), 0)


def _row_ids(i, t):
    return i * t + lax.broadcasted_iota(jnp.int32, (t, 1), 0)


def _even_mix(i, p_ref, ph_ref, ln_g, ln_b, wcat_ref, bsg_ref, bconv_ref, wbuf):
    t = p_ref.shape[0]
    u = p_ref[:, 0:BW].astype(F32)
    v = p_ref[:, BW:2 * BW].astype(F32)
    bg = p_ref[:, 2 * BW:3 * BW].astype(F32)
    cg = p_ref[:, 3 * BW:4 * BW].astype(F32)
    xin = p_ref[:, 4 * BW:5 * BW].astype(F32)
    vn, vhat, rstd = _layer_norm_fwd(v, ln_g, ln_b)
    masks = _head_masks()
    sgs, vsts = [], []
    for n in range(t // CH):
        vn_c = vn[n * CH:(n + 1) * CH]
        vst = jnp.concatenate([jnp.where(m, vn_c, 0.0) for m in masks], axis=0).astype(BF16)
        sgs.append(_dot(wcat_ref[...], vst) + bsg_ref[...])
        vsts.append(vst)
    sg = jnp.concatenate(sgs, axis=0)
    ya = u * sg
    w_halo = ph_ref[:, 3 * BW:4 * BW].astype(F32) * ph_ref[:, 4 * BW:5 * BW].astype(F32)
    wbuf[0:HALO, :] = jnp.where(i > 0, w_halo, 0.0)
    wbuf[HALO:HALO + t, :] = cg * xin
    conv = (bconv_ref[0:1, :] * wbuf[pl.ds(HALO - 2, t), :] + bconv_ref[1:2, :] * wbuf[pl.ds(HALO - 1, t), :]
            + bconv_ref[2:3, :] * wbuf[pl.ds(HALO, t), :])
    yb = bg * conv
    return dict(u=u, bg=bg, cg=cg, xin=xin, vhat=vhat, rstd=rstd, sg=sg, vsts=vsts, conv=conv, ya=ya, yb=yb,
                masks=masks)


def _pool_select(vals):
    col = lax.broadcasted_iota(jnp.int32, (1, BW), 1)
    g = BW // 4
    return jnp.where(col < g, vals[0], jnp.where(col < 2 * g, vals[1], jnp.where(col < 3 * g, vals[2], vals[3])))


def _inv_counts(i, t):
    rows = _row_ids(i, t) + 1
    return [1.0 / jnp.minimum(rows, w).astype(F32) for w in POOL_WINDOWS]


def _band_matrices(t, forward):
    j = jnp.arange(t)[:, None]
    r = jnp.arange(HALO + t)[None, :]
    if forward:
        return jnp.stack([(r >= j) & (r < j + w) for w in POOL_WINDOWS]).astype(BF16)
    return jnp.stack([(r <= HALO + j) & (r > HALO + j - w) for w in POOL_WINDOWS]).astype(BF16)


SHIFT_ROWS = HALO + TS - 8


def _shifted_copies(buf, sh):
    for b in range(1, 8):
        sh[b - 1] = buf[pl.ds(b, SHIFT_ROWS), :]


def _rows_at(buf, sh, off, t):
    a, b = divmod(off, 8)
    return buf[pl.ds(8 * a, t), :] if b == 0 else sh[b - 1, pl.ds(8 * a, t), :]


def _tap_sums(d_ref, buf, sh, base, out_ref):
    t = d_ref.shape[0]
    group = 4
    for k0 in range(0, CONF, group):
        taps = list(range(k0, min(k0 + group, CONF)))

        def step(r, accs, taps=taps):
            row = pl.multiple_of(r * 8, 8)
            d = d_ref[pl.ds(row, 8), :]
            new = []
            for acc, k in zip(accs, taps):
                a, b = divmod(base + k, 8)
                src = buf[pl.ds(row + 8 * a, 8), :] if b == 0 else sh[b - 1, pl.ds(row + 8 * a, 8), :]
                new.append(acc + d * src)
            return tuple(new)

        accs = lax.fori_loop(0, t // 8, step, tuple(jnp.zeros((8, BW), F32) for _ in taps), unroll=2)
        for acc, k in zip(accs, taps):
            out_ref[8 * k:8 * k + 8, :] += acc


def _odd_mix(i, p_ref, ph_ref, bands_ref, wbd_ref, cscale, dww_ref, dwb, ln_g, ln_b, pww_ref, pwb, gbuf, gsh,
             cv=None):
    t = p_ref.shape[0]
    zc_bf = p_ref[:, 0:BW]
    zc = zc_bf.astype(F32)
    ga = p_ref[:, BW:2 * BW].astype(F32)
    gb = p_ref[:, 2 * BW:3 * BW].astype(F32)
    zh = ph_ref[:, 0:BW]
    zcat = jnp.concatenate([jnp.where(i > 0, zh, jnp.zeros_like(zh)), zc_bf], axis=0)
    inv = _inv_counts(i, t)
    pooled = _pool_select([_dot(bands_ref[w], zcat) * inv[w] for w in range(len(POOL_WINDOWS))]) - zc
    pooled_bf = pooled.astype(BF16)
    pre = _dot(pooled_bf, wbd_ref[...])
    yc = pre * cscale
    sgb = _sigmoid(gb)
    z = ga * sgb
    gh_a = ph_ref[:, BW:2 * BW].astype(F32)
    gh_b = ph_ref[:, 2 * BW:3 * BW].astype(F32)
    gbuf[0:HALO, :] = jnp.where(i > 0, gh_a * _sigmoid(gh_b), 0.0)
    gbuf[HALO:HALO + t, :] = z
    _shifted_copies(gbuf, gsh)
    if cv is None:
        cv = dwb + dww_ref[CONF - 1:CONF, :] * z
        for k in range(CONF - 1):
            cv = cv + dww_ref[k:k + 1, :] * _rows_at(gbuf, gsh, HALO - (CONF - 1) + k, t)
    zl, zhat, rstd = _layer_norm_fwd(cv, ln_g, ln_b)
    szl = _sigmoid(zl)
    zs = (zl * szl).astype(BF16)
    yd = _dot(zs, pww_ref[...]) + pwb
    return dict(ga=ga, sgb=sgb, pooled_bf=pooled_bf, pre=pre, yc=yc, zhat=zhat, rstd=rstd, zl=zl, szl=szl,
                zs=zs, yd=yd, inv=inv, cv=cv)


def _post_norm(o, post_g):
    r = lax.rsqrt(jnp.mean(o * o, axis=-1, keepdims=True) + EPS)
    return o * r, r


def _gate_out(y_a, y_b, y_x, gate, wout_ref):
    sgt = _sigmoid(gate)
    sgate = gate * sgt
    ys = [(y_a * sgate[:, 0:BW]).astype(BF16), (y_b * sgate[:, BW:2 * BW]).astype(BF16),
          (y_x * sgate[:, 2 * BW:MIX]).astype(BF16)]
    o = (_dot(ys[0], wout_ref[0:BW, :]) + _dot(ys[1], wout_ref[BW:2 * BW, :]) + _dot(ys[2], wout_ref[2 * BW:MIX, :]))
    return o, ys, sgt, sgate


def _tile_specs(s, n):
    nh = TS // HALO
    return pl.BlockSpec((TS, n), lambda i: (i, 0)), pl.BlockSpec((HALO, n), _halo_prev(nh))


def _even_fwd(x, p, kv, ln_g, ln_b, wcat, bsg, bconv, wout, post_g, rider=None):
    s = x.shape[0]

    def body(x_ref, p_ref, ph_ref, kv_ref, lng, lnb, wcat_ref, bsg_ref, bconv_ref, wout_ref, pg, x1_ref, o_ref,
             y_ref, wbuf):
        i = pl.program_id(0)
        mx = _even_mix(i, p_ref, ph_ref, lng[...], lnb[...], wcat_ref, bsg_ref, bconv_ref, wbuf)
        yx, _ = _xattn_fwd(p_ref[:, 5 * BW:5 * BW + XA], kv_ref)
        gate = p_ref[:, 5 * BW + XA:EVEN_IN].astype(F32)
        o, ys, _, _ = _gate_out(mx["ya"], mx["yb"], yx, gate, wout_ref)
        y_ref[:, 0:BW] = ys[0]
        y_ref[:, BW:2 * BW] = ys[1]
        y_ref[:, 2 * BW:MIX] = ys[2]
        n, _ = _post_norm(o, pg[...])
        o_ref[...] = o
        x1_ref[...] = x_ref[...] + n * pg[...]

    tile, halo = _tile_specs(s, EVEN_IN)
    row = pl.BlockSpec((TS, D), lambda i: (i, 0))
    return _host_call(
        body, grid=(s // TS,), name="even_fwd", rider=rider,
        out_shape=(jax.ShapeDtypeStruct((s, D), F32), jax.ShapeDtypeStruct((s, D), F32),
                   jax.ShapeDtypeStruct((s, MIX), BF16)),
        in_specs=[row, tile, halo, _const((N_MEM, D)), _const((1, BW)), _const((1, BW)), _const((CH, 4 * CH)),
                  _const((CH, BW)), _const((3, BW)), _resident((MIX, D)), _const((1, D))],
        out_specs=(row, row, pl.BlockSpec((TS, MIX), lambda i: (i, 0))),
        scratch_shapes=[pltpu.VMEM((HALO + TS, BW), F32)],
        args=(x, p, p, kv, ln_g, ln_b, wcat, bsg, bconv, wout, post_g))


def _odd_fwd(x1, p, kv, wbd, cscale, dww, dwb, ln_g, ln_b, pww, pwb, wout, post_g, target):
    s = x1.shape[0]

    def body(x_ref, p_ref, ph_ref, kv_ref, bands_ref, wbd_ref, cs, dww_ref, dwb_ref, lng, lnb, pww_ref, pwb_ref,
             wout_ref, pg, tgt_ref, dx_ref, o_ref, cv_ref, loss_ref, gbuf, gsh):
        i = pl.program_id(0)
        mx = _odd_mix(i, p_ref, ph_ref, bands_ref, wbd_ref, cs[...], dww_ref, dwb_ref[...], lng[...], lnb[...],
                      pww_ref, pwb_ref[...], gbuf, gsh)
        cv_ref[...] = mx["cv"]
        yx, _ = _xattn_fwd(p_ref[:, 3 * BW:3 * BW + XA], kv_ref)
        gate = p_ref[:, 3 * BW + XA:ODD_IN].astype(F32)
        o, _, _, _ = _gate_out(mx["yc"], mx["yd"], yx, gate, wout_ref)
        n, _ = _post_norm(o, pg[...])
        o_ref[...] = o
        err = x_ref[...] + n * pg[...] - tgt_ref[...]
        dx_ref[...] = err * (1.0 / D)

        @pl.when(i == 0)
        def _():
            loss_ref[...] = jnp.zeros_like(loss_ref)

        loss_ref[...] += 0.5 * jnp.sum(jnp.sum(err * err, axis=-1, keepdims=True) * (1.0 / D), axis=0, keepdims=True)

    tile, halo = _tile_specs(s, ODD_IN)
    row = pl.BlockSpec((TS, D), lambda i: (i, 0))
    vec = _const((1, BW))
    return pl.pallas_call(
        body, grid=(s // TS,), name="odd_fwd",
        out_shape=(jax.ShapeDtypeStruct((s, D), F32), jax.ShapeDtypeStruct((s, D), F32),
                   jax.ShapeDtypeStruct((s, BW), F32), jax.ShapeDtypeStruct((8, 128), F32)),
        in_specs=[row, tile, halo, _const((N_MEM, D)), _const((4, TS, HALO + TS)), _const((BW, BW)), vec,
                  _const((CONF, BW)), vec, vec, vec, _const((BW, BW)), vec, _resident((MIX, D)), _const((1, D)), row],
        out_specs=(row, row, pl.BlockSpec((TS, BW), lambda i: (i, 0)), _const((8, 128))),
        scratch_shapes=[pltpu.VMEM((HALO + TS, BW), F32), pltpu.VMEM((7, SHIFT_ROWS, BW), F32)],
        compiler_params=_cp(("arbitrary",)),
    )(x1, p, p, kv, _band_matrices(TS, False), wbd, cscale, dww, dwb, ln_g, ln_b, pww, pwb, wout, post_g, target)


def _acc_init(i, refs):
    @pl.when(i == 0)
    def _():
        for r in refs:
            r[...] = jnp.zeros_like(r)


def _post_norm_bwd(dx, o, pg, dpg_ref):
    n, r = _post_norm(o, pg)
    dpg_ref[...] += jnp.sum(dx * n, axis=0, keepdims=True)
    dn = dx * pg
    return (r * (dn - n * jnp.mean(dn * n, axis=-1, keepdims=True))).astype(BF16)


def _gate_bwd(do, wout_ref, ys_f32, gate, y_ref):
    dy = _dot_nt(do, wout_ref[...])
    sgt = _sigmoid(gate)
    sgate = gate * sgt
    dsilu = sgt * (1.0 + gate * (1.0 - sgt))
    offs = (0, BW, 2 * BW, MIX)
    dys, dgs = [], []
    for j, yv in enumerate(ys_f32):
        a, b = offs[j], offs[j + 1]
        if y_ref is not None:
            y_ref[:, a:b] = (yv * sgate[:, a:b]).astype(BF16)
        dys.append(dy[:, a:b] * sgate[:, a:b])
        dgs.append(dy[:, a:b] * yv * dsilu[:, a:b])
    return dys, jnp.concatenate(dgs, axis=-1)


NEXT = 16


def _even_bwd1(dx, o, p, kv, ln_g, ln_b, wcat, bsg, hsel, bconv, wout, post_g, rider=None):
    s = dx.shape[0]
    nt = s // TS

    def body(dx_ref, o_ref, p_ref, ph_ref, dxn_ref, on_ref, pn_ref, kv_ref, lng, lnb, wcat_ref, bsg_ref, hsel_ref,
             bconv_ref, wout_ref, pg,
             dp_ref, do_ref, dpg_ref, dlng_ref, dlnb_ref, dwcat_ref, dbs_ref, dbconv_ref, dkv_ref, wbuf, dbuf):
        i = pl.program_id(0)
        _acc_init(i, (dpg_ref, dlng_ref, dlnb_ref, dwcat_ref, dbs_ref, dbconv_ref, dkv_ref))
        mx = _even_mix(i, p_ref, ph_ref, lng[...], lnb[...], wcat_ref, bsg_ref, bconv_ref, wbuf)
        q = p_ref[:, 5 * BW:5 * BW + XA]
        yx, probs = _xattn_fwd(q, kv_ref)
        gate = p_ref[:, 5 * BW + XA:EVEN_IN].astype(F32)
        do = _post_norm_bwd(dx_ref[...], o_ref[...], pg[...], dpg_ref)
        do_ref[...] = do
        (dya, dyb, dyx), dgate = _gate_bwd(do, wout_ref, (mx["ya"], mx["yb"], yx), gate, None)
        dp_ref[:, 0:BW] = (dya * mx["sg"]).astype(BF16)
        dsg = (dya * mx["u"]).astype(BF16)
        dvns = []
        for n in range(TS // CH):
            dsg_c = dsg[n * CH:(n + 1) * CH]
            dvst = _dot_tn(wcat_ref[...], dsg_c)
            dvn_c = jnp.where(mx["masks"][0], dvst[0:CH], 0.0)
            for h in range(1, 4):
                dvn_c = dvn_c + jnp.where(mx["masks"][h], dvst[h * CH:(h + 1) * CH], 0.0)
            dvns.append(dvn_c)
            dwcat_ref[...] += _dot_nt(dsg_c, mx["vsts"][n])
            dbs_ref[...] += _dot(dsg_c, hsel_ref[...])
        dvn = jnp.concatenate(dvns, axis=0)
        dlng_ref[...] += jnp.sum(dvn * mx["vhat"], axis=0, keepdims=True)
        dlnb_ref[...] += jnp.sum(dvn, axis=0, keepdims=True)
        dp_ref[:, BW:2 * BW] = _layer_norm_bwd(dvn, mx["vhat"], mx["rstd"], lng[...]).astype(BF16)
        dp_ref[:, 2 * BW:3 * BW] = (dyb * mx["conv"]).astype(BF16)
        dconv = dyb * mx["bg"]
        for k in range(3):
            dbconv_ref[k:k + 1, :] += jnp.sum(dconv * wbuf[pl.ds(HALO - 2 + k, TS), :], axis=0, keepdims=True)
        n_n, r_n = _post_norm(on_ref[...], pg[...])
        dn_n = dxn_ref[...] * pg[...]
        do_n = (r_n * (dn_n - n_n * jnp.mean(dn_n * n_n, axis=-1, keepdims=True))).astype(BF16)
        dy_n = _dot_nt(do_n, wout_ref[BW:2 * BW, :])
        g_n = pn_ref[:, 5 * BW + XA + BW:5 * BW + XA + 2 * BW].astype(F32)
        dconv_n = dy_n * (g_n * _sigmoid(g_n)) * pn_ref[:, 2 * BW:3 * BW].astype(F32)
        dbuf[0:TS, :] = dconv
        dbuf[TS:TS + NEXT, :] = jnp.where(i < nt - 1, dconv_n, 0.0)
        dw = (bconv_ref[2:3, :] * dconv + bconv_ref[1:2, :] * dbuf[pl.ds(1, TS), :]
              + bconv_ref[0:1, :] * dbuf[pl.ds(2, TS), :])
        dp_ref[:, 3 * BW:4 * BW] = (dw * mx["xin"]).astype(BF16)
        dp_ref[:, 4 * BW:5 * BW] = (dw * mx["cg"]).astype(BF16)
        dp_ref[:, 5 * BW:5 * BW + XA] = _xattn_bwd(dyx, q, probs, kv_ref, dkv_ref).astype(BF16)
        dp_ref[:, 5 * BW + XA:EVEN_IN] = dgate.astype(BF16)

    tile, halo = _tile_specs(s, EVEN_IN)
    row = pl.BlockSpec((TS, D), lambda i: (i, 0))
    vec = _const((1, BW))
    nxt = _halo_next(TS // NEXT, s // NEXT)

    def out(n):
        return pl.BlockSpec((TS, n), lambda i: (i, 0))

    return _host_call(
        body, grid=(nt,), name="even_bwd1", rider=rider,
        out_shape=(jax.ShapeDtypeStruct((s, EVEN_IN), BF16), jax.ShapeDtypeStruct((s, D), BF16),
                   jax.ShapeDtypeStruct((1, D), F32), jax.ShapeDtypeStruct((1, BW), F32),
                   jax.ShapeDtypeStruct((1, BW), F32), jax.ShapeDtypeStruct((CH, 4 * CH), F32),
                   jax.ShapeDtypeStruct((CH, 128), F32), jax.ShapeDtypeStruct((8, BW), F32),
                   jax.ShapeDtypeStruct((N_MEM, D), F32)),
        in_specs=[row, row, tile, halo, pl.BlockSpec((NEXT, D), nxt), pl.BlockSpec((NEXT, D), nxt),
                  pl.BlockSpec((NEXT, EVEN_IN), nxt), _const((N_MEM, D)), vec, vec, _const((CH, 4 * CH)),
                  _const((CH, BW)), _const((BW, 128)), _const((3, BW)), _resident((MIX, D)), _const((1, D))],
        out_specs=(out(EVEN_IN), out(D),
                   _const((1, D)), vec, vec, _const((CH, 4 * CH)), _const((CH, 128)), _const((8, BW)),
                   _const((N_MEM, D))),
        scratch_shapes=[pltpu.VMEM((HALO + TS, BW), F32), pltpu.VMEM((TS + NEXT, BW), F32)],
        args=(dx, o, p, p, dx, o, p, kv, ln_g, ln_b, wcat, bsg, hsel, bconv, wout, post_g))


def _odd_bwd1(dx, o, cv, p, kv, wbd, cscale, dww, dwb, ln_g, ln_b, pww, pwb, wout, post_g):
    s = dx.shape[0]

    def body(dx_ref, o_ref, cv_ref, p_ref, ph_ref, kv_ref, bands_ref, wbd_ref, cs, dww_ref, dwb_ref, lng, lnb,
             pww_ref, pwb_ref, wout_ref, pg,
             dpc_ref, tmpc_ref, tmpd_ref, do_ref, y_ref, dpg_ref, dcs_ref, dwbd_ref, ddww_ref, ddwb_ref, dlng_ref,
             dlnb_ref, dpww_ref, dpwb_ref, dkv_ref, gbuf, gsh, dcv_buf):
        i = pl.program_id(0)
        _acc_init(i, (dpg_ref, dcs_ref, dwbd_ref, ddww_ref, ddwb_ref, dlng_ref, dlnb_ref, dpww_ref, dpwb_ref,
                      dkv_ref))
        mx = _odd_mix(i, p_ref, ph_ref, bands_ref, wbd_ref, cs[...], dww_ref, dwb_ref[...], lng[...], lnb[...],
                      pww_ref, pwb_ref[...], gbuf, gsh, cv=cv_ref[...])
        q = p_ref[:, 3 * BW:3 * BW + XA]
        yx, probs = _xattn_fwd(q, kv_ref)
        gate = p_ref[:, 3 * BW + XA:ODD_IN].astype(F32)
        do = _post_norm_bwd(dx_ref[...], o_ref[...], pg[...], dpg_ref)
        do_ref[...] = do
        (dyc, dyd, dyx), dgate = _gate_bwd(do, wout_ref, (mx["yc"], mx["yd"], yx), gate, y_ref)
        dcs_ref[...] += jnp.sum(dyc * mx["pre"], axis=0, keepdims=True)
        dpre = (dyc * cs[...]).astype(BF16)
        dwbd_ref[...] += _dot_tn(mx["pooled_bf"], dpre)
        dpooled = _dot_nt(dpre, wbd_ref[...])
        tmpc_ref[...] = _pool_select([dpooled * c_ for c_ in mx["inv"]]).astype(BF16)
        dyd_bf = dyd.astype(BF16)
        dpwb_ref[...] += jnp.sum(dyd, axis=0, keepdims=True)
        dpww_ref[...] += _dot_tn(mx["zs"], dyd_bf)
        dzs = _dot_nt(dyd_bf, pww_ref[...])
        zl, szl = mx["zl"], mx["szl"]
        dzl = dzs * (szl * (1.0 + zl * (1.0 - szl)))
        dlng_ref[...] += jnp.sum(dzl * mx["zhat"], axis=0, keepdims=True)
        dlnb_ref[...] += jnp.sum(dzl, axis=0, keepdims=True)
        dcv = _layer_norm_bwd(dzl, mx["zhat"], mx["rstd"], lng[...])
        tmpd_ref[...] = dcv.astype(BF16)
        ddwb_ref[...] += jnp.sum(dcv, axis=0, keepdims=True)
        dcv_buf[...] = dcv
        _tap_sums(dcv_buf, gbuf, gsh, HALO - (CONF - 1), ddww_ref)
        dpc_ref[:, 0:XA] = _xattn_bwd(dyx, q, probs, kv_ref, dkv_ref).astype(BF16)
        dpc_ref[:, XA:XA + MIX] = dgate.astype(BF16)

    tile, halo = _tile_specs(s, ODD_IN)
    row = pl.BlockSpec((TS, D), lambda i: (i, 0))
    vec = _const((1, BW))

    def out(n):
        return pl.BlockSpec((TS, n), lambda i: (i, 0))

    return pl.pallas_call(
        body, grid=(s // TS,), name="odd_bwd1",
        out_shape=(jax.ShapeDtypeStruct((s, XA + MIX), BF16), jax.ShapeDtypeStruct((s, BW), BF16),
                   jax.ShapeDtypeStruct((s, BW), BF16), jax.ShapeDtypeStruct((s, D), BF16),
                   jax.ShapeDtypeStruct((s, MIX), BF16),
                   jax.ShapeDtypeStruct((1, D), F32), jax.ShapeDtypeStruct((1, BW), F32),
                   jax.ShapeDtypeStruct((BW, BW), F32), jax.ShapeDtypeStruct((8 * CONF, BW), F32),
                   jax.ShapeDtypeStruct((1, BW), F32), jax.ShapeDtypeStruct((1, BW), F32),
                   jax.ShapeDtypeStruct((1, BW), F32), jax.ShapeDtypeStruct((BW, BW), F32),
                   jax.ShapeDtypeStruct((1, BW), F32), jax.ShapeDtypeStruct((N_MEM, D), F32)),
        in_specs=[row, row, out(BW), tile, halo, _const((N_MEM, D)), _const((4, TS, HALO + TS)), _const((BW, BW)), vec,
                  _const((CONF, BW)), vec, vec, vec, _const((BW, BW)), vec, _resident((MIX, D)), _const((1, D))],
        out_specs=(out(XA + MIX), out(BW), out(BW), out(D), out(MIX),
                   _const((1, D)), vec, _const((BW, BW)), _const((8 * CONF, BW)), vec, vec, vec, _const((BW, BW)), vec,
                   _const((N_MEM, D))),
        scratch_shapes=[pltpu.VMEM((HALO + TS, BW), F32), pltpu.VMEM((7, SHIFT_ROWS, BW), F32),
                        pltpu.VMEM((TS, BW), F32)],
        compiler_params=_cp(("arbitrary",)),
    )(dx, o, cv, p, p, kv, _band_matrices(TS, False), wbd, cscale, dww, dwb, ln_g, ln_b, pww, pwb, wout, post_g)


def _halo_next(nblk_per_tile, nblk):
    return lambda i: (jnp.minimum((i + 1) * nblk_per_tile, nblk - 1), 0)


def _pre_norm_bwd(dh, x, pre_g, dres, dpre_ref):
    r = lax.rsqrt(jnp.mean(x * x, axis=-1, keepdims=True) + EPS)
    xh = x * r
    dpre_ref[...] += jnp.sum(dh * xh, axis=0, keepdims=True)
    dxh = dh * pre_g
    return dres + r * (dxh - xh * jnp.mean(dxh * xh, axis=-1, keepdims=True))


def _even_bwd2(dp, w_t, x, pre_g, dres, rider=None):
    s = x.shape[0]
    tm = min(512, s)

    def body(dp_ref, w_ref, x_ref, pg, dres_ref, dx_ref, dpre_ref):
        _acc_init(pl.program_id(0), (dpre_ref,))
        dh = _dot(dp_ref[...], w_ref[...])
        dx_ref[...] = _pre_norm_bwd(dh, x_ref[...], pg[...], dres_ref[...], dpre_ref)

    row = pl.BlockSpec((tm, D), lambda i: (i, 0))
    return _host_call(
        body, grid=(s // tm,), name="even_bwd2", rider=rider,
        out_shape=(jax.ShapeDtypeStruct((s, D), F32), jax.ShapeDtypeStruct((1, D), F32)),
        in_specs=[pl.BlockSpec((tm, EVEN_IN), lambda i: (i, 0)), _resident((EVEN_IN, D)), row, _const((1, D)), row],
        out_specs=(row, _const((1, D))),
        args=(dp, w_t, x, pre_g, dres))


def _odd_bwd2(dpc, tmpc, tmpd, p, dww, w_t, x, pre_g, dres):
    s = x.shape[0]
    nt = s // TS

    def body(dpc_ref, tc_ref, tch_ref, td_ref, tdh_ref, ga_ref, gb_ref, bands_ref, dww_ref, w_ref, x_ref, pg,
             dres_ref, dpb_ref, dx_ref, dpre_ref, dbuf, dsh):
        i = pl.program_id(0)
        _acc_init(i, (dpre_ref,))
        more = i < nt - 1
        e_bf = tc_ref[...]
        eh = tch_ref[...]
        ecat = jnp.concatenate([e_bf, jnp.where(more, eh, jnp.zeros_like(eh))], axis=0)
        dbuf[0:TS, :] = td_ref[...].astype(F32)
        dbuf[TS:TS + HALO, :] = jnp.where(more, tdh_ref[...].astype(F32), 0.0)
        sums = [_dot(bands_ref[w], ecat) for w in range(len(POOL_WINDOWS))]
        rows = _row_ids(i, TS) + 1
        cnt = _pool_select([jnp.minimum(rows, w).astype(F32) for w in POOL_WINDOWS])
        dzc = (_pool_select(sums) - e_bf.astype(F32) * cnt).astype(BF16)
        _shifted_copies(dbuf, dsh)
        dz = dww_ref[CONF - 1:CONF, :] * dbuf[pl.ds(0, TS), :]
        for sft in range(1, CONF):
            dz = dz + dww_ref[CONF - 1 - sft:CONF - sft, :] * _rows_at(dbuf, dsh, sft, TS)
        ga = ga_ref[...].astype(F32)
        sgb = _sigmoid(gb_ref[...].astype(F32))
        dga = (dz * sgb).astype(BF16)
        dgb = (dz * ga * sgb * (1.0 - sgb)).astype(BF16)
        dpb_ref[:, 0:BW] = dzc
        dpb_ref[:, BW:2 * BW] = dga
        dpb_ref[:, 2 * BW:3 * BW] = dgb
        dh = (_dot(dzc, w_ref[0:BW, :]) + _dot(dga, w_ref[BW:2 * BW, :]) + _dot(dgb, w_ref[2 * BW:3 * BW, :])
              + _dot(dpc_ref[...], w_ref[3 * BW:ODD_IN, :]))
        dx_ref[...] = _pre_norm_bwd(dh, x_ref[...], pg[...], dres_ref[...], dpre_ref)

    row = pl.BlockSpec((TS, D), lambda i: (i, 0))

    def tile(n, j=0):
        return pl.BlockSpec((TS, n), lambda i: (i, j))

    nxt = pl.BlockSpec((HALO, BW), _halo_next(TS // HALO, s // HALO))
    return pl.pallas_call(
        body, grid=(nt,), name="odd_bwd2",
        out_shape=(jax.ShapeDtypeStruct((s, 3 * BW), BF16), jax.ShapeDtypeStruct((s, D), F32),
                   jax.ShapeDtypeStruct((1, D), F32)),
        in_specs=[tile(XA + MIX), tile(BW), nxt, tile(BW), nxt, tile(BW, 1), tile(BW, 2), _const((4, TS, HALO + TS)),
                  _const((CONF, BW)), _resident((ODD_IN, D)), row, _const((1, D)), row],
        out_specs=(tile(3 * BW), row, _const((1, D))),
        scratch_shapes=[pltpu.VMEM((TS + HALO, BW), F32), pltpu.VMEM((7, SHIFT_ROWS, BW), F32)],
        compiler_params=_cp(("arbitrary",)),
    )(dpc, tmpc, tmpc, tmpd, tmpd, p, p, _band_matrices(TS, True), dww, w_t, x, pre_g, dres)


def _grad_tn(a, b, tm, out=None, rows=None, row0=0, name="grad_tn", rider=None):
    s, m = a.shape
    n = b.shape[1]
    ts = min(2048, s)
    rows = m if rows is None else rows
    assert m % tm == 0 and s % ts == 0
    ns = s // ts
    if row0 % tm == 0:
        out_spec = pl.BlockSpec((tm, n), lambda i, k: (row0 // tm + i, 0))
    else:
        align = 16
        assert row0 % align == 0 and tm % align == 0
        out_spec = pl.BlockSpec((pl.Element(tm), pl.Element(n)),
                                lambda i, k: (pl.multiple_of(row0 + i * tm, align), 0))

    def body(*refs):
        a_ref, b_ref = refs[0], refs[1]
        o_ref, acc = refs[-2], refs[-1]
        k = pl.program_id(1)

        @pl.when(k == 0)
        def _():
            acc[...] = jnp.zeros_like(acc)

        acc[...] += _dot_tn(a_ref[...], b_ref[...])

        @pl.when(k == ns - 1)
        def _():
            o_ref[...] = acc[...].astype(BF16)

    in_specs = [pl.BlockSpec((ts, tm), lambda i, k: (k, i)), pl.BlockSpec((ts, n), lambda i, k: (k, 0))]
    args = [a, b]
    aliases = {}
    if out is not None:
        in_specs.append(pl.BlockSpec(memory_space=pltpu.HBM))
        args.append(out)
        aliases = {2: 0}
    (res,), got = _host_call(
        body, grid=(m // tm, ns), name=name, rider=rider, aliases=aliases,
        out_shape=(jax.ShapeDtypeStruct((rows, n), BF16),), in_specs=in_specs, out_specs=(out_spec,),
        scratch_shapes=[pltpu.VMEM((tm, n), F32)], args=args)
    return res if rider is None else (res, got)


def _place():
    x, y, c = lax.axis_index("x"), lax.axis_index("y"), lax.axis_index("c")
    chips = [(1 - x, y), (x, 1 - y), (1 - x, 1 - y)]
    return x, y, c, chips


def _hbm_specs(n):
    return [pl.BlockSpec(memory_space=pltpu.HBM)] * n


def _row_tile(r):
    for cand in (512, 400, 304, 256, 192, 128, 96, 16):
        if r % cand == 0:
            return cand
    raise ValueError(r)


def _place_shard(shard, place, dtype, name, after=None):
    r, cc = shard.shape
    tr = _row_tile(r)
    nt = r // tr

    def body(place_ref, s_ref, *rest):
        rest[-1][...] = s_ref[...].astype(dtype)

    in_specs = [pl.BlockSpec((tr, cc), lambda i, pr: (i, 0))]
    args = [shard]
    if after is not None:
        in_specs.append(pl.BlockSpec(after.shape, lambda i, pr: (0, 0)))
        args.append(after)
    return pl.pallas_call(
        body, name=name, out_shape=jax.ShapeDtypeStruct((N_CHIPS * r, cc), dtype),
        grid_spec=pltpu.PrefetchScalarGridSpec(
            num_scalar_prefetch=1, grid=(nt,), in_specs=in_specs,
            out_specs=pl.BlockSpec((tr, cc), lambda i, pr: (pr[1] * nt + i, 0))),
        compiler_params=_cp(("arbitrary",)),
    )(place, *args)


class _GatherRider:
    has_mid = True

    def __init__(self, fulls):
        n = len(fulls)
        self.inputs = list(fulls)
        self.out_shapes = [jax.ShapeDtypeStruct(a.shape, a.dtype) for a in fulls]
        self.aliases = {a: a for a in range(n)}
        self.sems = [pltpu.SemaphoreType.DMA((6 * n,)), pltpu.SemaphoreType.DMA((6 * n,))]
        self.block_rows = [a.shape[0] // N_CHIPS for a in fulls]

    def _ctx(self, outs, sems):
        send_sems, recv_sems = sems
        x, y, c, chips = _place()

        def rows(a, k, half):
            r = self.block_rows[a]
            return outs[a].at[pl.ds(k * r + half * (r // 2), r // 2)]

        def copy(a, j, blk, to):
            return pltpu.make_async_remote_copy(src_ref=blk, dst_ref=blk, send_sem=send_sems.at[a * 6 + j],
                                                recv_sem=recv_sems.at[a * 6 + j], device_id=to, device_id_type=MESH)

        return x, y, c, chips, rows, copy

    def start(self, ins, outs, sems, peers=(0, 1, 2)):
        x, y, c, chips, rows, copy = self._ctx(outs, sems)
        for j in peers:
            for a in range(len(outs)):
                copy(a, j, rows(a, 2 * x + y, c), (*chips[j], c)).start()

    def mid(self, ins, outs, sems, peers=(0, 1, 2)):
        x, y, c, chips, rows, copy = self._ctx(outs, sems)
        for j in peers:
            px, py = chips[j]
            for a in range(len(outs)):
                copy(a, j, rows(a, 2 * px + py, c), (px, py, c)).wait_recv()
                copy(a, 3 + j, rows(a, 2 * px + py, c), (x, y, 1 - c)).start()

    def wait_forwarded(self, outs, sems, peers=(0, 1, 2)):
        x, y, c, chips, rows, copy = self._ctx(outs, sems)
        for j in peers:
            px, py = chips[j]
            for a in range(len(outs)):
                copy(a, 3 + j, rows(a, 2 * px + py, 1 - c), (x, y, 1 - c)).wait_recv()

    def wait_sends(self, outs, sems):
        x, y, c, chips, rows, copy = self._ctx(outs, sems)
        for j, (px, py) in enumerate(chips):
            for a in range(len(outs)):
                copy(a, j, rows(a, 2 * x + y, c), (px, py, c)).wait_send()
                copy(a, 3 + j, rows(a, 2 * px + py, c), (x, y, 1 - c)).wait_send()

    def end(self, ins, outs, sems):
        self.wait_forwarded(outs, sems)
        self.wait_sends(outs, sems)


def _swap_halves(grads, name, share=()):
    n, k = len(grads), len(share)
    m = n + k

    def body(*refs):
        ins, outs = refs[:m], refs[m:2 * m]
        send_sems, recv_sems = refs[2 * m:]
        x, y, c, _ = _place()
        sibling = (x, y, 1 - c)
        cps, waits = [], []
        for a in range(m):
            if a < n:
                cp = pltpu.make_async_remote_copy(src_ref=ins[a].at[:, 1 - c], dst_ref=outs[a],
                                                  send_sem=send_sems.at[a], recv_sem=recv_sems.at[a],
                                                  device_id=sibling, device_id_type=MESH)
                waits.append(cp)
            else:
                cp = pltpu.make_async_remote_copy(src_ref=outs[a].at[c], dst_ref=outs[a].at[c],
                                                  send_sem=send_sems.at[a], recv_sem=recv_sems.at[a],
                                                  device_id=sibling, device_id_type=MESH)
                waits.append(pltpu.make_async_remote_copy(
                    src_ref=outs[a].at[1 - c], dst_ref=outs[a].at[1 - c], send_sem=send_sems.at[a],
                    recv_sem=recv_sems.at[a], device_id=sibling, device_id_type=MESH))
            cp.start()
            cps.append(cp)
        for cp in waits:
            cp.wait_recv()
        for cp in cps:
            cp.wait_send()

    outs = tuple(jax.ShapeDtypeStruct((g.shape[0],) + g.shape[2:], g.dtype) for g in grads)
    outs += tuple(jax.ShapeDtypeStruct(g.shape, g.dtype) for g in share)
    res = pl.pallas_call(
        body, name=name, out_shape=outs, in_specs=_hbm_specs(m), out_specs=tuple(_hbm_specs(m)),
        input_output_aliases={n + a: n + a for a in range(k)},
        scratch_shapes=[pltpu.SemaphoreType.DMA((m,)), pltpu.SemaphoreType.DMA((m,))],
    )(*grads, *share)
    return tuple(res[:n]), tuple(res[n:])


def _pair_sum(g, recv, place, name):
    _, _, h, cc = g.shape
    th = h

    def body(c_ref, g_ref, r_ref, o_ref):
        o_ref[...] = (g_ref[...].astype(F32) + r_ref[...].astype(F32)).astype(o_ref.dtype)

    return pl.pallas_call(
        body, name=name, out_shape=jax.ShapeDtypeStruct(recv.shape, recv.dtype),
        grid_spec=pltpu.PrefetchScalarGridSpec(
            num_scalar_prefetch=1, grid=(N_CHIPS, h // th),
            in_specs=[pl.BlockSpec((None, None, th, cc), lambda k, r, c_ref: (k, c_ref[0], r, 0)),
                      pl.BlockSpec((None, th, cc), lambda k, r, c_ref: (k, r, 0))],
            out_specs=pl.BlockSpec((None, th, cc), lambda k, r, c_ref: (k, r, 0))),
        compiler_params=_cp(("arbitrary", "arbitrary")),
    )(place, g, recv)


def _finish_reduce(pack, halves):
    rows, cc = pack.shape
    hs = rows // 2
    n = len(halves)

    def body(*refs):
        pack_ref = refs[0]
        out_ref = refs[1 + n]
        big = refs[2 + n:2 + 2 * n]
        sib_ref, parts_ref, send_sems, recv_sems, big_send, big_recv = refs[2 + 2 * n:]
        x, y, c, chips = _place()
        me_k = 2 * x + y
        sibling = (x, y, 1 - c)
        mine = pl.ds(pl.multiple_of(c * hs, hs), hs)
        theirs = pl.ds(pl.multiple_of((1 - c) * hs, hs), hs)
        shared = [pltpu.make_async_remote_copy(src_ref=big[a].at[c], dst_ref=big[a].at[c], send_sem=big_send.at[a],
                                               recv_sem=big_recv.at[a], device_id=sibling, device_id_type=MESH)
                  for a in range(n)]
        for cp in shared:
            cp.start()
        first = pltpu.make_async_remote_copy(src_ref=pack_ref, dst_ref=sib_ref, send_sem=send_sems.at[0],
                                             recv_sem=recv_sems.at[0], device_id=sibling, device_id_type=MESH)
        first.start()
        first.wait()
        parts_ref[me_k] = pack_ref[mine, :] + sib_ref[mine, :]
        cps = [pltpu.make_async_remote_copy(src_ref=parts_ref.at[me_k], dst_ref=parts_ref.at[me_k],
                                            send_sem=send_sems.at[1 + j], recv_sem=recv_sems.at[1 + j],
                                            device_id=(px, py, c), device_id_type=MESH)
               for j, (px, py) in enumerate(chips)]
        for cp in cps:
            cp.start()
        for j, (px, py) in enumerate(chips):
            pltpu.make_async_remote_copy(src_ref=parts_ref.at[2 * px + py], dst_ref=parts_ref.at[2 * px + py],
                                         send_sem=send_sems.at[1 + j], recv_sem=recv_sems.at[1 + j],
                                         device_id=(px, py, c), device_id_type=MESH).wait_recv()
        for cp in cps:
            cp.wait_send()
        out_ref[mine, :] = ((parts_ref[0] + parts_ref[1]) + parts_ref[2]) + parts_ref[3]
        last = pltpu.make_async_remote_copy(src_ref=out_ref.at[mine], dst_ref=out_ref.at[mine],
                                            send_sem=send_sems.at[4], recv_sem=recv_sems.at[4], device_id=sibling,
                                            device_id_type=MESH)
        last.start()
        pltpu.make_async_remote_copy(src_ref=out_ref.at[theirs], dst_ref=out_ref.at[theirs],
                                     send_sem=send_sems.at[4], recv_sem=recv_sems.at[4], device_id=sibling,
                                     device_id_type=MESH).wait_recv()
        last.wait_send()
        for a in range(n):
            pltpu.make_async_remote_copy(src_ref=big[a].at[1 - c], dst_ref=big[a].at[1 - c], send_sem=big_send.at[a],
                                         recv_sem=big_recv.at[a], device_id=sibling,
                                         device_id_type=MESH).wait_recv()
        for cp in shared:
            cp.wait_send()

    vmem = pl.BlockSpec(memory_space=pltpu.VMEM)
    res = pl.pallas_call(
        body, name="finish_reduce",
        out_shape=(jax.ShapeDtypeStruct(pack.shape, pack.dtype),)
        + tuple(jax.ShapeDtypeStruct(g.shape, g.dtype) for g in halves),
        in_specs=[vmem] + _hbm_specs(n), out_specs=(vmem,) + tuple(_hbm_specs(n)),
        input_output_aliases={1 + a: 1 + a for a in range(n)},
        scratch_shapes=[pltpu.VMEM((rows, cc), F32), pltpu.VMEM((N_CHIPS, hs, cc), F32),
                        pltpu.SemaphoreType.DMA((5,)), pltpu.SemaphoreType.DMA((5,)),
                        pltpu.SemaphoreType.DMA((n,)), pltpu.SemaphoreType.DMA((n,))],
        compiler_params=_cp(),
    )(pack, *halves)
    return res[0], tuple(res[1:])


class _ExchangeRider:
    has_mid = False

    def __init__(self, sums):
        self.inputs = list(sums)
        self.out_shapes = [jax.ShapeDtypeStruct((3,) + g.shape[1:], g.dtype) for g in sums]
        m = len(self.inputs)
        self.aliases = {}
        self.sems = [pltpu.SemaphoreType.DMA((3 * m,)), pltpu.SemaphoreType.DMA((3 * m,))]

    def _copies(self, ins, outs, sems):
        send_sems, recv_sems = sems
        _, _, c, chips = _place()
        return [pltpu.make_async_remote_copy(
            src_ref=ins[a].at[2 * px + py], dst_ref=outs[a].at[j], send_sem=send_sems.at[a * 3 + j],
            recv_sem=recv_sems.at[a * 3 + j], device_id=(px, py, c), device_id_type=MESH)
            for j, (px, py) in enumerate(chips) for a in range(len(ins))]

    def start(self, ins, outs, sems):
        for cp in self._copies(ins, outs, sems):
            cp.start()

    def end(self, ins, outs, sems):
        cps = self._copies(ins, outs, sems)
        for cp in cps:
            cp.wait_recv()
        for cp in cps:
            cp.wait_send()


def _chip_sum(own, parts, place, name):
    npart, h, cc = parts.shape
    th = _row_tile(h)

    def body(place_ref, own_ref, p_ref, o_ref):
        acc = own_ref[...].astype(F32) + p_ref[0].astype(F32)
        for k in range(1, npart):
            acc = acc + p_ref[k].astype(F32)
        o_ref[...] = acc

    return pl.pallas_call(
        body, name=name, out_shape=jax.ShapeDtypeStruct((2, h, cc), F32),
        grid_spec=pltpu.PrefetchScalarGridSpec(
            num_scalar_prefetch=1, grid=(h // th,),
            in_specs=[pl.BlockSpec((None, th, cc), lambda r, pr: (pr[1], r, 0)),
                      pl.BlockSpec((npart, th, cc), lambda r, pr: (0, r, 0))],
            out_specs=pl.BlockSpec((None, th, cc), lambda r, pr: (pr[0], r, 0))),
        compiler_params=_cp(("arbitrary",)),
    )(place, own, parts)


def _adamw_math(w, g, m, v):
    m = ADAM_B1 * m + (1.0 - ADAM_B1) * g
    v = ADAM_B2 * v + (1.0 - ADAM_B2) * (g * g)
    m_hat = m / (1.0 - ADAM_B1 ** ADAM_STEP)
    v_hat = v / (1.0 - ADAM_B2 ** ADAM_STEP)
    delta = -ADAM_LR * (m_hat / (jnp.sqrt(v_hat) + ADAM_EPS) + ADAM_WD * w)
    return delta, m, v


def _adamw_big(w, g, m, v, name):
    r, cc = w.shape
    tr = min(_row_tile(r), 256) if r % 256 == 0 else _row_tile(r)

    def body(w_ref, g_ref, m_ref, v_ref, go_ref, d_ref, mo_ref, vo_ref):
        g = g_ref[...]
        d, mm, vv = _adamw_math(w_ref[...], g, m_ref[...], v_ref[...])
        go_ref[...] = g
        d_ref[...] = d
        mo_ref[...] = mm
        vo_ref[...] = vv

    blk = pl.BlockSpec((tr, cc), lambda i: (i, 0))
    sd = jax.ShapeDtypeStruct((r, cc), F32)
    return pl.pallas_call(body, grid=(r // tr,), name=name, out_shape=(sd, sd, sd, sd), in_specs=[blk] * 4,
                          out_specs=(blk, blk, blk, blk), compiler_params=_cp(("arbitrary",)))(w, g, m, v)


SC_TILES = 32


def _adamw_sparsecore(w, g, m, v, name):
    r, cc = w.shape
    rows_per = 8
    groups = r // rows_per
    assert r % rows_per == 0 and cc % 16 == 0

    def body(w_hbm, g_hbm, m_hbm, v_hbm, go_hbm, d_hbm, mo_hbm, vo_hbm, wb, gb, mb, vb, db):
        tile = lax.axis_index("sc_tile") * 2 + lax.axis_index("sc_core")

        @pl.loop(0, -(-groups // SC_TILES))
        def _(q):
            grp = tile + q * SC_TILES

            @pl.when(grp < groups)
            def _():
                rows = pl.ds(pl.multiple_of(grp * rows_per, rows_per), rows_per)
                pltpu.sync_copy(w_hbm.at[rows], wb)
                pltpu.sync_copy(g_hbm.at[rows], gb)
                pltpu.sync_copy(m_hbm.at[rows], mb)
                pltpu.sync_copy(v_hbm.at[rows], vb)

                @pl.loop(0, rows_per)
                def _(i):
                    @pl.loop(0, cc, step=16)
                    def _(j):
                        at = (i, pl.ds(j, 16))
                        d, mm, vv = _adamw_math(wb[at], gb[at], mb[at], vb[at])
                        db[at] = d
                        mb[at] = mm
                        vb[at] = vv

                pltpu.sync_copy(gb, go_hbm.at[rows])
                pltpu.sync_copy(db, d_hbm.at[rows])
                pltpu.sync_copy(mb, mo_hbm.at[rows])
                pltpu.sync_copy(vb, vo_hbm.at[rows])

    sd = jax.ShapeDtypeStruct((r, cc), F32)
    return pl.kernel(
        body, name=name, out_type=(sd, sd, sd, sd),
        mesh=plsc.VectorSubcoreMesh(core_axis_name="sc_core", subcore_axis_name="sc_tile"),
        scratch_types=[pltpu.VMEM((rows_per, cc), F32)] * 5,
    )(w, g, m, v)


def _adamw_small(ws, gs, ms, vs):
    n = len(ws)

    def body(*refs):
        for a in range(n):
            w_ref, g_ref, m_ref, v_ref = refs[4 * a:4 * a + 4]
            d_ref, mo_ref, vo_ref = refs[4 * n + 3 * a:4 * n + 3 * a + 3]
            d, mm, vv = _adamw_math(w_ref[...], g_ref[...], m_ref[...], v_ref[...])
            d_ref[...] = d
            mo_ref[...] = mm
            vo_ref[...] = vv

    args, outs = [], []
    for a in range(n):
        args += [ws[a], gs[a], ms[a], vs[a]]
        outs += [jax.ShapeDtypeStruct(ws[a].shape, F32)] * 3
    res = pl.pallas_call(body, name="adamw_small", out_shape=tuple(outs), compiler_params=_cp())(*args)
    return [res[3 * a:3 * a + 3] for a in range(n)]


def _flat_pack(arrs, rows):
    flat = jnp.concatenate([a.reshape(-1) for a in arrs])
    return jnp.pad(flat, (0, rows * D - flat.shape[0])).reshape(rows, D)


def _flat_unpack(flat, shapes):
    out, off = [], 0
    for shp in shapes:
        size = 1
        for d_ in shp:
            size *= d_
        out.append(flat[off:off + size].reshape(shp))
        off += size
    return out


SMALL_EVEN = ("even_pre_g", "even_a_ln_g", "even_a_ln_b", "even_a_ws", "even_a_bs", "even_b_conv", "even_mem_g",
              "even_post_g")
SMALL_ODD = ("odd_pre_g", "odd_c_wgrp", "odd_c_scale", "odd_d_dw_w", "odd_d_dw_b", "odd_d_ln_g", "odd_d_ln_b",
             "odd_d_pw_b", "odd_mem_g", "odd_post_g")
BIG = ("even_w_in", "even_w_kv", "even_w_out", "odd_w_in", "odd_d_pw_w", "odd_w_kv", "odd_w_out")
WEIGHTS = ("even_pre_g", "even_w_in", "even_a_ln_g", "even_a_ln_b", "even_a_ws", "even_a_bs", "even_b_conv",
           "even_mem_g", "even_w_kv", "even_w_out", "even_post_g", "odd_pre_g", "odd_w_in", "odd_c_wgrp",
           "odd_c_scale", "odd_d_dw_w", "odd_d_dw_b", "odd_d_ln_g", "odd_d_ln_b", "odd_d_pw_w", "odd_d_pw_b",
           "odd_mem_g", "odd_w_kv", "odd_w_out", "odd_post_g")
PACKED = (("even_b_conv", (3, 192)), ("odd_pre_g", (1, 256)), ("odd_c_scale", (1, 192)), ("odd_d_dw_w", (31, 192)),
          ("odd_d_dw_b", (1, 192)), ("odd_d_ln_g", (1, 192)), ("odd_d_ln_b", (1, 192)), ("odd_d_pw_b", (1, 192)),
          ("odd_mem_g", (1, 256)), ("odd_post_g", (1, 256)))
PACK_ROWS = 16
SMALL_ROWS = 256


def _four(g):
    return g.reshape(N_CHIPS, 2, g.shape[0] // (2 * N_CHIPS), g.shape[1])


def _step(x, mem, target, w, mom, var, place):
    wt = {}
    pack = _flat_pack([w[n][0] for n, _ in PACKED], PACK_ROWS)
    shards = {"even_w_in_t": w["even_w_in"][0].T, "odd_w_in_t": w["odd_w_in"][0].T, "even_w_kv": w["even_w_kv"][0],
              "odd_w_kv": w["odd_w_kv"][0], "even_w_out": w["even_w_out"][0], "odd_w_out": w["odd_w_out"][0],
              "odd_d_pw_w": w["odd_d_pw_w"][0]}
    placed = {n: _place_shard(shards[n], place, BF16, "place_" + n) for n in ("even_w_in_t", "even_w_kv", "even_w_out")}
    placed["pack"] = _place_shard(pack, place, F32, "place_pack")

    order, group = _stream_tables(place[0], place[1], EVEN_IN)
    p_e, h_e, (wt["even_w_in_t"], packs), (wt["even_w_kv"], wt["even_w_out"]) = _in_fwd_streamed(
        x, w["even_pre_g"], [placed["even_w_in_t"], placed["pack"]], [placed["even_w_kv"], placed["even_w_out"]],
        order, group, "even_in_streamed")
    for n in ("odd_w_in_t", "odd_w_kv", "odd_w_out", "odd_d_pw_w"):
        placed[n] = _place_shard(shards[n], place, BF16, "place_" + n, after=p_e[0:16, 0:128])
    packs = packs.reshape(N_CHIPS, PACK_ROWS * D)
    per_chip = [_flat_unpack(packs[k], [shp for _, shp in PACKED]) for k in range(N_CHIPS)]
    for a, (name, _) in enumerate(PACKED):
        wt[name] = jnp.concatenate([per_chip[k][a] for k in range(N_CHIPS)], axis=-1)
    for name in ("even_pre_g", "even_a_ln_g", "even_a_ln_b", "even_mem_g", "even_post_g"):
        wt[name] = w[name]

    tril = jnp.tril(jnp.ones((CH, CH), dtype=bool))
    wcat = jnp.where(tril[None], w["even_a_ws"][0], 0.0).transpose(1, 0, 2).reshape(CH, 4 * CH).astype(BF16)
    bsg = jnp.repeat(w["even_a_bs"][0].T, BW // 4, axis=1)
    hsel = (jnp.arange(BW)[:, None] // (BW // 4) == jnp.arange(128)[None, :]).astype(BF16)
    g4 = BW // 4
    eye = jnp.eye(4, dtype=F32)
    wbd = (w["odd_c_wgrp"][0][:, :, None, :] * eye[:, None, :, None]).reshape(BW, BW).astype(BF16)

    kv_e = _kv_fwd(mem, wt["even_mem_g"], wt["even_w_kv"], "even_kv")
    (x1, o_e, y_e), (wt["odd_w_in_t"],) = _even_fwd(
        x, p_e, kv_e, wt["even_a_ln_g"], wt["even_a_ln_b"], wcat, bsg, wt["even_b_conv"], wt["even_w_out"],
        wt["even_post_g"], rider=_GatherRider([placed["odd_w_in_t"]]))
    names = ("odd_w_out", "odd_d_pw_w", "odd_w_kv")
    (p_o, h_o), got = _in_fwd(x1, wt["odd_pre_g"], wt["odd_w_in_t"], "odd_in",
                              rider=_GatherRider([placed[n] for n in names]))
    wt.update(zip(names, got))
    kv_o = _kv_fwd(mem, wt["odd_mem_g"], wt["odd_w_kv"], "odd_kv")
    dx2, o_o, cv_o, loss = _odd_fwd(x1, p_o, kv_o, wbd, wt["odd_c_scale"], wt["odd_d_dw_w"], wt["odd_d_dw_b"],
                                    wt["odd_d_ln_g"], wt["odd_d_ln_b"], wt["odd_d_pw_w"], wt["odd_d_pw_b"],
                                    wt["odd_w_out"], wt["odd_post_g"], target)
    (dpc_o, tmpc, tmpd, do_o, y_o, g_post_o, g_cs, g_wbd, g_dww, g_dwb, g_lng_o, g_lnb_o, g_pww, g_pwb,
     dkv_o) = _odd_bwd1(dx2, o_o, cv_o, p_o, kv_o, wbd, wt["odd_c_scale"], wt["odd_d_dw_w"], wt["odd_d_dw_b"],
                        wt["odd_d_ln_g"], wt["odd_d_ln_b"], wt["odd_d_pw_w"], wt["odd_d_pw_b"], wt["odd_w_out"],
                        wt["odd_post_g"])
    dpb_o, dx1, g_pre_o = _odd_bwd2(dpc_o, tmpc, tmpd, p_o, wt["odd_d_dw_w"], wt["odd_w_in_t"], x1,
                                    wt["odd_pre_g"], dx2)
    g_win_o = _grad_tn(dpb_o, h_o, 768, rows=ODD_IN, name="odd_gw_in_b")
    g_win_o = _grad_tn(dpc_o, h_o, 1280, out=g_win_o, rows=ODD_IN, row0=3 * BW, name="odd_gw_in_c")
    g_wout_o = _grad_tn(y_o, do_o, 1024, name="odd_gw_out")
    g_wkv_o, g_memg_o = _kv_bwd(mem, wt["odd_mem_g"], wt["odd_w_kv"], dkv_o, "odd_kv_bwd")
    big_o = [_four(g) for g in (g_win_o, g_pww.astype(BF16), g_wkv_o, g_wout_o)]
    recv_o, _ = _swap_halves(big_o, "swap_halves_odd")
    sums_o = [_pair_sum(big_o[a], recv_o[a], place, "pair_sum_odd_%d" % a) for a in range(len(big_o))]
    (dp_e, do_e, g_post_e, g_lng_e, g_lnb_e, g_wcat, g_bs, g_bconv,
     dkv_e), parts_o = _even_bwd1(dx1, o_e, p_e, kv_e, wt["even_a_ln_g"], wt["even_a_ln_b"], wcat, bsg, hsel,
                                  wt["even_b_conv"], wt["even_w_out"], wt["even_post_g"],
                                  rider=_ExchangeRider(sums_o))
    halves_o = [_chip_sum(sums_o[a], parts_o[a], place, "chip_sum_odd_%d" % a) for a in range(len(big_o))]
    g_wout_e = _grad_tn(y_e, do_e, 1024, name="even_gw_out")
    g_wkv_e, g_memg_e = _kv_bwd(mem, wt["even_mem_g"], wt["even_w_kv"], dkv_e, "even_kv_bwd")
    big_x = [_four(g) for g in (g_wkv_e, g_wout_e)]
    recv_x, full_o = _swap_halves(big_x, "swap_halves_kv_out", share=halves_o)
    upd_odd = {}
    for a, n in enumerate(("odd_w_in", "odd_d_pw_w", "odd_w_kv", "odd_w_out")):
        g = full_o[a].reshape(full_o[a].shape[1] * 2, full_o[a].shape[2])
        ops = [w[n][0], mom[n][0], var[n][0]]
        if n == "odd_w_in":
            ops = [o_.T for o_ in ops]
        res = _adamw_sparsecore(ops[0], g, ops[1], ops[2], "adamw_sc_" + n)
        upd_odd[n] = tuple(r_.T for r_ in res) if n == "odd_w_in" else tuple(res)
    sums_x = [_pair_sum(big_x[a], recv_x[a], place, "pair_sum_kv_out_%d" % a) for a in range(len(big_x))]
    g_win_e, parts_x = _grad_tn(dp_e, h_e, 1280, name="even_gw_in", rider=_ExchangeRider(sums_x))
    halves_x = [_chip_sum(sums_x[a], parts_x[a], place, "chip_sum_kv_out_%d" % a) for a in range(len(big_x))]
    big_e = [_four(g_win_e)]
    recv_e, _ = _swap_halves(big_e, "swap_halves_even")
    sums_e = [_pair_sum(big_e[0], recv_e[0], place, "pair_sum_even_w_in")]
    (dx0, g_pre_e), parts_e = _even_bwd2(dp_e, wt["even_w_in_t"], x, wt["even_pre_g"], dx1,
                                         rider=_ExchangeRider(sums_e))
    halves_e = [_chip_sum(sums_e[0], parts_e[0], place, "chip_sum_even_w_in")]

    g_aws = jnp.where(tril[None], g_wcat.reshape(CH, 4, CH).transpose(1, 0, 2), 0.0)
    g_wgrp = jnp.stack([lax.dynamic_slice(g_wbd, (g * g4, g * g4), (g4, g4)) for g in range(4)])
    small = {
        "even_pre_g": g_pre_e, "even_a_ln_g": g_lng_e, "even_a_ln_b": g_lnb_e, "even_a_ws": g_aws,
        "even_a_bs": g_bs[:, 0:4].T, "even_b_conv": g_bconv[0:3], "even_mem_g": g_memg_e, "even_post_g": g_post_e,
        "odd_pre_g": g_pre_o, "odd_c_wgrp": g_wgrp, "odd_c_scale": g_cs, "odd_d_dw_w": g_dww.reshape(CONF, 8, BW).sum(axis=1),
        "odd_d_dw_b": g_dwb, "odd_d_ln_g": g_lng_o, "odd_d_ln_b": g_lnb_o, "odd_d_pw_b": g_pwb,
        "odd_mem_g": g_memg_o, "odd_post_g": g_post_o,
    }
    small_names = SMALL_EVEN + SMALL_ODD
    small_pack = _flat_pack([small[n] for n in small_names] + [loss[0, 0].reshape(1)], SMALL_ROWS)
    small_total, full = _finish_reduce(small_pack, halves_e + halves_x)
    order = ("even_w_in", "even_w_kv", "even_w_out")
    gbig = {n: full[a].reshape(full[a].shape[1] * 2, full[a].shape[2]) for a, n in enumerate(order)}
    return dx0, gbig, upd_odd, small_total.reshape(-1), [small[n].shape for n in small_names]


def kernel(x, mem, even_pre_g, even_w_in, even_a_ln_g, even_a_ln_b, even_a_ws, even_a_bs, even_b_conv, even_mem_g, even_w_kv, even_w_out, even_post_g, odd_pre_g, odd_w_in, odd_c_wgrp, odd_c_scale, odd_d_dw_w, odd_d_dw_b, odd_d_ln_g, odd_d_ln_b, odd_d_pw_w, odd_d_pw_b, odd_mem_g, odd_w_kv, odd_w_out, odd_post_g, loss_target, m_even_pre_g, m_even_w_in, m_even_a_ln_g, m_even_a_ln_b, m_even_a_ws, m_even_a_bs, m_even_b_conv, m_even_mem_g, m_even_w_kv, m_even_w_out, m_even_post_g, m_odd_pre_g, m_odd_w_in, m_odd_c_wgrp, m_odd_c_scale, m_odd_d_dw_w, m_odd_d_dw_b, m_odd_d_ln_g, m_odd_d_ln_b, m_odd_d_pw_w, m_odd_d_pw_b, m_odd_mem_g, m_odd_w_kv, m_odd_w_out, m_odd_post_g, v_even_pre_g, v_even_w_in, v_even_a_ln_g, v_even_a_ln_b, v_even_a_ws, v_even_a_bs, v_even_b_conv, v_even_mem_g, v_even_w_kv, v_even_w_out, v_even_post_g, v_odd_pre_g, v_odd_w_in, v_odd_c_wgrp, v_odd_c_scale, v_odd_d_dw_w, v_odd_d_dw_b, v_odd_d_ln_g, v_odd_d_ln_b, v_odd_d_pw_w, v_odd_d_pw_b, v_odd_mem_g, v_odd_w_kv, v_odd_w_out, v_odd_post_g):
    given = dict(locals())
    w = {n: given[n] for n in WEIGHTS}
    mom = {n: given["m_" + n] for n in WEIGHTS}
    var = {n: given["v_" + n] for n in WEIGHTS}

    x_, y_, c_ = lax.axis_index("x"), lax.axis_index("y"), lax.axis_index("c")
    chip = 2 * x_ + y_
    place = jnp.stack([c_, chip]).astype(jnp.int32)
    grad_x, gbig, upd_odd, gsmall_flat, small_shapes = _step(x[0], mem[0], loss_target[0], w, mom, var, place)

    names = SMALL_EVEN + SMALL_ODD
    grads = {}
    unpacked = _flat_unpack(gsmall_flat, small_shapes + [(1,)])
    loss = unpacked[-1][0]
    for n, g in zip(names, unpacked[:-1]):
        shard_shape = w[n].shape[1:]
        if g.shape[-1] != shard_shape[-1]:
            g = lax.dynamic_slice_in_dim(g, chip * shard_shape[-1], shard_shape[-1], axis=g.ndim - 1)
        grads[n] = g.reshape(shard_shape)

    def two_d(a):
        return a.reshape(-1, a.shape[-1])

    upd = {}
    for n in BIG:
        if n in upd_odd:
            res = upd_odd[n]
        elif n.endswith("w_in"):
            res = _adamw_big(w[n][0].T, gbig[n], mom[n][0].T, var[n][0].T, "adamw_" + n)
            res = tuple(r.T for r in res)
        else:
            res = _adamw_big(w[n][0], gbig[n], mom[n][0], var[n][0], "adamw_" + n)
        grads[n], upd[n] = res[0], res[1:]
    res = _adamw_small([two_d(w[n][0]) for n in names], [two_d(grads[n]) for n in names],
                       [two_d(mom[n][0]) for n in names], [two_d(var[n][0]) for n in names])
    for n, r in zip(names, res):
        upd[n] = r

    outs = [loss, grad_x[None]]
    outs += [grads[n].reshape(w[n].shape) for n in WEIGHTS]
    for j in range(3):
        outs += [upd[n][j].reshape(w[n].shape) for n in WEIGHTS]
    return tuple(outs)
```

```python
import jax
import jax.numpy as jnp
from jax import lax
from jax.experimental import pallas as pl
from jax.experimental.pallas import tpu as pltpu
from jax.experimental.pallas import tpu_sc as plsc

F32 = jnp.float32
BF16 = jnp.bfloat16
MESH = pl.DeviceIdType.MESH

D = 1024
N_MEM = 256
MIX = 2048
XA = 512
HD = 128
BW = 768
CH = 128
EPS = 1e-6
SCALE = HD ** -0.5
POOL_WINDOWS = (2, 4, 8, 16)
CONF = 31
EVEN_IN = 6400
ODD_IN = 4864
N_CHIPS = 4

ADAM_LR = 0.001
ADAM_B1 = 0.9
ADAM_B2 = 0.999
ADAM_EPS = 1e-08
ADAM_WD = 0.01
ADAM_STEP = 10

TS = 256
HALO = 32
VMEM_LIMIT = 56 * 1024 * 1024


def _cp(sem=None):
    return pltpu.CompilerParams(dimension_semantics=sem, vmem_limit_bytes=VMEM_LIMIT)


def _dot(a, b):
    return jnp.dot(a, b, preferred_element_type=F32)


def _dot_nt(a, b):
    return lax.dot_general(a, b, (((1,), (1,)), ((), ())), preferred_element_type=F32)


def _dot_tn(a, b):
    return lax.dot_general(a, b, (((0,), (0,)), ((), ())), preferred_element_type=F32)


def _sigmoid(x):
    return 1.0 / (1.0 + jnp.exp(-x))


def _resident(shape):
    return pl.BlockSpec(shape, lambda *_: (0,) * len(shape), pipeline_mode=pl.Buffered(1))


def _const(shape):
    return pl.BlockSpec(shape, lambda *_: (0,) * len(shape))


def _kv_fwd(mem, mem_g, wkv, name):
    def body(mem_ref, g_ref, w_ref, kv_ref):
        m = mem_ref[...]
        r = lax.rsqrt(jnp.mean(m * m, axis=-1, keepdims=True) + EPS)
        mn = (m * r * g_ref[...]).astype(BF16)
        kv_ref[...] = _dot(mn, w_ref[...]).astype(BF16)

    return pl.pallas_call(body, out_shape=jax.ShapeDtypeStruct((N_MEM, D), BF16), name=name,
                          compiler_params=_cp())(mem, mem_g, wkv)


def _kv_bwd(mem, mem_g, wkv, dkv, name):
    def body(mem_ref, g_ref, w_ref, dkv_ref, dw_ref, dg_ref):
        m = mem_ref[...]
        r = lax.rsqrt(jnp.mean(m * m, axis=-1, keepdims=True) + EPS)
        mh = m * r
        mn = (mh * g_ref[...]).astype(BF16)
        dkv = dkv_ref[...].astype(BF16)
        dw_ref[...] = _dot_tn(mn, dkv).astype(BF16)
        dmn = _dot_nt(dkv, w_ref[...])
        dg_ref[...] = jnp.sum(dmn * mh, axis=0, keepdims=True)

    return pl.pallas_call(body, out_shape=(jax.ShapeDtypeStruct((D, D), BF16), jax.ShapeDtypeStruct((1, D), F32)),
                          name=name, compiler_params=_cp())(mem, mem_g, wkv, dkv)


def _host_call(body, *, grid, name, out_shape, in_specs, out_specs, args, scratch_shapes=(), aliases=None,
               rider=None):
    sem = ("arbitrary",) * len(grid)
    aliases = dict(aliases or {})
    if rider is None:
        res = pl.pallas_call(body, grid=grid, name=name, out_shape=tuple(out_shape), in_specs=list(in_specs),
                             out_specs=tuple(out_specs), scratch_shapes=list(scratch_shapes),
                             input_output_aliases=aliases, compiler_params=_cp(sem))(*args)
        return tuple(res), ()
    n_in, n_out, n_sc = len(in_specs), len(out_specs), len(scratch_shapes)
    r_in, r_out = len(rider.inputs), len(rider.out_shapes)

    def full_body(*refs):
        host_in = refs[:n_in]
        rid_in = refs[n_in:n_in + r_in]
        host_out = refs[n_in + r_in:n_in + r_in + n_out]
        rid_out = refs[n_in + r_in + n_out:n_in + r_in + n_out + r_out]
        host_sc = refs[n_in + r_in + n_out + r_out:n_in + r_in + n_out + r_out + n_sc]
        sems = refs[n_in + r_in + n_out + r_out + n_sc:]
        first = pl.program_id(0) == 0
        last = pl.program_id(0) == grid[0] - 1
        for ax in range(1, len(grid)):
            first = jnp.logical_and(first, pl.program_id(ax) == 0)
            last = jnp.logical_and(last, pl.program_id(ax) == grid[ax] - 1)

        @pl.when(first)
        def _():
            rider.start(rid_in, rid_out, sems)

        if rider.has_mid:
            @pl.when(last)
            def _():
                rider.mid(rid_in, rid_out, sems)

        body(*host_in, *host_out, *host_sc)

        @pl.when(last)
        def _():
            rider.end(rid_in, rid_out, sems)

    aliases.update({n_in + j: n_out + k for j, k in rider.aliases.items()})
    res = pl.pallas_call(
        full_body, grid=grid, name=name, out_shape=tuple(out_shape) + tuple(rider.out_shapes),
        in_specs=list(in_specs) + _hbm_specs(r_in), out_specs=tuple(out_specs) + tuple(_hbm_specs(r_out)),
        scratch_shapes=list(scratch_shapes) + list(rider.sems), input_output_aliases=aliases,
        compiler_params=_cp(sem),
    )(*args, *rider.inputs)
    return tuple(res[:n_out]), tuple(res[n_out:])


def _in_fwd(x, pre_g, w_t, name, rider=None):
    s, n = x.shape[0], w_t.shape[0]
    tm = min(512, s)
    nc = 256

    def body(x_ref, g_ref, w_ref, p_ref, h_ref):
        xv = x_ref[...]
        r = lax.rsqrt(jnp.mean(xv * xv, axis=-1, keepdims=True) + EPS)
        h = (xv * r * g_ref[...]).astype(BF16)
        h_ref[...] = h
        for j in range(n // nc):
            p_ref[:, j * nc:(j + 1) * nc] = _dot_nt(h, w_ref[j * nc:(j + 1) * nc, :]).astype(BF16)

    return _host_call(
        body, grid=(s // tm,), name=name, rider=rider,
        out_shape=(jax.ShapeDtypeStruct((s, n), BF16), jax.ShapeDtypeStruct((s, D), BF16)),
        in_specs=[pl.BlockSpec((tm, D), lambda i: (i, 0)), _const((1, D)), _resident((n, D))],
        out_specs=(pl.BlockSpec((tm, n), lambda i: (i, 0)), pl.BlockSpec((tm, D), lambda i: (i, 0))),
        args=(x, pre_g, w_t))


NC = 256


def _stream_tables(core, chip, n):
    nchunk = n // NC
    idx = jnp.arange(nchunk, dtype=jnp.int32)
    src = jnp.array([0, 2, 1, 3], jnp.int32)
    r = n // N_CHIPS

    def group_of(row):
        j = src[(row // r) ^ chip]
        through_sibling = ((row % r) // (r // 2) != core).astype(jnp.int32)
        return jnp.where(j == 0, 0, 2 * j - 1 + through_sibling)

    grp = jnp.maximum(group_of(idx * NC), group_of(idx * NC + NC - 1))
    order = jnp.argsort(grp * 64 + idx).astype(jnp.int32)
    return order, grp[order]


def _in_fwd_streamed(x, pre_g, first, later, order, group, name):
    s, n = x.shape[0], first[0].shape[0]
    nchunk = n // NC
    rider = _GatherRider(first)
    rider2 = _GatherRider(later) if later else None
    a, m = len(first), len(later)
    tr = min(256, s)

    def body(*refs):
        order_ref, group_ref, x_ref, g_ref = refs[0:4]
        p_ref, h_ref = refs[4 + a + m:6 + a + m]
        outs = refs[6 + a + m:6 + 2 * a + m]
        outs2 = refs[6 + 2 * a + m:6 + 2 * a + 2 * m]
        wbuf, wsem, send_sems, recv_sems = refs[6 + 2 * a + 2 * m:10 + 2 * a + 2 * m]
        sems2 = refs[10 + 2 * a + 2 * m:]
        w_hbm = outs[0]
        j = pl.program_id(0)
        sems = (send_sems, recv_sems)
        grp = group_ref[j]
        new_group = jnp.logical_or(j == 0, group_ref[jnp.maximum(j - 1, 0)] != grp)
        slot = j % 2

        def fetch(step, sl):
            rows = pl.ds(pl.multiple_of(order_ref[step] * NC, NC), NC)
            return pltpu.make_async_copy(w_hbm.at[rows], wbuf.at[sl], wsem.at[sl])

        @pl.when(j == 0)
        def _():
            rider.start(None, outs, sems, peers=(0, 1))

            @pl.loop(0, s // tr)
            def _(t):
                rows = pl.ds(pl.multiple_of(t * tr, tr), tr)
                xv = x_ref[rows, :]
                r = lax.rsqrt(jnp.mean(xv * xv, axis=-1, keepdims=True) + EPS)
                h_ref[rows, :] = (xv * r * g_ref[...]).astype(BF16)

        before = jnp.where(j == 0, 0, group_ref[jnp.maximum(j - 1, 0)])

        def entering(b):
            return jnp.logical_and(before < b, b <= grp)

        for src in range(3):
            @pl.when(entering(2 * src + 1))
            def _(src=src):
                if src == 0:
                    rider.start(None, outs, sems, peers=(2,))
                rider.mid(None, outs, sems, peers=(src,))
                if src == 1 and rider2 is not None:
                    rider2.start(None, outs2, sems2)

            @pl.when(entering(2 * src + 2))
            def _(src=src):
                rider.wait_forwarded(outs, sems, peers=(src,))

        @pl.when(new_group)
        def _():
            fetch(j, slot).start()

        fetch(j, slot).wait()
        nxt = jnp.minimum(j + 1, nchunk - 1)

        @pl.when(jnp.logical_and(j + 1 < nchunk, group_ref[nxt] == grp))
        def _():
            fetch(nxt, 1 - slot).start()

        p_ref[...] = _dot_nt(h_ref[...], wbuf[slot]).astype(BF16)

        @pl.when(j == nchunk - 1)
        def _():
            rider.wait_sends(outs, sems)
            if rider2 is not None:
                rider2.mid(None, outs2, sems2)
                rider2.end(None, outs2, sems2)

    hbm = pl.BlockSpec(memory_space=pltpu.HBM)
    arrs = list(first) + list(later)
    whole = pl.BlockSpec((s, D), lambda j, o, g: (0, 0), pipeline_mode=pl.Buffered(1))
    res = pl.pallas_call(
        body, name=name,
        out_shape=(jax.ShapeDtypeStruct((s, n), BF16), jax.ShapeDtypeStruct((s, D), BF16))
        + tuple(jax.ShapeDtypeStruct(v.shape, v.dtype) for v in arrs),
        grid_spec=pltpu.PrefetchScalarGridSpec(
            num_scalar_prefetch=2, grid=(nchunk,),
            in_specs=[whole, pl.BlockSpec((1, D), lambda j, o, g: (0, 0))] + [hbm] * (a + m),
            out_specs=(pl.BlockSpec((s, NC), lambda j, o, g: (0, o[j])),
                       pl.BlockSpec((s, D), lambda j, o, g: (0, 0))) + (hbm,) * (a + m),
            scratch_shapes=[pltpu.VMEM((2, NC, D), BF16), pltpu.SemaphoreType.DMA((2,))] + list(rider.sems)
            + (list(rider2.sems) if rider2 is not None else [])),
        input_output_aliases={4 + v: 2 + v for v in range(a + m)},
        compiler_params=_cp(("arbitrary",)),
    )(order, group, x, pre_g, *arrs)
    return res[0], res[1], tuple(res[2:2 + a]), tuple(res[2 + a:])


def _xattn_fwd(q, kv_ref):
    outs, probs = [], []
    for h in range(XA // HD):
        qh = q[:, h * HD:(h + 1) * HD]
        kh = kv_ref[:, h * HD:(h + 1) * HD]
        vh = kv_ref[:, XA + h * HD:XA + (h + 1) * HD]
        sc = _dot_nt(qh, kh) * SCALE
        e = jnp.exp(sc - jnp.max(sc, axis=-1, keepdims=True))
        pr = e / jnp.sum(e, axis=-1, keepdims=True)
        outs.append(_dot(pr.astype(BF16), vh))
        probs.append(pr)
    return jnp.concatenate(outs, axis=-1), probs


def _xattn_bwd(dyx, q, probs, kv_ref, dkv_ref):
    dqs = []
    for h in range(XA // HD):
        qh = q[:, h * HD:(h + 1) * HD]
        kh = kv_ref[:, h * HD:(h + 1) * HD]
        vh = kv_ref[:, XA + h * HD:XA + (h + 1) * HD]
        dy = dyx[:, h * HD:(h + 1) * HD].astype(BF16)
        pr = probs[h]
        dp = _dot_nt(dy, vh)
        ds = (pr * (dp - jnp.sum(dp * pr, axis=-1, keepdims=True))).astype(BF16)
        dqs.append(_dot(ds, kh) * SCALE)
        dkv_ref[:, h * HD:(h + 1) * HD] += _dot_tn(ds, qh) * SCALE
        dkv_ref[:, XA + h * HD:XA + (h + 1) * HD] += _dot_tn(pr.astype(BF16), dy)
    return jnp.concatenate(dqs, axis=-1)


def _layer_norm_fwd(v, g, b):
    mu = jnp.mean(v, axis=-1, keepdims=True)
    vc = v - mu
    rstd = lax.rsqrt(jnp.mean(vc * vc, axis=-1, keepdims=True) + EPS)
    vhat = vc * rstd
    return vhat * g + b, vhat, rstd


def _layer_norm_bwd(dy, vhat, rstd, g):
    dvh = dy * g
    return rstd * (dvh - jnp.mean(dvh, axis=-1, keepdims=True) - vhat * jnp.mean(dvh * vhat, axis=-1, keepdims=True))


def _head_masks():
    col = lax.broadcasted_iota(jnp.int32, (1, BW), 1)
    return [(col >= h * (BW // 4)) & (col < (h + 1) * (BW // 4)) for h in range(4)]


def _halo_prev(nblk_per_tile):
    return lambda i: (jnp.maximum(i * nblk_per_tile - 1, 0), 0)


def _row_ids(i, t):
    return i * t + lax.broadcasted_iota(jnp.int32, (t, 1), 0)


def _even_mix(i, p_ref, ph_ref, ln_g, ln_b, wcat_ref, bsg_ref, bconv_ref, wbuf):
    t = p_ref.shape[0]
    u = p_ref[:, 0:BW].astype(F32)
    v = p_ref[:, BW:2 * BW].astype(F32)
    bg = p_ref[:, 2 * BW:3 * BW].astype(F32)
    cg = p_ref[:, 3 * BW:4 * BW].astype(F32)
    xin = p_ref[:, 4 * BW:5 * BW].astype(F32)
    vn, vhat, rstd = _layer_norm_fwd(v, ln_g, ln_b)
    masks = _head_masks()
    sgs, vsts = [], []
    for n in range(t // CH):
        vn_c = vn[n * CH:(n + 1) * CH]
        vst = jnp.concatenate([jnp.where(m, vn_c, 0.0) for m in masks], axis=0).astype(BF16)
        sgs.append(_dot(wcat_ref[...], vst) + bsg_ref[...])
        vsts.append(vst)
    sg = jnp.concatenate(sgs, axis=0)
    ya = u * sg
    w_halo = ph_ref[:, 3 * BW:4 * BW].astype(F32) * ph_ref[:, 4 * BW:5 * BW].astype(F32)
    wbuf[0:HALO, :] = jnp.where(i > 0, w_halo, 0.0)
    wbuf[HALO:HALO + t, :] = cg * xin
    conv = (bconv_ref[0:1, :] * wbuf[pl.ds(HALO - 2, t), :] + bconv_ref[1:2, :] * wbuf[pl.ds(HALO - 1, t), :]
            + bconv_ref[2:3, :] * wbuf[pl.ds(HALO, t), :])
    yb = bg * conv
    return dict(u=u, bg=bg, cg=cg, xin=xin, vhat=vhat, rstd=rstd, sg=sg, vsts=vsts, conv=conv, ya=ya, yb=yb,
                masks=masks)


def _pool_select(vals):
    col = lax.broadcasted_iota(jnp.int32, (1, BW), 1)
    g = BW // 4
    return jnp.where(col < g, vals[0], jnp.where(col < 2 * g, vals[1], jnp.where(col < 3 * g, vals[2], vals[3])))


def _inv_counts(i, t):
    rows = _row_ids(i, t) + 1
    return [1.0 / jnp.minimum(rows, w).astype(F32) for w in POOL_WINDOWS]


def _band_matrices(t, forward):
    j = jnp.arange(t)[:, None]
    r = jnp.arange(HALO + t)[None, :]
    if forward:
        return jnp.stack([(r >= j) & (r < j + w) for w in POOL_WINDOWS]).astype(BF16)
    return jnp.stack([(r <= HALO + j) & (r > HALO + j - w) for w in POOL_WINDOWS]).astype(BF16)


SHIFT_ROWS = HALO + TS - 8


def _shifted_copies(buf, sh):
    for b in range(1, 8):
        sh[b - 1] = buf[pl.ds(b, SHIFT_ROWS), :]


def _rows_at(buf, sh, off, t):
    a, b = divmod(off, 8)
    return buf[pl.ds(8 * a, t), :] if b == 0 else sh[b - 1, pl.ds(8 * a, t), :]


def _tap_sums(d_ref, buf, sh, base, out_ref):
    t = d_ref.shape[0]
    group = 4
    for k0 in range(0, CONF, group):
        taps = list(range(k0, min(k0 + group, CONF)))

        def step(r, accs, taps=taps):
            row = pl.multiple_of(r * 8, 8)
            d = d_ref[pl.ds(row, 8), :]
            new = []
            for acc, k in zip(accs, taps):
                a, b = divmod(base + k, 8)
                src = buf[pl.ds(row + 8 * a, 8), :] if b == 0 else sh[b - 1, pl.ds(row + 8 * a, 8), :]
                new.append(acc + d * src)
            return tuple(new)

        accs = lax.fori_loop(0, t // 8, step, tuple(jnp.zeros((8, BW), F32) for _ in taps), unroll=2)
        for acc, k in zip(accs, taps):
            out_ref[8 * k:8 * k + 8, :] += acc


def _odd_mix(i, p_ref, ph_ref, bands_ref, wbd_ref, cscale, dww_ref, dwb, ln_g, ln_b, pww_ref, pwb, gbuf, gsh,
             cv=None):
    t = p_ref.shape[0]
    zc_bf = p_ref[:, 0:BW]
    zc = zc_bf.astype(F32)
    ga = p_ref[:, BW:2 * BW].astype(F32)
    gb = p_ref[:, 2 * BW:3 * BW].astype(F32)
    zh = ph_ref[:, 0:BW]
    zcat = jnp.concatenate([jnp.where(i > 0, zh, jnp.zeros_like(zh)), zc_bf], axis=0)
    inv = _inv_counts(i, t)
    pooled = _pool_select([_dot(bands_ref[w], zcat) * inv[w] for w in range(len(POOL_WINDOWS))]) - zc
    pooled_bf = pooled.astype(BF16)
    pre = _dot(pooled_bf, wbd_ref[...])
    yc = pre * cscale
    sgb = _sigmoid(gb)
    z = ga * sgb
    gh_a = ph_ref[:, BW:2 * BW].astype(F32)
    gh_b = ph_ref[:, 2 * BW:3 * BW].astype(F32)
    gbuf[0:HALO, :] = jnp.where(i > 0, gh_a * _sigmoid(gh_b), 0.0)
    gbuf[HALO:HALO + t, :] = z
    _shifted_copies(gbuf, gsh)
    if cv is None:
        cv = dwb + dww_ref[CONF - 1:CONF, :] * z
        for k in range(CONF - 1):
            cv = cv + dww_ref[k:k + 1, :] * _rows_at(gbuf, gsh, HALO - (CONF - 1) + k, t)
    zl, zhat, rstd = _layer_norm_fwd(cv, ln_g, ln_b)
    szl = _sigmoid(zl)
    zs = (zl * szl).astype(BF16)
    yd = _dot(zs, pww_ref[...]) + pwb
    return dict(ga=ga, sgb=sgb, pooled_bf=pooled_bf, pre=pre, yc=yc, zhat=zhat, rstd=rstd, zl=zl, szl=szl,
                zs=zs, yd=yd, inv=inv, cv=cv)


def _post_norm(o, post_g):
    r = lax.rsqrt(jnp.mean(o * o, axis=-1, keepdims=True) + EPS)
    return o * r, r


def _gate_out(y_a, y_b, y_x, gate, wout_ref):
    sgt = _sigmoid(gate)
    sgate = gate * sgt
    ys = [(y_a * sgate[:, 0:BW]).astype(BF16), (y_b * sgate[:, BW:2 * BW]).astype(BF16),
          (y_x * sgate[:, 2 * BW:MIX]).astype(BF16)]
    o = (_dot(ys[0], wout_ref[0:BW, :]) + _dot(ys[1], wout_ref[BW:2 * BW, :]) + _dot(ys[2], wout_ref[2 * BW:MIX, :]))
    return o, ys, sgt, sgate


def _tile_specs(s, n):
    nh = TS // HALO
    return pl.BlockSpec((TS, n), lambda i: (i, 0)), pl.BlockSpec((HALO, n), _halo_prev(nh))


def _even_fwd(x, p, kv, ln_g, ln_b, wcat, bsg, bconv, wout, post_g, rider=None):
    s = x.shape[0]

    def body(x_ref, p_ref, ph_ref, kv_ref, lng, lnb, wcat_ref, bsg_ref, bconv_ref, wout_ref, pg, x1_ref, o_ref,
             y_ref, wbuf):
        i = pl.program_id(0)
        mx = _even_mix(i, p_ref, ph_ref, lng[...], lnb[...], wcat_ref, bsg_ref, bconv_ref, wbuf)
        yx, _ = _xattn_fwd(p_ref[:, 5 * BW:5 * BW + XA], kv_ref)
        gate = p_ref[:, 5 * BW + XA:EVEN_IN].astype(F32)
        o, ys, _, _ = _gate_out(mx["ya"], mx["yb"], yx, gate, wout_ref)
        y_ref[:, 0:BW] = ys[0]
        y_ref[:, BW:2 * BW] = ys[1]
        y_ref[:, 2 * BW:MIX] = ys[2]
        n, _ = _post_norm(o, pg[...])
        o_ref[...] = o
        x1_ref[...] = x_ref[...] + n * pg[...]

    tile, halo = _tile_specs(s, EVEN_IN)
    row = pl.BlockSpec((TS, D), lambda i: (i, 0))
    return _host_call(
        body, grid=(s // TS,), name="even_fwd", rider=rider,
        out_shape=(jax.ShapeDtypeStruct((s, D), F32), jax.ShapeDtypeStruct((s, D), F32),
                   jax.ShapeDtypeStruct((s, MIX), BF16)),
        in_specs=[row, tile, halo, _const((N_MEM, D)), _const((1, BW)), _const((1, BW)), _const((CH, 4 * CH)),
                  _const((CH, BW)), _const((3, BW)), _resident((MIX, D)), _const((1, D))],
        out_specs=(row, row, pl.BlockSpec((TS, MIX), lambda i: (i, 0))),
        scratch_shapes=[pltpu.VMEM((HALO + TS, BW), F32)],
        args=(x, p, p, kv, ln_g, ln_b, wcat, bsg, bconv, wout, post_g))


def _odd_fwd(x1, p, kv, wbd, cscale, dww, dwb, ln_g, ln_b, pww, pwb, wout, post_g, target):
    s = x1.shape[0]

    def body(x_ref, p_ref, ph_ref, kv_ref, bands_ref, wbd_ref, cs, dww_ref, dwb_ref, lng, lnb, pww_ref, pwb_ref,
             wout_ref, pg, tgt_ref, dx_ref, o_ref, cv_ref, loss_ref, gbuf, gsh):
        i = pl.program_id(0)
        mx = _odd_mix(i, p_ref, ph_ref, bands_ref, wbd_ref, cs[...], dww_ref, dwb_ref[...], lng[...], lnb[...],
                      pww_ref, pwb_ref[...], gbuf, gsh)
        cv_ref[...] = mx["cv"]
        yx, _ = _xattn_fwd(p_ref[:, 3 * BW:3 * BW + XA], kv_ref)
        gate = p_ref[:, 3 * BW + XA:ODD_IN].astype(F32)
        o, _, _, _ = _gate_out(mx["yc"], mx["yd"], yx, gate, wout_ref)
        n, _ = _post_norm(o, pg[...])
        o_ref[...] = o
        err = x_ref[...] + n * pg[...] - tgt_ref[...]
        dx_ref[...] = err * (1.0 / D)

        @pl.when(i == 0)
        def _():
            loss_ref[...] = jnp.zeros_like(loss_ref)

        loss_ref[...] += 0.5 * jnp.sum(jnp.sum(err * err, axis=-1, keepdims=True) * (1.0 / D), axis=0, keepdims=True)

    tile, halo = _tile_specs(s, ODD_IN)
    row = pl.BlockSpec((TS, D), lambda i: (i, 0))
    vec = _const((1, BW))
    return pl.pallas_call(
        body, grid=(s // TS,), name="odd_fwd",
        out_shape=(jax.ShapeDtypeStruct((s, D), F32), jax.ShapeDtypeStruct((s, D), F32),
                   jax.ShapeDtypeStruct((s, BW), F32), jax.ShapeDtypeStruct((8, 128), F32)),
        in_specs=[row, tile, halo, _const((N_MEM, D)), _const((4, TS, HALO + TS)), _const((BW, BW)), vec,
                  _const((CONF, BW)), vec, vec, vec, _const((BW, BW)), vec, _resident((MIX, D)), _const((1, D)), row],
        out_specs=(row, row, pl.BlockSpec((TS, BW), lambda i: (i, 0)), _const((8, 128))),
        scratch_shapes=[pltpu.VMEM((HALO + TS, BW), F32), pltpu.VMEM((7, SHIFT_ROWS, BW), F32)],
        compiler_params=_cp(("arbitrary",)),
    )(x1, p, p, kv, _band_matrices(TS, False), wbd, cscale, dww, dwb, ln_g, ln_b, pww, pwb, wout, post_g, target)


def _acc_init(i, refs):
    @pl.when(i == 0)
    def _():
        for r in refs:
            r[...] = jnp.zeros_like(r)


def _post_norm_bwd(dx, o, pg, dpg_ref):
    n, r = _post_norm(o, pg)
    dpg_ref[...] += jnp.sum(dx * n, axis=0, keepdims=True)
    dn = dx * pg
    return (r * (dn - n * jnp.mean(dn * n, axis=-1, keepdims=True))).astype(BF16)


def _gate_bwd(do, wout_ref, ys_f32, gate, y_ref):
    dy = _dot_nt(do, wout_ref[...])
    sgt = _sigmoid(gate)
    sgate = gate * sgt
    dsilu = sgt * (1.0 + gate * (1.0 - sgt))
    offs = (0, BW, 2 * BW, MIX)
    dys, dgs = [], []
    for j, yv in enumerate(ys_f32):
        a, b = offs[j], offs[j + 1]
        if y_ref is not None:
            y_ref[:, a:b] = (yv * sgate[:, a:b]).astype(BF16)
        dys.append(dy[:, a:b] * sgate[:, a:b])
        dgs.append(dy[:, a:b] * yv * dsilu[:, a:b])
    return dys, jnp.concatenate(dgs, axis=-1)


NEXT = 16


def _even_bwd1(dx, o, p, kv, ln_g, ln_b, wcat, bsg, hsel, bconv, wout, post_g, rider=None):
    s = dx.shape[0]
    nt = s // TS

    def body(dx_ref, o_ref, p_ref, ph_ref, dxn_ref, on_ref, pn_ref, kv_ref, lng, lnb, wcat_ref, bsg_ref, hsel_ref,
             bconv_ref, wout_ref, pg,
             dp_ref, do_ref, dpg_ref, dlng_ref, dlnb_ref, dwcat_ref, dbs_ref, dbconv_ref, dkv_ref, wbuf, dbuf):
        i = pl.program_id(0)
        _acc_init(i, (dpg_ref, dlng_ref, dlnb_ref, dwcat_ref, dbs_ref, dbconv_ref, dkv_ref))
        mx = _even_mix(i, p_ref, ph_ref, lng[...], lnb[...], wcat_ref, bsg_ref, bconv_ref, wbuf)
        q = p_ref[:, 5 * BW:5 * BW + XA]
        yx, probs = _xattn_fwd(q, kv_ref)
        gate = p_ref[:, 5 * BW + XA:EVEN_IN].astype(F32)
        do = _post_norm_bwd(dx_ref[...], o_ref[...], pg[...], dpg_ref)
        do_ref[...] = do
        (dya, dyb, dyx), dgate = _gate_bwd(do, wout_ref, (mx["ya"], mx["yb"], yx), gate, None)
        dp_ref[:, 0:BW] = (dya * mx["sg"]).astype(BF16)
        dsg = (dya * mx["u"]).astype(BF16)
        dvns = []
        for n in range(TS // CH):
            dsg_c = dsg[n * CH:(n + 1) * CH]
            dvst = _dot_tn(wcat_ref[...], dsg_c)
            dvn_c = jnp.where(mx["masks"][0], dvst[0:CH], 0.0)
            for h in range(1, 4):
                dvn_c = dvn_c + jnp.where(mx["masks"][h], dvst[h * CH:(h + 1) * CH], 0.0)
            dvns.append(dvn_c)
            dwcat_ref[...] += _dot_nt(dsg_c, mx["vsts"][n])
            dbs_ref[...] += _dot(dsg_c, hsel_ref[...])
        dvn = jnp.concatenate(dvns, axis=0)
        dlng_ref[...] += jnp.sum(dvn * mx["vhat"], axis=0, keepdims=True)
        dlnb_ref[...] += jnp.sum(dvn, axis=0, keepdims=True)
        dp_ref[:, BW:2 * BW] = _layer_norm_bwd(dvn, mx["vhat"], mx["rstd"], lng[...]).astype(BF16)
        dp_ref[:, 2 * BW:3 * BW] = (dyb * mx["conv"]).astype(BF16)
        dconv = dyb * mx["bg"]
        for k in range(3):
            dbconv_ref[k:k + 1, :] += jnp.sum(dconv * wbuf[pl.ds(HALO - 2 + k, TS), :], axis=0, keepdims=True)
        n_n, r_n = _post_norm(on_ref[...], pg[...])
        dn_n = dxn_ref[...] * pg[...]
        do_n = (r_n * (dn_n - n_n * jnp.mean(dn_n * n_n, axis=-1, keepdims=True))).astype(BF16)
        dy_n = _dot_nt(do_n, wout_ref[BW:2 * BW, :])
        g_n = pn_ref[:, 5 * BW + XA + BW:5 * BW + XA + 2 * BW].astype(F32)
        dconv_n = dy_n * (g_n * _sigmoid(g_n)) * pn_ref[:, 2 * BW:3 * BW].astype(F32)
        dbuf[0:TS, :] = dconv
        dbuf[TS:TS + NEXT, :] = jnp.where(i < nt - 1, dconv_n, 0.0)
        dw = (bconv_ref[2:3, :] * dconv + bconv_ref[1:2, :] * dbuf[pl.ds(1, TS), :]
              + bconv_ref[0:1, :] * dbuf[pl.ds(2, TS), :])
        dp_ref[:, 3 * BW:4 * BW] = (dw * mx["xin"]).astype(BF16)
        dp_ref[:, 4 * BW:5 * BW] = (dw * mx["cg"]).astype(BF16)
        dp_ref[:, 5 * BW:5 * BW + XA] = _xattn_bwd(dyx, q, probs, kv_ref, dkv_ref).astype(BF16)
        dp_ref[:, 5 * BW + XA:EVEN_IN] = dgate.astype(BF16)

    tile, halo = _tile_specs(s, EVEN_IN)
    row = pl.BlockSpec((TS, D), lambda i: (i, 0))
    vec = _const((1, BW))
    nxt = _halo_next(TS // NEXT, s // NEXT)

    def out(n):
        return pl.BlockSpec((TS, n), lambda i: (i, 0))

    return _host_call(
        body, grid=(nt,), name="even_bwd1", rider=rider,
        out_shape=(jax.ShapeDtypeStruct((s, EVEN_IN), BF16), jax.ShapeDtypeStruct((s, D), BF16),
                   jax.ShapeDtypeStruct((1, D), F32), jax.ShapeDtypeStruct((1, BW), F32),
                   jax.ShapeDtypeStruct((1, BW), F32), jax.ShapeDtypeStruct((CH, 4 * CH), F32),
                   jax.ShapeDtypeStruct((CH, 128), F32), jax.ShapeDtypeStruct((8, BW), F32),
                   jax.ShapeDtypeStruct((N_MEM, D), F32)),
        in_specs=[row, row, tile, halo, pl.BlockSpec((NEXT, D), nxt), pl.BlockSpec((NEXT, D), nxt),
                  pl.BlockSpec((NEXT, EVEN_IN), nxt), _const((N_MEM, D)), vec, vec, _const((CH, 4 * CH)),
                  _const((CH, BW)), _const((BW, 128)), _const((3, BW)), _resident((MIX, D)), _const((1, D))],
        out_specs=(out(EVEN_IN), out(D),
                   _const((1, D)), vec, vec, _const((CH, 4 * CH)), _const((CH, 128)), _const((8, BW)),
                   _const((N_MEM, D))),
        scratch_shapes=[pltpu.VMEM((HALO + TS, BW), F32), pltpu.VMEM((TS + NEXT, BW), F32)],
        args=(dx, o, p, p, dx, o, p, kv, ln_g, ln_b, wcat, bsg, hsel, bconv, wout, post_g))


def _odd_bwd1(dx, o, cv, p, kv, wbd, cscale, dww, dwb, ln_g, ln_b, pww, pwb, wout, post_g):
    s = dx.shape[0]

    def body(dx_ref, o_ref, cv_ref, p_ref, ph_ref, kv_ref, bands_ref, wbd_ref, cs, dww_ref, dwb_ref, lng, lnb,
             pww_ref, pwb_ref, wout_ref, pg,
             dpc_ref, tmpc_ref, tmpd_ref, do_ref, y_ref, dpg_ref, dcs_ref, dwbd_ref, ddww_ref, ddwb_ref, dlng_ref,
             dlnb_ref, dpww_ref, dpwb_ref, dkv_ref, gbuf, gsh, dcv_buf):
        i = pl.program_id(0)
        _acc_init(i, (dpg_ref, dcs_ref, dwbd_ref, ddww_ref, ddwb_ref, dlng_ref, dlnb_ref, dpww_ref, dpwb_ref,
                      dkv_ref))
        mx = _odd_mix(i, p_ref, ph_ref, bands_ref, wbd_ref, cs[...], dww_ref, dwb_ref[...], lng[...], lnb[...],
                      pww_ref, pwb_ref[...], gbuf, gsh, cv=cv_ref[...])
        q = p_ref[:, 3 * BW:3 * BW + XA]
        yx, probs = _xattn_fwd(q, kv_ref)
        gate = p_ref[:, 3 * BW + XA:ODD_IN].astype(F32)
        do = _post_norm_bwd(dx_ref[...], o_ref[...], pg[...], dpg_ref)
        do_ref[...] = do
        (dyc, dyd, dyx), dgate = _gate_bwd(do, wout_ref, (mx["yc"], mx["yd"], yx), gate, y_ref)
        dcs_ref[...] += jnp.sum(dyc * mx["pre"], axis=0, keepdims=True)
        dpre = (dyc * cs[...]).astype(BF16)
        dwbd_ref[...] += _dot_tn(mx["pooled_bf"], dpre)
        dpooled = _dot_nt(dpre, wbd_ref[...])
        tmpc_ref[...] = _pool_select([dpooled * c_ for c_ in mx["inv"]]).astype(BF16)
        dyd_bf = dyd.astype(BF16)
        dpwb_ref[...] += jnp.sum(dyd, axis=0, keepdims=True)
        dpww_ref[...] += _dot_tn(mx["zs"], dyd_bf)
        dzs = _dot_nt(dyd_bf, pww_ref[...])
        zl, szl = mx["zl"], mx["szl"]
        dzl = dzs * (szl * (1.0 + zl * (1.0 - szl)))
        dlng_ref[...] += jnp.sum(dzl * mx["zhat"], axis=0, keepdims=True)
        dlnb_ref[...] += jnp.sum(dzl, axis=0, keepdims=True)
        dcv = _layer_norm_bwd(dzl, mx["zhat"], mx["rstd"], lng[...])
        tmpd_ref[...] = dcv.astype(BF16)
        ddwb_ref[...] += jnp.sum(dcv, axis=0, keepdims=True)
        dcv_buf[...] = dcv
        _tap_sums(dcv_buf, gbuf, gsh, HALO - (CONF - 1), ddww_ref)
        dpc_ref[:, 0:XA] = _xattn_bwd(dyx, q, probs, kv_ref, dkv_ref).astype(BF16)
        dpc_ref[:, XA:XA + MIX] = dgate.astype(BF16)

    tile, halo = _tile_specs(s, ODD_IN)
    row = pl.BlockSpec((TS, D), lambda i: (i, 0))
    vec = _const((1, BW))

    def out(n):
        return pl.BlockSpec((TS, n), lambda i: (i, 0))

    return pl.pallas_call(
        body, grid=(s // TS,), name="odd_bwd1",
        out_shape=(jax.ShapeDtypeStruct((s, XA + MIX), BF16), jax.ShapeDtypeStruct((s, BW), BF16),
                   jax.ShapeDtypeStruct((s, BW), BF16), jax.ShapeDtypeStruct((s, D), BF16),
                   jax.ShapeDtypeStruct((s, MIX), BF16),
                   jax.ShapeDtypeStruct((1, D), F32), jax.ShapeDtypeStruct((1, BW), F32),
                   jax.ShapeDtypeStruct((BW, BW), F32), jax.ShapeDtypeStruct((8 * CONF, BW), F32),
                   jax.ShapeDtypeStruct((1, BW), F32), jax.ShapeDtypeStruct((1, BW), F32),
                   jax.ShapeDtypeStruct((1, BW), F32), jax.ShapeDtypeStruct((BW, BW), F32),
                   jax.ShapeDtypeStruct((1, BW), F32), jax.ShapeDtypeStruct((N_MEM, D), F32)),
        in_specs=[row, row, out(BW), tile, halo, _const((N_MEM, D)), _const((4, TS, HALO + TS)), _const((BW, BW)), vec,
                  _const((CONF, BW)), vec, vec, vec, _const((BW, BW)), vec, _resident((MIX, D)), _const((1, D))],
        out_specs=(out(XA + MIX), out(BW), out(BW), out(D), out(MIX),
                   _const((1, D)), vec, _const((BW, BW)), _const((8 * CONF, BW)), vec, vec, vec, _const((BW, BW)), vec,
                   _const((N_MEM, D))),
        scratch_shapes=[pltpu.VMEM((HALO + TS, BW), F32), pltpu.VMEM((7, SHIFT_ROWS, BW), F32),
                        pltpu.VMEM((TS, BW), F32)],
        compiler_params=_cp(("arbitrary",)),
    )(dx, o, cv, p, p, kv, _band_matrices(TS, False), wbd, cscale, dww, dwb, ln_g, ln_b, pww, pwb, wout, post_g)


def _halo_next(nblk_per_tile, nblk):
    return lambda i: (jnp.minimum((i + 1) * nblk_per_tile, nblk - 1), 0)


def _pre_norm_bwd(dh, x, pre_g, dres, dpre_ref):
    r = lax.rsqrt(jnp.mean(x * x, axis=-1, keepdims=True) + EPS)
    xh = x * r
    dpre_ref[...] += jnp.sum(dh * xh, axis=0, keepdims=True)
    dxh = dh * pre_g
    return dres + r * (dxh - xh * jnp.mean(dxh * xh, axis=-1, keepdims=True))


def _even_bwd2(dp, w_t, x, pre_g, dres, rider=None):
    s = x.shape[0]
    tm = min(512, s)

    def body(dp_ref, w_ref, x_ref, pg, dres_ref, dx_ref, dpre_ref):
        _acc_init(pl.program_id(0), (dpre_ref,))
        dh = _dot(dp_ref[...], w_ref[...])
        dx_ref[...] = _pre_norm_bwd(dh, x_ref[...], pg[...], dres_ref[...], dpre_ref)

    row = pl.BlockSpec((tm, D), lambda i: (i, 0))
    return _host_call(
        body, grid=(s // tm,), name="even_bwd2", rider=rider,
        out_shape=(jax.ShapeDtypeStruct((s, D), F32), jax.ShapeDtypeStruct((1, D), F32)),
        in_specs=[pl.BlockSpec((tm, EVEN_IN), lambda i: (i, 0)), _resident((EVEN_IN, D)), row, _const((1, D)), row],
        out_specs=(row, _const((1, D))),
        args=(dp, w_t, x, pre_g, dres))


def _odd_bwd2(dpc, tmpc, tmpd, p, dww, w_t, x, pre_g, dres):
    s = x.shape[0]
    nt = s // TS

    def body(dpc_ref, tc_ref, tch_ref, td_ref, tdh_ref, ga_ref, gb_ref, bands_ref, dww_ref, w_ref, x_ref, pg,
             dres_ref, dpb_ref, dx_ref, dpre_ref, dbuf, dsh):
        i = pl.program_id(0)
        _acc_init(i, (dpre_ref,))
        more = i < nt - 1
        e_bf = tc_ref[...]
        eh = tch_ref[...]
        ecat = jnp.concatenate([e_bf, jnp.where(more, eh, jnp.zeros_like(eh))], axis=0)
        dbuf[0:TS, :] = td_ref[...].astype(F32)
        dbuf[TS:TS + HALO, :] = jnp.where(more, tdh_ref[...].astype(F32), 0.0)
        sums = [_dot(bands_ref[w], ecat) for w in range(len(POOL_WINDOWS))]
        rows = _row_ids(i, TS) + 1
        cnt = _pool_select([jnp.minimum(rows, w).astype(F32) for w in POOL_WINDOWS])
        dzc = (_pool_select(sums) - e_bf.astype(F32) * cnt).astype(BF16)
        _shifted_copies(dbuf, dsh)
        dz = dww_ref[CONF - 1:CONF, :] * dbuf[pl.ds(0, TS), :]
        for sft in range(1, CONF):
            dz = dz + dww_ref[CONF - 1 - sft:CONF - sft, :] * _rows_at(dbuf, dsh, sft, TS)
        ga = ga_ref[...].astype(F32)
        sgb = _sigmoid(gb_ref[...].astype(F32))
        dga = (dz * sgb).astype(BF16)
        dgb = (dz * ga * sgb * (1.0 - sgb)).astype(BF16)
        dpb_ref[:, 0:BW] = dzc
        dpb_ref[:, BW:2 * BW] = dga
        dpb_ref[:, 2 * BW:3 * BW] = dgb
        dh = (_dot(dzc, w_ref[0:BW, :]) + _dot(dga, w_ref[BW:2 * BW, :]) + _dot(dgb, w_ref[2 * BW:3 * BW, :])
              + _dot(dpc_ref[...], w_ref[3 * BW:ODD_IN, :]))
        dx_ref[...] = _pre_norm_bwd(dh, x_ref[...], pg[...], dres_ref[...], dpre_ref)

    row = pl.BlockSpec((TS, D), lambda i: (i, 0))

    def tile(n, j=0):
        return pl.BlockSpec((TS, n), lambda i: (i, j))

    nxt = pl.BlockSpec((HALO, BW), _halo_next(TS // HALO, s // HALO))
    return pl.pallas_call(
        body, grid=(nt,), name="odd_bwd2",
        out_shape=(jax.ShapeDtypeStruct((s, 3 * BW), BF16), jax.ShapeDtypeStruct((s, D), F32),
                   jax.ShapeDtypeStruct((1, D), F32)),
        in_specs=[tile(XA + MIX), tile(BW), nxt, tile(BW), nxt, tile(BW, 1), tile(BW, 2), _const((4, TS, HALO + TS)),
                  _const((CONF, BW)), _resident((ODD_IN, D)), row, _const((1, D)), row],
        out_specs=(tile(3 * BW), row, _const((1, D))),
        scratch_shapes=[pltpu.VMEM((TS + HALO, BW), F32), pltpu.VMEM((7, SHIFT_ROWS, BW), F32)],
        compiler_params=_cp(("arbitrary",)),
    )(dpc, tmpc, tmpc, tmpd, tmpd, p, p, _band_matrices(TS, True), dww, w_t, x, pre_g, dres)


def _grad_tn(a, b, tm, out=None, rows=None, row0=0, name="grad_tn", rider=None):
    s, m = a.shape
    n = b.shape[1]
    ts = min(2048, s)
    rows = m if rows is None else rows
    assert m % tm == 0 and s % ts == 0
    ns = s // ts
    if row0 % tm == 0:
        out_spec = pl.BlockSpec((tm, n), lambda i, k: (row0 // tm + i, 0))
    else:
        align = 16
        assert row0 % align == 0 and tm % align == 0
        out_spec = pl.BlockSpec((pl.Element(tm), pl.Element(n)),
                                lambda i, k: (pl.multiple_of(row0 + i * tm, align), 0))

    def body(*refs):
        a_ref, b_ref = refs[0], refs[1]
        o_ref, acc = refs[-2], refs[-1]
        k = pl.program_id(1)

        @pl.when(k == 0)
        def _():
            acc[...] = jnp.zeros_like(acc)

        acc[...] += _dot_tn(a_ref[...], b_ref[...])

        @pl.when(k == ns - 1)
        def _():
            o_ref[...] = acc[...].astype(BF16)

    in_specs = [pl.BlockSpec((ts, tm), lambda i, k: (k, i)), pl.BlockSpec((ts, n), lambda i, k: (k, 0))]
    args = [a, b]
    aliases = {}
    if out is not None:
        in_specs.append(pl.BlockSpec(memory_space=pltpu.HBM))
        args.append(out)
        aliases = {2: 0}
    (res,), got = _host_call(
        body, grid=(m // tm, ns), name=name, rider=rider, aliases=aliases,
        out_shape=(jax.ShapeDtypeStruct((rows, n), BF16),), in_specs=in_specs, out_specs=(out_spec,),
        scratch_shapes=[pltpu.VMEM((tm, n), F32)], args=args)
    return res if rider is None else (res, got)


def _place():
    x, y, c = lax.axis_index("x"), lax.axis_index("y"), lax.axis_index("c")
    chips = [(1 - x, y), (x, 1 - y), (1 - x, 1 - y)]
    return x, y, c, chips


def _hbm_specs(n):
    return [pl.BlockSpec(memory_space=pltpu.HBM)] * n


def _row_tile(r):
    for cand in (512, 400, 304, 256, 192, 128, 96, 16):
        if r % cand == 0:
            return cand
    raise ValueError(r)


def _place_shard(shard, place, dtype, name, after=None):
    r, cc = shard.shape
    tr = _row_tile(r)
    nt = r // tr

    def body(place_ref, s_ref, *rest):
        rest[-1][...] = s_ref[...].astype(dtype)

    in_specs = [pl.BlockSpec((tr, cc), lambda i, pr: (i, 0))]
    args = [shard]
    if after is not None:
        in_specs.append(pl.BlockSpec(after.shape, lambda i, pr: (0, 0)))
        args.append(after)
    return pl.pallas_call(
        body, name=name, out_shape=jax.ShapeDtypeStruct((N_CHIPS * r, cc), dtype),
        grid_spec=pltpu.PrefetchScalarGridSpec(
            num_scalar_prefetch=1, grid=(nt,), in_specs=in_specs,
            out_specs=pl.BlockSpec((tr, cc), lambda i, pr: (pr[1] * nt + i, 0))),
        compiler_params=_cp(("arbitrary",)),
    )(place, *args)


class _GatherRider:
    has_mid = True

    def __init__(self, fulls):
        n = len(fulls)
        self.inputs = list(fulls)
        self.out_shapes = [jax.ShapeDtypeStruct(a.shape, a.dtype) for a in fulls]
        self.aliases = {a: a for a in range(n)}
        self.sems = [pltpu.SemaphoreType.DMA((6 * n,)), pltpu.SemaphoreType.DMA((6 * n,))]
        self.block_rows = [a.shape[0] // N_CHIPS for a in fulls]

    def _ctx(self, outs, sems):
        send_sems, recv_sems = sems
        x, y, c, chips = _place()

        def rows(a, k, half):
            r = self.block_rows[a]
            return outs[a].at[pl.ds(k * r + half * (r // 2), r // 2)]

        def copy(a, j, blk, to):
            return pltpu.make_async_remote_copy(src_ref=blk, dst_ref=blk, send_sem=send_sems.at[a * 6 + j],
                                                recv_sem=recv_sems.at[a * 6 + j], device_id=to, device_id_type=MESH)

        return x, y, c, chips, rows, copy

    def start(self, ins, outs, sems, peers=(0, 1, 2)):
        x, y, c, chips, rows, copy = self._ctx(outs, sems)
        for j in peers:
            for a in range(len(outs)):
                copy(a, j, rows(a, 2 * x + y, c), (*chips[j], c)).start()

    def mid(self, ins, outs, sems, peers=(0, 1, 2)):
        x, y, c, chips, rows, copy = self._ctx(outs, sems)
        for j in peers:
            px, py = chips[j]
            for a in range(len(outs)):
                copy(a, j, rows(a, 2 * px + py, c), (px, py, c)).wait_recv()
                copy(a, 3 + j, rows(a, 2 * px + py, c), (x, y, 1 - c)).start()

    def wait_forwarded(self, outs, sems, peers=(0, 1, 2)):
        x, y, c, chips, rows, copy = self._ctx(outs, sems)
        for j in peers:
            px, py = chips[j]
            for a in range(len(outs)):
                copy(a, 3 + j, rows(a, 2 * px + py, 1 - c), (x, y, 1 - c)).wait_recv()

    def wait_sends(self, outs, sems):
        x, y, c, chips, rows, copy = self._ctx(outs, sems)
        for j, (px, py) in enumerate(chips):
            for a in range(len(outs)):
                copy(a, j, rows(a, 2 * x + y, c), (px, py, c)).wait_send()
                copy(a, 3 + j, rows(a, 2 * px + py, c), (x, y, 1 - c)).wait_send()

    def end(self, ins, outs, sems):
        self.wait_forwarded(outs, sems)
        self.wait_sends(outs, sems)


def _swap_halves(grads, name, share=()):
    n, k = len(grads), len(share)
    m = n + k

    def body(*refs):
        ins, outs = refs[:m], refs[m:2 * m]
        send_sems, recv_sems = refs[2 * m:]
        x, y, c, _ = _place()
        sibling = (x, y, 1 - c)
        cps, waits = [], []
        for a in range(m):
            if a < n:
                cp = pltpu.make_async_remote_copy(src_ref=ins[a].at[:, 1 - c], dst_ref=outs[a],
                                                  send_sem=send_sems.at[a], recv_sem=recv_sems.at[a],
                                                  device_id=sibling, device_id_type=MESH)
                waits.append(cp)
            else:
                cp = pltpu.make_async_remote_copy(src_ref=outs[a].at[c], dst_ref=outs[a].at[c],
                                                  send_sem=send_sems.at[a], recv_sem=recv_sems.at[a],
                                                  device_id=sibling, device_id_type=MESH)
                waits.append(pltpu.make_async_remote_copy(
                    src_ref=outs[a].at[1 - c], dst_ref=outs[a].at[1 - c], send_sem=send_sems.at[a],
                    recv_sem=recv_sems.at[a], device_id=sibling, device_id_type=MESH))
            cp.start()
            cps.append(cp)
        for cp in waits:
            cp.wait_recv()
        for cp in cps:
            cp.wait_send()

    outs = tuple(jax.ShapeDtypeStruct((g.shape[0],) + g.shape[2:], g.dtype) for g in grads)
    outs += tuple(jax.ShapeDtypeStruct(g.shape, g.dtype) for g in share)
    res = pl.pallas_call(
        body, name=name, out_shape=outs, in_specs=_hbm_specs(m), out_specs=tuple(_hbm_specs(m)),
        input_output_aliases={n + a: n + a for a in range(k)},
        scratch_shapes=[pltpu.SemaphoreType.DMA((m,)), pltpu.SemaphoreType.DMA((m,))],
    )(*grads, *share)
    return tuple(res[:n]), tuple(res[n:])


def _pair_sum(g, recv, place, name):
    _, _, h, cc = g.shape
    th = h

    def body(c_ref, g_ref, r_ref, o_ref):
        o_ref[...] = (g_ref[...].astype(F32) + r_ref[...].astype(F32)).astype(o_ref.dtype)

    return pl.pallas_call(
        body, name=name, out_shape=jax.ShapeDtypeStruct(recv.shape, recv.dtype),
        grid_spec=pltpu.PrefetchScalarGridSpec(
            num_scalar_prefetch=1, grid=(N_CHIPS, h // th),
            in_specs=[pl.BlockSpec((None, None, th, cc), lambda k, r, c_ref: (k, c_ref[0], r, 0)),
                      pl.BlockSpec((None, th, cc), lambda k, r, c_ref: (k, r, 0))],
            out_specs=pl.BlockSpec((None, th, cc), lambda k, r, c_ref: (k, r, 0))),
        compiler_params=_cp(("arbitrary", "arbitrary")),
    )(place, g, recv)


def _finish_reduce(pack, halves):
    rows, cc = pack.shape
    hs = rows // 2
    n = len(halves)

    def body(*refs):
        pack_ref = refs[0]
        out_ref = refs[1 + n]
        big = refs[2 + n:2 + 2 * n]
        sib_ref, parts_ref, send_sems, recv_sems, big_send, big_recv = refs[2 + 2 * n:]
        x, y, c, chips = _place()
        me_k = 2 * x + y
        sibling = (x, y, 1 - c)
        mine = pl.ds(pl.multiple_of(c * hs, hs), hs)
        theirs = pl.ds(pl.multiple_of((1 - c) * hs, hs), hs)
        shared = [pltpu.make_async_remote_copy(src_ref=big[a].at[c], dst_ref=big[a].at[c], send_sem=big_send.at[a],
                                               recv_sem=big_recv.at[a], device_id=sibling, device_id_type=MESH)
                  for a in range(n)]
        for cp in shared:
            cp.start()
        first = pltpu.make_async_remote_copy(src_ref=pack_ref, dst_ref=sib_ref, send_sem=send_sems.at[0],
                                             recv_sem=recv_sems.at[0], device_id=sibling, device_id_type=MESH)
        first.start()
        first.wait()
        parts_ref[me_k] = pack_ref[mine, :] + sib_ref[mine, :]
        cps = [pltpu.make_async_remote_copy(src_ref=parts_ref.at[me_k], dst_ref=parts_ref.at[me_k],
                                            send_sem=send_sems.at[1 + j], recv_sem=recv_sems.at[1 + j],
                                            device_id=(px, py, c), device_id_type=MESH)
               for j, (px, py) in enumerate(chips)]
        for cp in cps:
            cp.start()
        for j, (px, py) in enumerate(chips):
            pltpu.make_async_remote_copy(src_ref=parts_ref.at[2 * px + py], dst_ref=parts_ref.at[2 * px + py],
                                         send_sem=send_sems.at[1 + j], recv_sem=recv_sems.at[1 + j],
                                         device_id=(px, py, c), device_id_type=MESH).wait_recv()
        for cp in cps:
            cp.wait_send()
        out_ref[mine, :] = ((parts_ref[0] + parts_ref[1]) + parts_ref[2]) + parts_ref[3]
        last = pltpu.make_async_remote_copy(src_ref=out_ref.at[mine], dst_ref=out_ref.at[mine],
                                            send_sem=send_sems.at[4], recv_sem=recv_sems.at[4], device_id=sibling,
                                            device_id_type=MESH)
        last.start()
        pltpu.make_async_remote_copy(src_ref=out_ref.at[theirs], dst_ref=out_ref.at[theirs],
                                     send_sem=send_sems.at[4], recv_sem=recv_sems.at[4], device_id=sibling,
                                     device_id_type=MESH).wait_recv()
        last.wait_send()
        for a in range(n):
            pltpu.make_async_remote_copy(src_ref=big[a].at[1 - c], dst_ref=big[a].at[1 - c], send_sem=big_send.at[a],
                                         recv_sem=big_recv.at[a], device_id=sibling,
                                         device_id_type=MESH).wait_recv()
        for cp in shared:
            cp.wait_send()

    vmem = pl.BlockSpec(memory_space=pltpu.VMEM)
    res = pl.pallas_call(
        body, name="finish_reduce",
        out_shape=(jax.ShapeDtypeStruct(pack.shape, pack.dtype),)
        + tuple(jax.ShapeDtypeStruct(g.shape, g.dtype) for g in halves),
        in_specs=[vmem] + _hbm_specs(n), out_specs=(vmem,) + tuple(_hbm_specs(n)),
        input_output_aliases={1 + a: 1 + a for a in range(n)},
        scratch_shapes=[pltpu.VMEM((rows, cc), F32), pltpu.VMEM((N_CHIPS, hs, cc), F32),
                        pltpu.SemaphoreType.DMA((5,)), pltpu.SemaphoreType.DMA((5,)),
                        pltpu.SemaphoreType.DMA((n,)), pltpu.SemaphoreType.DMA((n,))],
        compiler_params=_cp(),
    )(pack, *halves)
    return res[0], tuple(res[1:])


class _ExchangeRider:
    has_mid = False

    def __init__(self, sums):
        self.inputs = list(sums)
        self.out_shapes = [jax.ShapeDtypeStruct((3,) + g.shape[1:], g.dtype) for g in sums]
        m = len(self.inputs)
        self.aliases = {}
        self.sems = [pltpu.SemaphoreType.DMA((3 * m,)), pltpu.SemaphoreType.DMA((3 * m,))]

    def _copies(self, ins, outs, sems):
        send_sems, recv_sems = sems
        _, _, c, chips = _place()
        return [pltpu.make_async_remote_copy(
            src_ref=ins[a].at[2 * px + py], dst_ref=outs[a].at[j], send_sem=send_sems.at[a * 3 + j],
            recv_sem=recv_sems.at[a * 3 + j], device_id=(px, py, c), device_id_type=MESH)
            for j, (px, py) in enumerate(chips) for a in range(len(ins))]

    def start(self, ins, outs, sems):
        for cp in self._copies(ins, outs, sems):
            cp.start()

    def end(self, ins, outs, sems):
        cps = self._copies(ins, outs, sems)
        for cp in cps:
            cp.wait_recv()
        for cp in cps:
            cp.wait_send()


def _chip_sum(own, parts, place, name):
    npart, h, cc = parts.shape
    th = _row_tile(h)

    def body(place_ref, own_ref, p_ref, o_ref):
        acc = own_ref[...].astype(F32) + p_ref[0].astype(F32)
        for k in range(1, npart):
            acc = acc + p_ref[k].astype(F32)
        o_ref[...] = acc

    return pl.pallas_call(
        body, name=name, out_shape=jax.ShapeDtypeStruct((2, h, cc), F32),
        grid_spec=pltpu.PrefetchScalarGridSpec(
            num_scalar_prefetch=1, grid=(h // th,),
            in_specs=[pl.BlockSpec((None, th, cc), lambda r, pr: (pr[1], r, 0)),
                      pl.BlockSpec((npart, th, cc), lambda r, pr: (0, r, 0))],
            out_specs=pl.BlockSpec((None, th, cc), lambda r, pr: (pr[0], r, 0))),
        compiler_params=_cp(("arbitrary",)),
    )(place, own, parts)


def _adamw_math(w, g, m, v):
    m = ADAM_B1 * m + (1.0 - ADAM_B1) * g
    v = ADAM_B2 * v + (1.0 - ADAM_B2) * (g * g)
    m_hat = m / (1.0 - ADAM_B1 ** ADAM_STEP)
    v_hat = v / (1.0 - ADAM_B2 ** ADAM_STEP)
    delta = -ADAM_LR * (m_hat / (jnp.sqrt(v_hat) + ADAM_EPS) + ADAM_WD * w)
    return delta, m, v


def _adamw_big(w, g, m, v, name):
    r, cc = w.shape
    tr = min(_row_tile(r), 256) if r % 256 == 0 else _row_tile(r)

    def body(w_ref, g_ref, m_ref, v_ref, go_ref, d_ref, mo_ref, vo_ref):
        g = g_ref[...]
        d, mm, vv = _adamw_math(w_ref[...], g, m_ref[...], v_ref[...])
        go_ref[...] = g
        d_ref[...] = d
        mo_ref[...] = mm
        vo_ref[...] = vv

    blk = pl.BlockSpec((tr, cc), lambda i: (i, 0))
    sd = jax.ShapeDtypeStruct((r, cc), F32)
    return pl.pallas_call(body, grid=(r // tr,), name=name, out_shape=(sd, sd, sd, sd), in_specs=[blk] * 4,
                          out_specs=(blk, blk, blk, blk), compiler_params=_cp(("arbitrary",)))(w, g, m, v)


SC_TILES = 32


def _adamw_sparsecore(ws, gs, ms, vs, name):
    n = len(ws)
    rows_per = 8
    widths = sorted({a.shape[1] for a in ws})
    assert all(a.shape[0] % rows_per == 0 and a.shape[1] % 16 == 0 for a in ws)

    def body(*refs):
        ins, outs, bufs = refs[:4 * n], refs[4 * n:8 * n], refs[8 * n:]
        tile = lax.axis_index("sc_tile") * 2 + lax.axis_index("sc_core")
        for a in range(n):
            w_hbm, g_hbm, m_hbm, v_hbm = ins[4 * a:4 * a + 4]
            go_hbm, d_hbm, mo_hbm, vo_hbm = outs[4 * a:4 * a + 4]
            r, cc = ws[a].shape
            k = widths.index(cc)
            wb, gb, mb, vb, db = bufs[5 * k:5 * k + 5]
            groups = r // rows_per

            @pl.loop(0, -(-groups // SC_TILES))
            def _(q):
                grp = tile + q * SC_TILES

                @pl.when(grp < groups)
                def _():
                    rows = pl.ds(pl.multiple_of(grp * rows_per, rows_per), rows_per)
                    pltpu.sync_copy(w_hbm.at[rows], wb)
                    pltpu.sync_copy(g_hbm.at[rows], gb)
                    pltpu.sync_copy(m_hbm.at[rows], mb)
                    pltpu.sync_copy(v_hbm.at[rows], vb)

                    @pl.loop(0, rows_per)
                    def _(i):
                        @pl.loop(0, cc, step=16)
                        def _(j):
                            at = (i, pl.ds(j, 16))
                            d, mm, vv = _adamw_math(wb[at], gb[at], mb[at], vb[at])
                            db[at] = d
                            mb[at] = mm
                            vb[at] = vv

                    pltpu.sync_copy(gb, go_hbm.at[rows])
                    pltpu.sync_copy(db, d_hbm.at[rows])
                    pltpu.sync_copy(mb, mo_hbm.at[rows])
                    pltpu.sync_copy(vb, vo_hbm.at[rows])

    args, out_type = [], []
    for a in range(n):
        args += [ws[a], gs[a], ms[a], vs[a]]
        out_type += [jax.ShapeDtypeStruct(ws[a].shape, F32)] * 4
    res = pl.kernel(
        body, name=name, out_type=tuple(out_type),
        mesh=plsc.VectorSubcoreMesh(core_axis_name="sc_core", subcore_axis_name="sc_tile"),
        scratch_types=[pltpu.VMEM((rows_per, cc), F32) for cc in widths for _ in range(5)],
    )(*args)
    return [tuple(res[4 * a:4 * a + 4]) for a in range(n)]


def _adamw_small(ws, gs, ms, vs):
    n = len(ws)

    def body(*refs):
        for a in range(n):
            w_ref, g_ref, m_ref, v_ref = refs[4 * a:4 * a + 4]
            d_ref, mo_ref, vo_ref = refs[4 * n + 3 * a:4 * n + 3 * a + 3]
            d, mm, vv = _adamw_math(w_ref[...], g_ref[...], m_ref[...], v_ref[...])
            d_ref[...] = d
            mo_ref[...] = mm
            vo_ref[...] = vv

    args, outs = [], []
    for a in range(n):
        args += [ws[a], gs[a], ms[a], vs[a]]
        outs += [jax.ShapeDtypeStruct(ws[a].shape, F32)] * 3
    res = pl.pallas_call(body, name="adamw_small", out_shape=tuple(outs), compiler_params=_cp())(*args)
    return [res[3 * a:3 * a + 3] for a in range(n)]


def _flat_pack(arrs, rows):
    flat = jnp.concatenate([a.reshape(-1) for a in arrs])
    return jnp.pad(flat, (0, rows * D - flat.shape[0])).reshape(rows, D)


def _flat_unpack(flat, shapes):
    out, off = [], 0
    for shp in shapes:
        size = 1
        for d_ in shp:
            size *= d_
        out.append(flat[off:off + size].reshape(shp))
        off += size
    return out


SMALL_EVEN = ("even_pre_g", "even_a_ln_g", "even_a_ln_b", "even_a_ws", "even_a_bs", "even_b_conv", "even_mem_g",
              "even_post_g")
SMALL_ODD = ("odd_pre_g", "odd_c_wgrp", "odd_c_scale", "odd_d_dw_w", "odd_d_dw_b", "odd_d_ln_g", "odd_d_ln_b",
             "odd_d_pw_b", "odd_mem_g", "odd_post_g")
BIG = ("even_w_in", "even_w_kv", "even_w_out", "odd_w_in", "odd_d_pw_w", "odd_w_kv", "odd_w_out")
WEIGHTS = ("even_pre_g", "even_w_in", "even_a_ln_g", "even_a_ln_b", "even_a_ws", "even_a_bs", "even_b_conv",
           "even_mem_g", "even_w_kv", "even_w_out", "even_post_g", "odd_pre_g", "odd_w_in", "odd_c_wgrp",
           "odd_c_scale", "odd_d_dw_w", "odd_d_dw_b", "odd_d_ln_g", "odd_d_ln_b", "odd_d_pw_w", "odd_d_pw_b",
           "odd_mem_g", "odd_w_kv", "odd_w_out", "odd_post_g")
PACKED = (("even_b_conv", (3, 192)), ("odd_pre_g", (1, 256)), ("odd_c_scale", (1, 192)), ("odd_d_dw_w", (31, 192)),
          ("odd_d_dw_b", (1, 192)), ("odd_d_ln_g", (1, 192)), ("odd_d_ln_b", (1, 192)), ("odd_d_pw_b", (1, 192)),
          ("odd_mem_g", (1, 256)), ("odd_post_g", (1, 256)))
PACK_ROWS = 16
SMALL_ROWS = 256


def _four(g):
    return g.reshape(N_CHIPS, 2, g.shape[0] // (2 * N_CHIPS), g.shape[1])


def _step(x, mem, target, w, mom, var, place):
    wt = {}
    pack = _flat_pack([w[n][0] for n, _ in PACKED], PACK_ROWS)
    shards = {"even_w_in_t": w["even_w_in"][0].T, "odd_w_in_t": w["odd_w_in"][0].T, "even_w_kv": w["even_w_kv"][0],
              "odd_w_kv": w["odd_w_kv"][0], "even_w_out": w["even_w_out"][0], "odd_w_out": w["odd_w_out"][0],
              "odd_d_pw_w": w["odd_d_pw_w"][0]}
    placed = {n: _place_shard(shards[n], place, BF16, "place_" + n) for n in ("even_w_in_t", "even_w_kv", "even_w_out")}
    placed["pack"] = _place_shard(pack, place, F32, "place_pack")

    order, group = _stream_tables(place[0], place[1], EVEN_IN)
    p_e, h_e, (wt["even_w_in_t"], packs), (wt["even_w_kv"], wt["even_w_out"]) = _in_fwd_streamed(
        x, w["even_pre_g"], [placed["even_w_in_t"], placed["pack"]], [placed["even_w_kv"], placed["even_w_out"]],
        order, group, "even_in_streamed")
    for n in ("odd_w_in_t", "odd_w_kv", "odd_w_out", "odd_d_pw_w"):
        placed[n] = _place_shard(shards[n], place, BF16, "place_" + n, after=p_e[0:16, 0:128])
    packs = packs.reshape(N_CHIPS, PACK_ROWS * D)
    per_chip = [_flat_unpack(packs[k], [shp for _, shp in PACKED]) for k in range(N_CHIPS)]
    for a, (name, _) in enumerate(PACKED):
        wt[name] = jnp.concatenate([per_chip[k][a] for k in range(N_CHIPS)], axis=-1)
    for name in ("even_pre_g", "even_a_ln_g", "even_a_ln_b", "even_mem_g", "even_post_g"):
        wt[name] = w[name]

    tril = jnp.tril(jnp.ones((CH, CH), dtype=bool))
    wcat = jnp.where(tril[None], w["even_a_ws"][0], 0.0).transpose(1, 0, 2).reshape(CH, 4 * CH).astype(BF16)
    bsg = jnp.repeat(w["even_a_bs"][0].T, BW // 4, axis=1)
    hsel = (jnp.arange(BW)[:, None] // (BW // 4) == jnp.arange(128)[None, :]).astype(BF16)
    g4 = BW // 4
    eye = jnp.eye(4, dtype=F32)
    wbd = (w["odd_c_wgrp"][0][:, :, None, :] * eye[:, None, :, None]).reshape(BW, BW).astype(BF16)

    kv_e = _kv_fwd(mem, wt["even_mem_g"], wt["even_w_kv"], "even_kv")
    (x1, o_e, y_e), (wt["odd_w_in_t"],) = _even_fwd(
        x, p_e, kv_e, wt["even_a_ln_g"], wt["even_a_ln_b"], wcat, bsg, wt["even_b_conv"], wt["even_w_out"],
        wt["even_post_g"], rider=_GatherRider([placed["odd_w_in_t"]]))
    names = ("odd_w_out", "odd_d_pw_w", "odd_w_kv")
    (p_o, h_o), got = _in_fwd(x1, wt["odd_pre_g"], wt["odd_w_in_t"], "odd_in",
                              rider=_GatherRider([placed[n] for n in names]))
    wt.update(zip(names, got))
    kv_o = _kv_fwd(mem, wt["odd_mem_g"], wt["odd_w_kv"], "odd_kv")
    dx2, o_o, cv_o, loss = _odd_fwd(x1, p_o, kv_o, wbd, wt["odd_c_scale"], wt["odd_d_dw_w"], wt["odd_d_dw_b"],
                                    wt["odd_d_ln_g"], wt["odd_d_ln_b"], wt["odd_d_pw_w"], wt["odd_d_pw_b"],
                                    wt["odd_w_out"], wt["odd_post_g"], target)
    (dpc_o, tmpc, tmpd, do_o, y_o, g_post_o, g_cs, g_wbd, g_dww, g_dwb, g_lng_o, g_lnb_o, g_pww, g_pwb,
     dkv_o) = _odd_bwd1(dx2, o_o, cv_o, p_o, kv_o, wbd, wt["odd_c_scale"], wt["odd_d_dw_w"], wt["odd_d_dw_b"],
                        wt["odd_d_ln_g"], wt["odd_d_ln_b"], wt["odd_d_pw_w"], wt["odd_d_pw_b"], wt["odd_w_out"],
                        wt["odd_post_g"])
    dpb_o, dx1, g_pre_o = _odd_bwd2(dpc_o, tmpc, tmpd, p_o, wt["odd_d_dw_w"], wt["odd_w_in_t"], x1,
                                    wt["odd_pre_g"], dx2)
    g_win_o = _grad_tn(dpb_o, h_o, 768, rows=ODD_IN, name="odd_gw_in_b")
    g_win_o = _grad_tn(dpc_o, h_o, 1280, out=g_win_o, rows=ODD_IN, row0=3 * BW, name="odd_gw_in_c")
    g_wout_o = _grad_tn(y_o, do_o, 1024, name="odd_gw_out")
    g_wkv_o, g_memg_o = _kv_bwd(mem, wt["odd_mem_g"], wt["odd_w_kv"], dkv_o, "odd_kv_bwd")
    big_o = [_four(g) for g in (g_win_o, g_pww.astype(BF16), g_wkv_o, g_wout_o)]
    recv_o, _ = _swap_halves(big_o, "swap_halves_odd")
    sums_o = [_pair_sum(big_o[a], recv_o[a], place, "pair_sum_odd_%d" % a) for a in range(len(big_o))]
    (dp_e, do_e, g_post_e, g_lng_e, g_lnb_e, g_wcat, g_bs, g_bconv,
     dkv_e), parts_o = _even_bwd1(dx1, o_e, p_e, kv_e, wt["even_a_ln_g"], wt["even_a_ln_b"], wcat, bsg, hsel,
                                  wt["even_b_conv"], wt["even_w_out"], wt["even_post_g"],
                                  rider=_ExchangeRider(sums_o))
    halves_o = [_chip_sum(sums_o[a], parts_o[a], place, "chip_sum_odd_%d" % a) for a in range(len(big_o))]
    g_wout_e = _grad_tn(y_e, do_e, 1024, name="even_gw_out")
    g_wkv_e, g_memg_e = _kv_bwd(mem, wt["even_mem_g"], wt["even_w_kv"], dkv_e, "even_kv_bwd")
    big_x = [_four(g) for g in (g_wkv_e, g_wout_e)]
    recv_x, full_o = _swap_halves(big_x, "swap_halves_kv_out", share=halves_o)
    odd_names = ("odd_w_in", "odd_d_pw_w", "odd_w_kv", "odd_w_out")
    g_odd = [f.reshape(f.shape[1] * 2, f.shape[2]) for f in full_o]
    as_kept = [(lambda t: t.T) if n == "odd_w_in" else (lambda t: t) for n in odd_names]
    res = _adamw_sparsecore([f(w[n][0]) for f, n in zip(as_kept, odd_names)], g_odd,
                            [f(mom[n][0]) for f, n in zip(as_kept, odd_names)],
                            [f(var[n][0]) for f, n in zip(as_kept, odd_names)], "adamw_odd_sparsecore")
    upd_odd = {n: tuple(f(t) for t in r_) for f, n, r_ in zip(as_kept, odd_names, res)}
    sums_x = [_pair_sum(big_x[a], recv_x[a], place, "pair_sum_kv_out_%d" % a) for a in range(len(big_x))]
    g_win_e, parts_x = _grad_tn(dp_e, h_e, 1280, name="even_gw_in", rider=_ExchangeRider(sums_x))
    halves_x = [_chip_sum(sums_x[a], parts_x[a], place, "chip_sum_kv_out_%d" % a) for a in range(len(big_x))]
    big_e = [_four(g_win_e)]
    recv_e, _ = _swap_halves(big_e, "swap_halves_even")
    sums_e = [_pair_sum(big_e[0], recv_e[0], place, "pair_sum_even_w_in")]
    (dx0, g_pre_e), parts_e = _even_bwd2(dp_e, wt["even_w_in_t"], x, wt["even_pre_g"], dx1,
                                         rider=_ExchangeRider(sums_e))
    halves_e = [_chip_sum(sums_e[0], parts_e[0], place, "chip_sum_even_w_in")]

    g_aws = jnp.where(tril[None], g_wcat.reshape(CH, 4, CH).transpose(1, 0, 2), 0.0)
    g_wgrp = jnp.stack([lax.dynamic_slice(g_wbd, (g * g4, g * g4), (g4, g4)) for g in range(4)])
    small = {
        "even_pre_g": g_pre_e, "even_a_ln_g": g_lng_e, "even_a_ln_b": g_lnb_e, "even_a_ws": g_aws,
        "even_a_bs": g_bs[:, 0:4].T, "even_b_conv": g_bconv[0:3], "even_mem_g": g_memg_e, "even_post_g": g_post_e,
        "odd_pre_g": g_pre_o, "odd_c_wgrp": g_wgrp, "odd_c_scale": g_cs, "odd_d_dw_w": g_dww.reshape(CONF, 8, BW).sum(axis=1),
        "odd_d_dw_b": g_dwb, "odd_d_ln_g": g_lng_o, "odd_d_ln_b": g_lnb_o, "odd_d_pw_b": g_pwb,
        "odd_mem_g": g_memg_o, "odd_post_g": g_post_o,
    }
    small_names = SMALL_EVEN + SMALL_ODD
    small_pack = _flat_pack([small[n] for n in small_names] + [loss[0, 0].reshape(1)], SMALL_ROWS)
    small_total, full = _finish_reduce(small_pack, halves_e + halves_x)
    order = ("even_w_in", "even_w_kv", "even_w_out")
    gbig = {n: full[a].reshape(full[a].shape[1] * 2, full[a].shape[2]) for a, n in enumerate(order)}
    return dx0, gbig, upd_odd, small_total.reshape(-1), [small[n].shape for n in small_names]


def kernel(x, mem, even_pre_g, even_w_in, even_a_ln_g, even_a_ln_b, even_a_ws, even_a_bs, even_b_conv, even_mem_g, even_w_kv, even_w_out, even_post_g, odd_pre_g, odd_w_in, odd_c_wgrp, odd_c_scale, odd_d_dw_w, odd_d_dw_b, odd_d_ln_g, odd_d_ln_b, odd_d_pw_w, odd_d_pw_b, odd_mem_g, odd_w_kv, odd_w_out, odd_post_g, loss_target, m_even_pre_g, m_even_w_in, m_even_a_ln_g, m_even_a_ln_b, m_even_a_ws, m_even_a_bs, m_even_b_conv, m_even_mem_g, m_even_w_kv, m_even_w_out, m_even_post_g, m_odd_pre_g, m_odd_w_in, m_odd_c_wgrp, m_odd_c_scale, m_odd_d_dw_w, m_odd_d_dw_b, m_odd_d_ln_g, m_odd_d_ln_b, m_odd_d_pw_w, m_odd_d_pw_b, m_odd_mem_g, m_odd_w_kv, m_odd_w_out, m_odd_post_g, v_even_pre_g, v_even_w_in, v_even_a_ln_g, v_even_a_ln_b, v_even_a_ws, v_even_a_bs, v_even_b_conv, v_even_mem_g, v_even_w_kv, v_even_w_out, v_even_post_g, v_odd_pre_g, v_odd_w_in, v_odd_c_wgrp, v_odd_c_scale, v_odd_d_dw_w, v_odd_d_dw_b, v_odd_d_ln_g, v_odd_d_ln_b, v_odd_d_pw_w, v_odd_d_pw_b, v_odd_mem_g, v_odd_w_kv, v_odd_w_out, v_odd_post_g):
    given = dict(locals())
    w = {n: given[n] for n in WEIGHTS}
    mom = {n: given["m_" + n] for n in WEIGHTS}
    var = {n: given["v_" + n] for n in WEIGHTS}

    x_, y_, c_ = lax.axis_index("x"), lax.axis_index("y"), lax.axis_index("c")
    chip = 2 * x_ + y_
    place = jnp.stack([c_, chip]).astype(jnp.int32)
    grad_x, gbig, upd_odd, gsmall_flat, small_shapes = _step(x[0], mem[0], loss_target[0], w, mom, var, place)

    names = SMALL_EVEN + SMALL_ODD
    grads = {}
    unpacked = _flat_unpack(gsmall_flat, small_shapes + [(1,)])
    loss = unpacked[-1][0]
    for n, g in zip(names, unpacked[:-1]):
        shard_shape = w[n].shape[1:]
        if g.shape[-1] != shard_shape[-1]:
            g = lax.dynamic_slice_in_dim(g, chip * shard_shape[-1], shard_shape[-1], axis=g.ndim - 1)
        grads[n] = g.reshape(shard_shape)

    def two_d(a):
        return a.reshape(-1, a.shape[-1])

    upd = {}
    for n in BIG:
        if n in upd_odd:
            res = upd_odd[n]
        elif n.endswith("w_in"):
            res = _adamw_big(w[n][0].T, gbig[n], mom[n][0].T, var[n][0].T, "adamw_" + n)
            res = tuple(r.T for r in res)
        else:
            res = _adamw_big(w[n][0], gbig[n], mom[n][0], var[n][0], "adamw_" + n)
        grads[n], upd[n] = res[0], res[1:]
    res = _adamw_small([two_d(w[n][0]) for n in names], [two_d(grads[n]) for n in names],
                       [two_d(mom[n][0]) for n in names], [two_d(var[n][0]) for n in names])
    for n, r in zip(names, res):
        upd[n] = r

    outs = [loss, grad_x[None]]
    outs += [grads[n].reshape(w[n].shape) for n in WEIGHTS]
    for j in range(3):
        outs += [upd[n][j].reshape(w[n].shape) for n in WEIGHTS]
    return tuple(outs)
```

```python
import jax
import jax.numpy as jnp
from jax import lax
from jax.experimental import pallas as pl
from jax.experimental.pallas import tpu as pltpu
from jax.experimental.pallas import tpu_sc as plsc

F32 = jnp.float32
BF16 = jnp.bfloat16
MESH = pl.DeviceIdType.MESH

D = 1024
N_MEM = 256
MIX = 2048
XA = 512
HD = 128
BW = 768
CH = 128
EPS = 1e-6
SCALE = HD ** -0.5
POOL_WINDOWS = (2, 4, 8, 16)
CONF = 31
EVEN_IN = 6400
ODD_IN = 4864
N_CHIPS = 4

ADAM_LR = 0.001
ADAM_B1 = 0.9
ADAM_B2 = 0.999
ADAM_EPS = 1e-08
ADAM_WD = 0.01
ADAM_STEP = 10

TS = 256
HALO = 32
VMEM_LIMIT = 56 * 1024 * 1024


def _cp(sem=None):
    return pltpu.CompilerParams(dimension_semantics=sem, vmem_limit_bytes=VMEM_LIMIT)


def _dot(a, b):
    return jnp.dot(a, b, preferred_element_type=F32)


def _dot_nt(a, b):
    return lax.dot_general(a, b, (((1,), (1,)), ((), ())), preferred_element_type=F32)


def _dot_tn(a, b):
    return lax.dot_general(a, b, (((0,), (0,)), ((), ())), preferred_element_type=F32)


def _sigmoid(x):
    return 1.0 / (1.0 + jnp.exp(-x))


def _resident(shape):
    return pl.BlockSpec(shape, lambda *_: (0,) * len(shape), pipeline_mode=pl.Buffered(1))


def _const(shape):
    return pl.BlockSpec(shape, lambda *_: (0,) * len(shape))


def _kv_fwd(mem, mem_g, wkv, name):
    def body(mem_ref, g_ref, w_ref, kv_ref):
        m = mem_ref[...]
        r = lax.rsqrt(jnp.mean(m * m, axis=-1, keepdims=True) + EPS)
        mn = (m * r * g_ref[...]).astype(BF16)
        kv_ref[...] = _dot(mn, w_ref[...]).astype(BF16)

    return pl.pallas_call(body, out_shape=jax.ShapeDtypeStruct((N_MEM, D), BF16), name=name,
                          compiler_params=_cp())(mem, mem_g, wkv)


def _kv_bwd(mem, mem_g, wkv, dkv, name):
    def body(mem_ref, g_ref, w_ref, dkv_ref, dw_ref, dg_ref):
        m = mem_ref[...]
        r = lax.rsqrt(jnp.mean(m * m, axis=-1, keepdims=True) + EPS)
        mh = m * r
        mn = (mh * g_ref[...]).astype(BF16)
        dkv = dkv_ref[...].astype(BF16)
        dw_ref[...] = _dot_tn(mn, dkv).astype(BF16)
        dmn = _dot_nt(dkv, w_ref[...])
        dg_ref[...] = jnp.sum(dmn * mh, axis=0, keepdims=True)

    return pl.pallas_call(body, out_shape=(jax.ShapeDtypeStruct((D, D), BF16), jax.ShapeDtypeStruct((1, D), F32)),
                          name=name, compiler_params=_cp())(mem, mem_g, wkv, dkv)


def _host_call(body, *, grid, name, out_shape, in_specs, out_specs, args, scratch_shapes=(), aliases=None,
               rider=None):
    sem = ("arbitrary",) * len(grid)
    aliases = dict(aliases or {})
    if rider is None:
        res = pl.pallas_call(body, grid=grid, name=name, out_shape=tuple(out_shape), in_specs=list(in_specs),
                             out_specs=tuple(out_specs), scratch_shapes=list(scratch_shapes),
                             input_output_aliases=aliases, compiler_params=_cp(sem))(*args)
        return tuple(res), ()
    n_in, n_out, n_sc = len(in_specs), len(out_specs), len(scratch_shapes)
    r_in, r_out = len(rider.inputs), len(rider.out_shapes)

    def full_body(*refs):
        host_in = refs[:n_in]
        rid_in = refs[n_in:n_in + r_in]
        host_out = refs[n_in + r_in:n_in + r_in + n_out]
        rid_out = refs[n_in + r_in + n_out:n_in + r_in + n_out + r_out]
        host_sc = refs[n_in + r_in + n_out + r_out:n_in + r_in + n_out + r_out + n_sc]
        sems = refs[n_in + r_in + n_out + r_out + n_sc:]
        first = pl.program_id(0) == 0
        last = pl.program_id(0) == grid[0] - 1
        for ax in range(1, len(grid)):
            first = jnp.logical_and(first, pl.program_id(ax) == 0)
            last = jnp.logical_and(last, pl.program_id(ax) == grid[ax] - 1)

        @pl.when(first)
        def _():
            rider.start(rid_in, rid_out, sems)

        if rider.has_mid:
            @pl.when(last)
            def _():
                rider.mid(rid_in, rid_out, sems)

        body(*host_in, *host_out, *host_sc)

        @pl.when(last)
        def _():
            rider.end(rid_in, rid_out, sems)

    aliases.update({n_in + j: n_out + k for j, k in rider.aliases.items()})
    res = pl.pallas_call(
        full_body, grid=grid, name=name, out_shape=tuple(out_shape) + tuple(rider.out_shapes),
        in_specs=list(in_specs) + _hbm_specs(r_in), out_specs=tuple(out_specs) + tuple(_hbm_specs(r_out)),
        scratch_shapes=list(scratch_shapes) + list(rider.sems), input_output_aliases=aliases,
        compiler_params=_cp(sem),
    )(*args, *rider.inputs)
    return tuple(res[:n_out]), tuple(res[n_out:])


def _in_fwd(x, pre_g, w_t, name, rider=None):
    s, n = x.shape[0], w_t.shape[0]
    tm = min(512, s)
    nc = 256

    def body(x_ref, g_ref, w_ref, p_ref, h_ref):
        xv = x_ref[...]
        r = lax.rsqrt(jnp.mean(xv * xv, axis=-1, keepdims=True) + EPS)
        h = (xv * r * g_ref[...]).astype(BF16)
        h_ref[...] = h
        for j in range(n // nc):
            p_ref[:, j * nc:(j + 1) * nc] = _dot_nt(h, w_ref[j * nc:(j + 1) * nc, :]).astype(BF16)

    return _host_call(
        body, grid=(s // tm,), name=name, rider=rider,
        out_shape=(jax.ShapeDtypeStruct((s, n), BF16), jax.ShapeDtypeStruct((s, D), BF16)),
        in_specs=[pl.BlockSpec((tm, D), lambda i: (i, 0)), _const((1, D)), _resident((n, D))],
        out_specs=(pl.BlockSpec((tm, n), lambda i: (i, 0)), pl.BlockSpec((tm, D), lambda i: (i, 0))),
        args=(x, pre_g, w_t))


NC = 256


def _stream_tables(core, chip, n):
    nchunk = n // NC
    idx = jnp.arange(nchunk, dtype=jnp.int32)
    src = jnp.array([0, 2, 1, 3], jnp.int32)
    r = n // N_CHIPS

    def group_of(row):
        j = src[(row // r) ^ chip]
        through_sibling = ((row % r) // (r // 2) != core).astype(jnp.int32)
        return jnp.where(j == 0, 0, 2 * j - 1 + through_sibling)

    grp = jnp.maximum(group_of(idx * NC), group_of(idx * NC + NC - 1))
    order = jnp.argsort(grp * 64 + idx).astype(jnp.int32)
    return order, grp[order]


def _in_fwd_streamed(x, pre_g, first, later, order, group, name):
    s, n = x.shape[0], first[0].shape[0]
    nchunk = n // NC
    rider = _GatherRider(first)
    rider2 = _GatherRider(later) if later else None
    a, m = len(first), len(later)
    tr = min(256, s)

    def body(*refs):
        order_ref, group_ref, x_ref, g_ref = refs[0:4]
        p_ref, h_ref = refs[4 + a + m:6 + a + m]
        outs = refs[6 + a + m:6 + 2 * a + m]
        outs2 = refs[6 + 2 * a + m:6 + 2 * a + 2 * m]
        wbuf, wsem, send_sems, recv_sems = refs[6 + 2 * a + 2 * m:10 + 2 * a + 2 * m]
        sems2 = refs[10 + 2 * a + 2 * m:]
        w_hbm = outs[0]
        j = pl.program_id(0)
        sems = (send_sems, recv_sems)
        grp = group_ref[j]
        new_group = jnp.logical_or(j == 0, group_ref[jnp.maximum(j - 1, 0)] != grp)
        slot = j % 2

        def fetch(step, sl):
            rows = pl.ds(pl.multiple_of(order_ref[step] * NC, NC), NC)
            return pltpu.make_async_copy(w_hbm.at[rows], wbuf.at[sl], wsem.at[sl])

        @pl.when(j == 0)
        def _():
            rider.start(None, outs, sems, peers=(0, 1))

            @pl.loop(0, s // tr)
            def _(t):
                rows = pl.ds(pl.multiple_of(t * tr, tr), tr)
                xv = x_ref[rows, :]
                r = lax.rsqrt(jnp.mean(xv * xv, axis=-1, keepdims=True) + EPS)
                h_ref[rows, :] = (xv * r * g_ref[...]).astype(BF16)

        before = jnp.where(j == 0, 0, group_ref[jnp.maximum(j - 1, 0)])

        def entering(b):
            return jnp.logical_and(before < b, b <= grp)

        for src in range(3):
            @pl.when(entering(2 * src + 1))
            def _(src=src):
                if src == 0:
                    rider.start(None, outs, sems, peers=(2,))
                rider.mid(None, outs, sems, peers=(src,))
                if src == 1 and rider2 is not None:
                    rider2.start(None, outs2, sems2)

            @pl.when(entering(2 * src + 2))
            def _(src=src):
                rider.wait_forwarded(outs, sems, peers=(src,))

        @pl.when(new_group)
        def _():
            fetch(j, slot).start()

        fetch(j, slot).wait()
        nxt = jnp.minimum(j + 1, nchunk - 1)

        @pl.when(jnp.logical_and(j + 1 < nchunk, group_ref[nxt] == grp))
        def _():
            fetch(nxt, 1 - slot).start()

        p_ref[...] = _dot_nt(h_ref[...], wbuf[slot]).astype(BF16)

        @pl.when(j == nchunk - 1)
        def _():
            rider.wait_sends(outs, sems)
            if rider2 is not None:
                rider2.mid(None, outs2, sems2)
                rider2.end(None, outs2, sems2)

    hbm = pl.BlockSpec(memory_space=pltpu.HBM)
    arrs = list(first) + list(later)
    whole = pl.BlockSpec((s, D), lambda j, o, g: (0, 0), pipeline_mode=pl.Buffered(1))
    res = pl.pallas_call(
        body, name=name,
        out_shape=(jax.ShapeDtypeStruct((s, n), BF16), jax.ShapeDtypeStruct((s, D), BF16))
        + tuple(jax.ShapeDtypeStruct(v.shape, v.dtype) for v in arrs),
        grid_spec=pltpu.PrefetchScalarGridSpec(
            num_scalar_prefetch=2, grid=(nchunk,),
            in_specs=[whole, pl.BlockSpec((1, D), lambda j, o, g: (0, 0))] + [hbm] * (a + m),
            out_specs=(pl.BlockSpec((s, NC), lambda j, o, g: (0, o[j])),
                       pl.BlockSpec((s, D), lambda j, o, g: (0, 0))) + (hbm,) * (a + m),
            scratch_shapes=[pltpu.VMEM((2, NC, D), BF16), pltpu.SemaphoreType.DMA((2,))] + list(rider.sems)
            + (list(rider2.sems) if rider2 is not None else [])),
        input_output_aliases={4 + v: 2 + v for v in range(a + m)},
        compiler_params=_cp(("arbitrary",)),
    )(order, group, x, pre_g, *arrs)
    return res[0], res[1], tuple(res[2:2 + a]), tuple(res[2 + a:])


def _xattn_fwd(q, kv_ref):
    outs, probs = [], []
    for h in range(XA // HD):
        qh = q[:, h * HD:(h + 1) * HD]
        kh = kv_ref[:, h * HD:(h + 1) * HD]
        vh = kv_ref[:, XA + h * HD:XA + (h + 1) * HD]
        sc = _dot_nt(qh, kh) * SCALE
        e = jnp.exp(sc - jnp.max(sc, axis=-1, keepdims=True))
        pr = e / jnp.sum(e, axis=-1, keepdims=True)
        outs.append(_dot(pr.astype(BF16), vh))
        probs.append(pr)
    return jnp.concatenate(outs, axis=-1), probs


def _xattn_bwd(dyx, q, probs, kv_ref, dkv_ref):
    dqs = []
    for h in range(XA // HD):
        qh = q[:, h * HD:(h + 1) * HD]
        kh = kv_ref[:, h * HD:(h + 1) * HD]
        vh = kv_ref[:, XA + h * HD:XA + (h + 1) * HD]
        dy = dyx[:, h * HD:(h + 1) * HD].astype(BF16)
        pr = probs[h]
        dp = _dot_nt(dy, vh)
        ds = (pr * (dp - jnp.sum(dp * pr, axis=-1, keepdims=True))).astype(BF16)
        dqs.append(_dot(ds, kh) * SCALE)
        dkv_ref[:, h * HD:(h + 1) * HD] += _dot_tn(ds, qh) * SCALE
        dkv_ref[:, XA + h * HD:XA + (h + 1) * HD] += _dot_tn(pr.astype(BF16), dy)
    return jnp.concatenate(dqs, axis=-1)


def _layer_norm_fwd(v, g, b):
    mu = jnp.mean(v, axis=-1, keepdims=True)
    vc = v - mu
    rstd = lax.rsqrt(jnp.mean(vc * vc, axis=-1, keepdims=True) + EPS)
    vhat = vc * rstd
    return vhat * g + b, vhat, rstd


def _layer_norm_bwd(dy, vhat, rstd, g):
    dvh = dy * g
    return rstd * (dvh - jnp.mean(dvh, axis=-1, keepdims=True) - vhat * jnp.mean(dvh * vhat, axis=-1, keepdims=True))


def _head_masks():
    col = lax.broadcasted_iota(jnp.int32, (1, BW), 1)
    return [(col >= h * (BW // 4)) & (col < (h + 1) * (BW // 4)) for h in range(4)]


def _halo_prev(nblk_per_tile):
    return lambda i: (jnp.maximum(i * nblk_per_tile - 1, 0), 0)


def _row_ids(i, t):
    return i * t + lax.broadcasted_iota(jnp.int32, (t, 1), 0)


def _even_mix(i, p_ref, ph_ref, ln_g, ln_b, wcat_ref, bsg_ref, bconv_ref, wbuf):
    t = p_ref.shape[0]
    u = p_ref[:, 0:BW].astype(F32)
    v = p_ref[:, BW:2 * BW].astype(F32)
    bg = p_ref[:, 2 * BW:3 * BW].astype(F32)
    cg = p_ref[:, 3 * BW:4 * BW].astype(F32)
    xin = p_ref[:, 4 * BW:5 * BW].astype(F32)
    vn, vhat, rstd = _layer_norm_fwd(v, ln_g, ln_b)
    masks = _head_masks()
    sgs, vsts = [], []
    for n in range(t // CH):
        vn_c = vn[n * CH:(n + 1) * CH]
        vst = jnp.concatenate([jnp.where(m, vn_c, 0.0) for m in masks], axis=0).astype(BF16)
        sgs.append(_dot(wcat_ref[...], vst) + bsg_ref[...])
        vsts.append(vst)
    sg = jnp.concatenate(sgs, axis=0)
    ya = u * sg
    w_halo = ph_ref[:, 3 * BW:4 * BW].astype(F32) * ph_ref[:, 4 * BW:5 * BW].astype(F32)
    wbuf[0:HALO, :] = jnp.where(i > 0, w_halo, 0.0)
    wbuf[HALO:HALO + t, :] = cg * xin
    conv = (bconv_ref[0:1, :] * wbuf[pl.ds(HALO - 2, t), :] + bconv_ref[1:2, :] * wbuf[pl.ds(HALO - 1, t), :]
            + bconv_ref[2:3, :] * wbuf[pl.ds(HALO, t), :])
    yb = bg * conv
    return dict(u=u, bg=bg, cg=cg, xin=xin, vhat=vhat, rstd=rstd, sg=sg, vsts=vsts, conv=conv, ya=ya, yb=yb,
                masks=masks)


def _pool_select(vals):
    col = lax.broadcasted_iota(jnp.int32, (1, BW), 1)
    g = BW // 4
    return jnp.where(col < g, vals[0], jnp.where(col < 2 * g, vals[1], jnp.where(col < 3 * g, vals[2], vals[3])))


def _inv_counts(i, t):
    rows = _row_ids(i, t) + 1
    return [1.0 / jnp.minimum(rows, w).astype(F32) for w in POOL_WINDOWS]


def _band_matrices(t, forward):
    j = jnp.arange(t)[:, None]
    r = jnp.arange(HALO + t)[None, :]
    if forward:
        return jnp.stack([(r >= j) & (r < j + w) for w in POOL_WINDOWS]).astype(BF16)
    return jnp.stack([(r <= HALO + j) & (r > HALO + j - w) for w in POOL_WINDOWS]).astype(BF16)


SHIFT_ROWS = HALO + TS - 8


def _shifted_copies(buf, sh):
    for b in range(1, 8):
        sh[b - 1] = buf[pl.ds(b, SHIFT_ROWS), :]


def _rows_at(buf, sh, off, t):
    a, b = divmod(off, 8)
    return buf[pl.ds(8 * a, t), :] if b == 0 else sh[b - 1, pl.ds(8 * a, t), :]


def _tap_sums(d_ref, buf, sh, base, out_ref):
    t = d_ref.shape[0]
    group = 4
    for k0 in range(0, CONF, group):
        taps = list(range(k0, min(k0 + group, CONF)))

        def step(r, accs, taps=taps):
            row = pl.multiple_of(r * 8, 8)
            d = d_ref[pl.ds(row, 8), :]
            new = []
            for acc, k in zip(accs, taps):
                a, b = divmod(base + k, 8)
                src = buf[pl.ds(row + 8 * a, 8), :] if b == 0 else sh[b - 1, pl.ds(row + 8 * a, 8), :]
                new.append(acc + d * src)
            return tuple(new)

        accs = lax.fori_loop(0, t // 8, step, tuple(jnp.zeros((8, BW), F32) for _ in taps), unroll=2)
        for acc, k in zip(accs, taps):
            out_ref[8 * k:8 * k + 8, :] += acc


def _odd_mix(i, p_ref, ph_ref, bands_ref, wbd_ref, cscale, dww_ref, dwb, ln_g, ln_b, pww_ref, pwb, gbuf, gsh,
             cv=None):
    t = p_ref.shape[0]
    zc_bf = p_ref[:, 0:BW]
    zc = zc_bf.astype(F32)
    ga = p_ref[:, BW:2 * BW].astype(F32)
    gb = p_ref[:, 2 * BW:3 * BW].astype(F32)
    zh = ph_ref[:, 0:BW]
    zcat = jnp.concatenate([jnp.where(i > 0, zh, jnp.zeros_like(zh)), zc_bf], axis=0)
    inv = _inv_counts(i, t)
    pooled = _pool_select([_dot(bands_ref[w], zcat) * inv[w] for w in range(len(POOL_WINDOWS))]) - zc
    pooled_bf = pooled.astype(BF16)
    pre = _dot(pooled_bf, wbd_ref[...])
    yc = pre * cscale
    sgb = _sigmoid(gb)
    z = ga * sgb
    gh_a = ph_ref[:, BW:2 * BW].astype(F32)
    gh_b = ph_ref[:, 2 * BW:3 * BW].astype(F32)
    gbuf[0:HALO, :] = jnp.where(i > 0, gh_a * _sigmoid(gh_b), 0.0)
    gbuf[HALO:HALO + t, :] = z
    _shifted_copies(gbuf, gsh)
    if cv is None:
        cv = dwb + dww_ref[CONF - 1:CONF, :] * z
        for k in range(CONF - 1):
            cv = cv + dww_ref[k:k + 1, :] * _rows_at(gbuf, gsh, HALO - (CONF - 1) + k, t)
    zl, zhat, rstd = _layer_norm_fwd(cv, ln_g, ln_b)
    szl = _sigmoid(zl)
    zs = (zl * szl).astype(BF16)
    yd = _dot(zs, pww_ref[...]) + pwb
    return dict(ga=ga, sgb=sgb, pooled_bf=pooled_bf, pre=pre, yc=yc, zhat=zhat, rstd=rstd, zl=zl, szl=szl,
                zs=zs, yd=yd, inv=inv, cv=cv)


def _post_norm(o, post_g):
    r = lax.rsqrt(jnp.mean(o * o, axis=-1, keepdims=True) + EPS)
    return o * r, r


def _gate_out(y_a, y_b, y_x, gate, wout_ref):
    sgt = _sigmoid(gate)
    sgate = gate * sgt
    ys = [(y_a * sgate[:, 0:BW]).astype(BF16), (y_b * sgate[:, BW:2 * BW]).astype(BF16),
          (y_x * sgate[:, 2 * BW:MIX]).astype(BF16)]
    o = (_dot(ys[0], wout_ref[0:BW, :]) + _dot(ys[1], wout_ref[BW:2 * BW, :]) + _dot(ys[2], wout_ref[2 * BW:MIX, :]))
    return o, ys, sgt, sgate


def _tile_specs(s, n):
    nh = TS // HALO
    return pl.BlockSpec((TS, n), lambda i: (i, 0)), pl.BlockSpec((HALO, n), _halo_prev(nh))


def _even_fwd(x, p, kv, ln_g, ln_b, wcat, bsg, bconv, wout, post_g, rider=None):
    s = x.shape[0]

    def body(x_ref, p_ref, ph_ref, kv_ref, lng, lnb, wcat_ref, bsg_ref, bconv_ref, wout_ref, pg, x1_ref, o_ref,
             y_ref, wbuf):
        i = pl.program_id(0)
        mx = _even_mix(i, p_ref, ph_ref, lng[...], lnb[...], wcat_ref, bsg_ref, bconv_ref, wbuf)
        yx, _ = _xattn_fwd(p_ref[:, 5 * BW:5 * BW + XA], kv_ref)
        gate = p_ref[:, 5 * BW + XA:EVEN_IN].astype(F32)
        o, ys, _, _ = _gate_out(mx["ya"], mx["yb"], yx, gate, wout_ref)
        y_ref[:, 0:BW] = ys[0]
        y_ref[:, BW:2 * BW] = ys[1]
        y_ref[:, 2 * BW:MIX] = ys[2]
        n, _ = _post_norm(o, pg[...])
        o_ref[...] = o
        x1_ref[...] = x_ref[...] + n * pg[...]

    tile, halo = _tile_specs(s, EVEN_IN)
    row = pl.BlockSpec((TS, D), lambda i: (i, 0))
    return _host_call(
        body, grid=(s // TS,), name="even_fwd", rider=rider,
        out_shape=(jax.ShapeDtypeStruct((s, D), F32), jax.ShapeDtypeStruct((s, D), F32),
                   jax.ShapeDtypeStruct((s, MIX), BF16)),
        in_specs=[row, tile, halo, _const((N_MEM, D)), _const((1, BW)), _const((1, BW)), _const((CH, 4 * CH)),
                  _const((CH, BW)), _const((3, BW)), _resident((MIX, D)), _const((1, D))],
        out_specs=(row, row, pl.BlockSpec((TS, MIX), lambda i: (i, 0))),
        scratch_shapes=[pltpu.VMEM((HALO + TS, BW), F32)],
        args=(x, p, p, kv, ln_g, ln_b, wcat, bsg, bconv, wout, post_g))


def _odd_fwd(x1, p, kv, wbd, cscale, dww, dwb, ln_g, ln_b, pww, pwb, wout, post_g, target):
    s = x1.shape[0]

    def body(x_ref, p_ref, ph_ref, kv_ref, bands_ref, wbd_ref, cs, dww_ref, dwb_ref, lng, lnb, pww_ref, pwb_ref,
             wout_ref, pg, tgt_ref, dx_ref, o_ref, cv_ref, loss_ref, gbuf, gsh):
        i = pl.program_id(0)
        mx = _odd_mix(i, p_ref, ph_ref, bands_ref, wbd_ref, cs[...], dww_ref, dwb_ref[...], lng[...], lnb[...],
                      pww_ref, pwb_ref[...], gbuf, gsh)
        cv_ref[...] = mx["cv"]
        yx, _ = _xattn_fwd(p_ref[:, 3 * BW:3 * BW + XA], kv_ref)
        gate = p_ref[:, 3 * BW + XA:ODD_IN].astype(F32)
        o, _, _, _ = _gate_out(mx["yc"], mx["yd"], yx, gate, wout_ref)
        n, _ = _post_norm(o, pg[...])
        o_ref[...] = o
        err = x_ref[...] + n * pg[...] - tgt_ref[...]
        dx_ref[...] = err * (1.0 / D)

        @pl.when(i == 0)
        def _():
            loss_ref[...] = jnp.zeros_like(loss_ref)

        loss_ref[...] += 0.5 * jnp.sum(jnp.sum(err * err, axis=-1, keepdims=True) * (1.0 / D), axis=0, keepdims=True)

    tile, halo = _tile_specs(s, ODD_IN)
    row = pl.BlockSpec((TS, D), lambda i: (i, 0))
    vec = _const((1, BW))
    return pl.pallas_call(
        body, grid=(s // TS,), name="odd_fwd",
        out_shape=(jax.ShapeDtypeStruct((s, D), F32), jax.ShapeDtypeStruct((s, D), F32),
                   jax.ShapeDtypeStruct((s, BW), F32), jax.ShapeDtypeStruct((8, 128), F32)),
        in_specs=[row, tile, halo, _const((N_MEM, D)), _const((4, TS, HALO + TS)), _const((BW, BW)), vec,
                  _const((CONF, BW)), vec, vec, vec, _const((BW, BW)), vec, _resident((MIX, D)), _const((1, D)), row],
        out_specs=(row, row, pl.BlockSpec((TS, BW), lambda i: (i, 0)), _const((8, 128))),
        scratch_shapes=[pltpu.VMEM((HALO + TS, BW), F32), pltpu.VMEM((7, SHIFT_ROWS, BW), F32)],
        compiler_params=_cp(("arbitrary",)),
    )(x1, p, p, kv, _band_matrices(TS, False), wbd, cscale, dww, dwb, ln_g, ln_b, pww, pwb, wout, post_g, target)


def _acc_init(i, refs):
    @pl.when(i == 0)
    def _():
        for r in refs:
            r[...] = jnp.zeros_like(r)


def _post_norm_bwd(dx, o, pg, dpg_ref):
    n, r = _post_norm(o, pg)
    dpg_ref[...] += jnp.sum(dx * n, axis=0, keepdims=True)
    dn = dx * pg
    return (r * (dn - n * jnp.mean(dn * n, axis=-1, keepdims=True))).astype(BF16)


def _gate_bwd(do, wout_ref, ys_f32, gate, y_ref):
    dy = _dot_nt(do, wout_ref[...])
    sgt = _sigmoid(gate)
    sgate = gate * sgt
    dsilu = sgt * (1.0 + gate * (1.0 - sgt))
    offs = (0, BW, 2 * BW, MIX)
    dys, dgs = [], []
    for j, yv in enumerate(ys_f32):
        a, b = offs[j], offs[j + 1]
        if y_ref is not None:
            y_ref[:, a:b] = (yv * sgate[:, a:b]).astype(BF16)
        dys.append(dy[:, a:b] * sgate[:, a:b])
        dgs.append(dy[:, a:b] * yv * dsilu[:, a:b])
    return dys, jnp.concatenate(dgs, axis=-1)


NEXT = 16


def _even_bwd1(dx, o, p, kv, ln_g, ln_b, wcat, bsg, hsel, bconv, wout, post_g, rider=None):
    s = dx.shape[0]
    nt = s // TS

    def body(dx_ref, o_ref, p_ref, ph_ref, dxn_ref, on_ref, pn_ref, kv_ref, lng, lnb, wcat_ref, bsg_ref, hsel_ref,
             bconv_ref, wout_ref, pg,
             dp_ref, do_ref, dpg_ref, dlng_ref, dlnb_ref, dwcat_ref, dbs_ref, dbconv_ref, dkv_ref, wbuf, dbuf):
        i = pl.program_id(0)
        _acc_init(i, (dpg_ref, dlng_ref, dlnb_ref, dwcat_ref, dbs_ref, dbconv_ref, dkv_ref))
        mx = _even_mix(i, p_ref, ph_ref, lng[...], lnb[...], wcat_ref, bsg_ref, bconv_ref, wbuf)
        q = p_ref[:, 5 * BW:5 * BW + XA]
        yx, probs = _xattn_fwd(q, kv_ref)
        gate = p_ref[:, 5 * BW + XA:EVEN_IN].astype(F32)
        do = _post_norm_bwd(dx_ref[...], o_ref[...], pg[...], dpg_ref)
        do_ref[...] = do
        (dya, dyb, dyx), dgate = _gate_bwd(do, wout_ref, (mx["ya"], mx["yb"], yx), gate, None)
        dp_ref[:, 0:BW] = (dya * mx["sg"]).astype(BF16)
        dsg = (dya * mx["u"]).astype(BF16)
        dvns = []
        for n in range(TS // CH):
            dsg_c = dsg[n * CH:(n + 1) * CH]
            dvst = _dot_tn(wcat_ref[...], dsg_c)
            dvn_c = jnp.where(mx["masks"][0], dvst[0:CH], 0.0)
            for h in range(1, 4):
                dvn_c = dvn_c + jnp.where(mx["masks"][h], dvst[h * CH:(h + 1) * CH], 0.0)
            dvns.append(dvn_c)
            dwcat_ref[...] += _dot_nt(dsg_c, mx["vsts"][n])
            dbs_ref[...] += _dot(dsg_c, hsel_ref[...])
        dvn = jnp.concatenate(dvns, axis=0)
        dlng_ref[...] += jnp.sum(dvn * mx["vhat"], axis=0, keepdims=True)
        dlnb_ref[...] += jnp.sum(dvn, axis=0, keepdims=True)
        dp_ref[:, BW:2 * BW] = _layer_norm_bwd(dvn, mx["vhat"], mx["rstd"], lng[...]).astype(BF16)
        dp_ref[:, 2 * BW:3 * BW] = (dyb * mx["conv"]).astype(BF16)
        dconv = dyb * mx["bg"]
        for k in range(3):
            dbconv_ref[k:k + 1, :] += jnp.sum(dconv * wbuf[pl.ds(HALO - 2 + k, TS), :], axis=0, keepdims=True)
        n_n, r_n = _post_norm(on_ref[...], pg[...])
        dn_n = dxn_ref[...] * pg[...]
        do_n = (r_n * (dn_n - n_n * jnp.mean(dn_n * n_n, axis=-1, keepdims=True))).astype(BF16)
        dy_n = _dot_nt(do_n, wout_ref[BW:2 * BW, :])
        g_n = pn_ref[:, 5 * BW + XA + BW:5 * BW + XA + 2 * BW].astype(F32)
        dconv_n = dy_n * (g_n * _sigmoid(g_n)) * pn_ref[:, 2 * BW:3 * BW].astype(F32)
        dbuf[0:TS, :] = dconv
        dbuf[TS:TS + NEXT, :] = jnp.where(i < nt - 1, dconv_n, 0.0)
        dw = (bconv_ref[2:3, :] * dconv + bconv_ref[1:2, :] * dbuf[pl.ds(1, TS), :]
              + bconv_ref[0:1, :] * dbuf[pl.ds(2, TS), :])
        dp_ref[:, 3 * BW:4 * BW] = (dw * mx["xin"]).astype(BF16)
        dp_ref[:, 4 * BW:5 * BW] = (dw * mx["cg"]).astype(BF16)
        dp_ref[:, 5 * BW:5 * BW + XA] = _xattn_bwd(dyx, q, probs, kv_ref, dkv_ref).astype(BF16)
        dp_ref[:, 5 * BW + XA:EVEN_IN] = dgate.astype(BF16)

    tile, halo = _tile_specs(s, EVEN_IN)
    row = pl.BlockSpec((TS, D), lambda i: (i, 0))
    vec = _const((1, BW))
    nxt = _halo_next(TS // NEXT, s // NEXT)

    def out(n):
        return pl.BlockSpec((TS, n), lambda i: (i, 0))

    return _host_call(
        body, grid=(nt,), name="even_bwd1", rider=rider,
        out_shape=(jax.ShapeDtypeStruct((s, EVEN_IN), BF16), jax.ShapeDtypeStruct((s, D), BF16),
                   jax.ShapeDtypeStruct((1, D), F32), jax.ShapeDtypeStruct((1, BW), F32),
                   jax.ShapeDtypeStruct((1, BW), F32), jax.ShapeDtypeStruct((CH, 4 * CH), F32),
                   jax.ShapeDtypeStruct((CH, 128), F32), jax.ShapeDtypeStruct((8, BW), F32),
                   jax.ShapeDtypeStruct((N_MEM, D), F32)),
        in_specs=[row, row, tile, halo, pl.BlockSpec((NEXT, D), nxt), pl.BlockSpec((NEXT, D), nxt),
                  pl.BlockSpec((NEXT, EVEN_IN), nxt), _const((N_MEM, D)), vec, vec, _const((CH, 4 * CH)),
                  _const((CH, BW)), _const((BW, 128)), _const((3, BW)), _resident((MIX, D)), _const((1, D))],
        out_specs=(out(EVEN_IN), out(D),
                   _const((1, D)), vec, vec, _const((CH, 4 * CH)), _const((CH, 128)), _const((8, BW)),
                   _const((N_MEM, D))),
        scratch_shapes=[pltpu.VMEM((HALO + TS, BW), F32), pltpu.VMEM((TS + NEXT, BW), F32)],
        args=(dx, o, p, p, dx, o, p, kv, ln_g, ln_b, wcat, bsg, hsel, bconv, wout, post_g))


def _odd_bwd1(dx, o, cv, p, kv, wbd, cscale, dww, dwb, ln_g, ln_b, pww, pwb, wout, post_g):
    s = dx.shape[0]

    def body(dx_ref, o_ref, cv_ref, p_ref, ph_ref, kv_ref, bands_ref, wbd_ref, cs, dww_ref, dwb_ref, lng, lnb,
             pww_ref, pwb_ref, wout_ref, pg,
             dpc_ref, tmpc_ref, tmpd_ref, do_ref, y_ref, dpg_ref, dcs_ref, dwbd_ref, ddww_ref, ddwb_ref, dlng_ref,
             dlnb_ref, dpww_ref, dpwb_ref, dkv_ref, gbuf, gsh, dcv_buf):
        i = pl.program_id(0)
        _acc_init(i, (dpg_ref, dcs_ref, dwbd_ref, ddww_ref, ddwb_ref, dlng_ref, dlnb_ref, dpww_ref, dpwb_ref,
                      dkv_ref))
        mx = _odd_mix(i, p_ref, ph_ref, bands_ref, wbd_ref, cs[...], dww_ref, dwb_ref[...], lng[...], lnb[...],
                      pww_ref, pwb_ref[...], gbuf, gsh, cv=cv_ref[...])
        q = p_ref[:, 3 * BW:3 * BW + XA]
        yx, probs = _xattn_fwd(q, kv_ref)
        gate = p_ref[:, 3 * BW + XA:ODD_IN].astype(F32)
        do = _post_norm_bwd(dx_ref[...], o_ref[...], pg[...], dpg_ref)
        do_ref[...] = do
        (dyc, dyd, dyx), dgate = _gate_bwd(do, wout_ref, (mx["yc"], mx["yd"], yx), gate, y_ref)
        dcs_ref[...] += jnp.sum(dyc * mx["pre"], axis=0, keepdims=True)
        dpre = (dyc * cs[...]).astype(BF16)
        dwbd_ref[...] += _dot_tn(mx["pooled_bf"], dpre)
        dpooled = _dot_nt(dpre, wbd_ref[...])
        tmpc_ref[...] = _pool_select([dpooled * c_ for c_ in mx["inv"]]).astype(BF16)
        dyd_bf = dyd.astype(BF16)
        dpwb_ref[...] += jnp.sum(dyd, axis=0, keepdims=True)
        dpww_ref[...] += _dot_tn(mx["zs"], dyd_bf)
        dzs = _dot_nt(dyd_bf, pww_ref[...])
        zl, szl = mx["zl"], mx["szl"]
        dzl = dzs * (szl * (1.0 + zl * (1.0 - szl)))
        dlng_ref[...] += jnp.sum(dzl * mx["zhat"], axis=0, keepdims=True)
        dlnb_ref[...] += jnp.sum(dzl, axis=0, keepdims=True)
        dcv = _layer_norm_bwd(dzl, mx["zhat"], mx["rstd"], lng[...])
        tmpd_ref[...] = dcv.astype(BF16)
        ddwb_ref[...] += jnp.sum(dcv, axis=0, keepdims=True)
        dcv_buf[...] = dcv
        _tap_sums(dcv_buf, gbuf, gsh, HALO - (CONF - 1), ddww_ref)
        dpc_ref[:, 0:XA] = _xattn_bwd(dyx, q, probs, kv_ref, dkv_ref).astype(BF16)
        dpc_ref[:, XA:XA + MIX] = dgate.astype(BF16)

    tile, halo = _tile_specs(s, ODD_IN)
    row = pl.BlockSpec((TS, D), lambda i: (i, 0))
    vec = _const((1, BW))

    def out(n):
        return pl.BlockSpec((TS, n), lambda i: (i, 0))

    return pl.pallas_call(
        body, grid=(s // TS,), name="odd_bwd1",
        out_shape=(jax.ShapeDtypeStruct((s, XA + MIX), BF16), jax.ShapeDtypeStruct((s, BW), BF16),
                   jax.ShapeDtypeStruct((s, BW), BF16), jax.ShapeDtypeStruct((s, D), BF16),
                   jax.ShapeDtypeStruct((s, MIX), BF16),
                   jax.ShapeDtypeStruct((1, D), F32), jax.ShapeDtypeStruct((1, BW), F32),
                   jax.ShapeDtypeStruct((BW, BW), F32), jax.ShapeDtypeStruct((8 * CONF, BW), F32),
                   jax.ShapeDtypeStruct((1, BW), F32), jax.ShapeDtypeStruct((1, BW), F32),
                   jax.ShapeDtypeStruct((1, BW), F32), jax.ShapeDtypeStruct((BW, BW), F32),
                   jax.ShapeDtypeStruct((1, BW), F32), jax.ShapeDtypeStruct((N_MEM, D), F32)),
        in_specs=[row, row, out(BW), tile, halo, _const((N_MEM, D)), _const((4, TS, HALO + TS)), _const((BW, BW)), vec,
                  _const((CONF, BW)), vec, vec, vec, _const((BW, BW)), vec, _resident((MIX, D)), _const((1, D))],
        out_specs=(out(XA + MIX), out(BW), out(BW), out(D), out(MIX),
                   _const((1, D)), vec, _const((BW, BW)), _const((8 * CONF, BW)), vec, vec, vec, _const((BW, BW)), vec,
                   _const((N_MEM, D))),
        scratch_shapes=[pltpu.VMEM((HALO + TS, BW), F32), pltpu.VMEM((7, SHIFT_ROWS, BW), F32),
                        pltpu.VMEM((TS, BW), F32)],
        compiler_params=_cp(("arbitrary",)),
    )(dx, o, cv, p, p, kv, _band_matrices(TS, False), wbd, cscale, dww, dwb, ln_g, ln_b, pww, pwb, wout, post_g)


def _halo_next(nblk_per_tile, nblk):
    return lambda i: (jnp.minimum((i + 1) * nblk_per_tile, nblk - 1), 0)


def _pre_norm_bwd(dh, x, pre_g, dres, dpre_ref):
    r = lax.rsqrt(jnp.mean(x * x, axis=-1, keepdims=True) + EPS)
    xh = x * r
    dpre_ref[...] += jnp.sum(dh * xh, axis=0, keepdims=True)
    dxh = dh * pre_g
    return dres + r * (dxh - xh * jnp.mean(dxh * xh, axis=-1, keepdims=True))


def _even_bwd2(dp, w_t, x, pre_g, dres, rider=None):
    s = x.shape[0]
    tm = min(512, s)

    def body(dp_ref, w_ref, x_ref, pg, dres_ref, dx_ref, dpre_ref):
        _acc_init(pl.program_id(0), (dpre_ref,))
        dh = _dot(dp_ref[...], w_ref[...])
        dx_ref[...] = _pre_norm_bwd(dh, x_ref[...], pg[...], dres_ref[...], dpre_ref)

    row = pl.BlockSpec((tm, D), lambda i: (i, 0))
    return _host_call(
        body, grid=(s // tm,), name="even_bwd2", rider=rider,
        out_shape=(jax.ShapeDtypeStruct((s, D), F32), jax.ShapeDtypeStruct((1, D), F32)),
        in_specs=[pl.BlockSpec((tm, EVEN_IN), lambda i: (i, 0)), _resident((EVEN_IN, D)), row, _const((1, D)), row],
        out_specs=(row, _const((1, D))),
        args=(dp, w_t, x, pre_g, dres))


def _odd_bwd2(dpc, tmpc, tmpd, p, dww, w_t, x, pre_g, dres):
    s = x.shape[0]
    nt = s // TS

    def body(dpc_ref, tc_ref, tch_ref, td_ref, tdh_ref, ga_ref, gb_ref, bands_ref, dww_ref, w_ref, x_ref, pg,
             dres_ref, dpb_ref, dx_ref, dpre_ref, dbuf, dsh):
        i = pl.program_id(0)
        _acc_init(i, (dpre_ref,))
        more = i < nt - 1
        e_bf = tc_ref[...]
        eh = tch_ref[...]
        ecat = jnp.concatenate([e_bf, jnp.where(more, eh, jnp.zeros_like(eh))], axis=0)
        dbuf[0:TS, :] = td_ref[...].astype(F32)
        dbuf[TS:TS + HALO, :] = jnp.where(more, tdh_ref[...].astype(F32), 0.0)
        sums = [_dot(bands_ref[w], ecat) for w in range(len(POOL_WINDOWS))]
        rows = _row_ids(i, TS) + 1
        cnt = _pool_select([jnp.minimum(rows, w).astype(F32) for w in POOL_WINDOWS])
        dzc = (_pool_select(sums) - e_bf.astype(F32) * cnt).astype(BF16)
        _shifted_copies(dbuf, dsh)
        dz = dww_ref[CONF - 1:CONF, :] * dbuf[pl.ds(0, TS), :]
        for sft in range(1, CONF):
            dz = dz + dww_ref[CONF - 1 - sft:CONF - sft, :] * _rows_at(dbuf, dsh, sft, TS)
        ga = ga_ref[...].astype(F32)
        sgb = _sigmoid(gb_ref[...].astype(F32))
        dga = (dz * sgb).astype(BF16)
        dgb = (dz * ga * sgb * (1.0 - sgb)).astype(BF16)
        dpb_ref[:, 0:BW] = dzc
        dpb_ref[:, BW:2 * BW] = dga
        dpb_ref[:, 2 * BW:3 * BW] = dgb
        dh = (_dot(dzc, w_ref[0:BW, :]) + _dot(dga, w_ref[BW:2 * BW, :]) + _dot(dgb, w_ref[2 * BW:3 * BW, :])
              + _dot(dpc_ref[...], w_ref[3 * BW:ODD_IN, :]))
        dx_ref[...] = _pre_norm_bwd(dh, x_ref[...], pg[...], dres_ref[...], dpre_ref)

    row = pl.BlockSpec((TS, D), lambda i: (i, 0))

    def tile(n, j=0):
        return pl.BlockSpec((TS, n), lambda i: (i, j))

    nxt = pl.BlockSpec((HALO, BW), _halo_next(TS // HALO, s // HALO))
    return pl.pallas_call(
        body, grid=(nt,), name="odd_bwd2",
        out_shape=(jax.ShapeDtypeStruct((s, 3 * BW), BF16), jax.ShapeDtypeStruct((s, D), F32),
                   jax.ShapeDtypeStruct((1, D), F32)),
        in_specs=[tile(XA + MIX), tile(BW), nxt, tile(BW), nxt, tile(BW, 1), tile(BW, 2), _const((4, TS, HALO + TS)),
                  _const((CONF, BW)), _resident((ODD_IN, D)), row, _const((1, D)), row],
        out_specs=(tile(3 * BW), row, _const((1, D))),
        scratch_shapes=[pltpu.VMEM((TS + HALO, BW), F32), pltpu.VMEM((7, SHIFT_ROWS, BW), F32)],
        compiler_params=_cp(("arbitrary",)),
    )(dpc, tmpc, tmpc, tmpd, tmpd, p, p, _band_matrices(TS, True), dww, w_t, x, pre_g, dres)


def _grad_tn(a, b, tm, out=None, rows=None, row0=0, name="grad_tn", rider=None):
    s, m = a.shape
    n = b.shape[1]
    ts = min(2048, s)
    rows = m if rows is None else rows
    assert m % tm == 0 and s % ts == 0
    ns = s // ts
    if row0 % tm == 0:
        out_spec = pl.BlockSpec((tm, n), lambda i, k: (row0 // tm + i, 0))
    else:
        align = 16
        assert row0 % align == 0 and tm % align == 0
        out_spec = pl.BlockSpec((pl.Element(tm), pl.Element(n)),
                                lambda i, k: (pl.multiple_of(row0 + i * tm, align), 0))

    def body(*refs):
        a_ref, b_ref = refs[0], refs[1]
        o_ref, acc = refs[-2], refs[-1]
        k = pl.program_id(1)

        @pl.when(k == 0)
        def _():
            acc[...] = jnp.zeros_like(acc)

        acc[...] += _dot_tn(a_ref[...], b_ref[...])

        @pl.when(k == ns - 1)
        def _():
            o_ref[...] = acc[...].astype(BF16)

    in_specs = [pl.BlockSpec((ts, tm), lambda i, k: (k, i)), pl.BlockSpec((ts, n), lambda i, k: (k, 0))]
    args = [a, b]
    aliases = {}
    if out is not None:
        in_specs.append(pl.BlockSpec(memory_space=pltpu.HBM))
        args.append(out)
        aliases = {2: 0}
    (res,), got = _host_call(
        body, grid=(m // tm, ns), name=name, rider=rider, aliases=aliases,
        out_shape=(jax.ShapeDtypeStruct((rows, n), BF16),), in_specs=in_specs, out_specs=(out_spec,),
        scratch_shapes=[pltpu.VMEM((tm, n), F32)], args=args)
    return res if rider is None else (res, got)


def _place():
    x, y, c = lax.axis_index("x"), lax.axis_index("y"), lax.axis_index("c")
    chips = [(1 - x, y), (x, 1 - y), (1 - x, 1 - y)]
    return x, y, c, chips


def _hbm_specs(n):
    return [pl.BlockSpec(memory_space=pltpu.HBM)] * n


def _row_tile(r):
    for cand in (512, 400, 304, 256, 192, 128, 96, 16):
        if r % cand == 0:
            return cand
    raise ValueError(r)


def _place_shard(shard, place, dtype, name, after=None):
    r, cc = shard.shape
    tr = _row_tile(r)
    nt = r // tr

    def body(place_ref, s_ref, *rest):
        rest[-1][...] = s_ref[...].astype(dtype)

    in_specs = [pl.BlockSpec((tr, cc), lambda i, pr: (i, 0))]
    args = [shard]
    if after is not None:
        in_specs.append(pl.BlockSpec(after.shape, lambda i, pr: (0, 0)))
        args.append(after)
    return pl.pallas_call(
        body, name=name, out_shape=jax.ShapeDtypeStruct((N_CHIPS * r, cc), dtype),
        grid_spec=pltpu.PrefetchScalarGridSpec(
            num_scalar_prefetch=1, grid=(nt,), in_specs=in_specs,
            out_specs=pl.BlockSpec((tr, cc), lambda i, pr: (pr[1] * nt + i, 0))),
        compiler_params=_cp(("arbitrary",)),
    )(place, *args)


class _GatherRider:
    has_mid = True

    def __init__(self, fulls):
        n = len(fulls)
        self.inputs = list(fulls)
        self.out_shapes = [jax.ShapeDtypeStruct(a.shape, a.dtype) for a in fulls]
        self.aliases = {a: a for a in range(n)}
        self.sems = [pltpu.SemaphoreType.DMA((6 * n,)), pltpu.SemaphoreType.DMA((6 * n,))]
        self.block_rows = [a.shape[0] // N_CHIPS for a in fulls]

    def _ctx(self, outs, sems):
        send_sems, recv_sems = sems
        x, y, c, chips = _place()

        def rows(a, k, half):
            r = self.block_rows[a]
            return outs[a].at[pl.ds(k * r + half * (r // 2), r // 2)]

        def copy(a, j, blk, to):
            return pltpu.make_async_remote_copy(src_ref=blk, dst_ref=blk, send_sem=send_sems.at[a * 6 + j],
                                                recv_sem=recv_sems.at[a * 6 + j], device_id=to, device_id_type=MESH)

        return x, y, c, chips, rows, copy

    def start(self, ins, outs, sems, peers=(0, 1, 2)):
        x, y, c, chips, rows, copy = self._ctx(outs, sems)
        for j in peers:
            for a in range(len(outs)):
                copy(a, j, rows(a, 2 * x + y, c), (*chips[j], c)).start()

    def mid(self, ins, outs, sems, peers=(0, 1, 2)):
        x, y, c, chips, rows, copy = self._ctx(outs, sems)
        for j in peers:
            px, py = chips[j]
            for a in range(len(outs)):
                copy(a, j, rows(a, 2 * px + py, c), (px, py, c)).wait_recv()
                copy(a, 3 + j, rows(a, 2 * px + py, c), (x, y, 1 - c)).start()

    def wait_forwarded(self, outs, sems, peers=(0, 1, 2)):
        x, y, c, chips, rows, copy = self._ctx(outs, sems)
        for j in peers:
            px, py = chips[j]
            for a in range(len(outs)):
                copy(a, 3 + j, rows(a, 2 * px + py, 1 - c), (x, y, 1 - c)).wait_recv()

    def wait_sends(self, outs, sems):
        x, y, c, chips, rows, copy = self._ctx(outs, sems)
        for j, (px, py) in enumerate(chips):
            for a in range(len(outs)):
                copy(a, j, rows(a, 2 * x + y, c), (px, py, c)).wait_send()
                copy(a, 3 + j, rows(a, 2 * px + py, c), (x, y, 1 - c)).wait_send()

    def end(self, ins, outs, sems):
        self.wait_forwarded(outs, sems)
        self.wait_sends(outs, sems)


def _swap_halves(grads, name, share=()):
    n, k = len(grads), len(share)
    m = n + k

    def body(*refs):
        ins, outs = refs[:m], refs[m:2 * m]
        send_sems, recv_sems = refs[2 * m:]
        x, y, c, _ = _place()
        sibling = (x, y, 1 - c)
        cps, waits = [], []
        for a in range(m):
            if a < n:
                cp = pltpu.make_async_remote_copy(src_ref=ins[a].at[:, 1 - c], dst_ref=outs[a],
                                                  send_sem=send_sems.at[a], recv_sem=recv_sems.at[a],
                                                  device_id=sibling, device_id_type=MESH)
                waits.append(cp)
            else:
                cp = pltpu.make_async_remote_copy(src_ref=outs[a].at[c], dst_ref=outs[a].at[c],
                                                  send_sem=send_sems.at[a], recv_sem=recv_sems.at[a],
                                                  device_id=sibling, device_id_type=MESH)
                waits.append(pltpu.make_async_remote_copy(
                    src_ref=outs[a].at[1 - c], dst_ref=outs[a].at[1 - c], send_sem=send_sems.at[a],
                    recv_sem=recv_sems.at[a], device_id=sibling, device_id_type=MESH))
            cp.start()
            cps.append(cp)
        for cp in waits:
            cp.wait_recv()
        for cp in cps:
            cp.wait_send()

    outs = tuple(jax.ShapeDtypeStruct((g.shape[0],) + g.shape[2:], g.dtype) for g in grads)
    outs += tuple(jax.ShapeDtypeStruct(g.shape, g.dtype) for g in share)
    res = pl.pallas_call(
        body, name=name, out_shape=outs, in_specs=_hbm_specs(m), out_specs=tuple(_hbm_specs(m)),
        input_output_aliases={n + a: n + a for a in range(k)},
        scratch_shapes=[pltpu.SemaphoreType.DMA((m,)), pltpu.SemaphoreType.DMA((m,))],
    )(*grads, *share)
    return tuple(res[:n]), tuple(res[n:])


def _pair_sum(g, recv, place, name):
    _, _, h, cc = g.shape
    th = h

    def body(c_ref, g_ref, r_ref, o_ref):
        o_ref[...] = (g_ref[...].astype(F32) + r_ref[...].astype(F32)).astype(o_ref.dtype)

    return pl.pallas_call(
        body, name=name, out_shape=jax.ShapeDtypeStruct(recv.shape, recv.dtype),
        grid_spec=pltpu.PrefetchScalarGridSpec(
            num_scalar_prefetch=1, grid=(N_CHIPS, h // th),
            in_specs=[pl.BlockSpec((None, None, th, cc), lambda k, r, c_ref: (k, c_ref[0], r, 0)),
                      pl.BlockSpec((None, th, cc), lambda k, r, c_ref: (k, r, 0))],
            out_specs=pl.BlockSpec((None, th, cc), lambda k, r, c_ref: (k, r, 0))),
        compiler_params=_cp(("arbitrary", "arbitrary")),
    )(place, g, recv)


def _finish_reduce(pack, halves):
    rows, cc = pack.shape
    hs = rows // 2
    n = len(halves)

    def body(*refs):
        pack_ref = refs[0]
        out_ref = refs[1 + n]
        big = refs[2 + n:2 + 2 * n]
        sib_ref, parts_ref, send_sems, recv_sems, big_send, big_recv = refs[2 + 2 * n:]
        x, y, c, chips = _place()
        me_k = 2 * x + y
        sibling = (x, y, 1 - c)
        mine = pl.ds(pl.multiple_of(c * hs, hs), hs)
        theirs = pl.ds(pl.multiple_of((1 - c) * hs, hs), hs)
        shared = [pltpu.make_async_remote_copy(src_ref=big[a].at[c], dst_ref=big[a].at[c], send_sem=big_send.at[a],
                                               recv_sem=big_recv.at[a], device_id=sibling, device_id_type=MESH)
                  for a in range(n)]
        for cp in shared:
            cp.start()
        first = pltpu.make_async_remote_copy(src_ref=pack_ref, dst_ref=sib_ref, send_sem=send_sems.at[0],
                                             recv_sem=recv_sems.at[0], device_id=sibling, device_id_type=MESH)
        first.start()
        first.wait()
        parts_ref[me_k] = pack_ref[mine, :] + sib_ref[mine, :]
        cps = [pltpu.make_async_remote_copy(src_ref=parts_ref.at[me_k], dst_ref=parts_ref.at[me_k],
                                            send_sem=send_sems.at[1 + j], recv_sem=recv_sems.at[1 + j],
                                            device_id=(px, py, c), device_id_type=MESH)
               for j, (px, py) in enumerate(chips)]
        for cp in cps:
            cp.start()
        for j, (px, py) in enumerate(chips):
            pltpu.make_async_remote_copy(src_ref=parts_ref.at[2 * px + py], dst_ref=parts_ref.at[2 * px + py],
                                         send_sem=send_sems.at[1 + j], recv_sem=recv_sems.at[1 + j],
                                         device_id=(px, py, c), device_id_type=MESH).wait_recv()
        for cp in cps:
            cp.wait_send()
        out_ref[mine, :] = ((parts_ref[0] + parts_ref[1]) + parts_ref[2]) + parts_ref[3]
        last = pltpu.make_async_remote_copy(src_ref=out_ref.at[mine], dst_ref=out_ref.at[mine],
                                            send_sem=send_sems.at[4], recv_sem=recv_sems.at[4], device_id=sibling,
                                            device_id_type=MESH)
        last.start()
        pltpu.make_async_remote_copy(src_ref=out_ref.at[theirs], dst_ref=out_ref.at[theirs],
                                     send_sem=send_sems.at[4], recv_sem=recv_sems.at[4], device_id=sibling,
                                     device_id_type=MESH).wait_recv()
        last.wait_send()
        for a in range(n):
            pltpu.make_async_remote_copy(src_ref=big[a].at[1 - c], dst_ref=big[a].at[1 - c], send_sem=big_send.at[a],
                                         recv_sem=big_recv.at[a], device_id=sibling,
                                         device_id_type=MESH).wait_recv()
        for cp in shared:
            cp.wait_send()

    vmem = pl.BlockSpec(memory_space=pltpu.VMEM)
    res = pl.pallas_call(
        body, name="finish_reduce",
        out_shape=(jax.ShapeDtypeStruct(pack.shape, pack.dtype),)
        + tuple(jax.ShapeDtypeStruct(g.shape, g.dtype) for g in halves),
        in_specs=[vmem] + _hbm_specs(n), out_specs=(vmem,) + tuple(_hbm_specs(n)),
        input_output_aliases={1 + a: 1 + a for a in range(n)},
        scratch_shapes=[pltpu.VMEM((rows, cc), F32), pltpu.VMEM((N_CHIPS, hs, cc), F32),
                        pltpu.SemaphoreType.DMA((5,)), pltpu.SemaphoreType.DMA((5,)),
                        pltpu.SemaphoreType.DMA((n,)), pltpu.SemaphoreType.DMA((n,))],
        compiler_params=_cp(),
    )(pack, *halves)
    return res[0], tuple(res[1:])


class _ExchangeRider:
    has_mid = False

    def __init__(self, sums):
        self.inputs = list(sums)
        self.out_shapes = [jax.ShapeDtypeStruct((3,) + g.shape[1:], g.dtype) for g in sums]
        m = len(self.inputs)
        self.aliases = {}
        self.sems = [pltpu.SemaphoreType.DMA((3 * m,)), pltpu.SemaphoreType.DMA((3 * m,))]

    def _copies(self, ins, outs, sems):
        send_sems, recv_sems = sems
        _, _, c, chips = _place()
        return [pltpu.make_async_remote_copy(
            src_ref=ins[a].at[2 * px + py], dst_ref=outs[a].at[j], send_sem=send_sems.at[a * 3 + j],
            recv_sem=recv_sems.at[a * 3 + j], device_id=(px, py, c), device_id_type=MESH)
            for j, (px, py) in enumerate(chips) for a in range(len(ins))]

    def start(self, ins, outs, sems):
        for cp in self._copies(ins, outs, sems):
            cp.start()

    def end(self, ins, outs, sems):
        cps = self._copies(ins, outs, sems)
        for cp in cps:
            cp.wait_recv()
        for cp in cps:
            cp.wait_send()


class _ShareRider:
    has_mid = False

    def __init__(self, halves):
        n = len(halves)
        self.inputs = list(halves)
        self.out_shapes = [jax.ShapeDtypeStruct(g.shape, g.dtype) for g in halves]
        self.aliases = {a: a for a in range(n)}
        self.sems = [pltpu.SemaphoreType.DMA((n,)), pltpu.SemaphoreType.DMA((n,))]

    def _copies(self, outs, sems, half):
        send_sems, recv_sems = sems
        x, y, c, _ = _place()
        h = c if half == "mine" else 1 - c
        return [pltpu.make_async_remote_copy(src_ref=outs[a].at[h], dst_ref=outs[a].at[h], send_sem=send_sems.at[a],
                                             recv_sem=recv_sems.at[a], device_id=(x, y, 1 - c), device_id_type=MESH)
                for a in range(len(outs))]

    def start(self, ins, outs, sems):
        for cp in self._copies(outs, sems, "mine"):
            cp.start()

    def end(self, ins, outs, sems):
        for cp in self._copies(outs, sems, "theirs"):
            cp.wait_recv()
        for cp in self._copies(outs, sems, "mine"):
            cp.wait_send()


class _Riders:
    def __init__(self, riders):
        self.riders = list(riders)
        self.inputs = [a for r in self.riders for a in r.inputs]
        self.out_shapes = [s for r in self.riders for s in r.out_shapes]
        self.sems = [s for r in self.riders for s in r.sems]
        self.has_mid = any(r.has_mid for r in self.riders)
        self.aliases = {}
        i0 = o0 = 0
        for r in self.riders:
            self.aliases.update({i0 + j: o0 + k for j, k in r.aliases.items()})
            i0 += len(r.inputs)
            o0 += len(r.out_shapes)

    def _each(self, ins, outs, sems):
        i0 = o0 = s0 = 0
        for r in self.riders:
            yield (r, ins[i0:i0 + len(r.inputs)], outs[o0:o0 + len(r.out_shapes)], sems[s0:s0 + len(r.sems)])
            i0, o0, s0 = i0 + len(r.inputs), o0 + len(r.out_shapes), s0 + len(r.sems)

    def start(self, ins, outs, sems):
        for r, i, o, s in self._each(ins, outs, sems):
            r.start(i, o, s)

    def mid(self, ins, outs, sems):
        for r, i, o, s in self._each(ins, outs, sems):
            if r.has_mid:
                r.mid(i, o, s)

    def end(self, ins, outs, sems):
        for r, i, o, s in self._each(ins, outs, sems):
            r.end(i, o, s)

    def split(self, outs):
        res, o0 = [], 0
        for r in self.riders:
            res.append(tuple(outs[o0:o0 + len(r.out_shapes)]))
            o0 += len(r.out_shapes)
        return res


def _chip_sum(own, parts, place, name):
    npart, h, cc = parts.shape
    th = _row_tile(h)

    def body(place_ref, own_ref, p_ref, o_ref):
        acc = own_ref[...].astype(F32) + p_ref[0].astype(F32)
        for k in range(1, npart):
            acc = acc + p_ref[k].astype(F32)
        o_ref[...] = acc

    return pl.pallas_call(
        body, name=name, out_shape=jax.ShapeDtypeStruct((2, h, cc), F32),
        grid_spec=pltpu.PrefetchScalarGridSpec(
            num_scalar_prefetch=1, grid=(h // th,),
            in_specs=[pl.BlockSpec((None, th, cc), lambda r, pr: (pr[1], r, 0)),
                      pl.BlockSpec((npart, th, cc), lambda r, pr: (0, r, 0))],
            out_specs=pl.BlockSpec((None, th, cc), lambda r, pr: (pr[0], r, 0))),
        compiler_params=_cp(("arbitrary",)),
    )(place, own, parts)


def _adamw_math(w, g, m, v):
    m = ADAM_B1 * m + (1.0 - ADAM_B1) * g
    v = ADAM_B2 * v + (1.0 - ADAM_B2) * (g * g)
    m_hat = m / (1.0 - ADAM_B1 ** ADAM_STEP)
    v_hat = v / (1.0 - ADAM_B2 ** ADAM_STEP)
    delta = -ADAM_LR * (m_hat / (jnp.sqrt(v_hat) + ADAM_EPS) + ADAM_WD * w)
    return delta, m, v


def _adamw_big(w, g, m, v, name):
    r, cc = w.shape
    tr = min(_row_tile(r), 256) if r % 256 == 0 else _row_tile(r)

    def body(w_ref, g_ref, m_ref, v_ref, go_ref, d_ref, mo_ref, vo_ref):
        g = g_ref[...]
        d, mm, vv = _adamw_math(w_ref[...], g, m_ref[...], v_ref[...])
        go_ref[...] = g
        d_ref[...] = d
        mo_ref[...] = mm
        vo_ref[...] = vv

    blk = pl.BlockSpec((tr, cc), lambda i: (i, 0))
    sd = jax.ShapeDtypeStruct((r, cc), F32)
    return pl.pallas_call(body, grid=(r // tr,), name=name, out_shape=(sd, sd, sd, sd), in_specs=[blk] * 4,
                          out_specs=(blk, blk, blk, blk), compiler_params=_cp(("arbitrary",)))(w, g, m, v)


SC_TILES = 32


def _adamw_sparsecore(ws, gs, ms, vs, name):
    n = len(ws)
    rows_per = 8
    widths = sorted({a.shape[1] for a in ws})
    assert all(a.shape[0] % rows_per == 0 and a.shape[1] % 16 == 0 for a in ws)

    def body(*refs):
        ins, outs, bufs = refs[:4 * n], refs[4 * n:8 * n], refs[8 * n:]
        tile = lax.axis_index("sc_tile") * 2 + lax.axis_index("sc_core")
        for a in range(n):
            w_hbm, g_hbm, m_hbm, v_hbm = ins[4 * a:4 * a + 4]
            go_hbm, d_hbm, mo_hbm, vo_hbm = outs[4 * a:4 * a + 4]
            r, cc = ws[a].shape
            k = widths.index(cc)
            wb, gb, mb, vb, db = bufs[5 * k:5 * k + 5]
            groups = r // rows_per

            @pl.loop(0, -(-groups // SC_TILES))
            def _(q):
                grp = tile + q * SC_TILES

                @pl.when(grp < groups)
                def _():
                    rows = pl.ds(pl.multiple_of(grp * rows_per, rows_per), rows_per)
                    pltpu.sync_copy(w_hbm.at[rows], wb)
                    pltpu.sync_copy(g_hbm.at[rows], gb)
                    pltpu.sync_copy(m_hbm.at[rows], mb)
                    pltpu.sync_copy(v_hbm.at[rows], vb)

                    @pl.loop(0, rows_per)
                    def _(i):
                        @pl.loop(0, cc, step=16)
                        def _(j):
                            at = (i, pl.ds(j, 16))
                            d, mm, vv = _adamw_math(wb[at], gb[at], mb[at], vb[at])
                            db[at] = d
                            mb[at] = mm
                            vb[at] = vv

                    pltpu.sync_copy(gb, go_hbm.at[rows])
                    pltpu.sync_copy(db, d_hbm.at[rows])
                    pltpu.sync_copy(mb, mo_hbm.at[rows])
                    pltpu.sync_copy(vb, vo_hbm.at[rows])

    args, out_type = [], []
    for a in range(n):
        args += [ws[a], gs[a], ms[a], vs[a]]
        out_type += [jax.ShapeDtypeStruct(ws[a].shape, F32)] * 4
    res = pl.kernel(
        body, name=name, out_type=tuple(out_type),
        mesh=plsc.VectorSubcoreMesh(core_axis_name="sc_core", subcore_axis_name="sc_tile"),
        scratch_types=[pltpu.VMEM((rows_per, cc), F32) for cc in widths for _ in range(5)],
    )(*args)
    return [tuple(res[4 * a:4 * a + 4]) for a in range(n)]


def _adamw_small(ws, gs, ms, vs):
    n = len(ws)

    def body(*refs):
        for a in range(n):
            w_ref, g_ref, m_ref, v_ref = refs[4 * a:4 * a + 4]
            d_ref, mo_ref, vo_ref = refs[4 * n + 3 * a:4 * n + 3 * a + 3]
            d, mm, vv = _adamw_math(w_ref[...], g_ref[...], m_ref[...], v_ref[...])
            d_ref[...] = d
            mo_ref[...] = mm
            vo_ref[...] = vv

    args, outs = [], []
    for a in range(n):
        args += [ws[a], gs[a], ms[a], vs[a]]
        outs += [jax.ShapeDtypeStruct(ws[a].shape, F32)] * 3
    res = pl.pallas_call(body, name="adamw_small", out_shape=tuple(outs), compiler_params=_cp())(*args)
    return [res[3 * a:3 * a + 3] for a in range(n)]


def _flat_pack(arrs, rows):
    flat = jnp.concatenate([a.reshape(-1) for a in arrs])
    return jnp.pad(flat, (0, rows * D - flat.shape[0])).reshape(rows, D)


def _flat_unpack(flat, shapes):
    out, off = [], 0
    for shp in shapes:
        size = 1
        for d_ in shp:
            size *= d_
        out.append(flat[off:off + size].reshape(shp))
        off += size
    return out


SMALL_EVEN = ("even_pre_g", "even_a_ln_g", "even_a_ln_b", "even_a_ws", "even_a_bs", "even_b_conv", "even_mem_g",
              "even_post_g")
SMALL_ODD = ("odd_pre_g", "odd_c_wgrp", "odd_c_scale", "odd_d_dw_w", "odd_d_dw_b", "odd_d_ln_g", "odd_d_ln_b",
             "odd_d_pw_b", "odd_mem_g", "odd_post_g")
BIG = ("even_w_in", "even_w_kv", "even_w_out", "odd_w_in", "odd_d_pw_w", "odd_w_kv", "odd_w_out")
WEIGHTS = ("even_pre_g", "even_w_in", "even_a_ln_g", "even_a_ln_b", "even_a_ws", "even_a_bs", "even_b_conv",
           "even_mem_g", "even_w_kv", "even_w_out", "even_post_g", "odd_pre_g", "odd_w_in", "odd_c_wgrp",
           "odd_c_scale", "odd_d_dw_w", "odd_d_dw_b", "odd_d_ln_g", "odd_d_ln_b", "odd_d_pw_w", "odd_d_pw_b",
           "odd_mem_g", "odd_w_kv", "odd_w_out", "odd_post_g")
PACKED = (("even_b_conv", (3, 192)), ("odd_pre_g", (1, 256)), ("odd_c_scale", (1, 192)), ("odd_d_dw_w", (31, 192)),
          ("odd_d_dw_b", (1, 192)), ("odd_d_ln_g", (1, 192)), ("odd_d_ln_b", (1, 192)), ("odd_d_pw_b", (1, 192)),
          ("odd_mem_g", (1, 256)), ("odd_post_g", (1, 256)))
PACK_ROWS = 16
SMALL_ROWS = 256


def _four(g):
    return g.reshape(N_CHIPS, 2, g.shape[0] // (2 * N_CHIPS), g.shape[1])


def _step(x, mem, target, w, mom, var, place):
    wt = {}
    pack = _flat_pack([w[n][0] for n, _ in PACKED], PACK_ROWS)
    shards = {"even_w_in_t": w["even_w_in"][0].T, "odd_w_in_t": w["odd_w_in"][0].T, "even_w_kv": w["even_w_kv"][0],
              "odd_w_kv": w["odd_w_kv"][0], "even_w_out": w["even_w_out"][0], "odd_w_out": w["odd_w_out"][0],
              "odd_d_pw_w": w["odd_d_pw_w"][0]}
    placed = {n: _place_shard(shards[n], place, BF16, "place_" + n) for n in ("even_w_in_t", "even_w_kv", "even_w_out")}
    placed["pack"] = _place_shard(pack, place, F32, "place_pack")

    order, group = _stream_tables(place[0], place[1], EVEN_IN)
    p_e, h_e, (wt["even_w_in_t"], packs), (wt["even_w_kv"], wt["even_w_out"]) = _in_fwd_streamed(
        x, w["even_pre_g"], [placed["even_w_in_t"], placed["pack"]], [placed["even_w_kv"], placed["even_w_out"]],
        order, group, "even_in_streamed")
    for n in ("odd_w_in_t", "odd_w_kv", "odd_w_out", "odd_d_pw_w"):
        placed[n] = _place_shard(shards[n], place, BF16, "place_" + n, after=p_e[0:16, 0:128])
    packs = packs.reshape(N_CHIPS, PACK_ROWS * D)
    per_chip = [_flat_unpack(packs[k], [shp for _, shp in PACKED]) for k in range(N_CHIPS)]
    for a, (name, _) in enumerate(PACKED):
        wt[name] = jnp.concatenate([per_chip[k][a] for k in range(N_CHIPS)], axis=-1)
    for name in ("even_pre_g", "even_a_ln_g", "even_a_ln_b", "even_mem_g", "even_post_g"):
        wt[name] = w[name]

    tril = jnp.tril(jnp.ones((CH, CH), dtype=bool))
    wcat = jnp.where(tril[None], w["even_a_ws"][0], 0.0).transpose(1, 0, 2).reshape(CH, 4 * CH).astype(BF16)
    bsg = jnp.repeat(w["even_a_bs"][0].T, BW // 4, axis=1)
    hsel = (jnp.arange(BW)[:, None] // (BW // 4) == jnp.arange(128)[None, :]).astype(BF16)
    g4 = BW // 4
    eye = jnp.eye(4, dtype=F32)
    wbd = (w["odd_c_wgrp"][0][:, :, None, :] * eye[:, None, :, None]).reshape(BW, BW).astype(BF16)

    kv_e = _kv_fwd(mem, wt["even_mem_g"], wt["even_w_kv"], "even_kv")
    (x1, o_e, y_e), (wt["odd_w_in_t"],) = _even_fwd(
        x, p_e, kv_e, wt["even_a_ln_g"], wt["even_a_ln_b"], wcat, bsg, wt["even_b_conv"], wt["even_w_out"],
        wt["even_post_g"], rider=_GatherRider([placed["odd_w_in_t"]]))
    names = ("odd_w_out", "odd_d_pw_w", "odd_w_kv")
    (p_o, h_o), got = _in_fwd(x1, wt["odd_pre_g"], wt["odd_w_in_t"], "odd_in",
                              rider=_GatherRider([placed[n] for n in names]))
    wt.update(zip(names, got))
    kv_o = _kv_fwd(mem, wt["odd_mem_g"], wt["odd_w_kv"], "odd_kv")
    dx2, o_o, cv_o, loss = _odd_fwd(x1, p_o, kv_o, wbd, wt["odd_c_scale"], wt["odd_d_dw_w"], wt["odd_d_dw_b"],
                                    wt["odd_d_ln_g"], wt["odd_d_ln_b"], wt["odd_d_pw_w"], wt["odd_d_pw_b"],
                                    wt["odd_w_out"], wt["odd_post_g"], target)
    (dpc_o, tmpc, tmpd, do_o, y_o, g_post_o, g_cs, g_wbd, g_dww, g_dwb, g_lng_o, g_lnb_o, g_pww, g_pwb,
     dkv_o) = _odd_bwd1(dx2, o_o, cv_o, p_o, kv_o, wbd, wt["odd_c_scale"], wt["odd_d_dw_w"], wt["odd_d_dw_b"],
                        wt["odd_d_ln_g"], wt["odd_d_ln_b"], wt["odd_d_pw_w"], wt["odd_d_pw_b"], wt["odd_w_out"],
                        wt["odd_post_g"])
    dpb_o, dx1, g_pre_o = _odd_bwd2(dpc_o, tmpc, tmpd, p_o, wt["odd_d_dw_w"], wt["odd_w_in_t"], x1,
                                    wt["odd_pre_g"], dx2)
    g_win_o = _grad_tn(dpb_o, h_o, 768, rows=ODD_IN, name="odd_gw_in_b")
    g_win_o = _grad_tn(dpc_o, h_o, 1280, out=g_win_o, rows=ODD_IN, row0=3 * BW, name="odd_gw_in_c")
    g_wout_o = _grad_tn(y_o, do_o, 1024, name="odd_gw_out")
    g_wkv_o, g_memg_o = _kv_bwd(mem, wt["odd_mem_g"], wt["odd_w_kv"], dkv_o, "odd_kv_bwd")
    big_o = [_four(g) for g in (g_win_o, g_pww.astype(BF16), g_wkv_o, g_wout_o)]
    recv_o, _ = _swap_halves(big_o, "swap_halves_odd")
    sums_o = [_pair_sum(big_o[a], recv_o[a], place, "pair_sum_odd_%d" % a) for a in range(len(big_o))]
    (dp_e, do_e, g_post_e, g_lng_e, g_lnb_e, g_wcat, g_bs, g_bconv,
     dkv_e), parts_o = _even_bwd1(dx1, o_e, p_e, kv_e, wt["even_a_ln_g"], wt["even_a_ln_b"], wcat, bsg, hsel,
                                  wt["even_b_conv"], wt["even_w_out"], wt["even_post_g"],
                                  rider=_ExchangeRider(sums_o))
    halves_o = [_chip_sum(sums_o[a], parts_o[a], place, "chip_sum_odd_%d" % a) for a in range(len(big_o))]
    g_wout_e = _grad_tn(y_e, do_e, 1024, name="even_gw_out")
    g_wkv_e, g_memg_e = _kv_bwd(mem, wt["even_mem_g"], wt["even_w_kv"], dkv_e, "even_kv_bwd")
    big_x = [_four(g) for g in (g_wkv_e, g_wout_e)]
    recv_x, _ = _swap_halves(big_x, "swap_halves_kv_out")
    sums_x = [_pair_sum(big_x[a], recv_x[a], place, "pair_sum_kv_out_%d" % a) for a in range(len(big_x))]
    riders = _Riders([_ExchangeRider(sums_x), _ShareRider(halves_o)])
    g_win_e, got = _grad_tn(dp_e, h_e, 1280, name="even_gw_in", rider=riders)
    parts_x, full_o = riders.split(got)

    def sparsecore_adamw(names, fulls, name):
        as_kept = [(lambda t: t.T) if n.endswith("w_in") else (lambda t: t) for n in names]
        res = _adamw_sparsecore([f(w[n][0]) for f, n in zip(as_kept, names)],
                                [g_.reshape(g_.shape[1] * 2, g_.shape[2]) for g_ in fulls],
                                [f(mom[n][0]) for f, n in zip(as_kept, names)],
                                [f(var[n][0]) for f, n in zip(as_kept, names)], name)
        return {n: tuple(f(t) for t in r_) for f, n, r_ in zip(as_kept, names, res)}

    upd_sc = sparsecore_adamw(("odd_w_in", "odd_d_pw_w", "odd_w_kv", "odd_w_out"), full_o, "adamw_odd_sparsecore")
    halves_x = [_chip_sum(sums_x[a], parts_x[a], place, "chip_sum_kv_out_%d" % a) for a in range(len(big_x))]
    big_e = [_four(g_win_e)]
    recv_e, full_x = _swap_halves(big_e, "swap_halves_even", share=halves_x)
    upd_sc.update(sparsecore_adamw(("even_w_kv", "even_w_out"), full_x, "adamw_kv_out_sparsecore"))
    sums_e = [_pair_sum(big_e[0], recv_e[0], place, "pair_sum_even_w_in")]
    (dx0, g_pre_e), parts_e = _even_bwd2(dp_e, wt["even_w_in_t"], x, wt["even_pre_g"], dx1,
                                         rider=_ExchangeRider(sums_e))
    halves_e = [_chip_sum(sums_e[0], parts_e[0], place, "chip_sum_even_w_in")]

    g_aws = jnp.where(tril[None], g_wcat.reshape(CH, 4, CH).transpose(1, 0, 2), 0.0)
    g_wgrp = jnp.stack([lax.dynamic_slice(g_wbd, (g * g4, g * g4), (g4, g4)) for g in range(4)])
    small = {
        "even_pre_g": g_pre_e, "even_a_ln_g": g_lng_e, "even_a_ln_b": g_lnb_e, "even_a_ws": g_aws,
        "even_a_bs": g_bs[:, 0:4].T, "even_b_conv": g_bconv[0:3], "even_mem_g": g_memg_e, "even_post_g": g_post_e,
        "odd_pre_g": g_pre_o, "odd_c_wgrp": g_wgrp, "odd_c_scale": g_cs, "odd_d_dw_w": g_dww.reshape(CONF, 8, BW).sum(axis=1),
        "odd_d_dw_b": g_dwb, "odd_d_ln_g": g_lng_o, "odd_d_ln_b": g_lnb_o, "odd_d_pw_b": g_pwb,
        "odd_mem_g": g_memg_o, "odd_post_g": g_post_o,
    }
    small_names = SMALL_EVEN + SMALL_ODD
    small_pack = _flat_pack([small[n] for n in small_names] + [loss[0, 0].reshape(1)], SMALL_ROWS)
    small_total, full = _finish_reduce(small_pack, halves_e)
    gbig = {"even_w_in": full[0].reshape(full[0].shape[1] * 2, full[0].shape[2])}
    return dx0, gbig, upd_sc, small_total.reshape(-1), [small[n].shape for n in small_names]


def kernel(x, mem, even_pre_g, even_w_in, even_a_ln_g, even_a_ln_b, even_a_ws, even_a_bs, even_b_conv, even_mem_g, even_w_kv, even_w_out, even_post_g, odd_pre_g, odd_w_in, odd_c_wgrp, odd_c_scale, odd_d_dw_w, odd_d_dw_b, odd_d_ln_g, odd_d_ln_b, odd_d_pw_w, odd_d_pw_b, odd_mem_g, odd_w_kv, odd_w_out, odd_post_g, loss_target, m_even_pre_g, m_even_w_in, m_even_a_ln_g, m_even_a_ln_b, m_even_a_ws, m_even_a_bs, m_even_b_conv, m_even_mem_g, m_even_w_kv, m_even_w_out, m_even_post_g, m_odd_pre_g, m_odd_w_in, m_odd_c_wgrp, m_odd_c_scale, m_odd_d_dw_w, m_odd_d_dw_b, m_odd_d_ln_g, m_odd_d_ln_b, m_odd_d_pw_w, m_odd_d_pw_b, m_odd_mem_g, m_odd_w_kv, m_odd_w_out, m_odd_post_g, v_even_pre_g, v_even_w_in, v_even_a_ln_g, v_even_a_ln_b, v_even_a_ws, v_even_a_bs, v_even_b_conv, v_even_mem_g, v_even_w_kv, v_even_w_out, v_even_post_g, v_odd_pre_g, v_odd_w_in, v_odd_c_wgrp, v_odd_c_scale, v_odd_d_dw_w, v_odd_d_dw_b, v_odd_d_ln_g, v_odd_d_ln_b, v_odd_d_pw_w, v_odd_d_pw_b, v_odd_mem_g, v_odd_w_kv, v_odd_w_out, v_odd_post_g):
    given = dict(locals())
    w = {n: given[n] for n in WEIGHTS}
    mom = {n: given["m_" + n] for n in WEIGHTS}
    var = {n: given["v_" + n] for n in WEIGHTS}

    x_, y_, c_ = lax.axis_index("x"), lax.axis_index("y"), lax.axis_index("c")
    chip = 2 * x_ + y_
    place = jnp.stack([c_, chip]).astype(jnp.int32)
    grad_x, gbig, upd_odd, gsmall_flat, small_shapes = _step(x[0], mem[0], loss_target[0], w, mom, var, place)

    names = SMALL_EVEN + SMALL_ODD
    grads = {}
    unpacked = _flat_unpack(gsmall_flat, small_shapes + [(1,)])
    loss = unpacked[-1][0]
    for n, g in zip(names, unpacked[:-1]):
        shard_shape = w[n].shape[1:]
        if g.shape[-1] != shard_shape[-1]:
            g = lax.dynamic_slice_in_dim(g, chip * shard_shape[-1], shard_shape[-1], axis=g.ndim - 1)
        grads[n] = g.reshape(shard_shape)

    def two_d(a):
        return a.reshape(-1, a.shape[-1])

    upd = {}
    for n in BIG:
        if n in upd_odd:
            res = upd_odd[n]
        elif n.endswith("w_in"):
            res = _adamw_big(w[n][0].T, gbig[n], mom[n][0].T, var[n][0].T, "adamw_" + n)
            res = tuple(r.T for r in res)
        else:
            res = _adamw_big(w[n][0], gbig[n], mom[n][0], var[n][0], "adamw_" + n)
        grads[n], upd[n] = res[0], res[1:]
    res = _adamw_small([two_d(w[n][0]) for n in names], [two_d(grads[n]) for n in names],
                       [two_d(mom[n][0]) for n in names], [two_d(var[n][0]) for n in names])
    for n, r in zip(names, res):
        upd[n] = r

    outs = [loss, grad_x[None]]
    outs += [grads[n].reshape(w[n].shape) for n in WEIGHTS]
    for j in range(3):
        outs += [upd[n][j].reshape(w[n].shape) for n in WEIGHTS]
    return tuple(outs)
```

```python
import jax
import jax.numpy as jnp
from jax import lax
from jax.experimental import pallas as pl
from jax.experimental.pallas import tpu as pltpu
from jax.experimental.pallas import tpu_sc as plsc

F32 = jnp.float32
BF16 = jnp.bfloat16
MESH = pl.DeviceIdType.MESH

D = 1024
N_MEM = 256
MIX = 2048
XA = 512
HD = 128
BW = 768
CH = 128
EPS = 1e-6
SCALE = HD ** -0.5
POOL_WINDOWS = (2, 4, 8, 16)
CONF = 31
EVEN_IN = 6400
ODD_IN = 4864
N_CHIPS = 4

ADAM_LR = 0.001
ADAM_B1 = 0.9
ADAM_B2 = 0.999
ADAM_EPS = 1e-08
ADAM_WD = 0.01
ADAM_STEP = 10

TS = 256
HALO = 32
VMEM_LIMIT = 56 * 1024 * 1024


def _cp(sem=None):
    return pltpu.CompilerParams(dimension_semantics=sem, vmem_limit_bytes=VMEM_LIMIT)


def _dot(a, b):
    return jnp.dot(a, b, preferred_element_type=F32)


def _dot_nt(a, b):
    return lax.dot_general(a, b, (((1,), (1,)), ((), ())), preferred_element_type=F32)


def _dot_tn(a, b):
    return lax.dot_general(a, b, (((0,), (0,)), ((), ())), preferred_element_type=F32)


def _sigmoid(x):
    return 1.0 / (1.0 + jnp.exp(-x))


def _resident(shape):
    return pl.BlockSpec(shape, lambda *_: (0,) * len(shape), pipeline_mode=pl.Buffered(1))


def _const(shape):
    return pl.BlockSpec(shape, lambda *_: (0,) * len(shape))


def _kv_fwd(mem, mem_g, wkv, name):
    def body(mem_ref, g_ref, w_ref, kv_ref):
        m = mem_ref[...]
        r = lax.rsqrt(jnp.mean(m * m, axis=-1, keepdims=True) + EPS)
        mn = (m * r * g_ref[...]).astype(BF16)
        kv_ref[...] = _dot(mn, w_ref[...]).astype(BF16)

    return pl.pallas_call(body, out_shape=jax.ShapeDtypeStruct((N_MEM, D), BF16), name=name,
                          compiler_params=_cp())(mem, mem_g, wkv)


def _kv_bwd(mem, mem_g, wkv, dkv, name):
    def body(mem_ref, g_ref, w_ref, dkv_ref, dw_ref, dg_ref):
        m = mem_ref[...]
        r = lax.rsqrt(jnp.mean(m * m, axis=-1, keepdims=True) + EPS)
        mh = m * r
        mn = (mh * g_ref[...]).astype(BF16)
        dkv = dkv_ref[...].astype(BF16)
        dw_ref[...] = _dot_tn(mn, dkv).astype(BF16)
        dmn = _dot_nt(dkv, w_ref[...])
        dg_ref[...] = jnp.sum(dmn * mh, axis=0, keepdims=True)

    return pl.pallas_call(body, out_shape=(jax.ShapeDtypeStruct((D, D), BF16), jax.ShapeDtypeStruct((1, D), F32)),
                          name=name, compiler_params=_cp())(mem, mem_g, wkv, dkv)


def _host_call(body, *, grid, name, out_shape, in_specs, out_specs, args, scratch_shapes=(), aliases=None,
               rider=None):
    sem = ("arbitrary",) * len(grid)
    aliases = dict(aliases or {})
    if rider is None:
        res = pl.pallas_call(body, grid=grid, name=name, out_shape=tuple(out_shape), in_specs=list(in_specs),
                             out_specs=tuple(out_specs), scratch_shapes=list(scratch_shapes),
                             input_output_aliases=aliases, compiler_params=_cp(sem))(*args)
        return tuple(res), ()
    n_in, n_out, n_sc = len(in_specs), len(out_specs), len(scratch_shapes)
    r_in, r_out = len(rider.inputs), len(rider.out_shapes)

    def full_body(*refs):
        host_in = refs[:n_in]
        rid_in = refs[n_in:n_in + r_in]
        host_out = refs[n_in + r_in:n_in + r_in + n_out]
        rid_out = refs[n_in + r_in + n_out:n_in + r_in + n_out + r_out]
        host_sc = refs[n_in + r_in + n_out + r_out:n_in + r_in + n_out + r_out + n_sc]
        sems = refs[n_in + r_in + n_out + r_out + n_sc:]
        first = pl.program_id(0) == 0
        last = pl.program_id(0) == grid[0] - 1
        for ax in range(1, len(grid)):
            first = jnp.logical_and(first, pl.program_id(ax) == 0)
            last = jnp.logical_and(last, pl.program_id(ax) == grid[ax] - 1)

        @pl.when(first)
        def _():
            rider.start(rid_in, rid_out, sems)

        if rider.has_mid:
            @pl.when(last)
            def _():
                rider.mid(rid_in, rid_out, sems)

        body(*host_in, *host_out, *host_sc)

        @pl.when(last)
        def _():
            rider.end(rid_in, rid_out, sems)

    aliases.update({n_in + j: n_out + k for j, k in rider.aliases.items()})
    res = pl.pallas_call(
        full_body, grid=grid, name=name, out_shape=tuple(out_shape) + tuple(rider.out_shapes),
        in_specs=list(in_specs) + _hbm_specs(r_in), out_specs=tuple(out_specs) + tuple(_hbm_specs(r_out)),
        scratch_shapes=list(scratch_shapes) + list(rider.sems), input_output_aliases=aliases,
        compiler_params=_cp(sem),
    )(*args, *rider.inputs)
    return tuple(res[:n_out]), tuple(res[n_out:])


def _in_fwd(x, pre_g, w_t, name, rider=None):
    s, n = x.shape[0], w_t.shape[0]
    tm = min(512, s)
    nc = 256

    def body(x_ref, g_ref, w_ref, p_ref, h_ref):
        xv = x_ref[...]
        r = lax.rsqrt(jnp.mean(xv * xv, axis=-1, keepdims=True) + EPS)
        h = (xv * r * g_ref[...]).astype(BF16)
        h_ref[...] = h
        for j in range(n // nc):
            p_ref[:, j * nc:(j + 1) * nc] = _dot_nt(h, w_ref[j * nc:(j + 1) * nc, :]).astype(BF16)

    return _host_call(
        body, grid=(s // tm,), name=name, rider=rider,
        out_shape=(jax.ShapeDtypeStruct((s, n), BF16), jax.ShapeDtypeStruct((s, D), BF16)),
        in_specs=[pl.BlockSpec((tm, D), lambda i: (i, 0)), _const((1, D)), _resident((n, D))],
        out_specs=(pl.BlockSpec((tm, n), lambda i: (i, 0)), pl.BlockSpec((tm, D), lambda i: (i, 0))),
        args=(x, pre_g, w_t))


NC = 256


def _stream_tables(core, chip, n):
    nchunk = n // NC
    idx = jnp.arange(nchunk, dtype=jnp.int32)
    src = jnp.array([0, 2, 1, 3], jnp.int32)
    r = n // N_CHIPS

    def group_of(row):
        j = src[(row // r) ^ chip]
        through_sibling = ((row % r) // (r // 2) != core).astype(jnp.int32)
        return jnp.where(j == 0, 0, 2 * j - 1 + through_sibling)

    grp = jnp.maximum(group_of(idx * NC), group_of(idx * NC + NC - 1))
    order = jnp.argsort(grp * 64 + idx).astype(jnp.int32)
    return order, grp[order]


def _in_fwd_streamed(x, pre_g, first, later, order, group, name):
    s, n = x.shape[0], first[0].shape[0]
    nchunk = n // NC
    rider = _GatherRider(first)
    rider2 = _GatherRider(later) if later else None
    a, m = len(first), len(later)
    tr = min(256, s)

    def body(*refs):
        order_ref, group_ref, x_ref, g_ref = refs[0:4]
        p_ref, h_ref = refs[4 + a + m:6 + a + m]
        outs = refs[6 + a + m:6 + 2 * a + m]
        outs2 = refs[6 + 2 * a + m:6 + 2 * a + 2 * m]
        wbuf, wsem, send_sems, recv_sems = refs[6 + 2 * a + 2 * m:10 + 2 * a + 2 * m]
        sems2 = refs[10 + 2 * a + 2 * m:]
        w_hbm = outs[0]
        j = pl.program_id(0)
        sems = (send_sems, recv_sems)
        grp = group_ref[j]
        new_group = jnp.logical_or(j == 0, group_ref[jnp.maximum(j - 1, 0)] != grp)
        slot = j % 2

        def fetch(step, sl):
            rows = pl.ds(pl.multiple_of(order_ref[step] * NC, NC), NC)
            return pltpu.make_async_copy(w_hbm.at[rows], wbuf.at[sl], wsem.at[sl])

        @pl.when(j == 0)
        def _():
            rider.start(None, outs, sems, peers=(0, 1))

            @pl.loop(0, s // tr)
            def _(t):
                rows = pl.ds(pl.multiple_of(t * tr, tr), tr)
                xv = x_ref[rows, :]
                r = lax.rsqrt(jnp.mean(xv * xv, axis=-1, keepdims=True) + EPS)
                h_ref[rows, :] = (xv * r * g_ref[...]).astype(BF16)

        before = jnp.where(j == 0, 0, group_ref[jnp.maximum(j - 1, 0)])

        def entering(b):
            return jnp.logical_and(before < b, b <= grp)

        for src in range(3):
            @pl.when(entering(2 * src + 1))
            def _(src=src):
                if src == 0:
                    rider.start(None, outs, sems, peers=(2,))
                rider.mid(None, outs, sems, peers=(src,))
                if src == 1 and rider2 is not None:
                    rider2.start(None, outs2, sems2)

            @pl.when(entering(2 * src + 2))
            def _(src=src):
                rider.wait_forwarded(outs, sems, peers=(src,))

        @pl.when(new_group)
        def _():
            fetch(j, slot).start()

        fetch(j, slot).wait()
        nxt = jnp.minimum(j + 1, nchunk - 1)

        @pl.when(jnp.logical_and(j + 1 < nchunk, group_ref[nxt] == grp))
        def _():
            fetch(nxt, 1 - slot).start()

        p_ref[...] = _dot_nt(h_ref[...], wbuf[slot]).astype(BF16)

        @pl.when(j == nchunk - 1)
        def _():
            rider.wait_sends(outs, sems)
            if rider2 is not None:
                rider2.mid(None, outs2, sems2)
                rider2.end(None, outs2, sems2)

    hbm = pl.BlockSpec(memory_space=pltpu.HBM)
    arrs = list(first) + list(later)
    whole = pl.BlockSpec((s, D), lambda j, o, g: (0, 0), pipeline_mode=pl.Buffered(1))
    res = pl.pallas_call(
        body, name=name,
        out_shape=(jax.ShapeDtypeStruct((s, n), BF16), jax.ShapeDtypeStruct((s, D), BF16))
        + tuple(jax.ShapeDtypeStruct(v.shape, v.dtype) for v in arrs),
        grid_spec=pltpu.PrefetchScalarGridSpec(
            num_scalar_prefetch=2, grid=(nchunk,),
            in_specs=[whole, pl.BlockSpec((1, D), lambda j, o, g: (0, 0))] + [hbm] * (a + m),
            out_specs=(pl.BlockSpec((s, NC), lambda j, o, g: (0, o[j])),
                       pl.BlockSpec((s, D), lambda j, o, g: (0, 0))) + (hbm,) * (a + m),
            scratch_shapes=[pltpu.VMEM((2, NC, D), BF16), pltpu.SemaphoreType.DMA((2,))] + list(rider.sems)
            + (list(rider2.sems) if rider2 is not None else [])),
        input_output_aliases={4 + v: 2 + v for v in range(a + m)},
        compiler_params=_cp(("arbitrary",)),
    )(order, group, x, pre_g, *arrs)
    return res[0], res[1], tuple(res[2:2 + a]), tuple(res[2 + a:])


def _xattn_fwd(q, kv_ref):
    outs, probs = [], []
    for h in range(XA // HD):
        qh = q[:, h * HD:(h + 1) * HD]
        kh = kv_ref[:, h * HD:(h + 1) * HD]
        vh = kv_ref[:, XA + h * HD:XA + (h + 1) * HD]
        sc = _dot_nt(qh, kh) * SCALE
        e = jnp.exp(sc - jnp.max(sc, axis=-1, keepdims=True))
        pr = e / jnp.sum(e, axis=-1, keepdims=True)
        outs.append(_dot(pr.astype(BF16), vh))
        probs.append(pr)
    return jnp.concatenate(outs, axis=-1), probs


def _xattn_bwd(dyx, q, probs, kv_ref, dkv_ref):
    dqs = []
    for h in range(XA // HD):
        qh = q[:, h * HD:(h + 1) * HD]
        kh = kv_ref[:, h * HD:(h + 1) * HD]
        vh = kv_ref[:, XA + h * HD:XA + (h + 1) * HD]
        dy = dyx[:, h * HD:(h + 1) * HD].astype(BF16)
        pr = probs[h]
        dp = _dot_nt(dy, vh)
        ds = (pr * (dp - jnp.sum(dp * pr, axis=-1, keepdims=True))).astype(BF16)
        dqs.append(_dot(ds, kh) * SCALE)
        dkv_ref[:, h * HD:(h + 1) * HD] += _dot_tn(ds, qh) * SCALE
        dkv_ref[:, XA + h * HD:XA + (h + 1) * HD] += _dot_tn(pr.astype(BF16), dy)
    return jnp.concatenate(dqs, axis=-1)


def _layer_norm_fwd(v, g, b):
    mu = jnp.mean(v, axis=-1, keepdims=True)
    vc = v - mu
    rstd = lax.rsqrt(jnp.mean(vc * vc, axis=-1, keepdims=True) + EPS)
    vhat = vc * rstd
    return vhat * g + b, vhat, rstd


def _layer_norm_bwd(dy, vhat, rstd, g):
    dvh = dy * g
    return rstd * (dvh - jnp.mean(dvh, axis=-1, keepdims=True) - vhat * jnp.mean(dvh * vhat, axis=-1, keepdims=True))


def _head_masks():
    col = lax.broadcasted_iota(jnp.int32, (1, BW), 1)
    return [(col >= h * (BW // 4)) & (col < (h + 1) * (BW // 4)) for h in range(4)]


def _halo_prev(nblk_per_tile):
    return lambda i: (jnp.maximum(i * nblk_per_tile - 1, 0), 0)


def _row_ids(i, t):
    return i * t + lax.broadcasted_iota(jnp.int32, (t, 1), 0)


def _even_mix(i, p_ref, ph_ref, ln_g, ln_b, wcat_ref, bsg_ref, bconv_ref, wbuf):
    t = p_ref.shape[0]
    u = p_ref[:, 0:BW].astype(F32)
    v = p_ref[:, BW:2 * BW].astype(F32)
    bg = p_ref[:, 2 * BW:3 * BW].astype(F32)
    cg = p_ref[:, 3 * BW:4 * BW].astype(F32)
    xin = p_ref[:, 4 * BW:5 * BW].astype(F32)
    vn, vhat, rstd = _layer_norm_fwd(v, ln_g, ln_b)
    masks = _head_masks()
    sgs, vsts = [], []
    for n in range(t // CH):
        vn_c = vn[n * CH:(n + 1) * CH]
        vst = jnp.concatenate([jnp.where(m, vn_c, 0.0) for m in masks], axis=0).astype(BF16)
        sgs.append(_dot(wcat_ref[...], vst) + bsg_ref[...])
        vsts.append(vst)
    sg = jnp.concatenate(sgs, axis=0)
    ya = u * sg
    w_halo = ph_ref[:, 3 * BW:4 * BW].astype(F32) * ph_ref[:, 4 * BW:5 * BW].astype(F32)
    wbuf[0:HALO, :] = jnp.where(i > 0, w_halo, 0.0)
    wbuf[HALO:HALO + t, :] = cg * xin
    conv = (bconv_ref[0:1, :] * wbuf[pl.ds(HALO - 2, t), :] + bconv_ref[1:2, :] * wbuf[pl.ds(HALO - 1, t), :]
            + bconv_ref[2:3, :] * wbuf[pl.ds(HALO, t), :])
    yb = bg * conv
    return dict(u=u, bg=bg, cg=cg, xin=xin, vhat=vhat, rstd=rstd, sg=sg, vsts=vsts, conv=conv, ya=ya, yb=yb,
                masks=masks)


def _pool_select(vals):
    col = lax.broadcasted_iota(jnp.int32, (1, BW), 1)
    g = BW // 4
    return jnp.where(col < g, vals[0], jnp.where(col < 2 * g, vals[1], jnp.where(col < 3 * g, vals[2], vals[3])))


def _inv_counts(i, t):
    rows = _row_ids(i, t) + 1
    return [1.0 / jnp.minimum(rows, w).astype(F32) for w in POOL_WINDOWS]


def _band_matrices(t, forward):
    j = jnp.arange(t)[:, None]
    r = jnp.arange(HALO + t)[None, :]
    if forward:
        return jnp.stack([(r >= j) & (r < j + w) for w in POOL_WINDOWS]).astype(BF16)
    return jnp.stack([(r <= HALO + j) & (r > HALO + j - w) for w in POOL_WINDOWS]).astype(BF16)


SHIFT_ROWS = HALO + TS - 8


def _shifted_copies(buf, sh):
    for b in range(1, 8):
        sh[b - 1] = buf[pl.ds(b, SHIFT_ROWS), :]


def _rows_at(buf, sh, off, t):
    a, b = divmod(off, 8)
    return buf[pl.ds(8 * a, t), :] if b == 0 else sh[b - 1, pl.ds(8 * a, t), :]


def _tap_sums(d_ref, buf, sh, base, out_ref):
    t = d_ref.shape[0]
    group = 4
    for k0 in range(0, CONF, group):
        taps = list(range(k0, min(k0 + group, CONF)))

        def step(r, accs, taps=taps):
            row = pl.multiple_of(r * 8, 8)
            d = d_ref[pl.ds(row, 8), :]
            new = []
            for acc, k in zip(accs, taps):
                a, b = divmod(base + k, 8)
                src = buf[pl.ds(row + 8 * a, 8), :] if b == 0 else sh[b - 1, pl.ds(row + 8 * a, 8), :]
                new.append(acc + d * src)
            return tuple(new)

        accs = lax.fori_loop(0, t // 8, step, tuple(jnp.zeros((8, BW), F32) for _ in taps), unroll=2)
        for acc, k in zip(accs, taps):
            out_ref[8 * k:8 * k + 8, :] += acc


def _odd_mix(i, p_ref, ph_ref, bands_ref, wbd_ref, cscale, dww_ref, dwb, ln_g, ln_b, pww_ref, pwb, gbuf, gsh,
             cv=None):
    t = p_ref.shape[0]
    zc_bf = p_ref[:, 0:BW]
    zc = zc_bf.astype(F32)
    ga = p_ref[:, BW:2 * BW].astype(F32)
    gb = p_ref[:, 2 * BW:3 * BW].astype(F32)
    zh = ph_ref[:, 0:BW]
    zcat = jnp.concatenate([jnp.where(i > 0, zh, jnp.zeros_like(zh)), zc_bf], axis=0)
    inv = _inv_counts(i, t)
    pooled = _pool_select([_dot(bands_ref[w], zcat) * inv[w] for w in range(len(POOL_WINDOWS))]) - zc
    pooled_bf = pooled.astype(BF16)
    pre = _dot(pooled_bf, wbd_ref[...])
    yc = pre * cscale
    sgb = _sigmoid(gb)
    z = ga * sgb
    gh_a = ph_ref[:, BW:2 * BW].astype(F32)
    gh_b = ph_ref[:, 2 * BW:3 * BW].astype(F32)
    gbuf[0:HALO, :] = jnp.where(i > 0, gh_a * _sigmoid(gh_b), 0.0)
    gbuf[HALO:HALO + t, :] = z
    _shifted_copies(gbuf, gsh)
    if cv is None:
        cv = dwb + dww_ref[CONF - 1:CONF, :] * z
        for k in range(CONF - 1):
            cv = cv + dww_ref[k:k + 1, :] * _rows_at(gbuf, gsh, HALO - (CONF - 1) + k, t)
    zl, zhat, rstd = _layer_norm_fwd(cv, ln_g, ln_b)
    szl = _sigmoid(zl)
    zs = (zl * szl).astype(BF16)
    yd = _dot(zs, pww_ref[...]) + pwb
    return dict(ga=ga, sgb=sgb, pooled_bf=pooled_bf, pre=pre, yc=yc, zhat=zhat, rstd=rstd, zl=zl, szl=szl,
                zs=zs, yd=yd, inv=inv, cv=cv)


def _post_norm(o, post_g):
    r = lax.rsqrt(jnp.mean(o * o, axis=-1, keepdims=True) + EPS)
    return o * r, r


def _gate_out(y_a, y_b, y_x, gate, wout_ref):
    sgt = _sigmoid(gate)
    sgate = gate * sgt
    ys = [(y_a * sgate[:, 0:BW]).astype(BF16), (y_b * sgate[:, BW:2 * BW]).astype(BF16),
          (y_x * sgate[:, 2 * BW:MIX]).astype(BF16)]
    o = (_dot(ys[0], wout_ref[0:BW, :]) + _dot(ys[1], wout_ref[BW:2 * BW, :]) + _dot(ys[2], wout_ref[2 * BW:MIX, :]))
    return o, ys, sgt, sgate


def _tile_specs(s, n):
    nh = TS // HALO
    return pl.BlockSpec((TS, n), lambda i: (i, 0)), pl.BlockSpec((HALO, n), _halo_prev(nh))


def _even_fwd(x, p, kv, ln_g, ln_b, wcat, bsg, bconv, wout, post_g, rider=None):
    s = x.shape[0]

    def body(x_ref, p_ref, ph_ref, kv_ref, lng, lnb, wcat_ref, bsg_ref, bconv_ref, wout_ref, pg, x1_ref, o_ref,
             y_ref, wbuf):
        i = pl.program_id(0)
        mx = _even_mix(i, p_ref, ph_ref, lng[...], lnb[...], wcat_ref, bsg_ref, bconv_ref, wbuf)
        yx, _ = _xattn_fwd(p_ref[:, 5 * BW:5 * BW + XA], kv_ref)
        gate = p_ref[:, 5 * BW + XA:EVEN_IN].astype(F32)
        o, ys, _, _ = _gate_out(mx["ya"], mx["yb"], yx, gate, wout_ref)
        y_ref[:, 0:BW] = ys[0]
        y_ref[:, BW:2 * BW] = ys[1]
        y_ref[:, 2 * BW:MIX] = ys[2]
        n, _ = _post_norm(o, pg[...])
        o_ref[...] = o
        x1_ref[...] = x_ref[...] + n * pg[...]

    tile, halo = _tile_specs(s, EVEN_IN)
    row = pl.BlockSpec((TS, D), lambda i: (i, 0))
    return _host_call(
        body, grid=(s // TS,), name="even_fwd", rider=rider,
        out_shape=(jax.ShapeDtypeStruct((s, D), F32), jax.ShapeDtypeStruct((s, D), F32),
                   jax.ShapeDtypeStruct((s, MIX), BF16)),
        in_specs=[row, tile, halo, _const((N_MEM, D)), _const((1, BW)), _const((1, BW)), _const((CH, 4 * CH)),
                  _const((CH, BW)), _const((3, BW)), _resident((MIX, D)), _const((1, D))],
        out_specs=(row, row, pl.BlockSpec((TS, MIX), lambda i: (i, 0))),
        scratch_shapes=[pltpu.VMEM((HALO + TS, BW), F32)],
        args=(x, p, p, kv, ln_g, ln_b, wcat, bsg, bconv, wout, post_g))


def _odd_fwd(x1, p, kv, wbd, cscale, dww, dwb, ln_g, ln_b, pww, pwb, wout, post_g, target):
    s = x1.shape[0]

    def body(x_ref, p_ref, ph_ref, kv_ref, bands_ref, wbd_ref, cs, dww_ref, dwb_ref, lng, lnb, pww_ref, pwb_ref,
             wout_ref, pg, tgt_ref, dx_ref, o_ref, cv_ref, loss_ref, gbuf, gsh):
        i = pl.program_id(0)
        mx = _odd_mix(i, p_ref, ph_ref, bands_ref, wbd_ref, cs[...], dww_ref, dwb_ref[...], lng[...], lnb[...],
                      pww_ref, pwb_ref[...], gbuf, gsh)
        cv_ref[...] = mx["cv"]
        yx, _ = _xattn_fwd(p_ref[:, 3 * BW:3 * BW + XA], kv_ref)
        gate = p_ref[:, 3 * BW + XA:ODD_IN].astype(F32)
        o, _, _, _ = _gate_out(mx["yc"], mx["yd"], yx, gate, wout_ref)
        n, _ = _post_norm(o, pg[...])
        o_ref[...] = o
        err = x_ref[...] + n * pg[...] - tgt_ref[...]
        dx_ref[...] = err * (1.0 / D)

        @pl.when(i == 0)
        def _():
            loss_ref[...] = jnp.zeros_like(loss_ref)

        loss_ref[...] += 0.5 * jnp.sum(jnp.sum(err * err, axis=-1, keepdims=True) * (1.0 / D), axis=0, keepdims=True)

    tile, halo = _tile_specs(s, ODD_IN)
    row = pl.BlockSpec((TS, D), lambda i: (i, 0))
    vec = _const((1, BW))
    return pl.pallas_call(
        body, grid=(s // TS,), name="odd_fwd",
        out_shape=(jax.ShapeDtypeStruct((s, D), F32), jax.ShapeDtypeStruct((s, D), F32),
                   jax.ShapeDtypeStruct((s, BW), F32), jax.ShapeDtypeStruct((8, 128), F32)),
        in_specs=[row, tile, halo, _const((N_MEM, D)), _const((4, TS, HALO + TS)), _const((BW, BW)), vec,
                  _const((CONF, BW)), vec, vec, vec, _const((BW, BW)), vec, _resident((MIX, D)), _const((1, D)), row],
        out_specs=(row, row, pl.BlockSpec((TS, BW), lambda i: (i, 0)), _const((8, 128))),
        scratch_shapes=[pltpu.VMEM((HALO + TS, BW), F32), pltpu.VMEM((7, SHIFT_ROWS, BW), F32)],
        compiler_params=_cp(("arbitrary",)),
    )(x1, p, p, kv, _band_matrices(TS, False), wbd, cscale, dww, dwb, ln_g, ln_b, pww, pwb, wout, post_g, target)


def _acc_init(i, refs):
    @pl.when(i == 0)
    def _():
        for r in refs:
            r[...] = jnp.zeros_like(r)


def _post_norm_bwd(dx, o, pg, dpg_ref):
    n, r = _post_norm(o, pg)
    dpg_ref[...] += jnp.sum(dx * n, axis=0, keepdims=True)
    dn = dx * pg
    return (r * (dn - n * jnp.mean(dn * n, axis=-1, keepdims=True))).astype(BF16)


def _gate_bwd(do, wout_ref, ys_f32, gate, y_ref):
    dy = _dot_nt(do, wout_ref[...])
    sgt = _sigmoid(gate)
    sgate = gate * sgt
    dsilu = sgt * (1.0 + gate * (1.0 - sgt))
    offs = (0, BW, 2 * BW, MIX)
    dys, dgs = [], []
    for j, yv in enumerate(ys_f32):
        a, b = offs[j], offs[j + 1]
        if y_ref is not None:
            y_ref[:, a:b] = (yv * sgate[:, a:b]).astype(BF16)
        dys.append(dy[:, a:b] * sgate[:, a:b])
        dgs.append(dy[:, a:b] * yv * dsilu[:, a:b])
    return dys, jnp.concatenate(dgs, axis=-1)


NEXT = 16


def _even_bwd1(dx, o, p, kv, ln_g, ln_b, wcat, bsg, hsel, bconv, wout, post_g, rider=None):
    s = dx.shape[0]
    nt = s // TS

    def body(dx_ref, o_ref, p_ref, ph_ref, dxn_ref, on_ref, pn_ref, kv_ref, lng, lnb, wcat_ref, bsg_ref, hsel_ref,
             bconv_ref, wout_ref, pg,
             dp_ref, do_ref, dpg_ref, dlng_ref, dlnb_ref, dwcat_ref, dbs_ref, dbconv_ref, dkv_ref, wbuf, dbuf):
        i = pl.program_id(0)
        _acc_init(i, (dpg_ref, dlng_ref, dlnb_ref, dwcat_ref, dbs_ref, dbconv_ref, dkv_ref))
        mx = _even_mix(i, p_ref, ph_ref, lng[...], lnb[...], wcat_ref, bsg_ref, bconv_ref, wbuf)
        q = p_ref[:, 5 * BW:5 * BW + XA]
        yx, probs = _xattn_fwd(q, kv_ref)
        gate = p_ref[:, 5 * BW + XA:EVEN_IN].astype(F32)
        do = _post_norm_bwd(dx_ref[...], o_ref[...], pg[...], dpg_ref)
        do_ref[...] = do
        (dya, dyb, dyx), dgate = _gate_bwd(do, wout_ref, (mx["ya"], mx["yb"], yx), gate, None)
        dp_ref[:, 0:BW] = (dya * mx["sg"]).astype(BF16)
        dsg = (dya * mx["u"]).astype(BF16)
        dvns = []
        for n in range(TS // CH):
            dsg_c = dsg[n * CH:(n + 1) * CH]
            dvst = _dot_tn(wcat_ref[...], dsg_c)
            dvn_c = jnp.where(mx["masks"][0], dvst[0:CH], 0.0)
            for h in range(1, 4):
                dvn_c = dvn_c + jnp.where(mx["masks"][h], dvst[h * CH:(h + 1) * CH], 0.0)
            dvns.append(dvn_c)
            dwcat_ref[...] += _dot_nt(dsg_c, mx["vsts"][n])
            dbs_ref[...] += _dot(dsg_c, hsel_ref[...])
        dvn = jnp.concatenate(dvns, axis=0)
        dlng_ref[...] += jnp.sum(dvn * mx["vhat"], axis=0, keepdims=True)
        dlnb_ref[...] += jnp.sum(dvn, axis=0, keepdims=True)
        dp_ref[:, BW:2 * BW] = _layer_norm_bwd(dvn, mx["vhat"], mx["rstd"], lng[...]).astype(BF16)
        dp_ref[:, 2 * BW:3 * BW] = (dyb * mx["conv"]).astype(BF16)
        dconv = dyb * mx["bg"]
        for k in range(3):
            dbconv_ref[k:k + 1, :] += jnp.sum(dconv * wbuf[pl.ds(HALO - 2 + k, TS), :], axis=0, keepdims=True)
        n_n, r_n = _post_norm(on_ref[...], pg[...])
        dn_n = dxn_ref[...] * pg[...]
        do_n = (r_n * (dn_n - n_n * jnp.mean(dn_n * n_n, axis=-1, keepdims=True))).astype(BF16)
        dy_n = _dot_nt(do_n, wout_ref[BW:2 * BW, :])
        g_n = pn_ref[:, 5 * BW + XA + BW:5 * BW + XA + 2 * BW].astype(F32)
        dconv_n = dy_n * (g_n * _sigmoid(g_n)) * pn_ref[:, 2 * BW:3 * BW].astype(F32)
        dbuf[0:TS, :] = dconv
        dbuf[TS:TS + NEXT, :] = jnp.where(i < nt - 1, dconv_n, 0.0)
        dw = (bconv_ref[2:3, :] * dconv + bconv_ref[1:2, :] * dbuf[pl.ds(1, TS), :]
              + bconv_ref[0:1, :] * dbuf[pl.ds(2, TS), :])
        dp_ref[:, 3 * BW:4 * BW] = (dw * mx["xin"]).astype(BF16)
        dp_ref[:, 4 * BW:5 * BW] = (dw * mx["cg"]).astype(BF16)
        dp_ref[:, 5 * BW:5 * BW + XA] = _xattn_bwd(dyx, q, probs, kv_ref, dkv_ref).astype(BF16)
        dp_ref[:, 5 * BW + XA:EVEN_IN] = dgate.astype(BF16)

    tile, halo = _tile_specs(s, EVEN_IN)
    row = pl.BlockSpec((TS, D), lambda i: (i, 0))
    vec = _const((1, BW))
    nxt = _halo_next(TS // NEXT, s // NEXT)

    def out(n):
        return pl.BlockSpec((TS, n), lambda i: (i, 0))

    return _host_call(
        body, grid=(nt,), name="even_bwd1", rider=rider,
        out_shape=(jax.ShapeDtypeStruct((s, EVEN_IN), BF16), jax.ShapeDtypeStruct((s, D), BF16),
                   jax.ShapeDtypeStruct((1, D), F32), jax.ShapeDtypeStruct((1, BW), F32),
                   jax.ShapeDtypeStruct((1, BW), F32), jax.ShapeDtypeStruct((CH, 4 * CH), F32),
                   jax.ShapeDtypeStruct((CH, 128), F32), jax.ShapeDtypeStruct((8, BW), F32),
                   jax.ShapeDtypeStruct((N_MEM, D), F32)),
        in_specs=[row, row, tile, halo, pl.BlockSpec((NEXT, D), nxt), pl.BlockSpec((NEXT, D), nxt),
                  pl.BlockSpec((NEXT, EVEN_IN), nxt), _const((N_MEM, D)), vec, vec, _const((CH, 4 * CH)),
                  _const((CH, BW)), _const((BW, 128)), _const((3, BW)), _resident((MIX, D)), _const((1, D))],
        out_specs=(out(EVEN_IN), out(D),
                   _const((1, D)), vec, vec, _const((CH, 4 * CH)), _const((CH, 128)), _const((8, BW)),
                   _const((N_MEM, D))),
        scratch_shapes=[pltpu.VMEM((HALO + TS, BW), F32), pltpu.VMEM((TS + NEXT, BW), F32)],
        args=(dx, o, p, p, dx, o, p, kv, ln_g, ln_b, wcat, bsg, hsel, bconv, wout, post_g))


def _odd_bwd1(dx, o, cv, p, kv, wbd, cscale, dww, dwb, ln_g, ln_b, pww, pwb, wout, post_g):
    s = dx.shape[0]

    def body(dx_ref, o_ref, cv_ref, p_ref, ph_ref, kv_ref, bands_ref, wbd_ref, cs, dww_ref, dwb_ref, lng, lnb,
             pww_ref, pwb_ref, wout_ref, pg,
             dpc_ref, tmpc_ref, tmpd_ref, do_ref, y_ref, dpg_ref, dcs_ref, dwbd_ref, ddww_ref, ddwb_ref, dlng_ref,
             dlnb_ref, dpww_ref, dpwb_ref, dkv_ref, gbuf, gsh, dcv_buf):
        i = pl.program_id(0)
        _acc_init(i, (dpg_ref, dcs_ref, dwbd_ref, ddww_ref, ddwb_ref, dlng_ref, dlnb_ref, dpww_ref, dpwb_ref,
                      dkv_ref))
        mx = _odd_mix(i, p_ref, ph_ref, bands_ref, wbd_ref, cs[...], dww_ref, dwb_ref[...], lng[...], lnb[...],
                      pww_ref, pwb_ref[...], gbuf, gsh, cv=cv_ref[...])
        q = p_ref[:, 3 * BW:3 * BW + XA]
        yx, probs = _xattn_fwd(q, kv_ref)
        gate = p_ref[:, 3 * BW + XA:ODD_IN].astype(F32)
        do = _post_norm_bwd(dx_ref[...], o_ref[...], pg[...], dpg_ref)
        do_ref[...] = do
        (dyc, dyd, dyx), dgate = _gate_bwd(do, wout_ref, (mx["yc"], mx["yd"], yx), gate, y_ref)
        dcs_ref[...] += jnp.sum(dyc * mx["pre"], axis=0, keepdims=True)
        dpre = (dyc * cs[...]).astype(BF16)
        dwbd_ref[...] += _dot_tn(mx["pooled_bf"], dpre)
        dpooled = _dot_nt(dpre, wbd_ref[...])
        tmpc_ref[...] = _pool_select([dpooled * c_ for c_ in mx["inv"]]).astype(BF16)
        dyd_bf = dyd.astype(BF16)
        dpwb_ref[...] += jnp.sum(dyd, axis=0, keepdims=True)
        dpww_ref[...] += _dot_tn(mx["zs"], dyd_bf)
        dzs = _dot_nt(dyd_bf, pww_ref[...])
        zl, szl = mx["zl"], mx["szl"]
        dzl = dzs * (szl * (1.0 + zl * (1.0 - szl)))
        dlng_ref[...] += jnp.sum(dzl * mx["zhat"], axis=0, keepdims=True)
        dlnb_ref[...] += jnp.sum(dzl, axis=0, keepdims=True)
        dcv = _layer_norm_bwd(dzl, mx["zhat"], mx["rstd"], lng[...])
        tmpd_ref[...] = dcv.astype(BF16)
        ddwb_ref[...] += jnp.sum(dcv, axis=0, keepdims=True)
        dcv_buf[...] = dcv
        _tap_sums(dcv_buf, gbuf, gsh, HALO - (CONF - 1), ddww_ref)
        dpc_ref[:, 0:XA] = _xattn_bwd(dyx, q, probs, kv_ref, dkv_ref).astype(BF16)
        dpc_ref[:, XA:XA + MIX] = dgate.astype(BF16)

    tile, halo = _tile_specs(s, ODD_IN)
    row = pl.BlockSpec((TS, D), lambda i: (i, 0))
    vec = _const((1, BW))

    def out(n):
        return pl.BlockSpec((TS, n), lambda i: (i, 0))

    return pl.pallas_call(
        body, grid=(s // TS,), name="odd_bwd1",
        out_shape=(jax.ShapeDtypeStruct((s, XA + MIX), BF16), jax.ShapeDtypeStruct((s, BW), BF16),
                   jax.ShapeDtypeStruct((s, BW), BF16), jax.ShapeDtypeStruct((s, D), BF16),
                   jax.ShapeDtypeStruct((s, MIX), BF16),
                   jax.ShapeDtypeStruct((1, D), F32), jax.ShapeDtypeStruct((1, BW), F32),
                   jax.ShapeDtypeStruct((BW, BW), F32), jax.ShapeDtypeStruct((8 * CONF, BW), F32),
                   jax.ShapeDtypeStruct((1, BW), F32), jax.ShapeDtypeStruct((1, BW), F32),
                   jax.ShapeDtypeStruct((1, BW), F32), jax.ShapeDtypeStruct((BW, BW), F32),
                   jax.ShapeDtypeStruct((1, BW), F32), jax.ShapeDtypeStruct((N_MEM, D), F32)),
        in_specs=[row, row, out(BW), tile, halo, _const((N_MEM, D)), _const((4, TS, HALO + TS)), _const((BW, BW)), vec,
                  _const((CONF, BW)), vec, vec, vec, _const((BW, BW)), vec, _resident((MIX, D)), _const((1, D))],
        out_specs=(out(XA + MIX), out(BW), out(BW), out(D), out(MIX),
                   _const((1, D)), vec, _const((BW, BW)), _const((8 * CONF, BW)), vec, vec, vec, _const((BW, BW)), vec,
                   _const((N_MEM, D))),
        scratch_shapes=[pltpu.VMEM((HALO + TS, BW), F32), pltpu.VMEM((7, SHIFT_ROWS, BW), F32),
                        pltpu.VMEM((TS, BW), F32)],
        compiler_params=_cp(("arbitrary",)),
    )(dx, o, cv, p, p, kv, _band_matrices(TS, False), wbd, cscale, dww, dwb, ln_g, ln_b, pww, pwb, wout, post_g)


def _halo_next(nblk_per_tile, nblk):
    return lambda i: (jnp.minimum((i + 1) * nblk_per_tile, nblk - 1), 0)


def _pre_norm_bwd(dh, x, pre_g, dres, dpre_ref):
    r = lax.rsqrt(jnp.mean(x * x, axis=-1, keepdims=True) + EPS)
    xh = x * r
    dpre_ref[...] += jnp.sum(dh * xh, axis=0, keepdims=True)
    dxh = dh * pre_g
    return dres + r * (dxh - xh * jnp.mean(dxh * xh, axis=-1, keepdims=True))


def _even_bwd2(dp, w_t, x, pre_g, dres, rider=None):
    s = x.shape[0]
    tm = min(512, s)

    def body(dp_ref, w_ref, x_ref, pg, dres_ref, dx_ref, dpre_ref):
        _acc_init(pl.program_id(0), (dpre_ref,))
        dh = _dot(dp_ref[...], w_ref[...])
        dx_ref[...] = _pre_norm_bwd(dh, x_ref[...], pg[...], dres_ref[...], dpre_ref)

    row = pl.BlockSpec((tm, D), lambda i: (i, 0))
    return _host_call(
        body, grid=(s // tm,), name="even_bwd2", rider=rider,
        out_shape=(jax.ShapeDtypeStruct((s, D), F32), jax.ShapeDtypeStruct((1, D), F32)),
        in_specs=[pl.BlockSpec((tm, EVEN_IN), lambda i: (i, 0)), _resident((EVEN_IN, D)), row, _const((1, D)), row],
        out_specs=(row, _const((1, D))),
        args=(dp, w_t, x, pre_g, dres))


def _odd_bwd2(dpc, tmpc, tmpd, p, dww, w_t, x, pre_g, dres):
    s = x.shape[0]
    nt = s // TS

    def body(dpc_ref, tc_ref, tch_ref, td_ref, tdh_ref, ga_ref, gb_ref, bands_ref, dww_ref, w_ref, x_ref, pg,
             dres_ref, dpb_ref, dx_ref, dpre_ref, dbuf, dsh):
        i = pl.program_id(0)
        _acc_init(i, (dpre_ref,))
        more = i < nt - 1
        e_bf = tc_ref[...]
        eh = tch_ref[...]
        ecat = jnp.concatenate([e_bf, jnp.where(more, eh, jnp.zeros_like(eh))], axis=0)
        dbuf[0:TS, :] = td_ref[...].astype(F32)
        dbuf[TS:TS + HALO, :] = jnp.where(more, tdh_ref[...].astype(F32), 0.0)
        sums = [_dot(bands_ref[w], ecat) for w in range(len(POOL_WINDOWS))]
        rows = _row_ids(i, TS) + 1
        cnt = _pool_select([jnp.minimum(rows, w).astype(F32) for w in POOL_WINDOWS])
        dzc = (_pool_select(sums) - e_bf.astype(F32) * cnt).astype(BF16)
        _shifted_copies(dbuf, dsh)
        dz = dww_ref[CONF - 1:CONF, :] * dbuf[pl.ds(0, TS), :]
        for sft in range(1, CONF):
            dz = dz + dww_ref[CONF - 1 - sft:CONF - sft, :] * _rows_at(dbuf, dsh, sft, TS)
        ga = ga_ref[...].astype(F32)
        sgb = _sigmoid(gb_ref[...].astype(F32))
        dga = (dz * sgb).astype(BF16)
        dgb = (dz * ga * sgb * (1.0 - sgb)).astype(BF16)
        dpb_ref[:, 0:BW] = dzc
        dpb_ref[:, BW:2 * BW] = dga
        dpb_ref[:, 2 * BW:3 * BW] = dgb
        dh = (_dot(dzc, w_ref[0:BW, :]) + _dot(dga, w_ref[BW:2 * BW, :]) + _dot(dgb, w_ref[2 * BW:3 * BW, :])
              + _dot(dpc_ref[...], w_ref[3 * BW:ODD_IN, :]))
        dx_ref[...] = _pre_norm_bwd(dh, x_ref[...], pg[...], dres_ref[...], dpre_ref)

    row = pl.BlockSpec((TS, D), lambda i: (i, 0))

    def tile(n, j=0):
        return pl.BlockSpec((TS, n), lambda i: (i, j))

    nxt = pl.BlockSpec((HALO, BW), _halo_next(TS // HALO, s // HALO))
    return pl.pallas_call(
        body, grid=(nt,), name="odd_bwd2",
        out_shape=(jax.ShapeDtypeStruct((s, 3 * BW), BF16), jax.ShapeDtypeStruct((s, D), F32),
                   jax.ShapeDtypeStruct((1, D), F32)),
        in_specs=[tile(XA + MIX), tile(BW), nxt, tile(BW), nxt, tile(BW, 1), tile(BW, 2), _const((4, TS, HALO + TS)),
                  _const((CONF, BW)), _resident((ODD_IN, D)), row, _const((1, D)), row],
        out_specs=(tile(3 * BW), row, _const((1, D))),
        scratch_shapes=[pltpu.VMEM((TS + HALO, BW), F32), pltpu.VMEM((7, SHIFT_ROWS, BW), F32)],
        compiler_params=_cp(("arbitrary",)),
    )(dpc, tmpc, tmpc, tmpd, tmpd, p, p, _band_matrices(TS, True), dww, w_t, x, pre_g, dres)


def _grad_tn(a, b, tm, out=None, rows=None, row0=0, name="grad_tn", rider=None):
    s, m = a.shape
    n = b.shape[1]
    ts = min(2048, s)
    rows = m if rows is None else rows
    assert m % tm == 0 and s % ts == 0
    ns = s // ts
    if row0 % tm == 0:
        out_spec = pl.BlockSpec((tm, n), lambda i, k: (row0 // tm + i, 0))
    else:
        align = 16
        assert row0 % align == 0 and tm % align == 0
        out_spec = pl.BlockSpec((pl.Element(tm), pl.Element(n)),
                                lambda i, k: (pl.multiple_of(row0 + i * tm, align), 0))

    def body(*refs):
        a_ref, b_ref = refs[0], refs[1]
        o_ref, acc = refs[-2], refs[-1]
        k = pl.program_id(1)

        @pl.when(k == 0)
        def _():
            acc[...] = jnp.zeros_like(acc)

        acc[...] += _dot_tn(a_ref[...], b_ref[...])

        @pl.when(k == ns - 1)
        def _():
            o_ref[...] = acc[...].astype(BF16)

    in_specs = [pl.BlockSpec((ts, tm), lambda i, k: (k, i)), pl.BlockSpec((ts, n), lambda i, k: (k, 0))]
    args = [a, b]
    aliases = {}
    if out is not None:
        in_specs.append(pl.BlockSpec(memory_space=pltpu.HBM))
        args.append(out)
        aliases = {2: 0}
    (res,), got = _host_call(
        body, grid=(m // tm, ns), name=name, rider=rider, aliases=aliases,
        out_shape=(jax.ShapeDtypeStruct((rows, n), BF16),), in_specs=in_specs, out_specs=(out_spec,),
        scratch_shapes=[pltpu.VMEM((tm, n), F32)], args=args)
    return res if rider is None else (res, got)


def _place():
    x, y, c = lax.axis_index("x"), lax.axis_index("y"), lax.axis_index("c")
    chips = [(1 - x, y), (x, 1 - y), (1 - x, 1 - y)]
    return x, y, c, chips


def _hbm_specs(n):
    return [pl.BlockSpec(memory_space=pltpu.HBM)] * n


def _row_tile(r):
    for cand in (512, 400, 304, 256, 192, 128, 96, 16):
        if r % cand == 0:
            return cand
    raise ValueError(r)


def _place_shard(shard, place, dtype, name, after=None):
    r, cc = shard.shape
    tr = _row_tile(r)
    nt = r // tr

    def body(place_ref, s_ref, *rest):
        rest[-1][...] = s_ref[...].astype(dtype)

    in_specs = [pl.BlockSpec((tr, cc), lambda i, pr: (i, 0))]
    args = [shard]
    if after is not None:
        in_specs.append(pl.BlockSpec(after.shape, lambda i, pr: (0, 0)))
        args.append(after)
    return pl.pallas_call(
        body, name=name, out_shape=jax.ShapeDtypeStruct((N_CHIPS * r, cc), dtype),
        grid_spec=pltpu.PrefetchScalarGridSpec(
            num_scalar_prefetch=1, grid=(nt,), in_specs=in_specs,
            out_specs=pl.BlockSpec((tr, cc), lambda i, pr: (pr[1] * nt + i, 0))),
        compiler_params=_cp(("arbitrary",)),
    )(place, *args)


class _GatherRider:
    has_mid = True

    def __init__(self, fulls):
        n = len(fulls)
        self.inputs = list(fulls)
        self.out_shapes = [jax.ShapeDtypeStruct(a.shape, a.dtype) for a in fulls]
        self.aliases = {a: a for a in range(n)}
        self.sems = [pltpu.SemaphoreType.DMA((6 * n,)), pltpu.SemaphoreType.DMA((6 * n,))]
        self.block_rows = [a.shape[0] // N_CHIPS for a in fulls]

    def _ctx(self, outs, sems):
        send_sems, recv_sems = sems
        x, y, c, chips = _place()

        def rows(a, k, half):
            r = self.block_rows[a]
            return outs[a].at[pl.ds(k * r + half * (r // 2), r // 2)]

        def copy(a, j, blk, to):
            return pltpu.make_async_remote_copy(src_ref=blk, dst_ref=blk, send_sem=send_sems.at[a * 6 + j],
                                                recv_sem=recv_sems.at[a * 6 + j], device_id=to, device_id_type=MESH)

        return x, y, c, chips, rows, copy

    def start(self, ins, outs, sems, peers=(0, 1, 2)):
        x, y, c, chips, rows, copy = self._ctx(outs, sems)
        for j in peers:
            for a in range(len(outs)):
                copy(a, j, rows(a, 2 * x + y, c), (*chips[j], c)).start()

    def mid(self, ins, outs, sems, peers=(0, 1, 2)):
        x, y, c, chips, rows, copy = self._ctx(outs, sems)
        for j in peers:
            px, py = chips[j]
            for a in range(len(outs)):
                copy(a, j, rows(a, 2 * px + py, c), (px, py, c)).wait_recv()
                copy(a, 3 + j, rows(a, 2 * px + py, c), (x, y, 1 - c)).start()

    def wait_forwarded(self, outs, sems, peers=(0, 1, 2)):
        x, y, c, chips, rows, copy = self._ctx(outs, sems)
        for j in peers:
            px, py = chips[j]
            for a in range(len(outs)):
                copy(a, 3 + j, rows(a, 2 * px + py, 1 - c), (x, y, 1 - c)).wait_recv()

    def wait_sends(self, outs, sems):
        x, y, c, chips, rows, copy = self._ctx(outs, sems)
        for j, (px, py) in enumerate(chips):
            for a in range(len(outs)):
                copy(a, j, rows(a, 2 * x + y, c), (px, py, c)).wait_send()
                copy(a, 3 + j, rows(a, 2 * px + py, c), (x, y, 1 - c)).wait_send()

    def end(self, ins, outs, sems):
        self.wait_forwarded(outs, sems)
        self.wait_sends(outs, sems)


def _swap_halves(grads, name, share=()):
    n, k = len(grads), len(share)
    m = n + k

    def body(*refs):
        ins, outs = refs[:m], refs[m:2 * m]
        send_sems, recv_sems = refs[2 * m:]
        x, y, c, _ = _place()
        sibling = (x, y, 1 - c)
        cps, waits = [], []
        for a in range(m):
            if a < n:
                cp = pltpu.make_async_remote_copy(src_ref=ins[a].at[:, 1 - c], dst_ref=outs[a],
                                                  send_sem=send_sems.at[a], recv_sem=recv_sems.at[a],
                                                  device_id=sibling, device_id_type=MESH)
                waits.append(cp)
            else:
                cp = pltpu.make_async_remote_copy(src_ref=outs[a].at[c], dst_ref=outs[a].at[c],
                                                  send_sem=send_sems.at[a], recv_sem=recv_sems.at[a],
                                                  device_id=sibling, device_id_type=MESH)
                waits.append(pltpu.make_async_remote_copy(
                    src_ref=outs[a].at[1 - c], dst_ref=outs[a].at[1 - c], send_sem=send_sems.at[a],
                    recv_sem=recv_sems.at[a], device_id=sibling, device_id_type=MESH))
            cp.start()
            cps.append(cp)
        for cp in waits:
            cp.wait_recv()
        for cp in cps:
            cp.wait_send()

    outs = tuple(jax.ShapeDtypeStruct((g.shape[0],) + g.shape[2:], g.dtype) for g in grads)
    outs += tuple(jax.ShapeDtypeStruct(g.shape, g.dtype) for g in share)
    res = pl.pallas_call(
        body, name=name, out_shape=outs, in_specs=_hbm_specs(m), out_specs=tuple(_hbm_specs(m)),
        input_output_aliases={n + a: n + a for a in range(k)},
        scratch_shapes=[pltpu.SemaphoreType.DMA((m,)), pltpu.SemaphoreType.DMA((m,))],
    )(*grads, *share)
    return tuple(res[:n]), tuple(res[n:])


def _pair_sum(g, recv, place, name):
    _, _, h, cc = g.shape
    th = h

    def body(c_ref, g_ref, r_ref, o_ref):
        o_ref[...] = (g_ref[...].astype(F32) + r_ref[...].astype(F32)).astype(o_ref.dtype)

    return pl.pallas_call(
        body, name=name, out_shape=jax.ShapeDtypeStruct(recv.shape, recv.dtype),
        grid_spec=pltpu.PrefetchScalarGridSpec(
            num_scalar_prefetch=1, grid=(N_CHIPS, h // th),
            in_specs=[pl.BlockSpec((None, None, th, cc), lambda k, r, c_ref: (k, c_ref[0], r, 0)),
                      pl.BlockSpec((None, th, cc), lambda k, r, c_ref: (k, r, 0))],
            out_specs=pl.BlockSpec((None, th, cc), lambda k, r, c_ref: (k, r, 0))),
        compiler_params=_cp(("arbitrary", "arbitrary")),
    )(place, g, recv)


def _finish_reduce(pack, halves):
    rows, cc = pack.shape
    hs = rows // 2
    n = len(halves)

    def body(*refs):
        pack_ref = refs[0]
        out_ref = refs[1 + n]
        big = refs[2 + n:2 + 2 * n]
        sib_ref, parts_ref, send_sems, recv_sems, big_send, big_recv = refs[2 + 2 * n:]
        x, y, c, chips = _place()
        me_k = 2 * x + y
        sibling = (x, y, 1 - c)
        mine = pl.ds(pl.multiple_of(c * hs, hs), hs)
        theirs = pl.ds(pl.multiple_of((1 - c) * hs, hs), hs)
        shared = [pltpu.make_async_remote_copy(src_ref=big[a].at[c], dst_ref=big[a].at[c], send_sem=big_send.at[a],
                                               recv_sem=big_recv.at[a], device_id=sibling, device_id_type=MESH)
                  for a in range(n)]
        first = pltpu.make_async_remote_copy(src_ref=pack_ref, dst_ref=sib_ref, send_sem=send_sems.at[0],
                                             recv_sem=recv_sems.at[0], device_id=sibling, device_id_type=MESH)
        first.start()
        first.wait()
        parts_ref[me_k] = pack_ref[mine, :] + sib_ref[mine, :]
        cps = [pltpu.make_async_remote_copy(src_ref=parts_ref.at[me_k], dst_ref=parts_ref.at[me_k],
                                            send_sem=send_sems.at[1 + j], recv_sem=recv_sems.at[1 + j],
                                            device_id=(px, py, c), device_id_type=MESH)
               for j, (px, py) in enumerate(chips)]
        for cp in cps:
            cp.start()
        for cp in shared:
            cp.start()
        for j, (px, py) in enumerate(chips):
            pltpu.make_async_remote_copy(src_ref=parts_ref.at[2 * px + py], dst_ref=parts_ref.at[2 * px + py],
                                         send_sem=send_sems.at[1 + j], recv_sem=recv_sems.at[1 + j],
                                         device_id=(px, py, c), device_id_type=MESH).wait_recv()
        for cp in cps:
            cp.wait_send()
        out_ref[mine, :] = ((parts_ref[0] + parts_ref[1]) + parts_ref[2]) + parts_ref[3]
        last = pltpu.make_async_remote_copy(src_ref=out_ref.at[mine], dst_ref=out_ref.at[mine],
                                            send_sem=send_sems.at[4], recv_sem=recv_sems.at[4], device_id=sibling,
                                            device_id_type=MESH)
        last.start()
        pltpu.make_async_remote_copy(src_ref=out_ref.at[theirs], dst_ref=out_ref.at[theirs],
                                     send_sem=send_sems.at[4], recv_sem=recv_sems.at[4], device_id=sibling,
                                     device_id_type=MESH).wait_recv()
        last.wait_send()
        for a in range(n):
            pltpu.make_async_remote_copy(src_ref=big[a].at[1 - c], dst_ref=big[a].at[1 - c], send_sem=big_send.at[a],
                                         recv_sem=big_recv.at[a], device_id=sibling,
                                         device_id_type=MESH).wait_recv()
        for cp in shared:
            cp.wait_send()

    vmem = pl.BlockSpec(memory_space=pltpu.VMEM)
    res = pl.pallas_call(
        body, name="finish_reduce",
        out_shape=(jax.ShapeDtypeStruct(pack.shape, pack.dtype),)
        + tuple(jax.ShapeDtypeStruct(g.shape, g.dtype) for g in halves),
        in_specs=[vmem] + _hbm_specs(n), out_specs=(vmem,) + tuple(_hbm_specs(n)),
        input_output_aliases={1 + a: 1 + a for a in range(n)},
        scratch_shapes=[pltpu.VMEM((rows, cc), F32), pltpu.VMEM((N_CHIPS, hs, cc), F32),
                        pltpu.SemaphoreType.DMA((5,)), pltpu.SemaphoreType.DMA((5,)),
                        pltpu.SemaphoreType.DMA((n,)), pltpu.SemaphoreType.DMA((n,))],
        compiler_params=_cp(),
    )(pack, *halves)
    return res[0], tuple(res[1:])


class _ExchangeRider:
    has_mid = False

    def __init__(self, sums):
        self.inputs = list(sums)
        self.out_shapes = [jax.ShapeDtypeStruct((3,) + g.shape[1:], g.dtype) for g in sums]
        m = len(self.inputs)
        self.aliases = {}
        self.sems = [pltpu.SemaphoreType.DMA((3 * m,)), pltpu.SemaphoreType.DMA((3 * m,))]

    def _copies(self, ins, outs, sems):
        send_sems, recv_sems = sems
        _, _, c, chips = _place()
        return [pltpu.make_async_remote_copy(
            src_ref=ins[a].at[2 * px + py], dst_ref=outs[a].at[j], send_sem=send_sems.at[a * 3 + j],
            recv_sem=recv_sems.at[a * 3 + j], device_id=(px, py, c), device_id_type=MESH)
            for j, (px, py) in enumerate(chips) for a in range(len(ins))]

    def start(self, ins, outs, sems):
        for cp in self._copies(ins, outs, sems):
            cp.start()

    def end(self, ins, outs, sems):
        cps = self._copies(ins, outs, sems)
        for cp in cps:
            cp.wait_recv()
        for cp in cps:
            cp.wait_send()


class _ShareRider:
    has_mid = False

    def __init__(self, halves):
        n = len(halves)
        self.inputs = list(halves)
        self.out_shapes = [jax.ShapeDtypeStruct(g.shape, g.dtype) for g in halves]
        self.aliases = {a: a for a in range(n)}
        self.sems = [pltpu.SemaphoreType.DMA((n,)), pltpu.SemaphoreType.DMA((n,))]

    def _copies(self, outs, sems, half):
        send_sems, recv_sems = sems
        x, y, c, _ = _place()
        h = c if half == "mine" else 1 - c
        return [pltpu.make_async_remote_copy(src_ref=outs[a].at[h], dst_ref=outs[a].at[h], send_sem=send_sems.at[a],
                                             recv_sem=recv_sems.at[a], device_id=(x, y, 1 - c), device_id_type=MESH)
                for a in range(len(outs))]

    def start(self, ins, outs, sems):
        for cp in self._copies(outs, sems, "mine"):
            cp.start()

    def end(self, ins, outs, sems):
        for cp in self._copies(outs, sems, "theirs"):
            cp.wait_recv()
        for cp in self._copies(outs, sems, "mine"):
            cp.wait_send()


class _Riders:
    def __init__(self, riders):
        self.riders = list(riders)
        self.inputs = [a for r in self.riders for a in r.inputs]
        self.out_shapes = [s for r in self.riders for s in r.out_shapes]
        self.sems = [s for r in self.riders for s in r.sems]
        self.has_mid = any(r.has_mid for r in self.riders)
        self.aliases = {}
        i0 = o0 = 0
        for r in self.riders:
            self.aliases.update({i0 + j: o0 + k for j, k in r.aliases.items()})
            i0 += len(r.inputs)
            o0 += len(r.out_shapes)

    def _each(self, ins, outs, sems):
        i0 = o0 = s0 = 0
        for r in self.riders:
            yield (r, ins[i0:i0 + len(r.inputs)], outs[o0:o0 + len(r.out_shapes)], sems[s0:s0 + len(r.sems)])
            i0, o0, s0 = i0 + len(r.inputs), o0 + len(r.out_shapes), s0 + len(r.sems)

    def start(self, ins, outs, sems):
        for r, i, o, s in self._each(ins, outs, sems):
            r.start(i, o, s)

    def mid(self, ins, outs, sems):
        for r, i, o, s in self._each(ins, outs, sems):
            if r.has_mid:
                r.mid(i, o, s)

    def end(self, ins, outs, sems):
        for r, i, o, s in self._each(ins, outs, sems):
            r.end(i, o, s)

    def split(self, outs):
        res, o0 = [], 0
        for r in self.riders:
            res.append(tuple(outs[o0:o0 + len(r.out_shapes)]))
            o0 += len(r.out_shapes)
        return res


def _chip_sum(own, parts, place, name):
    npart, h, cc = parts.shape
    th = _row_tile(h)

    def body(place_ref, own_ref, p_ref, o_ref):
        acc = own_ref[...].astype(F32) + p_ref[0].astype(F32)
        for k in range(1, npart):
            acc = acc + p_ref[k].astype(F32)
        o_ref[...] = acc

    return pl.pallas_call(
        body, name=name, out_shape=jax.ShapeDtypeStruct((2, h, cc), F32),
        grid_spec=pltpu.PrefetchScalarGridSpec(
            num_scalar_prefetch=1, grid=(h // th,),
            in_specs=[pl.BlockSpec((None, th, cc), lambda r, pr: (pr[1], r, 0)),
                      pl.BlockSpec((npart, th, cc), lambda r, pr: (0, r, 0))],
            out_specs=pl.BlockSpec((None, th, cc), lambda r, pr: (pr[0], r, 0))),
        compiler_params=_cp(("arbitrary",)),
    )(place, own, parts)


def _adamw_math(w, g, m, v):
    m = ADAM_B1 * m + (1.0 - ADAM_B1) * g
    v = ADAM_B2 * v + (1.0 - ADAM_B2) * (g * g)
    m_hat = m / (1.0 - ADAM_B1 ** ADAM_STEP)
    v_hat = v / (1.0 - ADAM_B2 ** ADAM_STEP)
    delta = -ADAM_LR * (m_hat / (jnp.sqrt(v_hat) + ADAM_EPS) + ADAM_WD * w)
    return delta, m, v


def _adamw_big(w, g, m, v, name):
    r, cc = w.shape
    tr = min(_row_tile(r), 256) if r % 256 == 0 else _row_tile(r)

    def body(w_ref, g_ref, m_ref, v_ref, go_ref, d_ref, mo_ref, vo_ref):
        g = g_ref[...]
        d, mm, vv = _adamw_math(w_ref[...], g, m_ref[...], v_ref[...])
        go_ref[...] = g
        d_ref[...] = d
        mo_ref[...] = mm
        vo_ref[...] = vv

    blk = pl.BlockSpec((tr, cc), lambda i: (i, 0))
    sd = jax.ShapeDtypeStruct((r, cc), F32)
    return pl.pallas_call(body, grid=(r // tr,), name=name, out_shape=(sd, sd, sd, sd), in_specs=[blk] * 4,
                          out_specs=(blk, blk, blk, blk), compiler_params=_cp(("arbitrary",)))(w, g, m, v)


SC_TILES = 32


def _adamw_sparsecore(ws, gs, ms, vs, name):
    n = len(ws)
    rows_per = 8
    widths = sorted({a.shape[1] for a in ws})
    assert all(a.shape[0] % rows_per == 0 and a.shape[1] % 16 == 0 for a in ws)

    def body(*refs):
        ins, outs, bufs = refs[:4 * n], refs[4 * n:8 * n], refs[8 * n:]
        tile = lax.axis_index("sc_tile") * 2 + lax.axis_index("sc_core")
        for a in range(n):
            w_hbm, g_hbm, m_hbm, v_hbm = ins[4 * a:4 * a + 4]
            go_hbm, d_hbm, mo_hbm, vo_hbm = outs[4 * a:4 * a + 4]
            r, cc = ws[a].shape
            k = widths.index(cc)
            wb, gb, mb, vb, db = bufs[5 * k:5 * k + 5]
            groups = r // rows_per

            @pl.loop(0, -(-groups // SC_TILES))
            def _(q):
                grp = tile + q * SC_TILES

                @pl.when(grp < groups)
                def _():
                    rows = pl.ds(pl.multiple_of(grp * rows_per, rows_per), rows_per)
                    pltpu.sync_copy(w_hbm.at[rows], wb)
                    pltpu.sync_copy(g_hbm.at[rows], gb)
                    pltpu.sync_copy(m_hbm.at[rows], mb)
                    pltpu.sync_copy(v_hbm.at[rows], vb)

                    @pl.loop(0, rows_per)
                    def _(i):
                        @pl.loop(0, cc, step=16)
                        def _(j):
                            at = (i, pl.ds(j, 16))
                            d, mm, vv = _adamw_math(wb[at], gb[at], mb[at], vb[at])
                            db[at] = d
                            mb[at] = mm
                            vb[at] = vv

                    pltpu.sync_copy(gb, go_hbm.at[rows])
                    pltpu.sync_copy(db, d_hbm.at[rows])
                    pltpu.sync_copy(mb, mo_hbm.at[rows])
                    pltpu.sync_copy(vb, vo_hbm.at[rows])

    args, out_type = [], []
    for a in range(n):
        args += [ws[a], gs[a], ms[a], vs[a]]
        out_type += [jax.ShapeDtypeStruct(ws[a].shape, F32)] * 4
    res = pl.kernel(
        body, name=name, out_type=tuple(out_type),
        mesh=plsc.VectorSubcoreMesh(core_axis_name="sc_core", subcore_axis_name="sc_tile"),
        scratch_types=[pltpu.VMEM((rows_per, cc), F32) for cc in widths for _ in range(5)],
    )(*args)
    return [tuple(res[4 * a:4 * a + 4]) for a in range(n)]


def _adamw_small(ws, gs, ms, vs):
    n = len(ws)

    def body(*refs):
        for a in range(n):
            w_ref, g_ref, m_ref, v_ref = refs[4 * a:4 * a + 4]
            d_ref, mo_ref, vo_ref = refs[4 * n + 3 * a:4 * n + 3 * a + 3]
            d, mm, vv = _adamw_math(w_ref[...], g_ref[...], m_ref[...], v_ref[...])
            d_ref[...] = d
            mo_ref[...] = mm
            vo_ref[...] = vv

    args, outs = [], []
    for a in range(n):
        args += [ws[a], gs[a], ms[a], vs[a]]
        outs += [jax.ShapeDtypeStruct(ws[a].shape, F32)] * 3
    res = pl.pallas_call(body, name="adamw_small", out_shape=tuple(outs), compiler_params=_cp())(*args)
    return [res[3 * a:3 * a + 3] for a in range(n)]


def _flat_pack(arrs, rows):
    flat = jnp.concatenate([a.reshape(-1) for a in arrs])
    return jnp.pad(flat, (0, rows * D - flat.shape[0])).reshape(rows, D)


def _flat_unpack(flat, shapes):
    out, off = [], 0
    for shp in shapes:
        size = 1
        for d_ in shp:
            size *= d_
        out.append(flat[off:off + size].reshape(shp))
        off += size
    return out


SMALL_EVEN = ("even_pre_g", "even_a_ln_g", "even_a_ln_b", "even_a_ws", "even_a_bs", "even_b_conv", "even_mem_g",
              "even_post_g")
SMALL_ODD = ("odd_pre_g", "odd_c_wgrp", "odd_c_scale", "odd_d_dw_w", "odd_d_dw_b", "odd_d_ln_g", "odd_d_ln_b",
             "odd_d_pw_b", "odd_mem_g", "odd_post_g")
BIG = ("even_w_in", "even_w_kv", "even_w_out", "odd_w_in", "odd_d_pw_w", "odd_w_kv", "odd_w_out")
WEIGHTS = ("even_pre_g", "even_w_in", "even_a_ln_g", "even_a_ln_b", "even_a_ws", "even_a_bs", "even_b_conv",
           "even_mem_g", "even_w_kv", "even_w_out", "even_post_g", "odd_pre_g", "odd_w_in", "odd_c_wgrp",
           "odd_c_scale", "odd_d_dw_w", "odd_d_dw_b", "odd_d_ln_g", "odd_d_ln_b", "odd_d_pw_w", "odd_d_pw_b",
           "odd_mem_g", "odd_w_kv", "odd_w_out", "odd_post_g")
PACKED = (("even_b_conv", (3, 192)), ("odd_pre_g", (1, 256)), ("odd_c_scale", (1, 192)), ("odd_d_dw_w", (31, 192)),
          ("odd_d_dw_b", (1, 192)), ("odd_d_ln_g", (1, 192)), ("odd_d_ln_b", (1, 192)), ("odd_d_pw_b", (1, 192)),
          ("odd_mem_g", (1, 256)), ("odd_post_g", (1, 256)))
PACK_ROWS = 16
SMALL_ROWS = 256


def _four(g):
    return g.reshape(N_CHIPS, 2, g.shape[0] // (2 * N_CHIPS), g.shape[1])


def _step(x, mem, target, w, mom, var, place):
    wt = {}
    pack = _flat_pack([w[n][0] for n, _ in PACKED], PACK_ROWS)
    shards = {"even_w_in_t": w["even_w_in"][0].T, "odd_w_in_t": w["odd_w_in"][0].T, "even_w_kv": w["even_w_kv"][0],
              "odd_w_kv": w["odd_w_kv"][0], "even_w_out": w["even_w_out"][0], "odd_w_out": w["odd_w_out"][0],
              "odd_d_pw_w": w["odd_d_pw_w"][0]}
    placed = {n: _place_shard(shards[n], place, BF16, "place_" + n) for n in ("even_w_in_t", "even_w_kv", "even_w_out")}
    placed["pack"] = _place_shard(pack, place, F32, "place_pack")

    order, group = _stream_tables(place[0], place[1], EVEN_IN)
    p_e, h_e, (wt["even_w_in_t"], packs), (wt["even_w_kv"], wt["even_w_out"]) = _in_fwd_streamed(
        x, w["even_pre_g"], [placed["even_w_in_t"], placed["pack"]], [placed["even_w_kv"], placed["even_w_out"]],
        order, group, "even_in_streamed")
    for n in ("odd_w_in_t", "odd_w_kv", "odd_w_out", "odd_d_pw_w"):
        placed[n] = _place_shard(shards[n], place, BF16, "place_" + n, after=p_e[0:16, 0:128])
    packs = packs.reshape(N_CHIPS, PACK_ROWS * D)
    per_chip = [_flat_unpack(packs[k], [shp for _, shp in PACKED]) for k in range(N_CHIPS)]
    for a, (name, _) in enumerate(PACKED):
        wt[name] = jnp.concatenate([per_chip[k][a] for k in range(N_CHIPS)], axis=-1)
    for name in ("even_pre_g", "even_a_ln_g", "even_a_ln_b", "even_mem_g", "even_post_g"):
        wt[name] = w[name]

    tril = jnp.tril(jnp.ones((CH, CH), dtype=bool))
    wcat = jnp.where(tril[None], w["even_a_ws"][0], 0.0).transpose(1, 0, 2).reshape(CH, 4 * CH).astype(BF16)
    bsg = jnp.repeat(w["even_a_bs"][0].T, BW // 4, axis=1)
    hsel = (jnp.arange(BW)[:, None] // (BW // 4) == jnp.arange(128)[None, :]).astype(BF16)
    g4 = BW // 4
    eye = jnp.eye(4, dtype=F32)
    wbd = (w["odd_c_wgrp"][0][:, :, None, :] * eye[:, None, :, None]).reshape(BW, BW).astype(BF16)

    kv_e = _kv_fwd(mem, wt["even_mem_g"], wt["even_w_kv"], "even_kv")
    (x1, o_e, y_e), (wt["odd_w_in_t"],) = _even_fwd(
        x, p_e, kv_e, wt["even_a_ln_g"], wt["even_a_ln_b"], wcat, bsg, wt["even_b_conv"], wt["even_w_out"],
        wt["even_post_g"], rider=_GatherRider([placed["odd_w_in_t"]]))
    names = ("odd_w_out", "odd_d_pw_w", "odd_w_kv")
    (p_o, h_o), got = _in_fwd(x1, wt["odd_pre_g"], wt["odd_w_in_t"], "odd_in",
                              rider=_GatherRider([placed[n] for n in names]))
    wt.update(zip(names, got))
    kv_o = _kv_fwd(mem, wt["odd_mem_g"], wt["odd_w_kv"], "odd_kv")
    dx2, o_o, cv_o, loss = _odd_fwd(x1, p_o, kv_o, wbd, wt["odd_c_scale"], wt["odd_d_dw_w"], wt["odd_d_dw_b"],
                                    wt["odd_d_ln_g"], wt["odd_d_ln_b"], wt["odd_d_pw_w"], wt["odd_d_pw_b"],
                                    wt["odd_w_out"], wt["odd_post_g"], target)
    (dpc_o, tmpc, tmpd, do_o, y_o, g_post_o, g_cs, g_wbd, g_dww, g_dwb, g_lng_o, g_lnb_o, g_pww, g_pwb,
     dkv_o) = _odd_bwd1(dx2, o_o, cv_o, p_o, kv_o, wbd, wt["odd_c_scale"], wt["odd_d_dw_w"], wt["odd_d_dw_b"],
                        wt["odd_d_ln_g"], wt["odd_d_ln_b"], wt["odd_d_pw_w"], wt["odd_d_pw_b"], wt["odd_w_out"],
                        wt["odd_post_g"])
    dpb_o, dx1, g_pre_o = _odd_bwd2(dpc_o, tmpc, tmpd, p_o, wt["odd_d_dw_w"], wt["odd_w_in_t"], x1,
                                    wt["odd_pre_g"], dx2)
    g_win_o = _grad_tn(dpb_o, h_o, 768, rows=ODD_IN, name="odd_gw_in_b")
    g_win_o = _grad_tn(dpc_o, h_o, 1280, out=g_win_o, rows=ODD_IN, row0=3 * BW, name="odd_gw_in_c")
    g_wout_o = _grad_tn(y_o, do_o, 1024, name="odd_gw_out")
    g_wkv_o, g_memg_o = _kv_bwd(mem, wt["odd_mem_g"], wt["odd_w_kv"], dkv_o, "odd_kv_bwd")
    big_o = [_four(g) for g in (g_win_o, g_pww.astype(BF16), g_wkv_o, g_wout_o)]
    recv_o, _ = _swap_halves(big_o, "swap_halves_odd")
    sums_o = [_pair_sum(big_o[a], recv_o[a], place, "pair_sum_odd_%d" % a) for a in range(len(big_o))]
    (dp_e, do_e, g_post_e, g_lng_e, g_lnb_e, g_wcat, g_bs, g_bconv,
     dkv_e), parts_o = _even_bwd1(dx1, o_e, p_e, kv_e, wt["even_a_ln_g"], wt["even_a_ln_b"], wcat, bsg, hsel,
                                  wt["even_b_conv"], wt["even_w_out"], wt["even_post_g"],
                                  rider=_ExchangeRider(sums_o))
    halves_o = [_chip_sum(sums_o[a], parts_o[a], place, "chip_sum_odd_%d" % a) for a in range(len(big_o))]
    g_wout_e = _grad_tn(y_e, do_e, 1024, name="even_gw_out")
    g_wkv_e, g_memg_e = _kv_bwd(mem, wt["even_mem_g"], wt["even_w_kv"], dkv_e, "even_kv_bwd")
    big_x = [_four(g) for g in (g_wkv_e, g_wout_e)]
    recv_x, _ = _swap_halves(big_x, "swap_halves_kv_out")
    sums_x = [_pair_sum(big_x[a], recv_x[a], place, "pair_sum_kv_out_%d" % a) for a in range(len(big_x))]
    riders = _Riders([_ExchangeRider(sums_x), _ShareRider(halves_o)])
    g_win_e, got = _grad_tn(dp_e, h_e, 1280, name="even_gw_in", rider=riders)
    parts_x, full_o = riders.split(got)

    def sparsecore_adamw(names, fulls, name):
        as_kept = [(lambda t: t.T) if n.endswith("w_in") else (lambda t: t) for n in names]
        res = _adamw_sparsecore([f(w[n][0]) for f, n in zip(as_kept, names)],
                                [g_.reshape(g_.shape[1] * 2, g_.shape[2]) for g_ in fulls],
                                [f(mom[n][0]) for f, n in zip(as_kept, names)],
                                [f(var[n][0]) for f, n in zip(as_kept, names)], name)
        return {n: tuple(f(t) for t in r_) for f, n, r_ in zip(as_kept, names, res)}

    upd_sc = sparsecore_adamw(("odd_w_in", "odd_d_pw_w", "odd_w_kv", "odd_w_out"), full_o, "adamw_odd_sparsecore")
    halves_x = [_chip_sum(sums_x[a], parts_x[a], place, "chip_sum_kv_out_%d" % a) for a in range(len(big_x))]
    big_e = [_four(g_win_e)]
    recv_e, full_x = _swap_halves(big_e, "swap_halves_even", share=halves_x)
    upd_sc.update(sparsecore_adamw(("even_w_kv", "even_w_out"), full_x, "adamw_kv_out_sparsecore"))
    sums_e = [_pair_sum(big_e[0], recv_e[0], place, "pair_sum_even_w_in")]
    (dx0, g_pre_e), parts_e = _even_bwd2(dp_e, wt["even_w_in_t"], x, wt["even_pre_g"], dx1,
                                         rider=_ExchangeRider(sums_e))
    halves_e = [_chip_sum(sums_e[0], parts_e[0], place, "chip_sum_even_w_in")]

    g_aws = jnp.where(tril[None], g_wcat.reshape(CH, 4, CH).transpose(1, 0, 2), 0.0)
    g_wgrp = jnp.stack([lax.dynamic_slice(g_wbd, (g * g4, g * g4), (g4, g4)) for g in range(4)])
    small = {
        "even_pre_g": g_pre_e, "even_a_ln_g": g_lng_e, "even_a_ln_b": g_lnb_e, "even_a_ws": g_aws,
        "even_a_bs": g_bs[:, 0:4].T, "even_b_conv": g_bconv[0:3], "even_mem_g": g_memg_e, "even_post_g": g_post_e,
        "odd_pre_g": g_pre_o, "odd_c_wgrp": g_wgrp, "odd_c_scale": g_cs, "odd_d_dw_w": g_dww.reshape(CONF, 8, BW).sum(axis=1),
        "odd_d_dw_b": g_dwb, "odd_d_ln_g": g_lng_o, "odd_d_ln_b": g_lnb_o, "odd_d_pw_b": g_pwb,
        "odd_mem_g": g_memg_o, "odd_post_g": g_post_o,
    }
    small_names = SMALL_EVEN + SMALL_ODD
    small_pack = _flat_pack([small[n] for n in small_names] + [loss[0, 0].reshape(1)], SMALL_ROWS)
    small_total, full = _finish_reduce(small_pack, halves_e)
    gbig = {"even_w_in": full[0].reshape(full[0].shape[1] * 2, full[0].shape[2])}
    return dx0, gbig, upd_sc, small_total.reshape(-1), [small[n].shape for n in small_names]


def kernel(x, mem, even_pre_g, even_w_in, even_a_ln_g, even_a_ln_b, even_a_ws, even_a_bs, even_b_conv, even_mem_g, even_w_kv, even_w_out, even_post_g, odd_pre_g, odd_w_in, odd_c_wgrp, odd_c_scale, odd_d_dw_w, odd_d_dw_b, odd_d_ln_g, odd_d_ln_b, odd_d_pw_w, odd_d_pw_b, odd_mem_g, odd_w_kv, odd_w_out, odd_post_g, loss_target, m_even_pre_g, m_even_w_in, m_even_a_ln_g, m_even_a_ln_b, m_even_a_ws, m_even_a_bs, m_even_b_conv, m_even_mem_g, m_even_w_kv, m_even_w_out, m_even_post_g, m_odd_pre_g, m_odd_w_in, m_odd_c_wgrp, m_odd_c_scale, m_odd_d_dw_w, m_odd_d_dw_b, m_odd_d_ln_g, m_odd_d_ln_b, m_odd_d_pw_w, m_odd_d_pw_b, m_odd_mem_g, m_odd_w_kv, m_odd_w_out, m_odd_post_g, v_even_pre_g, v_even_w_in, v_even_a_ln_g, v_even_a_ln_b, v_even_a_ws, v_even_a_bs, v_even_b_conv, v_even_mem_g, v_even_w_kv, v_even_w_out, v_even_post_g, v_odd_pre_g, v_odd_w_in, v_odd_c_wgrp, v_odd_c_scale, v_odd_d_dw_w, v_odd_d_dw_b, v_odd_d_ln_g, v_odd_d_ln_b, v_odd_d_pw_w, v_odd_d_pw_b, v_odd_mem_g, v_odd_w_kv, v_odd_w_out, v_odd_post_g):
    given = dict(locals())
    w = {n: given[n] for n in WEIGHTS}
    mom = {n: given["m_" + n] for n in WEIGHTS}
    var = {n: given["v_" + n] for n in WEIGHTS}

    x_, y_, c_ = lax.axis_index("x"), lax.axis_index("y"), lax.axis_index("c")
    chip = 2 * x_ + y_
    place = jnp.stack([c_, chip]).astype(jnp.int32)
    grad_x, gbig, upd_odd, gsmall_flat, small_shapes = _step(x[0], mem[0], loss_target[0], w, mom, var, place)

    names = SMALL_EVEN + SMALL_ODD
    grads = {}
    unpacked = _flat_unpack(gsmall_flat, small_shapes + [(1,)])
    loss = unpacked[-1][0]
    for n, g in zip(names, unpacked[:-1]):
        shard_shape = w[n].shape[1:]
        if g.shape[-1] != shard_shape[-1]:
            g = lax.dynamic_slice_in_dim(g, chip * shard_shape[-1], shard_shape[-1], axis=g.ndim - 1)
        grads[n] = g.reshape(shard_shape)

    def two_d(a):
        return a.reshape(-1, a.shape[-1])

    upd = {}
    for n in BIG:
        if n in upd_odd:
            res = upd_odd[n]
        elif n.endswith("w_in"):
            res = _adamw_big(w[n][0].T, gbig[n], mom[n][0].T, var[n][0].T, "adamw_" + n)
            res = tuple(r.T for r in res)
        else:
            res = _adamw_big(w[n][0], gbig[n], mom[n][0], var[n][0], "adamw_" + n)
        grads[n], upd[n] = res[0], res[1:]
    res = _adamw_small([two_d(w[n][0]) for n in names], [two_d(grads[n]) for n in names],
                       [two_d(mom[n][0]) for n in names], [two_d(var[n][0]) for n in names])
    for n, r in zip(names, res):
        upd[n] = r

    outs = [loss, grad_x[None]]
    outs += [grads[n].reshape(w[n].shape) for n in WEIGHTS]
    for j in range(3):
        outs += [upd[n][j].reshape(w[n].shape) for n in WEIGHTS]
    return tuple(outs)
```

```python
import jax
import jax.numpy as jnp
from jax import lax
from jax.experimental import pallas as pl
from jax.experimental.pallas import tpu as pltpu
from jax.experimental.pallas import tpu_sc as plsc

F32 = jnp.float32
BF16 = jnp.bfloat16
MESH = pl.DeviceIdType.MESH

D = 1024
N_MEM = 256
MIX = 2048
XA = 512
HD = 128
BW = 768
CH = 128
EPS = 1e-6
SCALE = HD ** -0.5
POOL_WINDOWS = (2, 4, 8, 16)
CONF = 31
EVEN_IN = 6400
ODD_IN = 4864
N_CHIPS = 4

ADAM_LR = 0.001
ADAM_B1 = 0.9
ADAM_B2 = 0.999
ADAM_EPS = 1e-08
ADAM_WD = 0.01
ADAM_STEP = 10

TS = 256
HALO = 32
VMEM_LIMIT = 56 * 1024 * 1024


def _cp(sem=None):
    return pltpu.CompilerParams(dimension_semantics=sem, vmem_limit_bytes=VMEM_LIMIT)


def _dot(a, b):
    return jnp.dot(a, b, preferred_element_type=F32)


def _dot_nt(a, b):
    return lax.dot_general(a, b, (((1,), (1,)), ((), ())), preferred_element_type=F32)


def _dot_tn(a, b):
    return lax.dot_general(a, b, (((0,), (0,)), ((), ())), preferred_element_type=F32)


def _sigmoid(x):
    return 1.0 / (1.0 + jnp.exp(-x))


def _resident(shape):
    return pl.BlockSpec(shape, lambda *_: (0,) * len(shape), pipeline_mode=pl.Buffered(1))


def _const(shape):
    return pl.BlockSpec(shape, lambda *_: (0,) * len(shape))


def _kv_fwd(mem, mem_g, wkv, name):
    def body(mem_ref, g_ref, w_ref, kv_ref):
        m = mem_ref[...]
        r = lax.rsqrt(jnp.mean(m * m, axis=-1, keepdims=True) + EPS)
        mn = (m * r * g_ref[...]).astype(BF16)
        kv_ref[...] = _dot(mn, w_ref[...]).astype(BF16)

    return pl.pallas_call(body, out_shape=jax.ShapeDtypeStruct((N_MEM, D), BF16), name=name,
                          compiler_params=_cp())(mem, mem_g, wkv)


def _kv_bwd(mem, mem_g, wkv, dkv, name):
    def body(mem_ref, g_ref, w_ref, dkv_ref, dw_ref, dg_ref):
        m = mem_ref[...]
        r = lax.rsqrt(jnp.mean(m * m, axis=-1, keepdims=True) + EPS)
        mh = m * r
        mn = (mh * g_ref[...]).astype(BF16)
        dkv = dkv_ref[...].astype(BF16)
        dw_ref[...] = _dot_tn(mn, dkv).astype(BF16)
        dmn = _dot_nt(dkv, w_ref[...])
        dg_ref[...] = jnp.sum(dmn * mh, axis=0, keepdims=True)

    return pl.pallas_call(body, out_shape=(jax.ShapeDtypeStruct((D, D), BF16), jax.ShapeDtypeStruct((1, D), F32)),
                          name=name, compiler_params=_cp())(mem, mem_g, wkv, dkv)


def _host_call(body, *, grid, name, out_shape, in_specs, out_specs, args, scratch_shapes=(), aliases=None,
               rider=None):
    sem = ("arbitrary",) * len(grid)
    aliases = dict(aliases or {})
    if rider is None:
        res = pl.pallas_call(body, grid=grid, name=name, out_shape=tuple(out_shape), in_specs=list(in_specs),
                             out_specs=tuple(out_specs), scratch_shapes=list(scratch_shapes),
                             input_output_aliases=aliases, compiler_params=_cp(sem))(*args)
        return tuple(res), ()
    n_in, n_out, n_sc = len(in_specs), len(out_specs), len(scratch_shapes)
    r_in, r_out = len(rider.inputs), len(rider.out_shapes)

    def full_body(*refs):
        host_in = refs[:n_in]
        rid_in = refs[n_in:n_in + r_in]
        host_out = refs[n_in + r_in:n_in + r_in + n_out]
        rid_out = refs[n_in + r_in + n_out:n_in + r_in + n_out + r_out]
        host_sc = refs[n_in + r_in + n_out + r_out:n_in + r_in + n_out + r_out + n_sc]
        sems = refs[n_in + r_in + n_out + r_out + n_sc:]
        first = pl.program_id(0) == 0
        last = pl.program_id(0) == grid[0] - 1
        for ax in range(1, len(grid)):
            first = jnp.logical_and(first, pl.program_id(ax) == 0)
            last = jnp.logical_and(last, pl.program_id(ax) == grid[ax] - 1)

        @pl.when(first)
        def _():
            rider.start(rid_in, rid_out, sems)

        if rider.has_mid:
            @pl.when(last)
            def _():
                rider.mid(rid_in, rid_out, sems)

        body(*host_in, *host_out, *host_sc)

        @pl.when(last)
        def _():
            rider.end(rid_in, rid_out, sems)

    aliases.update({n_in + j: n_out + k for j, k in rider.aliases.items()})
    res = pl.pallas_call(
        full_body, grid=grid, name=name, out_shape=tuple(out_shape) + tuple(rider.out_shapes),
        in_specs=list(in_specs) + _hbm_specs(r_in), out_specs=tuple(out_specs) + tuple(_hbm_specs(r_out)),
        scratch_shapes=list(scratch_shapes) + list(rider.sems), input_output_aliases=aliases,
        compiler_params=_cp(sem),
    )(*args, *rider.inputs)
    return tuple(res[:n_out]), tuple(res[n_out:])


def _in_fwd(x, pre_g, w_t, name, rider=None):
    s, n = x.shape[0], w_t.shape[0]
    tm = min(512, s)
    nc = 256

    def body(x_ref, g_ref, w_ref, p_ref, h_ref):
        xv = x_ref[...]
        r = lax.rsqrt(jnp.mean(xv * xv, axis=-1, keepdims=True) + EPS)
        h = (xv * r * g_ref[...]).astype(BF16)
        h_ref[...] = h
        for j in range(n // nc):
            p_ref[:, j * nc:(j + 1) * nc] = _dot_nt(h, w_ref[j * nc:(j + 1) * nc, :]).astype(BF16)

    return _host_call(
        body, grid=(s // tm,), name=name, rider=rider,
        out_shape=(jax.ShapeDtypeStruct((s, n), BF16), jax.ShapeDtypeStruct((s, D), BF16)),
        in_specs=[pl.BlockSpec((tm, D), lambda i: (i, 0)), _const((1, D)), _resident((n, D))],
        out_specs=(pl.BlockSpec((tm, n), lambda i: (i, 0)), pl.BlockSpec((tm, D), lambda i: (i, 0))),
        args=(x, pre_g, w_t))


NC = 256


def _stream_tables(core, chip, n):
    nchunk = n // NC
    idx = jnp.arange(nchunk, dtype=jnp.int32)
    src = jnp.array([0, 2, 1, 3], jnp.int32)
    r = n // N_CHIPS

    def group_of(row):
        j = src[(row // r) ^ chip]
        through_sibling = ((row % r) // (r // 2) != core).astype(jnp.int32)
        return jnp.where(j == 0, 0, 2 * j - 1 + through_sibling)

    grp = jnp.maximum(group_of(idx * NC), group_of(idx * NC + NC - 1))
    order = jnp.argsort(grp * 64 + idx).astype(jnp.int32)
    return order, grp[order]


def _in_fwd_streamed(x, pre_g, first, later, order, group, name):
    s, n = x.shape[0], first[0].shape[0]
    nchunk = n // NC
    rider = _GatherRider(first)
    rider2 = _GatherRider(later) if later else None
    a, m = len(first), len(later)
    tr = min(256, s)

    def body(*refs):
        order_ref, group_ref, x_ref, g_ref = refs[0:4]
        p_ref, h_ref = refs[4 + a + m:6 + a + m]
        outs = refs[6 + a + m:6 + 2 * a + m]
        outs2 = refs[6 + 2 * a + m:6 + 2 * a + 2 * m]
        wbuf, wsem, send_sems, recv_sems = refs[6 + 2 * a + 2 * m:10 + 2 * a + 2 * m]
        sems2 = refs[10 + 2 * a + 2 * m:]
        w_hbm = outs[0]
        j = pl.program_id(0)
        sems = (send_sems, recv_sems)
        grp = group_ref[j]
        new_group = jnp.logical_or(j == 0, group_ref[jnp.maximum(j - 1, 0)] != grp)
        slot = j % 2

        def fetch(step, sl):
            rows = pl.ds(pl.multiple_of(order_ref[step] * NC, NC), NC)
            return pltpu.make_async_copy(w_hbm.at[rows], wbuf.at[sl], wsem.at[sl])

        @pl.when(j == 0)
        def _():
            rider.start(None, outs, sems, peers=(0, 1))

            @pl.loop(0, s // tr)
            def _(t):
                rows = pl.ds(pl.multiple_of(t * tr, tr), tr)
                xv = x_ref[rows, :]
                r = lax.rsqrt(jnp.mean(xv * xv, axis=-1, keepdims=True) + EPS)
                h_ref[rows, :] = (xv * r * g_ref[...]).astype(BF16)

        before = jnp.where(j == 0, 0, group_ref[jnp.maximum(j - 1, 0)])

        def entering(b):
            return jnp.logical_and(before < b, b <= grp)

        for src in range(3):
            @pl.when(entering(2 * src + 1))
            def _(src=src):
                if src == 0:
                    rider.start(None, outs, sems, peers=(2,))
                rider.mid(None, outs, sems, peers=(src,))
                if src == 1 and rider2 is not None:
                    rider2.start(None, outs2, sems2)

            @pl.when(entering(2 * src + 2))
            def _(src=src):
                rider.wait_forwarded(outs, sems, peers=(src,))

        @pl.when(new_group)
        def _():
            fetch(j, slot).start()

        fetch(j, slot).wait()
        nxt = jnp.minimum(j + 1, nchunk - 1)

        @pl.when(jnp.logical_and(j + 1 < nchunk, group_ref[nxt] == grp))
        def _():
            fetch(nxt, 1 - slot).start()

        p_ref[...] = _dot_nt(h_ref[...], wbuf[slot]).astype(BF16)

        @pl.when(j == nchunk - 1)
        def _():
            rider.wait_sends(outs, sems)
            if rider2 is not None:
                rider2.mid(None, outs2, sems2)
                rider2.end(None, outs2, sems2)

    hbm = pl.BlockSpec(memory_space=pltpu.HBM)
    arrs = list(first) + list(later)
    whole = pl.BlockSpec((s, D), lambda j, o, g: (0, 0), pipeline_mode=pl.Buffered(1))
    res = pl.pallas_call(
        body, name=name,
        out_shape=(jax.ShapeDtypeStruct((s, n), BF16), jax.ShapeDtypeStruct((s, D), BF16))
        + tuple(jax.ShapeDtypeStruct(v.shape, v.dtype) for v in arrs),
        grid_spec=pltpu.PrefetchScalarGridSpec(
            num_scalar_prefetch=2, grid=(nchunk,),
            in_specs=[whole, pl.BlockSpec((1, D), lambda j, o, g: (0, 0))] + [hbm] * (a + m),
            out_specs=(pl.BlockSpec((s, NC), lambda j, o, g: (0, o[j])),
                       pl.BlockSpec((s, D), lambda j, o, g: (0, 0))) + (hbm,) * (a + m),
            scratch_shapes=[pltpu.VMEM((2, NC, D), BF16), pltpu.SemaphoreType.DMA((2,))] + list(rider.sems)
            + (list(rider2.sems) if rider2 is not None else [])),
        input_output_aliases={4 + v: 2 + v for v in range(a + m)},
        compiler_params=_cp(("arbitrary",)),
    )(order, group, x, pre_g, *arrs)
    return res[0], res[1], tuple(res[2:2 + a]), tuple(res[2 + a:])


def _xattn_fwd(q, kv_ref):
    outs, probs = [], []
    for h in range(XA // HD):
        qh = q[:, h * HD:(h + 1) * HD]
        kh = kv_ref[:, h * HD:(h + 1) * HD]
        vh = kv_ref[:, XA + h * HD:XA + (h + 1) * HD]
        sc = _dot_nt(qh, kh) * SCALE
        e = jnp.exp(sc - jnp.max(sc, axis=-1, keepdims=True))
        pr = e / jnp.sum(e, axis=-1, keepdims=True)
        outs.append(_dot(pr.astype(BF16), vh))
        probs.append(pr)
    return jnp.concatenate(outs, axis=-1), probs


def _xattn_bwd(dyx, q, probs, kv_ref, dkv_ref):
    dqs = []
    for h in range(XA // HD):
        qh = q[:, h * HD:(h + 1) * HD]
        kh = kv_ref[:, h * HD:(h + 1) * HD]
        vh = kv_ref[:, XA + h * HD:XA + (h + 1) * HD]
        dy = dyx[:, h * HD:(h + 1) * HD].astype(BF16)
        pr = probs[h]
        dp = _dot_nt(dy, vh)
        ds = (pr * (dp - jnp.sum(dp * pr, axis=-1, keepdims=True))).astype(BF16)
        dqs.append(_dot(ds, kh) * SCALE)
        dkv_ref[:, h * HD:(h + 1) * HD] += _dot_tn(ds, qh) * SCALE
        dkv_ref[:, XA + h * HD:XA + (h + 1) * HD] += _dot_tn(pr.astype(BF16), dy)
    return jnp.concatenate(dqs, axis=-1)


def _layer_norm_fwd(v, g, b):
    mu = jnp.mean(v, axis=-1, keepdims=True)
    vc = v - mu
    rstd = lax.rsqrt(jnp.mean(vc * vc, axis=-1, keepdims=True) + EPS)
    vhat = vc * rstd
    return vhat * g + b, vhat, rstd


def _layer_norm_bwd(dy, vhat, rstd, g):
    dvh = dy * g
    return rstd * (dvh - jnp.mean(dvh, axis=-1, keepdims=True) - vhat * jnp.mean(dvh * vhat, axis=-1, keepdims=True))


def _head_masks():
    col = lax.broadcasted_iota(jnp.int32, (1, BW), 1)
    return [(col >= h * (BW // 4)) & (col < (h + 1) * (BW // 4)) for h in range(4)]


def _halo_prev(nblk_per_tile):
    return lambda i: (jnp.maximum(i * nblk_per_tile - 1, 0), 0)


def _row_ids(i, t):
    return i * t + lax.broadcasted_iota(jnp.int32, (t, 1), 0)


def _even_mix(i, p_ref, ph_ref, ln_g, ln_b, wcat_ref, bsg_ref, bconv_ref, wbuf):
    t = p_ref.shape[0]
    u = p_ref[:, 0:BW].astype(F32)
    v = p_ref[:, BW:2 * BW].astype(F32)
    bg = p_ref[:, 2 * BW:3 * BW].astype(F32)
    cg = p_ref[:, 3 * BW:4 * BW].astype(F32)
    xin = p_ref[:, 4 * BW:5 * BW].astype(F32)
    vn, vhat, rstd = _layer_norm_fwd(v, ln_g, ln_b)
    masks = _head_masks()
    sgs, vsts = [], []
    for n in range(t // CH):
        vn_c = vn[n * CH:(n + 1) * CH]
        vst = jnp.concatenate([jnp.where(m, vn_c, 0.0) for m in masks], axis=0).astype(BF16)
        sgs.append(_dot(wcat_ref[...], vst) + bsg_ref[...])
        vsts.append(vst)
    sg = jnp.concatenate(sgs, axis=0)
    ya = u * sg
    w_halo = ph_ref[:, 3 * BW:4 * BW].astype(F32) * ph_ref[:, 4 * BW:5 * BW].astype(F32)
    wbuf[0:HALO, :] = jnp.where(i > 0, w_halo, 0.0)
    wbuf[HALO:HALO + t, :] = cg * xin
    conv = (bconv_ref[0:1, :] * wbuf[pl.ds(HALO - 2, t), :] + bconv_ref[1:2, :] * wbuf[pl.ds(HALO - 1, t), :]
            + bconv_ref[2:3, :] * wbuf[pl.ds(HALO, t), :])
    yb = bg * conv
    return dict(u=u, bg=bg, cg=cg, xin=xin, vhat=vhat, rstd=rstd, sg=sg, vsts=vsts, conv=conv, ya=ya, yb=yb,
                masks=masks)


def _pool_select(vals):
    col = lax.broadcasted_iota(jnp.int32, (1, BW), 1)
    g = BW // 4
    return jnp.where(col < g, vals[0], jnp.where(col < 2 * g, vals[1], jnp.where(col < 3 * g, vals[2], vals[3])))


def _inv_counts(i, t):
    rows = _row_ids(i, t) + 1
    return [1.0 / jnp.minimum(rows, w).astype(F32) for w in POOL_WINDOWS]


def _band_matrices(t, forward):
    j = jnp.arange(t)[:, None]
    r = jnp.arange(HALO + t)[None, :]
    if forward:
        return jnp.stack([(r >= j) & (r < j + w) for w in POOL_WINDOWS]).astype(BF16)
    return jnp.stack([(r <= HALO + j) & (r > HALO + j - w) for w in POOL_WINDOWS]).astype(BF16)


SHIFT_ROWS = HALO + TS - 8


def _shifted_copies(buf, sh):
    for b in range(1, 8):
        sh[b - 1] = buf[pl.ds(b, SHIFT_ROWS), :]


def _rows_at(buf, sh, off, t):
    a, b = divmod(off, 8)
    return buf[pl.ds(8 * a, t), :] if b == 0 else sh[b - 1, pl.ds(8 * a, t), :]


def _tap_sums(d_ref, buf, sh, base, out_ref):
    t = d_ref.shape[0]
    group = 4
    for k0 in range(0, CONF, group):
        taps = list(range(k0, min(k0 + group, CONF)))

        def step(r, accs, taps=taps):
            row = pl.multiple_of(r * 8, 8)
            d = d_ref[pl.ds(row, 8), :]
            new = []
            for acc, k in zip(accs, taps):
                a, b = divmod(base + k, 8)
                src = buf[pl.ds(row + 8 * a, 8), :] if b == 0 else sh[b - 1, pl.ds(row + 8 * a, 8), :]
                new.append(acc + d * src)
            return tuple(new)

        accs = lax.fori_loop(0, t // 8, step, tuple(jnp.zeros((8, BW), F32) for _ in taps), unroll=2)
        for acc, k in zip(accs, taps):
            out_ref[8 * k:8 * k + 8, :] += acc


def _odd_mix(i, p_ref, ph_ref, bands_ref, wbd_ref, cscale, dww_ref, dwb, ln_g, ln_b, pww_ref, pwb, gbuf, gsh,
             cv=None):
    t = p_ref.shape[0]
    zc_bf = p_ref[:, 0:BW]
    zc = zc_bf.astype(F32)
    ga = p_ref[:, BW:2 * BW].astype(F32)
    gb = p_ref[:, 2 * BW:3 * BW].astype(F32)
    zh = ph_ref[:, 0:BW]
    zcat = jnp.concatenate([jnp.where(i > 0, zh, jnp.zeros_like(zh)), zc_bf], axis=0)
    inv = _inv_counts(i, t)
    pooled = _pool_select([_dot(bands_ref[w], zcat) * inv[w] for w in range(len(POOL_WINDOWS))]) - zc
    pooled_bf = pooled.astype(BF16)
    pre = _dot(pooled_bf, wbd_ref[...])
    yc = pre * cscale
    sgb = _sigmoid(gb)
    z = ga * sgb
    gh_a = ph_ref[:, BW:2 * BW].astype(F32)
    gh_b = ph_ref[:, 2 * BW:3 * BW].astype(F32)
    gbuf[0:HALO, :] = jnp.where(i > 0, gh_a * _sigmoid(gh_b), 0.0)
    gbuf[HALO:HALO + t, :] = z
    _shifted_copies(gbuf, gsh)
    if cv is None:
        cv = dwb + dww_ref[CONF - 1:CONF, :] * z
        for k in range(CONF - 1):
            cv = cv + dww_ref[k:k + 1, :] * _rows_at(gbuf, gsh, HALO - (CONF - 1) + k, t)
    zl, zhat, rstd = _layer_norm_fwd(cv, ln_g, ln_b)
    szl = _sigmoid(zl)
    zs = (zl * szl).astype(BF16)
    yd = _dot(zs, pww_ref[...]) + pwb
    return dict(ga=ga, sgb=sgb, pooled_bf=pooled_bf, pre=pre, yc=yc, zhat=zhat, rstd=rstd, zl=zl, szl=szl,
                zs=zs, yd=yd, inv=inv, cv=cv)


def _post_norm(o, post_g):
    r = lax.rsqrt(jnp.mean(o * o, axis=-1, keepdims=True) + EPS)
    return o * r, r


def _gate_out(y_a, y_b, y_x, gate, wout_ref):
    sgt = _sigmoid(gate)
    sgate = gate * sgt
    ys = [(y_a * sgate[:, 0:BW]).astype(BF16), (y_b * sgate[:, BW:2 * BW]).astype(BF16),
          (y_x * sgate[:, 2 * BW:MIX]).astype(BF16)]
    o = (_dot(ys[0], wout_ref[0:BW, :]) + _dot(ys[1], wout_ref[BW:2 * BW, :]) + _dot(ys[2], wout_ref[2 * BW:MIX, :]))
    return o, ys, sgt, sgate


def _tile_specs(s, n):
    nh = TS // HALO
    return pl.BlockSpec((TS, n), lambda i: (i, 0)), pl.BlockSpec((HALO, n), _halo_prev(nh))


def _even_fwd(x, p, kv, ln_g, ln_b, wcat, bsg, bconv, wout, post_g, rider=None):
    s = x.shape[0]

    def body(x_ref, p_ref, ph_ref, kv_ref, lng, lnb, wcat_ref, bsg_ref, bconv_ref, wout_ref, pg, x1_ref, o_ref,
             y_ref, wbuf):
        i = pl.program_id(0)
        mx = _even_mix(i, p_ref, ph_ref, lng[...], lnb[...], wcat_ref, bsg_ref, bconv_ref, wbuf)
        yx, _ = _xattn_fwd(p_ref[:, 5 * BW:5 * BW + XA], kv_ref)
        gate = p_ref[:, 5 * BW + XA:EVEN_IN].astype(F32)
        o, ys, _, _ = _gate_out(mx["ya"], mx["yb"], yx, gate, wout_ref)
        y_ref[:, 0:BW] = ys[0]
        y_ref[:, BW:2 * BW] = ys[1]
        y_ref[:, 2 * BW:MIX] = ys[2]
        n, _ = _post_norm(o, pg[...])
        o_ref[...] = o
        x1_ref[...] = x_ref[...] + n * pg[...]

    tile, halo = _tile_specs(s, EVEN_IN)
    row = pl.BlockSpec((TS, D), lambda i: (i, 0))
    return _host_call(
        body, grid=(s // TS,), name="even_fwd", rider=rider,
        out_shape=(jax.ShapeDtypeStruct((s, D), F32), jax.ShapeDtypeStruct((s, D), F32),
                   jax.ShapeDtypeStruct((s, MIX), BF16)),
        in_specs=[row, tile, halo, _const((N_MEM, D)), _const((1, BW)), _const((1, BW)), _const((CH, 4 * CH)),
                  _const((CH, BW)), _const((3, BW)), _resident((MIX, D)), _const((1, D))],
        out_specs=(row, row, pl.BlockSpec((TS, MIX), lambda i: (i, 0))),
        scratch_shapes=[pltpu.VMEM((HALO + TS, BW), F32)],
        args=(x, p, p, kv, ln_g, ln_b, wcat, bsg, bconv, wout, post_g))


def _odd_fwd(x1, p, kv, wbd, cscale, dww, dwb, ln_g, ln_b, pww, pwb, wout, post_g, target):
    s = x1.shape[0]

    def body(x_ref, p_ref, ph_ref, kv_ref, bands_ref, wbd_ref, cs, dww_ref, dwb_ref, lng, lnb, pww_ref, pwb_ref,
             wout_ref, pg, tgt_ref, dx_ref, o_ref, cv_ref, loss_ref, gbuf, gsh):
        i = pl.program_id(0)
        mx = _odd_mix(i, p_ref, ph_ref, bands_ref, wbd_ref, cs[...], dww_ref, dwb_ref[...], lng[...], lnb[...],
                      pww_ref, pwb_ref[...], gbuf, gsh)
        cv_ref[...] = mx["cv"]
        yx, _ = _xattn_fwd(p_ref[:, 3 * BW:3 * BW + XA], kv_ref)
        gate = p_ref[:, 3 * BW + XA:ODD_IN].astype(F32)
        o, _, _, _ = _gate_out(mx["yc"], mx["yd"], yx, gate, wout_ref)
        n, _ = _post_norm(o, pg[...])
        o_ref[...] = o
        err = x_ref[...] + n * pg[...] - tgt_ref[...]
        dx_ref[...] = err * (1.0 / D)

        @pl.when(i == 0)
        def _():
            loss_ref[...] = jnp.zeros_like(loss_ref)

        loss_ref[...] += 0.5 * jnp.sum(jnp.sum(err * err, axis=-1, keepdims=True) * (1.0 / D), axis=0, keepdims=True)

    tile, halo = _tile_specs(s, ODD_IN)
    row = pl.BlockSpec((TS, D), lambda i: (i, 0))
    vec = _const((1, BW))
    return pl.pallas_call(
        body, grid=(s // TS,), name="odd_fwd",
        out_shape=(jax.ShapeDtypeStruct((s, D), F32), jax.ShapeDtypeStruct((s, D), F32),
                   jax.ShapeDtypeStruct((s, BW), F32), jax.ShapeDtypeStruct((8, 128), F32)),
        in_specs=[row, tile, halo, _const((N_MEM, D)), _const((4, TS, HALO + TS)), _const((BW, BW)), vec,
                  _const((CONF, BW)), vec, vec, vec, _const((BW, BW)), vec, _resident((MIX, D)), _const((1, D)), row],
        out_specs=(row, row, pl.BlockSpec((TS, BW), lambda i: (i, 0)), _const((8, 128))),
        scratch_shapes=[pltpu.VMEM((HALO + TS, BW), F32), pltpu.VMEM((7, SHIFT_ROWS, BW), F32)],
        compiler_params=_cp(("arbitrary",)),
    )(x1, p, p, kv, _band_matrices(TS, False), wbd, cscale, dww, dwb, ln_g, ln_b, pww, pwb, wout, post_g, target)


def _acc_init(i, refs):
    @pl.when(i == 0)
    def _():
        for r in refs:
            r[...] = jnp.zeros_like(r)


def _post_norm_bwd(dx, o, pg, dpg_ref):
    n, r = _post_norm(o, pg)
    dpg_ref[...] += jnp.sum(dx * n, axis=0, keepdims=True)
    dn = dx * pg
    return (r * (dn - n * jnp.mean(dn * n, axis=-1, keepdims=True))).astype(BF16)


def _gate_bwd(do, wout_ref, ys_f32, gate, y_ref):
    dy = _dot_nt(do, wout_ref[...])
    sgt = _sigmoid(gate)
    sgate = gate * sgt
    dsilu = sgt * (1.0 + gate * (1.0 - sgt))
    offs = (0, BW, 2 * BW, MIX)
    dys, dgs = [], []
    for j, yv in enumerate(ys_f32):
        a, b = offs[j], offs[j + 1]
        if y_ref is not None:
            y_ref[:, a:b] = (yv * sgate[:, a:b]).astype(BF16)
        dys.append(dy[:, a:b] * sgate[:, a:b])
        dgs.append(dy[:, a:b] * yv * dsilu[:, a:b])
    return dys, jnp.concatenate(dgs, axis=-1)


NEXT = 16


def _even_bwd1(dx, o, p, kv, ln_g, ln_b, wcat, bsg, hsel, bconv, wout, post_g, rider=None):
    s = dx.shape[0]
    nt = s // TS

    def body(dx_ref, o_ref, p_ref, ph_ref, dxn_ref, on_ref, pn_ref, kv_ref, lng, lnb, wcat_ref, bsg_ref, hsel_ref,
             bconv_ref, wout_ref, pg,
             dp_ref, do_ref, dpg_ref, dlng_ref, dlnb_ref, dwcat_ref, dbs_ref, dbconv_ref, dkv_ref, wbuf, dbuf):
        i = pl.program_id(0)
        _acc_init(i, (dpg_ref, dlng_ref, dlnb_ref, dwcat_ref, dbs_ref, dbconv_ref, dkv_ref))
        mx = _even_mix(i, p_ref, ph_ref, lng[...], lnb[...], wcat_ref, bsg_ref, bconv_ref, wbuf)
        q = p_ref[:, 5 * BW:5 * BW + XA]
        yx, probs = _xattn_fwd(q, kv_ref)
        gate = p_ref[:, 5 * BW + XA:EVEN_IN].astype(F32)
        do = _post_norm_bwd(dx_ref[...], o_ref[...], pg[...], dpg_ref)
        do_ref[...] = do
        (dya, dyb, dyx), dgate = _gate_bwd(do, wout_ref, (mx["ya"], mx["yb"], yx), gate, None)
        dp_ref[:, 0:BW] = (dya * mx["sg"]).astype(BF16)
        dsg = (dya * mx["u"]).astype(BF16)
        dvns = []
        for n in range(TS // CH):
            dsg_c = dsg[n * CH:(n + 1) * CH]
            dvst = _dot_tn(wcat_ref[...], dsg_c)
            dvn_c = jnp.where(mx["masks"][0], dvst[0:CH], 0.0)
            for h in range(1, 4):
                dvn_c = dvn_c + jnp.where(mx["masks"][h], dvst[h * CH:(h + 1) * CH], 0.0)
            dvns.append(dvn_c)
            dwcat_ref[...] += _dot_nt(dsg_c, mx["vsts"][n])
            dbs_ref[...] += _dot(dsg_c, hsel_ref[...])
        dvn = jnp.concatenate(dvns, axis=0)
        dlng_ref[...] += jnp.sum(dvn * mx["vhat"], axis=0, keepdims=True)
        dlnb_ref[...] += jnp.sum(dvn, axis=0, keepdims=True)
        dp_ref[:, BW:2 * BW] = _layer_norm_bwd(dvn, mx["vhat"], mx["rstd"], lng[...]).astype(BF16)
        dp_ref[:, 2 * BW:3 * BW] = (dyb * mx["conv"]).astype(BF16)
        dconv = dyb * mx["bg"]
        for k in range(3):
            dbconv_ref[k:k + 1, :] += jnp.sum(dconv * wbuf[pl.ds(HALO - 2 + k, TS), :], axis=0, keepdims=True)
        n_n, r_n = _post_norm(on_ref[...], pg[...])
        dn_n = dxn_ref[...] * pg[...]
        do_n = (r_n * (dn_n - n_n * jnp.mean(dn_n * n_n, axis=-1, keepdims=True))).astype(BF16)
        dy_n = _dot_nt(do_n, wout_ref[BW:2 * BW, :])
        g_n = pn_ref[:, 5 * BW + XA + BW:5 * BW + XA + 2 * BW].astype(F32)
        dconv_n = dy_n * (g_n * _sigmoid(g_n)) * pn_ref[:, 2 * BW:3 * BW].astype(F32)
        dbuf[0:TS, :] = dconv
        dbuf[TS:TS + NEXT, :] = jnp.where(i < nt - 1, dconv_n, 0.0)
        dw = (bconv_ref[2:3, :] * dconv + bconv_ref[1:2, :] * dbuf[pl.ds(1, TS), :]
              + bconv_ref[0:1, :] * dbuf[pl.ds(2, TS), :])
        dp_ref[:, 3 * BW:4 * BW] = (dw * mx["xin"]).astype(BF16)
        dp_ref[:, 4 * BW:5 * BW] = (dw * mx["cg"]).astype(BF16)
        dp_ref[:, 5 * BW:5 * BW + XA] = _xattn_bwd(dyx, q, probs, kv_ref, dkv_ref).astype(BF16)
        dp_ref[:, 5 * BW + XA:EVEN_IN] = dgate.astype(BF16)

    tile, halo = _tile_specs(s, EVEN_IN)
    row = pl.BlockSpec((TS, D), lambda i: (i, 0))
    vec = _const((1, BW))
    nxt = _halo_next(TS // NEXT, s // NEXT)

    def out(n):
        return pl.BlockSpec((TS, n), lambda i: (i, 0))

    return _host_call(
        body, grid=(nt,), name="even_bwd1", rider=rider,
        out_shape=(jax.ShapeDtypeStruct((s, EVEN_IN), BF16), jax.ShapeDtypeStruct((s, D), BF16),
                   jax.ShapeDtypeStruct((1, D), F32), jax.ShapeDtypeStruct((1, BW), F32),
                   jax.ShapeDtypeStruct((1, BW), F32), jax.ShapeDtypeStruct((CH, 4 * CH), F32),
                   jax.ShapeDtypeStruct((CH, 128), F32), jax.ShapeDtypeStruct((8, BW), F32),
                   jax.ShapeDtypeStruct((N_MEM, D), F32)),
        in_specs=[row, row, tile, halo, pl.BlockSpec((NEXT, D), nxt), pl.BlockSpec((NEXT, D), nxt),
                  pl.BlockSpec((NEXT, EVEN_IN), nxt), _const((N_MEM, D)), vec, vec, _const((CH, 4 * CH)),
                  _const((CH, BW)), _const((BW, 128)), _const((3, BW)), _resident((MIX, D)), _const((1, D))],
        out_specs=(out(EVEN_IN), out(D),
                   _const((1, D)), vec, vec, _const((CH, 4 * CH)), _const((CH, 128)), _const((8, BW)),
                   _const((N_MEM, D))),
        scratch_shapes=[pltpu.VMEM((HALO + TS, BW), F32), pltpu.VMEM((TS + NEXT, BW), F32)],
        args=(dx, o, p, p, dx, o, p, kv, ln_g, ln_b, wcat, bsg, hsel, bconv, wout, post_g))


def _odd_bwd1(dx, o, cv, p, kv, wbd, cscale, dww, dwb, ln_g, ln_b, pww, pwb, wout, post_g):
    s = dx.shape[0]

    def body(dx_ref, o_ref, cv_ref, p_ref, ph_ref, kv_ref, bands_ref, wbd_ref, cs, dww_ref, dwb_ref, lng, lnb,
             pww_ref, pwb_ref, wout_ref, pg,
             dpc_ref, tmpc_ref, tmpd_ref, do_ref, y_ref, dpg_ref, dcs_ref, dwbd_ref, ddww_ref, ddwb_ref, dlng_ref,
             dlnb_ref, dpww_ref, dpwb_ref, dkv_ref, gbuf, gsh, dcv_buf):
        i = pl.program_id(0)
        _acc_init(i, (dpg_ref, dcs_ref, dwbd_ref, ddww_ref, ddwb_ref, dlng_ref, dlnb_ref, dpww_ref, dpwb_ref,
                      dkv_ref))
        mx = _odd_mix(i, p_ref, ph_ref, bands_ref, wbd_ref, cs[...], dww_ref, dwb_ref[...], lng[...], lnb[...],
                      pww_ref, pwb_ref[...], gbuf, gsh, cv=cv_ref[...])
        q = p_ref[:, 3 * BW:3 * BW + XA]
        yx, probs = _xattn_fwd(q, kv_ref)
        gate = p_ref[:, 3 * BW + XA:ODD_IN].astype(F32)
        do = _post_norm_bwd(dx_ref[...], o_ref[...], pg[...], dpg_ref)
        do_ref[...] = do
        (dyc, dyd, dyx), dgate = _gate_bwd(do, wout_ref, (mx["yc"], mx["yd"], yx), gate, y_ref)
        dcs_ref[...] += jnp.sum(dyc * mx["pre"], axis=0, keepdims=True)
        dpre = (dyc * cs[...]).astype(BF16)
        dwbd_ref[...] += _dot_tn(mx["pooled_bf"], dpre)
        dpooled = _dot_nt(dpre, wbd_ref[...])
        tmpc_ref[...] = _pool_select([dpooled * c_ for c_ in mx["inv"]]).astype(BF16)
        dyd_bf = dyd.astype(BF16)
        dpwb_ref[...] += jnp.sum(dyd, axis=0, keepdims=True)
        dpww_ref[...] += _dot_tn(mx["zs"], dyd_bf)
        dzs = _dot_nt(dyd_bf, pww_ref[...])
        zl, szl = mx["zl"], mx["szl"]
        dzl = dzs * (szl * (1.0 + zl * (1.0 - szl)))
        dlng_ref[...] += jnp.sum(dzl * mx["zhat"], axis=0, keepdims=True)
        dlnb_ref[...] += jnp.sum(dzl, axis=0, keepdims=True)
        dcv = _layer_norm_bwd(dzl, mx["zhat"], mx["rstd"], lng[...])
        tmpd_ref[...] = dcv.astype(BF16)
        ddwb_ref[...] += jnp.sum(dcv, axis=0, keepdims=True)
        dcv_buf[...] = dcv
        _tap_sums(dcv_buf, gbuf, gsh, HALO - (CONF - 1), ddww_ref)
        dpc_ref[:, 0:XA] = _xattn_bwd(dyx, q, probs, kv_ref, dkv_ref).astype(BF16)
        dpc_ref[:, XA:XA + MIX] = dgate.astype(BF16)

    tile, halo = _tile_specs(s, ODD_IN)
    row = pl.BlockSpec((TS, D), lambda i: (i, 0))
    vec = _const((1, BW))

    def out(n):
        return pl.BlockSpec((TS, n), lambda i: (i, 0))

    return pl.pallas_call(
        body, grid=(s // TS,), name="odd_bwd1",
        out_shape=(jax.ShapeDtypeStruct((s, XA + MIX), BF16), jax.ShapeDtypeStruct((s, BW), BF16),
                   jax.ShapeDtypeStruct((s, BW), BF16), jax.ShapeDtypeStruct((s, D), BF16),
                   jax.ShapeDtypeStruct((s, MIX), BF16),
                   jax.ShapeDtypeStruct((1, D), F32), jax.ShapeDtypeStruct((1, BW), F32),
                   jax.ShapeDtypeStruct((BW, BW), F32), jax.ShapeDtypeStruct((8 * CONF, BW), F32),
                   jax.ShapeDtypeStruct((1, BW), F32), jax.ShapeDtypeStruct((1, BW), F32),
                   jax.ShapeDtypeStruct((1, BW), F32), jax.ShapeDtypeStruct((BW, BW), F32),
                   jax.ShapeDtypeStruct((1, BW), F32), jax.ShapeDtypeStruct((N_MEM, D), F32)),
        in_specs=[row, row, out(BW), tile, halo, _const((N_MEM, D)), _const((4, TS, HALO + TS)), _const((BW, BW)), vec,
                  _const((CONF, BW)), vec, vec, vec, _const((BW, BW)), vec, _resident((MIX, D)), _const((1, D))],
        out_specs=(out(XA + MIX), out(BW), out(BW), out(D), out(MIX),
                   _const((1, D)), vec, _const((BW, BW)), _const((8 * CONF, BW)), vec, vec, vec, _const((BW, BW)), vec,
                   _const((N_MEM, D))),
        scratch_shapes=[pltpu.VMEM((HALO + TS, BW), F32), pltpu.VMEM((7, SHIFT_ROWS, BW), F32),
                        pltpu.VMEM((TS, BW), F32)],
        compiler_params=_cp(("arbitrary",)),
    )(dx, o, cv, p, p, kv, _band_matrices(TS, False), wbd, cscale, dww, dwb, ln_g, ln_b, pww, pwb, wout, post_g)


def _halo_next(nblk_per_tile, nblk):
    return lambda i: (jnp.minimum((i + 1) * nblk_per_tile, nblk - 1), 0)


def _pre_norm_bwd(dh, x, pre_g, dres, dpre_ref):
    r = lax.rsqrt(jnp.mean(x * x, axis=-1, keepdims=True) + EPS)
    xh = x * r
    dpre_ref[...] += jnp.sum(dh * xh, axis=0, keepdims=True)
    dxh = dh * pre_g
    return dres + r * (dxh - xh * jnp.mean(dxh * xh, axis=-1, keepdims=True))


def _even_bwd2(dp, w_t, x, pre_g, dres, rider=None):
    s = x.shape[0]
    tm = min(512, s)

    def body(dp_ref, w_ref, x_ref, pg, dres_ref, dx_ref, dpre_ref):
        _acc_init(pl.program_id(0), (dpre_ref,))
        dh = _dot(dp_ref[...], w_ref[...])
        dx_ref[...] = _pre_norm_bwd(dh, x_ref[...], pg[...], dres_ref[...], dpre_ref)

    row = pl.BlockSpec((tm, D), lambda i: (i, 0))
    return _host_call(
        body, grid=(s // tm,), name="even_bwd2", rider=rider,
        out_shape=(jax.ShapeDtypeStruct((s, D), F32), jax.ShapeDtypeStruct((1, D), F32)),
        in_specs=[pl.BlockSpec((tm, EVEN_IN), lambda i: (i, 0)), _resident((EVEN_IN, D)), row, _const((1, D)), row],
        out_specs=(row, _const((1, D))),
        args=(dp, w_t, x, pre_g, dres))


def _odd_bwd2(dpc, tmpc, tmpd, p, dww, w_t, x, pre_g, dres):
    s = x.shape[0]
    nt = s // TS

    def body(dpc_ref, tc_ref, tch_ref, td_ref, tdh_ref, ga_ref, gb_ref, bands_ref, dww_ref, w_ref, x_ref, pg,
             dres_ref, dpb_ref, dx_ref, dpre_ref, dbuf, dsh):
        i = pl.program_id(0)
        _acc_init(i, (dpre_ref,))
        more = i < nt - 1
        e_bf = tc_ref[...]
        eh = tch_ref[...]
        ecat = jnp.concatenate([e_bf, jnp.where(more, eh, jnp.zeros_like(eh))], axis=0)
        dbuf[0:TS, :] = td_ref[...].astype(F32)
        dbuf[TS:TS + HALO, :] = jnp.where(more, tdh_ref[...].astype(F32), 0.0)
        sums = [_dot(bands_ref[w], ecat) for w in range(len(POOL_WINDOWS))]
        rows = _row_ids(i, TS) + 1
        cnt = _pool_select([jnp.minimum(rows, w).astype(F32) for w in POOL_WINDOWS])
        dzc = (_pool_select(sums) - e_bf.astype(F32) * cnt).astype(BF16)
        _shifted_copies(dbuf, dsh)
        dz = dww_ref[CONF - 1:CONF, :] * dbuf[pl.ds(0, TS), :]
        for sft in range(1, CONF):
            dz = dz + dww_ref[CONF - 1 - sft:CONF - sft, :] * _rows_at(dbuf, dsh, sft, TS)
        ga = ga_ref[...].astype(F32)
        sgb = _sigmoid(gb_ref[...].astype(F32))
        dga = (dz * sgb).astype(BF16)
        dgb = (dz * ga * sgb * (1.0 - sgb)).astype(BF16)
        dpb_ref[:, 0:BW] = dzc
        dpb_ref[:, BW:2 * BW] = dga
        dpb_ref[:, 2 * BW:3 * BW] = dgb
        dh = (_dot(dzc, w_ref[0:BW, :]) + _dot(dga, w_ref[BW:2 * BW, :]) + _dot(dgb, w_ref[2 * BW:3 * BW, :])
              + _dot(dpc_ref[...], w_ref[3 * BW:ODD_IN, :]))
        dx_ref[...] = _pre_norm_bwd(dh, x_ref[...], pg[...], dres_ref[...], dpre_ref)

    row = pl.BlockSpec((TS, D), lambda i: (i, 0))

    def tile(n, j=0):
        return pl.BlockSpec((TS, n), lambda i: (i, j))

    nxt = pl.BlockSpec((HALO, BW), _halo_next(TS // HALO, s // HALO))
    return pl.pallas_call(
        body, grid=(nt,), name="odd_bwd2",
        out_shape=(jax.ShapeDtypeStruct((s, 3 * BW), BF16), jax.ShapeDtypeStruct((s, D), F32),
                   jax.ShapeDtypeStruct((1, D), F32)),
        in_specs=[tile(XA + MIX), tile(BW), nxt, tile(BW), nxt, tile(BW, 1), tile(BW, 2), _const((4, TS, HALO + TS)),
                  _const((CONF, BW)), _resident((ODD_IN, D)), row, _const((1, D)), row],
        out_specs=(tile(3 * BW), row, _const((1, D))),
        scratch_shapes=[pltpu.VMEM((TS + HALO, BW), F32), pltpu.VMEM((7, SHIFT_ROWS, BW), F32)],
        compiler_params=_cp(("arbitrary",)),
    )(dpc, tmpc, tmpc, tmpd, tmpd, p, p, _band_matrices(TS, True), dww, w_t, x, pre_g, dres)


def _grad_tn(a, b, tm, out=None, rows=None, row0=0, name="grad_tn", rider=None):
    s, m = a.shape
    n = b.shape[1]
    ts = min(2048, s)
    rows = m if rows is None else rows
    assert m % tm == 0 and s % ts == 0
    ns = s // ts
    if row0 % tm == 0:
        out_spec = pl.BlockSpec((tm, n), lambda i, k: (row0 // tm + i, 0))
    else:
        align = 16
        assert row0 % align == 0 and tm % align == 0
        out_spec = pl.BlockSpec((pl.Element(tm), pl.Element(n)),
                                lambda i, k: (pl.multiple_of(row0 + i * tm, align), 0))

    def body(*refs):
        a_ref, b_ref = refs[0], refs[1]
        o_ref, acc = refs[-2], refs[-1]
        k = pl.program_id(1)

        @pl.when(k == 0)
        def _():
            acc[...] = jnp.zeros_like(acc)

        acc[...] += _dot_tn(a_ref[...], b_ref[...])

        @pl.when(k == ns - 1)
        def _():
            o_ref[...] = acc[...].astype(BF16)

    in_specs = [pl.BlockSpec((ts, tm), lambda i, k: (k, i)), pl.BlockSpec((ts, n), lambda i, k: (k, 0))]
    args = [a, b]
    aliases = {}
    if out is not None:
        in_specs.append(pl.BlockSpec(memory_space=pltpu.HBM))
        args.append(out)
        aliases = {2: 0}
    (res,), got = _host_call(
        body, grid=(m // tm, ns), name=name, rider=rider, aliases=aliases,
        out_shape=(jax.ShapeDtypeStruct((rows, n), BF16),), in_specs=in_specs, out_specs=(out_spec,),
        scratch_shapes=[pltpu.VMEM((tm, n), F32)], args=args)
    return res if rider is None else (res, got)


def _place():
    x, y, c = lax.axis_index("x"), lax.axis_index("y"), lax.axis_index("c")
    chips = [(1 - x, y), (x, 1 - y), (1 - x, 1 - y)]
    return x, y, c, chips


def _hbm_specs(n):
    return [pl.BlockSpec(memory_space=pltpu.HBM)] * n


def _row_tile(r):
    for cand in (512, 400, 304, 256, 192, 128, 96, 16):
        if r % cand == 0:
            return cand
    raise ValueError(r)


def _place_shard(shard, place, dtype, name, after=None):
    r, cc = shard.shape
    tr = _row_tile(r)
    nt = r // tr

    def body(place_ref, s_ref, *rest):
        rest[-1][...] = s_ref[...].astype(dtype)

    in_specs = [pl.BlockSpec((tr, cc), lambda i, pr: (i, 0))]
    args = [shard]
    if after is not None:
        in_specs.append(pl.BlockSpec(after.shape, lambda i, pr: (0, 0)))
        args.append(after)
    return pl.pallas_call(
        body, name=name, out_shape=jax.ShapeDtypeStruct((N_CHIPS * r, cc), dtype),
        grid_spec=pltpu.PrefetchScalarGridSpec(
            num_scalar_prefetch=1, grid=(nt,), in_specs=in_specs,
            out_specs=pl.BlockSpec((tr, cc), lambda i, pr: (pr[1] * nt + i, 0))),
        compiler_params=_cp(("arbitrary",)),
    )(place, *args)


class _GatherRider:
    has_mid = True

    def __init__(self, fulls):
        n = len(fulls)
        self.inputs = list(fulls)
        self.out_shapes = [jax.ShapeDtypeStruct(a.shape, a.dtype) for a in fulls]
        self.aliases = {a: a for a in range(n)}
        self.sems = [pltpu.SemaphoreType.DMA((6 * n,)), pltpu.SemaphoreType.DMA((6 * n,))]
        self.block_rows = [a.shape[0] // N_CHIPS for a in fulls]

    def _ctx(self, outs, sems):
        send_sems, recv_sems = sems
        x, y, c, chips = _place()

        def rows(a, k, half):
            r = self.block_rows[a]
            return outs[a].at[pl.ds(k * r + half * (r // 2), r // 2)]

        def copy(a, j, blk, to):
            return pltpu.make_async_remote_copy(src_ref=blk, dst_ref=blk, send_sem=send_sems.at[a * 6 + j],
                                                recv_sem=recv_sems.at[a * 6 + j], device_id=to, device_id_type=MESH)

        return x, y, c, chips, rows, copy

    def start(self, ins, outs, sems, peers=(0, 1, 2)):
        x, y, c, chips, rows, copy = self._ctx(outs, sems)
        for j in peers:
            for a in range(len(outs)):
                copy(a, j, rows(a, 2 * x + y, c), (*chips[j], c)).start()

    def mid(self, ins, outs, sems, peers=(0, 1, 2)):
        x, y, c, chips, rows, copy = self._ctx(outs, sems)
        for j in peers:
            px, py = chips[j]
            for a in range(len(outs)):
                copy(a, j, rows(a, 2 * px + py, c), (px, py, c)).wait_recv()
                copy(a, 3 + j, rows(a, 2 * px + py, c), (x, y, 1 - c)).start()

    def wait_forwarded(self, outs, sems, peers=(0, 1, 2)):
        x, y, c, chips, rows, copy = self._ctx(outs, sems)
        for j in peers:
            px, py = chips[j]
            for a in range(len(outs)):
                copy(a, 3 + j, rows(a, 2 * px + py, 1 - c), (x, y, 1 - c)).wait_recv()

    def wait_sends(self, outs, sems):
        x, y, c, chips, rows, copy = self._ctx(outs, sems)
        for j, (px, py) in enumerate(chips):
            for a in range(len(outs)):
                copy(a, j, rows(a, 2 * x + y, c), (px, py, c)).wait_send()
                copy(a, 3 + j, rows(a, 2 * px + py, c), (x, y, 1 - c)).wait_send()

    def end(self, ins, outs, sems):
        self.wait_forwarded(outs, sems)
        self.wait_sends(outs, sems)


def _swap_halves(grads, name, share=()):
    n, k = len(grads), len(share)
    m = n + k

    def body(*refs):
        ins, outs = refs[:m], refs[m:2 * m]
        send_sems, recv_sems = refs[2 * m:]
        x, y, c, _ = _place()
        sibling = (x, y, 1 - c)
        cps, waits = [], []
        for a in range(m):
            if a < n:
                cp = pltpu.make_async_remote_copy(src_ref=ins[a].at[:, 1 - c], dst_ref=outs[a],
                                                  send_sem=send_sems.at[a], recv_sem=recv_sems.at[a],
                                                  device_id=sibling, device_id_type=MESH)
                waits.append(cp)
            else:
                cp = pltpu.make_async_remote_copy(src_ref=outs[a].at[c], dst_ref=outs[a].at[c],
                                                  send_sem=send_sems.at[a], recv_sem=recv_sems.at[a],
                                                  device_id=sibling, device_id_type=MESH)
                waits.append(pltpu.make_async_remote_copy(
                    src_ref=outs[a].at[1 - c], dst_ref=outs[a].at[1 - c], send_sem=send_sems.at[a],
                    recv_sem=recv_sems.at[a], device_id=sibling, device_id_type=MESH))
            cp.start()
            cps.append(cp)
        for cp in waits:
            cp.wait_recv()
        for cp in cps:
            cp.wait_send()

    outs = tuple(jax.ShapeDtypeStruct((g.shape[0],) + g.shape[2:], g.dtype) for g in grads)
    outs += tuple(jax.ShapeDtypeStruct(g.shape, g.dtype) for g in share)
    res = pl.pallas_call(
        body, name=name, out_shape=outs, in_specs=_hbm_specs(m), out_specs=tuple(_hbm_specs(m)),
        input_output_aliases={n + a: n + a for a in range(k)},
        scratch_shapes=[pltpu.SemaphoreType.DMA((m,)), pltpu.SemaphoreType.DMA((m,))],
    )(*grads, *share)
    return tuple(res[:n]), tuple(res[n:])


def _pair_sum(g, recv, place, name):
    _, _, h, cc = g.shape
    th = h

    def body(c_ref, g_ref, r_ref, o_ref):
        o_ref[...] = (g_ref[...].astype(F32) + r_ref[...].astype(F32)).astype(o_ref.dtype)

    return pl.pallas_call(
        body, name=name, out_shape=jax.ShapeDtypeStruct(recv.shape, recv.dtype),
        grid_spec=pltpu.PrefetchScalarGridSpec(
            num_scalar_prefetch=1, grid=(N_CHIPS, h // th),
            in_specs=[pl.BlockSpec((None, None, th, cc), lambda k, r, c_ref: (k, c_ref[0], r, 0)),
                      pl.BlockSpec((None, th, cc), lambda k, r, c_ref: (k, r, 0))],
            out_specs=pl.BlockSpec((None, th, cc), lambda k, r, c_ref: (k, r, 0))),
        compiler_params=_cp(("arbitrary", "arbitrary")),
    )(place, g, recv)


def _finish_reduce(pack, halves):
    rows, cc = pack.shape
    hs = rows // 2
    n = len(halves)

    def body(*refs):
        pack_ref = refs[0]
        out_ref = refs[1 + n]
        big = refs[2 + n:2 + 2 * n]
        sib_ref, parts_ref, send_sems, recv_sems, big_send, big_recv = refs[2 + 2 * n:]
        x, y, c, chips = _place()
        me_k = 2 * x + y
        sibling = (x, y, 1 - c)
        mine = pl.ds(pl.multiple_of(c * hs, hs), hs)
        theirs = pl.ds(pl.multiple_of((1 - c) * hs, hs), hs)
        shared = [pltpu.make_async_remote_copy(src_ref=big[a].at[c], dst_ref=big[a].at[c], send_sem=big_send.at[a],
                                               recv_sem=big_recv.at[a], device_id=sibling, device_id_type=MESH)
                  for a in range(n)]
        first = pltpu.make_async_remote_copy(src_ref=pack_ref, dst_ref=sib_ref, send_sem=send_sems.at[0],
                                             recv_sem=recv_sems.at[0], device_id=sibling, device_id_type=MESH)
        first.start()
        first.wait()
        parts_ref[me_k] = pack_ref[mine, :] + sib_ref[mine, :]
        cps = [pltpu.make_async_remote_copy(src_ref=parts_ref.at[me_k], dst_ref=parts_ref.at[me_k],
                                            send_sem=send_sems.at[1 + j], recv_sem=recv_sems.at[1 + j],
                                            device_id=(px, py, c), device_id_type=MESH)
               for j, (px, py) in enumerate(chips)]
        for cp in cps:
            cp.start()
        for cp in shared:
            cp.start()
        for j, (px, py) in enumerate(chips):
            pltpu.make_async_remote_copy(src_ref=parts_ref.at[2 * px + py], dst_ref=parts_ref.at[2 * px + py],
                                         send_sem=send_sems.at[1 + j], recv_sem=recv_sems.at[1 + j],
                                         device_id=(px, py, c), device_id_type=MESH).wait_recv()
        for cp in cps:
            cp.wait_send()
        out_ref[mine, :] = ((parts_ref[0] + parts_ref[1]) + parts_ref[2]) + parts_ref[3]
        last = pltpu.make_async_remote_copy(src_ref=out_ref.at[mine], dst_ref=out_ref.at[mine],
                                            send_sem=send_sems.at[4], recv_sem=recv_sems.at[4], device_id=sibling,
                                            device_id_type=MESH)
        last.start()
        pltpu.make_async_remote_copy(src_ref=out_ref.at[theirs], dst_ref=out_ref.at[theirs],
                                     send_sem=send_sems.at[4], recv_sem=recv_sems.at[4], device_id=sibling,
                                     device_id_type=MESH).wait_recv()
        last.wait_send()
        for a in range(n):
            pltpu.make_async_remote_copy(src_ref=big[a].at[1 - c], dst_ref=big[a].at[1 - c], send_sem=big_send.at[a],
                                         recv_sem=big_recv.at[a], device_id=sibling,
                                         device_id_type=MESH).wait_recv()
        for cp in shared:
            cp.wait_send()

    vmem = pl.BlockSpec(memory_space=pltpu.VMEM)
    res = pl.pallas_call(
        body, name="finish_reduce",
        out_shape=(jax.ShapeDtypeStruct(pack.shape, pack.dtype),)
        + tuple(jax.ShapeDtypeStruct(g.shape, g.dtype) for g in halves),
        in_specs=[vmem] + _hbm_specs(n), out_specs=(vmem,) + tuple(_hbm_specs(n)),
        input_output_aliases={1 + a: 1 + a for a in range(n)},
        scratch_shapes=[pltpu.VMEM((rows, cc), F32), pltpu.VMEM((N_CHIPS, hs, cc), F32),
                        pltpu.SemaphoreType.DMA((5,)), pltpu.SemaphoreType.DMA((5,)),
                        pltpu.SemaphoreType.DMA((n,)), pltpu.SemaphoreType.DMA((n,))],
        compiler_params=_cp(),
    )(pack, *halves)
    return res[0], tuple(res[1:])


class _ExchangeRider:
    has_mid = False

    def __init__(self, sums):
        self.inputs = list(sums)
        self.out_shapes = [jax.ShapeDtypeStruct((3,) + g.shape[1:], g.dtype) for g in sums]
        m = len(self.inputs)
        self.aliases = {}
        self.sems = [pltpu.SemaphoreType.DMA((3 * m,)), pltpu.SemaphoreType.DMA((3 * m,))]

    def _copies(self, ins, outs, sems):
        send_sems, recv_sems = sems
        _, _, c, chips = _place()
        return [pltpu.make_async_remote_copy(
            src_ref=ins[a].at[2 * px + py], dst_ref=outs[a].at[j], send_sem=send_sems.at[a * 3 + j],
            recv_sem=recv_sems.at[a * 3 + j], device_id=(px, py, c), device_id_type=MESH)
            for j, (px, py) in enumerate(chips) for a in range(len(ins))]

    def start(self, ins, outs, sems):
        for cp in self._copies(ins, outs, sems):
            cp.start()

    def end(self, ins, outs, sems):
        cps = self._copies(ins, outs, sems)
        for cp in cps:
            cp.wait_recv()
        for cp in cps:
            cp.wait_send()


class _SwapRider:
    has_mid = False

    def __init__(self, grads):
        n = len(grads)
        self.inputs = list(grads)
        self.out_shapes = [jax.ShapeDtypeStruct((g.shape[0],) + g.shape[2:], g.dtype) for g in grads]
        self.aliases = {}
        self.sems = [pltpu.SemaphoreType.DMA((n,)), pltpu.SemaphoreType.DMA((n,))]

    def _copies(self, ins, outs, sems):
        send_sems, recv_sems = sems
        x, y, c, _ = _place()
        return [pltpu.make_async_remote_copy(src_ref=ins[a].at[:, 1 - c], dst_ref=outs[a], send_sem=send_sems.at[a],
                                             recv_sem=recv_sems.at[a], device_id=(x, y, 1 - c), device_id_type=MESH)
                for a in range(len(ins))]

    def start(self, ins, outs, sems):
        for cp in self._copies(ins, outs, sems):
            cp.start()

    def end(self, ins, outs, sems):
        cps = self._copies(ins, outs, sems)
        for cp in cps:
            cp.wait_recv()
        for cp in cps:
            cp.wait_send()


class _ShareRider:
    has_mid = False

    def __init__(self, halves):
        n = len(halves)
        self.inputs = list(halves)
        self.out_shapes = [jax.ShapeDtypeStruct(g.shape, g.dtype) for g in halves]
        self.aliases = {a: a for a in range(n)}
        self.sems = [pltpu.SemaphoreType.DMA((n,)), pltpu.SemaphoreType.DMA((n,))]

    def _copies(self, outs, sems, half):
        send_sems, recv_sems = sems
        x, y, c, _ = _place()
        h = c if half == "mine" else 1 - c
        return [pltpu.make_async_remote_copy(src_ref=outs[a].at[h], dst_ref=outs[a].at[h], send_sem=send_sems.at[a],
                                             recv_sem=recv_sems.at[a], device_id=(x, y, 1 - c), device_id_type=MESH)
                for a in range(len(outs))]

    def start(self, ins, outs, sems):
        for cp in self._copies(outs, sems, "mine"):
            cp.start()

    def end(self, ins, outs, sems):
        for cp in self._copies(outs, sems, "theirs"):
            cp.wait_recv()
        for cp in self._copies(outs, sems, "mine"):
            cp.wait_send()


class _Riders:
    def __init__(self, riders):
        self.riders = list(riders)
        self.inputs = [a for r in self.riders for a in r.inputs]
        self.out_shapes = [s for r in self.riders for s in r.out_shapes]
        self.sems = [s for r in self.riders for s in r.sems]
        self.has_mid = any(r.has_mid for r in self.riders)
        self.aliases = {}
        i0 = o0 = 0
        for r in self.riders:
            self.aliases.update({i0 + j: o0 + k for j, k in r.aliases.items()})
            i0 += len(r.inputs)
            o0 += len(r.out_shapes)

    def _each(self, ins, outs, sems):
        i0 = o0 = s0 = 0
        for r in self.riders:
            yield (r, ins[i0:i0 + len(r.inputs)], outs[o0:o0 + len(r.out_shapes)], sems[s0:s0 + len(r.sems)])
            i0, o0, s0 = i0 + len(r.inputs), o0 + len(r.out_shapes), s0 + len(r.sems)

    def start(self, ins, outs, sems):
        for r, i, o, s in self._each(ins, outs, sems):
            r.start(i, o, s)

    def mid(self, ins, outs, sems):
        for r, i, o, s in self._each(ins, outs, sems):
            if r.has_mid:
                r.mid(i, o, s)

    def end(self, ins, outs, sems):
        for r, i, o, s in self._each(ins, outs, sems):
            r.end(i, o, s)

    def split(self, outs):
        res, o0 = [], 0
        for r in self.riders:
            res.append(tuple(outs[o0:o0 + len(r.out_shapes)]))
            o0 += len(r.out_shapes)
        return res


def _chip_sum(own, parts, place, name):
    npart, h, cc = parts.shape
    th = _row_tile(h)

    def body(place_ref, own_ref, p_ref, o_ref):
        acc = own_ref[...].astype(F32) + p_ref[0].astype(F32)
        for k in range(1, npart):
            acc = acc + p_ref[k].astype(F32)
        o_ref[...] = acc

    return pl.pallas_call(
        body, name=name, out_shape=jax.ShapeDtypeStruct((2, h, cc), F32),
        grid_spec=pltpu.PrefetchScalarGridSpec(
            num_scalar_prefetch=1, grid=(h // th,),
            in_specs=[pl.BlockSpec((None, th, cc), lambda r, pr: (pr[1], r, 0)),
                      pl.BlockSpec((npart, th, cc), lambda r, pr: (0, r, 0))],
            out_specs=pl.BlockSpec((None, th, cc), lambda r, pr: (pr[0], r, 0))),
        compiler_params=_cp(("arbitrary",)),
    )(place, own, parts)


def _adamw_math(w, g, m, v):
    m = ADAM_B1 * m + (1.0 - ADAM_B1) * g
    v = ADAM_B2 * v + (1.0 - ADAM_B2) * (g * g)
    m_hat = m / (1.0 - ADAM_B1 ** ADAM_STEP)
    v_hat = v / (1.0 - ADAM_B2 ** ADAM_STEP)
    delta = -ADAM_LR * (m_hat / (jnp.sqrt(v_hat) + ADAM_EPS) + ADAM_WD * w)
    return delta, m, v


def _adamw_big(w, g, m, v, name):
    r, cc = w.shape
    tr = min(_row_tile(r), 256) if r % 256 == 0 else _row_tile(r)

    def body(w_ref, g_ref, m_ref, v_ref, go_ref, d_ref, mo_ref, vo_ref):
        g = g_ref[...]
        d, mm, vv = _adamw_math(w_ref[...], g, m_ref[...], v_ref[...])
        go_ref[...] = g
        d_ref[...] = d
        mo_ref[...] = mm
        vo_ref[...] = vv

    blk = pl.BlockSpec((tr, cc), lambda i: (i, 0))
    sd = jax.ShapeDtypeStruct((r, cc), F32)
    return pl.pallas_call(body, grid=(r // tr,), name=name, out_shape=(sd, sd, sd, sd), in_specs=[blk] * 4,
                          out_specs=(blk, blk, blk, blk), compiler_params=_cp(("arbitrary",)))(w, g, m, v)


SC_TILES = 32


def _adamw_sparsecore(ws, gs, ms, vs, name):
    n = len(ws)
    rows_per = 8
    widths = sorted({a.shape[1] for a in ws})
    assert all(a.shape[0] % rows_per == 0 and a.shape[1] % 16 == 0 for a in ws)

    def body(*refs):
        ins, outs, bufs = refs[:4 * n], refs[4 * n:8 * n], refs[8 * n:]
        tile = lax.axis_index("sc_tile") * 2 + lax.axis_index("sc_core")
        for a in range(n):
            w_hbm, g_hbm, m_hbm, v_hbm = ins[4 * a:4 * a + 4]
            go_hbm, d_hbm, mo_hbm, vo_hbm = outs[4 * a:4 * a + 4]
            r, cc = ws[a].shape
            k = widths.index(cc)
            wb, gb, mb, vb, db = bufs[5 * k:5 * k + 5]
            groups = r // rows_per

            @pl.loop(0, -(-groups // SC_TILES))
            def _(q):
                grp = tile + q * SC_TILES

                @pl.when(grp < groups)
                def _():
                    rows = pl.ds(pl.multiple_of(grp * rows_per, rows_per), rows_per)
                    pltpu.sync_copy(w_hbm.at[rows], wb)
                    pltpu.sync_copy(g_hbm.at[rows], gb)
                    pltpu.sync_copy(m_hbm.at[rows], mb)
                    pltpu.sync_copy(v_hbm.at[rows], vb)

                    @pl.loop(0, rows_per)
                    def _(i):
                        @pl.loop(0, cc, step=16)
                        def _(j):
                            at = (i, pl.ds(j, 16))
                            d, mm, vv = _adamw_math(wb[at], gb[at], mb[at], vb[at])
                            db[at] = d
                            mb[at] = mm
                            vb[at] = vv

                    pltpu.sync_copy(gb, go_hbm.at[rows])
                    pltpu.sync_copy(db, d_hbm.at[rows])
                    pltpu.sync_copy(mb, mo_hbm.at[rows])
                    pltpu.sync_copy(vb, vo_hbm.at[rows])

    args, out_type = [], []
    for a in range(n):
        args += [ws[a], gs[a], ms[a], vs[a]]
        out_type += [jax.ShapeDtypeStruct(ws[a].shape, F32)] * 4
    res = pl.kernel(
        body, name=name, out_type=tuple(out_type),
        mesh=plsc.VectorSubcoreMesh(core_axis_name="sc_core", subcore_axis_name="sc_tile"),
        scratch_types=[pltpu.VMEM((rows_per, cc), F32) for cc in widths for _ in range(5)],
    )(*args)
    return [tuple(res[4 * a:4 * a + 4]) for a in range(n)]


def _adamw_small(ws, gs, ms, vs):
    n = len(ws)

    def body(*refs):
        for a in range(n):
            w_ref, g_ref, m_ref, v_ref = refs[4 * a:4 * a + 4]
            d_ref, mo_ref, vo_ref = refs[4 * n + 3 * a:4 * n + 3 * a + 3]
            d, mm, vv = _adamw_math(w_ref[...], g_ref[...], m_ref[...], v_ref[...])
            d_ref[...] = d
            mo_ref[...] = mm
            vo_ref[...] = vv

    args, outs = [], []
    for a in range(n):
        args += [ws[a], gs[a], ms[a], vs[a]]
        outs += [jax.ShapeDtypeStruct(ws[a].shape, F32)] * 3
    res = pl.pallas_call(body, name="adamw_small", out_shape=tuple(outs), compiler_params=_cp())(*args)
    return [res[3 * a:3 * a + 3] for a in range(n)]


def _flat_pack(arrs, rows):
    flat = jnp.concatenate([a.reshape(-1) for a in arrs])
    return jnp.pad(flat, (0, rows * D - flat.shape[0])).reshape(rows, D)


def _flat_unpack(flat, shapes):
    out, off = [], 0
    for shp in shapes:
        size = 1
        for d_ in shp:
            size *= d_
        out.append(flat[off:off + size].reshape(shp))
        off += size
    return out


SMALL_EVEN = ("even_pre_g", "even_a_ln_g", "even_a_ln_b", "even_a_ws", "even_a_bs", "even_b_conv", "even_mem_g",
              "even_post_g")
SMALL_ODD = ("odd_pre_g", "odd_c_wgrp", "odd_c_scale", "odd_d_dw_w", "odd_d_dw_b", "odd_d_ln_g", "odd_d_ln_b",
             "odd_d_pw_b", "odd_mem_g", "odd_post_g")
BIG = ("even_w_in", "even_w_kv", "even_w_out", "odd_w_in", "odd_d_pw_w", "odd_w_kv", "odd_w_out")
WEIGHTS = ("even_pre_g", "even_w_in", "even_a_ln_g", "even_a_ln_b", "even_a_ws", "even_a_bs", "even_b_conv",
           "even_mem_g", "even_w_kv", "even_w_out", "even_post_g", "odd_pre_g", "odd_w_in", "odd_c_wgrp",
           "odd_c_scale", "odd_d_dw_w", "odd_d_dw_b", "odd_d_ln_g", "odd_d_ln_b", "odd_d_pw_w", "odd_d_pw_b",
           "odd_mem_g", "odd_w_kv", "odd_w_out", "odd_post_g")
PACKED = (("even_b_conv", (3, 192)), ("odd_pre_g", (1, 256)), ("odd_c_scale", (1, 192)), ("odd_d_dw_w", (31, 192)),
          ("odd_d_dw_b", (1, 192)), ("odd_d_ln_g", (1, 192)), ("odd_d_ln_b", (1, 192)), ("odd_d_pw_b", (1, 192)),
          ("odd_mem_g", (1, 256)), ("odd_post_g", (1, 256)))
PACK_ROWS = 16
SMALL_ROWS = 256


def _four(g):
    return g.reshape(N_CHIPS, 2, g.shape[0] // (2 * N_CHIPS), g.shape[1])


def _step(x, mem, target, w, mom, var, place):
    wt = {}
    pack = _flat_pack([w[n][0] for n, _ in PACKED], PACK_ROWS)
    shards = {"even_w_in_t": w["even_w_in"][0].T, "odd_w_in_t": w["odd_w_in"][0].T, "even_w_kv": w["even_w_kv"][0],
              "odd_w_kv": w["odd_w_kv"][0], "even_w_out": w["even_w_out"][0], "odd_w_out": w["odd_w_out"][0],
              "odd_d_pw_w": w["odd_d_pw_w"][0]}
    placed = {n: _place_shard(shards[n], place, BF16, "place_" + n) for n in ("even_w_in_t", "even_w_kv", "even_w_out")}
    placed["pack"] = _place_shard(pack, place, F32, "place_pack")

    order, group = _stream_tables(place[0], place[1], EVEN_IN)
    p_e, h_e, (wt["even_w_in_t"], packs), (wt["even_w_kv"], wt["even_w_out"]) = _in_fwd_streamed(
        x, w["even_pre_g"], [placed["even_w_in_t"], placed["pack"]], [placed["even_w_kv"], placed["even_w_out"]],
        order, group, "even_in_streamed")
    for n in ("odd_w_in_t", "odd_w_kv", "odd_w_out", "odd_d_pw_w"):
        placed[n] = _place_shard(shards[n], place, BF16, "place_" + n, after=p_e[0:16, 0:128])
    packs = packs.reshape(N_CHIPS, PACK_ROWS * D)
    per_chip = [_flat_unpack(packs[k], [shp for _, shp in PACKED]) for k in range(N_CHIPS)]
    for a, (name, _) in enumerate(PACKED):
        wt[name] = jnp.concatenate([per_chip[k][a] for k in range(N_CHIPS)], axis=-1)
    for name in ("even_pre_g", "even_a_ln_g", "even_a_ln_b", "even_mem_g", "even_post_g"):
        wt[name] = w[name]

    tril = jnp.tril(jnp.ones((CH, CH), dtype=bool))
    wcat = jnp.where(tril[None], w["even_a_ws"][0], 0.0).transpose(1, 0, 2).reshape(CH, 4 * CH).astype(BF16)
    bsg = jnp.repeat(w["even_a_bs"][0].T, BW // 4, axis=1)
    hsel = (jnp.arange(BW)[:, None] // (BW // 4) == jnp.arange(128)[None, :]).astype(BF16)
    g4 = BW // 4
    eye = jnp.eye(4, dtype=F32)
    wbd = (w["odd_c_wgrp"][0][:, :, None, :] * eye[:, None, :, None]).reshape(BW, BW).astype(BF16)

    kv_e = _kv_fwd(mem, wt["even_mem_g"], wt["even_w_kv"], "even_kv")
    (x1, o_e, y_e), (wt["odd_w_in_t"],) = _even_fwd(
        x, p_e, kv_e, wt["even_a_ln_g"], wt["even_a_ln_b"], wcat, bsg, wt["even_b_conv"], wt["even_w_out"],
        wt["even_post_g"], rider=_GatherRider([placed["odd_w_in_t"]]))
    names = ("odd_w_out", "odd_d_pw_w", "odd_w_kv")
    (p_o, h_o), got = _in_fwd(x1, wt["odd_pre_g"], wt["odd_w_in_t"], "odd_in",
                              rider=_GatherRider([placed[n] for n in names]))
    wt.update(zip(names, got))
    kv_o = _kv_fwd(mem, wt["odd_mem_g"], wt["odd_w_kv"], "odd_kv")
    dx2, o_o, cv_o, loss = _odd_fwd(x1, p_o, kv_o, wbd, wt["odd_c_scale"], wt["odd_d_dw_w"], wt["odd_d_dw_b"],
                                    wt["odd_d_ln_g"], wt["odd_d_ln_b"], wt["odd_d_pw_w"], wt["odd_d_pw_b"],
                                    wt["odd_w_out"], wt["odd_post_g"], target)
    (dpc_o, tmpc, tmpd, do_o, y_o, g_post_o, g_cs, g_wbd, g_dww, g_dwb, g_lng_o, g_lnb_o, g_pww, g_pwb,
     dkv_o) = _odd_bwd1(dx2, o_o, cv_o, p_o, kv_o, wbd, wt["odd_c_scale"], wt["odd_d_dw_w"], wt["odd_d_dw_b"],
                        wt["odd_d_ln_g"], wt["odd_d_ln_b"], wt["odd_d_pw_w"], wt["odd_d_pw_b"], wt["odd_w_out"],
                        wt["odd_post_g"])
    dpb_o, dx1, g_pre_o = _odd_bwd2(dpc_o, tmpc, tmpd, p_o, wt["odd_d_dw_w"], wt["odd_w_in_t"], x1,
                                    wt["odd_pre_g"], dx2)
    g_win_o = _grad_tn(dpb_o, h_o, 768, rows=ODD_IN, name="odd_gw_in_b")
    g_win_o = _grad_tn(dpc_o, h_o, 1280, out=g_win_o, rows=ODD_IN, row0=3 * BW, name="odd_gw_in_c")
    g_wout_o = _grad_tn(y_o, do_o, 1024, name="odd_gw_out")
    g_wkv_o, g_memg_o = _kv_bwd(mem, wt["odd_mem_g"], wt["odd_w_kv"], dkv_o, "odd_kv_bwd")
    big_o = [_four(g) for g in (g_win_o, g_pww.astype(BF16), g_wkv_o, g_wout_o)]
    recv_o, _ = _swap_halves(big_o, "swap_halves_odd")
    sums_o = [_pair_sum(big_o[a], recv_o[a], place, "pair_sum_odd_%d" % a) for a in range(len(big_o))]
    (dp_e, do_e, g_post_e, g_lng_e, g_lnb_e, g_wcat, g_bs, g_bconv,
     dkv_e), parts_o = _even_bwd1(dx1, o_e, p_e, kv_e, wt["even_a_ln_g"], wt["even_a_ln_b"], wcat, bsg, hsel,
                                  wt["even_b_conv"], wt["even_w_out"], wt["even_post_g"],
                                  rider=_ExchangeRider(sums_o))
    halves_o = [_chip_sum(sums_o[a], parts_o[a], place, "chip_sum_odd_%d" % a) for a in range(len(big_o))]
    g_wout_e = _grad_tn(y_e, do_e, 1024, name="even_gw_out")
    g_wkv_e, g_memg_e = _kv_bwd(mem, wt["even_mem_g"], wt["even_w_kv"], dkv_e, "even_kv_bwd")
    big_x = [_four(g) for g in (g_wkv_e, g_wout_e)]
    riders = _Riders([_SwapRider(big_x), _ShareRider(halves_o)])
    g_win_e, got = _grad_tn(dp_e, h_e, 1280, name="even_gw_in", rider=riders)
    recv_x, full_o = riders.split(got)
    sums_x = [_pair_sum(big_x[a], recv_x[a], place, "pair_sum_kv_out_%d" % a) for a in range(len(big_x))]

    def sparsecore_adamw(names, fulls, name):
        as_kept = [(lambda t: t.T) if n.endswith("w_in") else (lambda t: t) for n in names]
        res = _adamw_sparsecore([f(w[n][0]) for f, n in zip(as_kept, names)],
                                [g_.reshape(g_.shape[1] * 2, g_.shape[2]) for g_ in fulls],
                                [f(mom[n][0]) for f, n in zip(as_kept, names)],
                                [f(var[n][0]) for f, n in zip(as_kept, names)], name)
        return {n: tuple(f(t) for t in r_) for f, n, r_ in zip(as_kept, names, res)}

    upd_sc = sparsecore_adamw(("odd_w_in", "odd_d_pw_w", "odd_w_kv", "odd_w_out"), full_o, "adamw_odd_sparsecore")
    big_e = [_four(g_win_e)]
    recv_e, _ = _swap_halves(big_e, "swap_halves_even")
    sums_e = [_pair_sum(big_e[0], recv_e[0], place, "pair_sum_even_w_in")]
    (dx0, g_pre_e), parts_e = _even_bwd2(dp_e, wt["even_w_in_t"], x, wt["even_pre_g"], dx1,
                                         rider=_ExchangeRider(sums_e + sums_x))
    halves_e = [_chip_sum(sums_e[0], parts_e[0], place, "chip_sum_even_w_in")]
    halves_x = [_chip_sum(sums_x[a], parts_e[1 + a], place, "chip_sum_kv_out_%d" % a) for a in range(len(big_x))]

    g_aws = jnp.where(tril[None], g_wcat.reshape(CH, 4, CH).transpose(1, 0, 2), 0.0)
    g_wgrp = jnp.stack([lax.dynamic_slice(g_wbd, (g * g4, g * g4), (g4, g4)) for g in range(4)])
    small = {
        "even_pre_g": g_pre_e, "even_a_ln_g": g_lng_e, "even_a_ln_b": g_lnb_e, "even_a_ws": g_aws,
        "even_a_bs": g_bs[:, 0:4].T, "even_b_conv": g_bconv[0:3], "even_mem_g": g_memg_e, "even_post_g": g_post_e,
        "odd_pre_g": g_pre_o, "odd_c_wgrp": g_wgrp, "odd_c_scale": g_cs, "odd_d_dw_w": g_dww.reshape(CONF, 8, BW).sum(axis=1),
        "odd_d_dw_b": g_dwb, "odd_d_ln_g": g_lng_o, "odd_d_ln_b": g_lnb_o, "odd_d_pw_b": g_pwb,
        "odd_mem_g": g_memg_o, "odd_post_g": g_post_o,
    }
    small_names = SMALL_EVEN + SMALL_ODD
    small_pack = _flat_pack([small[n] for n in small_names] + [loss[0, 0].reshape(1)], SMALL_ROWS)
    small_total, full = _finish_reduce(small_pack, halves_e + halves_x)
    upd_sc.update(sparsecore_adamw(("even_w_kv", "even_w_out"), full[1:], "adamw_kv_out_sparsecore"))
    gbig = {"even_w_in": full[0].reshape(full[0].shape[1] * 2, full[0].shape[2])}
    return dx0, gbig, upd_sc, small_total.reshape(-1), [small[n].shape for n in small_names]


def kernel(x, mem, even_pre_g, even_w_in, even_a_ln_g, even_a_ln_b, even_a_ws, even_a_bs, even_b_conv, even_mem_g, even_w_kv, even_w_out, even_post_g, odd_pre_g, odd_w_in, odd_c_wgrp, odd_c_scale, odd_d_dw_w, odd_d_dw_b, odd_d_ln_g, odd_d_ln_b, odd_d_pw_w, odd_d_pw_b, odd_mem_g, odd_w_kv, odd_w_out, odd_post_g, loss_target, m_even_pre_g, m_even_w_in, m_even_a_ln_g, m_even_a_ln_b, m_even_a_ws, m_even_a_bs, m_even_b_conv, m_even_mem_g, m_even_w_kv, m_even_w_out, m_even_post_g, m_odd_pre_g, m_odd_w_in, m_odd_c_wgrp, m_odd_c_scale, m_odd_d_dw_w, m_odd_d_dw_b, m_odd_d_ln_g, m_odd_d_ln_b, m_odd_d_pw_w, m_odd_d_pw_b, m_odd_mem_g, m_odd_w_kv, m_odd_w_out, m_odd_post_g, v_even_pre_g, v_even_w_in, v_even_a_ln_g, v_even_a_ln_b, v_even_a_ws, v_even_a_bs, v_even_b_conv, v_even_mem_g, v_even_w_kv, v_even_w_out, v_even_post_g, v_odd_pre_g, v_odd_w_in, v_odd_c_wgrp, v_odd_c_scale, v_odd_d_dw_w, v_odd_d_dw_b, v_odd_d_ln_g, v_odd_d_ln_b, v_odd_d_pw_w, v_odd_d_pw_b, v_odd_mem_g, v_odd_w_kv, v_odd_w_out, v_odd_post_g):
    given = dict(locals())
    w = {n: given[n] for n in WEIGHTS}
    mom = {n: given["m_" + n] for n in WEIGHTS}
    var = {n: given["v_" + n] for n in WEIGHTS}

    x_, y_, c_ = lax.axis_index("x"), lax.axis_index("y"), lax.axis_index("c")
    chip = 2 * x_ + y_
    place = jnp.stack([c_, chip]).astype(jnp.int32)
    grad_x, gbig, upd_odd, gsmall_flat, small_shapes = _step(x[0], mem[0], loss_target[0], w, mom, var, place)

    names = SMALL_EVEN + SMALL_ODD
    grads = {}
    unpacked = _flat_unpack(gsmall_flat, small_shapes + [(1,)])
    loss = unpacked[-1][0]
    for n, g in zip(names, unpacked[:-1]):
        shard_shape = w[n].shape[1:]
        if g.shape[-1] != shard_shape[-1]:
            g = lax.dynamic_slice_in_dim(g, chip * shard_shape[-1], shard_shape[-1], axis=g.ndim - 1)
        grads[n] = g.reshape(shard_shape)

    def two_d(a):
        return a.reshape(-1, a.shape[-1])

    upd = {}
    for n in BIG:
        if n in upd_odd:
            res = upd_odd[n]
        elif n.endswith("w_in"):
            res = _adamw_big(w[n][0].T, gbig[n], mom[n][0].T, var[n][0].T, "adamw_" + n)
            res = tuple(r.T for r in res)
        else:
            res = _adamw_big(w[n][0], gbig[n], mom[n][0], var[n][0], "adamw_" + n)
        grads[n], upd[n] = res[0], res[1:]
    res = _adamw_small([two_d(w[n][0]) for n in names], [two_d(grads[n]) for n in names],
                       [two_d(mom[n][0]) for n in names], [two_d(var[n][0]) for n in names])
    for n, r in zip(names, res):
        upd[n] = r

    outs = [loss, grad_x[None]]
    outs += [grads[n].reshape(w[n].shape) for n in WEIGHTS]
    for j in range(3):
        outs += [upd[n][j].reshape(w[n].shape) for n in WEIGHTS]
    return tuple(outs)
```

```python
import jax
import jax.numpy as jnp
from jax import lax
from jax.experimental import pallas as pl
from jax.experimental.pallas import tpu as pltpu
from jax.experimental.pallas import tpu_sc as plsc

F32 = jnp.float32
BF16 = jnp.bfloat16
MESH = pl.DeviceIdType.MESH

D = 1024
N_MEM = 256
MIX = 2048
XA = 512
HD = 128
BW = 768
CH = 128
EPS = 1e-6
SCALE = HD ** -0.5
POOL_WINDOWS = (2, 4, 8, 16)
CONF = 31
EVEN_IN = 6400
ODD_IN = 4864
N_CHIPS = 4

ADAM_LR = 0.001
ADAM_B1 = 0.9
ADAM_B2 = 0.999
ADAM_EPS = 1e-08
ADAM_WD = 0.01
ADAM_STEP = 10

TS = 256
HALO = 32
VMEM_LIMIT = 56 * 1024 * 1024


def _cp(sem=None):
    return pltpu.CompilerParams(dimension_semantics=sem, vmem_limit_bytes=VMEM_LIMIT)


def _dot(a, b):
    return jnp.dot(a, b, preferred_element_type=F32)


def _dot_nt(a, b):
    return lax.dot_general(a, b, (((1,), (1,)), ((), ())), preferred_element_type=F32)


def _dot_tn(a, b):
    return lax.dot_general(a, b, (((0,), (0,)), ((), ())), preferred_element_type=F32)


def _sigmoid(x):
    return 1.0 / (1.0 + jnp.exp(-x))


def _resident(shape):
    return pl.BlockSpec(shape, lambda *_: (0,) * len(shape), pipeline_mode=pl.Buffered(1))


def _const(shape):
    return pl.BlockSpec(shape, lambda *_: (0,) * len(shape))


def _kv_fwd(mem, mem_g, wkv, name):
    def body(mem_ref, g_ref, w_ref, kv_ref):
        m = mem_ref[...]
        r = lax.rsqrt(jnp.mean(m * m, axis=-1, keepdims=True) + EPS)
        mn = (m * r * g_ref[...]).astype(BF16)
        kv_ref[...] = _dot(mn, w_ref[...]).astype(BF16)

    return pl.pallas_call(body, out_shape=jax.ShapeDtypeStruct((N_MEM, D), BF16), name=name,
                          compiler_params=_cp())(mem, mem_g, wkv)


def _kv_bwd(mem, mem_g, wkv, dkv, name):
    def body(mem_ref, g_ref, w_ref, dkv_ref, dw_ref, dg_ref):
        m = mem_ref[...]
        r = lax.rsqrt(jnp.mean(m * m, axis=-1, keepdims=True) + EPS)
        mh = m * r
        mn = (mh * g_ref[...]).astype(BF16)
        dkv = dkv_ref[...].astype(BF16)
        dw_ref[...] = _dot_tn(mn, dkv).astype(BF16)
        dmn = _dot_nt(dkv, w_ref[...])
        dg_ref[...] = jnp.sum(dmn * mh, axis=0, keepdims=True)

    return pl.pallas_call(body, out_shape=(jax.ShapeDtypeStruct((D, D), BF16), jax.ShapeDtypeStruct((1, D), F32)),
                          name=name, compiler_params=_cp())(mem, mem_g, wkv, dkv)


def _host_call(body, *, grid, name, out_shape, in_specs, out_specs, args, scratch_shapes=(), aliases=None,
               rider=None):
    sem = ("arbitrary",) * len(grid)
    aliases = dict(aliases or {})
    if rider is None:
        res = pl.pallas_call(body, grid=grid, name=name, out_shape=tuple(out_shape), in_specs=list(in_specs),
                             out_specs=tuple(out_specs), scratch_shapes=list(scratch_shapes),
                             input_output_aliases=aliases, compiler_params=_cp(sem))(*args)
        return tuple(res), ()
    n_in, n_out, n_sc = len(in_specs), len(out_specs), len(scratch_shapes)
    r_in, r_out = len(rider.inputs), len(rider.out_shapes)

    def full_body(*refs):
        host_in = refs[:n_in]
        rid_in = refs[n_in:n_in + r_in]
        host_out = refs[n_in + r_in:n_in + r_in + n_out]
        rid_out = refs[n_in + r_in + n_out:n_in + r_in + n_out + r_out]
        host_sc = refs[n_in + r_in + n_out + r_out:n_in + r_in + n_out + r_out + n_sc]
        sems = refs[n_in + r_in + n_out + r_out + n_sc:]
        first = pl.program_id(0) == 0
        last = pl.program_id(0) == grid[0] - 1
        for ax in range(1, len(grid)):
            first = jnp.logical_and(first, pl.program_id(ax) == 0)
            last = jnp.logical_and(last, pl.program_id(ax) == grid[ax] - 1)

        @pl.when(first)
        def _():
            rider.start(rid_in, rid_out, sems)

        if rider.has_mid:
            @pl.when(last)
            def _():
                rider.mid(rid_in, rid_out, sems)

        body(*host_in, *host_out, *host_sc)

        @pl.when(last)
        def _():
            rider.end(rid_in, rid_out, sems)

    aliases.update({n_in + j: n_out + k for j, k in rider.aliases.items()})
    res = pl.pallas_call(
        full_body, grid=grid, name=name, out_shape=tuple(out_shape) + tuple(rider.out_shapes),
        in_specs=list(in_specs) + _hbm_specs(r_in), out_specs=tuple(out_specs) + tuple(_hbm_specs(r_out)),
        scratch_shapes=list(scratch_shapes) + list(rider.sems), input_output_aliases=aliases,
        compiler_params=_cp(sem),
    )(*args, *rider.inputs)
    return tuple(res[:n_out]), tuple(res[n_out:])


def _in_fwd(x, pre_g, w_t, name, rider=None):
    s, n = x.shape[0], w_t.shape[0]
    tm = min(512, s)
    nc = 256

    def body(x_ref, g_ref, w_ref, p_ref, h_ref):
        xv = x_ref[...]
        r = lax.rsqrt(jnp.mean(xv * xv, axis=-1, keepdims=True) + EPS)
        h = (xv * r * g_ref[...]).astype(BF16)
        h_ref[...] = h
        for j in range(n // nc):
            p_ref[:, j * nc:(j + 1) * nc] = _dot_nt(h, w_ref[j * nc:(j + 1) * nc, :]).astype(BF16)

    return _host_call(
        body, grid=(s // tm,), name=name, rider=rider,
        out_shape=(jax.ShapeDtypeStruct((s, n), BF16), jax.ShapeDtypeStruct((s, D), BF16)),
        in_specs=[pl.BlockSpec((tm, D), lambda i: (i, 0)), _const((1, D)), _resident((n, D))],
        out_specs=(pl.BlockSpec((tm, n), lambda i: (i, 0)), pl.BlockSpec((tm, D), lambda i: (i, 0))),
        args=(x, pre_g, w_t))


NC = 256


def _stream_tables(core, chip, n):
    nchunk = n // NC
    idx = jnp.arange(nchunk, dtype=jnp.int32)
    src = jnp.array([0, 2, 1, 3], jnp.int32)
    r = n // N_CHIPS

    def group_of(row):
        j = src[(row // r) ^ chip]
        through_sibling = ((row % r) // (r // 2) != core).astype(jnp.int32)
        return jnp.where(j == 0, 0, 2 * j - 1 + through_sibling)

    grp = jnp.maximum(group_of(idx * NC), group_of(idx * NC + NC - 1))
    order = jnp.argsort(grp * 64 + idx).astype(jnp.int32)
    return order, grp[order]


def _in_fwd_streamed(x, pre_g, first, later, order, group, name):
    s, n = x.shape[0], first[0].shape[0]
    nchunk = n // NC
    rider = _GatherRider(first)
    rider2 = _GatherRider(later) if later else None
    a, m = len(first), len(later)
    tr = min(256, s)

    def body(*refs):
        order_ref, group_ref, x_ref, g_ref = refs[0:4]
        p_ref, h_ref = refs[4 + a + m:6 + a + m]
        outs = refs[6 + a + m:6 + 2 * a + m]
        outs2 = refs[6 + 2 * a + m:6 + 2 * a + 2 * m]
        wbuf, wsem, send_sems, recv_sems = refs[6 + 2 * a + 2 * m:10 + 2 * a + 2 * m]
        sems2 = refs[10 + 2 * a + 2 * m:]
        w_hbm = outs[0]
        j = pl.program_id(0)
        sems = (send_sems, recv_sems)
        grp = group_ref[j]
        new_group = jnp.logical_or(j == 0, group_ref[jnp.maximum(j - 1, 0)] != grp)
        slot = j % 2

        def fetch(step, sl):
            rows = pl.ds(pl.multiple_of(order_ref[step] * NC, NC), NC)
            return pltpu.make_async_copy(w_hbm.at[rows], wbuf.at[sl], wsem.at[sl])

        @pl.when(j == 0)
        def _():
            rider.start(None, outs, sems, peers=(0, 1))

            @pl.loop(0, s // tr)
            def _(t):
                rows = pl.ds(pl.multiple_of(t * tr, tr), tr)
                xv = x_ref[rows, :]
                r = lax.rsqrt(jnp.mean(xv * xv, axis=-1, keepdims=True) + EPS)
                h_ref[rows, :] = (xv * r * g_ref[...]).astype(BF16)

        before = jnp.where(j == 0, 0, group_ref[jnp.maximum(j - 1, 0)])

        def entering(b):
            return jnp.logical_and(before < b, b <= grp)

        for src in range(3):
            @pl.when(entering(2 * src + 1))
            def _(src=src):
                if src == 0:
                    rider.start(None, outs, sems, peers=(2,))
                rider.mid(None, outs, sems, peers=(src,))
                if src == 1 and rider2 is not None:
                    rider2.start(None, outs2, sems2)

            @pl.when(entering(2 * src + 2))
            def _(src=src):
                rider.wait_forwarded(outs, sems, peers=(src,))

        @pl.when(new_group)
        def _():
            fetch(j, slot).start()

        fetch(j, slot).wait()
        nxt = jnp.minimum(j + 1, nchunk - 1)

        @pl.when(jnp.logical_and(j + 1 < nchunk, group_ref[nxt] == grp))
        def _():
            fetch(nxt, 1 - slot).start()

        p_ref[...] = _dot_nt(h_ref[...], wbuf[slot]).astype(BF16)

        @pl.when(j == nchunk - 1)
        def _():
            rider.wait_sends(outs, sems)
            if rider2 is not None:
                rider2.mid(None, outs2, sems2)
                rider2.end(None, outs2, sems2)

    hbm = pl.BlockSpec(memory_space=pltpu.HBM)
    arrs = list(first) + list(later)
    whole = pl.BlockSpec((s, D), lambda j, o, g: (0, 0), pipeline_mode=pl.Buffered(1))
    res = pl.pallas_call(
        body, name=name,
        out_shape=(jax.ShapeDtypeStruct((s, n), BF16), jax.ShapeDtypeStruct((s, D), BF16))
        + tuple(jax.ShapeDtypeStruct(v.shape, v.dtype) for v in arrs),
        grid_spec=pltpu.PrefetchScalarGridSpec(
            num_scalar_prefetch=2, grid=(nchunk,),
            in_specs=[whole, pl.BlockSpec((1, D), lambda j, o, g: (0, 0))] + [hbm] * (a + m),
            out_specs=(pl.BlockSpec((s, NC), lambda j, o, g: (0, o[j])),
                       pl.BlockSpec((s, D), lambda j, o, g: (0, 0))) + (hbm,) * (a + m),
            scratch_shapes=[pltpu.VMEM((2, NC, D), BF16), pltpu.SemaphoreType.DMA((2,))] + list(rider.sems)
            + (list(rider2.sems) if rider2 is not None else [])),
        input_output_aliases={4 + v: 2 + v for v in range(a + m)},
        compiler_params=_cp(("arbitrary",)),
    )(order, group, x, pre_g, *arrs)
    return res[0], res[1], tuple(res[2:2 + a]), tuple(res[2 + a:])


def _xattn_fwd(q, kv_ref):
    outs, probs = [], []
    for h in range(XA // HD):
        qh = q[:, h * HD:(h + 1) * HD]
        kh = kv_ref[:, h * HD:(h + 1) * HD]
        vh = kv_ref[:, XA + h * HD:XA + (h + 1) * HD]
        sc = _dot_nt(qh, kh) * SCALE
        e = jnp.exp(sc - jnp.max(sc, axis=-1, keepdims=True))
        pr = e / jnp.sum(e, axis=-1, keepdims=True)
        outs.append(_dot(pr.astype(BF16), vh))
        probs.append(pr)
    return jnp.concatenate(outs, axis=-1), probs


def _xattn_bwd(dyx, q, probs, kv_ref, dkv_ref):
    dqs = []
    for h in range(XA // HD):
        qh = q[:, h * HD:(h + 1) * HD]
        kh = kv_ref[:, h * HD:(h + 1) * HD]
        vh = kv_ref[:, XA + h * HD:XA + (h + 1) * HD]
        dy = dyx[:, h * HD:(h + 1) * HD].astype(BF16)
        pr = probs[h]
        dp = _dot_nt(dy, vh)
        ds = (pr * (dp - jnp.sum(dp * pr, axis=-1, keepdims=True))).astype(BF16)
        dqs.append(_dot(ds, kh) * SCALE)
        dkv_ref[:, h * HD:(h + 1) * HD] += _dot_tn(ds, qh) * SCALE
        dkv_ref[:, XA + h * HD:XA + (h + 1) * HD] += _dot_tn(pr.astype(BF16), dy)
    return jnp.concatenate(dqs, axis=-1)


def _layer_norm_fwd(v, g, b):
    mu = jnp.mean(v, axis=-1, keepdims=True)
    vc = v - mu
    rstd = lax.rsqrt(jnp.mean(vc * vc, axis=-1, keepdims=True) + EPS)
    vhat = vc * rstd
    return vhat * g + b, vhat, rstd


def _layer_norm_bwd(dy, vhat, rstd, g):
    dvh = dy * g
    return rstd * (dvh - jnp.mean(dvh, axis=-1, keepdims=True) - vhat * jnp.mean(dvh * vhat, axis=-1, keepdims=True))


def _head_masks():
    col = lax.broadcasted_iota(jnp.int32, (1, BW), 1)
    return [(col >= h * (BW // 4)) & (col < (h + 1) * (BW // 4)) for h in range(4)]


def _halo_prev(nblk_per_tile):
    return lambda i: (jnp.maximum(i * nblk_per_tile - 1, 0), 0)


def _row_ids(i, t):
    return i * t + lax.broadcasted_iota(jnp.int32, (t, 1), 0)


def _even_mix(i, p_ref, ph_ref, ln_g, ln_b, wcat_ref, bsg_ref, bconv_ref, wbuf):
    t = p_ref.shape[0]
    u = p_ref[:, 0:BW].astype(F32)
    v = p_ref[:, BW:2 * BW].astype(F32)
    bg = p_ref[:, 2 * BW:3 * BW].astype(F32)
    cg = p_ref[:, 3 * BW:4 * BW].astype(F32)
    xin = p_ref[:, 4 * BW:5 * BW].astype(F32)
    vn, vhat, rstd = _layer_norm_fwd(v, ln_g, ln_b)
    masks = _head_masks()
    sgs, vsts = [], []
    for n in range(t // CH):
        vn_c = vn[n * CH:(n + 1) * CH]
        vst = jnp.concatenate([jnp.where(m, vn_c, 0.0) for m in masks], axis=0).astype(BF16)
        sgs.append(_dot(wcat_ref[...], vst) + bsg_ref[...])
        vsts.append(vst)
    sg = jnp.concatenate(sgs, axis=0)
    ya = u * sg
    w_halo = ph_ref[:, 3 * BW:4 * BW].astype(F32) * ph_ref[:, 4 * BW:5 * BW].astype(F32)
    wbuf[0:HALO, :] = jnp.where(i > 0, w_halo, 0.0)
    wbuf[HALO:HALO + t, :] = cg * xin
    conv = (bconv_ref[0:1, :] * wbuf[pl.ds(HALO - 2, t), :] + bconv_ref[1:2, :] * wbuf[pl.ds(HALO - 1, t), :]
            + bconv_ref[2:3, :] * wbuf[pl.ds(HALO, t), :])
    yb = bg * conv
    return dict(u=u, bg=bg, cg=cg, xin=xin, vhat=vhat, rstd=rstd, sg=sg, vsts=vsts, conv=conv, ya=ya, yb=yb,
                masks=masks)


def _pool_select(vals):
    col = lax.broadcasted_iota(jnp.int32, (1, BW), 1)
    g = BW // 4
    return jnp.where(col < g, vals[0], jnp.where(col < 2 * g, vals[1], jnp.where(col < 3 * g, vals[2], vals[3])))


def _inv_counts(i, t):
    rows = _row_ids(i, t) + 1
    return [1.0 / jnp.minimum(rows, w).astype(F32) for w in POOL_WINDOWS]


def _band_matrices(t, forward):
    j = jnp.arange(t)[:, None]
    r = jnp.arange(HALO + t)[None, :]
    if forward:
        return jnp.stack([(r >= j) & (r < j + w) for w in POOL_WINDOWS]).astype(BF16)
    return jnp.stack([(r <= HALO + j) & (r > HALO + j - w) for w in POOL_WINDOWS]).astype(BF16)


SHIFT_ROWS = HALO + TS - 8


def _shifted_copies(buf, sh):
    for b in range(1, 8):
        sh[b - 1] = buf[pl.ds(b, SHIFT_ROWS), :]


def _rows_at(buf, sh, off, t):
    a, b = divmod(off, 8)
    return buf[pl.ds(8 * a, t), :] if b == 0 else sh[b - 1, pl.ds(8 * a, t), :]


def _tap_sums(d_ref, buf, sh, base, out_ref):
    t = d_ref.shape[0]
    group = 4
    for k0 in range(0, CONF, group):
        taps = list(range(k0, min(k0 + group, CONF)))

        def step(r, accs, taps=taps):
            row = pl.multiple_of(r * 8, 8)
            d = d_ref[pl.ds(row, 8), :]
            new = []
            for acc, k in zip(accs, taps):
                a, b = divmod(base + k, 8)
                src = buf[pl.ds(row + 8 * a, 8), :] if b == 0 else sh[b - 1, pl.ds(row + 8 * a, 8), :]
                new.append(acc + d * src)
            return tuple(new)

        accs = lax.fori_loop(0, t // 8, step, tuple(jnp.zeros((8, BW), F32) for _ in taps), unroll=2)
        for acc, k in zip(accs, taps):
            out_ref[8 * k:8 * k + 8, :] += acc


def _odd_mix(i, p_ref, ph_ref, bands_ref, wbd_ref, cscale, dww_ref, dwb, ln_g, ln_b, pww_ref, pwb, gbuf, gsh,
             cv=None):
    t = p_ref.shape[0]
    zc_bf = p_ref[:, 0:BW]
    zc = zc_bf.astype(F32)
    ga = p_ref[:, BW:2 * BW].astype(F32)
    gb = p_ref[:, 2 * BW:3 * BW].astype(F32)
    zh = ph_ref[:, 0:BW]
    zcat = jnp.concatenate([jnp.where(i > 0, zh, jnp.zeros_like(zh)), zc_bf], axis=0)
    inv = _inv_counts(i, t)
    pooled = _pool_select([_dot(bands_ref[w], zcat) * inv[w] for w in range(len(POOL_WINDOWS))]) - zc
    pooled_bf = pooled.astype(BF16)
    pre = _dot(pooled_bf, wbd_ref[...])
    yc = pre * cscale
    sgb = _sigmoid(gb)
    z = ga * sgb
    gh_a = ph_ref[:, BW:2 * BW].astype(F32)
    gh_b = ph_ref[:, 2 * BW:3 * BW].astype(F32)
    gbuf[0:HALO, :] = jnp.where(i > 0, gh_a * _sigmoid(gh_b), 0.0)
    gbuf[HALO:HALO + t, :] = z
    _shifted_copies(gbuf, gsh)
    if cv is None:
        cv = dwb + dww_ref[CONF - 1:CONF, :] * z
        for k in range(CONF - 1):
            cv = cv + dww_ref[k:k + 1, :] * _rows_at(gbuf, gsh, HALO - (CONF - 1) + k, t)
    zl, zhat, rstd = _layer_norm_fwd(cv, ln_g, ln_b)
    szl = _sigmoid(zl)
    zs = (zl * szl).astype(BF16)
    yd = _dot(zs, pww_ref[...]) + pwb
    return dict(ga=ga, sgb=sgb, pooled_bf=pooled_bf, pre=pre, yc=yc, zhat=zhat, rstd=rstd, zl=zl, szl=szl,
                zs=zs, yd=yd, inv=inv, cv=cv)


def _post_norm(o, post_g):
    r = lax.rsqrt(jnp.mean(o * o, axis=-1, keepdims=True) + EPS)
    return o * r, r


def _gate_out(y_a, y_b, y_x, gate, wout_ref):
    sgt = _sigmoid(gate)
    sgate = gate * sgt
    ys = [(y_a * sgate[:, 0:BW]).astype(BF16), (y_b * sgate[:, BW:2 * BW]).astype(BF16),
          (y_x * sgate[:, 2 * BW:MIX]).astype(BF16)]
    o = (_dot(ys[0], wout_ref[0:BW, :]) + _dot(ys[1], wout_ref[BW:2 * BW, :]) + _dot(ys[2], wout_ref[2 * BW:MIX, :]))
    return o, ys, sgt, sgate


def _tile_specs(s, n):
    nh = TS // HALO
    return pl.BlockSpec((TS, n), lambda i: (i, 0)), pl.BlockSpec((HALO, n), _halo_prev(nh))


def _even_fwd(x, p, kv, ln_g, ln_b, wcat, bsg, bconv, wout, post_g, rider=None):
    s = x.shape[0]

    def body(x_ref, p_ref, ph_ref, kv_ref, lng, lnb, wcat_ref, bsg_ref, bconv_ref, wout_ref, pg, x1_ref, o_ref,
             y_ref, wbuf):
        i = pl.program_id(0)
        mx = _even_mix(i, p_ref, ph_ref, lng[...], lnb[...], wcat_ref, bsg_ref, bconv_ref, wbuf)
        yx, _ = _xattn_fwd(p_ref[:, 5 * BW:5 * BW + XA], kv_ref)
        gate = p_ref[:, 5 * BW + XA:EVEN_IN].astype(F32)
        o, ys, _, _ = _gate_out(mx["ya"], mx["yb"], yx, gate, wout_ref)
        y_ref[:, 0:BW] = ys[0]
        y_ref[:, BW:2 * BW] = ys[1]
        y_ref[:, 2 * BW:MIX] = ys[2]
        n, _ = _post_norm(o, pg[...])
        o_ref[...] = o
        x1_ref[...] = x_ref[...] + n * pg[...]

    tile, halo = _tile_specs(s, EVEN_IN)
    row = pl.BlockSpec((TS, D), lambda i: (i, 0))
    return _host_call(
        body, grid=(s // TS,), name="even_fwd", rider=rider,
        out_shape=(jax.ShapeDtypeStruct((s, D), F32), jax.ShapeDtypeStruct((s, D), F32),
                   jax.ShapeDtypeStruct((s, MIX), BF16)),
        in_specs=[row, tile, halo, _const((N_MEM, D)), _const((1, BW)), _const((1, BW)), _const((CH, 4 * CH)),
                  _const((CH, BW)), _const((3, BW)), _resident((MIX, D)), _const((1, D))],
        out_specs=(row, row, pl.BlockSpec((TS, MIX), lambda i: (i, 0))),
        scratch_shapes=[pltpu.VMEM((HALO + TS, BW), F32)],
        args=(x, p, p, kv, ln_g, ln_b, wcat, bsg, bconv, wout, post_g))


def _odd_fwd(x1, p, kv, wbd, cscale, dww, dwb, ln_g, ln_b, pww, pwb, wout, post_g, target):
    s = x1.shape[0]

    def body(x_ref, p_ref, ph_ref, kv_ref, bands_ref, wbd_ref, cs, dww_ref, dwb_ref, lng, lnb, pww_ref, pwb_ref,
             wout_ref, pg, tgt_ref, dx_ref, o_ref, cv_ref, loss_ref, gbuf, gsh):
        i = pl.program_id(0)
        mx = _odd_mix(i, p_ref, ph_ref, bands_ref, wbd_ref, cs[...], dww_ref, dwb_ref[...], lng[...], lnb[...],
                      pww_ref, pwb_ref[...], gbuf, gsh)
        cv_ref[...] = mx["cv"]
        yx, _ = _xattn_fwd(p_ref[:, 3 * BW:3 * BW + XA], kv_ref)
        gate = p_ref[:, 3 * BW + XA:ODD_IN].astype(F32)
        o, _, _, _ = _gate_out(mx["yc"], mx["yd"], yx, gate, wout_ref)
        n, _ = _post_norm(o, pg[...])
        o_ref[...] = o
        err = x_ref[...] + n * pg[...] - tgt_ref[...]
        dx_ref[...] = err * (1.0 / D)

        @pl.when(i == 0)
        def _():
            loss_ref[...] = jnp.zeros_like(loss_ref)

        loss_ref[...] += 0.5 * jnp.sum(jnp.sum(err * err, axis=-1, keepdims=True) * (1.0 / D), axis=0, keepdims=True)

    tile, halo = _tile_specs(s, ODD_IN)
    row = pl.BlockSpec((TS, D), lambda i: (i, 0))
    vec = _const((1, BW))
    return pl.pallas_call(
        body, grid=(s // TS,), name="odd_fwd",
        out_shape=(jax.ShapeDtypeStruct((s, D), F32), jax.ShapeDtypeStruct((s, D), F32),
                   jax.ShapeDtypeStruct((s, BW), F32), jax.ShapeDtypeStruct((8, 128), F32)),
        in_specs=[row, tile, halo, _const((N_MEM, D)), _const((4, TS, HALO + TS)), _const((BW, BW)), vec,
                  _const((CONF, BW)), vec, vec, vec, _const((BW, BW)), vec, _resident((MIX, D)), _const((1, D)), row],
        out_specs=(row, row, pl.BlockSpec((TS, BW), lambda i: (i, 0)), _const((8, 128))),
        scratch_shapes=[pltpu.VMEM((HALO + TS, BW), F32), pltpu.VMEM((7, SHIFT_ROWS, BW), F32)],
        compiler_params=_cp(("arbitrary",)),
    )(x1, p, p, kv, _band_matrices(TS, False), wbd, cscale, dww, dwb, ln_g, ln_b, pww, pwb, wout, post_g, target)


def _acc_init(i, refs):
    @pl.when(i == 0)
    def _():
        for r in refs:
            r[...] = jnp.zeros_like(r)


def _post_norm_bwd(dx, o, pg, dpg_ref):
    n, r = _post_norm(o, pg)
    dpg_ref[...] += jnp.sum(dx * n, axis=0, keepdims=True)
    dn = dx * pg
    return (r * (dn - n * jnp.mean(dn * n, axis=-1, keepdims=True))).astype(BF16)


def _gate_bwd(do, wout_ref, ys_f32, gate, y_ref):
    dy = _dot_nt(do, wout_ref[...])
    sgt = _sigmoid(gate)
    sgate = gate * sgt
    dsilu = sgt * (1.0 + gate * (1.0 - sgt))
    offs = (0, BW, 2 * BW, MIX)
    dys, dgs = [], []
    for j, yv in enumerate(ys_f32):
        a, b = offs[j], offs[j + 1]
        if y_ref is not None:
            y_ref[:, a:b] = (yv * sgate[:, a:b]).astype(BF16)
        dys.append(dy[:, a:b] * sgate[:, a:b])
        dgs.append(dy[:, a:b] * yv * dsilu[:, a:b])
    return dys, jnp.concatenate(dgs, axis=-1)


NEXT = 16


def _even_bwd1(dx, o, p, kv, ln_g, ln_b, wcat, bsg, hsel, bconv, wout, post_g, rider=None):
    s = dx.shape[0]
    nt = s // TS

    def body(dx_ref, o_ref, p_ref, ph_ref, dxn_ref, on_ref, pn_ref, kv_ref, lng, lnb, wcat_ref, bsg_ref, hsel_ref,
             bconv_ref, wout_ref, pg,
             dp_ref, do_ref, dpg_ref, dlng_ref, dlnb_ref, dwcat_ref, dbs_ref, dbconv_ref, dkv_ref, wbuf, dbuf):
        i = pl.program_id(0)
        _acc_init(i, (dpg_ref, dlng_ref, dlnb_ref, dwcat_ref, dbs_ref, dbconv_ref, dkv_ref))
        mx = _even_mix(i, p_ref, ph_ref, lng[...], lnb[...], wcat_ref, bsg_ref, bconv_ref, wbuf)
        q = p_ref[:, 5 * BW:5 * BW + XA]
        yx, probs = _xattn_fwd(q, kv_ref)
        gate = p_ref[:, 5 * BW + XA:EVEN_IN].astype(F32)
        do = _post_norm_bwd(dx_ref[...], o_ref[...], pg[...], dpg_ref)
        do_ref[...] = do
        (dya, dyb, dyx), dgate = _gate_bwd(do, wout_ref, (mx["ya"], mx["yb"], yx), gate, None)
        dp_ref[:, 0:BW] = (dya * mx["sg"]).astype(BF16)
        dsg = (dya * mx["u"]).astype(BF16)
        dvns = []
        for n in range(TS // CH):
            dsg_c = dsg[n * CH:(n + 1) * CH]
            dvst = _dot_tn(wcat_ref[...], dsg_c)
            dvn_c = jnp.where(mx["masks"][0], dvst[0:CH], 0.0)
            for h in range(1, 4):
                dvn_c = dvn_c + jnp.where(mx["masks"][h], dvst[h * CH:(h + 1) * CH], 0.0)
            dvns.append(dvn_c)
            dwcat_ref[...] += _dot_nt(dsg_c, mx["vsts"][n])
            dbs_ref[...] += _dot(dsg_c, hsel_ref[...])
        dvn = jnp.concatenate(dvns, axis=0)
        dlng_ref[...] += jnp.sum(dvn * mx["vhat"], axis=0, keepdims=True)
        dlnb_ref[...] += jnp.sum(dvn, axis=0, keepdims=True)
        dp_ref[:, BW:2 * BW] = _layer_norm_bwd(dvn, mx["vhat"], mx["rstd"], lng[...]).astype(BF16)
        dp_ref[:, 2 * BW:3 * BW] = (dyb * mx["conv"]).astype(BF16)
        dconv = dyb * mx["bg"]
        for k in range(3):
            dbconv_ref[k:k + 1, :] += jnp.sum(dconv * wbuf[pl.ds(HALO - 2 + k, TS), :], axis=0, keepdims=True)
        n_n, r_n = _post_norm(on_ref[...], pg[...])
        dn_n = dxn_ref[...] * pg[...]
        do_n = (r_n * (dn_n - n_n * jnp.mean(dn_n * n_n, axis=-1, keepdims=True))).astype(BF16)
        dy_n = _dot_nt(do_n, wout_ref[BW:2 * BW, :])
        g_n = pn_ref[:, 5 * BW + XA + BW:5 * BW + XA + 2 * BW].astype(F32)
        dconv_n = dy_n * (g_n * _sigmoid(g_n)) * pn_ref[:, 2 * BW:3 * BW].astype(F32)
        dbuf[0:TS, :] = dconv
        dbuf[TS:TS + NEXT, :] = jnp.where(i < nt - 1, dconv_n, 0.0)
        dw = (bconv_ref[2:3, :] * dconv + bconv_ref[1:2, :] * dbuf[pl.ds(1, TS), :]
              + bconv_ref[0:1, :] * dbuf[pl.ds(2, TS), :])
        dp_ref[:, 3 * BW:4 * BW] = (dw * mx["xin"]).astype(BF16)
        dp_ref[:, 4 * BW:5 * BW] = (dw * mx["cg"]).astype(BF16)
        dp_ref[:, 5 * BW:5 * BW + XA] = _xattn_bwd(dyx, q, probs, kv_ref, dkv_ref).astype(BF16)
        dp_ref[:, 5 * BW + XA:EVEN_IN] = dgate.astype(BF16)

    tile, halo = _tile_specs(s, EVEN_IN)
    row = pl.BlockSpec((TS, D), lambda i: (i, 0))
    vec = _const((1, BW))
    nxt = _halo_next(TS // NEXT, s // NEXT)

    def out(n):
        return pl.BlockSpec((TS, n), lambda i: (i, 0))

    return _host_call(
        body, grid=(nt,), name="even_bwd1", rider=rider,
        out_shape=(jax.ShapeDtypeStruct((s, EVEN_IN), BF16), jax.ShapeDtypeStruct((s, D), BF16),
                   jax.ShapeDtypeStruct((1, D), F32), jax.ShapeDtypeStruct((1, BW), F32),
                   jax.ShapeDtypeStruct((1, BW), F32), jax.ShapeDtypeStruct((CH, 4 * CH), F32),
                   jax.ShapeDtypeStruct((CH, 128), F32), jax.ShapeDtypeStruct((8, BW), F32),
                   jax.ShapeDtypeStruct((N_MEM, D), F32)),
        in_specs=[row, row, tile, halo, pl.BlockSpec((NEXT, D), nxt), pl.BlockSpec((NEXT, D), nxt),
                  pl.BlockSpec((NEXT, EVEN_IN), nxt), _const((N_MEM, D)), vec, vec, _const((CH, 4 * CH)),
                  _const((CH, BW)), _const((BW, 128)), _const((3, BW)), _resident((MIX, D)), _const((1, D))],
        out_specs=(out(EVEN_IN), out(D),
                   _const((1, D)), vec, vec, _const((CH, 4 * CH)), _const((CH, 128)), _const((8, BW)),
                   _const((N_MEM, D))),
        scratch_shapes=[pltpu.VMEM((HALO + TS, BW), F32), pltpu.VMEM((TS + NEXT, BW), F32)],
        args=(dx, o, p, p, dx, o, p, kv, ln_g, ln_b, wcat, bsg, hsel, bconv, wout, post_g))


def _odd_bwd1(dx, o, cv, p, kv, wbd, cscale, dww, dwb, ln_g, ln_b, pww, pwb, wout, post_g):
    s = dx.shape[0]

    def body(dx_ref, o_ref, cv_ref, p_ref, ph_ref, kv_ref, bands_ref, wbd_ref, cs, dww_ref, dwb_ref, lng, lnb,
             pww_ref, pwb_ref, wout_ref, pg,
             dpc_ref, tmpc_ref, tmpd_ref, do_ref, y_ref, dpg_ref, dcs_ref, dwbd_ref, ddww_ref, ddwb_ref, dlng_ref,
             dlnb_ref, dpww_ref, dpwb_ref, dkv_ref, gbuf, gsh, dcv_buf):
        i = pl.program_id(0)
        _acc_init(i, (dpg_ref, dcs_ref, dwbd_ref, ddww_ref, ddwb_ref, dlng_ref, dlnb_ref, dpww_ref, dpwb_ref,
                      dkv_ref))
        mx = _odd_mix(i, p_ref, ph_ref, bands_ref, wbd_ref, cs[...], dww_ref, dwb_ref[...], lng[...], lnb[...],
                      pww_ref, pwb_ref[...], gbuf, gsh, cv=cv_ref[...])
        q = p_ref[:, 3 * BW:3 * BW + XA]
        yx, probs = _xattn_fwd(q, kv_ref)
        gate = p_ref[:, 3 * BW + XA:ODD_IN].astype(F32)
        do = _post_norm_bwd(dx_ref[...], o_ref[...], pg[...], dpg_ref)
        do_ref[...] = do
        (dyc, dyd, dyx), dgate = _gate_bwd(do, wout_ref, (mx["yc"], mx["yd"], yx), gate, y_ref)
        dcs_ref[...] += jnp.sum(dyc * mx["pre"], axis=0, keepdims=True)
        dpre = (dyc * cs[...]).astype(BF16)
        dwbd_ref[...] += _dot_tn(mx["pooled_bf"], dpre)
        dpooled = _dot_nt(dpre, wbd_ref[...])
        tmpc_ref[...] = _pool_select([dpooled * c_ for c_ in mx["inv"]]).astype(BF16)
        dyd_bf = dyd.astype(BF16)
        dpwb_ref[...] += jnp.sum(dyd, axis=0, keepdims=True)
        dpww_ref[...] += _dot_tn(mx["zs"], dyd_bf)
        dzs = _dot_nt(dyd_bf, pww_ref[...])
        zl, szl = mx["zl"], mx["szl"]
        dzl = dzs * (szl * (1.0 + zl * (1.0 - szl)))
        dlng_ref[...] += jnp.sum(dzl * mx["zhat"], axis=0, keepdims=True)
        dlnb_ref[...] += jnp.sum(dzl, axis=0, keepdims=True)
        dcv = _layer_norm_bwd(dzl, mx["zhat"], mx["rstd"], lng[...])
        tmpd_ref[...] = dcv.astype(BF16)
        ddwb_ref[...] += jnp.sum(dcv, axis=0, keepdims=True)
        dcv_buf[...] = dcv
        _tap_sums(dcv_buf, gbuf, gsh, HALO - (CONF - 1), ddww_ref)
        dpc_ref[:, 0:XA] = _xattn_bwd(dyx, q, probs, kv_ref, dkv_ref).astype(BF16)
        dpc_ref[:, XA:XA + MIX] = dgate.astype(BF16)

    tile, halo = _tile_specs(s, ODD_IN)
    row = pl.BlockSpec((TS, D), lambda i: (i, 0))
    vec = _const((1, BW))

    def out(n):
        return pl.BlockSpec((TS, n), lambda i: (i, 0))

    return pl.pallas_call(
        body, grid=(s // TS,), name="odd_bwd1",
        out_shape=(jax.ShapeDtypeStruct((s, XA + MIX), BF16), jax.ShapeDtypeStruct((s, BW), BF16),
                   jax.ShapeDtypeStruct((s, BW), BF16), jax.ShapeDtypeStruct((s, D), BF16),
                   jax.ShapeDtypeStruct((s, MIX), BF16),
                   jax.ShapeDtypeStruct((1, D), F32), jax.ShapeDtypeStruct((1, BW), F32),
                   jax.ShapeDtypeStruct((BW, BW), F32), jax.ShapeDtypeStruct((8 * CONF, BW), F32),
                   jax.ShapeDtypeStruct((1, BW), F32), jax.ShapeDtypeStruct((1, BW), F32),
                   jax.ShapeDtypeStruct((1, BW), F32), jax.ShapeDtypeStruct((BW, BW), F32),
                   jax.ShapeDtypeStruct((1, BW), F32), jax.ShapeDtypeStruct((N_MEM, D), F32)),
        in_specs=[row, row, out(BW), tile, halo, _const((N_MEM, D)), _const((4, TS, HALO + TS)), _const((BW, BW)), vec,
                  _const((CONF, BW)), vec, vec, vec, _const((BW, BW)), vec, _resident((MIX, D)), _const((1, D))],
        out_specs=(out(XA + MIX), out(BW), out(BW), out(D), out(MIX),
                   _const((1, D)), vec, _const((BW, BW)), _const((8 * CONF, BW)), vec, vec, vec, _const((BW, BW)), vec,
                   _const((N_MEM, D))),
        scratch_shapes=[pltpu.VMEM((HALO + TS, BW), F32), pltpu.VMEM((7, SHIFT_ROWS, BW), F32),
                        pltpu.VMEM((TS, BW), F32)],
        compiler_params=_cp(("arbitrary",)),
    )(dx, o, cv, p, p, kv, _band_matrices(TS, False), wbd, cscale, dww, dwb, ln_g, ln_b, pww, pwb, wout, post_g)


def _halo_next(nblk_per_tile, nblk):
    return lambda i: (jnp.minimum((i + 1) * nblk_per_tile, nblk - 1), 0)


def _pre_norm_bwd(dh, x, pre_g, dres, dpre_ref):
    r = lax.rsqrt(jnp.mean(x * x, axis=-1, keepdims=True) + EPS)
    xh = x * r
    dpre_ref[...] += jnp.sum(dh * xh, axis=0, keepdims=True)
    dxh = dh * pre_g
    return dres + r * (dxh - xh * jnp.mean(dxh * xh, axis=-1, keepdims=True))


def _even_bwd2(dp, w_t, x, pre_g, dres, rider=None):
    s = x.shape[0]
    tm = min(512, s)

    def body(dp_ref, w_ref, x_ref, pg, dres_ref, dx_ref, dpre_ref):
        _acc_init(pl.program_id(0), (dpre_ref,))
        dh = _dot(dp_ref[...], w_ref[...])
        dx_ref[...] = _pre_norm_bwd(dh, x_ref[...], pg[...], dres_ref[...], dpre_ref)

    row = pl.BlockSpec((tm, D), lambda i: (i, 0))
    return _host_call(
        body, grid=(s // tm,), name="even_bwd2", rider=rider,
        out_shape=(jax.ShapeDtypeStruct((s, D), F32), jax.ShapeDtypeStruct((1, D), F32)),
        in_specs=[pl.BlockSpec((tm, EVEN_IN), lambda i: (i, 0)), _resident((EVEN_IN, D)), row, _const((1, D)), row],
        out_specs=(row, _const((1, D))),
        args=(dp, w_t, x, pre_g, dres))


def _odd_bwd2(dpc, tmpc, tmpd, p, dww, w_t, x, pre_g, dres):
    s = x.shape[0]
    nt = s // TS

    def body(dpc_ref, tc_ref, tch_ref, td_ref, tdh_ref, ga_ref, gb_ref, bands_ref, dww_ref, w_ref, x_ref, pg,
             dres_ref, dpb_ref, dx_ref, dpre_ref, dbuf, dsh):
        i = pl.program_id(0)
        _acc_init(i, (dpre_ref,))
        more = i < nt - 1
        e_bf = tc_ref[...]
        eh = tch_ref[...]
        ecat = jnp.concatenate([e_bf, jnp.where(more, eh, jnp.zeros_like(eh))], axis=0)
        dbuf[0:TS, :] = td_ref[...].astype(F32)
        dbuf[TS:TS + HALO, :] = jnp.where(more, tdh_ref[...].astype(F32), 0.0)
        sums = [_dot(bands_ref[w], ecat) for w in range(len(POOL_WINDOWS))]
        rows = _row_ids(i, TS) + 1
        cnt = _pool_select([jnp.minimum(rows, w).astype(F32) for w in POOL_WINDOWS])
        dzc = (_pool_select(sums) - e_bf.astype(F32) * cnt).astype(BF16)
        _shifted_copies(dbuf, dsh)
        dz = dww_ref[CONF - 1:CONF, :] * dbuf[pl.ds(0, TS), :]
        for sft in range(1, CONF):
            dz = dz + dww_ref[CONF - 1 - sft:CONF - sft, :] * _rows_at(dbuf, dsh, sft, TS)
        ga = ga_ref[...].astype(F32)
        sgb = _sigmoid(gb_ref[...].astype(F32))
        dga = (dz * sgb).astype(BF16)
        dgb = (dz * ga * sgb * (1.0 - sgb)).astype(BF16)
        dpb_ref[:, 0:BW] = dzc
        dpb_ref[:, BW:2 * BW] = dga
        dpb_ref[:, 2 * BW:3 * BW] = dgb
        dh = (_dot(dzc, w_ref[0:BW, :]) + _dot(dga, w_ref[BW:2 * BW, :]) + _dot(dgb, w_ref[2 * BW:3 * BW, :])
              + _dot(dpc_ref[...], w_ref[3 * BW:ODD_IN, :]))
        dx_ref[...] = _pre_norm_bwd(dh, x_ref[...], pg[...], dres_ref[...], dpre_ref)

    row = pl.BlockSpec((TS, D), lambda i: (i, 0))

    def tile(n, j=0):
        return pl.BlockSpec((TS, n), lambda i: (i, j))

    nxt = pl.BlockSpec((HALO, BW), _halo_next(TS // HALO, s // HALO))
    return pl.pallas_call(
        body, grid=(nt,), name="odd_bwd2",
        out_shape=(jax.ShapeDtypeStruct((s, 3 * BW), BF16), jax.ShapeDtypeStruct((s, D), F32),
                   jax.ShapeDtypeStruct((1, D), F32)),
        in_specs=[tile(XA + MIX), tile(BW), nxt, tile(BW), nxt, tile(BW, 1), tile(BW, 2), _const((4, TS, HALO + TS)),
                  _const((CONF, BW)), _resident((ODD_IN, D)), row, _const((1, D)), row],
        out_specs=(tile(3 * BW), row, _const((1, D))),
        scratch_shapes=[pltpu.VMEM((TS + HALO, BW), F32), pltpu.VMEM((7, SHIFT_ROWS, BW), F32)],
        compiler_params=_cp(("arbitrary",)),
    )(dpc, tmpc, tmpc, tmpd, tmpd, p, p, _band_matrices(TS, True), dww, w_t, x, pre_g, dres)


def _grad_tn(a, b, tm, out=None, rows=None, row0=0, name="grad_tn", rider=None):
    s, m = a.shape
    n = b.shape[1]
    ts = min(2048, s)
    rows = m if rows is None else rows
    assert m % tm == 0 and s % ts == 0
    ns = s // ts
    if row0 % tm == 0:
        out_spec = pl.BlockSpec((tm, n), lambda i, k: (row0 // tm + i, 0))
    else:
        align = 16
        assert row0 % align == 0 and tm % align == 0
        out_spec = pl.BlockSpec((pl.Element(tm), pl.Element(n)),
                                lambda i, k: (pl.multiple_of(row0 + i * tm, align), 0))

    def body(*refs):
        a_ref, b_ref = refs[0], refs[1]
        o_ref, acc = refs[-2], refs[-1]
        k = pl.program_id(1)

        @pl.when(k == 0)
        def _():
            acc[...] = jnp.zeros_like(acc)

        acc[...] += _dot_tn(a_ref[...], b_ref[...])

        @pl.when(k == ns - 1)
        def _():
            o_ref[...] = acc[...].astype(BF16)

    in_specs = [pl.BlockSpec((ts, tm), lambda i, k: (k, i)), pl.BlockSpec((ts, n), lambda i, k: (k, 0))]
    args = [a, b]
    aliases = {}
    if out is not None:
        in_specs.append(pl.BlockSpec(memory_space=pltpu.HBM))
        args.append(out)
        aliases = {2: 0}
    (res,), got = _host_call(
        body, grid=(m // tm, ns), name=name, rider=rider, aliases=aliases,
        out_shape=(jax.ShapeDtypeStruct((rows, n), BF16),), in_specs=in_specs, out_specs=(out_spec,),
        scratch_shapes=[pltpu.VMEM((tm, n), F32)], args=args)
    return res if rider is None else (res, got)


def _place():
    x, y, c = lax.axis_index("x"), lax.axis_index("y"), lax.axis_index("c")
    chips = [(1 - x, y), (x, 1 - y), (1 - x, 1 - y)]
    return x, y, c, chips


def _hbm_specs(n):
    return [pl.BlockSpec(memory_space=pltpu.HBM)] * n


def _row_tile(r):
    for cand in (512, 400, 304, 256, 192, 128, 96, 16):
        if r % cand == 0:
            return cand
    raise ValueError(r)


def _place_shard(shard, place, dtype, name, after=None):
    r, cc = shard.shape
    tr = _row_tile(r)
    nt = r // tr

    def body(place_ref, s_ref, *rest):
        rest[-1][...] = s_ref[...].astype(dtype)

    in_specs = [pl.BlockSpec((tr, cc), lambda i, pr: (i, 0))]
    args = [shard]
    if after is not None:
        in_specs.append(pl.BlockSpec(after.shape, lambda i, pr: (0, 0)))
        args.append(after)
    return pl.pallas_call(
        body, name=name, out_shape=jax.ShapeDtypeStruct((N_CHIPS * r, cc), dtype),
        grid_spec=pltpu.PrefetchScalarGridSpec(
            num_scalar_prefetch=1, grid=(nt,), in_specs=in_specs,
            out_specs=pl.BlockSpec((tr, cc), lambda i, pr: (pr[1] * nt + i, 0))),
        compiler_params=_cp(("arbitrary",)),
    )(place, *args)


class _GatherRider:
    has_mid = True

    def __init__(self, fulls):
        n = len(fulls)
        self.inputs = list(fulls)
        self.out_shapes = [jax.ShapeDtypeStruct(a.shape, a.dtype) for a in fulls]
        self.aliases = {a: a for a in range(n)}
        self.sems = [pltpu.SemaphoreType.DMA((6 * n,)), pltpu.SemaphoreType.DMA((6 * n,))]
        self.block_rows = [a.shape[0] // N_CHIPS for a in fulls]

    def _ctx(self, outs, sems):
        send_sems, recv_sems = sems
        x, y, c, chips = _place()

        def rows(a, k, half):
            r = self.block_rows[a]
            return outs[a].at[pl.ds(k * r + half * (r // 2), r // 2)]

        def copy(a, j, blk, to):
            return pltpu.make_async_remote_copy(src_ref=blk, dst_ref=blk, send_sem=send_sems.at[a * 6 + j],
                                                recv_sem=recv_sems.at[a * 6 + j], device_id=to, device_id_type=MESH)

        return x, y, c, chips, rows, copy

    def start(self, ins, outs, sems, peers=(0, 1, 2)):
        x, y, c, chips, rows, copy = self._ctx(outs, sems)
        for j in peers:
            for a in range(len(outs)):
                copy(a, j, rows(a, 2 * x + y, c), (*chips[j], c)).start()

    def mid(self, ins, outs, sems, peers=(0, 1, 2)):
        x, y, c, chips, rows, copy = self._ctx(outs, sems)
        for j in peers:
            px, py = chips[j]
            for a in range(len(outs)):
                copy(a, j, rows(a, 2 * px + py, c), (px, py, c)).wait_recv()
                copy(a, 3 + j, rows(a, 2 * px + py, c), (x, y, 1 - c)).start()

    def wait_forwarded(self, outs, sems, peers=(0, 1, 2)):
        x, y, c, chips, rows, copy = self._ctx(outs, sems)
        for j in peers:
            px, py = chips[j]
            for a in range(len(outs)):
                copy(a, 3 + j, rows(a, 2 * px + py, 1 - c), (x, y, 1 - c)).wait_recv()

    def wait_sends(self, outs, sems):
        x, y, c, chips, rows, copy = self._ctx(outs, sems)
        for j, (px, py) in enumerate(chips):
            for a in range(len(outs)):
                copy(a, j, rows(a, 2 * x + y, c), (px, py, c)).wait_send()
                copy(a, 3 + j, rows(a, 2 * px + py, c), (x, y, 1 - c)).wait_send()

    def end(self, ins, outs, sems):
        self.wait_forwarded(outs, sems)
        self.wait_sends(outs, sems)


def _swap_halves(grads, name, share=()):
    n, k = len(grads), len(share)
    m = n + k

    def body(*refs):
        ins, outs = refs[:m], refs[m:2 * m]
        send_sems, recv_sems = refs[2 * m:]
        x, y, c, _ = _place()
        sibling = (x, y, 1 - c)
        cps, waits = [], []
        for a in range(m):
            if a < n:
                cp = pltpu.make_async_remote_copy(src_ref=ins[a].at[:, 1 - c], dst_ref=outs[a],
                                                  send_sem=send_sems.at[a], recv_sem=recv_sems.at[a],
                                                  device_id=sibling, device_id_type=MESH)
                waits.append(cp)
            else:
                cp = pltpu.make_async_remote_copy(src_ref=outs[a].at[c], dst_ref=outs[a].at[c],
                                                  send_sem=send_sems.at[a], recv_sem=recv_sems.at[a],
                                                  device_id=sibling, device_id_type=MESH)
                waits.append(pltpu.make_async_remote_copy(
                    src_ref=outs[a].at[1 - c], dst_ref=outs[a].at[1 - c], send_sem=send_sems.at[a],
                    recv_sem=recv_sems.at[a], device_id=sibling, device_id_type=MESH))
            cp.start()
            cps.append(cp)
        for cp in waits:
            cp.wait_recv()
        for cp in cps:
            cp.wait_send()

    outs = tuple(jax.ShapeDtypeStruct((g.shape[0],) + g.shape[2:], g.dtype) for g in grads)
    outs += tuple(jax.ShapeDtypeStruct(g.shape, g.dtype) for g in share)
    res = pl.pallas_call(
        body, name=name, out_shape=outs, in_specs=_hbm_specs(m), out_specs=tuple(_hbm_specs(m)),
        input_output_aliases={n + a: n + a for a in range(k)},
        scratch_shapes=[pltpu.SemaphoreType.DMA((m,)), pltpu.SemaphoreType.DMA((m,))],
    )(*grads, *share)
    return tuple(res[:n]), tuple(res[n:])


def _pair_sum(g, recv, place, name):
    _, _, h, cc = g.shape
    th = h

    def body(c_ref, g_ref, r_ref, o_ref):
        o_ref[...] = (g_ref[...].astype(F32) + r_ref[...].astype(F32)).astype(o_ref.dtype)

    return pl.pallas_call(
        body, name=name, out_shape=jax.ShapeDtypeStruct(recv.shape, recv.dtype),
        grid_spec=pltpu.PrefetchScalarGridSpec(
            num_scalar_prefetch=1, grid=(N_CHIPS, h // th),
            in_specs=[pl.BlockSpec((None, None, th, cc), lambda k, r, c_ref: (k, c_ref[0], r, 0)),
                      pl.BlockSpec((None, th, cc), lambda k, r, c_ref: (k, r, 0))],
            out_specs=pl.BlockSpec((None, th, cc), lambda k, r, c_ref: (k, r, 0))),
        compiler_params=_cp(("arbitrary", "arbitrary")),
    )(place, g, recv)


def _finish_reduce(pack, halves):
    rows, cc = pack.shape
    hs = rows // 2
    n = len(halves)

    def body(*refs):
        pack_ref = refs[0]
        out_ref = refs[1 + n]
        big = refs[2 + n:2 + 2 * n]
        sib_ref, parts_ref, send_sems, recv_sems, big_send, big_recv = refs[2 + 2 * n:]
        x, y, c, chips = _place()
        me_k = 2 * x + y
        sibling = (x, y, 1 - c)
        mine = pl.ds(pl.multiple_of(c * hs, hs), hs)
        theirs = pl.ds(pl.multiple_of((1 - c) * hs, hs), hs)
        shared = [pltpu.make_async_remote_copy(src_ref=big[a].at[c], dst_ref=big[a].at[c], send_sem=big_send.at[a],
                                               recv_sem=big_recv.at[a], device_id=sibling, device_id_type=MESH)
                  for a in range(n)]
        first = pltpu.make_async_remote_copy(src_ref=pack_ref, dst_ref=sib_ref, send_sem=send_sems.at[0],
                                             recv_sem=recv_sems.at[0], device_id=sibling, device_id_type=MESH)
        first.start()
        first.wait()
        parts_ref[me_k] = pack_ref[mine, :] + sib_ref[mine, :]
        cps = [pltpu.make_async_remote_copy(src_ref=parts_ref.at[me_k], dst_ref=parts_ref.at[me_k],
                                            send_sem=send_sems.at[1 + j], recv_sem=recv_sems.at[1 + j],
                                            device_id=(px, py, c), device_id_type=MESH)
               for j, (px, py) in enumerate(chips)]
        for cp in cps:
            cp.start()
        for cp in shared:
            cp.start()
        for j, (px, py) in enumerate(chips):
            pltpu.make_async_remote_copy(src_ref=parts_ref.at[2 * px + py], dst_ref=parts_ref.at[2 * px + py],
                                         send_sem=send_sems.at[1 + j], recv_sem=recv_sems.at[1 + j],
                                         device_id=(px, py, c), device_id_type=MESH).wait_recv()
        for cp in cps:
            cp.wait_send()
        out_ref[mine, :] = ((parts_ref[0] + parts_ref[1]) + parts_ref[2]) + parts_ref[3]
        last = pltpu.make_async_remote_copy(src_ref=out_ref.at[mine], dst_ref=out_ref.at[mine],
                                            send_sem=send_sems.at[4], recv_sem=recv_sems.at[4], device_id=sibling,
                                            device_id_type=MESH)
        last.start()
        pltpu.make_async_remote_copy(src_ref=out_ref.at[theirs], dst_ref=out_ref.at[theirs],
                                     send_sem=send_sems.at[4], recv_sem=recv_sems.at[4], device_id=sibling,
                                     device_id_type=MESH).wait_recv()
        last.wait_send()
        for a in range(n):
            pltpu.make_async_remote_copy(src_ref=big[a].at[1 - c], dst_ref=big[a].at[1 - c], send_sem=big_send.at[a],
                                         recv_sem=big_recv.at[a], device_id=sibling,
                                         device_id_type=MESH).wait_recv()
        for cp in shared:
            cp.wait_send()

    vmem = pl.BlockSpec(memory_space=pltpu.VMEM)
    res = pl.pallas_call(
        body, name="finish_reduce",
        out_shape=(jax.ShapeDtypeStruct(pack.shape, pack.dtype),)
        + tuple(jax.ShapeDtypeStruct(g.shape, g.dtype) for g in halves),
        in_specs=[vmem] + _hbm_specs(n), out_specs=(vmem,) + tuple(_hbm_specs(n)),
        input_output_aliases={1 + a: 1 + a for a in range(n)},
        scratch_shapes=[pltpu.VMEM((rows, cc), F32), pltpu.VMEM((N_CHIPS, hs, cc), F32),
                        pltpu.SemaphoreType.DMA((5,)), pltpu.SemaphoreType.DMA((5,)),
                        pltpu.SemaphoreType.DMA((n,)), pltpu.SemaphoreType.DMA((n,))],
        compiler_params=_cp(),
    )(pack, *halves)
    return res[0], tuple(res[1:])


class _ExchangeRider:
    has_mid = False

    def __init__(self, sums):
        self.inputs = list(sums)
        self.out_shapes = [jax.ShapeDtypeStruct((3,) + g.shape[1:], g.dtype) for g in sums]
        m = len(self.inputs)
        self.aliases = {}
        self.sems = [pltpu.SemaphoreType.DMA((3 * m,)), pltpu.SemaphoreType.DMA((3 * m,))]

    def _copies(self, ins, outs, sems):
        send_sems, recv_sems = sems
        _, _, c, chips = _place()
        return [pltpu.make_async_remote_copy(
            src_ref=ins[a].at[2 * px + py], dst_ref=outs[a].at[j], send_sem=send_sems.at[a * 3 + j],
            recv_sem=recv_sems.at[a * 3 + j], device_id=(px, py, c), device_id_type=MESH)
            for j, (px, py) in enumerate(chips) for a in range(len(ins))]

    def start(self, ins, outs, sems):
        for cp in self._copies(ins, outs, sems):
            cp.start()

    def end(self, ins, outs, sems):
        cps = self._copies(ins, outs, sems)
        for cp in cps:
            cp.wait_recv()
        for cp in cps:
            cp.wait_send()


class _ShareRider:
    has_mid = False

    def __init__(self, halves):
        n = len(halves)
        self.inputs = list(halves)
        self.out_shapes = [jax.ShapeDtypeStruct(g.shape, g.dtype) for g in halves]
        self.aliases = {a: a for a in range(n)}
        self.sems = [pltpu.SemaphoreType.DMA((n,)), pltpu.SemaphoreType.DMA((n,))]

    def _copies(self, outs, sems, half):
        send_sems, recv_sems = sems
        x, y, c, _ = _place()
        h = c if half == "mine" else 1 - c
        return [pltpu.make_async_remote_copy(src_ref=outs[a].at[h], dst_ref=outs[a].at[h], send_sem=send_sems.at[a],
                                             recv_sem=recv_sems.at[a], device_id=(x, y, 1 - c), device_id_type=MESH)
                for a in range(len(outs))]

    def start(self, ins, outs, sems):
        for cp in self._copies(outs, sems, "mine"):
            cp.start()

    def end(self, ins, outs, sems):
        for cp in self._copies(outs, sems, "theirs"):
            cp.wait_recv()
        for cp in self._copies(outs, sems, "mine"):
            cp.wait_send()


class _Riders:
    def __init__(self, riders):
        self.riders = list(riders)
        self.inputs = [a for r in self.riders for a in r.inputs]
        self.out_shapes = [s for r in self.riders for s in r.out_shapes]
        self.sems = [s for r in self.riders for s in r.sems]
        self.has_mid = any(r.has_mid for r in self.riders)
        self.aliases = {}
        i0 = o0 = 0
        for r in self.riders:
            self.aliases.update({i0 + j: o0 + k for j, k in r.aliases.items()})
            i0 += len(r.inputs)
            o0 += len(r.out_shapes)

    def _each(self, ins, outs, sems):
        i0 = o0 = s0 = 0
        for r in self.riders:
            yield (r, ins[i0:i0 + len(r.inputs)], outs[o0:o0 + len(r.out_shapes)], sems[s0:s0 + len(r.sems)])
            i0, o0, s0 = i0 + len(r.inputs), o0 + len(r.out_shapes), s0 + len(r.sems)

    def start(self, ins, outs, sems):
        for r, i, o, s in self._each(ins, outs, sems):
            r.start(i, o, s)

    def mid(self, ins, outs, sems):
        for r, i, o, s in self._each(ins, outs, sems):
            if r.has_mid:
                r.mid(i, o, s)

    def end(self, ins, outs, sems):
        for r, i, o, s in self._each(ins, outs, sems):
            r.end(i, o, s)

    def split(self, outs):
        res, o0 = [], 0
        for r in self.riders:
            res.append(tuple(outs[o0:o0 + len(r.out_shapes)]))
            o0 += len(r.out_shapes)
        return res


def _chip_sum(own, parts, place, name):
    npart, h, cc = parts.shape
    th = _row_tile(h)

    def body(place_ref, own_ref, p_ref, o_ref):
        acc = own_ref[...].astype(F32) + p_ref[0].astype(F32)
        for k in range(1, npart):
            acc = acc + p_ref[k].astype(F32)
        o_ref[...] = acc

    return pl.pallas_call(
        body, name=name, out_shape=jax.ShapeDtypeStruct((2, h, cc), F32),
        grid_spec=pltpu.PrefetchScalarGridSpec(
            num_scalar_prefetch=1, grid=(h // th,),
            in_specs=[pl.BlockSpec((None, th, cc), lambda r, pr: (pr[1], r, 0)),
                      pl.BlockSpec((npart, th, cc), lambda r, pr: (0, r, 0))],
            out_specs=pl.BlockSpec((None, th, cc), lambda r, pr: (pr[0], r, 0))),
        compiler_params=_cp(("arbitrary",)),
    )(place, own, parts)


def _adamw_math(w, g, m, v):
    m = ADAM_B1 * m + (1.0 - ADAM_B1) * g
    v = ADAM_B2 * v + (1.0 - ADAM_B2) * (g * g)
    m_hat = m / (1.0 - ADAM_B1 ** ADAM_STEP)
    v_hat = v / (1.0 - ADAM_B2 ** ADAM_STEP)
    delta = -ADAM_LR * (m_hat / (jnp.sqrt(v_hat) + ADAM_EPS) + ADAM_WD * w)
    return delta, m, v


def _adamw_big(w, g, m, v, name):
    r, cc = w.shape
    tr = min(_row_tile(r), 256) if r % 256 == 0 else _row_tile(r)

    def body(w_ref, g_ref, m_ref, v_ref, go_ref, d_ref, mo_ref, vo_ref):
        g = g_ref[...]
        d, mm, vv = _adamw_math(w_ref[...], g, m_ref[...], v_ref[...])
        go_ref[...] = g
        d_ref[...] = d
        mo_ref[...] = mm
        vo_ref[...] = vv

    blk = pl.BlockSpec((tr, cc), lambda i: (i, 0))
    sd = jax.ShapeDtypeStruct((r, cc), F32)
    return pl.pallas_call(body, grid=(r // tr,), name=name, out_shape=(sd, sd, sd, sd), in_specs=[blk] * 4,
                          out_specs=(blk, blk, blk, blk), compiler_params=_cp(("arbitrary",)))(w, g, m, v)


SC_TILES = 32


def _adamw_sparsecore(ws, gs, ms, vs, name):
    n = len(ws)
    rows_per = 8
    widths = sorted({a.shape[1] for a in ws})
    assert all(a.shape[0] % rows_per == 0 and a.shape[1] % 16 == 0 for a in ws)

    def body(*refs):
        ins, outs, bufs = refs[:4 * n], refs[4 * n:8 * n], refs[8 * n:]
        tile = lax.axis_index("sc_tile") * 2 + lax.axis_index("sc_core")
        for a in range(n):
            w_hbm, g_hbm, m_hbm, v_hbm = ins[4 * a:4 * a + 4]
            go_hbm, d_hbm, mo_hbm, vo_hbm = outs[4 * a:4 * a + 4]
            r, cc = ws[a].shape
            k = widths.index(cc)
            wb, gb, mb, vb, db = bufs[5 * k:5 * k + 5]
            groups = r // rows_per

            @pl.loop(0, -(-groups // SC_TILES))
            def _(q):
                grp = tile + q * SC_TILES

                @pl.when(grp < groups)
                def _():
                    rows = pl.ds(pl.multiple_of(grp * rows_per, rows_per), rows_per)
                    pltpu.sync_copy(w_hbm.at[rows], wb)
                    pltpu.sync_copy(g_hbm.at[rows], gb)
                    pltpu.sync_copy(m_hbm.at[rows], mb)
                    pltpu.sync_copy(v_hbm.at[rows], vb)

                    @pl.loop(0, rows_per)
                    def _(i):
                        @pl.loop(0, cc, step=16)
                        def _(j):
                            at = (i, pl.ds(j, 16))
                            d, mm, vv = _adamw_math(wb[at], gb[at], mb[at], vb[at])
                            db[at] = d
                            mb[at] = mm
                            vb[at] = vv

                    pltpu.sync_copy(gb, go_hbm.at[rows])
                    pltpu.sync_copy(db, d_hbm.at[rows])
                    pltpu.sync_copy(mb, mo_hbm.at[rows])
                    pltpu.sync_copy(vb, vo_hbm.at[rows])

    args, out_type = [], []
    for a in range(n):
        args += [ws[a], gs[a], ms[a], vs[a]]
        out_type += [jax.ShapeDtypeStruct(ws[a].shape, F32)] * 4
    res = pl.kernel(
        body, name=name, out_type=tuple(out_type),
        mesh=plsc.VectorSubcoreMesh(core_axis_name="sc_core", subcore_axis_name="sc_tile"),
        scratch_types=[pltpu.VMEM((rows_per, cc), F32) for cc in widths for _ in range(5)],
    )(*args)
    return [tuple(res[4 * a:4 * a + 4]) for a in range(n)]


def _adamw_small(ws, gs, ms, vs):
    n = len(ws)

    def body(*refs):
        for a in range(n):
            w_ref, g_ref, m_ref, v_ref = refs[4 * a:4 * a + 4]
            d_ref, mo_ref, vo_ref = refs[4 * n + 3 * a:4 * n + 3 * a + 3]
            d, mm, vv = _adamw_math(w_ref[...], g_ref[...], m_ref[...], v_ref[...])
            d_ref[...] = d
            mo_ref[...] = mm
            vo_ref[...] = vv

    args, outs = [], []
    for a in range(n):
        args += [ws[a], gs[a], ms[a], vs[a]]
        outs += [jax.ShapeDtypeStruct(ws[a].shape, F32)] * 3
    res = pl.pallas_call(body, name="adamw_small", out_shape=tuple(outs), compiler_params=_cp())(*args)
    return [res[3 * a:3 * a + 3] for a in range(n)]


def _flat_pack(arrs, rows):
    flat = jnp.concatenate([a.reshape(-1) for a in arrs])
    return jnp.pad(flat, (0, rows * D - flat.shape[0])).reshape(rows, D)


def _flat_unpack(flat, shapes):
    out, off = [], 0
    for shp in shapes:
        size = 1
        for d_ in shp:
            size *= d_
        out.append(flat[off:off + size].reshape(shp))
        off += size
    return out


SMALL_EVEN = ("even_pre_g", "even_a_ln_g", "even_a_ln_b", "even_a_ws", "even_a_bs", "even_b_conv", "even_mem_g",
              "even_post_g")
SMALL_ODD = ("odd_pre_g", "odd_c_wgrp", "odd_c_scale", "odd_d_dw_w", "odd_d_dw_b", "odd_d_ln_g", "odd_d_ln_b",
             "odd_d_pw_b", "odd_mem_g", "odd_post_g")
BIG = ("even_w_in", "even_w_kv", "even_w_out", "odd_w_in", "odd_d_pw_w", "odd_w_kv", "odd_w_out")
WEIGHTS = ("even_pre_g", "even_w_in", "even_a_ln_g", "even_a_ln_b", "even_a_ws", "even_a_bs", "even_b_conv",
           "even_mem_g", "even_w_kv", "even_w_out", "even_post_g", "odd_pre_g", "odd_w_in", "odd_c_wgrp",
           "odd_c_scale", "odd_d_dw_w", "odd_d_dw_b", "odd_d_ln_g", "odd_d_ln_b", "odd_d_pw_w", "odd_d_pw_b",
           "odd_mem_g", "odd_w_kv", "odd_w_out", "odd_post_g")
PACKED = (("even_b_conv", (3, 192)), ("odd_pre_g", (1, 256)), ("odd_c_scale", (1, 192)), ("odd_d_dw_w", (31, 192)),
          ("odd_d_dw_b", (1, 192)), ("odd_d_ln_g", (1, 192)), ("odd_d_ln_b", (1, 192)), ("odd_d_pw_b", (1, 192)),
          ("odd_mem_g", (1, 256)), ("odd_post_g", (1, 256)))
PACK_ROWS = 16
SMALL_ROWS = 256


def _four(g):
    return g.reshape(N_CHIPS, 2, g.shape[0] // (2 * N_CHIPS), g.shape[1])


def _step(x, mem, target, w, mom, var, place):
    wt = {}
    pack = _flat_pack([w[n][0] for n, _ in PACKED], PACK_ROWS)
    shards = {"even_w_in_t": w["even_w_in"][0].T, "odd_w_in_t": w["odd_w_in"][0].T, "even_w_kv": w["even_w_kv"][0],
              "odd_w_kv": w["odd_w_kv"][0], "even_w_out": w["even_w_out"][0], "odd_w_out": w["odd_w_out"][0],
              "odd_d_pw_w": w["odd_d_pw_w"][0]}
    placed = {n: _place_shard(shards[n], place, BF16, "place_" + n) for n in ("even_w_in_t", "even_w_kv", "even_w_out")}
    placed["pack"] = _place_shard(pack, place, F32, "place_pack")

    order, group = _stream_tables(place[0], place[1], EVEN_IN)
    p_e, h_e, (wt["even_w_in_t"], packs), (wt["even_w_kv"], wt["even_w_out"]) = _in_fwd_streamed(
        x, w["even_pre_g"], [placed["even_w_in_t"], placed["pack"]], [placed["even_w_kv"], placed["even_w_out"]],
        order, group, "even_in_streamed")
    for n in ("odd_w_in_t", "odd_w_kv", "odd_w_out", "odd_d_pw_w"):
        placed[n] = _place_shard(shards[n], place, BF16, "place_" + n, after=p_e[0:16, 0:128])
    packs = packs.reshape(N_CHIPS, PACK_ROWS * D)
    per_chip = [_flat_unpack(packs[k], [shp for _, shp in PACKED]) for k in range(N_CHIPS)]
    for a, (name, _) in enumerate(PACKED):
        wt[name] = jnp.concatenate([per_chip[k][a] for k in range(N_CHIPS)], axis=-1)
    for name in ("even_pre_g", "even_a_ln_g", "even_a_ln_b", "even_mem_g", "even_post_g"):
        wt[name] = w[name]

    tril = jnp.tril(jnp.ones((CH, CH), dtype=bool))
    wcat = jnp.where(tril[None], w["even_a_ws"][0], 0.0).transpose(1, 0, 2).reshape(CH, 4 * CH).astype(BF16)
    bsg = jnp.repeat(w["even_a_bs"][0].T, BW // 4, axis=1)
    hsel = (jnp.arange(BW)[:, None] // (BW // 4) == jnp.arange(128)[None, :]).astype(BF16)
    g4 = BW // 4
    eye = jnp.eye(4, dtype=F32)
    wbd = (w["odd_c_wgrp"][0][:, :, None, :] * eye[:, None, :, None]).reshape(BW, BW).astype(BF16)

    kv_e = _kv_fwd(mem, wt["even_mem_g"], wt["even_w_kv"], "even_kv")
    (x1, o_e, y_e), (wt["odd_w_in_t"],) = _even_fwd(
        x, p_e, kv_e, wt["even_a_ln_g"], wt["even_a_ln_b"], wcat, bsg, wt["even_b_conv"], wt["even_w_out"],
        wt["even_post_g"], rider=_GatherRider([placed["odd_w_in_t"]]))
    names = ("odd_w_out", "odd_d_pw_w", "odd_w_kv")
    (p_o, h_o), got = _in_fwd(x1, wt["odd_pre_g"], wt["odd_w_in_t"], "odd_in",
                              rider=_GatherRider([placed[n] for n in names]))
    wt.update(zip(names, got))
    kv_o = _kv_fwd(mem, wt["odd_mem_g"], wt["odd_w_kv"], "odd_kv")
    dx2, o_o, cv_o, loss = _odd_fwd(x1, p_o, kv_o, wbd, wt["odd_c_scale"], wt["odd_d_dw_w"], wt["odd_d_dw_b"],
                                    wt["odd_d_ln_g"], wt["odd_d_ln_b"], wt["odd_d_pw_w"], wt["odd_d_pw_b"],
                                    wt["odd_w_out"], wt["odd_post_g"], target)
    (dpc_o, tmpc, tmpd, do_o, y_o, g_post_o, g_cs, g_wbd, g_dww, g_dwb, g_lng_o, g_lnb_o, g_pww, g_pwb,
     dkv_o) = _odd_bwd1(dx2, o_o, cv_o, p_o, kv_o, wbd, wt["odd_c_scale"], wt["odd_d_dw_w"], wt["odd_d_dw_b"],
                        wt["odd_d_ln_g"], wt["odd_d_ln_b"], wt["odd_d_pw_w"], wt["odd_d_pw_b"], wt["odd_w_out"],
                        wt["odd_post_g"])
    dpb_o, dx1, g_pre_o = _odd_bwd2(dpc_o, tmpc, tmpd, p_o, wt["odd_d_dw_w"], wt["odd_w_in_t"], x1,
                                    wt["odd_pre_g"], dx2)
    g_win_o = _grad_tn(dpb_o, h_o, 768, rows=ODD_IN, name="odd_gw_in_b")
    g_win_o = _grad_tn(dpc_o, h_o, 1280, out=g_win_o, rows=ODD_IN, row0=3 * BW, name="odd_gw_in_c")
    g_wout_o = _grad_tn(y_o, do_o, 1024, name="odd_gw_out")
    g_wkv_o, g_memg_o = _kv_bwd(mem, wt["odd_mem_g"], wt["odd_w_kv"], dkv_o, "odd_kv_bwd")
    big_o = [_four(g) for g in (g_win_o, g_pww.astype(BF16), g_wkv_o, g_wout_o)]
    recv_o, _ = _swap_halves(big_o, "swap_halves_odd")
    sums_o = [_pair_sum(big_o[a], recv_o[a], place, "pair_sum_odd_%d" % a) for a in range(len(big_o))]
    (dp_e, do_e, g_post_e, g_lng_e, g_lnb_e, g_wcat, g_bs, g_bconv,
     dkv_e), parts_o = _even_bwd1(dx1, o_e, p_e, kv_e, wt["even_a_ln_g"], wt["even_a_ln_b"], wcat, bsg, hsel,
                                  wt["even_b_conv"], wt["even_w_out"], wt["even_post_g"],
                                  rider=_ExchangeRider(sums_o))
    halves_o = [_chip_sum(sums_o[a], parts_o[a], place, "chip_sum_odd_%d" % a) for a in range(len(big_o))]
    g_wout_e = _grad_tn(y_e, do_e, 1024, name="even_gw_out")
    g_wkv_e, g_memg_e = _kv_bwd(mem, wt["even_mem_g"], wt["even_w_kv"], dkv_e, "even_kv_bwd")
    big_x = [_four(g) for g in (g_wkv_e, g_wout_e)]
    recv_x, _ = _swap_halves(big_x, "swap_halves_kv_out")
    sums_x = [_pair_sum(big_x[a], recv_x[a], place, "pair_sum_kv_out_%d" % a) for a in range(len(big_x))]
    riders = _Riders([_ExchangeRider(sums_x), _ShareRider(halves_o)])
    g_win_e, got = _grad_tn(dp_e, h_e, 1280, name="even_gw_in", rider=riders)
    parts_x, full_o = riders.split(got)

    def sparsecore_adamw(names, fulls, name):
        as_kept = [(lambda t: t.T) if n.endswith("w_in") else (lambda t: t) for n in names]
        res = _adamw_sparsecore([f(w[n][0]) for f, n in zip(as_kept, names)],
                                [g_.reshape(g_.shape[1] * 2, g_.shape[2]) for g_ in fulls],
                                [f(mom[n][0]) for f, n in zip(as_kept, names)],
                                [f(var[n][0]) for f, n in zip(as_kept, names)], name)
        return {n: tuple(f(t) for t in r_) for f, n, r_ in zip(as_kept, names, res)}

    upd_sc = sparsecore_adamw(("odd_w_in", "odd_d_pw_w", "odd_w_kv", "odd_w_out"), full_o, "adamw_odd_sparsecore")
    halves_x = [_chip_sum(sums_x[a], parts_x[a], place, "chip_sum_kv_out_%d" % a) for a in range(len(big_x))]
    big_e = [_four(g_win_e)]
    recv_e, _ = _swap_halves(big_e, "swap_halves_even")
    sums_e = [_pair_sum(big_e[0], recv_e[0], place, "pair_sum_even_w_in")]
    riders = _Riders([_ExchangeRider(sums_e), _ShareRider(halves_x)])
    (dx0, g_pre_e), got = _even_bwd2(dp_e, wt["even_w_in_t"], x, wt["even_pre_g"], dx1, rider=riders)
    parts_e, full_x = riders.split(got)
    upd_sc.update(sparsecore_adamw(("even_w_kv", "even_w_out"), full_x, "adamw_kv_out_sparsecore"))
    halves_e = [_chip_sum(sums_e[0], parts_e[0], place, "chip_sum_even_w_in")]

    g_aws = jnp.where(tril[None], g_wcat.reshape(CH, 4, CH).transpose(1, 0, 2), 0.0)
    g_wgrp = jnp.stack([lax.dynamic_slice(g_wbd, (g * g4, g * g4), (g4, g4)) for g in range(4)])
    small = {
        "even_pre_g": g_pre_e, "even_a_ln_g": g_lng_e, "even_a_ln_b": g_lnb_e, "even_a_ws": g_aws,
        "even_a_bs": g_bs[:, 0:4].T, "even_b_conv": g_bconv[0:3], "even_mem_g": g_memg_e, "even_post_g": g_post_e,
        "odd_pre_g": g_pre_o, "odd_c_wgrp": g_wgrp, "odd_c_scale": g_cs, "odd_d_dw_w": g_dww.reshape(CONF, 8, BW).sum(axis=1),
        "odd_d_dw_b": g_dwb, "odd_d_ln_g": g_lng_o, "odd_d_ln_b": g_lnb_o, "odd_d_pw_b": g_pwb,
        "odd_mem_g": g_memg_o, "odd_post_g": g_post_o,
    }
    small_names = SMALL_EVEN + SMALL_ODD
    small_pack = _flat_pack([small[n] for n in small_names] + [loss[0, 0].reshape(1)], SMALL_ROWS)
    small_total, full = _finish_reduce(small_pack, halves_e)
    gbig = {"even_w_in": full[0].reshape(full[0].shape[1] * 2, full[0].shape[2])}
    return dx0, gbig, upd_sc, small_total.reshape(-1), [small[n].shape for n in small_names]


def kernel(x, mem, even_pre_g, even_w_in, even_a_ln_g, even_a_ln_b, even_a_ws, even_a_bs, even_b_conv, even_mem_g, even_w_kv, even_w_out, even_post_g, odd_pre_g, odd_w_in, odd_c_wgrp, odd_c_scale, odd_d_dw_w, odd_d_dw_b, odd_d_ln_g, odd_d_ln_b, odd_d_pw_w, odd_d_pw_b, odd_mem_g, odd_w_kv, odd_w_out, odd_post_g, loss_target, m_even_pre_g, m_even_w_in, m_even_a_ln_g, m_even_a_ln_b, m_even_a_ws, m_even_a_bs, m_even_b_conv, m_even_mem_g, m_even_w_kv, m_even_w_out, m_even_post_g, m_odd_pre_g, m_odd_w_in, m_odd_c_wgrp, m_odd_c_scale, m_odd_d_dw_w, m_odd_d_dw_b, m_odd_d_ln_g, m_odd_d_ln_b, m_odd_d_pw_w, m_odd_d_pw_b, m_odd_mem_g, m_odd_w_kv, m_odd_w_out, m_odd_post_g, v_even_pre_g, v_even_w_in, v_even_a_ln_g, v_even_a_ln_b, v_even_a_ws, v_even_a_bs, v_even_b_conv, v_even_mem_g, v_even_w_kv, v_even_w_out, v_even_post_g, v_odd_pre_g, v_odd_w_in, v_odd_c_wgrp, v_odd_c_scale, v_odd_d_dw_w, v_odd_d_dw_b, v_odd_d_ln_g, v_odd_d_ln_b, v_odd_d_pw_w, v_odd_d_pw_b, v_odd_mem_g, v_odd_w_kv, v_odd_w_out, v_odd_post_g):
    given = dict(locals())
    w = {n: given[n] for n in WEIGHTS}
    mom = {n: given["m_" + n] for n in WEIGHTS}
    var = {n: given["v_" + n] for n in WEIGHTS}

    x_, y_, c_ = lax.axis_index("x"), lax.axis_index("y"), lax.axis_index("c")
    chip = 2 * x_ + y_
    place = jnp.stack([c_, chip]).astype(jnp.int32)
    grad_x, gbig, upd_odd, gsmall_flat, small_shapes = _step(x[0], mem[0], loss_target[0], w, mom, var, place)

    names = SMALL_EVEN + SMALL_ODD
    grads = {}
    unpacked = _flat_unpack(gsmall_flat, small_shapes + [(1,)])
    loss = unpacked[-1][0]
    for n, g in zip(names, unpacked[:-1]):
        shard_shape = w[n].shape[1:]
        if g.shape[-1] != shard_shape[-1]:
            g = lax.dynamic_slice_in_dim(g, chip * shard_shape[-1], shard_shape[-1], axis=g.ndim - 1)
        grads[n] = g.reshape(shard_shape)

    def two_d(a):
        return a.reshape(-1, a.shape[-1])

    upd = {}
    for n in BIG:
        if n in upd_odd:
            res = upd_odd[n]
        elif n.endswith("w_in"):
            res = _adamw_big(w[n][0].T, gbig[n], mom[n][0].T, var[n][0].T, "adamw_" + n)
            res = tuple(r.T for r in res)
        else:
            res = _adamw_big(w[n][0], gbig[n], mom[n][0], var[n][0], "adamw_" + n)
        grads[n], upd[n] = res[0], res[1:]
    res = _adamw_small([two_d(w[n][0]) for n in names], [two_d(grads[n]) for n in names],
                       [two_d(mom[n][0]) for n in names], [two_d(var[n][0]) for n in names])
    for n, r in zip(names, res):
        upd[n] = r

    outs = [loss, grad_x[None]]
    outs += [grads[n].reshape(w[n].shape) for n in WEIGHTS]
    for j in range(3):
        outs += [upd[n][j].reshape(w[n].shape) for n in WEIGHTS]
    return tuple(outs)
```

```python
import jax
import jax.numpy as jnp
from jax import lax
from jax.experimental import pallas as pl
from jax.experimental.pallas import tpu as pltpu
from jax.experimental.pallas import tpu_sc as plsc

F32 = jnp.float32
BF16 = jnp.bfloat16
MESH = pl.DeviceIdType.MESH

D = 1024
N_MEM = 256
MIX = 2048
XA = 512
HD = 128
BW = 768
CH = 128
EPS = 1e-6
SCALE = HD ** -0.5
POOL_WINDOWS = (2, 4, 8, 16)
CONF = 31
EVEN_IN = 6400
ODD_IN = 4864
N_CHIPS = 4

ADAM_LR = 0.001
ADAM_B1 = 0.9
ADAM_B2 = 0.999
ADAM_EPS = 1e-08
ADAM_WD = 0.01
ADAM_STEP = 10

TS = 256
HALO = 32
VMEM_LIMIT = 56 * 1024 * 1024


def _cp(sem=None):
    return pltpu.CompilerParams(dimension_semantics=sem, vmem_limit_bytes=VMEM_LIMIT)


def _dot(a, b):
    return jnp.dot(a, b, preferred_element_type=F32)


def _dot_nt(a, b):
    return lax.dot_general(a, b, (((1,), (1,)), ((), ())), preferred_element_type=F32)


def _dot_tn(a, b):
    return lax.dot_general(a, b, (((0,), (0,)), ((), ())), preferred_element_type=F32)


def _sigmoid(x):
    return 1.0 / (1.0 + jnp.exp(-x))


def _resident(shape):
    return pl.BlockSpec(shape, lambda *_: (0,) * len(shape), pipeline_mode=pl.Buffered(1))


def _const(shape):
    return pl.BlockSpec(shape, lambda *_: (0,) * len(shape))


def _kv_fwd(mem, mem_g, wkv, name):
    def body(mem_ref, g_ref, w_ref, kv_ref):
        m = mem_ref[...]
        r = lax.rsqrt(jnp.mean(m * m, axis=-1, keepdims=True) + EPS)
        mn = (m * r * g_ref[...]).astype(BF16)
        kv_ref[...] = _dot(mn, w_ref[...]).astype(BF16)

    return pl.pallas_call(body, out_shape=jax.ShapeDtypeStruct((N_MEM, D), BF16), name=name,
                          compiler_params=_cp())(mem, mem_g, wkv)


def _kv_bwd(mem, mem_g, wkv, dkv, name):
    def body(mem_ref, g_ref, w_ref, dkv_ref, dw_ref, dg_ref):
        m = mem_ref[...]
        r = lax.rsqrt(jnp.mean(m * m, axis=-1, keepdims=True) + EPS)
        mh = m * r
        mn = (mh * g_ref[...]).astype(BF16)
        dkv = dkv_ref[...].astype(BF16)
        dw_ref[...] = _dot_tn(mn, dkv).astype(BF16)
        dmn = _dot_nt(dkv, w_ref[...])
        dg_ref[...] = jnp.sum(dmn * mh, axis=0, keepdims=True)

    return pl.pallas_call(body, out_shape=(jax.ShapeDtypeStruct((D, D), BF16), jax.ShapeDtypeStruct((1, D), F32)),
                          name=name, compiler_params=_cp())(mem, mem_g, wkv, dkv)


def _host_call(body, *, grid, name, out_shape, in_specs, out_specs, args, scratch_shapes=(), aliases=None,
               rider=None):
    sem = ("arbitrary",) * len(grid)
    aliases = dict(aliases or {})
    if rider is None:
        res = pl.pallas_call(body, grid=grid, name=name, out_shape=tuple(out_shape), in_specs=list(in_specs),
                             out_specs=tuple(out_specs), scratch_shapes=list(scratch_shapes),
                             input_output_aliases=aliases, compiler_params=_cp(sem))(*args)
        return tuple(res), ()
    n_in, n_out, n_sc = len(in_specs), len(out_specs), len(scratch_shapes)
    r_in, r_out = len(rider.inputs), len(rider.out_shapes)

    def full_body(*refs):
        host_in = refs[:n_in]
        rid_in = refs[n_in:n_in + r_in]
        host_out = refs[n_in + r_in:n_in + r_in + n_out]
        rid_out = refs[n_in + r_in + n_out:n_in + r_in + n_out + r_out]
        host_sc = refs[n_in + r_in + n_out + r_out:n_in + r_in + n_out + r_out + n_sc]
        sems = refs[n_in + r_in + n_out + r_out + n_sc:]
        first = pl.program_id(0) == 0
        last = pl.program_id(0) == grid[0] - 1
        for ax in range(1, len(grid)):
            first = jnp.logical_and(first, pl.program_id(ax) == 0)
            last = jnp.logical_and(last, pl.program_id(ax) == grid[ax] - 1)

        @pl.when(first)
        def _():
            rider.start(rid_in, rid_out, sems)

        if rider.has_mid:
            @pl.when(last)
            def _():
                rider.mid(rid_in, rid_out, sems)

        body(*host_in, *host_out, *host_sc)

        @pl.when(last)
        def _():
            rider.end(rid_in, rid_out, sems)

    aliases.update({n_in + j: n_out + k for j, k in rider.aliases.items()})
    res = pl.pallas_call(
        full_body, grid=grid, name=name, out_shape=tuple(out_shape) + tuple(rider.out_shapes),
        in_specs=list(in_specs) + _hbm_specs(r_in), out_specs=tuple(out_specs) + tuple(_hbm_specs(r_out)),
        scratch_shapes=list(scratch_shapes) + list(rider.sems), input_output_aliases=aliases,
        compiler_params=_cp(sem),
    )(*args, *rider.inputs)
    return tuple(res[:n_out]), tuple(res[n_out:])


def _in_fwd(x, pre_g, w_t, name, rider=None):
    s, n = x.shape[0], w_t.shape[0]
    tm = min(512, s)
    nc = 256

    def body(x_ref, g_ref, w_ref, p_ref, h_ref):
        xv = x_ref[...]
        r = lax.rsqrt(jnp.mean(xv * xv, axis=-1, keepdims=True) + EPS)
        h = (xv * r * g_ref[...]).astype(BF16)
        h_ref[...] = h
        for j in range(n // nc):
            p_ref[:, j * nc:(j + 1) * nc] = _dot_nt(h, w_ref[j * nc:(j + 1) * nc, :]).astype(BF16)

    return _host_call(
        body, grid=(s // tm,), name=name, rider=rider,
        out_shape=(jax.ShapeDtypeStruct((s, n), BF16), jax.ShapeDtypeStruct((s, D), BF16)),
        in_specs=[pl.BlockSpec((tm, D), lambda i: (i, 0)), _const((1, D)), _resident((n, D))],
        out_specs=(pl.BlockSpec((tm, n), lambda i: (i, 0)), pl.BlockSpec((tm, D), lambda i: (i, 0))),
        args=(x, pre_g, w_t))


NC = 256


def _stream_tables(core, chip, n):
    nchunk = n // NC
    idx = jnp.arange(nchunk, dtype=jnp.int32)
    src = jnp.array([0, 2, 1, 3], jnp.int32)
    r = n // N_CHIPS

    def group_of(row):
        j = src[(row // r) ^ chip]
        through_sibling = ((row % r) // (r // 2) != core).astype(jnp.int32)
        return jnp.where(j == 0, 0, 2 * j - 1 + through_sibling)

    grp = jnp.maximum(group_of(idx * NC), group_of(idx * NC + NC - 1))
    order = jnp.argsort(grp * 64 + idx).astype(jnp.int32)
    return order, grp[order]


def _in_fwd_streamed(x, pre_g, first, later, order, group, name):
    s, n = x.shape[0], first[0].shape[0]
    nchunk = n // NC
    rider = _GatherRider(first)
    rider2 = _GatherRider(later) if later else None
    a, m = len(first), len(later)
    tr = min(256, s)

    def body(*refs):
        order_ref, group_ref, x_ref, g_ref = refs[0:4]
        p_ref, h_ref = refs[4 + a + m:6 + a + m]
        outs = refs[6 + a + m:6 + 2 * a + m]
        outs2 = refs[6 + 2 * a + m:6 + 2 * a + 2 * m]
        wbuf, wsem, send_sems, recv_sems = refs[6 + 2 * a + 2 * m:10 + 2 * a + 2 * m]
        sems2 = refs[10 + 2 * a + 2 * m:]
        w_hbm = outs[0]
        j = pl.program_id(0)
        sems = (send_sems, recv_sems)
        grp = group_ref[j]
        new_group = jnp.logical_or(j == 0, group_ref[jnp.maximum(j - 1, 0)] != grp)
        slot = j % 2

        def fetch(step, sl):
            rows = pl.ds(pl.multiple_of(order_ref[step] * NC, NC), NC)
            return pltpu.make_async_copy(w_hbm.at[rows], wbuf.at[sl], wsem.at[sl])

        @pl.when(j == 0)
        def _():
            rider.start(None, outs, sems, peers=(0, 1))

            @pl.loop(0, s // tr)
            def _(t):
                rows = pl.ds(pl.multiple_of(t * tr, tr), tr)
                xv = x_ref[rows, :]
                r = lax.rsqrt(jnp.mean(xv * xv, axis=-1, keepdims=True) + EPS)
                h_ref[rows, :] = (xv * r * g_ref[...]).astype(BF16)

        before = jnp.where(j == 0, 0, group_ref[jnp.maximum(j - 1, 0)])

        def entering(b):
            return jnp.logical_and(before < b, b <= grp)

        for src in range(3):
            @pl.when(entering(2 * src + 1))
            def _(src=src):
                if src == 0:
                    rider.start(None, outs, sems, peers=(2,))
                rider.mid(None, outs, sems, peers=(src,))
                if src == 1 and rider2 is not None:
                    rider2.start(None, outs2, sems2)

            @pl.when(entering(2 * src + 2))
            def _(src=src):
                rider.wait_forwarded(outs, sems, peers=(src,))

        @pl.when(new_group)
        def _():
            fetch(j, slot).start()

        fetch(j, slot).wait()
        nxt = jnp.minimum(j + 1, nchunk - 1)

        @pl.when(jnp.logical_and(j + 1 < nchunk, group_ref[nxt] == grp))
        def _():
            fetch(nxt, 1 - slot).start()

        p_ref[...] = _dot_nt(h_ref[...], wbuf[slot]).astype(BF16)

        @pl.when(j == nchunk - 1)
        def _():
            rider.wait_sends(outs, sems)
            if rider2 is not None:
                rider2.mid(None, outs2, sems2)
                rider2.end(None, outs2, sems2)

    hbm = pl.BlockSpec(memory_space=pltpu.HBM)
    arrs = list(first) + list(later)
    whole = pl.BlockSpec((s, D), lambda j, o, g: (0, 0), pipeline_mode=pl.Buffered(1))
    res = pl.pallas_call(
        body, name=name,
        out_shape=(jax.ShapeDtypeStruct((s, n), BF16), jax.ShapeDtypeStruct((s, D), BF16))
        + tuple(jax.ShapeDtypeStruct(v.shape, v.dtype) for v in arrs),
        grid_spec=pltpu.PrefetchScalarGridSpec(
            num_scalar_prefetch=2, grid=(nchunk,),
            in_specs=[whole, pl.BlockSpec((1, D), lambda j, o, g: (0, 0))] + [hbm] * (a + m),
            out_specs=(pl.BlockSpec((s, NC), lambda j, o, g: (0, o[j])),
                       pl.BlockSpec((s, D), lambda j, o, g: (0, 0))) + (hbm,) * (a + m),
            scratch_shapes=[pltpu.VMEM((2, NC, D), BF16), pltpu.SemaphoreType.DMA((2,))] + list(rider.sems)
            + (list(rider2.sems) if rider2 is not None else [])),
        input_output_aliases={4 + v: 2 + v for v in range(a + m)},
        compiler_params=_cp(("arbitrary",)),
    )(order, group, x, pre_g, *arrs)
    return res[0], res[1], tuple(res[2:2 + a]), tuple(res[2 + a:])


def _xattn_fwd(q, kv_ref):
    outs, probs = [], []
    for h in range(XA // HD):
        qh = q[:, h * HD:(h + 1) * HD]
        kh = kv_ref[:, h * HD:(h + 1) * HD]
        vh = kv_ref[:, XA + h * HD:XA + (h + 1) * HD]
        sc = _dot_nt(qh, kh) * SCALE
        e = jnp.exp(sc - jnp.max(sc, axis=-1, keepdims=True))
        pr = e / jnp.sum(e, axis=-1, keepdims=True)
        outs.append(_dot(pr.astype(BF16), vh))
        probs.append(pr)
    return jnp.concatenate(outs, axis=-1), probs


def _xattn_bwd(dyx, q, probs, kv_ref, dkv_ref):
    dqs = []
    for h in range(XA // HD):
        qh = q[:, h * HD:(h + 1) * HD]
        kh = kv_ref[:, h * HD:(h + 1) * HD]
        vh = kv_ref[:, XA + h * HD:XA + (h + 1) * HD]
        dy = dyx[:, h * HD:(h + 1) * HD].astype(BF16)
        pr = probs[h]
        dp = _dot_nt(dy, vh)
        ds = (pr * (dp - jnp.sum(dp * pr, axis=-1, keepdims=True))).astype(BF16)
        dqs.append(_dot(ds, kh) * SCALE)
        dkv_ref[:, h * HD:(h + 1) * HD] += _dot_tn(ds, qh) * SCALE
        dkv_ref[:, XA + h * HD:XA + (h + 1) * HD] += _dot_tn(pr.astype(BF16), dy)
    return jnp.concatenate(dqs, axis=-1)


def _layer_norm_fwd(v, g, b):
    mu = jnp.mean(v, axis=-1, keepdims=True)
    vc = v - mu
    rstd = lax.rsqrt(jnp.mean(vc * vc, axis=-1, keepdims=True) + EPS)
    vhat = vc * rstd
    return vhat * g + b, vhat, rstd


def _layer_norm_bwd(dy, vhat, rstd, g):
    dvh = dy * g
    return rstd * (dvh - jnp.mean(dvh, axis=-1, keepdims=True) - vhat * jnp.mean(dvh * vhat, axis=-1, keepdims=True))


def _head_masks():
    col = lax.broadcasted_iota(jnp.int32, (1, BW), 1)
    return [(col >= h * (BW // 4)) & (col < (h + 1) * (BW // 4)) for h in range(4)]


def _halo_prev(nblk_per_tile):
    return lambda i: (jnp.maximum(i * nblk_per_tile - 1, 0), 0)


def _row_ids(i, t):
    return i * t + lax.broadcasted_iota(jnp.int32, (t, 1), 0)


def _even_mix(i, p_ref, ph_ref, ln_g, ln_b, wcat_ref, bsg_ref, bconv_ref, wbuf):
    t = p_ref.shape[0]
    u = p_ref[:, 0:BW].astype(F32)
    v = p_ref[:, BW:2 * BW].astype(F32)
    bg = p_ref[:, 2 * BW:3 * BW].astype(F32)
    cg = p_ref[:, 3 * BW:4 * BW].astype(F32)
    xin = p_ref[:, 4 * BW:5 * BW].astype(F32)
    vn, vhat, rstd = _layer_norm_fwd(v, ln_g, ln_b)
    masks = _head_masks()
    sgs, vsts = [], []
    for n in range(t // CH):
        vn_c = vn[n * CH:(n + 1) * CH]
        vst = jnp.concatenate([jnp.where(m, vn_c, 0.0) for m in masks], axis=0).astype(BF16)
        sgs.append(_dot(wcat_ref[...], vst) + bsg_ref[...])
        vsts.append(vst)
    sg = jnp.concatenate(sgs, axis=0)
    ya = u * sg
    w_halo = ph_ref[:, 3 * BW:4 * BW].astype(F32) * ph_ref[:, 4 * BW:5 * BW].astype(F32)
    wbuf[0:HALO, :] = jnp.where(i > 0, w_halo, 0.0)
    wbuf[HALO:HALO + t, :] = cg * xin
    conv = (bconv_ref[0:1, :] * wbuf[pl.ds(HALO - 2, t), :] + bconv_ref[1:2, :] * wbuf[pl.ds(HALO - 1, t), :]
            + bconv_ref[2:3, :] * wbuf[pl.ds(HALO, t), :])
    yb = bg * conv
    return dict(u=u, bg=bg, cg=cg, xin=xin, vhat=vhat, rstd=rstd, sg=sg, vsts=vsts, conv=conv, ya=ya, yb=yb,
                masks=masks)


def _pool_select(vals):
    col = lax.broadcasted_iota(jnp.int32, (1, BW), 1)
    g = BW // 4
    return jnp.where(col < g, vals[0], jnp.where(col < 2 * g, vals[1], jnp.where(col < 3 * g, vals[2], vals[3])))


def _inv_counts(i, t):
    rows = _row_ids(i, t) + 1
    return [1.0 / jnp.minimum(rows, w).astype(F32) for w in POOL_WINDOWS]


def _band_matrices(t, forward):
    j = jnp.arange(t)[:, None]
    r = jnp.arange(HALO + t)[None, :]
    if forward:
        return jnp.stack([(r >= j) & (r < j + w) for w in POOL_WINDOWS]).astype(BF16)
    return jnp.stack([(r <= HALO + j) & (r > HALO + j - w) for w in POOL_WINDOWS]).astype(BF16)


SHIFT_ROWS = HALO + TS - 8


def _shifted_copies(buf, sh):
    for b in range(1, 8):
        sh[b - 1] = buf[pl.ds(b, SHIFT_ROWS), :]


def _rows_at(buf, sh, off, t):
    a, b = divmod(off, 8)
    return buf[pl.ds(8 * a, t), :] if b == 0 else sh[b - 1, pl.ds(8 * a, t), :]


def _tap_sums(d_ref, buf, sh, base, out_ref):
    t = d_ref.shape[0]
    group = 4
    for k0 in range(0, CONF, group):
        taps = list(range(k0, min(k0 + group, CONF)))

        def step(r, accs, taps=taps):
            row = pl.multiple_of(r * 8, 8)
            d = d_ref[pl.ds(row, 8), :]
            new = []
            for acc, k in zip(accs, taps):
                a, b = divmod(base + k, 8)
                src = buf[pl.ds(row + 8 * a, 8), :] if b == 0 else sh[b - 1, pl.ds(row + 8 * a, 8), :]
                new.append(acc + d * src)
            return tuple(new)

        accs = lax.fori_loop(0, t // 8, step, tuple(jnp.zeros((8, BW), F32) for _ in taps), unroll=2)
        for acc, k in zip(accs, taps):
            out_ref[8 * k:8 * k + 8, :] += acc


def _odd_mix(i, p_ref, ph_ref, bands_ref, wbd_ref, cscale, dww_ref, dwb, ln_g, ln_b, pww_ref, pwb, gbuf, gsh,
             cv=None):
    t = p_ref.shape[0]
    zc_bf = p_ref[:, 0:BW]
    zc = zc_bf.astype(F32)
    ga = p_ref[:, BW:2 * BW].astype(F32)
    gb = p_ref[:, 2 * BW:3 * BW].astype(F32)
    zh = ph_ref[:, 0:BW]
    zcat = jnp.concatenate([jnp.where(i > 0, zh, jnp.zeros_like(zh)), zc_bf], axis=0)
    inv = _inv_counts(i, t)
    pooled = _pool_select([_dot(bands_ref[w], zcat) * inv[w] for w in range(len(POOL_WINDOWS))]) - zc
    pooled_bf = pooled.astype(BF16)
    pre = _dot(pooled_bf, wbd_ref[...])
    yc = pre * cscale
    sgb = _sigmoid(gb)
    z = ga * sgb
    gh_a = ph_ref[:, BW:2 * BW].astype(F32)
    gh_b = ph_ref[:, 2 * BW:3 * BW].astype(F32)
    gbuf[0:HALO, :] = jnp.where(i > 0, gh_a * _sigmoid(gh_b), 0.0)
    gbuf[HALO:HALO + t, :] = z
    _shifted_copies(gbuf, gsh)
    if cv is None:
        cv = dwb + dww_ref[CONF - 1:CONF, :] * z
        for k in range(CONF - 1):
            cv = cv + dww_ref[k:k + 1, :] * _rows_at(gbuf, gsh, HALO - (CONF - 1) + k, t)
    zl, zhat, rstd = _layer_norm_fwd(cv, ln_g, ln_b)
    szl = _sigmoid(zl)
    zs = (zl * szl).astype(BF16)
    yd = _dot(zs, pww_ref[...]) + pwb
    return dict(ga=ga, sgb=sgb, pooled_bf=pooled_bf, pre=pre, yc=yc, zhat=zhat, rstd=rstd, zl=zl, szl=szl,
                zs=zs, yd=yd, inv=inv, cv=cv)


def _post_norm(o, post_g):
    r = lax.rsqrt(jnp.mean(o * o, axis=-1, keepdims=True) + EPS)
    return o * r, r


def _gate_out(y_a, y_b, y_x, gate, wout_ref):
    sgt = _sigmoid(gate)
    sgate = gate * sgt
    ys = [(y_a * sgate[:, 0:BW]).astype(BF16), (y_b * sgate[:, BW:2 * BW]).astype(BF16),
          (y_x * sgate[:, 2 * BW:MIX]).astype(BF16)]
    o = (_dot(ys[0], wout_ref[0:BW, :]) + _dot(ys[1], wout_ref[BW:2 * BW, :]) + _dot(ys[2], wout_ref[2 * BW:MIX, :]))
    return o, ys, sgt, sgate


def _tile_specs(s, n):
    nh = TS // HALO
    return pl.BlockSpec((TS, n), lambda i: (i, 0)), pl.BlockSpec((HALO, n), _halo_prev(nh))


def _even_fwd(x, p, kv, ln_g, ln_b, wcat, bsg, bconv, wout, post_g, rider=None):
    s = x.shape[0]

    def body(x_ref, p_ref, ph_ref, kv_ref, lng, lnb, wcat_ref, bsg_ref, bconv_ref, wout_ref, pg, x1_ref, o_ref,
             y_ref, wbuf):
        i = pl.program_id(0)
        mx = _even_mix(i, p_ref, ph_ref, lng[...], lnb[...], wcat_ref, bsg_ref, bconv_ref, wbuf)
        yx, _ = _xattn_fwd(p_ref[:, 5 * BW:5 * BW + XA], kv_ref)
        gate = p_ref[:, 5 * BW + XA:EVEN_IN].astype(F32)
        o, ys, _, _ = _gate_out(mx["ya"], mx["yb"], yx, gate, wout_ref)
        y_ref[:, 0:BW] = ys[0]
        y_ref[:, BW:2 * BW] = ys[1]
        y_ref[:, 2 * BW:MIX] = ys[2]
        n, _ = _post_norm(o, pg[...])
        o_ref[...] = o
        x1_ref[...] = x_ref[...] + n * pg[...]

    tile, halo = _tile_specs(s, EVEN_IN)
    row = pl.BlockSpec((TS, D), lambda i: (i, 0))
    return _host_call(
        body, grid=(s // TS,), name="even_fwd", rider=rider,
        out_shape=(jax.ShapeDtypeStruct((s, D), F32), jax.ShapeDtypeStruct((s, D), F32),
                   jax.ShapeDtypeStruct((s, MIX), BF16)),
        in_specs=[row, tile, halo, _const((N_MEM, D)), _const((1, BW)), _const((1, BW)), _const((CH, 4 * CH)),
                  _const((CH, BW)), _const((3, BW)), _resident((MIX, D)), _const((1, D))],
        out_specs=(row, row, pl.BlockSpec((TS, MIX), lambda i: (i, 0))),
        scratch_shapes=[pltpu.VMEM((HALO + TS, BW), F32)],
        args=(x, p, p, kv, ln_g, ln_b, wcat, bsg, bconv, wout, post_g))


def _odd_fwd(x1, p, kv, wbd, cscale, dww, dwb, ln_g, ln_b, pww, pwb, wout, post_g, target):
    s = x1.shape[0]

    def body(x_ref, p_ref, ph_ref, kv_ref, bands_ref, wbd_ref, cs, dww_ref, dwb_ref, lng, lnb, pww_ref, pwb_ref,
             wout_ref, pg, tgt_ref, dx_ref, o_ref, cv_ref, loss_ref, gbuf, gsh):
        i = pl.program_id(0)
        mx = _odd_mix(i, p_ref, ph_ref, bands_ref, wbd_ref, cs[...], dww_ref, dwb_ref[...], lng[...], lnb[...],
                      pww_ref, pwb_ref[...], gbuf, gsh)
        cv_ref[...] = mx["cv"]
        yx, _ = _xattn_fwd(p_ref[:, 3 * BW:3 * BW + XA], kv_ref)
        gate = p_ref[:, 3 * BW + XA:ODD_IN].astype(F32)
        o, _, _, _ = _gate_out(mx["yc"], mx["yd"], yx, gate, wout_ref)
        n, _ = _post_norm(o, pg[...])
        o_ref[...] = o
        err = x_ref[...] + n * pg[...] - tgt_ref[...]
        dx_ref[...] = err * (1.0 / D)

        @pl.when(i == 0)
        def _():
            loss_ref[...] = jnp.zeros_like(loss_ref)

        loss_ref[...] += 0.5 * jnp.sum(jnp.sum(err * err, axis=-1, keepdims=True) * (1.0 / D), axis=0, keepdims=True)

    tile, halo = _tile_specs(s, ODD_IN)
    row = pl.BlockSpec((TS, D), lambda i: (i, 0))
    vec = _const((1, BW))
    return pl.pallas_call(
        body, grid=(s // TS,), name="odd_fwd",
        out_shape=(jax.ShapeDtypeStruct((s, D), F32), jax.ShapeDtypeStruct((s, D), F32),
                   jax.ShapeDtypeStruct((s, BW), F32), jax.ShapeDtypeStruct((8, 128), F32)),
        in_specs=[row, tile, halo, _const((N_MEM, D)), _const((4, TS, HALO + TS)), _const((BW, BW)), vec,
                  _const((CONF, BW)), vec, vec, vec, _const((BW, BW)), vec, _resident((MIX, D)), _const((1, D)), row],
        out_specs=(row, row, pl.BlockSpec((TS, BW), lambda i: (i, 0)), _const((8, 128))),
        scratch_shapes=[pltpu.VMEM((HALO + TS, BW), F32), pltpu.VMEM((7, SHIFT_ROWS, BW), F32)],
        compiler_params=_cp(("arbitrary",)),
    )(x1, p, p, kv, _band_matrices(TS, False), wbd, cscale, dww, dwb, ln_g, ln_b, pww, pwb, wout, post_g, target)


def _acc_init(i, refs):
    @pl.when(i == 0)
    def _():
        for r in refs:
            r[...] = jnp.zeros_like(r)


def _post_norm_bwd(dx, o, pg, dpg_ref):
    n, r = _post_norm(o, pg)
    dpg_ref[...] += jnp.sum(dx * n, axis=0, keepdims=True)
    dn = dx * pg
    return (r * (dn - n * jnp.mean(dn * n, axis=-1, keepdims=True))).astype(BF16)


def _gate_bwd(do, wout_ref, ys_f32, gate, y_ref):
    dy = _dot_nt(do, wout_ref[...])
    sgt = _sigmoid(gate)
    sgate = gate * sgt
    dsilu = sgt * (1.0 + gate * (1.0 - sgt))
    offs = (0, BW, 2 * BW, MIX)
    dys, dgs = [], []
    for j, yv in enumerate(ys_f32):
        a, b = offs[j], offs[j + 1]
        if y_ref is not None:
            y_ref[:, a:b] = (yv * sgate[:, a:b]).astype(BF16)
        dys.append(dy[:, a:b] * sgate[:, a:b])
        dgs.append(dy[:, a:b] * yv * dsilu[:, a:b])
    return dys, jnp.concatenate(dgs, axis=-1)


NEXT = 16


def _even_bwd1(dx, o, p, kv, ln_g, ln_b, wcat, bsg, hsel, bconv, wout, post_g, rider=None):
    s = dx.shape[0]
    nt = s // TS

    def body(dx_ref, o_ref, p_ref, ph_ref, dxn_ref, on_ref, pn_ref, kv_ref, lng, lnb, wcat_ref, bsg_ref, hsel_ref,
             bconv_ref, wout_ref, pg,
             dp_ref, do_ref, dpg_ref, dlng_ref, dlnb_ref, dwcat_ref, dbs_ref, dbconv_ref, dkv_ref, wbuf, dbuf):
        i = pl.program_id(0)
        _acc_init(i, (dpg_ref, dlng_ref, dlnb_ref, dwcat_ref, dbs_ref, dbconv_ref, dkv_ref))
        mx = _even_mix(i, p_ref, ph_ref, lng[...], lnb[...], wcat_ref, bsg_ref, bconv_ref, wbuf)
        q = p_ref[:, 5 * BW:5 * BW + XA]
        yx, probs = _xattn_fwd(q, kv_ref)
        gate = p_ref[:, 5 * BW + XA:EVEN_IN].astype(F32)
        do = _post_norm_bwd(dx_ref[...], o_ref[...], pg[...], dpg_ref)
        do_ref[...] = do
        (dya, dyb, dyx), dgate = _gate_bwd(do, wout_ref, (mx["ya"], mx["yb"], yx), gate, None)
        dp_ref[:, 0:BW] = (dya * mx["sg"]).astype(BF16)
        dsg = (dya * mx["u"]).astype(BF16)
        dvns = []
        for n in range(TS // CH):
            dsg_c = dsg[n * CH:(n + 1) * CH]
            dvst = _dot_tn(wcat_ref[...], dsg_c)
            dvn_c = jnp.where(mx["masks"][0], dvst[0:CH], 0.0)
            for h in range(1, 4):
                dvn_c = dvn_c + jnp.where(mx["masks"][h], dvst[h * CH:(h + 1) * CH], 0.0)
            dvns.append(dvn_c)
            dwcat_ref[...] += _dot_nt(dsg_c, mx["vsts"][n])
            dbs_ref[...] += _dot(dsg_c, hsel_ref[...])
        dvn = jnp.concatenate(dvns, axis=0)
        dlng_ref[...] += jnp.sum(dvn * mx["vhat"], axis=0, keepdims=True)
        dlnb_ref[...] += jnp.sum(dvn, axis=0, keepdims=True)
        dp_ref[:, BW:2 * BW] = _layer_norm_bwd(dvn, mx["vhat"], mx["rstd"], lng[...]).astype(BF16)
        dp_ref[:, 2 * BW:3 * BW] = (dyb * mx["conv"]).astype(BF16)
        dconv = dyb * mx["bg"]
        for k in range(3):
            dbconv_ref[k:k + 1, :] += jnp.sum(dconv * wbuf[pl.ds(HALO - 2 + k, TS), :], axis=0, keepdims=True)
        n_n, r_n = _post_norm(on_ref[...], pg[...])
        dn_n = dxn_ref[...] * pg[...]
        do_n = (r_n * (dn_n - n_n * jnp.mean(dn_n * n_n, axis=-1, keepdims=True))).astype(BF16)
        dy_n = _dot_nt(do_n, wout_ref[BW:2 * BW, :])
        g_n = pn_ref[:, 5 * BW + XA + BW:5 * BW + XA + 2 * BW].astype(F32)
        dconv_n = dy_n * (g_n * _sigmoid(g_n)) * pn_ref[:, 2 * BW:3 * BW].astype(F32)
        dbuf[0:TS, :] = dconv
        dbuf[TS:TS + NEXT, :] = jnp.where(i < nt - 1, dconv_n, 0.0)
        dw = (bconv_ref[2:3, :] * dconv + bconv_ref[1:2, :] * dbuf[pl.ds(1, TS), :]
              + bconv_ref[0:1, :] * dbuf[pl.ds(2, TS), :])
        dp_ref[:, 3 * BW:4 * BW] = (dw * mx["xin"]).astype(BF16)
        dp_ref[:, 4 * BW:5 * BW] = (dw * mx["cg"]).astype(BF16)
        dp_ref[:, 5 * BW:5 * BW + XA] = _xattn_bwd(dyx, q, probs, kv_ref, dkv_ref).astype(BF16)
        dp_ref[:, 5 * BW + XA:EVEN_IN] = dgate.astype(BF16)

    tile, halo = _tile_specs(s, EVEN_IN)
    row = pl.BlockSpec((TS, D), lambda i: (i, 0))
    vec = _const((1, BW))
    nxt = _halo_next(TS // NEXT, s // NEXT)

    def out(n):
        return pl.BlockSpec((TS, n), lambda i: (i, 0))

    return _host_call(
        body, grid=(nt,), name="even_bwd1", rider=rider,
        out_shape=(jax.ShapeDtypeStruct((s, EVEN_IN), BF16), jax.ShapeDtypeStruct((s, D), BF16),
                   jax.ShapeDtypeStruct((1, D), F32), jax.ShapeDtypeStruct((1, BW), F32),
                   jax.ShapeDtypeStruct((1, BW), F32), jax.ShapeDtypeStruct((CH, 4 * CH), F32),
                   jax.ShapeDtypeStruct((CH, 128), F32), jax.ShapeDtypeStruct((8, BW), F32),
                   jax.ShapeDtypeStruct((N_MEM, D), F32)),
        in_specs=[row, row, tile, halo, pl.BlockSpec((NEXT, D), nxt), pl.BlockSpec((NEXT, D), nxt),
                  pl.BlockSpec((NEXT, EVEN_IN), nxt), _const((N_MEM, D)), vec, vec, _const((CH, 4 * CH)),
                  _const((CH, BW)), _const((BW, 128)), _const((3, BW)), _resident((MIX, D)), _const((1, D))],
        out_specs=(out(EVEN_IN), out(D),
                   _const((1, D)), vec, vec, _const((CH, 4 * CH)), _const((CH, 128)), _const((8, BW)),
                   _const((N_MEM, D))),
        scratch_shapes=[pltpu.VMEM((HALO + TS, BW), F32), pltpu.VMEM((TS + NEXT, BW), F32)],
        args=(dx, o, p, p, dx, o, p, kv, ln_g, ln_b, wcat, bsg, hsel, bconv, wout, post_g))


def _odd_bwd1(dx, o, cv, p, kv, wbd, cscale, dww, dwb, ln_g, ln_b, pww, pwb, wout, post_g):
    s = dx.shape[0]

    def body(dx_ref, o_ref, cv_ref, p_ref, ph_ref, kv_ref, bands_ref, wbd_ref, cs, dww_ref, dwb_ref, lng, lnb,
             pww_ref, pwb_ref, wout_ref, pg,
             dpc_ref, tmpc_ref, tmpd_ref, do_ref, y_ref, dpg_ref, dcs_ref, dwbd_ref, ddww_ref, ddwb_ref, dlng_ref,
             dlnb_ref, dpww_ref, dpwb_ref, dkv_ref, gbuf, gsh, dcv_buf):
        i = pl.program_id(0)
        _acc_init(i, (dpg_ref, dcs_ref, dwbd_ref, ddww_ref, ddwb_ref, dlng_ref, dlnb_ref, dpww_ref, dpwb_ref,
                      dkv_ref))
        mx = _odd_mix(i, p_ref, ph_ref, bands_ref, wbd_ref, cs[...], dww_ref, dwb_ref[...], lng[...], lnb[...],
                      pww_ref, pwb_ref[...], gbuf, gsh, cv=cv_ref[...])
        q = p_ref[:, 3 * BW:3 * BW + XA]
        yx, probs = _xattn_fwd(q, kv_ref)
        gate = p_ref[:, 3 * BW + XA:ODD_IN].astype(F32)
        do = _post_norm_bwd(dx_ref[...], o_ref[...], pg[...], dpg_ref)
        do_ref[...] = do
        (dyc, dyd, dyx), dgate = _gate_bwd(do, wout_ref, (mx["yc"], mx["yd"], yx), gate, y_ref)
        dcs_ref[...] += jnp.sum(dyc * mx["pre"], axis=0, keepdims=True)
        dpre = (dyc * cs[...]).astype(BF16)
        dwbd_ref[...] += _dot_tn(mx["pooled_bf"], dpre)
        dpooled = _dot_nt(dpre, wbd_ref[...])
        tmpc_ref[...] = _pool_select([dpooled * c_ for c_ in mx["inv"]]).astype(BF16)
        dyd_bf = dyd.astype(BF16)
        dpwb_ref[...] += jnp.sum(dyd, axis=0, keepdims=True)
        dpww_ref[...] += _dot_tn(mx["zs"], dyd_bf)
        dzs = _dot_nt(dyd_bf, pww_ref[...])
        zl, szl = mx["zl"], mx["szl"]
        dzl = dzs * (szl * (1.0 + zl * (1.0 - szl)))
        dlng_ref[...] += jnp.sum(dzl * mx["zhat"], axis=0, keepdims=True)
        dlnb_ref[...] += jnp.sum(dzl, axis=0, keepdims=True)
        dcv = _layer_norm_bwd(dzl, mx["zhat"], mx["rstd"], lng[...])
        tmpd_ref[...] = dcv.astype(BF16)
        ddwb_ref[...] += jnp.sum(dcv, axis=0, keepdims=True)
        dcv_buf[...] = dcv
        _tap_sums(dcv_buf, gbuf, gsh, HALO - (CONF - 1), ddww_ref)
        dpc_ref[:, 0:XA] = _xattn_bwd(dyx, q, probs, kv_ref, dkv_ref).astype(BF16)
        dpc_ref[:, XA:XA + MIX] = dgate.astype(BF16)

    tile, halo = _tile_specs(s, ODD_IN)
    row = pl.BlockSpec((TS, D), lambda i: (i, 0))
    vec = _const((1, BW))

    def out(n):
        return pl.BlockSpec((TS, n), lambda i: (i, 0))

    return pl.pallas_call(
        body, grid=(s // TS,), name="odd_bwd1",
        out_shape=(jax.ShapeDtypeStruct((s, XA + MIX), BF16), jax.ShapeDtypeStruct((s, BW), BF16),
                   jax.ShapeDtypeStruct((s, BW), BF16), jax.ShapeDtypeStruct((s, D), BF16),
                   jax.ShapeDtypeStruct((s, MIX), BF16),
                   jax.ShapeDtypeStruct((1, D), F32), jax.ShapeDtypeStruct((1, BW), F32),
                   jax.ShapeDtypeStruct((BW, BW), F32), jax.ShapeDtypeStruct((8 * CONF, BW), F32),
                   jax.ShapeDtypeStruct((1, BW), F32), jax.ShapeDtypeStruct((1, BW), F32),
                   jax.ShapeDtypeStruct((1, BW), F32), jax.ShapeDtypeStruct((BW, BW), F32),
                   jax.ShapeDtypeStruct((1, BW), F32), jax.ShapeDtypeStruct((N_MEM, D), F32)),
        in_specs=[row, row, out(BW), tile, halo, _const((N_MEM, D)), _const((4, TS, HALO + TS)), _const((BW, BW)), vec,
                  _const((CONF, BW)), vec, vec, vec, _const((BW, BW)), vec, _resident((MIX, D)), _const((1, D))],
        out_specs=(out(XA + MIX), out(BW), out(BW), out(D), out(MIX),
                   _const((1, D)), vec, _const((BW, BW)), _const((8 * CONF, BW)), vec, vec, vec, _const((BW, BW)), vec,
                   _const((N_MEM, D))),
        scratch_shapes=[pltpu.VMEM((HALO + TS, BW), F32), pltpu.VMEM((7, SHIFT_ROWS, BW), F32),
                        pltpu.VMEM((TS, BW), F32)],
        compiler_params=_cp(("arbitrary",)),
    )(dx, o, cv, p, p, kv, _band_matrices(TS, False), wbd, cscale, dww, dwb, ln_g, ln_b, pww, pwb, wout, post_g)


def _halo_next(nblk_per_tile, nblk):
    return lambda i: (jnp.minimum((i + 1) * nblk_per_tile, nblk - 1), 0)


def _pre_norm_bwd(dh, x, pre_g, dres, dpre_ref):
    r = lax.rsqrt(jnp.mean(x * x, axis=-1, keepdims=True) + EPS)
    xh = x * r
    dpre_ref[...] += jnp.sum(dh * xh, axis=0, keepdims=True)
    dxh = dh * pre_g
    return dres + r * (dxh - xh * jnp.mean(dxh * xh, axis=-1, keepdims=True))


def _even_bwd2(dp, w_t, x, pre_g, dres, rider=None):
    s = x.shape[0]
    tm = min(512, s)

    def body(dp_ref, w_ref, x_ref, pg, dres_ref, dx_ref, dpre_ref):
        _acc_init(pl.program_id(0), (dpre_ref,))
        dh = _dot(dp_ref[...], w_ref[...])
        dx_ref[...] = _pre_norm_bwd(dh, x_ref[...], pg[...], dres_ref[...], dpre_ref)

    row = pl.BlockSpec((tm, D), lambda i: (i, 0))
    return _host_call(
        body, grid=(s // tm,), name="even_bwd2", rider=rider,
        out_shape=(jax.ShapeDtypeStruct((s, D), F32), jax.ShapeDtypeStruct((1, D), F32)),
        in_specs=[pl.BlockSpec((tm, EVEN_IN), lambda i: (i, 0)), _resident((EVEN_IN, D)), row, _const((1, D)), row],
        out_specs=(row, _const((1, D))),
        args=(dp, w_t, x, pre_g, dres))


def _odd_bwd2(dpc, tmpc, tmpd, p, dww, w_t, x, pre_g, dres):
    s = x.shape[0]
    nt = s // TS

    def body(dpc_ref, tc_ref, tch_ref, td_ref, tdh_ref, ga_ref, gb_ref, bands_ref, dww_ref, w_ref, x_ref, pg,
             dres_ref, dpb_ref, dx_ref, dpre_ref, dbuf, dsh):
        i = pl.program_id(0)
        _acc_init(i, (dpre_ref,))
        more = i < nt - 1
        e_bf = tc_ref[...]
        eh = tch_ref[...]
        ecat = jnp.concatenate([e_bf, jnp.where(more, eh, jnp.zeros_like(eh))], axis=0)
        dbuf[0:TS, :] = td_ref[...].astype(F32)
        dbuf[TS:TS + HALO, :] = jnp.where(more, tdh_ref[...].astype(F32), 0.0)
        sums = [_dot(bands_ref[w], ecat) for w in range(len(POOL_WINDOWS))]
        rows = _row_ids(i, TS) + 1
        cnt = _pool_select([jnp.minimum(rows, w).astype(F32) for w in POOL_WINDOWS])
        dzc = (_pool_select(sums) - e_bf.astype(F32) * cnt).astype(BF16)
        _shifted_copies(dbuf, dsh)
        dz = dww_ref[CONF - 1:CONF, :] * dbuf[pl.ds(0, TS), :]
        for sft in range(1, CONF):
            dz = dz + dww_ref[CONF - 1 - sft:CONF - sft, :] * _rows_at(dbuf, dsh, sft, TS)
        ga = ga_ref[...].astype(F32)
        sgb = _sigmoid(gb_ref[...].astype(F32))
        dga = (dz * sgb).astype(BF16)
        dgb = (dz * ga * sgb * (1.0 - sgb)).astype(BF16)
        dpb_ref[:, 0:BW] = dzc
        dpb_ref[:, BW:2 * BW] = dga
        dpb_ref[:, 2 * BW:3 * BW] = dgb
        dh = (_dot(dzc, w_ref[0:BW, :]) + _dot(dga, w_ref[BW:2 * BW, :]) + _dot(dgb, w_ref[2 * BW:3 * BW, :])
              + _dot(dpc_ref[...], w_ref[3 * BW:ODD_IN, :]))
        dx_ref[...] = _pre_norm_bwd(dh, x_ref[...], pg[...], dres_ref[...], dpre_ref)

    row = pl.BlockSpec((TS, D), lambda i: (i, 0))

    def tile(n, j=0):
        return pl.BlockSpec((TS, n), lambda i: (i, j))

    nxt = pl.BlockSpec((HALO, BW), _halo_next(TS // HALO, s // HALO))
    return pl.pallas_call(
        body, grid=(nt,), name="odd_bwd2",
        out_shape=(jax.ShapeDtypeStruct((s, 3 * BW), BF16), jax.ShapeDtypeStruct((s, D), F32),
                   jax.ShapeDtypeStruct((1, D), F32)),
        in_specs=[tile(XA + MIX), tile(BW), nxt, tile(BW), nxt, tile(BW, 1), tile(BW, 2), _const((4, TS, HALO + TS)),
                  _const((CONF, BW)), _resident((ODD_IN, D)), row, _const((1, D)), row],
        out_specs=(tile(3 * BW), row, _const((1, D))),
        scratch_shapes=[pltpu.VMEM((TS + HALO, BW), F32), pltpu.VMEM((7, SHIFT_ROWS, BW), F32)],
        compiler_params=_cp(("arbitrary",)),
    )(dpc, tmpc, tmpc, tmpd, tmpd, p, p, _band_matrices(TS, True), dww, w_t, x, pre_g, dres)


def _grad_tn(a, b, tm, out=None, rows=None, row0=0, name="grad_tn", rider=None):
    s, m = a.shape
    n = b.shape[1]
    ts = min(2048, s)
    rows = m if rows is None else rows
    assert m % tm == 0 and s % ts == 0
    ns = s // ts
    if row0 % tm == 0:
        out_spec = pl.BlockSpec((tm, n), lambda i, k: (row0 // tm + i, 0))
    else:
        align = 16
        assert row0 % align == 0 and tm % align == 0
        out_spec = pl.BlockSpec((pl.Element(tm), pl.Element(n)),
                                lambda i, k: (pl.multiple_of(row0 + i * tm, align), 0))

    def body(*refs):
        a_ref, b_ref = refs[0], refs[1]
        o_ref, acc = refs[-2], refs[-1]
        k = pl.program_id(1)

        @pl.when(k == 0)
        def _():
            acc[...] = jnp.zeros_like(acc)

        acc[...] += _dot_tn(a_ref[...], b_ref[...])

        @pl.when(k == ns - 1)
        def _():
            o_ref[...] = acc[...].astype(BF16)

    in_specs = [pl.BlockSpec((ts, tm), lambda i, k: (k, i)), pl.BlockSpec((ts, n), lambda i, k: (k, 0))]
    args = [a, b]
    aliases = {}
    if out is not None:
        in_specs.append(pl.BlockSpec(memory_space=pltpu.HBM))
        args.append(out)
        aliases = {2: 0}
    (res,), got = _host_call(
        body, grid=(m // tm, ns), name=name, rider=rider, aliases=aliases,
        out_shape=(jax.ShapeDtypeStruct((rows, n), BF16),), in_specs=in_specs, out_specs=(out_spec,),
        scratch_shapes=[pltpu.VMEM((tm, n), F32)], args=args)
    return res if rider is None else (res, got)


def _place():
    x, y, c = lax.axis_index("x"), lax.axis_index("y"), lax.axis_index("c")
    chips = [(1 - x, y), (x, 1 - y), (1 - x, 1 - y)]
    return x, y, c, chips


def _hbm_specs(n):
    return [pl.BlockSpec(memory_space=pltpu.HBM)] * n


def _row_tile(r):
    for cand in (512, 400, 304, 256, 192, 128, 96, 16):
        if r % cand == 0:
            return cand
    raise ValueError(r)


def _place_shard(shard, place, dtype, name, after=None):
    r, cc = shard.shape
    tr = _row_tile(r)
    nt = r // tr

    def body(place_ref, s_ref, *rest):
        rest[-1][...] = s_ref[...].astype(dtype)

    in_specs = [pl.BlockSpec((tr, cc), lambda i, pr: (i, 0))]
    args = [shard]
    if after is not None:
        in_specs.append(pl.BlockSpec(after.shape, lambda i, pr: (0, 0)))
        args.append(after)
    return pl.pallas_call(
        body, name=name, out_shape=jax.ShapeDtypeStruct((N_CHIPS * r, cc), dtype),
        grid_spec=pltpu.PrefetchScalarGridSpec(
            num_scalar_prefetch=1, grid=(nt,), in_specs=in_specs,
            out_specs=pl.BlockSpec((tr, cc), lambda i, pr: (pr[1] * nt + i, 0))),
        compiler_params=_cp(("arbitrary",)),
    )(place, *args)


class _GatherRider:
    has_mid = True

    def __init__(self, fulls):
        n = len(fulls)
        self.inputs = list(fulls)
        self.out_shapes = [jax.ShapeDtypeStruct(a.shape, a.dtype) for a in fulls]
        self.aliases = {a: a for a in range(n)}
        self.sems = [pltpu.SemaphoreType.DMA((6 * n,)), pltpu.SemaphoreType.DMA((6 * n,))]
        self.block_rows = [a.shape[0] // N_CHIPS for a in fulls]

    def _ctx(self, outs, sems):
        send_sems, recv_sems = sems
        x, y, c, chips = _place()

        def rows(a, k, half):
            r = self.block_rows[a]
            return outs[a].at[pl.ds(k * r + half * (r // 2), r // 2)]

        def copy(a, j, blk, to):
            return pltpu.make_async_remote_copy(src_ref=blk, dst_ref=blk, send_sem=send_sems.at[a * 6 + j],
                                                recv_sem=recv_sems.at[a * 6 + j], device_id=to, device_id_type=MESH)

        return x, y, c, chips, rows, copy

    def start(self, ins, outs, sems, peers=(0, 1, 2)):
        x, y, c, chips, rows, copy = self._ctx(outs, sems)
        for j in peers:
            for a in range(len(outs)):
                copy(a, j, rows(a, 2 * x + y, c), (*chips[j], c)).start()

    def mid(self, ins, outs, sems, peers=(0, 1, 2)):
        x, y, c, chips, rows, copy = self._ctx(outs, sems)
        for j in peers:
            px, py = chips[j]
            for a in range(len(outs)):
                copy(a, j, rows(a, 2 * px + py, c), (px, py, c)).wait_recv()
                copy(a, 3 + j, rows(a, 2 * px + py, c), (x, y, 1 - c)).start()

    def wait_forwarded(self, outs, sems, peers=(0, 1, 2)):
        x, y, c, chips, rows, copy = self._ctx(outs, sems)
        for j in peers:
            px, py = chips[j]
            for a in range(len(outs)):
                copy(a, 3 + j, rows(a, 2 * px + py, 1 - c), (x, y, 1 - c)).wait_recv()

    def wait_sends(self, outs, sems):
        x, y, c, chips, rows, copy = self._ctx(outs, sems)
        for j, (px, py) in enumerate(chips):
            for a in range(len(outs)):
                copy(a, j, rows(a, 2 * x + y, c), (px, py, c)).wait_send()
                copy(a, 3 + j, rows(a, 2 * px + py, c), (x, y, 1 - c)).wait_send()

    def end(self, ins, outs, sems):
        self.wait_forwarded(outs, sems)
        self.wait_sends(outs, sems)


def _swap_halves(grads, name):
    n = len(grads)

    def body(*refs):
        ins, outs = refs[:n], refs[n:2 * n]
        send_sems, recv_sems = refs[2 * n:]
        x, y, c, _ = _place()
        cps = [pltpu.make_async_remote_copy(src_ref=ins[a].at[:, 1 - c], dst_ref=outs[a], send_sem=send_sems.at[a],
                                            recv_sem=recv_sems.at[a], device_id=(x, y, 1 - c), device_id_type=MESH)
               for a in range(n)]
        for cp in cps:
            cp.start()
        for cp in cps:
            cp.wait_recv()
        for cp in cps:
            cp.wait_send()

    outs = tuple(jax.ShapeDtypeStruct((g.shape[0],) + g.shape[2:], g.dtype) for g in grads)
    return tuple(pl.pallas_call(
        body, name=name, out_shape=outs, in_specs=_hbm_specs(n), out_specs=tuple(_hbm_specs(n)),
        scratch_shapes=[pltpu.SemaphoreType.DMA((n,)), pltpu.SemaphoreType.DMA((n,))],
    )(*grads))


def _pair_sum(g, recv, place, name):
    _, _, h, cc = g.shape
    th = h

    def body(c_ref, g_ref, r_ref, o_ref):
        o_ref[...] = (g_ref[...].astype(F32) + r_ref[...].astype(F32)).astype(o_ref.dtype)

    return pl.pallas_call(
        body, name=name, out_shape=jax.ShapeDtypeStruct(recv.shape, recv.dtype),
        grid_spec=pltpu.PrefetchScalarGridSpec(
            num_scalar_prefetch=1, grid=(N_CHIPS, h // th),
            in_specs=[pl.BlockSpec((None, None, th, cc), lambda k, r, c_ref: (k, c_ref[0], r, 0)),
                      pl.BlockSpec((None, th, cc), lambda k, r, c_ref: (k, r, 0))],
            out_specs=pl.BlockSpec((None, th, cc), lambda k, r, c_ref: (k, r, 0))),
        compiler_params=_cp(("arbitrary", "arbitrary")),
    )(place, g, recv)


def _finish_reduce(pack, halves):
    rows, cc = pack.shape
    hs = rows // 2
    n = len(halves)

    def body(*refs):
        pack_ref = refs[0]
        out_ref = refs[1 + n]
        big = refs[2 + n:2 + 2 * n]
        sib_ref, parts_ref, send_sems, recv_sems, big_send, big_recv = refs[2 + 2 * n:]
        x, y, c, chips = _place()
        me_k = 2 * x + y
        sibling = (x, y, 1 - c)
        mine = pl.ds(pl.multiple_of(c * hs, hs), hs)
        theirs = pl.ds(pl.multiple_of((1 - c) * hs, hs), hs)
        shared = [pltpu.make_async_remote_copy(src_ref=big[a].at[c], dst_ref=big[a].at[c], send_sem=big_send.at[a],
                                               recv_sem=big_recv.at[a], device_id=sibling, device_id_type=MESH)
                  for a in range(n)]
        first = pltpu.make_async_remote_copy(src_ref=pack_ref, dst_ref=sib_ref, send_sem=send_sems.at[0],
                                             recv_sem=recv_sems.at[0], device_id=sibling, device_id_type=MESH)
        first.start()
        first.wait()
        parts_ref[me_k] = pack_ref[mine, :] + sib_ref[mine, :]
        cps = [pltpu.make_async_remote_copy(src_ref=parts_ref.at[me_k], dst_ref=parts_ref.at[me_k],
                                            send_sem=send_sems.at[1 + j], recv_sem=recv_sems.at[1 + j],
                                            device_id=(px, py, c), device_id_type=MESH)
               for j, (px, py) in enumerate(chips)]
        for cp in cps:
            cp.start()
        for cp in shared:
            cp.start()
        for j, (px, py) in enumerate(chips):
            pltpu.make_async_remote_copy(src_ref=parts_ref.at[2 * px + py], dst_ref=parts_ref.at[2 * px + py],
                                         send_sem=send_sems.at[1 + j], recv_sem=recv_sems.at[1 + j],
                                         device_id=(px, py, c), device_id_type=MESH).wait_recv()
        for cp in cps:
            cp.wait_send()
        out_ref[mine, :] = ((parts_ref[0] + parts_ref[1]) + parts_ref[2]) + parts_ref[3]
        last = pltpu.make_async_remote_copy(src_ref=out_ref.at[mine], dst_ref=out_ref.at[mine],
                                            send_sem=send_sems.at[4], recv_sem=recv_sems.at[4], device_id=sibling,
                                            device_id_type=MESH)
        last.start()
        pltpu.make_async_remote_copy(src_ref=out_ref.at[theirs], dst_ref=out_ref.at[theirs],
                                     send_sem=send_sems.at[4], recv_sem=recv_sems.at[4], device_id=sibling,
                                     device_id_type=MESH).wait_recv()
        last.wait_send()
        for a in range(n):
            pltpu.make_async_remote_copy(src_ref=big[a].at[1 - c], dst_ref=big[a].at[1 - c], send_sem=big_send.at[a],
                                         recv_sem=big_recv.at[a], device_id=sibling,
                                         device_id_type=MESH).wait_recv()
        for cp in shared:
            cp.wait_send()

    vmem = pl.BlockSpec(memory_space=pltpu.VMEM)
    res = pl.pallas_call(
        body, name="finish_reduce",
        out_shape=(jax.ShapeDtypeStruct(pack.shape, pack.dtype),)
        + tuple(jax.ShapeDtypeStruct(g.shape, g.dtype) for g in halves),
        in_specs=[vmem] + _hbm_specs(n), out_specs=(vmem,) + tuple(_hbm_specs(n)),
        input_output_aliases={1 + a: 1 + a for a in range(n)},
        scratch_shapes=[pltpu.VMEM((rows, cc), F32), pltpu.VMEM((N_CHIPS, hs, cc), F32),
                        pltpu.SemaphoreType.DMA((5,)), pltpu.SemaphoreType.DMA((5,)),
                        pltpu.SemaphoreType.DMA((n,)), pltpu.SemaphoreType.DMA((n,))],
        compiler_params=_cp(),
    )(pack, *halves)
    return res[0], tuple(res[1:])


class _ExchangeRider:
    has_mid = False

    def __init__(self, sums):
        self.inputs = list(sums)
        self.out_shapes = [jax.ShapeDtypeStruct((3,) + g.shape[1:], g.dtype) for g in sums]
        m = len(self.inputs)
        self.aliases = {}
        self.sems = [pltpu.SemaphoreType.DMA((3 * m,)), pltpu.SemaphoreType.DMA((3 * m,))]

    def _copies(self, ins, outs, sems):
        send_sems, recv_sems = sems
        _, _, c, chips = _place()
        return [pltpu.make_async_remote_copy(
            src_ref=ins[a].at[2 * px + py], dst_ref=outs[a].at[j], send_sem=send_sems.at[a * 3 + j],
            recv_sem=recv_sems.at[a * 3 + j], device_id=(px, py, c), device_id_type=MESH)
            for j, (px, py) in enumerate(chips) for a in range(len(ins))]

    def start(self, ins, outs, sems):
        for cp in self._copies(ins, outs, sems):
            cp.start()

    def end(self, ins, outs, sems):
        cps = self._copies(ins, outs, sems)
        for cp in cps:
            cp.wait_recv()
        for cp in cps:
            cp.wait_send()


class _ShareRider:
    has_mid = False

    def __init__(self, halves):
        n = len(halves)
        self.inputs = list(halves)
        self.out_shapes = [jax.ShapeDtypeStruct(g.shape, g.dtype) for g in halves]
        self.aliases = {a: a for a in range(n)}
        self.sems = [pltpu.SemaphoreType.DMA((n,)), pltpu.SemaphoreType.DMA((n,))]

    def _copies(self, outs, sems, half):
        send_sems, recv_sems = sems
        x, y, c, _ = _place()
        h = c if half == "mine" else 1 - c
        return [pltpu.make_async_remote_copy(src_ref=outs[a].at[h], dst_ref=outs[a].at[h], send_sem=send_sems.at[a],
                                             recv_sem=recv_sems.at[a], device_id=(x, y, 1 - c), device_id_type=MESH)
                for a in range(len(outs))]

    def start(self, ins, outs, sems):
        for cp in self._copies(outs, sems, "mine"):
            cp.start()

    def end(self, ins, outs, sems):
        for cp in self._copies(outs, sems, "theirs"):
            cp.wait_recv()
        for cp in self._copies(outs, sems, "mine"):
            cp.wait_send()


class _Riders:
    def __init__(self, riders):
        self.riders = list(riders)
        self.inputs = [a for r in self.riders for a in r.inputs]
        self.out_shapes = [s for r in self.riders for s in r.out_shapes]
        self.sems = [s for r in self.riders for s in r.sems]
        self.has_mid = any(r.has_mid for r in self.riders)
        self.aliases = {}
        i0 = o0 = 0
        for r in self.riders:
            self.aliases.update({i0 + j: o0 + k for j, k in r.aliases.items()})
            i0 += len(r.inputs)
            o0 += len(r.out_shapes)

    def _each(self, ins, outs, sems):
        i0 = o0 = s0 = 0
        for r in self.riders:
            yield (r, ins[i0:i0 + len(r.inputs)], outs[o0:o0 + len(r.out_shapes)], sems[s0:s0 + len(r.sems)])
            i0, o0, s0 = i0 + len(r.inputs), o0 + len(r.out_shapes), s0 + len(r.sems)

    def start(self, ins, outs, sems):
        for r, i, o, s in self._each(ins, outs, sems):
            r.start(i, o, s)

    def mid(self, ins, outs, sems):
        for r, i, o, s in self._each(ins, outs, sems):
            if r.has_mid:
                r.mid(i, o, s)

    def end(self, ins, outs, sems):
        for r, i, o, s in self._each(ins, outs, sems):
            r.end(i, o, s)

    def split(self, outs):
        res, o0 = [], 0
        for r in self.riders:
            res.append(tuple(outs[o0:o0 + len(r.out_shapes)]))
            o0 += len(r.out_shapes)
        return res


def _chip_sum(own, parts, place, name):
    npart, h, cc = parts.shape
    th = _row_tile(h)

    def body(place_ref, own_ref, p_ref, o_ref):
        acc = own_ref[...].astype(F32) + p_ref[0].astype(F32)
        for k in range(1, npart):
            acc = acc + p_ref[k].astype(F32)
        o_ref[...] = acc

    return pl.pallas_call(
        body, name=name, out_shape=jax.ShapeDtypeStruct((2, h, cc), F32),
        grid_spec=pltpu.PrefetchScalarGridSpec(
            num_scalar_prefetch=1, grid=(h // th,),
            in_specs=[pl.BlockSpec((None, th, cc), lambda r, pr: (pr[1], r, 0)),
                      pl.BlockSpec((npart, th, cc), lambda r, pr: (0, r, 0))],
            out_specs=pl.BlockSpec((None, th, cc), lambda r, pr: (pr[0], r, 0))),
        compiler_params=_cp(("arbitrary",)),
    )(place, own, parts)


def _adamw_math(w, g, m, v):
    m = ADAM_B1 * m + (1.0 - ADAM_B1) * g
    v = ADAM_B2 * v + (1.0 - ADAM_B2) * (g * g)
    m_hat = m / (1.0 - ADAM_B1 ** ADAM_STEP)
    v_hat = v / (1.0 - ADAM_B2 ** ADAM_STEP)
    delta = -ADAM_LR * (m_hat / (jnp.sqrt(v_hat) + ADAM_EPS) + ADAM_WD * w)
    return delta, m, v


def _adamw_big(w, g, m, v, name):
    r, cc = w.shape
    tr = min(_row_tile(r), 256) if r % 256 == 0 else _row_tile(r)

    def body(w_ref, g_ref, m_ref, v_ref, go_ref, d_ref, mo_ref, vo_ref):
        g = g_ref[...]
        d, mm, vv = _adamw_math(w_ref[...], g, m_ref[...], v_ref[...])
        go_ref[...] = g
        d_ref[...] = d
        mo_ref[...] = mm
        vo_ref[...] = vv

    blk = pl.BlockSpec((tr, cc), lambda i: (i, 0))
    sd = jax.ShapeDtypeStruct((r, cc), F32)
    return pl.pallas_call(body, grid=(r // tr,), name=name, out_shape=(sd, sd, sd, sd), in_specs=[blk] * 4,
                          out_specs=(blk, blk, blk, blk), compiler_params=_cp(("arbitrary",)))(w, g, m, v)


SC_TILES = 32


def _adamw_sparsecore(ws, gs, ms, vs, name):
    n = len(ws)
    rows_per = 8
    widths = sorted({a.shape[1] for a in ws})
    assert all(a.shape[0] % rows_per == 0 and a.shape[1] % 16 == 0 for a in ws)

    def body(*refs):
        ins, outs, bufs = refs[:4 * n], refs[4 * n:8 * n], refs[8 * n:]
        tile = lax.axis_index("sc_tile") * 2 + lax.axis_index("sc_core")
        for a in range(n):
            w_hbm, g_hbm, m_hbm, v_hbm = ins[4 * a:4 * a + 4]
            go_hbm, d_hbm, mo_hbm, vo_hbm = outs[4 * a:4 * a + 4]
            r, cc = ws[a].shape
            k = widths.index(cc)
            wb, gb, mb, vb, db = bufs[5 * k:5 * k + 5]
            groups = r // rows_per

            @pl.loop(0, -(-groups // SC_TILES))
            def _(q):
                grp = tile + q * SC_TILES

                @pl.when(grp < groups)
                def _():
                    rows = pl.ds(pl.multiple_of(grp * rows_per, rows_per), rows_per)
                    pltpu.sync_copy(w_hbm.at[rows], wb)
                    pltpu.sync_copy(g_hbm.at[rows], gb)
                    pltpu.sync_copy(m_hbm.at[rows], mb)
                    pltpu.sync_copy(v_hbm.at[rows], vb)

                    @pl.loop(0, rows_per)
                    def _(i):
                        @pl.loop(0, cc, step=16)
                        def _(j):
                            at = (i, pl.ds(j, 16))
                            d, mm, vv = _adamw_math(wb[at], gb[at], mb[at], vb[at])
                            db[at] = d
                            mb[at] = mm
                            vb[at] = vv

                    pltpu.sync_copy(gb, go_hbm.at[rows])
                    pltpu.sync_copy(db, d_hbm.at[rows])
                    pltpu.sync_copy(mb, mo_hbm.at[rows])
                    pltpu.sync_copy(vb, vo_hbm.at[rows])

    args, out_type = [], []
    for a in range(n):
        args += [ws[a], gs[a], ms[a], vs[a]]
        out_type += [jax.ShapeDtypeStruct(ws[a].shape, F32)] * 4
    res = pl.kernel(
        body, name=name, out_type=tuple(out_type),
        mesh=plsc.VectorSubcoreMesh(core_axis_name="sc_core", subcore_axis_name="sc_tile"),
        scratch_types=[pltpu.VMEM((rows_per, cc), F32) for cc in widths for _ in range(5)],
    )(*args)
    return [tuple(res[4 * a:4 * a + 4]) for a in range(n)]


def _adamw_small(ws, gs, ms, vs):
    n = len(ws)

    def body(*refs):
        for a in range(n):
            w_ref, g_ref, m_ref, v_ref = refs[4 * a:4 * a + 4]
            d_ref, mo_ref, vo_ref = refs[4 * n + 3 * a:4 * n + 3 * a + 3]
            d, mm, vv = _adamw_math(w_ref[...], g_ref[...], m_ref[...], v_ref[...])
            d_ref[...] = d
            mo_ref[...] = mm
            vo_ref[...] = vv

    args, outs = [], []
    for a in range(n):
        args += [ws[a], gs[a], ms[a], vs[a]]
        outs += [jax.ShapeDtypeStruct(ws[a].shape, F32)] * 3
    res = pl.pallas_call(body, name="adamw_small", out_shape=tuple(outs), compiler_params=_cp())(*args)
    return [res[3 * a:3 * a + 3] for a in range(n)]


def _flat_pack(arrs, rows):
    flat = jnp.concatenate([a.reshape(-1) for a in arrs])
    return jnp.pad(flat, (0, rows * D - flat.shape[0])).reshape(rows, D)


def _flat_unpack(flat, shapes):
    out, off = [], 0
    for shp in shapes:
        size = 1
        for d_ in shp:
            size *= d_
        out.append(flat[off:off + size].reshape(shp))
        off += size
    return out


SMALL_EVEN = ("even_pre_g", "even_a_ln_g", "even_a_ln_b", "even_a_ws", "even_a_bs", "even_b_conv", "even_mem_g",
              "even_post_g")
SMALL_ODD = ("odd_pre_g", "odd_c_wgrp", "odd_c_scale", "odd_d_dw_w", "odd_d_dw_b", "odd_d_ln_g", "odd_d_ln_b",
             "odd_d_pw_b", "odd_mem_g", "odd_post_g")
BIG = ("even_w_in", "even_w_kv", "even_w_out", "odd_w_in", "odd_d_pw_w", "odd_w_kv", "odd_w_out")
WEIGHTS = ("even_pre_g", "even_w_in", "even_a_ln_g", "even_a_ln_b", "even_a_ws", "even_a_bs", "even_b_conv",
           "even_mem_g", "even_w_kv", "even_w_out", "even_post_g", "odd_pre_g", "odd_w_in", "odd_c_wgrp",
           "odd_c_scale", "odd_d_dw_w", "odd_d_dw_b", "odd_d_ln_g", "odd_d_ln_b", "odd_d_pw_w", "odd_d_pw_b",
           "odd_mem_g", "odd_w_kv", "odd_w_out", "odd_post_g")
PACKED = (("even_b_conv", (3, 192)), ("odd_pre_g", (1, 256)), ("odd_c_scale", (1, 192)), ("odd_d_dw_w", (31, 192)),
          ("odd_d_dw_b", (1, 192)), ("odd_d_ln_g", (1, 192)), ("odd_d_ln_b", (1, 192)), ("odd_d_pw_b", (1, 192)),
          ("odd_mem_g", (1, 256)), ("odd_post_g", (1, 256)))
PACK_ROWS = 16
SMALL_ROWS = 256


def _four(g):
    return g.reshape(N_CHIPS, 2, g.shape[0] // (2 * N_CHIPS), g.shape[1])


def _step(x, mem, target, w, mom, var, place):
    wt = {}
    pack = _flat_pack([w[n][0] for n, _ in PACKED], PACK_ROWS)
    shards = {"even_w_in_t": w["even_w_in"][0].T, "odd_w_in_t": w["odd_w_in"][0].T, "even_w_kv": w["even_w_kv"][0],
              "odd_w_kv": w["odd_w_kv"][0], "even_w_out": w["even_w_out"][0], "odd_w_out": w["odd_w_out"][0],
              "odd_d_pw_w": w["odd_d_pw_w"][0]}
    placed = {n: _place_shard(shards[n], place, BF16, "place_" + n) for n in ("even_w_in_t", "even_w_kv", "even_w_out")}
    placed["pack"] = _place_shard(pack, place, F32, "place_pack")

    order, group = _stream_tables(place[0], place[1], EVEN_IN)
    p_e, h_e, (wt["even_w_in_t"], packs), (wt["even_w_kv"], wt["even_w_out"]) = _in_fwd_streamed(
        x, w["even_pre_g"], [placed["even_w_in_t"], placed["pack"]], [placed["even_w_kv"], placed["even_w_out"]],
        order, group, "even_in_streamed")
    for n in ("odd_w_in_t", "odd_w_kv", "odd_w_out", "odd_d_pw_w"):
        placed[n] = _place_shard(shards[n], place, BF16, "place_" + n, after=p_e[0:16, 0:128])
    packs = packs.reshape(N_CHIPS, PACK_ROWS * D)
    per_chip = [_flat_unpack(packs[k], [shp for _, shp in PACKED]) for k in range(N_CHIPS)]
    for a, (name, _) in enumerate(PACKED):
        wt[name] = jnp.concatenate([per_chip[k][a] for k in range(N_CHIPS)], axis=-1)
    for name in ("even_pre_g", "even_a_ln_g", "even_a_ln_b", "even_mem_g", "even_post_g"):
        wt[name] = w[name]

    tril = jnp.tril(jnp.ones((CH, CH), dtype=bool))
    wcat = jnp.where(tril[None], w["even_a_ws"][0], 0.0).transpose(1, 0, 2).reshape(CH, 4 * CH).astype(BF16)
    bsg = jnp.repeat(w["even_a_bs"][0].T, BW // 4, axis=1)
    hsel = (jnp.arange(BW)[:, None] // (BW // 4) == jnp.arange(128)[None, :]).astype(BF16)
    g4 = BW // 4
    eye = jnp.eye(4, dtype=F32)
    wbd = (w["odd_c_wgrp"][0][:, :, None, :] * eye[:, None, :, None]).reshape(BW, BW).astype(BF16)

    kv_e = _kv_fwd(mem, wt["even_mem_g"], wt["even_w_kv"], "even_kv")
    (x1, o_e, y_e), (wt["odd_w_in_t"],) = _even_fwd(
        x, p_e, kv_e, wt["even_a_ln_g"], wt["even_a_ln_b"], wcat, bsg, wt["even_b_conv"], wt["even_w_out"],
        wt["even_post_g"], rider=_GatherRider([placed["odd_w_in_t"]]))
    names = ("odd_w_out", "odd_d_pw_w", "odd_w_kv")
    (p_o, h_o), got = _in_fwd(x1, wt["odd_pre_g"], wt["odd_w_in_t"], "odd_in",
                              rider=_GatherRider([placed[n] for n in names]))
    wt.update(zip(names, got))
    kv_o = _kv_fwd(mem, wt["odd_mem_g"], wt["odd_w_kv"], "odd_kv")
    dx2, o_o, cv_o, loss = _odd_fwd(x1, p_o, kv_o, wbd, wt["odd_c_scale"], wt["odd_d_dw_w"], wt["odd_d_dw_b"],
                                    wt["odd_d_ln_g"], wt["odd_d_ln_b"], wt["odd_d_pw_w"], wt["odd_d_pw_b"],
                                    wt["odd_w_out"], wt["odd_post_g"], target)
    (dpc_o, tmpc, tmpd, do_o, y_o, g_post_o, g_cs, g_wbd, g_dww, g_dwb, g_lng_o, g_lnb_o, g_pww, g_pwb,
     dkv_o) = _odd_bwd1(dx2, o_o, cv_o, p_o, kv_o, wbd, wt["odd_c_scale"], wt["odd_d_dw_w"], wt["odd_d_dw_b"],
                        wt["odd_d_ln_g"], wt["odd_d_ln_b"], wt["odd_d_pw_w"], wt["odd_d_pw_b"], wt["odd_w_out"],
                        wt["odd_post_g"])
    dpb_o, dx1, g_pre_o = _odd_bwd2(dpc_o, tmpc, tmpd, p_o, wt["odd_d_dw_w"], wt["odd_w_in_t"], x1,
                                    wt["odd_pre_g"], dx2)
    g_win_o = _grad_tn(dpb_o, h_o, 768, rows=ODD_IN, name="odd_gw_in_b")
    g_win_o = _grad_tn(dpc_o, h_o, 1280, out=g_win_o, rows=ODD_IN, row0=3 * BW, name="odd_gw_in_c")
    g_wout_o = _grad_tn(y_o, do_o, 1024, name="odd_gw_out")
    g_wkv_o, g_memg_o = _kv_bwd(mem, wt["odd_mem_g"], wt["odd_w_kv"], dkv_o, "odd_kv_bwd")
    big_o = [_four(g) for g in (g_win_o, g_pww.astype(BF16), g_wkv_o, g_wout_o)]
    recv_o = _swap_halves(big_o, "swap_halves_odd")
    sums_o = [_pair_sum(big_o[a], recv_o[a], place, "pair_sum_odd_%d" % a) for a in range(len(big_o))]
    (dp_e, do_e, g_post_e, g_lng_e, g_lnb_e, g_wcat, g_bs, g_bconv,
     dkv_e), parts_o = _even_bwd1(dx1, o_e, p_e, kv_e, wt["even_a_ln_g"], wt["even_a_ln_b"], wcat, bsg, hsel,
                                  wt["even_b_conv"], wt["even_w_out"], wt["even_post_g"],
                                  rider=_ExchangeRider(sums_o))
    halves_o = [_chip_sum(sums_o[a], parts_o[a], place, "chip_sum_odd_%d" % a) for a in range(len(big_o))]
    g_wout_e = _grad_tn(y_e, do_e, 1024, name="even_gw_out")
    g_wkv_e, g_memg_e = _kv_bwd(mem, wt["even_mem_g"], wt["even_w_kv"], dkv_e, "even_kv_bwd")
    big_x = [_four(g) for g in (g_wkv_e, g_wout_e)]
    recv_x = _swap_halves(big_x, "swap_halves_kv_out")
    sums_x = [_pair_sum(big_x[a], recv_x[a], place, "pair_sum_kv_out_%d" % a) for a in range(len(big_x))]
    riders = _Riders([_ExchangeRider(sums_x), _ShareRider(halves_o)])
    g_win_e, got = _grad_tn(dp_e, h_e, 1280, name="even_gw_in", rider=riders)
    parts_x, full_o = riders.split(got)

    def sparsecore_adamw(names, fulls, name):
        as_kept = [(lambda t: t.T) if n.endswith("w_in") else (lambda t: t) for n in names]
        res = _adamw_sparsecore([f(w[n][0]) for f, n in zip(as_kept, names)],
                                [g_.reshape(g_.shape[1] * 2, g_.shape[2]) for g_ in fulls],
                                [f(mom[n][0]) for f, n in zip(as_kept, names)],
                                [f(var[n][0]) for f, n in zip(as_kept, names)], name)
        return {n: tuple(f(t) for t in r_) for f, n, r_ in zip(as_kept, names, res)}

    upd_sc = sparsecore_adamw(("odd_w_in", "odd_d_pw_w", "odd_w_kv", "odd_w_out"), full_o, "adamw_odd_sparsecore")
    halves_x = [_chip_sum(sums_x[a], parts_x[a], place, "chip_sum_kv_out_%d" % a) for a in range(len(big_x))]
    big_e = [_four(g_win_e)]
    recv_e = _swap_halves(big_e, "swap_halves_even")
    sums_e = [_pair_sum(big_e[0], recv_e[0], place, "pair_sum_even_w_in")]
    riders = _Riders([_ExchangeRider(sums_e), _ShareRider(halves_x)])
    (dx0, g_pre_e), got = _even_bwd2(dp_e, wt["even_w_in_t"], x, wt["even_pre_g"], dx1, rider=riders)
    parts_e, full_x = riders.split(got)
    upd_sc.update(sparsecore_adamw(("even_w_kv", "even_w_out"), full_x, "adamw_kv_out_sparsecore"))
    halves_e = [_chip_sum(sums_e[0], parts_e[0], place, "chip_sum_even_w_in")]

    g_aws = jnp.where(tril[None], g_wcat.reshape(CH, 4, CH).transpose(1, 0, 2), 0.0)
    g_wgrp = jnp.stack([lax.dynamic_slice(g_wbd, (g * g4, g * g4), (g4, g4)) for g in range(4)])
    small = {
        "even_pre_g": g_pre_e, "even_a_ln_g": g_lng_e, "even_a_ln_b": g_lnb_e, "even_a_ws": g_aws,
        "even_a_bs": g_bs[:, 0:4].T, "even_b_conv": g_bconv[0:3], "even_mem_g": g_memg_e, "even_post_g": g_post_e,
        "odd_pre_g": g_pre_o, "odd_c_wgrp": g_wgrp, "odd_c_scale": g_cs, "odd_d_dw_w": g_dww.reshape(CONF, 8, BW).sum(axis=1),
        "odd_d_dw_b": g_dwb, "odd_d_ln_g": g_lng_o, "odd_d_ln_b": g_lnb_o, "odd_d_pw_b": g_pwb,
        "odd_mem_g": g_memg_o, "odd_post_g": g_post_o,
    }
    small_names = SMALL_EVEN + SMALL_ODD
    small_pack = _flat_pack([small[n] for n in small_names] + [loss[0, 0].reshape(1)], SMALL_ROWS)
    small_total, full = _finish_reduce(small_pack, halves_e)
    gbig = {"even_w_in": full[0].reshape(full[0].shape[1] * 2, full[0].shape[2])}
    return dx0, gbig, upd_sc, small_total.reshape(-1), [small[n].shape for n in small_names]


def kernel(x, mem, even_pre_g, even_w_in, even_a_ln_g, even_a_ln_b, even_a_ws, even_a_bs, even_b_conv, even_mem_g, even_w_kv, even_w_out, even_post_g, odd_pre_g, odd_w_in, odd_c_wgrp, odd_c_scale, odd_d_dw_w, odd_d_dw_b, odd_d_ln_g, odd_d_ln_b, odd_d_pw_w, odd_d_pw_b, odd_mem_g, odd_w_kv, odd_w_out, odd_post_g, loss_target, m_even_pre_g, m_even_w_in, m_even_a_ln_g, m_even_a_ln_b, m_even_a_ws, m_even_a_bs, m_even_b_conv, m_even_mem_g, m_even_w_kv, m_even_w_out, m_even_post_g, m_odd_pre_g, m_odd_w_in, m_odd_c_wgrp, m_odd_c_scale, m_odd_d_dw_w, m_odd_d_dw_b, m_odd_d_ln_g, m_odd_d_ln_b, m_odd_d_pw_w, m_odd_d_pw_b, m_odd_mem_g, m_odd_w_kv, m_odd_w_out, m_odd_post_g, v_even_pre_g, v_even_w_in, v_even_a_ln_g, v_even_a_ln_b, v_even_a_ws, v_even_a_bs, v_even_b_conv, v_even_mem_g, v_even_w_kv, v_even_w_out, v_even_post_g, v_odd_pre_g, v_odd_w_in, v_odd_c_wgrp, v_odd_c_scale, v_odd_d_dw_w, v_odd_d_dw_b, v_odd_d_ln_g, v_odd_d_ln_b, v_odd_d_pw_w, v_odd_d_pw_b, v_odd_mem_g, v_odd_w_kv, v_odd_w_out, v_odd_post_g):
    given = dict(locals())
    w = {n: given[n] for n in WEIGHTS}
    mom = {n: given["m_" + n] for n in WEIGHTS}
    var = {n: given["v_" + n] for n in WEIGHTS}

    x_, y_, c_ = lax.axis_index("x"), lax.axis_index("y"), lax.axis_index("c")
    chip = 2 * x_ + y_
    place = jnp.stack([c_, chip]).astype(jnp.int32)
    grad_x, gbig, upd_odd, gsmall_flat, small_shapes = _step(x[0], mem[0], loss_target[0], w, mom, var, place)

    names = SMALL_EVEN + SMALL_ODD
    grads = {}
    unpacked = _flat_unpack(gsmall_flat, small_shapes + [(1,)])
    loss = unpacked[-1][0]
    for n, g in zip(names, unpacked[:-1]):
        shard_shape = w[n].shape[1:]
        if g.shape[-1] != shard_shape[-1]:
            g = lax.dynamic_slice_in_dim(g, chip * shard_shape[-1], shard_shape[-1], axis=g.ndim - 1)
        grads[n] = g.reshape(shard_shape)

    def two_d(a):
        return a.reshape(-1, a.shape[-1])

    upd = {}
    for n in BIG:
        if n in upd_odd:
            res = upd_odd[n]
        elif n.endswith("w_in"):
            res = _adamw_big(w[n][0].T, gbig[n], mom[n][0].T, var[n][0].T, "adamw_" + n)
            res = tuple(r.T for r in res)
        else:
            res = _adamw_big(w[n][0], gbig[n], mom[n][0], var[n][0], "adamw_" + n)
        grads[n], upd[n] = res[0], res[1:]
    res = _adamw_small([two_d(w[n][0]) for n in names], [two_d(grads[n]) for n in names],
                       [two_d(mom[n][0]) for n in names], [two_d(var[n][0]) for n in names])
    for n, r in zip(names, res):
        upd[n] = r

    outs = [loss, grad_x[None]]
    outs += [grads[n].reshape(w[n].shape) for n in WEIGHTS]
    for j in range(3):
        outs += [upd[n][j].reshape(w[n].shape) for n in WEIGHTS]
    return tuple(outs)
```

```python
import jax
import jax.numpy as jnp
from jax import lax
from jax.experimental import pallas as pl
from jax.experimental.pallas import tpu as pltpu
from jax.experimental.pallas import tpu_sc as plsc

F32 = jnp.float32
BF16 = jnp.bfloat16
MESH = pl.DeviceIdType.MESH

D = 1024
N_MEM = 256
MIX = 2048
XA = 512
HD = 128
BW = 768
CH = 128
EPS = 1e-6
SCALE = HD ** -0.5
POOL_WINDOWS = (2, 4, 8, 16)
CONF = 31
EVEN_IN = 6400
ODD_IN = 4864
N_CHIPS = 4

ADAM_LR = 0.001
ADAM_B1 = 0.9
ADAM_B2 = 0.999
ADAM_EPS = 1e-08
ADAM_WD = 0.01
ADAM_STEP = 10

TS = 256
HALO = 32
VMEM_LIMIT = 56 * 1024 * 1024


def _cp(sem=None):
    return pltpu.CompilerParams(dimension_semantics=sem, vmem_limit_bytes=VMEM_LIMIT)


def _dot(a, b):
    return jnp.dot(a, b, preferred_element_type=F32)


def _dot_nt(a, b):
    return lax.dot_general(a, b, (((1,), (1,)), ((), ())), preferred_element_type=F32)


def _dot_tn(a, b):
    return lax.dot_general(a, b, (((0,), (0,)), ((), ())), preferred_element_type=F32)


def _sigmoid(x):
    return 1.0 / (1.0 + jnp.exp(-x))


def _resident(shape):
    return pl.BlockSpec(shape, lambda *_: (0,) * len(shape), pipeline_mode=pl.Buffered(1))


def _const(shape):
    return pl.BlockSpec(shape, lambda *_: (0,) * len(shape))


def _kv_fwd(mem, mem_g, wkv, name):
    def body(mem_ref, g_ref, w_ref, kv_ref):
        m = mem_ref[...]
        r = lax.rsqrt(jnp.mean(m * m, axis=-1, keepdims=True) + EPS)
        mn = (m * r * g_ref[...]).astype(BF16)
        kv_ref[...] = _dot(mn, w_ref[...]).astype(BF16)

    return pl.pallas_call(body, out_shape=jax.ShapeDtypeStruct((N_MEM, D), BF16), name=name,
                          compiler_params=_cp())(mem, mem_g, wkv)


def _kv_bwd(mem, mem_g, wkv, dkv, name):
    def body(mem_ref, g_ref, w_ref, dkv_ref, dw_ref, dg_ref):
        m = mem_ref[...]
        r = lax.rsqrt(jnp.mean(m * m, axis=-1, keepdims=True) + EPS)
        mh = m * r
        mn = (mh * g_ref[...]).astype(BF16)
        dkv = dkv_ref[...].astype(BF16)
        dw_ref[...] = _dot_tn(mn, dkv).astype(BF16)
        dmn = _dot_nt(dkv, w_ref[...])
        dg_ref[...] = jnp.sum(dmn * mh, axis=0, keepdims=True)

    return pl.pallas_call(body, out_shape=(jax.ShapeDtypeStruct((D, D), BF16), jax.ShapeDtypeStruct((1, D), F32)),
                          name=name, compiler_params=_cp())(mem, mem_g, wkv, dkv)


def _host_call(body, *, grid, name, out_shape, in_specs, out_specs, args, scratch_shapes=(), aliases=None,
               rider=None):
    sem = ("arbitrary",) * len(grid)
    aliases = dict(aliases or {})
    if rider is None:
        res = pl.pallas_call(body, grid=grid, name=name, out_shape=tuple(out_shape), in_specs=list(in_specs),
                             out_specs=tuple(out_specs), scratch_shapes=list(scratch_shapes),
                             input_output_aliases=aliases, compiler_params=_cp(sem))(*args)
        return tuple(res), ()
    n_in, n_out, n_sc = len(in_specs), len(out_specs), len(scratch_shapes)
    r_in, r_out = len(rider.inputs), len(rider.out_shapes)

    def full_body(*refs):
        host_in = refs[:n_in]
        rid_in = refs[n_in:n_in + r_in]
        host_out = refs[n_in + r_in:n_in + r_in + n_out]
        rid_out = refs[n_in + r_in + n_out:n_in + r_in + n_out + r_out]
        host_sc = refs[n_in + r_in + n_out + r_out:n_in + r_in + n_out + r_out + n_sc]
        sems = refs[n_in + r_in + n_out + r_out + n_sc:]
        first = pl.program_id(0) == 0
        last = pl.program_id(0) == grid[0] - 1
        for ax in range(1, len(grid)):
            first = jnp.logical_and(first, pl.program_id(ax) == 0)
            last = jnp.logical_and(last, pl.program_id(ax) == grid[ax] - 1)

        @pl.when(first)
        def _():
            rider.start(rid_in, rid_out, sems)

        if rider.has_mid:
            @pl.when(last)
            def _():
                rider.mid(rid_in, rid_out, sems)

        body(*host_in, *host_out, *host_sc)

        @pl.when(last)
        def _():
            rider.end(rid_in, rid_out, sems)

    aliases.update({n_in + j: n_out + k for j, k in rider.aliases.items()})
    res = pl.pallas_call(
        full_body, grid=grid, name=name, out_shape=tuple(out_shape) + tuple(rider.out_shapes),
        in_specs=list(in_specs) + _hbm_specs(r_in), out_specs=tuple(out_specs) + tuple(_hbm_specs(r_out)),
        scratch_shapes=list(scratch_shapes) + list(rider.sems), input_output_aliases=aliases,
        compiler_params=_cp(sem),
    )(*args, *rider.inputs)
    return tuple(res[:n_out]), tuple(res[n_out:])


def _in_fwd(x, pre_g, w_t, name, rider=None):
    s, n = x.shape[0], w_t.shape[0]
    tm = min(512, s)
    nc = 256

    def body(x_ref, g_ref, w_ref, p_ref, h_ref):
        xv = x_ref[...]
        r = lax.rsqrt(jnp.mean(xv * xv, axis=-1, keepdims=True) + EPS)
        h = (xv * r * g_ref[...]).astype(BF16)
        h_ref[...] = h
        for j in range(n // nc):
            p_ref[:, j * nc:(j + 1) * nc] = _dot_nt(h, w_ref[j * nc:(j + 1) * nc, :]).astype(BF16)

    return _host_call(
        body, grid=(s // tm,), name=name, rider=rider,
        out_shape=(jax.ShapeDtypeStruct((s, n), BF16), jax.ShapeDtypeStruct((s, D), BF16)),
        in_specs=[pl.BlockSpec((tm, D), lambda i: (i, 0)), _const((1, D)), _resident((n, D))],
        out_specs=(pl.BlockSpec((tm, n), lambda i: (i, 0)), pl.BlockSpec((tm, D), lambda i: (i, 0))),
        args=(x, pre_g, w_t))


NC = 256


def _stream_tables(core, chip, n):
    nchunk = n // NC
    idx = jnp.arange(nchunk, dtype=jnp.int32)
    src = jnp.array([0, 2, 1, 3], jnp.int32)
    r = n // N_CHIPS

    def group_of(row):
        j = src[(row // r) ^ chip]
        through_sibling = ((row % r) // (r // 2) != core).astype(jnp.int32)
        return jnp.where(j == 0, 0, 2 * j - 1 + through_sibling)

    grp = jnp.maximum(group_of(idx * NC), group_of(idx * NC + NC - 1))
    order = jnp.argsort(grp * 64 + idx).astype(jnp.int32)
    return order, grp[order]


def _in_fwd_streamed(x, pre_g, first, later, order, group, name):
    s, n = x.shape[0], first[0].shape[0]
    nchunk = n // NC
    rider = _GatherRider(first)
    rider2 = _GatherRider(later) if later else None
    a, m = len(first), len(later)
    tr = min(256, s)

    def body(*refs):
        order_ref, group_ref, x_ref, g_ref = refs[0:4]
        p_ref, h_ref = refs[4 + a + m:6 + a + m]
        outs = refs[6 + a + m:6 + 2 * a + m]
        outs2 = refs[6 + 2 * a + m:6 + 2 * a + 2 * m]
        wbuf, wsem, send_sems, recv_sems = refs[6 + 2 * a + 2 * m:10 + 2 * a + 2 * m]
        sems2 = refs[10 + 2 * a + 2 * m:]
        w_hbm = outs[0]
        j = pl.program_id(0)
        sems = (send_sems, recv_sems)
        grp = group_ref[j]
        new_group = jnp.logical_or(j == 0, group_ref[jnp.maximum(j - 1, 0)] != grp)
        slot = j % 2

        def fetch(step, sl):
            rows = pl.ds(pl.multiple_of(order_ref[step] * NC, NC), NC)
            return pltpu.make_async_copy(w_hbm.at[rows], wbuf.at[sl], wsem.at[sl])

        @pl.when(j == 0)
        def _():
            rider.start(None, outs, sems, peers=(0, 1))

            @pl.loop(0, s // tr)
            def _(t):
                rows = pl.ds(pl.multiple_of(t * tr, tr), tr)
                xv = x_ref[rows, :]
                r = lax.rsqrt(jnp.mean(xv * xv, axis=-1, keepdims=True) + EPS)
                h_ref[rows, :] = (xv * r * g_ref[...]).astype(BF16)

        before = jnp.where(j == 0, 0, group_ref[jnp.maximum(j - 1, 0)])

        def entering(b):
            return jnp.logical_and(before < b, b <= grp)

        for src in range(3):
            @pl.when(entering(2 * src + 1))
            def _(src=src):
                if src == 0:
                    rider.start(None, outs, sems, peers=(2,))
                rider.mid(None, outs, sems, peers=(src,))
                if src == 1 and rider2 is not None:
                    rider2.start(None, outs2, sems2)

            @pl.when(entering(2 * src + 2))
            def _(src=src):
                rider.wait_forwarded(outs, sems, peers=(src,))

        @pl.when(new_group)
        def _():
            fetch(j, slot).start()

        fetch(j, slot).wait()
        nxt = jnp.minimum(j + 1, nchunk - 1)

        @pl.when(jnp.logical_and(j + 1 < nchunk, group_ref[nxt] == grp))
        def _():
            fetch(nxt, 1 - slot).start()

        p_ref[...] = _dot_nt(h_ref[...], wbuf[slot]).astype(BF16)

        @pl.when(j == nchunk - 1)
        def _():
            rider.wait_sends(outs, sems)
            if rider2 is not None:
                rider2.mid(None, outs2, sems2)
                rider2.end(None, outs2, sems2)

    hbm = pl.BlockSpec(memory_space=pltpu.HBM)
    arrs = list(first) + list(later)
    whole = pl.BlockSpec((s, D), lambda j, o, g: (0, 0), pipeline_mode=pl.Buffered(1))
    res = pl.pallas_call(
        body, name=name,
        out_shape=(jax.ShapeDtypeStruct((s, n), BF16), jax.ShapeDtypeStruct((s, D), BF16))
        + tuple(jax.ShapeDtypeStruct(v.shape, v.dtype) for v in arrs),
        grid_spec=pltpu.PrefetchScalarGridSpec(
            num_scalar_prefetch=2, grid=(nchunk,),
            in_specs=[whole, pl.BlockSpec((1, D), lambda j, o, g: (0, 0))] + [hbm] * (a + m),
            out_specs=(pl.BlockSpec((s, NC), lambda j, o, g: (0, o[j])),
                       pl.BlockSpec((s, D), lambda j, o, g: (0, 0))) + (hbm,) * (a + m),
            scratch_shapes=[pltpu.VMEM((2, NC, D), BF16), pltpu.SemaphoreType.DMA((2,))] + list(rider.sems)
            + (list(rider2.sems) if rider2 is not None else [])),
        input_output_aliases={4 + v: 2 + v for v in range(a + m)},
        compiler_params=_cp(("arbitrary",)),
    )(order, group, x, pre_g, *arrs)
    return res[0], res[1], tuple(res[2:2 + a]), tuple(res[2 + a:])


def _xattn_fwd(q, kv_ref):
    outs, probs = [], []
    for h in range(XA // HD):
        qh = q[:, h * HD:(h + 1) * HD]
        kh = kv_ref[:, h * HD:(h + 1) * HD]
        vh = kv_ref[:, XA + h * HD:XA + (h + 1) * HD]
        sc = _dot_nt(qh, kh) * SCALE
        e = jnp.exp(sc - jnp.max(sc, axis=-1, keepdims=True))
        pr = e / jnp.sum(e, axis=-1, keepdims=True)
        outs.append(_dot(pr.astype(BF16), vh))
        probs.append(pr)
    return jnp.concatenate(outs, axis=-1), probs


def _xattn_bwd(dyx, q, probs, kv_ref, dkv_ref):
    dqs = []
    for h in range(XA // HD):
        qh = q[:, h * HD:(h + 1) * HD]
        kh = kv_ref[:, h * HD:(h + 1) * HD]
        vh = kv_ref[:, XA + h * HD:XA + (h + 1) * HD]
        dy = dyx[:, h * HD:(h + 1) * HD].astype(BF16)
        pr = probs[h]
        dp = _dot_nt(dy, vh)
        ds = (pr * (dp - jnp.sum(dp * pr, axis=-1, keepdims=True))).astype(BF16)
        dqs.append(_dot(ds, kh) * SCALE)
        dkv_ref[:, h * HD:(h + 1) * HD] += _dot_tn(ds, qh) * SCALE
        dkv_ref[:, XA + h * HD:XA + (h + 1) * HD] += _dot_tn(pr.astype(BF16), dy)
    return jnp.concatenate(dqs, axis=-1)


def _layer_norm_fwd(v, g, b):
    mu = jnp.mean(v, axis=-1, keepdims=True)
    vc = v - mu
    rstd = lax.rsqrt(jnp.mean(vc * vc, axis=-1, keepdims=True) + EPS)
    vhat = vc * rstd
    return vhat * g + b, vhat, rstd


def _layer_norm_bwd(dy, vhat, rstd, g):
    dvh = dy * g
    return rstd * (dvh - jnp.mean(dvh, axis=-1, keepdims=True) - vhat * jnp.mean(dvh * vhat, axis=-1, keepdims=True))


def _head_masks():
    col = lax.broadcasted_iota(jnp.int32, (1, BW), 1)
    return [(col >= h * (BW // 4)) & (col < (h + 1) * (BW // 4)) for h in range(4)]


def _halo_prev(nblk_per_tile):
    return lambda i: (jnp.maximum(i * nblk_per_tile - 1, 0), 0)


def _row_ids(i, t):
    return i * t + lax.broadcasted_iota(jnp.int32, (t, 1), 0)


def _even_mix(i, p_ref, ph_ref, ln_g, ln_b, wcat_ref, bsg_ref, bconv_ref, wbuf):
    t = p_ref.shape[0]
    u = p_ref[:, 0:BW].astype(F32)
    v = p_ref[:, BW:2 * BW].astype(F32)
    bg = p_ref[:, 2 * BW:3 * BW].astype(F32)
    cg = p_ref[:, 3 * BW:4 * BW].astype(F32)
    xin = p_ref[:, 4 * BW:5 * BW].astype(F32)
    vn, vhat, rstd = _layer_norm_fwd(v, ln_g, ln_b)
    masks = _head_masks()
    sgs, vsts = [], []
    for n in range(t // CH):
        vn_c = vn[n * CH:(n + 1) * CH]
        vst = jnp.concatenate([jnp.where(m, vn_c, 0.0) for m in masks], axis=0).astype(BF16)
        sgs.append(_dot(wcat_ref[...], vst) + bsg_ref[...])
        vsts.append(vst)
    sg = jnp.concatenate(sgs, axis=0)
    ya = u * sg
    w_halo = ph_ref[:, 3 * BW:4 * BW].astype(F32) * ph_ref[:, 4 * BW:5 * BW].astype(F32)
    wbuf[0:HALO, :] = jnp.where(i > 0, w_halo, 0.0)
    wbuf[HALO:HALO + t, :] = cg * xin
    conv = (bconv_ref[0:1, :] * wbuf[pl.ds(HALO - 2, t), :] + bconv_ref[1:2, :] * wbuf[pl.ds(HALO - 1, t), :]
            + bconv_ref[2:3, :] * wbuf[pl.ds(HALO, t), :])
    yb = bg * conv
    return dict(u=u, bg=bg, cg=cg, xin=xin, vhat=vhat, rstd=rstd, sg=sg, vsts=vsts, conv=conv, ya=ya, yb=yb,
                masks=masks)


def _pool_select(vals):
    col = lax.broadcasted_iota(jnp.int32, (1, BW), 1)
    g = BW // 4
    return jnp.where(col < g, vals[0], jnp.where(col < 2 * g, vals[1], jnp.where(col < 3 * g, vals[2], vals[3])))


def _inv_counts(i, t):
    rows = _row_ids(i, t) + 1
    return [1.0 / jnp.minimum(rows, w).astype(F32) for w in POOL_WINDOWS]


def _band_matrices(t, forward):
    j = jnp.arange(t)[:, None]
    r = jnp.arange(HALO + t)[None, :]
    if forward:
        return jnp.stack([(r >= j) & (r < j + w) for w in POOL_WINDOWS]).astype(BF16)
    return jnp.stack([(r <= HALO + j) & (r > HALO + j - w) for w in POOL_WINDOWS]).astype(BF16)


SHIFT_ROWS = HALO + TS - 8


def _shifted_copies(buf, sh):
    for b in range(1, 8):
        sh[b - 1] = buf[pl.ds(b, SHIFT_ROWS), :]


def _rows_at(buf, sh, off, t):
    a, b = divmod(off, 8)
    return buf[pl.ds(8 * a, t), :] if b == 0 else sh[b - 1, pl.ds(8 * a, t), :]


def _tap_sums(d_ref, buf, sh, base, out_ref):
    t = d_ref.shape[0]
    group = 4
    for k0 in range(0, CONF, group):
        taps = list(range(k0, min(k0 + group, CONF)))

        def step(r, accs, taps=taps):
            row = pl.multiple_of(r * 8, 8)
            d = d_ref[pl.ds(row, 8), :]
            new = []
            for acc, k in zip(accs, taps):
                a, b = divmod(base + k, 8)
                src = buf[pl.ds(row + 8 * a, 8), :] if b == 0 else sh[b - 1, pl.ds(row + 8 * a, 8), :]
                new.append(acc + d * src)
            return tuple(new)

        accs = lax.fori_loop(0, t // 8, step, tuple(jnp.zeros((8, BW), F32) for _ in taps), unroll=2)
        for acc, k in zip(accs, taps):
            out_ref[8 * k:8 * k + 8, :] += acc


def _odd_mix(i, p_ref, ph_ref, bands_ref, wbd_ref, cscale, dww_ref, dwb, ln_g, ln_b, pww_ref, pwb, gbuf, gsh,
             cv=None):
    t = p_ref.shape[0]
    zc_bf = p_ref[:, 0:BW]
    zc = zc_bf.astype(F32)
    ga = p_ref[:, BW:2 * BW].astype(F32)
    gb = p_ref[:, 2 * BW:3 * BW].astype(F32)
    zh = ph_ref[:, 0:BW]
    zcat = jnp.concatenate([jnp.where(i > 0, zh, jnp.zeros_like(zh)), zc_bf], axis=0)
    inv = _inv_counts(i, t)
    pooled = _pool_select([_dot(bands_ref[w], zcat) * inv[w] for w in range(len(POOL_WINDOWS))]) - zc
    pooled_bf = pooled.astype(BF16)
    pre = _dot(pooled_bf, wbd_ref[...])
    yc = pre * cscale
    sgb = _sigmoid(gb)
    z = ga * sgb
    gh_a = ph_ref[:, BW:2 * BW].astype(F32)
    gh_b = ph_ref[:, 2 * BW:3 * BW].astype(F32)
    gbuf[0:HALO, :] = jnp.where(i > 0, gh_a * _sigmoid(gh_b), 0.0)
    gbuf[HALO:HALO + t, :] = z
    _shifted_copies(gbuf, gsh)
    if cv is None:
        cv = dwb + dww_ref[CONF - 1:CONF, :] * z
        for k in range(CONF - 1):
            cv = cv + dww_ref[k:k + 1, :] * _rows_at(gbuf, gsh, HALO - (CONF - 1) + k, t)
    zl, zhat, rstd = _layer_norm_fwd(cv, ln_g, ln_b)
    szl = _sigmoid(zl)
    zs = (zl * szl).astype(BF16)
    yd = _dot(zs, pww_ref[...]) + pwb
    return dict(ga=ga, sgb=sgb, pooled_bf=pooled_bf, pre=pre, yc=yc, zhat=zhat, rstd=rstd, zl=zl, szl=szl,
                zs=zs, yd=yd, inv=inv, cv=cv)


def _post_norm(o, post_g):
    r = lax.rsqrt(jnp.mean(o * o, axis=-1, keepdims=True) + EPS)
    return o * r, r


def _gate_out(y_a, y_b, y_x, gate, wout_ref):
    sgt = _sigmoid(gate)
    sgate = gate * sgt
    ys = [(y_a * sgate[:, 0:BW]).astype(BF16), (y_b * sgate[:, BW:2 * BW]).astype(BF16),
          (y_x * sgate[:, 2 * BW:MIX]).astype(BF16)]
    o = (_dot(ys[0], wout_ref[0:BW, :]) + _dot(ys[1], wout_ref[BW:2 * BW, :]) + _dot(ys[2], wout_ref[2 * BW:MIX, :]))
    return o, ys, sgt, sgate


def _tile_specs(s, n):
    nh = TS // HALO
    return pl.BlockSpec((TS, n), lambda i: (i, 0)), pl.BlockSpec((HALO, n), _halo_prev(nh))


def _even_fwd(x, p, kv, ln_g, ln_b, wcat, bsg, bconv, wout, post_g, rider=None):
    s = x.shape[0]

    def body(x_ref, p_ref, ph_ref, kv_ref, lng, lnb, wcat_ref, bsg_ref, bconv_ref, wout_ref, pg, x1_ref, o_ref,
             y_ref, wbuf):
        i = pl.program_id(0)
        mx = _even_mix(i, p_ref, ph_ref, lng[...], lnb[...], wcat_ref, bsg_ref, bconv_ref, wbuf)
        yx, _ = _xattn_fwd(p_ref[:, 5 * BW:5 * BW + XA], kv_ref)
        gate = p_ref[:, 5 * BW + XA:EVEN_IN].astype(F32)
        o, ys, _, _ = _gate_out(mx["ya"], mx["yb"], yx, gate, wout_ref)
        y_ref[:, 0:BW] = ys[0]
        y_ref[:, BW:2 * BW] = ys[1]
        y_ref[:, 2 * BW:MIX] = ys[2]
        n, _ = _post_norm(o, pg[...])
        o_ref[...] = o
        x1_ref[...] = x_ref[...] + n * pg[...]

    tile, halo = _tile_specs(s, EVEN_IN)
    row = pl.BlockSpec((TS, D), lambda i: (i, 0))
    return _host_call(
        body, grid=(s // TS,), name="even_fwd", rider=rider,
        out_shape=(jax.ShapeDtypeStruct((s, D), F32), jax.ShapeDtypeStruct((s, D), F32),
                   jax.ShapeDtypeStruct((s, MIX), BF16)),
        in_specs=[row, tile, halo, _const((N_MEM, D)), _const((1, BW)), _const((1, BW)), _const((CH, 4 * CH)),
                  _const((CH, BW)), _const((3, BW)), _resident((MIX, D)), _const((1, D))],
        out_specs=(row, row, pl.BlockSpec((TS, MIX), lambda i: (i, 0))),
        scratch_shapes=[pltpu.VMEM((HALO + TS, BW), F32)],
        args=(x, p, p, kv, ln_g, ln_b, wcat, bsg, bconv, wout, post_g))


def _odd_fwd(x1, p, kv, wbd, cscale, dww, dwb, ln_g, ln_b, pww, pwb, wout, post_g, target):
    s = x1.shape[0]

    def body(x_ref, p_ref, ph_ref, kv_ref, bands_ref, wbd_ref, cs, dww_ref, dwb_ref, lng, lnb, pww_ref, pwb_ref,
             wout_ref, pg, tgt_ref, dx_ref, o_ref, cv_ref, loss_ref, gbuf, gsh):
        i = pl.program_id(0)
        mx = _odd_mix(i, p_ref, ph_ref, bands_ref, wbd_ref, cs[...], dww_ref, dwb_ref[...], lng[...], lnb[...],
                      pww_ref, pwb_ref[...], gbuf, gsh)
        cv_ref[...] = mx["cv"]
        yx, _ = _xattn_fwd(p_ref[:, 3 * BW:3 * BW + XA], kv_ref)
        gate = p_ref[:, 3 * BW + XA:ODD_IN].astype(F32)
        o, _, _, _ = _gate_out(mx["yc"], mx["yd"], yx, gate, wout_ref)
        n, _ = _post_norm(o, pg[...])
        o_ref[...] = o
        err = x_ref[...] + n * pg[...] - tgt_ref[...]
        dx_ref[...] = err * (1.0 / D)

        @pl.when(i == 0)
        def _():
            loss_ref[...] = jnp.zeros_like(loss_ref)

        loss_ref[...] += 0.5 * jnp.sum(jnp.sum(err * err, axis=-1, keepdims=True) * (1.0 / D), axis=0, keepdims=True)

    tile, halo = _tile_specs(s, ODD_IN)
    row = pl.BlockSpec((TS, D), lambda i: (i, 0))
    vec = _const((1, BW))
    return pl.pallas_call(
        body, grid=(s // TS,), name="odd_fwd",
        out_shape=(jax.ShapeDtypeStruct((s, D), F32), jax.ShapeDtypeStruct((s, D), F32),
                   jax.ShapeDtypeStruct((s, BW), F32), jax.ShapeDtypeStruct((8, 128), F32)),
        in_specs=[row, tile, halo, _const((N_MEM, D)), _const((4, TS, HALO + TS)), _const((BW, BW)), vec,
                  _const((CONF, BW)), vec, vec, vec, _const((BW, BW)), vec, _resident((MIX, D)), _const((1, D)), row],
        out_specs=(row, row, pl.BlockSpec((TS, BW), lambda i: (i, 0)), _const((8, 128))),
        scratch_shapes=[pltpu.VMEM((HALO + TS, BW), F32), pltpu.VMEM((7, SHIFT_ROWS, BW), F32)],
        compiler_params=_cp(("arbitrary",)),
    )(x1, p, p, kv, _band_matrices(TS, False), wbd, cscale, dww, dwb, ln_g, ln_b, pww, pwb, wout, post_g, target)


def _acc_init(i, refs):
    @pl.when(i == 0)
    def _():
        for r in refs:
            r[...] = jnp.zeros_like(r)


def _post_norm_bwd(dx, o, pg, dpg_ref):
    n, r = _post_norm(o, pg)
    dpg_ref[...] += jnp.sum(dx * n, axis=0, keepdims=True)
    dn = dx * pg
    return (r * (dn - n * jnp.mean(dn * n, axis=-1, keepdims=True))).astype(BF16)


def _gate_bwd(do, wout_ref, ys_f32, gate, y_ref):
    dy = _dot_nt(do, wout_ref[...])
    sgt = _sigmoid(gate)
    sgate = gate * sgt
    dsilu = sgt * (1.0 + gate * (1.0 - sgt))
    offs = (0, BW, 2 * BW, MIX)
    dys, dgs = [], []
    for j, yv in enumerate(ys_f32):
        a, b = offs[j], offs[j + 1]
        if y_ref is not None:
            y_ref[:, a:b] = (yv * sgate[:, a:b]).astype(BF16)
        dys.append(dy[:, a:b] * sgate[:, a:b])
        dgs.append(dy[:, a:b] * yv * dsilu[:, a:b])
    return dys, jnp.concatenate(dgs, axis=-1)


NEXT = 16


def _even_bwd1(dx, o, p, kv, ln_g, ln_b, wcat, bsg, hsel, bconv, wout, post_g, rider=None):
    s = dx.shape[0]
    nt = s // TS

    def body(dx_ref, o_ref, p_ref, ph_ref, dxn_ref, on_ref, pn_ref, kv_ref, lng, lnb, wcat_ref, bsg_ref, hsel_ref,
             bconv_ref, wout_ref, pg,
             dp_ref, do_ref, dpg_ref, dlng_ref, dlnb_ref, dwcat_ref, dbs_ref, dbconv_ref, dkv_ref, wbuf, dbuf):
        i = pl.program_id(0)
        _acc_init(i, (dpg_ref, dlng_ref, dlnb_ref, dwcat_ref, dbs_ref, dbconv_ref, dkv_ref))
        mx = _even_mix(i, p_ref, ph_ref, lng[...], lnb[...], wcat_ref, bsg_ref, bconv_ref, wbuf)
        q = p_ref[:, 5 * BW:5 * BW + XA]
        yx, probs = _xattn_fwd(q, kv_ref)
        gate = p_ref[:, 5 * BW + XA:EVEN_IN].astype(F32)
        do = _post_norm_bwd(dx_ref[...], o_ref[...], pg[...], dpg_ref)
        do_ref[...] = do
        (dya, dyb, dyx), dgate = _gate_bwd(do, wout_ref, (mx["ya"], mx["yb"], yx), gate, None)
        dp_ref[:, 0:BW] = (dya * mx["sg"]).astype(BF16)
        dsg = (dya * mx["u"]).astype(BF16)
        dvns = []
        for n in range(TS // CH):
            dsg_c = dsg[n * CH:(n + 1) * CH]
            dvst = _dot_tn(wcat_ref[...], dsg_c)
            dvn_c = jnp.where(mx["masks"][0], dvst[0:CH], 0.0)
            for h in range(1, 4):
                dvn_c = dvn_c + jnp.where(mx["masks"][h], dvst[h * CH:(h + 1) * CH], 0.0)
            dvns.append(dvn_c)
            dwcat_ref[...] += _dot_nt(dsg_c, mx["vsts"][n])
            dbs_ref[...] += _dot(dsg_c, hsel_ref[...])
        dvn = jnp.concatenate(dvns, axis=0)
        dlng_ref[...] += jnp.sum(dvn * mx["vhat"], axis=0, keepdims=True)
        dlnb_ref[...] += jnp.sum(dvn, axis=0, keepdims=True)
        dp_ref[:, BW:2 * BW] = _layer_norm_bwd(dvn, mx["vhat"], mx["rstd"], lng[...]).astype(BF16)
        dp_ref[:, 2 * BW:3 * BW] = (dyb * mx["conv"]).astype(BF16)
        dconv = dyb * mx["bg"]
        for k in range(3):
            dbconv_ref[k:k + 1, :] += jnp.sum(dconv * wbuf[pl.ds(HALO - 2 + k, TS), :], axis=0, keepdims=True)
        n_n, r_n = _post_norm(on_ref[...], pg[...])
        dn_n = dxn_ref[...] * pg[...]
        do_n = (r_n * (dn_n - n_n * jnp.mean(dn_n * n_n, axis=-1, keepdims=True))).astype(BF16)
        dy_n = _dot_nt(do_n, wout_ref[BW:2 * BW, :])
        g_n = pn_ref[:, 5 * BW + XA + BW:5 * BW + XA + 2 * BW].astype(F32)
        dconv_n = dy_n * (g_n * _sigmoid(g_n)) * pn_ref[:, 2 * BW:3 * BW].astype(F32)
        dbuf[0:TS, :] = dconv
        dbuf[TS:TS + NEXT, :] = jnp.where(i < nt - 1, dconv_n, 0.0)
        dw = (bconv_ref[2:3, :] * dconv + bconv_ref[1:2, :] * dbuf[pl.ds(1, TS), :]
              + bconv_ref[0:1, :] * dbuf[pl.ds(2, TS), :])
        dp_ref[:, 3 * BW:4 * BW] = (dw * mx["xin"]).astype(BF16)
        dp_ref[:, 4 * BW:5 * BW] = (dw * mx["cg"]).astype(BF16)
        dp_ref[:, 5 * BW:5 * BW + XA] = _xattn_bwd(dyx, q, probs, kv_ref, dkv_ref).astype(BF16)
        dp_ref[:, 5 * BW + XA:EVEN_IN] = dgate.astype(BF16)

    tile, halo = _tile_specs(s, EVEN_IN)
    row = pl.BlockSpec((TS, D), lambda i: (i, 0))
    vec = _const((1, BW))
    nxt = _halo_next(TS // NEXT, s // NEXT)

    def out(n):
        return pl.BlockSpec((TS, n), lambda i: (i, 0))

    return _host_call(
        body, grid=(nt,), name="even_bwd1", rider=rider,
        out_shape=(jax.ShapeDtypeStruct((s, EVEN_IN), BF16), jax.ShapeDtypeStruct((s, D), BF16),
                   jax.ShapeDtypeStruct((1, D), F32), jax.ShapeDtypeStruct((1, BW), F32),
                   jax.ShapeDtypeStruct((1, BW), F32), jax.ShapeDtypeStruct((CH, 4 * CH), F32),
                   jax.ShapeDtypeStruct((CH, 128), F32), jax.ShapeDtypeStruct((8, BW), F32),
                   jax.ShapeDtypeStruct((N_MEM, D), F32)),
        in_specs=[row, row, tile, halo, pl.BlockSpec((NEXT, D), nxt), pl.BlockSpec((NEXT, D), nxt),
                  pl.BlockSpec((NEXT, EVEN_IN), nxt), _const((N_MEM, D)), vec, vec, _const((CH, 4 * CH)),
                  _const((CH, BW)), _const((BW, 128)), _const((3, BW)), _resident((MIX, D)), _const((1, D))],
        out_specs=(out(EVEN_IN), out(D),
                   _const((1, D)), vec, vec, _const((CH, 4 * CH)), _const((CH, 128)), _const((8, BW)),
                   _const((N_MEM, D))),
        scratch_shapes=[pltpu.VMEM((HALO + TS, BW), F32), pltpu.VMEM((TS + NEXT, BW), F32)],
        args=(dx, o, p, p, dx, o, p, kv, ln_g, ln_b, wcat, bsg, hsel, bconv, wout, post_g))


def _odd_bwd1(dx, o, cv, p, kv, wbd, cscale, dww, dwb, ln_g, ln_b, pww, pwb, wout, post_g):
    s = dx.shape[0]

    def body(dx_ref, o_ref, cv_ref, p_ref, ph_ref, kv_ref, bands_ref, wbd_ref, cs, dww_ref, dwb_ref, lng, lnb,
             pww_ref, pwb_ref, wout_ref, pg,
             dpc_ref, tmpc_ref, tmpd_ref, do_ref, y_ref, dpg_ref, dcs_ref, dwbd_ref, ddww_ref, ddwb_ref, dlng_ref,
             dlnb_ref, dpww_ref, dpwb_ref, dkv_ref, gbuf, gsh, dcv_buf):
        i = pl.program_id(0)
        _acc_init(i, (dpg_ref, dcs_ref, dwbd_ref, ddww_ref, ddwb_ref, dlng_ref, dlnb_ref, dpww_ref, dpwb_ref,
                      dkv_ref))
        mx = _odd_mix(i, p_ref, ph_ref, bands_ref, wbd_ref, cs[...], dww_ref, dwb_ref[...], lng[...], lnb[...],
                      pww_ref, pwb_ref[...], gbuf, gsh, cv=cv_ref[...])
        q = p_ref[:, 3 * BW:3 * BW + XA]
        yx, probs = _xattn_fwd(q, kv_ref)
        gate = p_ref[:, 3 * BW + XA:ODD_IN].astype(F32)
        do = _post_norm_bwd(dx_ref[...], o_ref[...], pg[...], dpg_ref)
        do_ref[...] = do
        (dyc, dyd, dyx), dgate = _gate_bwd(do, wout_ref, (mx["yc"], mx["yd"], yx), gate, y_ref)
        dcs_ref[...] += jnp.sum(dyc * mx["pre"], axis=0, keepdims=True)
        dpre = (dyc * cs[...]).astype(BF16)
        dwbd_ref[...] += _dot_tn(mx["pooled_bf"], dpre)
        dpooled = _dot_nt(dpre, wbd_ref[...])
        tmpc_ref[...] = _pool_select([dpooled * c_ for c_ in mx["inv"]]).astype(BF16)
        dyd_bf = dyd.astype(BF16)
        dpwb_ref[...] += jnp.sum(dyd, axis=0, keepdims=True)
        dpww_ref[...] += _dot_tn(mx["zs"], dyd_bf)
        dzs = _dot_nt(dyd_bf, pww_ref[...])
        zl, szl = mx["zl"], mx["szl"]
        dzl = dzs * (szl * (1.0 + zl * (1.0 - szl)))
        dlng_ref[...] += jnp.sum(dzl * mx["zhat"], axis=0, keepdims=True)
        dlnb_ref[...] += jnp.sum(dzl, axis=0, keepdims=True)
        dcv = _layer_norm_bwd(dzl, mx["zhat"], mx["rstd"], lng[...])
        tmpd_ref[...] = dcv.astype(BF16)
        ddwb_ref[...] += jnp.sum(dcv, axis=0, keepdims=True)
        dcv_buf[...] = dcv
        _tap_sums(dcv_buf, gbuf, gsh, HALO - (CONF - 1), ddww_ref)
        dpc_ref[:, 0:XA] = _xattn_bwd(dyx, q, probs, kv_ref, dkv_ref).astype(BF16)
        dpc_ref[:, XA:XA + MIX] = dgate.astype(BF16)

    tile, halo = _tile_specs(s, ODD_IN)
    row = pl.BlockSpec((TS, D), lambda i: (i, 0))
    vec = _const((1, BW))

    def out(n):
        return pl.BlockSpec((TS, n), lambda i: (i, 0))

    return pl.pallas_call(
        body, grid=(s // TS,), name="odd_bwd1",
        out_shape=(jax.ShapeDtypeStruct((s, XA + MIX), BF16), jax.ShapeDtypeStruct((s, BW), BF16),
                   jax.ShapeDtypeStruct((s, BW), BF16), jax.ShapeDtypeStruct((s, D), BF16),
                   jax.ShapeDtypeStruct((s, MIX), BF16),
                   jax.ShapeDtypeStruct((1, D), F32), jax.ShapeDtypeStruct((1, BW), F32),
                   jax.ShapeDtypeStruct((BW, BW), F32), jax.ShapeDtypeStruct((8 * CONF, BW), F32),
                   jax.ShapeDtypeStruct((1, BW), F32), jax.ShapeDtypeStruct((1, BW), F32),
                   jax.ShapeDtypeStruct((1, BW), F32), jax.ShapeDtypeStruct((BW, BW), F32),
                   jax.ShapeDtypeStruct((1, BW), F32), jax.ShapeDtypeStruct((N_MEM, D), F32)),
        in_specs=[row, row, out(BW), tile, halo, _const((N_MEM, D)), _const((4, TS, HALO + TS)), _const((BW, BW)), vec,
                  _const((CONF, BW)), vec, vec, vec, _const((BW, BW)), vec, _resident((MIX, D)), _const((1, D))],
        out_specs=(out(XA + MIX), out(BW), out(BW), out(D), out(MIX),
                   _const((1, D)), vec, _const((BW, BW)), _const((8 * CONF, BW)), vec, vec, vec, _const((BW, BW)), vec,
                   _const((N_MEM, D))),
        scratch_shapes=[pltpu.VMEM((HALO + TS, BW), F32), pltpu.VMEM((7, SHIFT_ROWS, BW), F32),
                        pltpu.VMEM((TS, BW), F32)],
        compiler_params=_cp(("arbitrary",)),
    )(dx, o, cv, p, p, kv, _band_matrices(TS, False), wbd, cscale, dww, dwb, ln_g, ln_b, pww, pwb, wout, post_g)


def _halo_next(nblk_per_tile, nblk):
    return lambda i: (jnp.minimum((i + 1) * nblk_per_tile, nblk - 1), 0)


def _pre_norm_bwd(dh, x, pre_g, dres, dpre_ref):
    r = lax.rsqrt(jnp.mean(x * x, axis=-1, keepdims=True) + EPS)
    xh = x * r
    dpre_ref[...] += jnp.sum(dh * xh, axis=0, keepdims=True)
    dxh = dh * pre_g
    return dres + r * (dxh - xh * jnp.mean(dxh * xh, axis=-1, keepdims=True))


def _even_bwd2(dp, w_t, x, pre_g, dres, rider=None):
    s = x.shape[0]
    tm = min(512, s)

    def body(dp_ref, w_ref, x_ref, pg, dres_ref, dx_ref, dpre_ref):
        _acc_init(pl.program_id(0), (dpre_ref,))
        dh = _dot(dp_ref[...], w_ref[...])
        dx_ref[...] = _pre_norm_bwd(dh, x_ref[...], pg[...], dres_ref[...], dpre_ref)

    row = pl.BlockSpec((tm, D), lambda i: (i, 0))
    return _host_call(
        body, grid=(s // tm,), name="even_bwd2", rider=rider,
        out_shape=(jax.ShapeDtypeStruct((s, D), F32), jax.ShapeDtypeStruct((1, D), F32)),
        in_specs=[pl.BlockSpec((tm, EVEN_IN), lambda i: (i, 0)), _resident((EVEN_IN, D)), row, _const((1, D)), row],
        out_specs=(row, _const((1, D))),
        args=(dp, w_t, x, pre_g, dres))


def _odd_bwd2(dpc, tmpc, tmpd, p, dww, w_t, x, pre_g, dres):
    s = x.shape[0]
    nt = s // TS

    def body(dpc_ref, tc_ref, tch_ref, td_ref, tdh_ref, ga_ref, gb_ref, bands_ref, dww_ref, w_ref, x_ref, pg,
             dres_ref, dpb_ref, dx_ref, dpre_ref, dbuf, dsh):
        i = pl.program_id(0)
        _acc_init(i, (dpre_ref,))
        more = i < nt - 1
        e_bf = tc_ref[...]
        eh = tch_ref[...]
        ecat = jnp.concatenate([e_bf, jnp.where(more, eh, jnp.zeros_like(eh))], axis=0)
        dbuf[0:TS, :] = td_ref[...].astype(F32)
        dbuf[TS:TS + HALO, :] = jnp.where(more, tdh_ref[...].astype(F32), 0.0)
        sums = [_dot(bands_ref[w], ecat) for w in range(len(POOL_WINDOWS))]
        rows = _row_ids(i, TS) + 1
        cnt = _pool_select([jnp.minimum(rows, w).astype(F32) for w in POOL_WINDOWS])
        dzc = (_pool_select(sums) - e_bf.astype(F32) * cnt).astype(BF16)
        _shifted_copies(dbuf, dsh)
        dz = dww_ref[CONF - 1:CONF, :] * dbuf[pl.ds(0, TS), :]
        for sft in range(1, CONF):
            dz = dz + dww_ref[CONF - 1 - sft:CONF - sft, :] * _rows_at(dbuf, dsh, sft, TS)
        ga = ga_ref[...].astype(F32)
        sgb = _sigmoid(gb_ref[...].astype(F32))
        dga = (dz * sgb).astype(BF16)
        dgb = (dz * ga * sgb * (1.0 - sgb)).astype(BF16)
        dpb_ref[:, 0:BW] = dzc
        dpb_ref[:, BW:2 * BW] = dga
        dpb_ref[:, 2 * BW:3 * BW] = dgb
        dh = (_dot(dzc, w_ref[0:BW, :]) + _dot(dga, w_ref[BW:2 * BW, :]) + _dot(dgb, w_ref[2 * BW:3 * BW, :])
              + _dot(dpc_ref[...], w_ref[3 * BW:ODD_IN, :]))
        dx_ref[...] = _pre_norm_bwd(dh, x_ref[...], pg[...], dres_ref[...], dpre_ref)

    row = pl.BlockSpec((TS, D), lambda i: (i, 0))

    def tile(n, j=0):
        return pl.BlockSpec((TS, n), lambda i: (i, j))

    nxt = pl.BlockSpec((HALO, BW), _halo_next(TS // HALO, s // HALO))
    return pl.pallas_call(
        body, grid=(nt,), name="odd_bwd2",
        out_shape=(jax.ShapeDtypeStruct((s, 3 * BW), BF16), jax.ShapeDtypeStruct((s, D), F32),
                   jax.ShapeDtypeStruct((1, D), F32)),
        in_specs=[tile(XA + MIX), tile(BW), nxt, tile(BW), nxt, tile(BW, 1), tile(BW, 2), _const((4, TS, HALO + TS)),
                  _const((CONF, BW)), _resident((ODD_IN, D)), row, _const((1, D)), row],
        out_specs=(tile(3 * BW), row, _const((1, D))),
        scratch_shapes=[pltpu.VMEM((TS + HALO, BW), F32), pltpu.VMEM((7, SHIFT_ROWS, BW), F32)],
        compiler_params=_cp(("arbitrary",)),
    )(dpc, tmpc, tmpc, tmpd, tmpd, p, p, _band_matrices(TS, True), dww, w_t, x, pre_g, dres)


def _grad_tn(a, b, tm, out=None, rows=None, row0=0, name="grad_tn", rider=None):
    s, m = a.shape
    n = b.shape[1]
    ts = min(2048, s)
    rows = m if rows is None else rows
    assert m % tm == 0 and s % ts == 0
    ns = s // ts
    if row0 % tm == 0:
        out_spec = pl.BlockSpec((tm, n), lambda i, k: (row0 // tm + i, 0))
    else:
        align = 16
        assert row0 % align == 0 and tm % align == 0
        out_spec = pl.BlockSpec((pl.Element(tm), pl.Element(n)),
                                lambda i, k: (pl.multiple_of(row0 + i * tm, align), 0))

    def body(*refs):
        a_ref, b_ref = refs[0], refs[1]
        o_ref, acc = refs[-2], refs[-1]
        k = pl.program_id(1)

        @pl.when(k == 0)
        def _():
            acc[...] = jnp.zeros_like(acc)

        acc[...] += _dot_tn(a_ref[...], b_ref[...])

        @pl.when(k == ns - 1)
        def _():
            o_ref[...] = acc[...].astype(BF16)

    in_specs = [pl.BlockSpec((ts, tm), lambda i, k: (k, i)), pl.BlockSpec((ts, n), lambda i, k: (k, 0))]
    args = [a, b]
    aliases = {}
    if out is not None:
        in_specs.append(pl.BlockSpec(memory_space=pltpu.HBM))
        args.append(out)
        aliases = {2: 0}
    (res,), got = _host_call(
        body, grid=(m // tm, ns), name=name, rider=rider, aliases=aliases,
        out_shape=(jax.ShapeDtypeStruct((rows, n), BF16),), in_specs=in_specs, out_specs=(out_spec,),
        scratch_shapes=[pltpu.VMEM((tm, n), F32)], args=args)
    return res if rider is None else (res, got)


def _place():
    x, y, c = lax.axis_index("x"), lax.axis_index("y"), lax.axis_index("c")
    chips = [(1 - x, y), (x, 1 - y), (1 - x, 1 - y)]
    return x, y, c, chips


def _hbm_specs(n):
    return [pl.BlockSpec(memory_space=pltpu.HBM)] * n


def _row_tile(r):
    for cand in (512, 400, 304, 256, 192, 128, 96, 16):
        if r % cand == 0:
            return cand
    raise ValueError(r)


def _place_shard(shard, place, dtype, name, after=None):
    r, cc = shard.shape
    tr = _row_tile(r)
    nt = r // tr

    def body(place_ref, s_ref, *rest):
        rest[-1][...] = s_ref[...].astype(dtype)

    in_specs = [pl.BlockSpec((tr, cc), lambda i, pr: (i, 0))]
    args = [shard]
    if after is not None:
        in_specs.append(pl.BlockSpec(after.shape, lambda i, pr: (0, 0)))
        args.append(after)
    return pl.pallas_call(
        body, name=name, out_shape=jax.ShapeDtypeStruct((N_CHIPS * r, cc), dtype),
        grid_spec=pltpu.PrefetchScalarGridSpec(
            num_scalar_prefetch=1, grid=(nt,), in_specs=in_specs,
            out_specs=pl.BlockSpec((tr, cc), lambda i, pr: (pr[1] * nt + i, 0))),
        compiler_params=_cp(("arbitrary",)),
    )(place, *args)


class _GatherRider:
    has_mid = True

    def __init__(self, fulls):
        n = len(fulls)
        self.inputs = list(fulls)
        self.out_shapes = [jax.ShapeDtypeStruct(a.shape, a.dtype) for a in fulls]
        self.aliases = {a: a for a in range(n)}
        self.sems = [pltpu.SemaphoreType.DMA((6 * n,)), pltpu.SemaphoreType.DMA((6 * n,))]
        self.block_rows = [a.shape[0] // N_CHIPS for a in fulls]

    def _ctx(self, outs, sems):
        send_sems, recv_sems = sems
        x, y, c, chips = _place()

        def rows(a, k, half):
            r = self.block_rows[a]
            return outs[a].at[pl.ds(k * r + half * (r // 2), r // 2)]

        def copy(a, j, blk, to):
            return pltpu.make_async_remote_copy(src_ref=blk, dst_ref=blk, send_sem=send_sems.at[a * 6 + j],
                                                recv_sem=recv_sems.at[a * 6 + j], device_id=to, device_id_type=MESH)

        return x, y, c, chips, rows, copy

    def start(self, ins, outs, sems, peers=(0, 1, 2)):
        x, y, c, chips, rows, copy = self._ctx(outs, sems)
        for j in peers:
            for a in range(len(outs)):
                copy(a, j, rows(a, 2 * x + y, c), (*chips[j], c)).start()

    def mid(self, ins, outs, sems, peers=(0, 1, 2)):
        x, y, c, chips, rows, copy = self._ctx(outs, sems)
        for j in peers:
            px, py = chips[j]
            for a in range(len(outs)):
                copy(a, j, rows(a, 2 * px + py, c), (px, py, c)).wait_recv()
                copy(a, 3 + j, rows(a, 2 * px + py, c), (x, y, 1 - c)).start()

    def wait_forwarded(self, outs, sems, peers=(0, 1, 2)):
        x, y, c, chips, rows, copy = self._ctx(outs, sems)
        for j in peers:
            px, py = chips[j]
            for a in range(len(outs)):
                copy(a, 3 + j, rows(a, 2 * px + py, 1 - c), (x, y, 1 - c)).wait_recv()

    def wait_sends(self, outs, sems):
        x, y, c, chips, rows, copy = self._ctx(outs, sems)
        for j, (px, py) in enumerate(chips):
            for a in range(len(outs)):
                copy(a, j, rows(a, 2 * x + y, c), (px, py, c)).wait_send()
                copy(a, 3 + j, rows(a, 2 * px + py, c), (x, y, 1 - c)).wait_send()

    def end(self, ins, outs, sems):
        self.wait_forwarded(outs, sems)
        self.wait_sends(outs, sems)


def _swap_halves(grads, name, share=()):
    n, k = len(grads), len(share)
    m = n + k

    def body(*refs):
        ins, outs = refs[:m], refs[m:2 * m]
        send_sems, recv_sems = refs[2 * m:]
        x, y, c, _ = _place()
        sibling = (x, y, 1 - c)
        cps, waits = [], []
        for a in range(m):
            if a < n:
                cp = pltpu.make_async_remote_copy(src_ref=ins[a].at[:, 1 - c], dst_ref=outs[a],
                                                  send_sem=send_sems.at[a], recv_sem=recv_sems.at[a],
                                                  device_id=sibling, device_id_type=MESH)
                waits.append(cp)
            else:
                cp = pltpu.make_async_remote_copy(src_ref=outs[a].at[c], dst_ref=outs[a].at[c],
                                                  send_sem=send_sems.at[a], recv_sem=recv_sems.at[a],
                                                  device_id=sibling, device_id_type=MESH)
                waits.append(pltpu.make_async_remote_copy(
                    src_ref=outs[a].at[1 - c], dst_ref=outs[a].at[1 - c], send_sem=send_sems.at[a],
                    recv_sem=recv_sems.at[a], device_id=sibling, device_id_type=MESH))
            cp.start()
            cps.append(cp)
        for cp in waits:
            cp.wait_recv()
        for cp in cps:
            cp.wait_send()

    outs = tuple(jax.ShapeDtypeStruct((g.shape[0],) + g.shape[2:], g.dtype) for g in grads)
    outs += tuple(jax.ShapeDtypeStruct(g.shape, g.dtype) for g in share)
    res = pl.pallas_call(
        body, name=name, out_shape=outs, in_specs=_hbm_specs(m), out_specs=tuple(_hbm_specs(m)),
        input_output_aliases={n + a: n + a for a in range(k)},
        scratch_shapes=[pltpu.SemaphoreType.DMA((m,)), pltpu.SemaphoreType.DMA((m,))],
    )(*grads, *share)
    return tuple(res[:n]), tuple(res[n:])


def _pair_sum(g, recv, place, name):
    _, _, h, cc = g.shape
    th = h

    def body(c_ref, g_ref, r_ref, o_ref):
        o_ref[...] = (g_ref[...].astype(F32) + r_ref[...].astype(F32)).astype(o_ref.dtype)

    return pl.pallas_call(
        body, name=name, out_shape=jax.ShapeDtypeStruct(recv.shape, recv.dtype),
        grid_spec=pltpu.PrefetchScalarGridSpec(
            num_scalar_prefetch=1, grid=(N_CHIPS, h // th),
            in_specs=[pl.BlockSpec((None, None, th, cc), lambda k, r, c_ref: (k, c_ref[0], r, 0)),
                      pl.BlockSpec((None, th, cc), lambda k, r, c_ref: (k, r, 0))],
            out_specs=pl.BlockSpec((None, th, cc), lambda k, r, c_ref: (k, r, 0))),
        compiler_params=_cp(("arbitrary", "arbitrary")),
    )(place, g, recv)


def _finish_reduce(pack, halves):
    rows, cc = pack.shape
    hs = rows // 2
    n = len(halves)

    def body(*refs):
        pack_ref = refs[0]
        out_ref = refs[1 + n]
        big = refs[2 + n:2 + 2 * n]
        sib_ref, parts_ref, send_sems, recv_sems, big_send, big_recv = refs[2 + 2 * n:]
        x, y, c, chips = _place()
        me_k = 2 * x + y
        sibling = (x, y, 1 - c)
        mine = pl.ds(pl.multiple_of(c * hs, hs), hs)
        theirs = pl.ds(pl.multiple_of((1 - c) * hs, hs), hs)
        shared = [pltpu.make_async_remote_copy(src_ref=big[a].at[c], dst_ref=big[a].at[c], send_sem=big_send.at[a],
                                               recv_sem=big_recv.at[a], device_id=sibling, device_id_type=MESH)
                  for a in range(n)]
        first = pltpu.make_async_remote_copy(src_ref=pack_ref, dst_ref=sib_ref, send_sem=send_sems.at[0],
                                             recv_sem=recv_sems.at[0], device_id=sibling, device_id_type=MESH)
        first.start()
        first.wait()
        parts_ref[me_k] = pack_ref[mine, :] + sib_ref[mine, :]
        cps = [pltpu.make_async_remote_copy(src_ref=parts_ref.at[me_k], dst_ref=parts_ref.at[me_k],
                                            send_sem=send_sems.at[1 + j], recv_sem=recv_sems.at[1 + j],
                                            device_id=(px, py, c), device_id_type=MESH)
               for j, (px, py) in enumerate(chips)]
        for cp in cps:
            cp.start()
        for cp in shared:
            cp.start()
        for j, (px, py) in enumerate(chips):
            pltpu.make_async_remote_copy(src_ref=parts_ref.at[2 * px + py], dst_ref=parts_ref.at[2 * px + py],
                                         send_sem=send_sems.at[1 + j], recv_sem=recv_sems.at[1 + j],
                                         device_id=(px, py, c), device_id_type=MESH).wait_recv()
        for cp in cps:
            cp.wait_send()
        out_ref[mine, :] = ((parts_ref[0] + parts_ref[1]) + parts_ref[2]) + parts_ref[3]
        last = pltpu.make_async_remote_copy(src_ref=out_ref.at[mine], dst_ref=out_ref.at[mine],
                                            send_sem=send_sems.at[4], recv_sem=recv_sems.at[4], device_id=sibling,
                                            device_id_type=MESH)
        last.start()
        pltpu.make_async_remote_copy(src_ref=out_ref.at[theirs], dst_ref=out_ref.at[theirs],
                                     send_sem=send_sems.at[4], recv_sem=recv_sems.at[4], device_id=sibling,
                                     device_id_type=MESH).wait_recv()
        last.wait_send()
        for a in range(n):
            pltpu.make_async_remote_copy(src_ref=big[a].at[1 - c], dst_ref=big[a].at[1 - c], send_sem=big_send.at[a],
                                         recv_sem=big_recv.at[a], device_id=sibling,
                                         device_id_type=MESH).wait_recv()
        for cp in shared:
            cp.wait_send()

    vmem = pl.BlockSpec(memory_space=pltpu.VMEM)
    res = pl.pallas_call(
        body, name="finish_reduce",
        out_shape=(jax.ShapeDtypeStruct(pack.shape, pack.dtype),)
        + tuple(jax.ShapeDtypeStruct(g.shape, g.dtype) for g in halves),
        in_specs=[vmem] + _hbm_specs(n), out_specs=(vmem,) + tuple(_hbm_specs(n)),
        input_output_aliases={1 + a: 1 + a for a in range(n)},
        scratch_shapes=[pltpu.VMEM((rows, cc), F32), pltpu.VMEM((N_CHIPS, hs, cc), F32),
                        pltpu.SemaphoreType.DMA((5,)), pltpu.SemaphoreType.DMA((5,)),
                        pltpu.SemaphoreType.DMA((n,)), pltpu.SemaphoreType.DMA((n,))],
        compiler_params=_cp(),
    )(pack, *halves)
    return res[0], tuple(res[1:])


class _ExchangeRider:
    has_mid = False

    def __init__(self, sums):
        self.inputs = list(sums)
        self.out_shapes = [jax.ShapeDtypeStruct((3,) + g.shape[1:], g.dtype) for g in sums]
        m = len(self.inputs)
        self.aliases = {}
        self.sems = [pltpu.SemaphoreType.DMA((3 * m,)), pltpu.SemaphoreType.DMA((3 * m,))]

    def _copies(self, ins, outs, sems):
        send_sems, recv_sems = sems
        _, _, c, chips = _place()
        return [pltpu.make_async_remote_copy(
            src_ref=ins[a].at[2 * px + py], dst_ref=outs[a].at[j], send_sem=send_sems.at[a * 3 + j],
            recv_sem=recv_sems.at[a * 3 + j], device_id=(px, py, c), device_id_type=MESH)
            for j, (px, py) in enumerate(chips) for a in range(len(ins))]

    def start(self, ins, outs, sems):
        for cp in self._copies(ins, outs, sems):
            cp.start()

    def end(self, ins, outs, sems):
        cps = self._copies(ins, outs, sems)
        for cp in cps:
            cp.wait_recv()
        for cp in cps:
            cp.wait_send()


class _ShareRider:
    has_mid = False

    def __init__(self, halves):
        n = len(halves)
        self.inputs = list(halves)
        self.out_shapes = [jax.ShapeDtypeStruct(g.shape, g.dtype) for g in halves]
        self.aliases = {a: a for a in range(n)}
        self.sems = [pltpu.SemaphoreType.DMA((n,)), pltpu.SemaphoreType.DMA((n,))]

    def _copies(self, outs, sems, half):
        send_sems, recv_sems = sems
        x, y, c, _ = _place()
        h = c if half == "mine" else 1 - c
        return [pltpu.make_async_remote_copy(src_ref=outs[a].at[h], dst_ref=outs[a].at[h], send_sem=send_sems.at[a],
                                             recv_sem=recv_sems.at[a], device_id=(x, y, 1 - c), device_id_type=MESH)
                for a in range(len(outs))]

    def start(self, ins, outs, sems):
        for cp in self._copies(outs, sems, "mine"):
            cp.start()

    def end(self, ins, outs, sems):
        for cp in self._copies(outs, sems, "theirs"):
            cp.wait_recv()
        for cp in self._copies(outs, sems, "mine"):
            cp.wait_send()


class _Riders:
    def __init__(self, riders):
        self.riders = list(riders)
        self.inputs = [a for r in self.riders for a in r.inputs]
        self.out_shapes = [s for r in self.riders for s in r.out_shapes]
        self.sems = [s for r in self.riders for s in r.sems]
        self.has_mid = any(r.has_mid for r in self.riders)
        self.aliases = {}
        i0 = o0 = 0
        for r in self.riders:
            self.aliases.update({i0 + j: o0 + k for j, k in r.aliases.items()})
            i0 += len(r.inputs)
            o0 += len(r.out_shapes)

    def _each(self, ins, outs, sems):
        i0 = o0 = s0 = 0
        for r in self.riders:
            yield (r, ins[i0:i0 + len(r.inputs)], outs[o0:o0 + len(r.out_shapes)], sems[s0:s0 + len(r.sems)])
            i0, o0, s0 = i0 + len(r.inputs), o0 + len(r.out_shapes), s0 + len(r.sems)

    def start(self, ins, outs, sems):
        for r, i, o, s in self._each(ins, outs, sems):
            r.start(i, o, s)

    def mid(self, ins, outs, sems):
        for r, i, o, s in self._each(ins, outs, sems):
            if r.has_mid:
                r.mid(i, o, s)

    def end(self, ins, outs, sems):
        for r, i, o, s in self._each(ins, outs, sems):
            r.end(i, o, s)

    def split(self, outs):
        res, o0 = [], 0
        for r in self.riders:
            res.append(tuple(outs[o0:o0 + len(r.out_shapes)]))
            o0 += len(r.out_shapes)
        return res


def _chip_sum(own, parts, place, name):
    npart, h, cc = parts.shape
    th = _row_tile(h)

    def body(place_ref, own_ref, p_ref, o_ref):
        acc = own_ref[...].astype(F32) + p_ref[0].astype(F32)
        for k in range(1, npart):
            acc = acc + p_ref[k].astype(F32)
        o_ref[...] = acc

    return pl.pallas_call(
        body, name=name, out_shape=jax.ShapeDtypeStruct((2, h, cc), F32),
        grid_spec=pltpu.PrefetchScalarGridSpec(
            num_scalar_prefetch=1, grid=(h // th,),
            in_specs=[pl.BlockSpec((None, th, cc), lambda r, pr: (pr[1], r, 0)),
                      pl.BlockSpec((npart, th, cc), lambda r, pr: (0, r, 0))],
            out_specs=pl.BlockSpec((None, th, cc), lambda r, pr: (pr[0], r, 0))),
        compiler_params=_cp(("arbitrary",)),
    )(place, own, parts)


def _adamw_math(w, g, m, v):
    m = ADAM_B1 * m + (1.0 - ADAM_B1) * g
    v = ADAM_B2 * v + (1.0 - ADAM_B2) * (g * g)
    m_hat = m / (1.0 - ADAM_B1 ** ADAM_STEP)
    v_hat = v / (1.0 - ADAM_B2 ** ADAM_STEP)
    delta = -ADAM_LR * (m_hat / (jnp.sqrt(v_hat) + ADAM_EPS) + ADAM_WD * w)
    return delta, m, v


def _adamw_big(w, g, m, v, name):
    r, cc = w.shape
    tr = min(_row_tile(r), 256) if r % 256 == 0 else _row_tile(r)

    nt, nbuf = r // tr, 3

    def body(w_hbm, g_hbm, m_hbm, v_hbm, go_ref, d_ref, mo_ref, vo_ref, bufs, sems):
        i = pl.program_id(0)
        srcs = (w_hbm, g_hbm, m_hbm, v_hbm)

        def copies(step):
            row0 = step * tr if isinstance(step, int) else pl.multiple_of(step * tr, 8)
            return [pltpu.make_async_copy(srcs[a].at[pl.ds(row0, tr)], bufs.at[a, step % nbuf],
                                          sems.at[a, step % nbuf]) for a in range(4)]

        @pl.when(i == 0)
        def _():
            for s in range(min(nbuf - 1, nt)):
                for cp in copies(s):
                    cp.start()

        @pl.when(i + (nbuf - 1) < nt)
        def _():
            for cp in copies(i + (nbuf - 1)):
                cp.start()

        for cp in copies(i):
            cp.wait()
        slot = i % nbuf
        g_ = bufs[1, slot]
        d, mm, vv = _adamw_math(bufs[0, slot], g_, bufs[2, slot], bufs[3, slot])
        go_ref[...] = g_
        d_ref[...] = d
        mo_ref[...] = mm
        vo_ref[...] = vv

    blk = pl.BlockSpec((tr, cc), lambda i: (i, 0))
    sd = jax.ShapeDtypeStruct((r, cc), F32)
    return pl.pallas_call(body, grid=(nt,), name=name, out_shape=(sd, sd, sd, sd), in_specs=_hbm_specs(4),
                          out_specs=(blk, blk, blk, blk),
                          scratch_shapes=[pltpu.VMEM((4, nbuf, tr, cc), F32), pltpu.SemaphoreType.DMA((4, nbuf))],
                          compiler_params=_cp(("arbitrary",)))(w, g, m, v)


SC_TILES = 32


def _adamw_sparsecore(ws, gs, ms, vs, name):
    n = len(ws)
    rows_per = 8
    widths = sorted({a.shape[1] for a in ws})
    assert all(a.shape[0] % rows_per == 0 and a.shape[1] % 16 == 0 for a in ws)

    def body(*refs):
        ins, outs, bufs = refs[:4 * n], refs[4 * n:8 * n], refs[8 * n:]
        tile = lax.axis_index("sc_tile") * 2 + lax.axis_index("sc_core")
        for a in range(n):
            w_hbm, g_hbm, m_hbm, v_hbm = ins[4 * a:4 * a + 4]
            go_hbm, d_hbm, mo_hbm, vo_hbm = outs[4 * a:4 * a + 4]
            r, cc = ws[a].shape
            k = widths.index(cc)
            wb, gb, mb, vb, db = bufs[5 * k:5 * k + 5]
            groups = r // rows_per

            @pl.loop(0, -(-groups // SC_TILES))
            def _(q):
                grp = tile + q * SC_TILES

                @pl.when(grp < groups)
                def _():
                    rows = pl.ds(pl.multiple_of(grp * rows_per, rows_per), rows_per)
                    pltpu.sync_copy(w_hbm.at[rows], wb)
                    pltpu.sync_copy(g_hbm.at[rows], gb)
                    pltpu.sync_copy(m_hbm.at[rows], mb)
                    pltpu.sync_copy(v_hbm.at[rows], vb)

                    @pl.loop(0, rows_per)
                    def _(i):
                        @pl.loop(0, cc, step=16)
                        def _(j):
                            at = (i, pl.ds(j, 16))
                            d, mm, vv = _adamw_math(wb[at], gb[at], mb[at], vb[at])
                            db[at] = d
                            mb[at] = mm
                            vb[at] = vv

                    pltpu.sync_copy(gb, go_hbm.at[rows])
                    pltpu.sync_copy(db, d_hbm.at[rows])
                    pltpu.sync_copy(mb, mo_hbm.at[rows])
                    pltpu.sync_copy(vb, vo_hbm.at[rows])

    args, out_type = [], []
    for a in range(n):
        args += [ws[a], gs[a], ms[a], vs[a]]
        out_type += [jax.ShapeDtypeStruct(ws[a].shape, F32)] * 4
    res = pl.kernel(
        body, name=name, out_type=tuple(out_type),
        mesh=plsc.VectorSubcoreMesh(core_axis_name="sc_core", subcore_axis_name="sc_tile"),
        scratch_types=[pltpu.VMEM((rows_per, cc), F32) for cc in widths for _ in range(5)],
    )(*args)
    return [tuple(res[4 * a:4 * a + 4]) for a in range(n)]


def _adamw_small(ws, gs, ms, vs):
    n = len(ws)

    def body(*refs):
        for a in range(n):
            w_ref, g_ref, m_ref, v_ref = refs[4 * a:4 * a + 4]
            d_ref, mo_ref, vo_ref = refs[4 * n + 3 * a:4 * n + 3 * a + 3]
            d, mm, vv = _adamw_math(w_ref[...], g_ref[...], m_ref[...], v_ref[...])
            d_ref[...] = d
            mo_ref[...] = mm
            vo_ref[...] = vv

    args, outs = [], []
    for a in range(n):
        args += [ws[a], gs[a], ms[a], vs[a]]
        outs += [jax.ShapeDtypeStruct(ws[a].shape, F32)] * 3
    res = pl.pallas_call(body, name="adamw_small", out_shape=tuple(outs), compiler_params=_cp())(*args)
    return [res[3 * a:3 * a + 3] for a in range(n)]


def _flat_pack(arrs, rows):
    flat = jnp.concatenate([a.reshape(-1) for a in arrs])
    return jnp.pad(flat, (0, rows * D - flat.shape[0])).reshape(rows, D)


def _flat_unpack(flat, shapes):
    out, off = [], 0
    for shp in shapes:
        size = 1
        for d_ in shp:
            size *= d_
        out.append(flat[off:off + size].reshape(shp))
        off += size
    return out


SMALL_EVEN = ("even_pre_g", "even_a_ln_g", "even_a_ln_b", "even_a_ws", "even_a_bs", "even_b_conv", "even_mem_g",
              "even_post_g")
SMALL_ODD = ("odd_pre_g", "odd_c_wgrp", "odd_c_scale", "odd_d_dw_w", "odd_d_dw_b", "odd_d_ln_g", "odd_d_ln_b",
             "odd_d_pw_b", "odd_mem_g", "odd_post_g")
BIG = ("even_w_in", "even_w_kv", "even_w_out", "odd_w_in", "odd_d_pw_w", "odd_w_kv", "odd_w_out")
WEIGHTS = ("even_pre_g", "even_w_in", "even_a_ln_g", "even_a_ln_b", "even_a_ws", "even_a_bs", "even_b_conv",
           "even_mem_g", "even_w_kv", "even_w_out", "even_post_g", "odd_pre_g", "odd_w_in", "odd_c_wgrp",
           "odd_c_scale", "odd_d_dw_w", "odd_d_dw_b", "odd_d_ln_g", "odd_d_ln_b", "odd_d_pw_w", "odd_d_pw_b",
           "odd_mem_g", "odd_w_kv", "odd_w_out", "odd_post_g")
PACKED = (("even_b_conv", (3, 192)), ("odd_pre_g", (1, 256)), ("odd_c_scale", (1, 192)), ("odd_d_dw_w", (31, 192)),
          ("odd_d_dw_b", (1, 192)), ("odd_d_ln_g", (1, 192)), ("odd_d_ln_b", (1, 192)), ("odd_d_pw_b", (1, 192)),
          ("odd_mem_g", (1, 256)), ("odd_post_g", (1, 256)))
PACK_ROWS = 16
SMALL_ROWS = 256


def _four(g):
    return g.reshape(N_CHIPS, 2, g.shape[0] // (2 * N_CHIPS), g.shape[1])


def _step(x, mem, target, w, mom, var, place):
    wt = {}
    pack = _flat_pack([w[n][0] for n, _ in PACKED], PACK_ROWS)
    shards = {"even_w_in_t": w["even_w_in"][0].T, "odd_w_in_t": w["odd_w_in"][0].T, "even_w_kv": w["even_w_kv"][0],
              "odd_w_kv": w["odd_w_kv"][0], "even_w_out": w["even_w_out"][0], "odd_w_out": w["odd_w_out"][0],
              "odd_d_pw_w": w["odd_d_pw_w"][0]}
    placed = {n: _place_shard(shards[n], place, BF16, "place_" + n) for n in ("even_w_in_t", "even_w_kv", "even_w_out")}
    placed["pack"] = _place_shard(pack, place, F32, "place_pack")

    order, group = _stream_tables(place[0], place[1], EVEN_IN)
    p_e, h_e, (wt["even_w_in_t"], packs), (wt["even_w_kv"], wt["even_w_out"]) = _in_fwd_streamed(
        x, w["even_pre_g"], [placed["even_w_in_t"], placed["pack"]], [placed["even_w_kv"], placed["even_w_out"]],
        order, group, "even_in_streamed")
    for n in ("odd_w_in_t", "odd_w_kv", "odd_w_out", "odd_d_pw_w"):
        placed[n] = _place_shard(shards[n], place, BF16, "place_" + n, after=p_e[0:16, 0:128])
    packs = packs.reshape(N_CHIPS, PACK_ROWS * D)
    per_chip = [_flat_unpack(packs[k], [shp for _, shp in PACKED]) for k in range(N_CHIPS)]
    for a, (name, _) in enumerate(PACKED):
        wt[name] = jnp.concatenate([per_chip[k][a] for k in range(N_CHIPS)], axis=-1)
    for name in ("even_pre_g", "even_a_ln_g", "even_a_ln_b", "even_mem_g", "even_post_g"):
        wt[name] = w[name]

    tril = jnp.tril(jnp.ones((CH, CH), dtype=bool))
    wcat = jnp.where(tril[None], w["even_a_ws"][0], 0.0).transpose(1, 0, 2).reshape(CH, 4 * CH).astype(BF16)
    bsg = jnp.repeat(w["even_a_bs"][0].T, BW // 4, axis=1)
    hsel = (jnp.arange(BW)[:, None] // (BW // 4) == jnp.arange(128)[None, :]).astype(BF16)
    g4 = BW // 4
    eye = jnp.eye(4, dtype=F32)
    wbd = (w["odd_c_wgrp"][0][:, :, None, :] * eye[:, None, :, None]).reshape(BW, BW).astype(BF16)

    kv_e = _kv_fwd(mem, wt["even_mem_g"], wt["even_w_kv"], "even_kv")
    (x1, o_e, y_e), (wt["odd_w_in_t"],) = _even_fwd(
        x, p_e, kv_e, wt["even_a_ln_g"], wt["even_a_ln_b"], wcat, bsg, wt["even_b_conv"], wt["even_w_out"],
        wt["even_post_g"], rider=_GatherRider([placed["odd_w_in_t"]]))
    names = ("odd_w_out", "odd_d_pw_w", "odd_w_kv")
    (p_o, h_o), got = _in_fwd(x1, wt["odd_pre_g"], wt["odd_w_in_t"], "odd_in",
                              rider=_GatherRider([placed[n] for n in names]))
    wt.update(zip(names, got))
    kv_o = _kv_fwd(mem, wt["odd_mem_g"], wt["odd_w_kv"], "odd_kv")
    dx2, o_o, cv_o, loss = _odd_fwd(x1, p_o, kv_o, wbd, wt["odd_c_scale"], wt["odd_d_dw_w"], wt["odd_d_dw_b"],
                                    wt["odd_d_ln_g"], wt["odd_d_ln_b"], wt["odd_d_pw_w"], wt["odd_d_pw_b"],
                                    wt["odd_w_out"], wt["odd_post_g"], target)
    (dpc_o, tmpc, tmpd, do_o, y_o, g_post_o, g_cs, g_wbd, g_dww, g_dwb, g_lng_o, g_lnb_o, g_pww, g_pwb,
     dkv_o) = _odd_bwd1(dx2, o_o, cv_o, p_o, kv_o, wbd, wt["odd_c_scale"], wt["odd_d_dw_w"], wt["odd_d_dw_b"],
                        wt["odd_d_ln_g"], wt["odd_d_ln_b"], wt["odd_d_pw_w"], wt["odd_d_pw_b"], wt["odd_w_out"],
                        wt["odd_post_g"])
    dpb_o, dx1, g_pre_o = _odd_bwd2(dpc_o, tmpc, tmpd, p_o, wt["odd_d_dw_w"], wt["odd_w_in_t"], x1,
                                    wt["odd_pre_g"], dx2)
    g_win_o = _grad_tn(dpb_o, h_o, 768, rows=ODD_IN, name="odd_gw_in_b")
    g_win_o = _grad_tn(dpc_o, h_o, 1280, out=g_win_o, rows=ODD_IN, row0=3 * BW, name="odd_gw_in_c")
    g_wout_o = _grad_tn(y_o, do_o, 1024, name="odd_gw_out")
    g_wkv_o, g_memg_o = _kv_bwd(mem, wt["odd_mem_g"], wt["odd_w_kv"], dkv_o, "odd_kv_bwd")
    big_o = [_four(g) for g in (g_win_o, g_pww.astype(BF16), g_wkv_o, g_wout_o)]
    recv_o, _ = _swap_halves(big_o, "swap_halves_odd")
    sums_o = [_pair_sum(big_o[a], recv_o[a], place, "pair_sum_odd_%d" % a) for a in range(len(big_o))]
    (dp_e, do_e, g_post_e, g_lng_e, g_lnb_e, g_wcat, g_bs, g_bconv,
     dkv_e), parts_o = _even_bwd1(dx1, o_e, p_e, kv_e, wt["even_a_ln_g"], wt["even_a_ln_b"], wcat, bsg, hsel,
                                  wt["even_b_conv"], wt["even_w_out"], wt["even_post_g"],
                                  rider=_ExchangeRider(sums_o))
    halves_o = [_chip_sum(sums_o[a], parts_o[a], place, "chip_sum_odd_%d" % a) for a in range(len(big_o))]
    g_wout_e = _grad_tn(y_e, do_e, 1024, name="even_gw_out")
    g_wkv_e, g_memg_e = _kv_bwd(mem, wt["even_mem_g"], wt["even_w_kv"], dkv_e, "even_kv_bwd")
    big_x = [_four(g) for g in (g_wkv_e, g_wout_e)]
    recv_x, _ = _swap_halves(big_x, "swap_halves_kv_out")
    sums_x = [_pair_sum(big_x[a], recv_x[a], place, "pair_sum_kv_out_%d" % a) for a in range(len(big_x))]
    riders = _Riders([_ExchangeRider(sums_x), _ShareRider(halves_o)])
    g_win_e, got = _grad_tn(dp_e, h_e, 1280, name="even_gw_in", rider=riders)
    parts_x, full_o = riders.split(got)

    def sparsecore_adamw(names, fulls, name):
        as_kept = [(lambda t: t.T) if n.endswith("w_in") else (lambda t: t) for n in names]
        res = _adamw_sparsecore([f(w[n][0]) for f, n in zip(as_kept, names)],
                                [g_.reshape(g_.shape[1] * 2, g_.shape[2]) for g_ in fulls],
                                [f(mom[n][0]) for f, n in zip(as_kept, names)],
                                [f(var[n][0]) for f, n in zip(as_kept, names)], name)
        return {n: tuple(f(t) for t in r_) for f, n, r_ in zip(as_kept, names, res)}

    upd_sc = sparsecore_adamw(("odd_w_in", "odd_d_pw_w", "odd_w_kv", "odd_w_out"), full_o, "adamw_odd_sparsecore")
    halves_x = [_chip_sum(sums_x[a], parts_x[a], place, "chip_sum_kv_out_%d" % a) for a in range(len(big_x))]
    big_e = [_four(g_win_e)]
    recv_e, _ = _swap_halves(big_e, "swap_halves_even")
    sums_e = [_pair_sum(big_e[0], recv_e[0], place, "pair_sum_even_w_in")]
    riders = _Riders([_ExchangeRider(sums_e), _ShareRider(halves_x)])
    (dx0, g_pre_e), got = _even_bwd2(dp_e, wt["even_w_in_t"], x, wt["even_pre_g"], dx1, rider=riders)
    parts_e, full_x = riders.split(got)
    upd_sc.update(sparsecore_adamw(("even_w_kv", "even_w_out"), full_x, "adamw_kv_out_sparsecore"))
    halves_e = [_chip_sum(sums_e[0], parts_e[0], place, "chip_sum_even_w_in")]

    g_aws = jnp.where(tril[None], g_wcat.reshape(CH, 4, CH).transpose(1, 0, 2), 0.0)
    g_wgrp = jnp.stack([lax.dynamic_slice(g_wbd, (g * g4, g * g4), (g4, g4)) for g in range(4)])
    small = {
        "even_pre_g": g_pre_e, "even_a_ln_g": g_lng_e, "even_a_ln_b": g_lnb_e, "even_a_ws": g_aws,
        "even_a_bs": g_bs[:, 0:4].T, "even_b_conv": g_bconv[0:3], "even_mem_g": g_memg_e, "even_post_g": g_post_e,
        "odd_pre_g": g_pre_o, "odd_c_wgrp": g_wgrp, "odd_c_scale": g_cs, "odd_d_dw_w": g_dww.reshape(CONF, 8, BW).sum(axis=1),
        "odd_d_dw_b": g_dwb, "odd_d_ln_g": g_lng_o, "odd_d_ln_b": g_lnb_o, "odd_d_pw_b": g_pwb,
        "odd_mem_g": g_memg_o, "odd_post_g": g_post_o,
    }
    small_names = SMALL_EVEN + SMALL_ODD
    small_pack = _flat_pack([small[n] for n in small_names] + [loss[0, 0].reshape(1)], SMALL_ROWS)
    small_total, full = _finish_reduce(small_pack, halves_e)
    gbig = {"even_w_in": full[0].reshape(full[0].shape[1] * 2, full[0].shape[2])}
    return dx0, gbig, upd_sc, small_total.reshape(-1), [small[n].shape for n in small_names]


def kernel(x, mem, even_pre_g, even_w_in, even_a_ln_g, even_a_ln_b, even_a_ws, even_a_bs, even_b_conv, even_mem_g, even_w_kv, even_w_out, even_post_g, odd_pre_g, odd_w_in, odd_c_wgrp, odd_c_scale, odd_d_dw_w, odd_d_dw_b, odd_d_ln_g, odd_d_ln_b, odd_d_pw_w, odd_d_pw_b, odd_mem_g, odd_w_kv, odd_w_out, odd_post_g, loss_target, m_even_pre_g, m_even_w_in, m_even_a_ln_g, m_even_a_ln_b, m_even_a_ws, m_even_a_bs, m_even_b_conv, m_even_mem_g, m_even_w_kv, m_even_w_out, m_even_post_g, m_odd_pre_g, m_odd_w_in, m_odd_c_wgrp, m_odd_c_scale, m_odd_d_dw_w, m_odd_d_dw_b, m_odd_d_ln_g, m_odd_d_ln_b, m_odd_d_pw_w, m_odd_d_pw_b, m_odd_mem_g, m_odd_w_kv, m_odd_w_out, m_odd_post_g, v_even_pre_g, v_even_w_in, v_even_a_ln_g, v_even_a_ln_b, v_even_a_ws, v_even_a_bs, v_even_b_conv, v_even_mem_g, v_even_w_kv, v_even_w_out, v_even_post_g, v_odd_pre_g, v_odd_w_in, v_odd_c_wgrp, v_odd_c_scale, v_odd_d_dw_w, v_odd_d_dw_b, v_odd_d_ln_g, v_odd_d_ln_b, v_odd_d_pw_w, v_odd_d_pw_b, v_odd_mem_g, v_odd_w_kv, v_odd_w_out, v_odd_post_g):
    given = dict(locals())
    w = {n: given[n] for n in WEIGHTS}
    mom = {n: given["m_" + n] for n in WEIGHTS}
    var = {n: given["v_" + n] for n in WEIGHTS}

    x_, y_, c_ = lax.axis_index("x"), lax.axis_index("y"), lax.axis_index("c")
    chip = 2 * x_ + y_
    place = jnp.stack([c_, chip]).astype(jnp.int32)
    grad_x, gbig, upd_odd, gsmall_flat, small_shapes = _step(x[0], mem[0], loss_target[0], w, mom, var, place)

    names = SMALL_EVEN + SMALL_ODD
    grads = {}
    unpacked = _flat_unpack(gsmall_flat, small_shapes + [(1,)])
    loss = unpacked[-1][0]
    for n, g in zip(names, unpacked[:-1]):
        shard_shape = w[n].shape[1:]
        if g.shape[-1] != shard_shape[-1]:
            g = lax.dynamic_slice_in_dim(g, chip * shard_shape[-1], shard_shape[-1], axis=g.ndim - 1)
        grads[n] = g.reshape(shard_shape)

    def two_d(a):
        return a.reshape(-1, a.shape[-1])

    upd = {}
    for n in BIG:
        if n in upd_odd:
            res = upd_odd[n]
        elif n.endswith("w_in"):
            res = _adamw_big(w[n][0].T, gbig[n], mom[n][0].T, var[n][0].T, "adamw_" + n)
            res = tuple(r.T for r in res)
        else:
            res = _adamw_big(w[n][0], gbig[n], mom[n][0], var[n][0], "adamw_" + n)
        grads[n], upd[n] = res[0], res[1:]
    res = _adamw_small([two_d(w[n][0]) for n in names], [two_d(grads[n]) for n in names],
                       [two_d(mom[n][0]) for n in names], [two_d(var[n][0]) for n in names])
    for n, r in zip(names, res):
        upd[n] = r

    outs = [loss, grad_x[None]]
    outs += [grads[n].reshape(w[n].shape) for n in WEIGHTS]
    for j in range(3):
        outs += [upd[n][j].reshape(w[n].shape) for n in WEIGHTS]
    return tuple(outs)
```
